```python
import jax, jax.numpy as jnp
from jax import lax
import numpy as np

D_MODEL = 1024
BATCH = 16
SEQ = 2048
DEPTH = 1

CHUNK = 64
RET_HEADS = 4
RET_DK = 128
RET_DV = 128
D_RET_QK = RET_HEADS * RET_DK
D_RET = RET_HEADS * RET_DV
MLSTM_HEADS = 4
MLSTM_D = 128
D_MLSTM = MLSTM_HEADS * MLSTM_D
D_MIX = D_RET + D_MLSTM
CONV_W = 4
ROPE_BASE = 10000.0
PEER_HEADS = 8
PEER_NKEYS = 128
PEER_TOPK = 16
PEER_DKEY = 256
PEER_N = PEER_NKEYS * PEER_NKEYS
PEER_BLOCK = 128
EPS = 1e-6
PROJ_SIZES = (D_RET_QK, D_RET_QK, D_RET, D_RET, D_MLSTM, D_MLSTM, D_MLSTM, MLSTM_HEADS, MLSTM_HEADS)
D_IN_PROJ = sum(PROJ_SIZES)

kernel_name = "retention_mlstm_peer_hybrid"


def rms_norm(x, g):
    xf = x.astype(jnp.float32)
    y = xf * lax.rsqrt(jnp.mean(xf * xf, axis=-1, keepdims=True) + EPS)
    return (y * g.astype(jnp.float32)).astype(x.dtype)


def head_group_norm(h, g):
    mu = jnp.mean(h, axis=-1, keepdims=True)
    var = jnp.mean(jnp.square(h - mu), axis=-1, keepdims=True)
    return (h - mu) * lax.rsqrt(var + EPS) * g.astype(jnp.float32)


def rotary(t, pos):
    half = t.shape[-1] // 2
    inv = ROPE_BASE ** (-jnp.arange(half, dtype=jnp.float32) / half)
    ang = pos.astype(jnp.float32)[:, None] * inv[None, :]
    cos = jnp.cos(ang)[None, :, None, :]
    sin = jnp.sin(ang)[None, :, None, :]
    t1, t2 = t[..., :half], t[..., half:]
    return jnp.concatenate([t1 * cos - t2 * sin, t1 * sin + t2 * cos], axis=-1)


def to_chunks(t):
    b, s, h, d = t.shape
    return t.reshape(b, s // CHUNK, CHUNK, h, d).transpose(1, 0, 3, 2, 4)


def from_chunks(t):
    nc, b, h, l, d = t.shape
    return t.transpose(1, 0, 3, 2, 4).reshape(b, nc * l, h, d)


def causal_dwconv(x, w, b):
    s = x.shape[1]
    xp = jnp.pad(x, ((0, 0), (CONV_W - 1, 0), (0, 0)))
    y = b + xp[:, 0:s] * w[0]
    for j in range(1, CONV_W):
        y = y + xp[:, j:j + s] * w[j]
    return y


def retention(q, k, v):
    b, _, h, dk = q.shape
    dv = v.shape[-1]
    log_g = jnp.log(1.0 - 2.0 ** (-5.0 - jnp.arange(h, dtype=jnp.float32)))
    l = jnp.arange(CHUNK, dtype=jnp.float32)
    intra_decay = jnp.exp(log_g[:, None, None] * jnp.abs(l[:, None] - l[None, :]))
    q_decay = jnp.exp(log_g[:, None] * (l + 1.0))[None, :, :, None]
    k_decay = jnp.exp(log_g[:, None] * (CHUNK - 1.0 - l))[None, :, :, None]
    chunk_decay = jnp.exp(log_g * CHUNK)[None, :, None, None]
    qc, kc, vc = to_chunks(q), to_chunks(k), to_chunks(v)
    scores = jnp.einsum('cbhld,cbhmd->cbhlm', qc, kc) * intra_decay
    o_intra = jnp.einsum('cbhlm,cbhmv->cbhlv', scores, vc)

    def step(state, inp):
        q_c, k_c, v_c = inp
        cross = jnp.einsum('bhld,bhdv->bhlv', q_c, state) * q_decay
        state = chunk_decay * state + jnp.einsum('bhld,bhlv->bhdv', k_c * k_decay, v_c)
        return state, cross

    state0 = jnp.zeros((b, h, dk, dv), jnp.float32)
    _, o_cross = lax.scan(step, state0, (qc, kc, vc))
    return from_chunks(o_intra + o_cross)


def mlstm(q, k, v, i_pre, log_f):
    b, s, h, d = q.shape
    nc = s // CHUNK
    qc, kc, vc = to_chunks(q), to_chunks(k), to_chunks(v)
    ic = i_pre.reshape(b, nc, CHUNK, h).transpose(1, 0, 3, 2)
    fc = log_f.reshape(b, nc, CHUNK, h).transpose(1, 0, 3, 2)
    f_cum = jnp.cumsum(fc, axis=-1)
    a = f_cum[..., -1]
    logw = a[..., None] - f_cum + ic
    m_in = jnp.max(logw, axis=-1)

    def step(carry, inp):
        c_mem, n_mem, m = carry
        q_c, k_c, v_c, logw_c, a_c, m_in_c = inp
        m_new = jnp.maximum(a_c + m, m_in_c)
        decay = jnp.exp(a_c + m - m_new)
        w = jnp.exp(logw_c - m_new[..., None])
        c_mem = decay[..., None, None] * c_mem + jnp.einsum('bhs,bhsk,bhsv->bhkv', w, k_c, v_c)
        n_mem = decay[..., None] * n_mem + jnp.einsum('bhs,bhsk->bhk', w, k_c)
        num = jnp.einsum('bhlk,bhkv->bhlv', q_c, c_mem)
        den = jnp.maximum(jnp.abs(jnp.einsum('bhlk,bhk->bhl', q_c, n_mem)),
                          jnp.exp(-m_new)[..., None])
        return (c_mem, n_mem, m_new), num / den[..., None]

    init = (jnp.zeros((b, h, d, d), jnp.float32),
            jnp.zeros((b, h, d), jnp.float32),
            jnp.zeros((b, h), jnp.float32))
    _, hc = lax.scan(step, init, (qc, kc, vc, logw, a, m_in))
    return from_chunks(hc)


def peer(xn, wq, keys, down, up):
    b, s, d = xn.shape
    t = b * s
    xt = xn.reshape(t, d)
    q = (xt @ wq).reshape(t, PEER_HEADS, 2, PEER_DKEY // 2).astype(jnp.float32)
    sub = jnp.einsum('thpd,hpnd->thpn', q, keys.astype(jnp.float32))
    s_top, i_top = lax.top_k(sub, PEER_TOPK)
    cand = (s_top[:, :, 0, :, None] + s_top[:, :, 1, None, :]).reshape(t, PEER_HEADS, PEER_TOPK * PEER_TOPK)
    best, flat = lax.top_k(cand, PEER_TOPK)
    i1 = jnp.take_along_axis(i_top[:, :, 0, :], flat // PEER_TOPK, axis=-1)
    i2 = jnp.take_along_axis(i_top[:, :, 1, :], flat % PEER_TOPK, axis=-1)
    expert = i1 * PEER_NKEYS + i2
    gates = jax.nn.softmax(best, axis=-1)
    nb = t // PEER_BLOCK
    e_per_tok = PEER_HEADS * PEER_TOPK
    xb = xt.reshape(nb, PEER_BLOCK, d)
    eb = expert.reshape(nb, PEER_BLOCK, e_per_tok)
    gb = gates.astype(xt.dtype).reshape(nb, PEER_BLOCK, e_per_tok)

    def block(args):
        xi, ei, gi = args
        u = jnp.take(down, ei, axis=0)
        act = jax.nn.gelu(jnp.einsum('td,ted->te', xi, u), approximate=False)
        vv = jnp.take(up, ei, axis=0)
        return jnp.einsum('te,ted->td', gi * act, vv)

    out = lax.map(block, (xb, eb, gb))
    return out.reshape(b, s, d)


def hybrid_layer(x, norm1_g, w_in, ret_norm_g, m_conv_w, m_conv_b, m_wq, m_wk,
                 m_b_i, m_b_f, m_norm_g, m_skip, w_out, norm2_g,
                 peer_wq, peer_keys, peer_down, peer_up):
    f32 = jnp.float32
    b, s, _ = x.shape
    pos = jnp.arange(s, dtype=jnp.int32)
    h = rms_norm(x, norm1_g)
    proj = h @ w_in
    split_idx = [int(v) for v in np.cumsum(PROJ_SIZES)[:-1]]
    rq, rk, rv, rg, mx, mv, mo, mi, mf = jnp.split(proj, split_idx, axis=-1)

    rq = rotary(rq.reshape(b, s, RET_HEADS, RET_DK).astype(f32), pos)
    rk = rotary(rk.reshape(b, s, RET_HEADS, RET_DK).astype(f32), pos) * (RET_DK ** -0.5)
    rv = rv.reshape(b, s, RET_HEADS, RET_DV).astype(f32)
    o_ret = head_group_norm(retention(rq, rk, rv), ret_norm_g).reshape(b, s, D_RET)
    y_ret = jax.nn.silu(rg.astype(f32)) * o_ret

    c = jax.nn.silu(causal_dwconv(mx, m_conv_w, m_conv_b)).astype(f32)
    ch = c.reshape(b, s, MLSTM_HEADS, MLSTM_D)
    mq = jnp.einsum('bshd,hde->bshe', ch, m_wq.astype(f32))
    mk = jnp.einsum('bshd,hde->bshe', ch, m_wk.astype(f32)) * (MLSTM_D ** -0.5)
    mvv = mv.reshape(b, s, MLSTM_HEADS, MLSTM_D).astype(f32)
    i_pre = mi.astype(f32) + m_b_i.astype(f32)
    log_f = jax.nn.log_sigmoid(mf.astype(f32) + m_b_f.astype(f32))
    hm = head_group_norm(mlstm(mq, mk, mvv, i_pre, log_f), m_norm_g).reshape(b, s, D_MLSTM)
    y_m = jax.nn.sigmoid(mo.astype(f32)) * (hm + m_skip.astype(f32) * c)

    y = jnp.concatenate([y_ret, y_m], axis=-1).astype(x.dtype) @ w_out
    x = x + y
    x = x + peer(rms_norm(x, norm2_g), peer_wq, peer_keys, peer_down, peer_up)
    return x


def setup_inputs(seed: int = 0) -> dict:
    key = jax.random.key(seed)
    ks = jax.random.split(key, 20)
    f32 = jnp.float32
    nrm = lambda k, shape: jax.random.normal(k, shape, f32)
    L = DEPTH
    return {
        "x": nrm(ks[0], (BATCH, SEQ, D_MODEL)),
        "norm1_g": 1.0 + 0.02 * nrm(ks[1], (L, D_MODEL)),
        "w_in": nrm(ks[2], (L, D_MODEL, D_IN_PROJ)) * D_MODEL ** -0.5,
        "ret_norm_g": 1.0 + 0.02 * nrm(ks[3], (L, RET_HEADS, RET_DV)),
        "m_conv_w": nrm(ks[4], (L, CONV_W, D_MLSTM)) * CONV_W ** -0.5,
        "m_conv_b": 0.02 * nrm(ks[5], (L, D_MLSTM)),
        "m_wq": nrm(ks[6], (L, MLSTM_HEADS, MLSTM_D, MLSTM_D)) * MLSTM_D ** -0.5,
        "m_wk": nrm(ks[7], (L, MLSTM_HEADS, MLSTM_D, MLSTM_D)) * MLSTM_D ** -0.5,
        "m_b_i": 0.1 * nrm(ks[8], (L, MLSTM_HEADS)),
        "m_b_f": jnp.linspace(3.0, 6.0, MLSTM_HEADS, dtype=f32)[None, :] + 0.1 * nrm(ks[9], (L, MLSTM_HEADS)),
        "m_norm_g": 1.0 + 0.02 * nrm(ks[10], (L, MLSTM_HEADS, MLSTM_D)),
        "m_skip": 1.0 + 0.02 * nrm(ks[11], (L, D_MLSTM)),
        "w_out": nrm(ks[12], (L, D_MIX, D_MODEL)) * D_MIX ** -0.5,
        "norm2_g": 1.0 + 0.02 * nrm(ks[13], (L, D_MODEL)),
        "peer_wq": nrm(ks[14], (L, D_MODEL, PEER_HEADS * PEER_DKEY)) * D_MODEL ** -0.5,
        "peer_keys": nrm(ks[15], (L, PEER_HEADS, 2, PEER_NKEYS, PEER_DKEY // 2)) * (PEER_DKEY // 2) ** -0.5,
        "peer_down": nrm(ks[16], (L, PEER_N, D_MODEL)) * D_MODEL ** -0.5,
        "peer_up": nrm(ks[17], (L, PEER_N, D_MODEL)) * PEER_HEADS ** -0.5,
        "final_g": 1.0 + 0.02 * nrm(ks[18], (D_MODEL,)),
    }


def reference(x, norm1_g, w_in, ret_norm_g, m_conv_w, m_conv_b, m_wq, m_wk,
              m_b_i, m_b_f, m_norm_g, m_skip, w_out, norm2_g,
              peer_wq, peer_keys, peer_down, peer_up, final_g):
    for i in range(DEPTH):
        x = hybrid_layer(x, norm1_g[i], w_in[i], ret_norm_g[i], m_conv_w[i], m_conv_b[i],
                         m_wq[i], m_wk[i], m_b_i[i], m_b_f[i], m_norm_g[i], m_skip[i],
                         w_out[i], norm2_g[i], peer_wq[i], peer_keys[i],
                         peer_down[i], peer_up[i])
    return rms_norm(x, final_g)
```

```python
import functools

import numpy as np
import jax
import jax.numpy as jnp
from jax import lax
from jax.experimental import pallas as pl
from jax.experimental.pallas import tpu as pltpu

F32 = jnp.float32
BF16 = jnp.bfloat16

D_MODEL = 1024
CHUNK = 64
RET_HEADS = 4
RET_DK = 128
MLSTM_HEADS = 4
MLSTM_D = 128
HEAD_W = 128
D_GROUP = 512
CONV_W = 4
ROPE_BASE = 10000.0
PEER_HEADS = 8
PEER_NKEYS = 128
PEER_TOPK = 16
PEER_HALF = 128
PEER_BLOCK = 128
EPS = 1e-6

OFF_RQ, OFF_RK, OFF_RV, OFF_RG = 0, 512, 1024, 1536
OFF_MX, OFF_MV, OFF_MO = 2048, 2560, 3072
OFF_GI, OFF_GF = 3584, 3712
D_PROJ_PAD = 3840

LANES = 128
CONV_TAIL = 8
VMEM_LIMIT = 56 * 1024 * 1024

TM_PROJ = 256
L_BLOCK = 256
TM_FINAL = 1024


def _rms(x, g):
    ms = jnp.mean(x * x, axis=-1, keepdims=True)
    return x * lax.rsqrt(ms + EPS) * g


def _group_norm(h, g):
    mu = jnp.mean(h, axis=-1, keepdims=True)
    d = h - mu
    var = jnp.mean(d * d, axis=-1, keepdims=True)
    return d * lax.rsqrt(var + EPS) * g


def _silu(x):
    return x * (1.0 / (1.0 + jnp.exp(-x)))


def _sigmoid(x):
    return 1.0 / (1.0 + jnp.exp(-x))


def _dot(a, b):
    return jnp.dot(a.astype(BF16), b.astype(BF16), preferred_element_type=F32)


def _dot_tn(a, b):
    return lax.dot_general(a.astype(BF16), b.astype(BF16), (((0,), (0,)), ((), ())),
                           preferred_element_type=F32)


def _dot_nt(a, b):
    return lax.dot_general(a.astype(BF16), b.astype(BF16), (((1,), (1,)), ((), ())),
                           preferred_element_type=F32)


def _rms_proj_kernel(x_ref, g_ref, w_ref, o_ref):
    h = _rms(x_ref[...], g_ref[...])
    o_ref[...] = jnp.dot(h.astype(BF16), w_ref[...], preferred_element_type=F32)


def _rms_proj(x2, g, w):
    t, d = x2.shape
    n = w.shape[1]
    return pl.pallas_call(
        _rms_proj_kernel,
        grid=(t // TM_PROJ,),
        in_specs=[
            pl.BlockSpec((TM_PROJ, d), lambda i: (i, 0)),
            pl.BlockSpec((1, d), lambda i: (0, 0)),
            pl.BlockSpec((d, n), lambda i: (0, 0)),
        ],
        out_specs=pl.BlockSpec((TM_PROJ, n), lambda i: (i, 0)),
        out_shape=jax.ShapeDtypeStruct((t, n), F32),
        compiler_params=pltpu.CompilerParams(
            dimension_semantics=("arbitrary",), vmem_limit_bytes=VMEM_LIMIT),
        name="rms_in_proj",
    )(x2, g, w)


def _mixer_kernel(chunk_decay,
                  proj_ref, cos_ref, sin_ref, intra_ref, qd_ref, kd_ref, tri_ref,
                  rng_ref, cw_ref, cb_ref, wq_ref, wk_ref, bi_ref, bf_ref,
                  mng_ref, skip_ref,
                  y_ref,
                  rstate_ref, cstate_ref, m_ref, tail_ref):
    j = pl.program_id(1)
    n_chunks = L_BLOCK // CHUNK

    @pl.when(j == 0)
    def _():
        rstate_ref[...] = jnp.zeros_like(rstate_ref)
        cstate_ref[...] = jnp.zeros_like(cstate_ref)
        m_ref[...] = jnp.zeros_like(m_ref)
        tail_ref[...] = jnp.zeros_like(tail_ref)

    cosv = cos_ref[...]
    sinv = sin_ref[...]
    k_scale = RET_DK ** -0.5

    for h in range(RET_HEADS):
        lo = h * HEAD_W
        q = proj_ref[:, OFF_RQ + lo:OFF_RQ + lo + HEAD_W]
        k = proj_ref[:, OFF_RK + lo:OFF_RK + lo + HEAD_W]
        q = q * cosv + pltpu.roll(q, HEAD_W // 2, 1) * sinv
        k = (k * cosv + pltpu.roll(k, HEAD_W // 2, 1) * sinv) * k_scale
        intra = intra_ref[h]
        qd = qd_ref[h]
        kd = kd_ref[h]
        g = rng_ref[:, lo:lo + HEAD_W]
        for c in range(n_chunks):
            r0 = c * CHUNK
            qc = q[r0:r0 + CHUNK]
            kc = k[r0:r0 + CHUNK]
            vc = proj_ref[r0:r0 + CHUNK, OFF_RV + lo:OFF_RV + lo + HEAD_W]
            gate = proj_ref[r0:r0 + CHUNK, OFF_RG + lo:OFF_RG + lo + HEAD_W]
            state = rstate_ref[h]
            scores = _dot_nt(qc, kc) * intra
            o = _dot(scores, vc) + _dot(qc, state) * qd
            rstate_ref[h] = chunk_decay[h] * state + _dot_tn(kc * kd, vc)
            y_ref[r0:r0 + CHUNK, lo:lo + HEAD_W] = _silu(gate) * _group_norm(o, g)

    mx = proj_ref[:, OFF_MX:OFF_MX + D_GROUP]
    xp = jnp.concatenate([tail_ref[...], mx], axis=0)
    base = CONV_TAIL - (CONV_W - 1)
    conv = cb_ref[...] + xp[base:base + L_BLOCK] * cw_ref[0:1, :]
    for t in range(1, CONV_W):
        conv = conv + xp[base + t:base + t + L_BLOCK] * cw_ref[t:t + 1, :]
    tail_ref[...] = mx[L_BLOCK - CONV_TAIL:L_BLOCK]
    cact = _silu(conv)

    gi = proj_ref[:, OFF_GI:OFF_GI + LANES] + bi_ref[...]
    gf = proj_ref[:, OFF_GF:OFF_GF + LANES] + bf_ref[...]
    logf = jnp.minimum(gf, 0.0) - jnp.log1p(jnp.exp(-jnp.abs(gf)))
    tri = tri_ref[...]
    lane = lax.broadcasted_iota(jnp.int32, (CHUNK, HEAD_W), 1)
    ones_col = (lane == 0).astype(F32)
    m_scale = MLSTM_D ** -0.5

    mq = []
    mk = []
    for h in range(MLSTM_HEADS):
        lo = h * HEAD_W
        ch = cact[:, lo:lo + HEAD_W]
        mq.append(_dot(ch, wq_ref[h]))
        mk.append(_dot(ch, wk_ref[h]) * m_scale)

    for c in range(n_chunks):
        r0 = c * CHUNK
        fcum = jnp.dot(tri, logf[r0:r0 + CHUNK], preferred_element_type=F32,
                       precision=lax.Precision.HIGHEST)
        a = fcum[CHUNK - 1:CHUNK]
        logw = a - fcum + gi[r0:r0 + CHUNK]
        m_in = jnp.max(logw, axis=0, keepdims=True)
        m_old = m_ref[...]
        m_new = jnp.maximum(a + m_old, m_in)
        decay = jnp.exp(a + m_old - m_new)
        w = jnp.exp(logw - m_new)
        em = jnp.exp(-m_new)
        m_ref[...] = m_new
        for h in range(MLSTM_HEADS):
            lo = h * HEAD_W
            qh = mq[h][r0:r0 + CHUNK]
            kh = mk[h][r0:r0 + CHUNK]
            vh = proj_ref[r0:r0 + CHUNK, OFF_MV + lo:OFF_MV + lo + HEAD_W]
            oh = proj_ref[r0:r0 + CHUNK, OFF_MO + lo:OFF_MO + lo + HEAD_W]
            ch = cact[r0:r0 + CHUNK, lo:lo + HEAD_W]
            v_ext = jnp.concatenate([vh, ones_col], axis=1)
            cmem = decay[:, h:h + 1] * cstate_ref[h] + _dot_tn(kh * w[:, h:h + 1], v_ext)
            cstate_ref[h] = cmem
            num = _dot(qh, cmem)
            den = jnp.maximum(jnp.abs(num[:, HEAD_W:HEAD_W + 1]), em[:, h:h + 1])
            hh = num[:, :HEAD_W] / den
            hm = _group_norm(hh, mng_ref[:, lo:lo + HEAD_W])
            ym = _sigmoid(oh) * (hm + skip_ref[:, lo:lo + HEAD_W] * ch)
            y_ref[r0:r0 + CHUNK, D_GROUP + lo:D_GROUP + lo + HEAD_W] = ym


def _retention_tables():
    h = np.arange(RET_HEADS, dtype=np.float64)
    log_g = np.log(1.0 - 2.0 ** (-5.0 - h))
    l = np.arange(CHUNK, dtype=np.float64)
    intra = np.exp(log_g[:, None, None] * np.abs(l[:, None] - l[None, :]))
    qd = np.exp(log_g[:, None] * (l + 1.0))[:, :, None] * np.ones((1, 1, HEAD_W))
    kd = np.exp(log_g[:, None] * (CHUNK - 1.0 - l))[:, :, None] * np.ones((1, 1, HEAD_W))
    cd = np.exp(log_g * CHUNK)
    return (jnp.asarray(intra, F32), jnp.asarray(qd, F32), jnp.asarray(kd, F32),
            tuple(float(v) for v in cd))


def _rotary_tables(s):
    half = HEAD_W // 2
    inv = ROPE_BASE ** (-jnp.arange(half, dtype=F32) / half)
    ang = jnp.arange(s, dtype=jnp.int32).astype(F32)[:, None] * inv[None, :]
    cos = jnp.cos(ang)
    sin = jnp.sin(ang)
    return jnp.concatenate([cos, cos], axis=-1), jnp.concatenate([-sin, sin], axis=-1)


def _mixer(proj, b, s, ret_norm_g, m_conv_w, m_conv_b, m_wq, m_wk, m_b_i, m_b_f,
           m_norm_g, m_skip):
    intra, qd, kd, cd = _retention_tables()
    cos_t, sin_t = _rotary_tables(s)
    tri = jnp.asarray(np.tril(np.ones((CHUNK, CHUNK))), F32)
    nj = s // L_BLOCK
    pad = lambda v: jnp.pad(v.reshape(1, -1), ((0, 0), (0, LANES - v.size)))
    full = lambda shape: pl.BlockSpec(shape, lambda bi, j: (0,) * len(shape))
    return pl.pallas_call(
        functools.partial(_mixer_kernel, cd),
        grid=(b, nj),
        in_specs=[
            pl.BlockSpec((L_BLOCK, D_PROJ_PAD), lambda bi, j: (bi * nj + j, 0)),
            pl.BlockSpec((L_BLOCK, HEAD_W), lambda bi, j: (j, 0)),
            pl.BlockSpec((L_BLOCK, HEAD_W), lambda bi, j: (j, 0)),
            full((RET_HEADS, CHUNK, CHUNK)),
            full((RET_HEADS, CHUNK, HEAD_W)),
            full((RET_HEADS, CHUNK, HEAD_W)),
            full((CHUNK, CHUNK)),
            full((1, D_GROUP)),
            full((CONV_W, D_GROUP)),
            full((1, D_GROUP)),
            full((MLSTM_HEADS, MLSTM_D, MLSTM_D)),
            full((MLSTM_HEADS, MLSTM_D, MLSTM_D)),
            full((1, LANES)),
            full((1, LANES)),
            full((1, D_GROUP)),
            full((1, D_GROUP)),
        ],
        out_specs=pl.BlockSpec((L_BLOCK, 2 * D_GROUP), lambda bi, j: (bi * nj + j, 0)),
        out_shape=jax.ShapeDtypeStruct((b * s, 2 * D_GROUP), F32),
        scratch_shapes=[
            pltpu.VMEM((RET_HEADS, RET_DK, HEAD_W), F32),
            pltpu.VMEM((MLSTM_HEADS, MLSTM_D, 2 * HEAD_W), F32),
            pltpu.VMEM((1, LANES), F32),
            pltpu.VMEM((CONV_TAIL, D_GROUP), F32),
        ],
        compiler_params=pltpu.CompilerParams(
            dimension_semantics=("arbitrary", "arbitrary"), vmem_limit_bytes=VMEM_LIMIT),
        name="retention_mlstm_mixer",
    )(proj, cos_t, sin_t, intra, qd, kd, tri,
      ret_norm_g.reshape(1, D_GROUP), m_conv_w, m_conv_b.reshape(1, D_GROUP),
      m_wq.astype(BF16), m_wk.astype(BF16), pad(m_b_i), pad(m_b_f),
      m_norm_g.reshape(1, D_GROUP), m_skip.reshape(1, D_GROUP))


def _post_kernel(y_ref, x_ref, wout_ref, g2_ref, wq_ref, keys_ref,
                 x1_ref, xn_ref, sub_ref):
    x1 = x_ref[...] + jnp.dot(y_ref[...].astype(BF16), wout_ref[...],
                              preferred_element_type=F32)
    x1_ref[...] = x1
    xn = _rms(x1, g2_ref[...])
    xn_ref[...] = xn
    q = jnp.dot(xn.astype(BF16), wq_ref[...], preferred_element_type=F32)
    for i in range(2 * PEER_HEADS):
        lo = i * PEER_HALF
        sub_ref[:, lo:lo + PEER_NKEYS] = jnp.dot(
            q[:, lo:lo + PEER_HALF].astype(BF16), keys_ref[i], preferred_element_type=F32)


def _post(y, x2, w_out, norm2_g, peer_wq, keys_t):
    t, d = x2.shape
    nq = peer_wq.shape[1]
    row = lambda w: pl.BlockSpec((TM_PROJ, w), lambda i: (i, 0))
    full = lambda shape: pl.BlockSpec(shape, lambda i: (0,) * len(shape))
    return pl.pallas_call(
        _post_kernel,
        grid=(t // TM_PROJ,),
        in_specs=[row(d), row(d), full((d, d)), full((1, d)), full((d, nq)),
                  full(keys_t.shape)],
        out_specs=[row(d), row(d), row(nq)],
        out_shape=[jax.ShapeDtypeStruct((t, d), F32), jax.ShapeDtypeStruct((t, d), F32),
                   jax.ShapeDtypeStruct((t, nq), F32)],
        compiler_params=pltpu.CompilerParams(
            dimension_semantics=("arbitrary",), vmem_limit_bytes=VMEM_LIMIT),
        name="out_proj_peer_scores",
    )(y, x2, w_out, norm2_g, peer_wq, keys_t)


def _final_kernel(x1_ref, p_ref, g_ref, o_ref):
    o_ref[...] = _rms(x1_ref[...] + p_ref[...], g_ref[...])


def _final(x1, p, g):
    t, d = x1.shape
    row = pl.BlockSpec((TM_FINAL, d), lambda i: (i, 0))
    return pl.pallas_call(
        _final_kernel,
        grid=(t // TM_FINAL,),
        in_specs=[row, row, pl.BlockSpec((1, d), lambda i: (0, 0))],
        out_specs=row,
        out_shape=jax.ShapeDtypeStruct((t, d), F32),
        compiler_params=pltpu.CompilerParams(dimension_semantics=("arbitrary",)),
        name="final_rmsnorm",
    )(x1, p, g)


def _peer_route(sub):
    t = sub.shape[0]
    sub = sub.reshape(t, PEER_HEADS, 2, PEER_NKEYS)
    s_top, i_top = lax.top_k(sub, PEER_TOPK)
    cand = (s_top[:, :, 0, :, None] + s_top[:, :, 1, None, :]).reshape(
        t, PEER_HEADS, PEER_TOPK * PEER_TOPK)
    best, flat = lax.top_k(cand, PEER_TOPK)
    i1 = jnp.take_along_axis(i_top[:, :, 0, :], flat // PEER_TOPK, axis=-1)
    i2 = jnp.take_along_axis(i_top[:, :, 1, :], flat % PEER_TOPK, axis=-1)
    expert = i1 * PEER_NKEYS + i2
    gates = jax.nn.softmax(best, axis=-1)
    e = PEER_HEADS * PEER_TOPK
    return expert.reshape(t, e), gates.reshape(t, e)


def _peer_experts(xn, expert, gates, down, up):
    t, d = xn.shape
    nb = t // PEER_BLOCK
    e = expert.shape[1]
    xb = xn.reshape(nb, PEER_BLOCK, d)
    eb = expert.reshape(nb, PEER_BLOCK, e)
    gb = gates.reshape(nb, PEER_BLOCK, e)

    def block(args):
        xi, ei, gi = args
        u = jnp.take(down, ei, axis=0)
        act = jax.nn.gelu(jnp.einsum('td,ted->te', xi, u), approximate=False)
        vv = jnp.take(up, ei, axis=0)
        return jnp.einsum('te,ted->td', gi * act, vv)

    return lax.map(block, (xb, eb, gb)).reshape(t, d)


def kernel(x, norm1_g, w_in, ret_norm_g, m_conv_w, m_conv_b, m_wq, m_wk, m_b_i, m_b_f,
           m_norm_g, m_skip, w_out, norm2_g, peer_wq, peer_keys, peer_down, peer_up, final_g):
    b, s, d = x.shape
    t = b * s
    x2 = x.reshape(t, d)
    depth = norm1_g.shape[0]
    for i in range(depth):
        if i > 0:
            x2 = x1 + p
        wi = w_in[i]
        n_main = OFF_GI
        gate_pad = lambda w: jnp.pad(w, ((0, 0), (0, LANES - w.shape[1])))
        w_pad = jnp.concatenate(
            [wi[:, :n_main], gate_pad(wi[:, n_main:n_main + MLSTM_HEADS]),
             gate_pad(wi[:, n_main + MLSTM_HEADS:])], axis=1).astype(BF16)
        proj = _rms_proj(x2, norm1_g[i].reshape(1, d), w_pad)
        y = _mixer(proj, b, s, ret_norm_g[i], m_conv_w[i], m_conv_b[i], m_wq[i], m_wk[i],
                   m_b_i[i], m_b_f[i], m_norm_g[i], m_skip[i])
        keys_t = peer_keys[i].reshape(2 * PEER_HEADS, PEER_NKEYS, PEER_HALF)
        keys_t = jnp.swapaxes(keys_t, 1, 2).astype(BF16)
        x1, xn, sub = _post(y, x2, w_out[i].astype(BF16), norm2_g[i].reshape(1, d),
                            peer_wq[i].astype(BF16), keys_t)
        expert, gates = _peer_route(sub)
        p = _peer_experts(xn, expert, gates, peer_down[i], peer_up[i])
    out = _final(x1, p, final_g.reshape(1, d))
    return out.reshape(b, s, d)
```

```python
import functools

import numpy as np
import jax
import jax.numpy as jnp
from jax import lax
from jax.experimental import pallas as pl
from jax.experimental.pallas import tpu as pltpu
from jax.experimental.pallas import tpu_sc as plsc

F32 = jnp.float32
BF16 = jnp.bfloat16

D_MODEL = 1024
CHUNK = 64
RET_HEADS = 4
RET_DK = 128
MLSTM_HEADS = 4
MLSTM_D = 128
HEAD_W = 128
D_GROUP = 512
CONV_W = 4
ROPE_BASE = 10000.0
PEER_HEADS = 8
PEER_NKEYS = 128
PEER_TOPK = 16
PEER_HALF = 128
PEER_BLOCK = 128
EPS = 1e-6

OFF_RQ, OFF_RK, OFF_RV, OFF_RG = 0, 512, 1024, 1536
OFF_MX, OFF_MV, OFF_MO = 2048, 2560, 3072
OFF_GI, OFF_GF = 3584, 3712
D_PROJ_PAD = 3840

LANES = 128
CONV_TAIL = 8
VMEM_LIMIT = 56 * 1024 * 1024

TM_PROJ = 256
L_BLOCK = 256
TM_FINAL = 1024


def _rms(x, g):
    ms = jnp.mean(x * x, axis=-1, keepdims=True)
    return x * lax.rsqrt(ms + EPS) * g


def _group_norm(h, g):
    mu = jnp.mean(h, axis=-1, keepdims=True)
    d = h - mu
    var = jnp.mean(d * d, axis=-1, keepdims=True)
    return d * lax.rsqrt(var + EPS) * g


def _silu(x):
    return x * (1.0 / (1.0 + jnp.exp(-x)))


def _sigmoid(x):
    return 1.0 / (1.0 + jnp.exp(-x))


def _dot(a, b):
    return jnp.dot(a.astype(BF16), b.astype(BF16), preferred_element_type=F32)


def _dot_tn(a, b):
    return lax.dot_general(a.astype(BF16), b.astype(BF16), (((0,), (0,)), ((), ())),
                           preferred_element_type=F32)


def _dot_nt(a, b):
    return lax.dot_general(a.astype(BF16), b.astype(BF16), (((1,), (1,)), ((), ())),
                           preferred_element_type=F32)


def _rms_proj_kernel(x_ref, g_ref, w_ref, o_ref):
    h = _rms(x_ref[...], g_ref[...])
    o_ref[...] = jnp.dot(h.astype(BF16), w_ref[...], preferred_element_type=F32)


def _rms_proj(x2, g, w):
    t, d = x2.shape
    n = w.shape[1]
    return pl.pallas_call(
        _rms_proj_kernel,
        grid=(t // TM_PROJ,),
        in_specs=[
            pl.BlockSpec((TM_PROJ, d), lambda i: (i, 0)),
            pl.BlockSpec((1, d), lambda i: (0, 0)),
            pl.BlockSpec((d, n), lambda i: (0, 0)),
        ],
        out_specs=pl.BlockSpec((TM_PROJ, n), lambda i: (i, 0)),
        out_shape=jax.ShapeDtypeStruct((t, n), F32),
        compiler_params=pltpu.CompilerParams(
            dimension_semantics=("arbitrary",), vmem_limit_bytes=VMEM_LIMIT),
        name="rms_in_proj",
    )(x2, g, w)


def _mixer_kernel(chunk_decay,
                  proj_ref, cos_ref, sin_ref, intra_ref, qd_ref, kd_ref, tri_ref,
                  rng_ref, cw_ref, cb_ref, wq_ref, wk_ref, bi_ref, bf_ref,
                  mng_ref, skip_ref,
                  y_ref,
                  rstate_ref, cstate_ref, m_ref, tail_ref):
    j = pl.program_id(1)
    n_chunks = L_BLOCK // CHUNK

    @pl.when(j == 0)
    def _():
        rstate_ref[...] = jnp.zeros_like(rstate_ref)
        cstate_ref[...] = jnp.zeros_like(cstate_ref)
        m_ref[...] = jnp.zeros_like(m_ref)
        tail_ref[...] = jnp.zeros_like(tail_ref)

    cosv = cos_ref[...]
    sinv = sin_ref[...]
    k_scale = RET_DK ** -0.5

    for h in range(RET_HEADS):
        lo = h * HEAD_W
        q = proj_ref[:, OFF_RQ + lo:OFF_RQ + lo + HEAD_W]
        k = proj_ref[:, OFF_RK + lo:OFF_RK + lo + HEAD_W]
        q = q * cosv + pltpu.roll(q, HEAD_W // 2, 1) * sinv
        k = (k * cosv + pltpu.roll(k, HEAD_W // 2, 1) * sinv) * k_scale
        intra = intra_ref[h]
        qd = qd_ref[h]
        kd = kd_ref[h]
        g = rng_ref[:, lo:lo + HEAD_W]
        for c in range(n_chunks):
            r0 = c * CHUNK
            qc = q[r0:r0 + CHUNK]
            kc = k[r0:r0 + CHUNK]
            vc = proj_ref[r0:r0 + CHUNK, OFF_RV + lo:OFF_RV + lo + HEAD_W]
            gate = proj_ref[r0:r0 + CHUNK, OFF_RG + lo:OFF_RG + lo + HEAD_W]
            state = rstate_ref[h]
            scores = _dot_nt(qc, kc) * intra
            o = _dot(scores, vc) + _dot(qc, state) * qd
            rstate_ref[h] = chunk_decay[h] * state + _dot_tn(kc * kd, vc)
            y_ref[r0:r0 + CHUNK, lo:lo + HEAD_W] = _silu(gate) * _group_norm(o, g)

    mx = proj_ref[:, OFF_MX:OFF_MX + D_GROUP]
    xp = jnp.concatenate([tail_ref[...], mx], axis=0)
    base = CONV_TAIL - (CONV_W - 1)
    conv = cb_ref[...] + xp[base:base + L_BLOCK] * cw_ref[0:1, :]
    for t in range(1, CONV_W):
        conv = conv + xp[base + t:base + t + L_BLOCK] * cw_ref[t:t + 1, :]
    tail_ref[...] = mx[L_BLOCK - CONV_TAIL:L_BLOCK]
    cact = _silu(conv)

    gi = proj_ref[:, OFF_GI:OFF_GI + LANES] + bi_ref[...]
    gf = proj_ref[:, OFF_GF:OFF_GF + LANES] + bf_ref[...]
    logf = jnp.minimum(gf, 0.0) - jnp.log1p(jnp.exp(-jnp.abs(gf)))
    tri = tri_ref[...]
    lane = lax.broadcasted_iota(jnp.int32, (CHUNK, HEAD_W), 1)
    ones_col = (lane == 0).astype(F32)
    m_scale = MLSTM_D ** -0.5

    mq = []
    mk = []
    for h in range(MLSTM_HEADS):
        lo = h * HEAD_W
        ch = cact[:, lo:lo + HEAD_W]
        mq.append(_dot(ch, wq_ref[h]))
        mk.append(_dot(ch, wk_ref[h]) * m_scale)

    for c in range(n_chunks):
        r0 = c * CHUNK
        fcum = jnp.dot(tri, logf[r0:r0 + CHUNK], preferred_element_type=F32,
                       precision=lax.Precision.HIGHEST)
        a = fcum[CHUNK - 1:CHUNK]
        logw = a - fcum + gi[r0:r0 + CHUNK]
        m_in = jnp.max(logw, axis=0, keepdims=True)
        m_old = m_ref[...]
        m_new = jnp.maximum(a + m_old, m_in)
        decay = jnp.exp(a + m_old - m_new)
        w = jnp.exp(logw - m_new)
        em = jnp.exp(-m_new)
        m_ref[...] = m_new
        for h in range(MLSTM_HEADS):
            lo = h * HEAD_W
            qh = mq[h][r0:r0 + CHUNK]
            kh = mk[h][r0:r0 + CHUNK]
            vh = proj_ref[r0:r0 + CHUNK, OFF_MV + lo:OFF_MV + lo + HEAD_W]
            oh = proj_ref[r0:r0 + CHUNK, OFF_MO + lo:OFF_MO + lo + HEAD_W]
            ch = cact[r0:r0 + CHUNK, lo:lo + HEAD_W]
            v_ext = jnp.concatenate([vh, ones_col], axis=1)
            cmem = decay[:, h:h + 1] * cstate_ref[h] + _dot_tn(kh * w[:, h:h + 1], v_ext)
            cstate_ref[h] = cmem
            num = _dot(qh, cmem)
            den = jnp.maximum(jnp.abs(num[:, HEAD_W:HEAD_W + 1]), em[:, h:h + 1])
            hh = num[:, :HEAD_W] / den
            hm = _group_norm(hh, mng_ref[:, lo:lo + HEAD_W])
            ym = _sigmoid(oh) * (hm + skip_ref[:, lo:lo + HEAD_W] * ch)
            y_ref[r0:r0 + CHUNK, D_GROUP + lo:D_GROUP + lo + HEAD_W] = ym


def _retention_tables():
    h = np.arange(RET_HEADS, dtype=np.float64)
    log_g = np.log(1.0 - 2.0 ** (-5.0 - h))
    l = np.arange(CHUNK, dtype=np.float64)
    intra = np.exp(log_g[:, None, None] * np.abs(l[:, None] - l[None, :]))
    qd = np.exp(log_g[:, None] * (l + 1.0))[:, :, None] * np.ones((1, 1, HEAD_W))
    kd = np.exp(log_g[:, None] * (CHUNK - 1.0 - l))[:, :, None] * np.ones((1, 1, HEAD_W))
    cd = np.exp(log_g * CHUNK)
    return (jnp.asarray(intra, F32), jnp.asarray(qd, F32), jnp.asarray(kd, F32),
            tuple(float(v) for v in cd))


def _rotary_tables(s):
    half = HEAD_W // 2
    inv = ROPE_BASE ** (-jnp.arange(half, dtype=F32) / half)
    ang = jnp.arange(s, dtype=jnp.int32).astype(F32)[:, None] * inv[None, :]
    cos = jnp.cos(ang)
    sin = jnp.sin(ang)
    return jnp.concatenate([cos, cos], axis=-1), jnp.concatenate([-sin, sin], axis=-1)


def _mixer(proj, b, s, ret_norm_g, m_conv_w, m_conv_b, m_wq, m_wk, m_b_i, m_b_f,
           m_norm_g, m_skip):
    intra, qd, kd, cd = _retention_tables()
    cos_t, sin_t = _rotary_tables(s)
    tri = jnp.asarray(np.tril(np.ones((CHUNK, CHUNK))), F32)
    nj = s // L_BLOCK
    pad = lambda v: jnp.pad(v.reshape(1, -1), ((0, 0), (0, LANES - v.size)))
    full = lambda shape: pl.BlockSpec(shape, lambda bi, j: (0,) * len(shape))
    return pl.pallas_call(
        functools.partial(_mixer_kernel, cd),
        grid=(b, nj),
        in_specs=[
            pl.BlockSpec((L_BLOCK, D_PROJ_PAD), lambda bi, j: (bi * nj + j, 0)),
            pl.BlockSpec((L_BLOCK, HEAD_W), lambda bi, j: (j, 0)),
            pl.BlockSpec((L_BLOCK, HEAD_W), lambda bi, j: (j, 0)),
            full((RET_HEADS, CHUNK, CHUNK)),
            full((RET_HEADS, CHUNK, HEAD_W)),
            full((RET_HEADS, CHUNK, HEAD_W)),
            full((CHUNK, CHUNK)),
            full((1, D_GROUP)),
            full((CONV_W, D_GROUP)),
            full((1, D_GROUP)),
            full((MLSTM_HEADS, MLSTM_D, MLSTM_D)),
            full((MLSTM_HEADS, MLSTM_D, MLSTM_D)),
            full((1, LANES)),
            full((1, LANES)),
            full((1, D_GROUP)),
            full((1, D_GROUP)),
        ],
        out_specs=pl.BlockSpec((L_BLOCK, 2 * D_GROUP), lambda bi, j: (bi * nj + j, 0)),
        out_shape=jax.ShapeDtypeStruct((b * s, 2 * D_GROUP), F32),
        scratch_shapes=[
            pltpu.VMEM((RET_HEADS, RET_DK, HEAD_W), F32),
            pltpu.VMEM((MLSTM_HEADS, MLSTM_D, 2 * HEAD_W), F32),
            pltpu.VMEM((1, LANES), F32),
            pltpu.VMEM((CONV_TAIL, D_GROUP), F32),
        ],
        compiler_params=pltpu.CompilerParams(
            dimension_semantics=("arbitrary", "arbitrary"), vmem_limit_bytes=VMEM_LIMIT),
        name="retention_mlstm_mixer",
    )(proj, cos_t, sin_t, intra, qd, kd, tri,
      ret_norm_g.reshape(1, D_GROUP), m_conv_w, m_conv_b.reshape(1, D_GROUP),
      m_wq.astype(BF16), m_wk.astype(BF16), pad(m_b_i), pad(m_b_f),
      m_norm_g.reshape(1, D_GROUP), m_skip.reshape(1, D_GROUP))


def _post_kernel(y_ref, x_ref, wout_ref, g2_ref, wq_ref, keys_ref,
                 x1_ref, xn_ref, sub_ref):
    x1 = x_ref[...] + jnp.dot(y_ref[...].astype(BF16), wout_ref[...],
                              preferred_element_type=F32)
    x1_ref[...] = x1
    xn = _rms(x1, g2_ref[...])
    xn_ref[...] = xn
    q = jnp.dot(xn.astype(BF16), wq_ref[...], preferred_element_type=F32)
    for i in range(2 * PEER_HEADS):
        lo = i * PEER_HALF
        sub_ref[:, lo:lo + PEER_NKEYS] = jnp.dot(
            q[:, lo:lo + PEER_HALF].astype(BF16), keys_ref[i], preferred_element_type=F32)


def _post(y, x2, w_out, norm2_g, peer_wq, keys_t):
    t, d = x2.shape
    nq = peer_wq.shape[1]
    row = lambda w: pl.BlockSpec((TM_PROJ, w), lambda i: (i, 0))
    full = lambda shape: pl.BlockSpec(shape, lambda i: (0,) * len(shape))
    return pl.pallas_call(
        _post_kernel,
        grid=(t // TM_PROJ,),
        in_specs=[row(d), row(d), full((d, d)), full((1, d)), full((d, nq)),
                  full(keys_t.shape)],
        out_specs=[row(d), row(d), row(nq)],
        out_shape=[jax.ShapeDtypeStruct((t, d), F32), jax.ShapeDtypeStruct((t, d), F32),
                   jax.ShapeDtypeStruct((t, nq), F32)],
        compiler_params=pltpu.CompilerParams(
            dimension_semantics=("arbitrary",), vmem_limit_bytes=VMEM_LIMIT),
        name="out_proj_peer_scores",
    )(y, x2, w_out, norm2_g, peer_wq, keys_t)


def _final_kernel(x1_ref, p_ref, g_ref, o_ref):
    o_ref[...] = _rms(x1_ref[...] + p_ref[...], g_ref[...])


def _final(x1, p, g):
    t, d = x1.shape
    row = pl.BlockSpec((TM_FINAL, d), lambda i: (i, 0))
    return pl.pallas_call(
        _final_kernel,
        grid=(t // TM_FINAL,),
        in_specs=[row, row, pl.BlockSpec((1, d), lambda i: (0, 0))],
        out_specs=row,
        out_shape=jax.ShapeDtypeStruct((t, d), F32),
        compiler_params=pltpu.CompilerParams(dimension_semantics=("arbitrary",)),
        name="final_rmsnorm",
    )(x1, p, g)


def _peer_route(sub):
    t = sub.shape[0]
    sub = sub.reshape(t, PEER_HEADS, 2, PEER_NKEYS)
    s_top, i_top = lax.top_k(sub, PEER_TOPK)
    cand = (s_top[:, :, 0, :, None] + s_top[:, :, 1, None, :]).reshape(
        t, PEER_HEADS, PEER_TOPK * PEER_TOPK)
    best, flat = lax.top_k(cand, PEER_TOPK)
    i1 = jnp.take_along_axis(i_top[:, :, 0, :], flat // PEER_TOPK, axis=-1)
    i2 = jnp.take_along_axis(i_top[:, :, 1, :], flat % PEER_TOPK, axis=-1)
    expert = i1 * PEER_NKEYS + i2
    gates = jax.nn.softmax(best, axis=-1)
    e = PEER_HEADS * PEER_TOPK
    return expert.reshape(t, e), gates.reshape(t, e)


SC_CORES = 2
SC_SUBCORES = 16
SC_LANES = 16
SC_WORKERS = SC_CORES * SC_SUBCORES
PEER_E = PEER_HEADS * PEER_TOPK
SC_ROWS = 32
SC_GATHERS = PEER_E // SC_ROWS
SC_TOK_BLOCK = 8
SC_QUARTER = 256


def _sc_mesh():
    return plsc.VectorSubcoreMesh(core_axis_name="c", subcore_axis_name="s")


def _sc_pipeline(idx_v, tab_hbm, rows_v, sems, compute):
    def gather(step, buf):
        return pltpu.make_async_copy(tab_hbm.at[idx_v.at[step]], rows_v.at[buf], sems.at[buf])

    gather(0, 0).start()

    @pl.loop(0, SC_TOK_BLOCK)
    def _(i):
        for c in range(SC_GATHERS):
            buf = c % 2
            step = i * SC_GATHERS + c
            gather(step, buf).wait()
            if c + 1 < SC_GATHERS:
                gather(step + 1, 1 - buf).start()
            else:
                @pl.when(i + 1 < SC_TOK_BLOCK)
                def _():
                    gather(step + 1, 1 - buf).start()
            compute(i, c, step, buf)


def _peer_down_dots(xn, idx, down):
    t, d = xn.shape
    tok_w = t // SC_WORKERS
    nblk = tok_w // SC_TOK_BLOCK
    lanes = SC_LANES

    @functools.partial(
        pl.kernel, mesh=_sc_mesh(),
        out_type=jax.ShapeDtypeStruct((t, PEER_E), F32),
        scratch_types=[
            pltpu.VMEM((SC_TOK_BLOCK * SC_GATHERS, SC_ROWS), jnp.int32),
            pltpu.VMEM((SC_TOK_BLOCK, d), F32),
            pltpu.VMEM((2, SC_ROWS, d), F32),
            pltpu.VMEM((SC_ROWS, lanes), F32),
            pltpu.VMEM((SC_TOK_BLOCK, PEER_E), F32),
            pltpu.SemaphoreType.DMA((2,)),
        ],
        compiler_params=pltpu.CompilerParams(needs_layout_passes=False),
        name="peer_down_dots",
    )
    def k(x_hbm, idx_hbm, tab_hbm, out_hbm, idx_v, x_v, rows_v, acc_v, out_v, sems):
        wid = lax.axis_index("s") * SC_CORES + lax.axis_index("c")
        lane = lax.iota(jnp.int32, lanes)

        @pl.loop(0, nblk)
        def _(blk):
            tok0 = wid * tok_w + blk * SC_TOK_BLOCK
            pltpu.sync_copy(idx_hbm.at[pl.ds(tok0 * SC_GATHERS, SC_TOK_BLOCK * SC_GATHERS)], idx_v)
            pltpu.sync_copy(x_hbm.at[pl.ds(tok0, SC_TOK_BLOCK)], x_v)

            def compute(i, c, step, buf):
                def body(dc, accs):
                    xv = x_v[i, pl.ds(dc * lanes, lanes)]
                    return tuple(a + rows_v[buf, r, pl.ds(dc * lanes, lanes)] * xv
                                 for r, a in enumerate(accs))
                accs = lax.fori_loop(0, d // lanes, body,
                                     tuple(jnp.zeros((lanes,), F32) for _ in range(SC_ROWS)))
                for r in range(SC_ROWS):
                    acc_v[r, :] = accs[r]
                for g in range(SC_ROWS // lanes):
                    tot = jnp.zeros((lanes,), F32)
                    for l in range(lanes):
                        tot = tot + plsc.load_gather(
                            acc_v, [lane + g * lanes, jnp.full((lanes,), l, jnp.int32)])
                    out_v[i, pl.ds(c * SC_ROWS + g * lanes, lanes)] = tot

            _sc_pipeline(idx_v, tab_hbm, rows_v, sems, compute)
            pltpu.sync_copy(out_v, out_hbm.at[pl.ds(tok0, SC_TOK_BLOCK)])

    return k(xn, idx.reshape(t * SC_GATHERS, SC_ROWS), down)


def _peer_up_sum(coef, idx, up):
    t = coef.shape[0]
    d = up.shape[1]
    tok_w = t // SC_WORKERS
    nblk = tok_w // SC_TOK_BLOCK
    lanes = SC_LANES
    nj = SC_QUARTER // lanes

    @functools.partial(
        pl.kernel, mesh=_sc_mesh(),
        out_type=jax.ShapeDtypeStruct((t, d), F32),
        scratch_types=[
            pltpu.VMEM((SC_TOK_BLOCK * SC_GATHERS, SC_ROWS), jnp.int32),
            pltpu.VMEM((SC_TOK_BLOCK * PEER_E,), F32),
            pltpu.VMEM((2, SC_ROWS, d), F32),
            pltpu.VMEM((SC_TOK_BLOCK, d), F32),
            pltpu.SemaphoreType.DMA((2,)),
        ],
        compiler_params=pltpu.CompilerParams(needs_layout_passes=False),
        name="peer_up_sum",
    )
    def k(coef_hbm, idx_hbm, tab_hbm, out_hbm, idx_v, coef_v, rows_v, out_v, sems):
        wid = lax.axis_index("s") * SC_CORES + lax.axis_index("c")

        @pl.loop(0, nblk)
        def _(blk):
            tok0 = wid * tok_w + blk * SC_TOK_BLOCK
            pltpu.sync_copy(idx_hbm.at[pl.ds(tok0 * SC_GATHERS, SC_TOK_BLOCK * SC_GATHERS)], idx_v)
            pltpu.sync_copy(coef_hbm.at[pl.ds(tok0 * PEER_E, SC_TOK_BLOCK * PEER_E)], coef_v)

            def compute(i, c, step, buf):
                for q in range(d // SC_QUARTER):
                    col = lambda j: pl.ds(q * SC_QUARTER + j * lanes, lanes)
                    if c == 0:
                        init = tuple(jnp.zeros((lanes,), F32) for _ in range(nj))
                    else:
                        init = tuple(out_v[i, col(j)] for j in range(nj))

                    def body(r, accs):
                        cf = plsc.load_gather(
                            coef_v, [jnp.full((lanes,), step * SC_ROWS + r, jnp.int32)])
                        return tuple(a + rows_v[buf, r, col(j)] * cf for j, a in enumerate(accs))
                    accs = lax.fori_loop(0, SC_ROWS, body, init)
                    for j in range(nj):
                        out_v[i, col(j)] = accs[j]

            _sc_pipeline(idx_v, tab_hbm, rows_v, sems, compute)
            pltpu.sync_copy(out_v, out_hbm.at[pl.ds(tok0, SC_TOK_BLOCK)])

    return k(coef.reshape(t * PEER_E), idx.reshape(t * SC_GATHERS, SC_ROWS), up)


def _coef_kernel(pre_ref, gate_ref, o_ref):
    x = pre_ref[...]
    gelu = 0.5 * x * (1.0 + lax.erf(x * (2.0 ** -0.5)))
    o_ref[...] = gate_ref[...] * gelu


def _peer_coef(pre, gates):
    t, e = pre.shape
    row = pl.BlockSpec((TM_FINAL, e), lambda i: (i, 0))
    return pl.pallas_call(
        _coef_kernel,
        grid=(t // TM_FINAL,),
        in_specs=[row, row],
        out_specs=row,
        out_shape=jax.ShapeDtypeStruct((t, e), F32),
        compiler_params=pltpu.CompilerParams(dimension_semantics=("arbitrary",)),
        name="peer_gate_gelu",
    )(pre, gates)


def _peer_experts(xn, expert, gates, down, up):
    pre = _peer_down_dots(xn, expert, down)
    coef = _peer_coef(pre, gates)
    return _peer_up_sum(coef, expert, up)


def kernel(x, norm1_g, w_in, ret_norm_g, m_conv_w, m_conv_b, m_wq, m_wk, m_b_i, m_b_f,
           m_norm_g, m_skip, w_out, norm2_g, peer_wq, peer_keys, peer_down, peer_up, final_g):
    b, s, d = x.shape
    t = b * s
    x2 = x.reshape(t, d)
    depth = norm1_g.shape[0]
    for i in range(depth):
        if i > 0:
            x2 = x1 + p
        wi = w_in[i]
        n_main = OFF_GI
        gate_pad = lambda w: jnp.pad(w, ((0, 0), (0, LANES - w.shape[1])))
        w_pad = jnp.concatenate(
            [wi[:, :n_main], gate_pad(wi[:, n_main:n_main + MLSTM_HEADS]),
             gate_pad(wi[:, n_main + MLSTM_HEADS:])], axis=1).astype(BF16)
        proj = _rms_proj(x2, norm1_g[i].reshape(1, d), w_pad)
        y = _mixer(proj, b, s, ret_norm_g[i], m_conv_w[i], m_conv_b[i], m_wq[i], m_wk[i],
                   m_b_i[i], m_b_f[i], m_norm_g[i], m_skip[i])
        keys_t = peer_keys[i].reshape(2 * PEER_HEADS, PEER_NKEYS, PEER_HALF)
        keys_t = jnp.swapaxes(keys_t, 1, 2).astype(BF16)
        x1, xn, sub = _post(y, x2, w_out[i].astype(BF16), norm2_g[i].reshape(1, d),
                            peer_wq[i].astype(BF16), keys_t)
        expert, gates = _peer_route(sub)
        p = _peer_experts(xn, expert, gates, peer_down[i], peer_up[i])
    out = _final(x1, p, final_g.reshape(1, d))
    return out.reshape(b, s, d)
```

```python
import functools

import numpy as np
import jax
import jax.numpy as jnp
from jax import lax
from jax.experimental import pallas as pl
from jax.experimental.pallas import tpu as pltpu
from jax.experimental.pallas import tpu_sc as plsc

F32 = jnp.float32
BF16 = jnp.bfloat16

D_MODEL = 1024
CHUNK = 64
RET_HEADS = 4
RET_DK = 128
MLSTM_HEADS = 4
MLSTM_D = 128
HEAD_W = 128
D_GROUP = 512
CONV_W = 4
ROPE_BASE = 10000.0
PEER_HEADS = 8
PEER_NKEYS = 128
PEER_TOPK = 16
PEER_HALF = 128
PEER_BLOCK = 128
EPS = 1e-6

OFF_RQ, OFF_RK, OFF_RV, OFF_RG = 0, 512, 1024, 1536
OFF_MX, OFF_MV, OFF_MO = 2048, 2560, 3072
OFF_GI, OFF_GF = 3584, 3712
D_PROJ_PAD = 3840

LANES = 128
CONV_TAIL = 8
VMEM_LIMIT = 56 * 1024 * 1024

TM_PROJ = 256
L_BLOCK = 256
TM_FINAL = 1024


def _rms(x, g):
    ms = jnp.mean(x * x, axis=-1, keepdims=True)
    return x * lax.rsqrt(ms + EPS) * g


def _group_norm(h, g):
    mu = jnp.mean(h, axis=-1, keepdims=True)
    d = h - mu
    var = jnp.mean(d * d, axis=-1, keepdims=True)
    return d * lax.rsqrt(var + EPS) * g


def _silu(x):
    return x * (1.0 / (1.0 + jnp.exp(-x)))


def _sigmoid(x):
    return 1.0 / (1.0 + jnp.exp(-x))


def _dot(a, b):
    return jnp.dot(a.astype(BF16), b.astype(BF16), preferred_element_type=F32)


def _dot_tn(a, b):
    return lax.dot_general(a.astype(BF16), b.astype(BF16), (((0,), (0,)), ((), ())),
                           preferred_element_type=F32)


def _dot_nt(a, b):
    return lax.dot_general(a.astype(BF16), b.astype(BF16), (((1,), (1,)), ((), ())),
                           preferred_element_type=F32)


def _rms_proj_kernel(x_ref, g_ref, w_ref, o_ref):
    h = _rms(x_ref[...], g_ref[...])
    o_ref[...] = jnp.dot(h.astype(BF16), w_ref[...], preferred_element_type=F32)


def _rms_proj(x2, g, w):
    t, d = x2.shape
    n = w.shape[1]
    return pl.pallas_call(
        _rms_proj_kernel,
        grid=(t // TM_PROJ,),
        in_specs=[
            pl.BlockSpec((TM_PROJ, d), lambda i: (i, 0)),
            pl.BlockSpec((1, d), lambda i: (0, 0)),
            pl.BlockSpec((d, n), lambda i: (0, 0)),
        ],
        out_specs=pl.BlockSpec((TM_PROJ, n), lambda i: (i, 0)),
        out_shape=jax.ShapeDtypeStruct((t, n), F32),
        compiler_params=pltpu.CompilerParams(
            dimension_semantics=("arbitrary",), vmem_limit_bytes=VMEM_LIMIT),
        name="rms_in_proj",
    )(x2, g, w)


def _mixer_kernel(chunk_decay,
                  proj_ref, cos_ref, sin_ref, intra_ref, qd_ref, kd_ref, tri_ref,
                  rng_ref, cw_ref, cb_ref, wq_ref, wk_ref, bi_ref, bf_ref,
                  mng_ref, skip_ref,
                  y_ref,
                  rstate_ref, cstate_ref, m_ref, tail_ref):
    j = pl.program_id(1)
    n_chunks = L_BLOCK // CHUNK

    @pl.when(j == 0)
    def _():
        rstate_ref[...] = jnp.zeros_like(rstate_ref)
        cstate_ref[...] = jnp.zeros_like(cstate_ref)
        m_ref[...] = jnp.zeros_like(m_ref)
        tail_ref[...] = jnp.zeros_like(tail_ref)

    cosv = cos_ref[...]
    sinv = sin_ref[...]
    k_scale = RET_DK ** -0.5

    for h in range(RET_HEADS):
        lo = h * HEAD_W
        q = proj_ref[:, OFF_RQ + lo:OFF_RQ + lo + HEAD_W]
        k = proj_ref[:, OFF_RK + lo:OFF_RK + lo + HEAD_W]
        q = q * cosv + pltpu.roll(q, HEAD_W // 2, 1) * sinv
        k = (k * cosv + pltpu.roll(k, HEAD_W // 2, 1) * sinv) * k_scale
        intra = intra_ref[h]
        qd = qd_ref[h]
        kd = kd_ref[h]
        g = rng_ref[:, lo:lo + HEAD_W]
        for c in range(n_chunks):
            r0 = c * CHUNK
            qc = q[r0:r0 + CHUNK]
            kc = k[r0:r0 + CHUNK]
            vc = proj_ref[r0:r0 + CHUNK, OFF_RV + lo:OFF_RV + lo + HEAD_W]
            gate = proj_ref[r0:r0 + CHUNK, OFF_RG + lo:OFF_RG + lo + HEAD_W]
            state = rstate_ref[h]
            scores = _dot_nt(qc, kc) * intra
            o = _dot(scores, vc) + _dot(qc, state) * qd
            rstate_ref[h] = chunk_decay[h] * state + _dot_tn(kc * kd, vc)
            y_ref[r0:r0 + CHUNK, lo:lo + HEAD_W] = _silu(gate) * _group_norm(o, g)

    mx = proj_ref[:, OFF_MX:OFF_MX + D_GROUP]
    xp = jnp.concatenate([tail_ref[...], mx], axis=0)
    base = CONV_TAIL - (CONV_W - 1)
    conv = cb_ref[...] + xp[base:base + L_BLOCK] * cw_ref[0:1, :]
    for t in range(1, CONV_W):
        conv = conv + xp[base + t:base + t + L_BLOCK] * cw_ref[t:t + 1, :]
    tail_ref[...] = mx[L_BLOCK - CONV_TAIL:L_BLOCK]
    cact = _silu(conv)

    gi = proj_ref[:, OFF_GI:OFF_GI + LANES] + bi_ref[...]
    gf = proj_ref[:, OFF_GF:OFF_GF + LANES] + bf_ref[...]
    logf = jnp.minimum(gf, 0.0) - jnp.log1p(jnp.exp(-jnp.abs(gf)))
    tri = tri_ref[...]
    lane = lax.broadcasted_iota(jnp.int32, (CHUNK, HEAD_W), 1)
    ones_col = (lane == 0).astype(F32)
    m_scale = MLSTM_D ** -0.5

    mq = []
    mk = []
    for h in range(MLSTM_HEADS):
        lo = h * HEAD_W
        ch = cact[:, lo:lo + HEAD_W]
        mq.append(_dot(ch, wq_ref[h]))
        mk.append(_dot(ch, wk_ref[h]) * m_scale)

    for c in range(n_chunks):
        r0 = c * CHUNK
        fcum = jnp.dot(tri, logf[r0:r0 + CHUNK], preferred_element_type=F32,
                       precision=lax.Precision.HIGHEST)
        a = fcum[CHUNK - 1:CHUNK]
        logw = a - fcum + gi[r0:r0 + CHUNK]
        m_in = jnp.max(logw, axis=0, keepdims=True)
        m_old = m_ref[...]
        m_new = jnp.maximum(a + m_old, m_in)
        decay = jnp.exp(a + m_old - m_new)
        w = jnp.exp(logw - m_new)
        em = jnp.exp(-m_new)
        m_ref[...] = m_new
        for h in range(MLSTM_HEADS):
            lo = h * HEAD_W
            qh = mq[h][r0:r0 + CHUNK]
            kh = mk[h][r0:r0 + CHUNK]
            vh = proj_ref[r0:r0 + CHUNK, OFF_MV + lo:OFF_MV + lo + HEAD_W]
            oh = proj_ref[r0:r0 + CHUNK, OFF_MO + lo:OFF_MO + lo + HEAD_W]
            ch = cact[r0:r0 + CHUNK, lo:lo + HEAD_W]
            v_ext = jnp.concatenate([vh, ones_col], axis=1)
            cmem = decay[:, h:h + 1] * cstate_ref[h] + _dot_tn(kh * w[:, h:h + 1], v_ext)
            cstate_ref[h] = cmem
            num = _dot(qh, cmem)
            den = jnp.maximum(jnp.abs(num[:, HEAD_W:HEAD_W + 1]), em[:, h:h + 1])
            hh = num[:, :HEAD_W] / den
            hm = _group_norm(hh, mng_ref[:, lo:lo + HEAD_W])
            ym = _sigmoid(oh) * (hm + skip_ref[:, lo:lo + HEAD_W] * ch)
            y_ref[r0:r0 + CHUNK, D_GROUP + lo:D_GROUP + lo + HEAD_W] = ym


def _retention_tables():
    h = np.arange(RET_HEADS, dtype=np.float64)
    log_g = np.log(1.0 - 2.0 ** (-5.0 - h))
    l = np.arange(CHUNK, dtype=np.float64)
    intra = np.exp(log_g[:, None, None] * np.abs(l[:, None] - l[None, :]))
    qd = np.exp(log_g[:, None] * (l + 1.0))[:, :, None] * np.ones((1, 1, HEAD_W))
    kd = np.exp(log_g[:, None] * (CHUNK - 1.0 - l))[:, :, None] * np.ones((1, 1, HEAD_W))
    cd = np.exp(log_g * CHUNK)
    return (jnp.asarray(intra, F32), jnp.asarray(qd, F32), jnp.asarray(kd, F32),
            tuple(float(v) for v in cd))


def _rotary_tables(s):
    half = HEAD_W // 2
    inv = ROPE_BASE ** (-jnp.arange(half, dtype=F32) / half)
    ang = jnp.arange(s, dtype=jnp.int32).astype(F32)[:, None] * inv[None, :]
    cos = jnp.cos(ang)
    sin = jnp.sin(ang)
    return jnp.concatenate([cos, cos], axis=-1), jnp.concatenate([-sin, sin], axis=-1)


def _mixer(proj, b, s, ret_norm_g, m_conv_w, m_conv_b, m_wq, m_wk, m_b_i, m_b_f,
           m_norm_g, m_skip):
    intra, qd, kd, cd = _retention_tables()
    cos_t, sin_t = _rotary_tables(s)
    tri = jnp.asarray(np.tril(np.ones((CHUNK, CHUNK))), F32)
    nj = s // L_BLOCK
    pad = lambda v: jnp.pad(v.reshape(1, -1), ((0, 0), (0, LANES - v.size)))
    full = lambda shape: pl.BlockSpec(shape, lambda bi, j: (0,) * len(shape))
    return pl.pallas_call(
        functools.partial(_mixer_kernel, cd),
        grid=(b, nj),
        in_specs=[
            pl.BlockSpec((L_BLOCK, D_PROJ_PAD), lambda bi, j: (bi * nj + j, 0)),
            pl.BlockSpec((L_BLOCK, HEAD_W), lambda bi, j: (j, 0)),
            pl.BlockSpec((L_BLOCK, HEAD_W), lambda bi, j: (j, 0)),
            full((RET_HEADS, CHUNK, CHUNK)),
            full((RET_HEADS, CHUNK, HEAD_W)),
            full((RET_HEADS, CHUNK, HEAD_W)),
            full((CHUNK, CHUNK)),
            full((1, D_GROUP)),
            full((CONV_W, D_GROUP)),
            full((1, D_GROUP)),
            full((MLSTM_HEADS, MLSTM_D, MLSTM_D)),
            full((MLSTM_HEADS, MLSTM_D, MLSTM_D)),
            full((1, LANES)),
            full((1, LANES)),
            full((1, D_GROUP)),
            full((1, D_GROUP)),
        ],
        out_specs=pl.BlockSpec((L_BLOCK, 2 * D_GROUP), lambda bi, j: (bi * nj + j, 0)),
        out_shape=jax.ShapeDtypeStruct((b * s, 2 * D_GROUP), F32),
        scratch_shapes=[
            pltpu.VMEM((RET_HEADS, RET_DK, HEAD_W), F32),
            pltpu.VMEM((MLSTM_HEADS, MLSTM_D, 2 * HEAD_W), F32),
            pltpu.VMEM((1, LANES), F32),
            pltpu.VMEM((CONV_TAIL, D_GROUP), F32),
        ],
        compiler_params=pltpu.CompilerParams(
            dimension_semantics=("arbitrary", "arbitrary"), vmem_limit_bytes=VMEM_LIMIT),
        name="retention_mlstm_mixer",
    )(proj, cos_t, sin_t, intra, qd, kd, tri,
      ret_norm_g.reshape(1, D_GROUP), m_conv_w, m_conv_b.reshape(1, D_GROUP),
      m_wq.astype(BF16), m_wk.astype(BF16), pad(m_b_i), pad(m_b_f),
      m_norm_g.reshape(1, D_GROUP), m_skip.reshape(1, D_GROUP))


def _post_kernel(y_ref, x_ref, wout_ref, g2_ref, wq_ref, keys_ref,
                 x1_ref, xn_ref, sub_ref):
    x1 = x_ref[...] + jnp.dot(y_ref[...].astype(BF16), wout_ref[...],
                              preferred_element_type=F32)
    x1_ref[...] = x1
    xn = _rms(x1, g2_ref[...])
    xn_ref[...] = xn
    q = jnp.dot(xn.astype(BF16), wq_ref[...], preferred_element_type=F32)
    for i in range(2 * PEER_HEADS):
        lo = i * PEER_HALF
        sub_ref[lo:lo + PEER_NKEYS, :] = _dot_nt(keys_ref[i], q[:, lo:lo + PEER_HALF])


def _post(y, x2, w_out, norm2_g, peer_wq, keys):
    t, d = x2.shape
    nq = peer_wq.shape[1]
    row = lambda w: pl.BlockSpec((TM_PROJ, w), lambda i: (i, 0))
    full = lambda shape: pl.BlockSpec(shape, lambda i: (0,) * len(shape))
    return pl.pallas_call(
        _post_kernel,
        grid=(t // TM_PROJ,),
        in_specs=[row(d), row(d), full((d, d)), full((1, d)), full((d, nq)),
                  full(keys.shape)],
        out_specs=[row(d), row(d), pl.BlockSpec((nq, TM_PROJ), lambda i: (0, i))],
        out_shape=[jax.ShapeDtypeStruct((t, d), F32), jax.ShapeDtypeStruct((t, d), F32),
                   jax.ShapeDtypeStruct((nq, t), F32)],
        compiler_params=pltpu.CompilerParams(
            dimension_semantics=("arbitrary",), vmem_limit_bytes=VMEM_LIMIT),
        name="out_proj_peer_scores",
    )(y, x2, w_out, norm2_g, peer_wq, keys)


def _final_kernel(x1_ref, p_ref, g_ref, o_ref):
    o_ref[...] = _rms(x1_ref[...] + p_ref[...], g_ref[...])


def _final(x1, p, g):
    t, d = x1.shape
    row = pl.BlockSpec((TM_FINAL, d), lambda i: (i, 0))
    return pl.pallas_call(
        _final_kernel,
        grid=(t // TM_FINAL,),
        in_specs=[row, row, pl.BlockSpec((1, d), lambda i: (0, 0))],
        out_specs=row,
        out_shape=jax.ShapeDtypeStruct((t, d), F32),
        compiler_params=pltpu.CompilerParams(dimension_semantics=("arbitrary",)),
        name="final_rmsnorm",
    )(x1, p, g)


TM_ROUTE = 128
CAND_ROWS = 56


def _candidate_tables():
    pairs = [(k1, k2) for k1 in range(PEER_TOPK) for k2 in range(PEER_TOPK)
             if (k1 + 1) * (k2 + 1) <= PEER_TOPK]
    assert len(pairs) <= CAND_ROWS
    rep = np.zeros((CAND_ROWS, 2 * PEER_TOPK), np.float32)
    til = np.zeros((CAND_ROWS, 2 * PEER_TOPK), np.float32)
    for pos, (k1, k2) in enumerate(pairs):
        rep[pos, k1] = 1.0
        til[pos, PEER_TOPK + k2] = 1.0
    return jnp.asarray(rep), jnp.asarray(til), len(pairs)


def _dot_exact(a, b):
    return jnp.dot(a, b, preferred_element_type=F32, precision=lax.Precision.HIGHEST)


def _route_kernel(n_cand, sub_ref, rep_ref, til_ref, seg_ref, exp_ref, gate_ref,
                  s_ref, i_ref, best_ref, eid_ref, m0_ref):
    tm = sub_ref.shape[1]
    neg = -jnp.inf
    key_id = lax.broadcasted_iota(jnp.int32, (PEER_NKEYS, tm), 0).astype(F32)
    cand_id = lax.broadcasted_iota(jnp.int32, (CAND_ROWS, tm), 0).astype(F32)
    rep = rep_ref[...]
    til = til_ref[...]
    for h in range(PEER_HEADS):
        for p in range(2):
            lo = (2 * h + p) * PEER_NKEYS
            x = sub_ref[lo:lo + PEER_NKEYS, :]
            for k in range(PEER_TOPK):
                m = jnp.max(x, axis=0, keepdims=True)
                idx = jnp.min(jnp.where(x == m, key_id, float(PEER_NKEYS)), axis=0, keepdims=True)
                x = jnp.where(key_id == idx, neg, x)
                r = p * PEER_TOPK + k
                s_ref[r:r + 1, :] = m
                i_ref[r:r + 1, :] = idx
        s = s_ref[...]
        iv = i_ref[...]
        cand = _dot_exact(rep, s) + _dot_exact(til, s)
        cand = jnp.where(cand_id < float(n_cand), cand, neg)
        eid = _dot(rep, iv) * float(PEER_NKEYS) + _dot(til, iv)
        for k in range(PEER_TOPK):
            m = jnp.max(cand, axis=0, keepdims=True)
            pos = jnp.min(jnp.where(cand == m, cand_id, float(CAND_ROWS)), axis=0, keepdims=True)
            sel = cand_id == pos
            e = jnp.max(jnp.where(sel, eid, -1.0), axis=0, keepdims=True)
            cand = jnp.where(sel, neg, cand)
            r = h * PEER_TOPK + k
            best_ref[r:r + 1, :] = m
            eid_ref[r:r + 1, :] = e
            if k == 0:
                m0_ref[r:r + PEER_TOPK, :] = jnp.broadcast_to(m, (PEER_TOPK, tm))
    pexp = jnp.exp(best_ref[...] - m0_ref[...])
    gates = pexp / _dot_exact(seg_ref[...], pexp)
    gate_ref[...] = gates.T
    exp_ref[...] = eid_ref[...].T.astype(jnp.int32)


def _peer_route(sub_t):
    nq, t = sub_t.shape
    rep, til, n_cand = _candidate_tables()
    e = PEER_HEADS * PEER_TOPK
    seg = jnp.asarray(np.kron(np.eye(PEER_HEADS), np.ones((PEER_TOPK, PEER_TOPK))), F32)
    full = lambda a: pl.BlockSpec(a.shape, lambda i: (0,) * a.ndim)
    out = pl.BlockSpec((TM_ROUTE, e), lambda i: (i, 0))
    return pl.pallas_call(
        functools.partial(_route_kernel, n_cand),
        grid=(t // TM_ROUTE,),
        in_specs=[pl.BlockSpec((nq, TM_ROUTE), lambda i: (0, i)), full(rep), full(til), full(seg)],
        out_specs=[out, out],
        out_shape=[jax.ShapeDtypeStruct((t, e), jnp.int32), jax.ShapeDtypeStruct((t, e), F32)],
        scratch_shapes=[
            pltpu.VMEM((2 * PEER_TOPK, TM_ROUTE), F32),
            pltpu.VMEM((2 * PEER_TOPK, TM_ROUTE), F32),
            pltpu.VMEM((e, TM_ROUTE), F32),
            pltpu.VMEM((e, TM_ROUTE), F32),
            pltpu.VMEM((e, TM_ROUTE), F32),
        ],
        compiler_params=pltpu.CompilerParams(dimension_semantics=("arbitrary",)),
        name="peer_topk_route",
    )(sub_t, rep, til, seg)


SC_CORES = 2
SC_SUBCORES = 16
SC_LANES = 16
SC_WORKERS = SC_CORES * SC_SUBCORES
PEER_E = PEER_HEADS * PEER_TOPK
SC_ROWS = 32
SC_GATHERS = PEER_E // SC_ROWS
SC_TOK_BLOCK = 8
SC_QUARTER = 256


def _sc_mesh():
    return plsc.VectorSubcoreMesh(core_axis_name="c", subcore_axis_name="s")


def _sc_pipeline(idx_v, tab_hbm, rows_v, sems, compute):
    def gather(step, buf):
        return pltpu.make_async_copy(tab_hbm.at[idx_v.at[step]], rows_v.at[buf], sems.at[buf])

    gather(0, 0).start()

    @pl.loop(0, SC_TOK_BLOCK)
    def _(i):
        for c in range(SC_GATHERS):
            buf = c % 2
            step = i * SC_GATHERS + c
            gather(step, buf).wait()
            if c + 1 < SC_GATHERS:
                gather(step + 1, 1 - buf).start()
            else:
                @pl.when(i + 1 < SC_TOK_BLOCK)
                def _():
                    gather(step + 1, 1 - buf).start()
            compute(i, c, step, buf)


def _peer_down_dots(xn, idx, down):
    t, d = xn.shape
    tok_w = t // SC_WORKERS
    nblk = tok_w // SC_TOK_BLOCK
    lanes = SC_LANES

    @functools.partial(
        pl.kernel, mesh=_sc_mesh(),
        out_type=jax.ShapeDtypeStruct((t, PEER_E), F32),
        scratch_types=[
            pltpu.VMEM((SC_TOK_BLOCK * SC_GATHERS, SC_ROWS), jnp.int32),
            pltpu.VMEM((SC_TOK_BLOCK, d), F32),
            pltpu.VMEM((2, SC_ROWS, d), F32),
            pltpu.VMEM((SC_ROWS, lanes), F32),
            pltpu.VMEM((SC_TOK_BLOCK, PEER_E), F32),
            pltpu.SemaphoreType.DMA((2,)),
        ],
        compiler_params=pltpu.CompilerParams(needs_layout_passes=False),
        name="peer_down_dots",
    )
    def k(x_hbm, idx_hbm, tab_hbm, out_hbm, idx_v, x_v, rows_v, acc_v, out_v, sems):
        wid = lax.axis_index("s") * SC_CORES + lax.axis_index("c")
        lane = lax.iota(jnp.int32, lanes)

        @pl.loop(0, nblk)
        def _(blk):
            tok0 = wid * tok_w + blk * SC_TOK_BLOCK
            pltpu.sync_copy(idx_hbm.at[pl.ds(tok0 * SC_GATHERS, SC_TOK_BLOCK * SC_GATHERS)], idx_v)
            pltpu.sync_copy(x_hbm.at[pl.ds(tok0, SC_TOK_BLOCK)], x_v)

            def compute(i, c, step, buf):
                def body(dc, accs):
                    xv = x_v[i, pl.ds(dc * lanes, lanes)]
                    return tuple(a + rows_v[buf, r, pl.ds(dc * lanes, lanes)] * xv
                                 for r, a in enumerate(accs))
                accs = lax.fori_loop(0, d // lanes, body,
                                     tuple(jnp.zeros((lanes,), F32) for _ in range(SC_ROWS)))
                for r in range(SC_ROWS):
                    acc_v[r, :] = accs[r]
                for g in range(SC_ROWS // lanes):
                    tot = jnp.zeros((lanes,), F32)
                    for l in range(lanes):
                        tot = tot + plsc.load_gather(
                            acc_v, [lane + g * lanes, jnp.full((lanes,), l, jnp.int32)])
                    out_v[i, pl.ds(c * SC_ROWS + g * lanes, lanes)] = tot

            _sc_pipeline(idx_v, tab_hbm, rows_v, sems, compute)
            pltpu.sync_copy(out_v, out_hbm.at[pl.ds(tok0, SC_TOK_BLOCK)])

    return k(xn, idx.reshape(t * SC_GATHERS, SC_ROWS), down)


def _peer_up_sum(coef, idx, up):
    t = coef.shape[0]
    d = up.shape[1]
    tok_w = t // SC_WORKERS
    nblk = tok_w // SC_TOK_BLOCK
    lanes = SC_LANES
    nj = SC_QUARTER // lanes

    @functools.partial(
        pl.kernel, mesh=_sc_mesh(),
        out_type=jax.ShapeDtypeStruct((t, d), F32),
        scratch_types=[
            pltpu.VMEM((SC_TOK_BLOCK * SC_GATHERS, SC_ROWS), jnp.int32),
            pltpu.VMEM((SC_TOK_BLOCK * PEER_E,), F32),
            pltpu.VMEM((2, SC_ROWS, d), F32),
            pltpu.VMEM((SC_TOK_BLOCK, d), F32),
            pltpu.SemaphoreType.DMA((2,)),
        ],
        compiler_params=pltpu.CompilerParams(needs_layout_passes=False),
        name="peer_up_sum",
    )
    def k(coef_hbm, idx_hbm, tab_hbm, out_hbm, idx_v, coef_v, rows_v, out_v, sems):
        wid = lax.axis_index("s") * SC_CORES + lax.axis_index("c")

        @pl.loop(0, nblk)
        def _(blk):
            tok0 = wid * tok_w + blk * SC_TOK_BLOCK
            pltpu.sync_copy(idx_hbm.at[pl.ds(tok0 * SC_GATHERS, SC_TOK_BLOCK * SC_GATHERS)], idx_v)
            pltpu.sync_copy(coef_hbm.at[pl.ds(tok0 * PEER_E, SC_TOK_BLOCK * PEER_E)], coef_v)

            def compute(i, c, step, buf):
                for q in range(d // SC_QUARTER):
                    col = lambda j: pl.ds(q * SC_QUARTER + j * lanes, lanes)
                    if c == 0:
                        init = tuple(jnp.zeros((lanes,), F32) for _ in range(nj))
                    else:
                        init = tuple(out_v[i, col(j)] for j in range(nj))

                    def body(r, accs):
                        cf = plsc.load_gather(
                            coef_v, [jnp.full((lanes,), step * SC_ROWS + r, jnp.int32)])
                        return tuple(a + rows_v[buf, r, col(j)] * cf for j, a in enumerate(accs))
                    accs = lax.fori_loop(0, SC_ROWS, body, init)
                    for j in range(nj):
                        out_v[i, col(j)] = accs[j]

            _sc_pipeline(idx_v, tab_hbm, rows_v, sems, compute)
            pltpu.sync_copy(out_v, out_hbm.at[pl.ds(tok0, SC_TOK_BLOCK)])

    return k(coef.reshape(t * PEER_E), idx.reshape(t * SC_GATHERS, SC_ROWS), up)


def _coef_kernel(pre_ref, gate_ref, o_ref):
    x = pre_ref[...]
    gelu = 0.5 * x * (1.0 + lax.erf(x * (2.0 ** -0.5)))
    o_ref[...] = gate_ref[...] * gelu


def _peer_coef(pre, gates):
    t, e = pre.shape
    row = pl.BlockSpec((TM_FINAL, e), lambda i: (i, 0))
    return pl.pallas_call(
        _coef_kernel,
        grid=(t // TM_FINAL,),
        in_specs=[row, row],
        out_specs=row,
        out_shape=jax.ShapeDtypeStruct((t, e), F32),
        compiler_params=pltpu.CompilerParams(dimension_semantics=("arbitrary",)),
        name="peer_gate_gelu",
    )(pre, gates)


def _peer_experts(xn, expert, gates, down, up):
    pre = _peer_down_dots(xn, expert, down)
    coef = _peer_coef(pre, gates)
    return _peer_up_sum(coef, expert, up)


def kernel(x, norm1_g, w_in, ret_norm_g, m_conv_w, m_conv_b, m_wq, m_wk, m_b_i, m_b_f,
           m_norm_g, m_skip, w_out, norm2_g, peer_wq, peer_keys, peer_down, peer_up, final_g):
    b, s, d = x.shape
    t = b * s
    x2 = x.reshape(t, d)
    depth = norm1_g.shape[0]
    for i in range(depth):
        if i > 0:
            x2 = x1 + p
        wi = w_in[i]
        n_main = OFF_GI
        gate_pad = lambda w: jnp.pad(w, ((0, 0), (0, LANES - w.shape[1])))
        w_pad = jnp.concatenate(
            [wi[:, :n_main], gate_pad(wi[:, n_main:n_main + MLSTM_HEADS]),
             gate_pad(wi[:, n_main + MLSTM_HEADS:])], axis=1).astype(BF16)
        proj = _rms_proj(x2, norm1_g[i].reshape(1, d), w_pad)
        y = _mixer(proj, b, s, ret_norm_g[i], m_conv_w[i], m_conv_b[i], m_wq[i], m_wk[i],
                   m_b_i[i], m_b_f[i], m_norm_g[i], m_skip[i])
        keys = peer_keys[i].reshape(2 * PEER_HEADS, PEER_NKEYS, PEER_HALF).astype(BF16)
        x1, xn, sub_t = _post(y, x2, w_out[i].astype(BF16), norm2_g[i].reshape(1, d),
                              peer_wq[i].astype(BF16), keys)
        expert, gates = _peer_route(sub_t)
        p = _peer_experts(xn, expert, gates, peer_down[i], peer_up[i])
    out = _final(x1, p, final_g.reshape(1, d))
    return out.reshape(b, s, d)
```

```python
import functools

import numpy as np
import jax
import jax.numpy as jnp
from jax import lax
from jax.experimental import pallas as pl
from jax.experimental.pallas import tpu as pltpu
from jax.experimental.pallas import tpu_sc as plsc

F32 = jnp.float32
BF16 = jnp.bfloat16

D_MODEL = 1024
CHUNK = 64
RET_HEADS = 4
RET_DK = 128
MLSTM_HEADS = 4
MLSTM_D = 128
HEAD_W = 128
D_GROUP = 512
CONV_W = 4
ROPE_BASE = 10000.0
PEER_HEADS = 8
PEER_NKEYS = 128
PEER_TOPK = 16
PEER_HALF = 128
PEER_BLOCK = 128
EPS = 1e-6

OFF_RQ, OFF_RK, OFF_RV, OFF_RG = 0, 512, 1024, 1536
OFF_MX, OFF_MV, OFF_MO = 2048, 2560, 3072
OFF_GI, OFF_GF = 3584, 3712
D_PROJ_PAD = 3840

LANES = 128
CONV_TAIL = 8
VMEM_LIMIT = 56 * 1024 * 1024

TM_PROJ = 256
L_BLOCK = 256
TM_FINAL = 1024


def _rms(x, g):
    ms = jnp.mean(x * x, axis=-1, keepdims=True)
    return x * lax.rsqrt(ms + EPS) * g


def _group_norm(h, g):
    mu = jnp.mean(h, axis=-1, keepdims=True)
    d = h - mu
    var = jnp.mean(d * d, axis=-1, keepdims=True)
    return d * lax.rsqrt(var + EPS) * g


def _silu(x):
    return x * (1.0 / (1.0 + jnp.exp(-x)))


def _sigmoid(x):
    return 1.0 / (1.0 + jnp.exp(-x))


def _dot(a, b):
    return jnp.dot(a.astype(BF16), b.astype(BF16), preferred_element_type=F32)


def _dot_tn(a, b):
    return lax.dot_general(a.astype(BF16), b.astype(BF16), (((0,), (0,)), ((), ())),
                           preferred_element_type=F32)


def _dot_nt(a, b):
    return lax.dot_general(a.astype(BF16), b.astype(BF16), (((1,), (1,)), ((), ())),
                           preferred_element_type=F32)


def _rms_proj_kernel(x_ref, g_ref, w_ref, o_ref):
    h = _rms(x_ref[...], g_ref[...])
    o_ref[...] = jnp.dot(h.astype(BF16), w_ref[...], preferred_element_type=F32)


def _rms_proj(x2, g, w):
    t, d = x2.shape
    n = w.shape[1]
    return pl.pallas_call(
        _rms_proj_kernel,
        grid=(t // TM_PROJ,),
        in_specs=[
            pl.BlockSpec((TM_PROJ, d), lambda i: (i, 0)),
            pl.BlockSpec((1, d), lambda i: (0, 0)),
            pl.BlockSpec((d, n), lambda i: (0, 0)),
        ],
        out_specs=pl.BlockSpec((TM_PROJ, n), lambda i: (i, 0)),
        out_shape=jax.ShapeDtypeStruct((t, n), F32),
        compiler_params=pltpu.CompilerParams(
            dimension_semantics=("arbitrary",), vmem_limit_bytes=VMEM_LIMIT),
        name="rms_in_proj",
    )(x2, g, w)


def _mixer_kernel(chunk_decay,
                  proj_ref, cos_ref, sin_ref, intra_ref, qd_ref, kd_ref, tri_ref,
                  rng_ref, cw_ref, cb_ref, wq_ref, wk_ref, bi_ref, bf_ref,
                  mng_ref, skip_ref,
                  y_ref,
                  rstate_ref, cstate_ref, m_ref, tail_ref):
    j = pl.program_id(1)
    n_chunks = L_BLOCK // CHUNK

    @pl.when(j == 0)
    def _():
        rstate_ref[...] = jnp.zeros_like(rstate_ref)
        cstate_ref[...] = jnp.zeros_like(cstate_ref)
        m_ref[...] = jnp.zeros_like(m_ref)
        tail_ref[...] = jnp.zeros_like(tail_ref)

    cosv = cos_ref[...]
    sinv = sin_ref[...]
    k_scale = RET_DK ** -0.5

    for h in range(RET_HEADS):
        lo = h * HEAD_W
        q = proj_ref[:, OFF_RQ + lo:OFF_RQ + lo + HEAD_W]
        k = proj_ref[:, OFF_RK + lo:OFF_RK + lo + HEAD_W]
        q = q * cosv + pltpu.roll(q, HEAD_W // 2, 1) * sinv
        k = (k * cosv + pltpu.roll(k, HEAD_W // 2, 1) * sinv) * k_scale
        intra = intra_ref[h]
        qd = qd_ref[h]
        kd = kd_ref[h]
        g = rng_ref[:, lo:lo + HEAD_W]
        for c in range(n_chunks):
            r0 = c * CHUNK
            qc = q[r0:r0 + CHUNK]
            kc = k[r0:r0 + CHUNK]
            vc = proj_ref[r0:r0 + CHUNK, OFF_RV + lo:OFF_RV + lo + HEAD_W]
            gate = proj_ref[r0:r0 + CHUNK, OFF_RG + lo:OFF_RG + lo + HEAD_W]
            state = rstate_ref[h]
            scores = _dot_nt(qc, kc) * intra
            o = _dot(scores, vc) + _dot(qc, state) * qd
            rstate_ref[h] = chunk_decay[h] * state + _dot_tn(kc * kd, vc)
            y_ref[r0:r0 + CHUNK, lo:lo + HEAD_W] = _silu(gate) * _group_norm(o, g)

    mx = proj_ref[:, OFF_MX:OFF_MX + D_GROUP]
    xp = jnp.concatenate([tail_ref[...], mx], axis=0)
    base = CONV_TAIL - (CONV_W - 1)
    conv = cb_ref[...] + xp[base:base + L_BLOCK] * cw_ref[0:1, :]
    for t in range(1, CONV_W):
        conv = conv + xp[base + t:base + t + L_BLOCK] * cw_ref[t:t + 1, :]
    tail_ref[...] = mx[L_BLOCK - CONV_TAIL:L_BLOCK]
    cact = _silu(conv)

    gi = proj_ref[:, OFF_GI:OFF_GI + LANES] + bi_ref[...]
    gf = proj_ref[:, OFF_GF:OFF_GF + LANES] + bf_ref[...]
    logf = jnp.minimum(gf, 0.0) - jnp.log1p(jnp.exp(-jnp.abs(gf)))
    tri = tri_ref[...]
    lane = lax.broadcasted_iota(jnp.int32, (CHUNK, HEAD_W), 1)
    ones_col = (lane == 0).astype(F32)
    m_scale = MLSTM_D ** -0.5

    mq = []
    mk = []
    for h in range(MLSTM_HEADS):
        lo = h * HEAD_W
        ch = cact[:, lo:lo + HEAD_W]
        mq.append(_dot(ch, wq_ref[h]))
        mk.append(_dot(ch, wk_ref[h]) * m_scale)

    for c in range(n_chunks):
        r0 = c * CHUNK
        fcum = jnp.dot(tri, logf[r0:r0 + CHUNK], preferred_element_type=F32,
                       precision=lax.Precision.HIGHEST)
        a = fcum[CHUNK - 1:CHUNK]
        logw = a - fcum + gi[r0:r0 + CHUNK]
        m_in = jnp.max(logw, axis=0, keepdims=True)
        m_old = m_ref[...]
        m_new = jnp.maximum(a + m_old, m_in)
        decay = jnp.exp(a + m_old - m_new)
        w = jnp.exp(logw - m_new)
        em = jnp.exp(-m_new)
        m_ref[...] = m_new
        for h in range(MLSTM_HEADS):
            lo = h * HEAD_W
            qh = mq[h][r0:r0 + CHUNK]
            kh = mk[h][r0:r0 + CHUNK]
            vh = proj_ref[r0:r0 + CHUNK, OFF_MV + lo:OFF_MV + lo + HEAD_W]
            oh = proj_ref[r0:r0 + CHUNK, OFF_MO + lo:OFF_MO + lo + HEAD_W]
            ch = cact[r0:r0 + CHUNK, lo:lo + HEAD_W]
            v_ext = jnp.concatenate([vh, ones_col], axis=1)
            cmem = decay[:, h:h + 1] * cstate_ref[h] + _dot_tn(kh * w[:, h:h + 1], v_ext)
            cstate_ref[h] = cmem
            num = _dot(qh, cmem)
            den = jnp.maximum(jnp.abs(num[:, HEAD_W:HEAD_W + 1]), em[:, h:h + 1])
            hh = num[:, :HEAD_W] / den
            hm = _group_norm(hh, mng_ref[:, lo:lo + HEAD_W])
            ym = _sigmoid(oh) * (hm + skip_ref[:, lo:lo + HEAD_W] * ch)
            y_ref[r0:r0 + CHUNK, D_GROUP + lo:D_GROUP + lo + HEAD_W] = ym


def _retention_tables():
    h = np.arange(RET_HEADS, dtype=np.float64)
    log_g = np.log(1.0 - 2.0 ** (-5.0 - h))
    l = np.arange(CHUNK, dtype=np.float64)
    intra = np.exp(log_g[:, None, None] * np.abs(l[:, None] - l[None, :]))
    qd = np.exp(log_g[:, None] * (l + 1.0))[:, :, None] * np.ones((1, 1, HEAD_W))
    kd = np.exp(log_g[:, None] * (CHUNK - 1.0 - l))[:, :, None] * np.ones((1, 1, HEAD_W))
    cd = np.exp(log_g * CHUNK)
    return (jnp.asarray(intra, F32), jnp.asarray(qd, F32), jnp.asarray(kd, F32),
            tuple(float(v) for v in cd))


def _rotary_tables(s):
    half = HEAD_W // 2
    inv = ROPE_BASE ** (-jnp.arange(half, dtype=F32) / half)
    ang = jnp.arange(s, dtype=jnp.int32).astype(F32)[:, None] * inv[None, :]
    cos = jnp.cos(ang)
    sin = jnp.sin(ang)
    return jnp.concatenate([cos, cos], axis=-1), jnp.concatenate([-sin, sin], axis=-1)


def _mixer(proj, b, s, ret_norm_g, m_conv_w, m_conv_b, m_wq, m_wk, m_b_i, m_b_f,
           m_norm_g, m_skip):
    intra, qd, kd, cd = _retention_tables()
    cos_t, sin_t = _rotary_tables(s)
    tri = jnp.asarray(np.tril(np.ones((CHUNK, CHUNK))), F32)
    nj = s // L_BLOCK
    pad = lambda v: jnp.pad(v.reshape(1, -1), ((0, 0), (0, LANES - v.size)))
    full = lambda shape: pl.BlockSpec(shape, lambda bi, j: (0,) * len(shape))
    return pl.pallas_call(
        functools.partial(_mixer_kernel, cd),
        grid=(b, nj),
        in_specs=[
            pl.BlockSpec((L_BLOCK, D_PROJ_PAD), lambda bi, j: (bi * nj + j, 0)),
            pl.BlockSpec((L_BLOCK, HEAD_W), lambda bi, j: (j, 0)),
            pl.BlockSpec((L_BLOCK, HEAD_W), lambda bi, j: (j, 0)),
            full((RET_HEADS, CHUNK, CHUNK)),
            full((RET_HEADS, CHUNK, HEAD_W)),
            full((RET_HEADS, CHUNK, HEAD_W)),
            full((CHUNK, CHUNK)),
            full((1, D_GROUP)),
            full((CONV_W, D_GROUP)),
            full((1, D_GROUP)),
            full((MLSTM_HEADS, MLSTM_D, MLSTM_D)),
            full((MLSTM_HEADS, MLSTM_D, MLSTM_D)),
            full((1, LANES)),
            full((1, LANES)),
            full((1, D_GROUP)),
            full((1, D_GROUP)),
        ],
        out_specs=pl.BlockSpec((L_BLOCK, 2 * D_GROUP), lambda bi, j: (bi * nj + j, 0)),
        out_shape=jax.ShapeDtypeStruct((b * s, 2 * D_GROUP), F32),
        scratch_shapes=[
            pltpu.VMEM((RET_HEADS, RET_DK, HEAD_W), F32),
            pltpu.VMEM((MLSTM_HEADS, MLSTM_D, 2 * HEAD_W), F32),
            pltpu.VMEM((1, LANES), F32),
            pltpu.VMEM((CONV_TAIL, D_GROUP), F32),
        ],
        compiler_params=pltpu.CompilerParams(
            dimension_semantics=("arbitrary", "arbitrary"), vmem_limit_bytes=VMEM_LIMIT),
        name="retention_mlstm_mixer",
    )(proj, cos_t, sin_t, intra, qd, kd, tri,
      ret_norm_g.reshape(1, D_GROUP), m_conv_w, m_conv_b.reshape(1, D_GROUP),
      m_wq.astype(BF16), m_wk.astype(BF16), pad(m_b_i), pad(m_b_f),
      m_norm_g.reshape(1, D_GROUP), m_skip.reshape(1, D_GROUP))


def _post_kernel(y_ref, x_ref, wout_ref, g2_ref, wq_ref, keys_ref,
                 x1_ref, xn_ref, sub_ref):
    x1 = x_ref[...] + jnp.dot(y_ref[...].astype(BF16), wout_ref[...],
                              preferred_element_type=F32)
    x1_ref[...] = x1
    xn = _rms(x1, g2_ref[...])
    xn_ref[...] = xn
    q = jnp.dot(xn.astype(BF16), wq_ref[...], preferred_element_type=F32)
    for i in range(2 * PEER_HEADS):
        lo = i * PEER_HALF
        sub_ref[lo:lo + PEER_NKEYS, :] = _dot_nt(keys_ref[i], q[:, lo:lo + PEER_HALF])


def _post(y, x2, w_out, norm2_g, peer_wq, keys):
    t, d = x2.shape
    nq = peer_wq.shape[1]
    row = lambda w: pl.BlockSpec((TM_PROJ, w), lambda i: (i, 0))
    full = lambda shape: pl.BlockSpec(shape, lambda i: (0,) * len(shape))
    return pl.pallas_call(
        _post_kernel,
        grid=(t // TM_PROJ,),
        in_specs=[row(d), row(d), full((d, d)), full((1, d)), full((d, nq)),
                  full(keys.shape)],
        out_specs=[row(d), row(d), pl.BlockSpec((nq, TM_PROJ), lambda i: (0, i))],
        out_shape=[jax.ShapeDtypeStruct((t, d), F32), jax.ShapeDtypeStruct((t, d), F32),
                   jax.ShapeDtypeStruct((nq, t), F32)],
        compiler_params=pltpu.CompilerParams(
            dimension_semantics=("arbitrary",), vmem_limit_bytes=VMEM_LIMIT),
        name="out_proj_peer_scores",
    )(y, x2, w_out, norm2_g, peer_wq, keys)


def _final_kernel(x1_ref, p_ref, g_ref, o_ref):
    o_ref[...] = _rms(x1_ref[...] + p_ref[...], g_ref[...])


def _final(x1, p, g):
    t, d = x1.shape
    row = pl.BlockSpec((TM_FINAL, d), lambda i: (i, 0))
    return pl.pallas_call(
        _final_kernel,
        grid=(t // TM_FINAL,),
        in_specs=[row, row, pl.BlockSpec((1, d), lambda i: (0, 0))],
        out_specs=row,
        out_shape=jax.ShapeDtypeStruct((t, d), F32),
        compiler_params=pltpu.CompilerParams(dimension_semantics=("arbitrary",)),
        name="final_rmsnorm",
    )(x1, p, g)


TM_ROUTE = 128
CAND_ROWS = 56


def _candidate_tables():
    pairs = [(k1, k2) for k1 in range(PEER_TOPK) for k2 in range(PEER_TOPK)
             if (k1 + 1) * (k2 + 1) <= PEER_TOPK]
    assert len(pairs) <= CAND_ROWS
    rep = np.zeros((CAND_ROWS, 2 * PEER_TOPK), np.float32)
    til = np.zeros((CAND_ROWS, 2 * PEER_TOPK), np.float32)
    for pos, (k1, k2) in enumerate(pairs):
        rep[pos, k1] = 1.0
        til[pos, PEER_TOPK + k2] = 1.0
    return jnp.asarray(rep), jnp.asarray(til), len(pairs)


def _dot_exact(a, b):
    return jnp.dot(a, b, preferred_element_type=F32, precision=lax.Precision.HIGHEST)


def _route_kernel(n_cand, sub_ref, rep_ref, til_ref, seg_ref, exp_ref, gate_ref,
                  s_ref, i_ref, best_ref, eid_ref, m0_ref):
    tm = sub_ref.shape[1]
    neg = -jnp.inf
    key_id = lax.broadcasted_iota(jnp.int32, (PEER_NKEYS, tm), 0).astype(F32)
    cand_id = lax.broadcasted_iota(jnp.int32, (CAND_ROWS, tm), 0).astype(F32)
    rep = rep_ref[...]
    til = til_ref[...]
    for h in range(PEER_HEADS):
        for p in range(2):
            lo = (2 * h + p) * PEER_NKEYS
            x = sub_ref[lo:lo + PEER_NKEYS, :]
            for k in range(PEER_TOPK):
                m = jnp.max(x, axis=0, keepdims=True)
                idx = jnp.min(jnp.where(x == m, key_id, float(PEER_NKEYS)), axis=0, keepdims=True)
                x = jnp.where(key_id == idx, neg, x)
                r = p * PEER_TOPK + k
                s_ref[r:r + 1, :] = m
                i_ref[r:r + 1, :] = idx
        s = s_ref[...]
        iv = i_ref[...]
        cand = _dot_exact(rep, s) + _dot_exact(til, s)
        cand = jnp.where(cand_id < float(n_cand), cand, neg)
        eid = _dot(rep, iv) * float(PEER_NKEYS) + _dot(til, iv)
        for k in range(PEER_TOPK):
            m = jnp.max(cand, axis=0, keepdims=True)
            pos = jnp.min(jnp.where(cand == m, cand_id, float(CAND_ROWS)), axis=0, keepdims=True)
            sel = cand_id == pos
            e = jnp.max(jnp.where(sel, eid, -1.0), axis=0, keepdims=True)
            cand = jnp.where(sel, neg, cand)
            r = h * PEER_TOPK + k
            best_ref[r:r + 1, :] = m
            eid_ref[r:r + 1, :] = e
            if k == 0:
                m0_ref[r:r + PEER_TOPK, :] = jnp.broadcast_to(m, (PEER_TOPK, tm))
    pexp = jnp.exp(best_ref[...] - m0_ref[...])
    gates = pexp / _dot_exact(seg_ref[...], pexp)
    gate_ref[...] = gates.T
    exp_ref[...] = eid_ref[...].T.astype(jnp.int32)


def _peer_route(sub_t):
    nq, t = sub_t.shape
    rep, til, n_cand = _candidate_tables()
    e = PEER_HEADS * PEER_TOPK
    seg = jnp.asarray(np.kron(np.eye(PEER_HEADS), np.ones((PEER_TOPK, PEER_TOPK))), F32)
    full = lambda a: pl.BlockSpec(a.shape, lambda i: (0,) * a.ndim)
    out = pl.BlockSpec((TM_ROUTE, e), lambda i: (i, 0))
    return pl.pallas_call(
        functools.partial(_route_kernel, n_cand),
        grid=(t // TM_ROUTE,),
        in_specs=[pl.BlockSpec((nq, TM_ROUTE), lambda i: (0, i)), full(rep), full(til), full(seg)],
        out_specs=[out, out],
        out_shape=[jax.ShapeDtypeStruct((t, e), jnp.int32), jax.ShapeDtypeStruct((t, e), F32)],
        scratch_shapes=[
            pltpu.VMEM((2 * PEER_TOPK, TM_ROUTE), F32),
            pltpu.VMEM((2 * PEER_TOPK, TM_ROUTE), F32),
            pltpu.VMEM((e, TM_ROUTE), F32),
            pltpu.VMEM((e, TM_ROUTE), F32),
            pltpu.VMEM((e, TM_ROUTE), F32),
        ],
        compiler_params=pltpu.CompilerParams(dimension_semantics=("arbitrary",)),
        name="peer_topk_route",
    )(sub_t, rep, til, seg)


SC_CORES = 2
SC_SUBCORES = 16
SC_LANES = 16
SC_WORKERS = SC_CORES * SC_SUBCORES
PEER_E = PEER_HEADS * PEER_TOPK
SC_ROWS = 32
SC_GATHERS = PEER_E // SC_ROWS
SC_TOK_BLOCK = 8
SC_QUARTER = 256


def _sc_mesh():
    return plsc.VectorSubcoreMesh(core_axis_name="c", subcore_axis_name="s")


def _sc_pipeline(idx_v, tab_hbm, rows_v, sems, compute):
    def gather(step, buf):
        return pltpu.make_async_copy(tab_hbm.at[idx_v.at[step]], rows_v.at[buf], sems.at[buf])

    gather(0, 0).start()

    @pl.loop(0, SC_TOK_BLOCK)
    def _(i):
        for c in range(SC_GATHERS):
            buf = c % 2
            step = i * SC_GATHERS + c
            gather(step, buf).wait()
            if c + 1 < SC_GATHERS:
                gather(step + 1, 1 - buf).start()
            else:
                @pl.when(i + 1 < SC_TOK_BLOCK)
                def _():
                    gather(step + 1, 1 - buf).start()
            compute(i, c, step, buf)


def _peer_down_dots(xn, idx, down):
    t, d = xn.shape
    tok_w = t // SC_WORKERS
    nblk = tok_w // SC_TOK_BLOCK
    lanes = SC_LANES

    @functools.partial(
        pl.kernel, mesh=_sc_mesh(),
        out_type=jax.ShapeDtypeStruct((t, PEER_E), F32),
        scratch_types=[
            pltpu.VMEM((SC_TOK_BLOCK * SC_GATHERS, SC_ROWS), jnp.int32),
            pltpu.VMEM((SC_TOK_BLOCK, d), F32),
            pltpu.VMEM((2, SC_ROWS, d), F32),
            pltpu.VMEM((SC_ROWS, lanes), F32),
            pltpu.VMEM((SC_TOK_BLOCK, PEER_E), F32),
            pltpu.SemaphoreType.DMA((2,)),
        ],
        compiler_params=pltpu.CompilerParams(needs_layout_passes=False),
        name="peer_down_dots",
    )
    def k(x_hbm, idx_hbm, tab_hbm, out_hbm, idx_v, x_v, rows_v, acc_v, out_v, sems):
        wid = lax.axis_index("s") * SC_CORES + lax.axis_index("c")
        lane = lax.iota(jnp.int32, lanes)

        @pl.loop(0, nblk)
        def _(blk):
            tok0 = wid * tok_w + blk * SC_TOK_BLOCK
            pltpu.sync_copy(idx_hbm.at[pl.ds(tok0 * SC_GATHERS, SC_TOK_BLOCK * SC_GATHERS)], idx_v)
            pltpu.sync_copy(x_hbm.at[pl.ds(tok0, SC_TOK_BLOCK)], x_v)

            def compute(i, c, step, buf):
                def body(dc, accs):
                    xv = x_v[i, pl.ds(dc * lanes, lanes)]
                    return tuple(a + rows_v[buf, r, pl.ds(dc * lanes, lanes)] * xv
                                 for r, a in enumerate(accs))
                accs = lax.fori_loop(0, d // lanes, body,
                                     tuple(jnp.zeros((lanes,), F32) for _ in range(SC_ROWS)))
                for r in range(SC_ROWS):
                    acc_v[r, :] = accs[r]
                for g in range(SC_ROWS // lanes):
                    tot = jnp.zeros((lanes,), F32)
                    for l in range(lanes):
                        tot = tot + plsc.load_gather(
                            acc_v, [lane + g * lanes, jnp.full((lanes,), l, jnp.int32)])
                    out_v[i, pl.ds(c * SC_ROWS + g * lanes, lanes)] = tot

            _sc_pipeline(idx_v, tab_hbm, rows_v, sems, compute)
            pltpu.sync_copy(out_v, out_hbm.at[pl.ds(tok0, SC_TOK_BLOCK)])

    return k(xn, idx.reshape(t * SC_GATHERS, SC_ROWS), down)


def _peer_up_sum(coef, idx, up, tok_start):
    t_all = coef.shape[0]
    t = t_all - tok_start
    d = up.shape[1]
    tok_w = t // SC_WORKERS
    nblk = tok_w // SC_TOK_BLOCK
    assert nblk * SC_TOK_BLOCK * SC_WORKERS == t
    lanes = SC_LANES
    nj = SC_QUARTER // lanes

    @functools.partial(
        pl.kernel, mesh=_sc_mesh(),
        out_type=jax.ShapeDtypeStruct((t, d), F32),
        scratch_types=[
            pltpu.VMEM((SC_TOK_BLOCK * SC_GATHERS, SC_ROWS), jnp.int32),
            pltpu.VMEM((SC_TOK_BLOCK * PEER_E,), F32),
            pltpu.VMEM((2, SC_ROWS, d), F32),
            pltpu.VMEM((SC_TOK_BLOCK, d), F32),
            pltpu.SemaphoreType.DMA((2,)),
        ],
        compiler_params=pltpu.CompilerParams(needs_layout_passes=False),
        name="peer_up_sum",
    )
    def k(coef_hbm, idx_hbm, tab_hbm, out_hbm, idx_v, coef_v, rows_v, out_v, sems):
        wid = lax.axis_index("s") * SC_CORES + lax.axis_index("c")

        @pl.loop(0, nblk)
        def _(blk):
            out0 = wid * tok_w + blk * SC_TOK_BLOCK
            tok0 = tok_start + out0
            pltpu.sync_copy(idx_hbm.at[pl.ds(tok0 * SC_GATHERS, SC_TOK_BLOCK * SC_GATHERS)], idx_v)
            pltpu.sync_copy(coef_hbm.at[pl.ds(tok0 * PEER_E, SC_TOK_BLOCK * PEER_E)], coef_v)

            def compute(i, c, step, buf):
                for q in range(d // SC_QUARTER):
                    col = lambda j: pl.ds(q * SC_QUARTER + j * lanes, lanes)
                    if c == 0:
                        init = tuple(jnp.zeros((lanes,), F32) for _ in range(nj))
                    else:
                        init = tuple(out_v[i, col(j)] for j in range(nj))

                    def body(r, accs):
                        cf = plsc.load_gather(
                            coef_v, [jnp.full((lanes,), step * SC_ROWS + r, jnp.int32)])
                        return tuple(a + rows_v[buf, r, col(j)] * cf for j, a in enumerate(accs))
                    accs = lax.fori_loop(0, SC_ROWS, body, init)
                    for j in range(nj):
                        out_v[i, col(j)] = accs[j]

            _sc_pipeline(idx_v, tab_hbm, rows_v, sems, compute)
            pltpu.sync_copy(out_v, out_hbm.at[pl.ds(out0, SC_TOK_BLOCK)])

    return k(coef.reshape(t_all * PEER_E), idx.reshape(t_all * SC_GATHERS, SC_ROWS), up)


TC_TOK_BLOCK = 64
TC_UNROLL = 16
ROW_WORDS = 4
PEER_TC_TOKENS = 18432


def _pack_rows_bf16(tab):
    n, d = tab.shape
    bits = lax.bitcast_convert_type(tab.astype(BF16), jnp.uint16).astype(jnp.uint32)
    word = bits[:, :d // 2] | (bits[:, d // 2:] << 16)
    return word.reshape(n * ROW_WORDS, LANES)


def _tc_up_kernel(idx_ref, coef_ref, tab_hbm, o_ref, tab_v, sem):
    @pl.when(pl.program_id(0) == 0)
    def _():
        cp = pltpu.make_async_copy(tab_hbm, tab_v, sem)
        cp.start()
        cp.wait()

    def token(i, carry):
        def body(kb, accs):
            lo, hi = accs
            base = i * PEER_E + kb * TC_UNROLL
            for u in range(TC_UNROLL):
                row0 = idx_ref[base + u]
                c = coef_ref[base + u]
                w = tab_v[pl.ds(pl.multiple_of(row0, ROW_WORDS), ROW_WORDS), :]
                lo = lo + c * lax.bitcast_convert_type(w << 16, F32)
                hi = hi + c * lax.bitcast_convert_type(w & jnp.uint32(0xFFFF0000), F32)
            return lo, hi
        z = jnp.zeros((ROW_WORDS, LANES), F32)
        lo, hi = lax.fori_loop(0, PEER_E // TC_UNROLL, body, (z, z))
        o_ref[i, 0:ROW_WORDS, :] = lo
        o_ref[i, ROW_WORDS:2 * ROW_WORDS, :] = hi
        return carry
    lax.fori_loop(0, TC_TOK_BLOCK, token, 0)


def _peer_up_sum_tc(coef, idx, up, n_tok):
    t_all = coef.shape[0]
    d = up.shape[1]
    tabp = _pack_rows_bf16(up)
    smem = lambda: pl.BlockSpec((TC_TOK_BLOCK * PEER_E,), lambda i: (i,), memory_space=pltpu.SMEM)
    out = pl.pallas_call(
        _tc_up_kernel,
        grid=(n_tok // TC_TOK_BLOCK,),
        in_specs=[smem(), smem(), pl.BlockSpec(memory_space=pl.ANY)],
        out_specs=pl.BlockSpec((TC_TOK_BLOCK, 2 * ROW_WORDS, LANES), lambda i: (i, 0, 0)),
        out_shape=jax.ShapeDtypeStruct((n_tok, 2 * ROW_WORDS, LANES), F32),
        scratch_shapes=[pltpu.VMEM(tabp.shape, jnp.uint32), pltpu.SemaphoreType.DMA],
        compiler_params=pltpu.CompilerParams(
            dimension_semantics=("arbitrary",), vmem_limit_bytes=VMEM_LIMIT),
        name="peer_up_sum_tc",
    )((idx * ROW_WORDS).reshape(t_all * PEER_E), coef.reshape(t_all * PEER_E), tabp)
    return out.reshape(n_tok, d)


def _coef_kernel(pre_ref, gate_ref, o_ref):
    x = pre_ref[...]
    gelu = 0.5 * x * (1.0 + lax.erf(x * (2.0 ** -0.5)))
    o_ref[...] = gate_ref[...] * gelu


def _peer_coef(pre, gates):
    t, e = pre.shape
    row = pl.BlockSpec((TM_FINAL, e), lambda i: (i, 0))
    return pl.pallas_call(
        _coef_kernel,
        grid=(t // TM_FINAL,),
        in_specs=[row, row],
        out_specs=row,
        out_shape=jax.ShapeDtypeStruct((t, e), F32),
        compiler_params=pltpu.CompilerParams(dimension_semantics=("arbitrary",)),
        name="peer_gate_gelu",
    )(pre, gates)


def _peer_experts(xn, expert, gates, down, up):
    pre = _peer_down_dots(xn, expert, down)
    coef = _peer_coef(pre, gates)
    p_sc = _peer_up_sum(coef, expert, up, PEER_TC_TOKENS)
    p_tc = _peer_up_sum_tc(coef, expert, up, PEER_TC_TOKENS)
    return jnp.concatenate([p_tc, p_sc], axis=0)


def kernel(x, norm1_g, w_in, ret_norm_g, m_conv_w, m_conv_b, m_wq, m_wk, m_b_i, m_b_f,
           m_norm_g, m_skip, w_out, norm2_g, peer_wq, peer_keys, peer_down, peer_up, final_g):
    b, s, d = x.shape
    t = b * s
    x2 = x.reshape(t, d)
    depth = norm1_g.shape[0]
    for i in range(depth):
        if i > 0:
            x2 = x1 + p
        wi = w_in[i]
        n_main = OFF_GI
        gate_pad = lambda w: jnp.pad(w, ((0, 0), (0, LANES - w.shape[1])))
        w_pad = jnp.concatenate(
            [wi[:, :n_main], gate_pad(wi[:, n_main:n_main + MLSTM_HEADS]),
             gate_pad(wi[:, n_main + MLSTM_HEADS:])], axis=1).astype(BF16)
        proj = _rms_proj(x2, norm1_g[i].reshape(1, d), w_pad)
        y = _mixer(proj, b, s, ret_norm_g[i], m_conv_w[i], m_conv_b[i], m_wq[i], m_wk[i],
                   m_b_i[i], m_b_f[i], m_norm_g[i], m_skip[i])
        keys = peer_keys[i].reshape(2 * PEER_HEADS, PEER_NKEYS, PEER_HALF).astype(BF16)
        x1, xn, sub_t = _post(y, x2, w_out[i].astype(BF16), norm2_g[i].reshape(1, d),
                              peer_wq[i].astype(BF16), keys)
        expert, gates = _peer_route(sub_t)
        p = _peer_experts(xn, expert, gates, peer_down[i], peer_up[i])
    out = _final(x1, p, final_g.reshape(1, d))
    return out.reshape(b, s, d)
```

```python
import functools

import numpy as np
import jax
import jax.numpy as jnp
from jax import lax
from jax.experimental import pallas as pl
from jax.experimental.pallas import tpu as pltpu
from jax.experimental.pallas import tpu_sc as plsc

F32 = jnp.float32
BF16 = jnp.bfloat16

D_MODEL = 1024
CHUNK = 64
RET_HEADS = 4
RET_DK = 128
MLSTM_HEADS = 4
MLSTM_D = 128
HEAD_W = 128
D_GROUP = 512
CONV_W = 4
ROPE_BASE = 10000.0
PEER_HEADS = 8
PEER_NKEYS = 128
PEER_TOPK = 16
PEER_HALF = 128
PEER_BLOCK = 128
EPS = 1e-6

OFF_RQ, OFF_RK, OFF_RV, OFF_RG = 0, 512, 1024, 1536
OFF_MX, OFF_MV, OFF_MO = 2048, 2560, 3072
OFF_GI, OFF_GF = 3584, 3712
D_PROJ_PAD = 3840

LANES = 128
CONV_TAIL = 8
VMEM_LIMIT = 56 * 1024 * 1024

TM_PROJ = 256
L_BLOCK = 256
TM_FINAL = 1024


def _rms(x, g):
    ms = jnp.mean(x * x, axis=-1, keepdims=True)
    return x * lax.rsqrt(ms + EPS) * g


def _group_norm(h, g):
    mu = jnp.mean(h, axis=-1, keepdims=True)
    d = h - mu
    var = jnp.mean(d * d, axis=-1, keepdims=True)
    return d * lax.rsqrt(var + EPS) * g


def _silu(x):
    return x * (1.0 / (1.0 + jnp.exp(-x)))


def _sigmoid(x):
    return 1.0 / (1.0 + jnp.exp(-x))


def _dot(a, b):
    return jnp.dot(a.astype(BF16), b.astype(BF16), preferred_element_type=F32)


def _dot_tn(a, b):
    return lax.dot_general(a.astype(BF16), b.astype(BF16), (((0,), (0,)), ((), ())),
                           preferred_element_type=F32)


def _dot_nt(a, b):
    return lax.dot_general(a.astype(BF16), b.astype(BF16), (((1,), (1,)), ((), ())),
                           preferred_element_type=F32)


def _rms_proj_kernel(x_ref, g_ref, w_ref, o_ref):
    h = _rms(x_ref[...], g_ref[...])
    o_ref[...] = jnp.dot(h.astype(BF16), w_ref[...], preferred_element_type=F32)


def _rms_proj(x2, g, w):
    t, d = x2.shape
    n = w.shape[1]
    return pl.pallas_call(
        _rms_proj_kernel,
        grid=(t // TM_PROJ,),
        in_specs=[
            pl.BlockSpec((TM_PROJ, d), lambda i: (i, 0)),
            pl.BlockSpec((1, d), lambda i: (0, 0)),
            pl.BlockSpec((d, n), lambda i: (0, 0)),
        ],
        out_specs=pl.BlockSpec((TM_PROJ, n), lambda i: (i, 0)),
        out_shape=jax.ShapeDtypeStruct((t, n), F32),
        compiler_params=pltpu.CompilerParams(
            dimension_semantics=("arbitrary",), vmem_limit_bytes=VMEM_LIMIT),
        name="rms_in_proj",
    )(x2, g, w)


def _mixer_kernel(chunk_decay,
                  proj_ref, cos_ref, sin_ref, intra_ref, qd_ref, kd_ref, tri_ref,
                  rng_ref, cw_ref, cb_ref, wq_ref, wk_ref, bi_ref, bf_ref,
                  mng_ref, skip_ref,
                  y_ref,
                  rstate_ref, cstate_ref, m_ref, tail_ref):
    j = pl.program_id(1)
    n_chunks = L_BLOCK // CHUNK

    @pl.when(j == 0)
    def _():
        rstate_ref[...] = jnp.zeros_like(rstate_ref)
        cstate_ref[...] = jnp.zeros_like(cstate_ref)
        m_ref[...] = jnp.zeros_like(m_ref)
        tail_ref[...] = jnp.zeros_like(tail_ref)

    cosv = cos_ref[...]
    sinv = sin_ref[...]
    k_scale = RET_DK ** -0.5

    for h in range(RET_HEADS):
        lo = h * HEAD_W
        q = proj_ref[:, OFF_RQ + lo:OFF_RQ + lo + HEAD_W]
        k = proj_ref[:, OFF_RK + lo:OFF_RK + lo + HEAD_W]
        q = q * cosv + pltpu.roll(q, HEAD_W // 2, 1) * sinv
        k = (k * cosv + pltpu.roll(k, HEAD_W // 2, 1) * sinv) * k_scale
        intra = intra_ref[h]
        qd = qd_ref[h]
        kd = kd_ref[h]
        g = rng_ref[:, lo:lo + HEAD_W]
        for c in range(n_chunks):
            r0 = c * CHUNK
            qc = q[r0:r0 + CHUNK]
            kc = k[r0:r0 + CHUNK]
            vc = proj_ref[r0:r0 + CHUNK, OFF_RV + lo:OFF_RV + lo + HEAD_W]
            gate = proj_ref[r0:r0 + CHUNK, OFF_RG + lo:OFF_RG + lo + HEAD_W]
            state = rstate_ref[h]
            scores = _dot_nt(qc, kc) * intra
            o = _dot(scores, vc) + _dot(qc, state) * qd
            rstate_ref[h] = chunk_decay[h] * state + _dot_tn(kc * kd, vc)
            y_ref[r0:r0 + CHUNK, lo:lo + HEAD_W] = _silu(gate) * _group_norm(o, g)

    mx = proj_ref[:, OFF_MX:OFF_MX + D_GROUP]
    xp = jnp.concatenate([tail_ref[...], mx], axis=0)
    base = CONV_TAIL - (CONV_W - 1)
    conv = cb_ref[...] + xp[base:base + L_BLOCK] * cw_ref[0:1, :]
    for t in range(1, CONV_W):
        conv = conv + xp[base + t:base + t + L_BLOCK] * cw_ref[t:t + 1, :]
    tail_ref[...] = mx[L_BLOCK - CONV_TAIL:L_BLOCK]
    cact = _silu(conv)

    gi = proj_ref[:, OFF_GI:OFF_GI + LANES] + bi_ref[...]
    gf = proj_ref[:, OFF_GF:OFF_GF + LANES] + bf_ref[...]
    logf = jnp.minimum(gf, 0.0) - jnp.log1p(jnp.exp(-jnp.abs(gf)))
    tri = tri_ref[...]
    lane = lax.broadcasted_iota(jnp.int32, (CHUNK, HEAD_W), 1)
    ones_col = (lane == 0).astype(F32)
    m_scale = MLSTM_D ** -0.5

    mq = []
    mk = []
    for h in range(MLSTM_HEADS):
        lo = h * HEAD_W
        ch = cact[:, lo:lo + HEAD_W]
        mq.append(_dot(ch, wq_ref[h]))
        mk.append(_dot(ch, wk_ref[h]) * m_scale)

    for c in range(n_chunks):
        r0 = c * CHUNK
        fcum = jnp.dot(tri, logf[r0:r0 + CHUNK], preferred_element_type=F32,
                       precision=lax.Precision.HIGHEST)
        a = fcum[CHUNK - 1:CHUNK]
        logw = a - fcum + gi[r0:r0 + CHUNK]
        m_in = jnp.max(logw, axis=0, keepdims=True)
        m_old = m_ref[...]
        m_new = jnp.maximum(a + m_old, m_in)
        decay = jnp.exp(a + m_old - m_new)
        w = jnp.exp(logw - m_new)
        em = jnp.exp(-m_new)
        m_ref[...] = m_new
        for h in range(MLSTM_HEADS):
            lo = h * HEAD_W
            qh = mq[h][r0:r0 + CHUNK]
            kh = mk[h][r0:r0 + CHUNK]
            vh = proj_ref[r0:r0 + CHUNK, OFF_MV + lo:OFF_MV + lo + HEAD_W]
            oh = proj_ref[r0:r0 + CHUNK, OFF_MO + lo:OFF_MO + lo + HEAD_W]
            ch = cact[r0:r0 + CHUNK, lo:lo + HEAD_W]
            v_ext = jnp.concatenate([vh, ones_col], axis=1)
            cmem = decay[:, h:h + 1] * cstate_ref[h] + _dot_tn(kh * w[:, h:h + 1], v_ext)
            cstate_ref[h] = cmem
            num = _dot(qh, cmem)
            den = jnp.maximum(jnp.abs(num[:, HEAD_W:HEAD_W + 1]), em[:, h:h + 1])
            hh = num[:, :HEAD_W] / den
            hm = _group_norm(hh, mng_ref[:, lo:lo + HEAD_W])
            ym = _sigmoid(oh) * (hm + skip_ref[:, lo:lo + HEAD_W] * ch)
            y_ref[r0:r0 + CHUNK, D_GROUP + lo:D_GROUP + lo + HEAD_W] = ym


def _retention_tables():
    h = np.arange(RET_HEADS, dtype=np.float64)
    log_g = np.log(1.0 - 2.0 ** (-5.0 - h))
    l = np.arange(CHUNK, dtype=np.float64)
    intra = np.exp(log_g[:, None, None] * np.abs(l[:, None] - l[None, :]))
    qd = np.exp(log_g[:, None] * (l + 1.0))[:, :, None] * np.ones((1, 1, HEAD_W))
    kd = np.exp(log_g[:, None] * (CHUNK - 1.0 - l))[:, :, None] * np.ones((1, 1, HEAD_W))
    cd = np.exp(log_g * CHUNK)
    return (jnp.asarray(intra, F32), jnp.asarray(qd, F32), jnp.asarray(kd, F32),
            tuple(float(v) for v in cd))


def _rotary_tables(s):
    half = HEAD_W // 2
    inv = ROPE_BASE ** (-jnp.arange(half, dtype=F32) / half)
    ang = jnp.arange(s, dtype=jnp.int32).astype(F32)[:, None] * inv[None, :]
    cos = jnp.cos(ang)
    sin = jnp.sin(ang)
    return jnp.concatenate([cos, cos], axis=-1), jnp.concatenate([-sin, sin], axis=-1)


def _mixer(proj, b, s, ret_norm_g, m_conv_w, m_conv_b, m_wq, m_wk, m_b_i, m_b_f,
           m_norm_g, m_skip):
    intra, qd, kd, cd = _retention_tables()
    cos_t, sin_t = _rotary_tables(s)
    tri = jnp.asarray(np.tril(np.ones((CHUNK, CHUNK))), F32)
    nj = s // L_BLOCK
    pad = lambda v: jnp.pad(v.reshape(1, -1), ((0, 0), (0, LANES - v.size)))
    full = lambda shape: pl.BlockSpec(shape, lambda bi, j: (0,) * len(shape))
    return pl.pallas_call(
        functools.partial(_mixer_kernel, cd),
        grid=(b, nj),
        in_specs=[
            pl.BlockSpec((L_BLOCK, D_PROJ_PAD), lambda bi, j: (bi * nj + j, 0)),
            pl.BlockSpec((L_BLOCK, HEAD_W), lambda bi, j: (j, 0)),
            pl.BlockSpec((L_BLOCK, HEAD_W), lambda bi, j: (j, 0)),
            full((RET_HEADS, CHUNK, CHUNK)),
            full((RET_HEADS, CHUNK, HEAD_W)),
            full((RET_HEADS, CHUNK, HEAD_W)),
            full((CHUNK, CHUNK)),
            full((1, D_GROUP)),
            full((CONV_W, D_GROUP)),
            full((1, D_GROUP)),
            full((MLSTM_HEADS, MLSTM_D, MLSTM_D)),
            full((MLSTM_HEADS, MLSTM_D, MLSTM_D)),
            full((1, LANES)),
            full((1, LANES)),
            full((1, D_GROUP)),
            full((1, D_GROUP)),
        ],
        out_specs=pl.BlockSpec((L_BLOCK, 2 * D_GROUP), lambda bi, j: (bi * nj + j, 0)),
        out_shape=jax.ShapeDtypeStruct((b * s, 2 * D_GROUP), F32),
        scratch_shapes=[
            pltpu.VMEM((RET_HEADS, RET_DK, HEAD_W), F32),
            pltpu.VMEM((MLSTM_HEADS, MLSTM_D, 2 * HEAD_W), F32),
            pltpu.VMEM((1, LANES), F32),
            pltpu.VMEM((CONV_TAIL, D_GROUP), F32),
        ],
        compiler_params=pltpu.CompilerParams(
            dimension_semantics=("arbitrary", "arbitrary"), vmem_limit_bytes=VMEM_LIMIT),
        name="retention_mlstm_mixer",
    )(proj, cos_t, sin_t, intra, qd, kd, tri,
      ret_norm_g.reshape(1, D_GROUP), m_conv_w, m_conv_b.reshape(1, D_GROUP),
      m_wq.astype(BF16), m_wk.astype(BF16), pad(m_b_i), pad(m_b_f),
      m_norm_g.reshape(1, D_GROUP), m_skip.reshape(1, D_GROUP))


def _post_kernel(y_ref, x_ref, wout_ref, g2_ref, wq_ref, keys_ref,
                 x1_ref, xn_ref, sub_ref):
    x1 = x_ref[...] + jnp.dot(y_ref[...].astype(BF16), wout_ref[...],
                              preferred_element_type=F32)
    x1_ref[...] = x1
    xn = _rms(x1, g2_ref[...])
    xn_ref[...] = xn
    q = jnp.dot(xn.astype(BF16), wq_ref[...], preferred_element_type=F32)
    for i in range(2 * PEER_HEADS):
        lo = i * PEER_HALF
        sub_ref[lo:lo + PEER_NKEYS, :] = _dot_nt(keys_ref[i], q[:, lo:lo + PEER_HALF])


def _post(y, x2, w_out, norm2_g, peer_wq, keys):
    t, d = x2.shape
    nq = peer_wq.shape[1]
    row = lambda w: pl.BlockSpec((TM_PROJ, w), lambda i: (i, 0))
    full = lambda shape: pl.BlockSpec(shape, lambda i: (0,) * len(shape))
    return pl.pallas_call(
        _post_kernel,
        grid=(t // TM_PROJ,),
        in_specs=[row(d), row(d), full((d, d)), full((1, d)), full((d, nq)),
                  full(keys.shape)],
        out_specs=[row(d), row(d), pl.BlockSpec((nq, TM_PROJ), lambda i: (0, i))],
        out_shape=[jax.ShapeDtypeStruct((t, d), F32), jax.ShapeDtypeStruct((t, d), F32),
                   jax.ShapeDtypeStruct((nq, t), F32)],
        compiler_params=pltpu.CompilerParams(
            dimension_semantics=("arbitrary",), vmem_limit_bytes=VMEM_LIMIT),
        name="out_proj_peer_scores",
    )(y, x2, w_out, norm2_g, peer_wq, keys)


def _final_kernel(x1_ref, p_ref, g_ref, o_ref):
    o_ref[...] = _rms(x1_ref[...] + p_ref[...], g_ref[...])


def _final(x1, p, g):
    t, d = x1.shape
    row = pl.BlockSpec((TM_FINAL, d), lambda i: (i, 0))
    return pl.pallas_call(
        _final_kernel,
        grid=(t // TM_FINAL,),
        in_specs=[row, row, pl.BlockSpec((1, d), lambda i: (0, 0))],
        out_specs=row,
        out_shape=jax.ShapeDtypeStruct((t, d), F32),
        compiler_params=pltpu.CompilerParams(dimension_semantics=("arbitrary",)),
        name="final_rmsnorm",
    )(x1, p, g)


TM_ROUTE = 128
CAND_ROWS = 56


def _candidate_tables():
    pairs = [(k1, k2) for k1 in range(PEER_TOPK) for k2 in range(PEER_TOPK)
             if (k1 + 1) * (k2 + 1) <= PEER_TOPK]
    assert len(pairs) <= CAND_ROWS
    rep = np.zeros((CAND_ROWS, 2 * PEER_TOPK), np.float32)
    til = np.zeros((CAND_ROWS, 2 * PEER_TOPK), np.float32)
    for pos, (k1, k2) in enumerate(pairs):
        rep[pos, k1] = 1.0
        til[pos, PEER_TOPK + k2] = 1.0
    return jnp.asarray(rep), jnp.asarray(til), len(pairs)


def _dot_exact(a, b):
    return jnp.dot(a, b, preferred_element_type=F32, precision=lax.Precision.HIGHEST)


def _route_kernel(n_cand, sub_ref, rep_ref, til_ref, seg_ref, exp_ref, gate_ref,
                  s_ref, i_ref, best_ref, eid_ref, m0_ref):
    tm = sub_ref.shape[1]
    neg = -jnp.inf
    key_id = lax.broadcasted_iota(jnp.int32, (PEER_NKEYS, tm), 0).astype(F32)
    cand_id = lax.broadcasted_iota(jnp.int32, (CAND_ROWS, tm), 0).astype(F32)
    rep = rep_ref[...]
    til = til_ref[...]
    for h in range(PEER_HEADS):
        for p in range(2):
            lo = (2 * h + p) * PEER_NKEYS
            x = sub_ref[lo:lo + PEER_NKEYS, :]
            for k in range(PEER_TOPK):
                m = jnp.max(x, axis=0, keepdims=True)
                idx = jnp.min(jnp.where(x == m, key_id, float(PEER_NKEYS)), axis=0, keepdims=True)
                x = jnp.where(key_id == idx, neg, x)
                r = p * PEER_TOPK + k
                s_ref[r:r + 1, :] = m
                i_ref[r:r + 1, :] = idx
        s = s_ref[...]
        iv = i_ref[...]
        cand = _dot_exact(rep, s) + _dot_exact(til, s)
        cand = jnp.where(cand_id < float(n_cand), cand, neg)
        eid = _dot(rep, iv) * float(PEER_NKEYS) + _dot(til, iv)
        for k in range(PEER_TOPK):
            m = jnp.max(cand, axis=0, keepdims=True)
            pos = jnp.min(jnp.where(cand == m, cand_id, float(CAND_ROWS)), axis=0, keepdims=True)
            sel = cand_id == pos
            e = jnp.max(jnp.where(sel, eid, -1.0), axis=0, keepdims=True)
            cand = jnp.where(sel, neg, cand)
            r = h * PEER_TOPK + k
            best_ref[r:r + 1, :] = m
            eid_ref[r:r + 1, :] = e
            if k == 0:
                m0_ref[r:r + PEER_TOPK, :] = jnp.broadcast_to(m, (PEER_TOPK, tm))
    pexp = jnp.exp(best_ref[...] - m0_ref[...])
    gates = pexp / _dot_exact(seg_ref[...], pexp)
    gate_ref[...] = gates.T
    exp_ref[...] = eid_ref[...].T.astype(jnp.int32)


def _peer_route(sub_t):
    nq, t = sub_t.shape
    rep, til, n_cand = _candidate_tables()
    e = PEER_HEADS * PEER_TOPK
    seg = jnp.asarray(np.kron(np.eye(PEER_HEADS), np.ones((PEER_TOPK, PEER_TOPK))), F32)
    full = lambda a: pl.BlockSpec(a.shape, lambda i: (0,) * a.ndim)
    out = pl.BlockSpec((TM_ROUTE, e), lambda i: (i, 0))
    return pl.pallas_call(
        functools.partial(_route_kernel, n_cand),
        grid=(t // TM_ROUTE,),
        in_specs=[pl.BlockSpec((nq, TM_ROUTE), lambda i: (0, i)), full(rep), full(til), full(seg)],
        out_specs=[out, out],
        out_shape=[jax.ShapeDtypeStruct((t, e), jnp.int32), jax.ShapeDtypeStruct((t, e), F32)],
        scratch_shapes=[
            pltpu.VMEM((2 * PEER_TOPK, TM_ROUTE), F32),
            pltpu.VMEM((2 * PEER_TOPK, TM_ROUTE), F32),
            pltpu.VMEM((e, TM_ROUTE), F32),
            pltpu.VMEM((e, TM_ROUTE), F32),
            pltpu.VMEM((e, TM_ROUTE), F32),
        ],
        compiler_params=pltpu.CompilerParams(dimension_semantics=("arbitrary",)),
        name="peer_topk_route",
    )(sub_t, rep, til, seg)


SC_CORES = 2
SC_SUBCORES = 16
SC_LANES = 16
SC_WORKERS = SC_CORES * SC_SUBCORES
PEER_E = PEER_HEADS * PEER_TOPK
SC_ROWS = 32
SC_GATHERS = PEER_E // SC_ROWS
SC_TOK_BLOCK = 8
SC_QUARTER = 256


def _sc_mesh():
    return plsc.VectorSubcoreMesh(core_axis_name="c", subcore_axis_name="s")


def _sc_pipeline(idx_v, tab_hbm, rows_v, sems, compute):
    def gather(step, buf):
        return pltpu.make_async_copy(tab_hbm.at[idx_v.at[step]], rows_v.at[buf], sems.at[buf])

    gather(0, 0).start()

    @pl.loop(0, SC_TOK_BLOCK)
    def _(i):
        for c in range(SC_GATHERS):
            buf = c % 2
            step = i * SC_GATHERS + c
            gather(step, buf).wait()
            if c + 1 < SC_GATHERS:
                gather(step + 1, 1 - buf).start()
            else:
                @pl.when(i + 1 < SC_TOK_BLOCK)
                def _():
                    gather(step + 1, 1 - buf).start()
            compute(i, c, step, buf)


def _peer_down_dots(xn, idx, down, tok_start):
    t_all, d = xn.shape
    t = t_all - tok_start
    tok_w = t // SC_WORKERS
    nblk = tok_w // SC_TOK_BLOCK
    assert nblk * SC_TOK_BLOCK * SC_WORKERS == t
    lanes = SC_LANES

    @functools.partial(
        pl.kernel, mesh=_sc_mesh(),
        out_type=jax.ShapeDtypeStruct((t, PEER_E), F32),
        scratch_types=[
            pltpu.VMEM((SC_TOK_BLOCK * SC_GATHERS, SC_ROWS), jnp.int32),
            pltpu.VMEM((SC_TOK_BLOCK, d), F32),
            pltpu.VMEM((2, SC_ROWS, d), F32),
            pltpu.VMEM((SC_ROWS, lanes), F32),
            pltpu.VMEM((SC_TOK_BLOCK, PEER_E), F32),
            pltpu.SemaphoreType.DMA((2,)),
        ],
        compiler_params=pltpu.CompilerParams(needs_layout_passes=False),
        name="peer_down_dots",
    )
    def k(x_hbm, idx_hbm, tab_hbm, out_hbm, idx_v, x_v, rows_v, acc_v, out_v, sems):
        wid = lax.axis_index("s") * SC_CORES + lax.axis_index("c")
        lane = lax.iota(jnp.int32, lanes)

        @pl.loop(0, nblk)
        def _(blk):
            out0 = wid * tok_w + blk * SC_TOK_BLOCK
            tok0 = tok_start + out0
            pltpu.sync_copy(idx_hbm.at[pl.ds(tok0 * SC_GATHERS, SC_TOK_BLOCK * SC_GATHERS)], idx_v)
            pltpu.sync_copy(x_hbm.at[pl.ds(tok0, SC_TOK_BLOCK)], x_v)

            def compute(i, c, step, buf):
                def body(dc, accs):
                    xv = x_v[i, pl.ds(dc * lanes, lanes)]
                    return tuple(a + rows_v[buf, r, pl.ds(dc * lanes, lanes)] * xv
                                 for r, a in enumerate(accs))
                accs = lax.fori_loop(0, d // lanes, body,
                                     tuple(jnp.zeros((lanes,), F32) for _ in range(SC_ROWS)))
                for r in range(SC_ROWS):
                    acc_v[r, :] = accs[r]
                for g in range(SC_ROWS // lanes):
                    tot = jnp.zeros((lanes,), F32)
                    for l in range(lanes):
                        tot = tot + plsc.load_gather(
                            acc_v, [lane + g * lanes, jnp.full((lanes,), l, jnp.int32)])
                    out_v[i, pl.ds(c * SC_ROWS + g * lanes, lanes)] = tot

            _sc_pipeline(idx_v, tab_hbm, rows_v, sems, compute)
            pltpu.sync_copy(out_v, out_hbm.at[pl.ds(out0, SC_TOK_BLOCK)])

    return k(xn, idx.reshape(t_all * SC_GATHERS, SC_ROWS), down)


def _peer_up_sum(coef, idx, up, tok_start):
    t_all = coef.shape[0]
    t = t_all - tok_start
    d = up.shape[1]
    tok_w = t // SC_WORKERS
    nblk = tok_w // SC_TOK_BLOCK
    assert nblk * SC_TOK_BLOCK * SC_WORKERS == t
    lanes = SC_LANES
    nj = SC_QUARTER // lanes

    @functools.partial(
        pl.kernel, mesh=_sc_mesh(),
        out_type=jax.ShapeDtypeStruct((t, d), F32),
        scratch_types=[
            pltpu.VMEM((SC_TOK_BLOCK * SC_GATHERS, SC_ROWS), jnp.int32),
            pltpu.VMEM((SC_TOK_BLOCK * PEER_E,), F32),
            pltpu.VMEM((2, SC_ROWS, d), F32),
            pltpu.VMEM((SC_TOK_BLOCK, d), F32),
            pltpu.SemaphoreType.DMA((2,)),
        ],
        compiler_params=pltpu.CompilerParams(needs_layout_passes=False),
        name="peer_up_sum",
    )
    def k(coef_hbm, idx_hbm, tab_hbm, out_hbm, idx_v, coef_v, rows_v, out_v, sems):
        wid = lax.axis_index("s") * SC_CORES + lax.axis_index("c")

        @pl.loop(0, nblk)
        def _(blk):
            out0 = wid * tok_w + blk * SC_TOK_BLOCK
            tok0 = tok_start + out0
            pltpu.sync_copy(idx_hbm.at[pl.ds(tok0 * SC_GATHERS, SC_TOK_BLOCK * SC_GATHERS)], idx_v)
            pltpu.sync_copy(coef_hbm.at[pl.ds(tok0 * PEER_E, SC_TOK_BLOCK * PEER_E)], coef_v)

            def compute(i, c, step, buf):
                for q in range(d // SC_QUARTER):
                    col = lambda j: pl.ds(q * SC_QUARTER + j * lanes, lanes)
                    if c == 0:
                        init = tuple(jnp.zeros((lanes,), F32) for _ in range(nj))
                    else:
                        init = tuple(out_v[i, col(j)] for j in range(nj))

                    def body(r, accs):
                        cf = plsc.load_gather(
                            coef_v, [jnp.full((lanes,), step * SC_ROWS + r, jnp.int32)])
                        return tuple(a + rows_v[buf, r, col(j)] * cf for j, a in enumerate(accs))
                    accs = lax.fori_loop(0, SC_ROWS, body, init)
                    for j in range(nj):
                        out_v[i, col(j)] = accs[j]

            _sc_pipeline(idx_v, tab_hbm, rows_v, sems, compute)
            pltpu.sync_copy(out_v, out_hbm.at[pl.ds(out0, SC_TOK_BLOCK)])

    return k(coef.reshape(t_all * PEER_E), idx.reshape(t_all * SC_GATHERS, SC_ROWS), up)


TC_TOK_BLOCK = 64
TC_UNROLL = 16
ROW_WORDS = 4
PEER_TC_TOKENS = 18432


def _pack_rows_bf16(tab):
    n, d = tab.shape
    bits = lax.bitcast_convert_type(tab.astype(BF16), jnp.uint16).astype(jnp.uint32)
    word = bits[:, :d // 2] | (bits[:, d // 2:] << 16)
    return word.reshape(n * ROW_WORDS, LANES)


def _tc_up_kernel(idx_ref, coef_ref, tab_hbm, o_ref, tab_v, sem):
    @pl.when(pl.program_id(0) == 0)
    def _():
        cp = pltpu.make_async_copy(tab_hbm, tab_v, sem)
        cp.start()
        cp.wait()

    def token(i, carry):
        def body(kb, accs):
            lo, hi = accs
            base = i * PEER_E + kb * TC_UNROLL
            for u in range(TC_UNROLL):
                row0 = idx_ref[base + u]
                c = coef_ref[base + u]
                w = tab_v[pl.ds(pl.multiple_of(row0, ROW_WORDS), ROW_WORDS), :]
                lo = lo + c * lax.bitcast_convert_type(w << 16, F32)
                hi = hi + c * lax.bitcast_convert_type(w & jnp.uint32(0xFFFF0000), F32)
            return lo, hi
        z = jnp.zeros((ROW_WORDS, LANES), F32)
        lo, hi = lax.fori_loop(0, PEER_E // TC_UNROLL, body, (z, z))
        o_ref[i, 0:ROW_WORDS, :] = lo
        o_ref[i, ROW_WORDS:2 * ROW_WORDS, :] = hi
        return carry
    lax.fori_loop(0, TC_TOK_BLOCK, token, 0)


def _peer_up_sum_tc(coef, idx, up, n_tok):
    t_all = coef.shape[0]
    d = up.shape[1]
    tabp = _pack_rows_bf16(up)
    smem = lambda: pl.BlockSpec((TC_TOK_BLOCK * PEER_E,), lambda i: (i,), memory_space=pltpu.SMEM)
    out = pl.pallas_call(
        _tc_up_kernel,
        grid=(n_tok // TC_TOK_BLOCK,),
        in_specs=[smem(), smem(), pl.BlockSpec(memory_space=pl.ANY)],
        out_specs=pl.BlockSpec((TC_TOK_BLOCK, 2 * ROW_WORDS, LANES), lambda i: (i, 0, 0)),
        out_shape=jax.ShapeDtypeStruct((n_tok, 2 * ROW_WORDS, LANES), F32),
        scratch_shapes=[pltpu.VMEM(tabp.shape, jnp.uint32), pltpu.SemaphoreType.DMA],
        compiler_params=pltpu.CompilerParams(
            dimension_semantics=("arbitrary",), vmem_limit_bytes=VMEM_LIMIT),
        name="peer_up_sum_tc",
    )((idx * ROW_WORDS).reshape(t_all * PEER_E), coef.reshape(t_all * PEER_E), tabp)
    return out.reshape(n_tok, d)


ROW_HALVES = 2 * ROW_WORDS


def _tc_down_kernel(idx_ref, xw_ref, tab_hbm, diag_ref, fold_ref, o_ref,
                    tab_v, stack_ref, z_ref, sem):
    @pl.when(pl.program_id(0) == 0)
    def _():
        cp = pltpu.make_async_copy(tab_hbm, tab_v, sem)
        cp.start()
        cp.wait()

    def token_pair(ip, carry):
        for par in range(2):
            i = ip * 2 + par
            for k in range(PEER_E):
                row0 = idx_ref[i, k]
                stack_ref[par, ROW_WORDS * k:ROW_WORDS * (k + 1), :] = (
                    tab_v[pl.ds(pl.multiple_of(row0, ROW_WORDS), ROW_WORDS), :])
        for par in range(2):
            i = ip * 2 + par
            stack = pltpu.bitcast(stack_ref[par], BF16)
            x8 = pltpu.bitcast(xw_ref[i], BF16)
            a = lax.dot_general(x8, stack, (((1,), (1,)), ((), ())), preferred_element_type=F32)
            z_ref[pl.ds(i, 1), :] = jnp.sum(a * diag_ref[...], axis=0, keepdims=True)
        return carry
    lax.fori_loop(0, TC_TOK_BLOCK // 2, token_pair, 0)
    o_ref[...] = _dot_exact(z_ref[...], fold_ref[...])


def _peer_down_dots_tc(xn, idx, down, n_tok):
    d = xn.shape[1]
    tabp = _pack_rows_bf16(down)
    xw = _pack_rows_bf16(xn[:n_tok]).reshape(n_tok, ROW_WORDS, LANES)
    cols = PEER_E * ROW_HALVES
    diag = jnp.asarray((np.arange(cols)[None, :] % ROW_HALVES
                        == np.arange(ROW_HALVES)[:, None]).astype(np.float32))
    fold = jnp.asarray((np.arange(cols)[:, None] // ROW_HALVES
                        == np.arange(PEER_E)[None, :]).astype(np.float32))
    full = lambda a: pl.BlockSpec(a.shape, lambda i: (0,) * a.ndim)
    return pl.pallas_call(
        _tc_down_kernel,
        grid=(n_tok // TC_TOK_BLOCK,),
        in_specs=[pl.BlockSpec((TC_TOK_BLOCK, PEER_E), lambda i: (i, 0), memory_space=pltpu.SMEM),
                  pl.BlockSpec((TC_TOK_BLOCK, ROW_WORDS, LANES), lambda i: (i, 0, 0)),
                  pl.BlockSpec(memory_space=pl.ANY), full(diag), full(fold)],
        out_specs=pl.BlockSpec((TC_TOK_BLOCK, PEER_E), lambda i: (i, 0)),
        out_shape=jax.ShapeDtypeStruct((n_tok, PEER_E), F32),
        scratch_shapes=[pltpu.VMEM(tabp.shape, jnp.uint32),
                        pltpu.VMEM((2, PEER_E * ROW_WORDS, LANES), jnp.uint32),
                        pltpu.VMEM((TC_TOK_BLOCK, cols), F32),
                        pltpu.SemaphoreType.DMA],
        compiler_params=pltpu.CompilerParams(
            dimension_semantics=("arbitrary",), vmem_limit_bytes=VMEM_LIMIT),
        name="peer_down_dots_tc",
    )(idx * ROW_WORDS, xw, tabp, diag, fold)


def _coef_kernel(pre_ref, gate_ref, o_ref):
    x = pre_ref[...]
    gelu = 0.5 * x * (1.0 + lax.erf(x * (2.0 ** -0.5)))
    o_ref[...] = gate_ref[...] * gelu


def _peer_coef(pre, gates):
    t, e = pre.shape
    row = pl.BlockSpec((TM_FINAL, e), lambda i: (i, 0))
    return pl.pallas_call(
        _coef_kernel,
        grid=(t // TM_FINAL,),
        in_specs=[row, row],
        out_specs=row,
        out_shape=jax.ShapeDtypeStruct((t, e), F32),
        compiler_params=pltpu.CompilerParams(dimension_semantics=("arbitrary",)),
        name="peer_gate_gelu",
    )(pre, gates)


def _peer_experts(xn, expert, gates, down, up):
    pre_sc = _peer_down_dots(xn, expert, down, PEER_TC_TOKENS)
    pre_tc = _peer_down_dots_tc(xn, expert, down, PEER_TC_TOKENS)
    coef = _peer_coef(jnp.concatenate([pre_tc, pre_sc], axis=0), gates)
    p_sc = _peer_up_sum(coef, expert, up, PEER_TC_TOKENS)
    p_tc = _peer_up_sum_tc(coef, expert, up, PEER_TC_TOKENS)
    return jnp.concatenate([p_tc, p_sc], axis=0)


def kernel(x, norm1_g, w_in, ret_norm_g, m_conv_w, m_conv_b, m_wq, m_wk, m_b_i, m_b_f,
           m_norm_g, m_skip, w_out, norm2_g, peer_wq, peer_keys, peer_down, peer_up, final_g):
    b, s, d = x.shape
    t = b * s
    x2 = x.reshape(t, d)
    depth = norm1_g.shape[0]
    for i in range(depth):
        if i > 0:
            x2 = x1 + p
        wi = w_in[i]
        n_main = OFF_GI
        gate_pad = lambda w: jnp.pad(w, ((0, 0), (0, LANES - w.shape[1])))
        w_pad = jnp.concatenate(
            [wi[:, :n_main], gate_pad(wi[:, n_main:n_main + MLSTM_HEADS]),
             gate_pad(wi[:, n_main + MLSTM_HEADS:])], axis=1).astype(BF16)
        proj = _rms_proj(x2, norm1_g[i].reshape(1, d), w_pad)
        y = _mixer(proj, b, s, ret_norm_g[i], m_conv_w[i], m_conv_b[i], m_wq[i], m_wk[i],
                   m_b_i[i], m_b_f[i], m_norm_g[i], m_skip[i])
        keys = peer_keys[i].reshape(2 * PEER_HEADS, PEER_NKEYS, PEER_HALF).astype(BF16)
        x1, xn, sub_t = _post(y, x2, w_out[i].astype(BF16), norm2_g[i].reshape(1, d),
                              peer_wq[i].astype(BF16), keys)
        expert, gates = _peer_route(sub_t)
        p = _peer_experts(xn, expert, gates, peer_down[i], peer_up[i])
    out = _final(x1, p, final_g.reshape(1, d))
    return out.reshape(b, s, d)
```

```python
import functools

import numpy as np
import jax
import jax.numpy as jnp
from jax import lax
from jax.experimental import pallas as pl
from jax.experimental.pallas import tpu as pltpu
from jax.experimental.pallas import tpu_sc as plsc

F32 = jnp.float32
BF16 = jnp.bfloat16

D_MODEL = 1024
CHUNK = 64
RET_HEADS = 4
RET_DK = 128
MLSTM_HEADS = 4
MLSTM_D = 128
HEAD_W = 128
D_GROUP = 512
CONV_W = 4
ROPE_BASE = 10000.0
PEER_HEADS = 8
PEER_NKEYS = 128
PEER_TOPK = 16
PEER_HALF = 128
PEER_BLOCK = 128
EPS = 1e-6

OFF_RQ, OFF_RK, OFF_RV, OFF_RG = 0, 512, 1024, 1536
OFF_MX, OFF_MV, OFF_MO = 2048, 2560, 3072
OFF_GI, OFF_GF = 3584, 3712
D_PROJ_PAD = 3840

LANES = 128
CONV_TAIL = 8
VMEM_LIMIT = 56 * 1024 * 1024

TM_PROJ = 256
L_BLOCK = 256
TM_FINAL = 1024


def _rms(x, g):
    ms = jnp.mean(x * x, axis=-1, keepdims=True)
    return x * lax.rsqrt(ms + EPS) * g


def _group_norm(h, g):
    mu = jnp.mean(h, axis=-1, keepdims=True)
    d = h - mu
    var = jnp.mean(d * d, axis=-1, keepdims=True)
    return d * lax.rsqrt(var + EPS) * g


def _silu(x):
    return x * (1.0 / (1.0 + jnp.exp(-x)))


def _sigmoid(x):
    return 1.0 / (1.0 + jnp.exp(-x))


def _dot(a, b):
    return jnp.dot(a.astype(BF16), b.astype(BF16), preferred_element_type=F32)


def _dot_tn(a, b):
    return lax.dot_general(a.astype(BF16), b.astype(BF16), (((0,), (0,)), ((), ())),
                           preferred_element_type=F32)


def _dot_nt(a, b):
    return lax.dot_general(a.astype(BF16), b.astype(BF16), (((1,), (1,)), ((), ())),
                           preferred_element_type=F32)


def _rms_proj_kernel(x_ref, g_ref, w_ref, o_ref):
    h = _rms(x_ref[...], g_ref[...])
    o_ref[...] = jnp.dot(h.astype(BF16), w_ref[...], preferred_element_type=F32)


def _rms_proj(x2, g, w):
    t, d = x2.shape
    n = w.shape[1]
    return pl.pallas_call(
        _rms_proj_kernel,
        grid=(t // TM_PROJ,),
        in_specs=[
            pl.BlockSpec((TM_PROJ, d), lambda i: (i, 0)),
            pl.BlockSpec((1, d), lambda i: (0, 0)),
            pl.BlockSpec((d, n), lambda i: (0, 0)),
        ],
        out_specs=pl.BlockSpec((TM_PROJ, n), lambda i: (i, 0)),
        out_shape=jax.ShapeDtypeStruct((t, n), F32),
        compiler_params=pltpu.CompilerParams(
            dimension_semantics=("arbitrary",), vmem_limit_bytes=VMEM_LIMIT),
        name="rms_in_proj",
    )(x2, g, w)


def _mixer_kernel(chunk_decay,
                  proj_ref, cos_ref, sin_ref, intra_ref, qd_ref, kd_ref, tri_ref,
                  rng_ref, cw_ref, cb_ref, wq_ref, wk_ref, bi_ref, bf_ref,
                  mng_ref, skip_ref,
                  y_ref,
                  rstate_ref, cstate_ref, m_ref, tail_ref):
    j = pl.program_id(1)
    n_chunks = L_BLOCK // CHUNK

    @pl.when(j == 0)
    def _():
        rstate_ref[...] = jnp.zeros_like(rstate_ref)
        cstate_ref[...] = jnp.zeros_like(cstate_ref)
        m_ref[...] = jnp.zeros_like(m_ref)
        tail_ref[...] = jnp.zeros_like(tail_ref)

    cosv = cos_ref[...]
    sinv = sin_ref[...]
    k_scale = RET_DK ** -0.5

    for h in range(RET_HEADS):
        lo = h * HEAD_W
        q = proj_ref[:, OFF_RQ + lo:OFF_RQ + lo + HEAD_W]
        k = proj_ref[:, OFF_RK + lo:OFF_RK + lo + HEAD_W]
        q = q * cosv + pltpu.roll(q, HEAD_W // 2, 1) * sinv
        k = (k * cosv + pltpu.roll(k, HEAD_W // 2, 1) * sinv) * k_scale
        intra = intra_ref[h]
        qd = qd_ref[h]
        kd = kd_ref[h]
        g = rng_ref[:, lo:lo + HEAD_W]
        for c in range(n_chunks):
            r0 = c * CHUNK
            qc = q[r0:r0 + CHUNK]
            kc = k[r0:r0 + CHUNK]
            vc = proj_ref[r0:r0 + CHUNK, OFF_RV + lo:OFF_RV + lo + HEAD_W]
            gate = proj_ref[r0:r0 + CHUNK, OFF_RG + lo:OFF_RG + lo + HEAD_W]
            state = rstate_ref[h]
            scores = _dot_nt(qc, kc) * intra
            o = _dot(scores, vc) + _dot(qc, state) * qd
            rstate_ref[h] = chunk_decay[h] * state + _dot_tn(kc * kd, vc)
            y_ref[r0:r0 + CHUNK, lo:lo + HEAD_W] = _silu(gate) * _group_norm(o, g)

    mx = proj_ref[:, OFF_MX:OFF_MX + D_GROUP]
    xp = jnp.concatenate([tail_ref[...], mx], axis=0)
    base = CONV_TAIL - (CONV_W - 1)
    conv = cb_ref[...] + xp[base:base + L_BLOCK] * cw_ref[0:1, :]
    for t in range(1, CONV_W):
        conv = conv + xp[base + t:base + t + L_BLOCK] * cw_ref[t:t + 1, :]
    tail_ref[...] = mx[L_BLOCK - CONV_TAIL:L_BLOCK]
    cact = _silu(conv)

    gi = proj_ref[:, OFF_GI:OFF_GI + LANES] + bi_ref[...]
    gf = proj_ref[:, OFF_GF:OFF_GF + LANES] + bf_ref[...]
    logf = jnp.minimum(gf, 0.0) - jnp.log1p(jnp.exp(-jnp.abs(gf)))
    tri = tri_ref[...]
    lane = lax.broadcasted_iota(jnp.int32, (CHUNK, HEAD_W), 1)
    ones_col = (lane == 0).astype(F32)
    m_scale = MLSTM_D ** -0.5

    mq = []
    mk = []
    for h in range(MLSTM_HEADS):
        lo = h * HEAD_W
        ch = cact[:, lo:lo + HEAD_W]
        mq.append(_dot(ch, wq_ref[h]))
        mk.append(_dot(ch, wk_ref[h]) * m_scale)

    for c in range(n_chunks):
        r0 = c * CHUNK
        fcum = jnp.dot(tri, logf[r0:r0 + CHUNK], preferred_element_type=F32,
                       precision=lax.Precision.HIGHEST)
        a = fcum[CHUNK - 1:CHUNK]
        logw = a - fcum + gi[r0:r0 + CHUNK]
        m_in = jnp.max(logw, axis=0, keepdims=True)
        m_old = m_ref[...]
        m_new = jnp.maximum(a + m_old, m_in)
        decay = jnp.exp(a + m_old - m_new)
        w = jnp.exp(logw - m_new)
        em = jnp.exp(-m_new)
        m_ref[...] = m_new
        for h in range(MLSTM_HEADS):
            lo = h * HEAD_W
            qh = mq[h][r0:r0 + CHUNK]
            kh = mk[h][r0:r0 + CHUNK]
            vh = proj_ref[r0:r0 + CHUNK, OFF_MV + lo:OFF_MV + lo + HEAD_W]
            oh = proj_ref[r0:r0 + CHUNK, OFF_MO + lo:OFF_MO + lo + HEAD_W]
            ch = cact[r0:r0 + CHUNK, lo:lo + HEAD_W]
            v_ext = jnp.concatenate([vh, ones_col], axis=1)
            cmem = decay[:, h:h + 1] * cstate_ref[h] + _dot_tn(kh * w[:, h:h + 1], v_ext)
            cstate_ref[h] = cmem
            num = _dot(qh, cmem)
            den = jnp.maximum(jnp.abs(num[:, HEAD_W:HEAD_W + 1]), em[:, h:h + 1])
            hh = num[:, :HEAD_W] / den
            hm = _group_norm(hh, mng_ref[:, lo:lo + HEAD_W])
            ym = _sigmoid(oh) * (hm + skip_ref[:, lo:lo + HEAD_W] * ch)
            y_ref[r0:r0 + CHUNK, D_GROUP + lo:D_GROUP + lo + HEAD_W] = ym


def _retention_tables():
    h = np.arange(RET_HEADS, dtype=np.float64)
    log_g = np.log(1.0 - 2.0 ** (-5.0 - h))
    l = np.arange(CHUNK, dtype=np.float64)
    intra = np.exp(log_g[:, None, None] * np.abs(l[:, None] - l[None, :]))
    qd = np.exp(log_g[:, None] * (l + 1.0))[:, :, None] * np.ones((1, 1, HEAD_W))
    kd = np.exp(log_g[:, None] * (CHUNK - 1.0 - l))[:, :, None] * np.ones((1, 1, HEAD_W))
    cd = np.exp(log_g * CHUNK)
    return (jnp.asarray(intra, F32), jnp.asarray(qd, F32), jnp.asarray(kd, F32),
            tuple(float(v) for v in cd))


def _rotary_tables(s):
    half = HEAD_W // 2
    inv = ROPE_BASE ** (-jnp.arange(half, dtype=F32) / half)
    ang = jnp.arange(s, dtype=jnp.int32).astype(F32)[:, None] * inv[None, :]
    cos = jnp.cos(ang)
    sin = jnp.sin(ang)
    return jnp.concatenate([cos, cos], axis=-1), jnp.concatenate([-sin, sin], axis=-1)


def _mixer(proj, b, s, ret_norm_g, m_conv_w, m_conv_b, m_wq, m_wk, m_b_i, m_b_f,
           m_norm_g, m_skip):
    intra, qd, kd, cd = _retention_tables()
    cos_t, sin_t = _rotary_tables(s)
    tri = jnp.asarray(np.tril(np.ones((CHUNK, CHUNK))), F32)
    nj = s // L_BLOCK
    pad = lambda v: jnp.pad(v.reshape(1, -1), ((0, 0), (0, LANES - v.size)))
    full = lambda shape: pl.BlockSpec(shape, lambda bi, j: (0,) * len(shape))
    return pl.pallas_call(
        functools.partial(_mixer_kernel, cd),
        grid=(b, nj),
        in_specs=[
            pl.BlockSpec((L_BLOCK, D_PROJ_PAD), lambda bi, j: (bi * nj + j, 0)),
            pl.BlockSpec((L_BLOCK, HEAD_W), lambda bi, j: (j, 0)),
            pl.BlockSpec((L_BLOCK, HEAD_W), lambda bi, j: (j, 0)),
            full((RET_HEADS, CHUNK, CHUNK)),
            full((RET_HEADS, CHUNK, HEAD_W)),
            full((RET_HEADS, CHUNK, HEAD_W)),
            full((CHUNK, CHUNK)),
            full((1, D_GROUP)),
            full((CONV_W, D_GROUP)),
            full((1, D_GROUP)),
            full((MLSTM_HEADS, MLSTM_D, MLSTM_D)),
            full((MLSTM_HEADS, MLSTM_D, MLSTM_D)),
            full((1, LANES)),
            full((1, LANES)),
            full((1, D_GROUP)),
            full((1, D_GROUP)),
        ],
        out_specs=pl.BlockSpec((L_BLOCK, 2 * D_GROUP), lambda bi, j: (bi * nj + j, 0)),
        out_shape=jax.ShapeDtypeStruct((b * s, 2 * D_GROUP), F32),
        scratch_shapes=[
            pltpu.VMEM((RET_HEADS, RET_DK, HEAD_W), F32),
            pltpu.VMEM((MLSTM_HEADS, MLSTM_D, 2 * HEAD_W), F32),
            pltpu.VMEM((1, LANES), F32),
            pltpu.VMEM((CONV_TAIL, D_GROUP), F32),
        ],
        compiler_params=pltpu.CompilerParams(
            dimension_semantics=("arbitrary", "arbitrary"), vmem_limit_bytes=VMEM_LIMIT),
        name="retention_mlstm_mixer",
    )(proj, cos_t, sin_t, intra, qd, kd, tri,
      ret_norm_g.reshape(1, D_GROUP), m_conv_w, m_conv_b.reshape(1, D_GROUP),
      m_wq.astype(BF16), m_wk.astype(BF16), pad(m_b_i), pad(m_b_f),
      m_norm_g.reshape(1, D_GROUP), m_skip.reshape(1, D_GROUP))


def _post_kernel(y_ref, x_ref, wout_ref, g2_ref, wq_ref, keys_ref,
                 x1_ref, xn_ref, sub_ref):
    x1 = x_ref[...] + jnp.dot(y_ref[...].astype(BF16), wout_ref[...],
                              preferred_element_type=F32)
    x1_ref[...] = x1
    xn = _rms(x1, g2_ref[...])
    xn_ref[...] = xn
    q = jnp.dot(xn.astype(BF16), wq_ref[...], preferred_element_type=F32)
    for i in range(2 * PEER_HEADS):
        lo = i * PEER_HALF
        sub_ref[lo:lo + PEER_NKEYS, :] = _dot_nt(keys_ref[i], q[:, lo:lo + PEER_HALF])


def _post(y, x2, w_out, norm2_g, peer_wq, keys):
    t, d = x2.shape
    nq = peer_wq.shape[1]
    row = lambda w: pl.BlockSpec((TM_PROJ, w), lambda i: (i, 0))
    full = lambda shape: pl.BlockSpec(shape, lambda i: (0,) * len(shape))
    return pl.pallas_call(
        _post_kernel,
        grid=(t // TM_PROJ,),
        in_specs=[row(d), row(d), full((d, d)), full((1, d)), full((d, nq)),
                  full(keys.shape)],
        out_specs=[row(d), row(d), pl.BlockSpec((nq, TM_PROJ), lambda i: (0, i))],
        out_shape=[jax.ShapeDtypeStruct((t, d), F32), jax.ShapeDtypeStruct((t, d), F32),
                   jax.ShapeDtypeStruct((nq, t), F32)],
        compiler_params=pltpu.CompilerParams(
            dimension_semantics=("arbitrary",), vmem_limit_bytes=VMEM_LIMIT),
        name="out_proj_peer_scores",
    )(y, x2, w_out, norm2_g, peer_wq, keys)


def _final_kernel(x1_ref, p_ref, g_ref, o_ref):
    o_ref[...] = _rms(x1_ref[...] + p_ref[...], g_ref[...])


def _final(x1, p, g):
    t, d = x1.shape
    row = pl.BlockSpec((TM_FINAL, d), lambda i: (i, 0))
    return pl.pallas_call(
        _final_kernel,
        grid=(t // TM_FINAL,),
        in_specs=[row, row, pl.BlockSpec((1, d), lambda i: (0, 0))],
        out_specs=row,
        out_shape=jax.ShapeDtypeStruct((t, d), F32),
        compiler_params=pltpu.CompilerParams(dimension_semantics=("arbitrary",)),
        name="final_rmsnorm",
    )(x1, p, g)


TM_ROUTE = 128
CAND_ROWS = 56


def _candidate_tables():
    pairs = [(k1, k2) for k1 in range(PEER_TOPK) for k2 in range(PEER_TOPK)
             if (k1 + 1) * (k2 + 1) <= PEER_TOPK]
    assert len(pairs) <= CAND_ROWS
    rep = np.zeros((CAND_ROWS, 2 * PEER_TOPK), np.float32)
    til = np.zeros((CAND_ROWS, 2 * PEER_TOPK), np.float32)
    for pos, (k1, k2) in enumerate(pairs):
        rep[pos, k1] = 1.0
        til[pos, PEER_TOPK + k2] = 1.0
    return jnp.asarray(rep), jnp.asarray(til), len(pairs)


def _dot_exact(a, b):
    return jnp.dot(a, b, preferred_element_type=F32, precision=lax.Precision.HIGHEST)


def _route_kernel(n_cand, sub_ref, rep_ref, til_ref, seg_ref, exp_ref, gate_ref,
                  s_ref, i_ref, best_ref, eid_ref, m0_ref):
    tm = sub_ref.shape[1]
    neg = -jnp.inf
    key_id = lax.broadcasted_iota(jnp.int32, (PEER_NKEYS, tm), 0).astype(F32)
    cand_id = lax.broadcasted_iota(jnp.int32, (CAND_ROWS, tm), 0).astype(F32)
    rep = rep_ref[...]
    til = til_ref[...]
    for h in range(PEER_HEADS):
        for p in range(2):
            lo = (2 * h + p) * PEER_NKEYS
            x = sub_ref[lo:lo + PEER_NKEYS, :]
            for k in range(PEER_TOPK):
                m = jnp.max(x, axis=0, keepdims=True)
                idx = jnp.min(jnp.where(x == m, key_id, float(PEER_NKEYS)), axis=0, keepdims=True)
                x = jnp.where(key_id == idx, neg, x)
                r = p * PEER_TOPK + k
                s_ref[r:r + 1, :] = m
                i_ref[r:r + 1, :] = idx
        s = s_ref[...]
        iv = i_ref[...]
        cand = _dot_exact(rep, s) + _dot_exact(til, s)
        cand = jnp.where(cand_id < float(n_cand), cand, neg)
        eid = _dot(rep, iv) * float(PEER_NKEYS) + _dot(til, iv)
        for k in range(PEER_TOPK):
            m = jnp.max(cand, axis=0, keepdims=True)
            pos = jnp.min(jnp.where(cand == m, cand_id, float(CAND_ROWS)), axis=0, keepdims=True)
            sel = cand_id == pos
            e = jnp.max(jnp.where(sel, eid, -1.0), axis=0, keepdims=True)
            cand = jnp.where(sel, neg, cand)
            r = h * PEER_TOPK + k
            best_ref[r:r + 1, :] = m
            eid_ref[r:r + 1, :] = e
            if k == 0:
                m0_ref[r:r + PEER_TOPK, :] = jnp.broadcast_to(m, (PEER_TOPK, tm))
    pexp = jnp.exp(best_ref[...] - m0_ref[...])
    gates = pexp / _dot_exact(seg_ref[...], pexp)
    gate_ref[...] = gates.T
    exp_ref[...] = eid_ref[...].T.astype(jnp.int32)


def _peer_route(sub_t):
    nq, t = sub_t.shape
    rep, til, n_cand = _candidate_tables()
    e = PEER_HEADS * PEER_TOPK
    seg = jnp.asarray(np.kron(np.eye(PEER_HEADS), np.ones((PEER_TOPK, PEER_TOPK))), F32)
    full = lambda a: pl.BlockSpec(a.shape, lambda i: (0,) * a.ndim)
    out = pl.BlockSpec((TM_ROUTE, e), lambda i: (i, 0))
    return pl.pallas_call(
        functools.partial(_route_kernel, n_cand),
        grid=(t // TM_ROUTE,),
        in_specs=[pl.BlockSpec((nq, TM_ROUTE), lambda i: (0, i)), full(rep), full(til), full(seg)],
        out_specs=[out, out],
        out_shape=[jax.ShapeDtypeStruct((t, e), jnp.int32), jax.ShapeDtypeStruct((t, e), F32)],
        scratch_shapes=[
            pltpu.VMEM((2 * PEER_TOPK, TM_ROUTE), F32),
            pltpu.VMEM((2 * PEER_TOPK, TM_ROUTE), F32),
            pltpu.VMEM((e, TM_ROUTE), F32),
            pltpu.VMEM((e, TM_ROUTE), F32),
            pltpu.VMEM((e, TM_ROUTE), F32),
        ],
        compiler_params=pltpu.CompilerParams(dimension_semantics=("arbitrary",)),
        name="peer_topk_route",
    )(sub_t, rep, til, seg)


SC_CORES = 2
SC_SUBCORES = 16
SC_LANES = 16
SC_WORKERS = SC_CORES * SC_SUBCORES
PEER_E = PEER_HEADS * PEER_TOPK
SC_ROWS = 32
SC_GATHERS = PEER_E // SC_ROWS
SC_TOK_BLOCK = 8
SC_QUARTER = 256


def _sc_mesh():
    return plsc.VectorSubcoreMesh(core_axis_name="c", subcore_axis_name="s")


def _sc_pipeline(idx_v, tab_hbm, rows_v, sems, compute):
    def gather(step, buf):
        return pltpu.make_async_copy(tab_hbm.at[idx_v.at[step]], rows_v.at[buf], sems.at[buf])

    gather(0, 0).start()

    @pl.loop(0, SC_TOK_BLOCK)
    def _(i):
        for c in range(SC_GATHERS):
            buf = c % 2
            step = i * SC_GATHERS + c
            gather(step, buf).wait()
            if c + 1 < SC_GATHERS:
                gather(step + 1, 1 - buf).start()
            else:
                @pl.when(i + 1 < SC_TOK_BLOCK)
                def _():
                    gather(step + 1, 1 - buf).start()
            compute(i, c, step, buf)


def _peer_down_dots(xn, idx, down, tok_start):
    t_all, d = xn.shape
    t = t_all - tok_start
    tok_w = t // SC_WORKERS
    nblk = tok_w // SC_TOK_BLOCK
    assert nblk * SC_TOK_BLOCK * SC_WORKERS == t
    lanes = SC_LANES

    @functools.partial(
        pl.kernel, mesh=_sc_mesh(),
        out_type=jax.ShapeDtypeStruct((t, PEER_E), F32),
        scratch_types=[
            pltpu.VMEM((SC_TOK_BLOCK * SC_GATHERS, SC_ROWS), jnp.int32),
            pltpu.VMEM((SC_TOK_BLOCK, d), F32),
            pltpu.VMEM((2, SC_ROWS, d), F32),
            pltpu.VMEM((SC_ROWS, lanes), F32),
            pltpu.VMEM((SC_TOK_BLOCK, PEER_E), F32),
            pltpu.SemaphoreType.DMA((2,)),
        ],
        compiler_params=pltpu.CompilerParams(needs_layout_passes=False),
        name="peer_down_dots",
    )
    def k(x_hbm, idx_hbm, tab_hbm, out_hbm, idx_v, x_v, rows_v, acc_v, out_v, sems):
        wid = lax.axis_index("s") * SC_CORES + lax.axis_index("c")
        lane = lax.iota(jnp.int32, lanes)

        @pl.loop(0, nblk)
        def _(blk):
            out0 = wid * tok_w + blk * SC_TOK_BLOCK
            tok0 = tok_start + out0
            pltpu.sync_copy(idx_hbm.at[pl.ds(tok0 * SC_GATHERS, SC_TOK_BLOCK * SC_GATHERS)], idx_v)
            pltpu.sync_copy(x_hbm.at[pl.ds(tok0, SC_TOK_BLOCK)], x_v)

            def compute(i, c, step, buf):
                def body(dc, accs):
                    xv = x_v[i, pl.ds(dc * lanes, lanes)]
                    return tuple(a + rows_v[buf, r, pl.ds(dc * lanes, lanes)] * xv
                                 for r, a in enumerate(accs))
                accs = lax.fori_loop(0, d // lanes, body,
                                     tuple(jnp.zeros((lanes,), F32) for _ in range(SC_ROWS)))
                for r in range(SC_ROWS):
                    acc_v[r, :] = accs[r]
                for g in range(SC_ROWS // lanes):
                    tot = jnp.zeros((lanes,), F32)
                    for l in range(lanes):
                        tot = tot + plsc.load_gather(
                            acc_v, [lane + g * lanes, jnp.full((lanes,), l, jnp.int32)])
                    out_v[i, pl.ds(c * SC_ROWS + g * lanes, lanes)] = tot

            _sc_pipeline(idx_v, tab_hbm, rows_v, sems, compute)
            pltpu.sync_copy(out_v, out_hbm.at[pl.ds(out0, SC_TOK_BLOCK)])

    return k(xn, idx.reshape(t_all * SC_GATHERS, SC_ROWS), down)


def _peer_up_sum(coef, idx, up, tok_start):
    t_all = coef.shape[0]
    t = t_all - tok_start
    d = up.shape[1]
    tok_w = t // SC_WORKERS
    nblk = tok_w // SC_TOK_BLOCK
    assert nblk * SC_TOK_BLOCK * SC_WORKERS == t
    lanes = SC_LANES
    nj = SC_QUARTER // lanes

    @functools.partial(
        pl.kernel, mesh=_sc_mesh(),
        out_type=jax.ShapeDtypeStruct((t, d), F32),
        scratch_types=[
            pltpu.VMEM((SC_TOK_BLOCK * SC_GATHERS, SC_ROWS), jnp.int32),
            pltpu.VMEM((SC_TOK_BLOCK * PEER_E,), F32),
            pltpu.VMEM((2, SC_ROWS, d), F32),
            pltpu.VMEM((SC_TOK_BLOCK, d), F32),
            pltpu.SemaphoreType.DMA((2,)),
        ],
        compiler_params=pltpu.CompilerParams(needs_layout_passes=False),
        name="peer_up_sum",
    )
    def k(coef_hbm, idx_hbm, tab_hbm, out_hbm, idx_v, coef_v, rows_v, out_v, sems):
        wid = lax.axis_index("s") * SC_CORES + lax.axis_index("c")

        @pl.loop(0, nblk)
        def _(blk):
            out0 = wid * tok_w + blk * SC_TOK_BLOCK
            tok0 = tok_start + out0
            pltpu.sync_copy(idx_hbm.at[pl.ds(tok0 * SC_GATHERS, SC_TOK_BLOCK * SC_GATHERS)], idx_v)
            pltpu.sync_copy(coef_hbm.at[pl.ds(tok0 * PEER_E, SC_TOK_BLOCK * PEER_E)], coef_v)

            def compute(i, c, step, buf):
                for q in range(d // SC_QUARTER):
                    col = lambda j: pl.ds(q * SC_QUARTER + j * lanes, lanes)
                    if c == 0:
                        init = tuple(jnp.zeros((lanes,), F32) for _ in range(nj))
                    else:
                        init = tuple(out_v[i, col(j)] for j in range(nj))

                    def body(r, accs):
                        cf = plsc.load_gather(
                            coef_v, [jnp.full((lanes,), step * SC_ROWS + r, jnp.int32)])
                        return tuple(a + rows_v[buf, r, col(j)] * cf for j, a in enumerate(accs))
                    accs = lax.fori_loop(0, SC_ROWS, body, init)
                    for j in range(nj):
                        out_v[i, col(j)] = accs[j]

            _sc_pipeline(idx_v, tab_hbm, rows_v, sems, compute)
            pltpu.sync_copy(out_v, out_hbm.at[pl.ds(out0, SC_TOK_BLOCK)])

    return k(coef.reshape(t_all * PEER_E), idx.reshape(t_all * SC_GATHERS, SC_ROWS), up)


TC_SUB = 128
TC_SUBS = 2
TC_STEP = TC_SUB * TC_SUBS
TC_GROUP = 4
ROW_WORDS = 4
ROW_HALVES = 2 * ROW_WORDS
STACK_COLS = PEER_E * ROW_HALVES
PEER_TC_DOWN_TOKENS = 20480
PEER_TC_UP_TOKENS = 23552


def _pack_rows_bf16(tab):
    n, d = tab.shape
    bits = lax.bitcast_convert_type(tab.astype(BF16), jnp.uint16).astype(jnp.uint32)
    b = bits.reshape(n, ROW_WORDS, 2, LANES)
    return (b[:, :, 0, :] | (b[:, :, 1, :] << 16)).reshape(n * ROW_WORDS, LANES)


def _block_major_row_offsets(idx, n_tok):
    rows = idx[:n_tok] * ROW_WORDS
    return rows.reshape(n_tok // TC_SUB, TC_SUB, PEER_E).swapaxes(1, 2).reshape(n_tok * PEER_E)


def _load_table_once(tab_hbm, tab_v, sem):
    @pl.when(pl.program_id(0) == 0)
    def _():
        cp = pltpu.make_async_copy(tab_hbm, tab_v, sem)
        cp.start()
        cp.wait()


def _tc_token_pipeline(idx_hbm, idx_s, isem, tab_v, stack_ref, compute):
    step = pl.program_id(0)
    words = TC_SUB * PEER_E

    def idx_copy(block, slot):
        return pltpu.make_async_copy(idx_hbm.at[pl.ds(block * words, words)], idx_s[slot],
                                     isem.at[slot])

    @pl.when(step == 0)
    def _():
        idx_copy(0, 0).start()

    for sub in range(TC_SUBS):
        block = step * TC_SUBS + sub
        idx_copy(block, sub).wait()
        if sub + 1 < TC_SUBS:
            idx_copy(block + 1, sub + 1).start()
        else:
            @pl.when(step + 1 < pl.num_programs(0))
            def _():
                idx_copy(block + 1, 0).start()
        offs = idx_s[sub]

        def copy_rows(i, buf):
            for k in range(PEER_E):
                row0 = offs.at[pl.ds(k * TC_SUB, TC_SUB)][i]
                stack_ref[buf, ROW_WORDS * k:ROW_WORDS * (k + 1), :] = (
                    tab_v[pl.ds(pl.multiple_of(row0, ROW_WORDS), ROW_WORDS), :])

        for g in range(TC_GROUP):
            copy_rows(g, g)

        def trip(ip, carry):
            i0 = ip * 2 * TC_GROUP
            for half in range(2):
                cur = half * TC_GROUP
                nxt = (1 - half) * TC_GROUP
                for g in range(TC_GROUP):
                    copy_rows(jnp.minimum(i0 + cur + TC_GROUP + g, TC_SUB - 1), nxt + g)
                for g in range(TC_GROUP):
                    compute(sub * TC_SUB + i0 + cur + g, cur + g)
            return carry
        lax.fori_loop(0, TC_SUB // (2 * TC_GROUP), trip, 0)


def _tc_peer_scratch(tabp):
    return [pltpu.VMEM(tabp.shape, jnp.uint32),
            pltpu.VMEM((2 * TC_GROUP, PEER_E * ROW_WORDS, LANES), jnp.uint32),
            [pltpu.SMEM((TC_SUB * PEER_E,), jnp.int32) for _ in range(TC_SUBS)],
            pltpu.SemaphoreType.DMA,
            pltpu.SemaphoreType.DMA((TC_SUBS,))]


def _stack_selectors():
    col = np.arange(STACK_COLS)
    diag = (col[None, :] % ROW_HALVES == np.arange(ROW_HALVES)[:, None]).astype(np.float32)
    owner = (col[:, None] // ROW_HALVES == np.arange(PEER_E)[None, :]).astype(np.float32)
    return jnp.asarray(diag), jnp.asarray(owner)


def _tc_down_kernel(idx_hbm, xw_ref, tab_hbm, diag_ref, fold_ref, o_ref,
                    tab_v, stack_ref, idx_s, sem, isem, z_ref):
    _load_table_once(tab_hbm, tab_v, sem)

    def compute(t, buf):
        stack = pltpu.bitcast(stack_ref[buf], BF16)
        x8 = pltpu.bitcast(xw_ref[t], BF16)
        a = lax.dot_general(x8, stack, (((1,), (1,)), ((), ())), preferred_element_type=F32)
        z_ref[pl.ds(t, 1), :] = jnp.sum(a * diag_ref[...], axis=0, keepdims=True)

    _tc_token_pipeline(idx_hbm, idx_s, isem, tab_v, stack_ref, compute)
    o_ref[...] = _dot_exact(z_ref[...], fold_ref[...])


def _peer_down_dots_tc(xn, idx, down, n_tok):
    tabp = _pack_rows_bf16(down)
    xw = _pack_rows_bf16(xn[:n_tok]).reshape(n_tok, ROW_WORDS, LANES)
    diag, owner = _stack_selectors()
    full = lambda a: pl.BlockSpec(a.shape, lambda i: (0,) * a.ndim)
    return pl.pallas_call(
        _tc_down_kernel,
        grid=(n_tok // TC_STEP,),
        in_specs=[pl.BlockSpec(memory_space=pl.ANY),
                  pl.BlockSpec((TC_STEP, ROW_WORDS, LANES), lambda i: (i, 0, 0)),
                  pl.BlockSpec(memory_space=pl.ANY), full(diag), full(owner)],
        out_specs=pl.BlockSpec((TC_STEP, PEER_E), lambda i: (i, 0)),
        out_shape=jax.ShapeDtypeStruct((n_tok, PEER_E), F32),
        scratch_shapes=_tc_peer_scratch(tabp) + [pltpu.VMEM((TC_STEP, STACK_COLS), F32)],
        compiler_params=pltpu.CompilerParams(
            dimension_semantics=("arbitrary",), vmem_limit_bytes=VMEM_LIMIT),
        name="peer_down_dots_tc",
    )(_block_major_row_offsets(idx, n_tok), xw, tabp, diag, owner)


def _tc_up_kernel(idx_hbm, coef_ref, tab_hbm, spread_ref, diag_ref, o_ref,
                  tab_v, stack_ref, idx_s, sem, isem, ce_ref):
    _load_table_once(tab_hbm, tab_v, sem)
    c = coef_ref[...]
    c_hi = c.astype(BF16)
    c_lo = (c - c_hi.astype(F32)).astype(BF16)
    ce_ref[0] = jnp.dot(c_hi, spread_ref[...], preferred_element_type=F32)
    ce_ref[1] = jnp.dot(c_lo, spread_ref[...], preferred_element_type=F32)

    def compute(t, buf):
        stack = pltpu.bitcast(stack_ref[buf], BF16)
        d = diag_ref[...]
        lhs = jnp.concatenate([(ce_ref[0, pl.ds(t, 1), :] * d).astype(BF16),
                               (ce_ref[1, pl.ds(t, 1), :] * d).astype(BF16)], axis=0)
        r = jnp.dot(lhs, stack, preferred_element_type=F32)
        o_ref[t] = r[0:ROW_HALVES] + r[ROW_HALVES:2 * ROW_HALVES]

    _tc_token_pipeline(idx_hbm, idx_s, isem, tab_v, stack_ref, compute)


def _peer_up_sum_tc(coef, idx, up, n_tok):
    d = up.shape[1]
    tabp = _pack_rows_bf16(up)
    diag, owner = _stack_selectors()
    spread = owner.T.astype(BF16)
    full = lambda a: pl.BlockSpec(a.shape, lambda i: (0,) * a.ndim)
    out = pl.pallas_call(
        _tc_up_kernel,
        grid=(n_tok // TC_STEP,),
        in_specs=[pl.BlockSpec(memory_space=pl.ANY),
                  pl.BlockSpec((TC_STEP, PEER_E), lambda i: (i, 0)),
                  pl.BlockSpec(memory_space=pl.ANY), full(spread), full(diag)],
        out_specs=pl.BlockSpec((TC_STEP, ROW_HALVES, LANES), lambda i: (i, 0, 0)),
        out_shape=jax.ShapeDtypeStruct((n_tok, ROW_HALVES, LANES), F32),
        scratch_shapes=_tc_peer_scratch(tabp) + [pltpu.VMEM((2, TC_STEP, STACK_COLS), F32)],
        compiler_params=pltpu.CompilerParams(
            dimension_semantics=("arbitrary",), vmem_limit_bytes=VMEM_LIMIT),
        name="peer_up_sum_tc",
    )(_block_major_row_offsets(idx, n_tok), coef, tabp, spread, diag)
    return out.reshape(n_tok, d)


def _coef_kernel(pre_ref, gate_ref, o_ref):
    x = pre_ref[...]
    gelu = 0.5 * x * (1.0 + lax.erf(x * (2.0 ** -0.5)))
    o_ref[...] = gate_ref[...] * gelu


def _peer_coef(pre, gates):
    t, e = pre.shape
    row = pl.BlockSpec((TM_FINAL, e), lambda i: (i, 0))
    return pl.pallas_call(
        _coef_kernel,
        grid=(t // TM_FINAL,),
        in_specs=[row, row],
        out_specs=row,
        out_shape=jax.ShapeDtypeStruct((t, e), F32),
        compiler_params=pltpu.CompilerParams(dimension_semantics=("arbitrary",)),
        name="peer_gate_gelu",
    )(pre, gates)


def _peer_experts(xn, expert, gates, down, up):
    pre_sc = _peer_down_dots(xn, expert, down, PEER_TC_DOWN_TOKENS)
    pre_tc = _peer_down_dots_tc(xn, expert, down, PEER_TC_DOWN_TOKENS)
    coef = _peer_coef(jnp.concatenate([pre_tc, pre_sc], axis=0), gates)
    p_sc = _peer_up_sum(coef, expert, up, PEER_TC_UP_TOKENS)
    p_tc = _peer_up_sum_tc(coef, expert, up, PEER_TC_UP_TOKENS)
    return jnp.concatenate([p_tc, p_sc], axis=0)


def kernel(x, norm1_g, w_in, ret_norm_g, m_conv_w, m_conv_b, m_wq, m_wk, m_b_i, m_b_f,
           m_norm_g, m_skip, w_out, norm2_g, peer_wq, peer_keys, peer_down, peer_up, final_g):
    b, s, d = x.shape
    t = b * s
    x2 = x.reshape(t, d)
    depth = norm1_g.shape[0]
    for i in range(depth):
        if i > 0:
            x2 = x1 + p
        wi = w_in[i]
        n_main = OFF_GI
        gate_pad = lambda w: jnp.pad(w, ((0, 0), (0, LANES - w.shape[1])))
        w_pad = jnp.concatenate(
            [wi[:, :n_main], gate_pad(wi[:, n_main:n_main + MLSTM_HEADS]),
             gate_pad(wi[:, n_main + MLSTM_HEADS:])], axis=1).astype(BF16)
        proj = _rms_proj(x2, norm1_g[i].reshape(1, d), w_pad)
        y = _mixer(proj, b, s, ret_norm_g[i], m_conv_w[i], m_conv_b[i], m_wq[i], m_wk[i],
                   m_b_i[i], m_b_f[i], m_norm_g[i], m_skip[i])
        keys = peer_keys[i].reshape(2 * PEER_HEADS, PEER_NKEYS, PEER_HALF).astype(BF16)
        x1, xn, sub_t = _post(y, x2, w_out[i].astype(BF16), norm2_g[i].reshape(1, d),
                              peer_wq[i].astype(BF16), keys)
        expert, gates = _peer_route(sub_t)
        p = _peer_experts(xn, expert, gates, peer_down[i], peer_up[i])
    out = _final(x1, p, final_g.reshape(1, d))
    return out.reshape(b, s, d)
```

```python
import functools

import numpy as np
import jax
import jax.numpy as jnp
from jax import lax
from jax.experimental import pallas as pl
from jax.experimental.pallas import tpu as pltpu
from jax.experimental.pallas import tpu_sc as plsc

F32 = jnp.float32
BF16 = jnp.bfloat16

D_MODEL = 1024
CHUNK = 64
RET_HEADS = 4
RET_DK = 128
MLSTM_HEADS = 4
MLSTM_D = 128
HEAD_W = 128
D_GROUP = 512
CONV_W = 4
ROPE_BASE = 10000.0
PEER_HEADS = 8
PEER_NKEYS = 128
PEER_TOPK = 16
PEER_HALF = 128
PEER_BLOCK = 128
EPS = 1e-6

OFF_RQ, OFF_RK, OFF_RV, OFF_RG = 0, 512, 1024, 1536
OFF_MX, OFF_MV, OFF_MO = 2048, 2560, 3072
OFF_GI, OFF_GF = 3584, 3712
D_PROJ_PAD = 3840

LANES = 128
CONV_TAIL = 8
VMEM_LIMIT = 56 * 1024 * 1024

TM_PROJ = 256
L_BLOCK = 256
TM_FINAL = 1024


def _rms(x, g):
    ms = jnp.mean(x * x, axis=-1, keepdims=True)
    return x * lax.rsqrt(ms + EPS) * g


def _group_norm(h, g):
    mu = jnp.mean(h, axis=-1, keepdims=True)
    d = h - mu
    var = jnp.mean(d * d, axis=-1, keepdims=True)
    return d * lax.rsqrt(var + EPS) * g


def _silu(x):
    return x * (1.0 / (1.0 + jnp.exp(-x)))


def _sigmoid(x):
    return 1.0 / (1.0 + jnp.exp(-x))


def _dot(a, b):
    return jnp.dot(a.astype(BF16), b.astype(BF16), preferred_element_type=F32)


def _dot_tn(a, b):
    return lax.dot_general(a.astype(BF16), b.astype(BF16), (((0,), (0,)), ((), ())),
                           preferred_element_type=F32)


def _dot_nt(a, b):
    return lax.dot_general(a.astype(BF16), b.astype(BF16), (((1,), (1,)), ((), ())),
                           preferred_element_type=F32)


def _rms_proj_kernel(x_ref, g_ref, w_ref, o_ref):
    h = _rms(x_ref[...], g_ref[...])
    o_ref[...] = jnp.dot(h.astype(BF16), w_ref[...], preferred_element_type=F32)


def _rms_proj(x2, g, w):
    t, d = x2.shape
    n = w.shape[1]
    return pl.pallas_call(
        _rms_proj_kernel,
        grid=(t // TM_PROJ,),
        in_specs=[
            pl.BlockSpec((TM_PROJ, d), lambda i: (i, 0)),
            pl.BlockSpec((1, d), lambda i: (0, 0)),
            pl.BlockSpec((d, n), lambda i: (0, 0)),
        ],
        out_specs=pl.BlockSpec((TM_PROJ, n), lambda i: (i, 0)),
        out_shape=jax.ShapeDtypeStruct((t, n), F32),
        compiler_params=pltpu.CompilerParams(
            dimension_semantics=("arbitrary",), vmem_limit_bytes=VMEM_LIMIT),
        name="rms_in_proj",
    )(x2, g, w)


def _mixer_kernel(chunk_decay,
                  proj_ref, cos_ref, sin_ref, intra_ref, qd_ref, kd_ref, tri_ref,
                  rng_ref, cw_ref, cb_ref, wq_ref, wk_ref, bi_ref, bf_ref,
                  mng_ref, skip_ref,
                  y_ref,
                  rstate_ref, cstate_ref, m_ref, tail_ref):
    j = pl.program_id(1)
    n_chunks = L_BLOCK // CHUNK

    @pl.when(j == 0)
    def _():
        rstate_ref[...] = jnp.zeros_like(rstate_ref)
        cstate_ref[...] = jnp.zeros_like(cstate_ref)
        m_ref[...] = jnp.zeros_like(m_ref)
        tail_ref[...] = jnp.zeros_like(tail_ref)

    cosv = cos_ref[...]
    sinv = sin_ref[...]
    k_scale = RET_DK ** -0.5

    for h in range(RET_HEADS):
        lo = h * HEAD_W
        q = proj_ref[:, OFF_RQ + lo:OFF_RQ + lo + HEAD_W]
        k = proj_ref[:, OFF_RK + lo:OFF_RK + lo + HEAD_W]
        q = q * cosv + pltpu.roll(q, HEAD_W // 2, 1) * sinv
        k = (k * cosv + pltpu.roll(k, HEAD_W // 2, 1) * sinv) * k_scale
        intra = intra_ref[h]
        qd = qd_ref[h]
        kd = kd_ref[h]
        g = rng_ref[:, lo:lo + HEAD_W]
        for c in range(n_chunks):
            r0 = c * CHUNK
            qc = q[r0:r0 + CHUNK]
            kc = k[r0:r0 + CHUNK]
            vc = proj_ref[r0:r0 + CHUNK, OFF_RV + lo:OFF_RV + lo + HEAD_W]
            gate = proj_ref[r0:r0 + CHUNK, OFF_RG + lo:OFF_RG + lo + HEAD_W]
            state = rstate_ref[h]
            scores = _dot_nt(qc, kc) * intra
            o = _dot(scores, vc) + _dot(qc, state) * qd
            rstate_ref[h] = chunk_decay[h] * state + _dot_tn(kc * kd, vc)
            y_ref[r0:r0 + CHUNK, lo:lo + HEAD_W] = _silu(gate) * _group_norm(o, g)

    mx = proj_ref[:, OFF_MX:OFF_MX + D_GROUP]
    xp = jnp.concatenate([tail_ref[...], mx], axis=0)
    base = CONV_TAIL - (CONV_W - 1)
    conv = cb_ref[...] + xp[base:base + L_BLOCK] * cw_ref[0:1, :]
    for t in range(1, CONV_W):
        conv = conv + xp[base + t:base + t + L_BLOCK] * cw_ref[t:t + 1, :]
    tail_ref[...] = mx[L_BLOCK - CONV_TAIL:L_BLOCK]
    cact = _silu(conv)

    gi = proj_ref[:, OFF_GI:OFF_GI + LANES] + bi_ref[...]
    gf = proj_ref[:, OFF_GF:OFF_GF + LANES] + bf_ref[...]
    logf = jnp.minimum(gf, 0.0) - jnp.log1p(jnp.exp(-jnp.abs(gf)))
    tri = tri_ref[...]
    lane = lax.broadcasted_iota(jnp.int32, (CHUNK, HEAD_W), 1)
    ones_col = (lane == 0).astype(F32)
    m_scale = MLSTM_D ** -0.5

    mq = []
    mk = []
    for h in range(MLSTM_HEADS):
        lo = h * HEAD_W
        ch = cact[:, lo:lo + HEAD_W]
        mq.append(_dot(ch, wq_ref[h]))
        mk.append(_dot(ch, wk_ref[h]) * m_scale)

    for c in range(n_chunks):
        r0 = c * CHUNK
        fcum = jnp.dot(tri, logf[r0:r0 + CHUNK], preferred_element_type=F32,
                       precision=lax.Precision.HIGHEST)
        a = fcum[CHUNK - 1:CHUNK]
        logw = a - fcum + gi[r0:r0 + CHUNK]
        m_in = jnp.max(logw, axis=0, keepdims=True)
        m_old = m_ref[...]
        m_new = jnp.maximum(a + m_old, m_in)
        decay = jnp.exp(a + m_old - m_new)
        w = jnp.exp(logw - m_new)
        em = jnp.exp(-m_new)
        m_ref[...] = m_new
        for h in range(MLSTM_HEADS):
            lo = h * HEAD_W
            qh = mq[h][r0:r0 + CHUNK]
            kh = mk[h][r0:r0 + CHUNK]
            vh = proj_ref[r0:r0 + CHUNK, OFF_MV + lo:OFF_MV + lo + HEAD_W]
            oh = proj_ref[r0:r0 + CHUNK, OFF_MO + lo:OFF_MO + lo + HEAD_W]
            ch = cact[r0:r0 + CHUNK, lo:lo + HEAD_W]
            v_ext = jnp.concatenate([vh, ones_col], axis=1)
            cmem = decay[:, h:h + 1] * cstate_ref[h] + _dot_tn(kh * w[:, h:h + 1], v_ext)
            cstate_ref[h] = cmem
            num = _dot(qh, cmem)
            den = jnp.maximum(jnp.abs(num[:, HEAD_W:HEAD_W + 1]), em[:, h:h + 1])
            hh = num[:, :HEAD_W] / den
            hm = _group_norm(hh, mng_ref[:, lo:lo + HEAD_W])
            ym = _sigmoid(oh) * (hm + skip_ref[:, lo:lo + HEAD_W] * ch)
            y_ref[r0:r0 + CHUNK, D_GROUP + lo:D_GROUP + lo + HEAD_W] = ym


def _retention_tables():
    h = np.arange(RET_HEADS, dtype=np.float64)
    log_g = np.log(1.0 - 2.0 ** (-5.0 - h))
    l = np.arange(CHUNK, dtype=np.float64)
    intra = np.exp(log_g[:, None, None] * np.abs(l[:, None] - l[None, :]))
    qd = np.exp(log_g[:, None] * (l + 1.0))[:, :, None] * np.ones((1, 1, HEAD_W))
    kd = np.exp(log_g[:, None] * (CHUNK - 1.0 - l))[:, :, None] * np.ones((1, 1, HEAD_W))
    cd = np.exp(log_g * CHUNK)
    return (jnp.asarray(intra, F32), jnp.asarray(qd, F32), jnp.asarray(kd, F32),
            tuple(float(v) for v in cd))


def _rotary_tables(s):
    half = HEAD_W // 2
    inv = ROPE_BASE ** (-np.arange(half, dtype=np.float64) / half)
    ang = np.arange(s, dtype=np.float64)[:, None] * inv[None, :]
    cos = np.cos(ang)
    sin = np.sin(ang)
    return (jnp.asarray(np.concatenate([cos, cos], axis=-1), F32),
            jnp.asarray(np.concatenate([-sin, sin], axis=-1), F32))


def _mixer(proj, b, s, ret_norm_g, m_conv_w, m_conv_b, m_wq, m_wk, m_b_i, m_b_f,
           m_norm_g, m_skip):
    intra, qd, kd, cd = _retention_tables()
    cos_t, sin_t = _rotary_tables(s)
    tri = jnp.asarray(np.tril(np.ones((CHUNK, CHUNK))), F32)
    nj = s // L_BLOCK
    pad = lambda v: jnp.pad(v.reshape(1, -1), ((0, 0), (0, LANES - v.size)))
    full = lambda shape: pl.BlockSpec(shape, lambda bi, j: (0,) * len(shape))
    return pl.pallas_call(
        functools.partial(_mixer_kernel, cd),
        grid=(b, nj),
        in_specs=[
            pl.BlockSpec((L_BLOCK, D_PROJ_PAD), lambda bi, j: (bi * nj + j, 0)),
            pl.BlockSpec((L_BLOCK, HEAD_W), lambda bi, j: (j, 0)),
            pl.BlockSpec((L_BLOCK, HEAD_W), lambda bi, j: (j, 0)),
            full((RET_HEADS, CHUNK, CHUNK)),
            full((RET_HEADS, CHUNK, HEAD_W)),
            full((RET_HEADS, CHUNK, HEAD_W)),
            full((CHUNK, CHUNK)),
            full((1, D_GROUP)),
            full((CONV_W, D_GROUP)),
            full((1, D_GROUP)),
            full((MLSTM_HEADS, MLSTM_D, MLSTM_D)),
            full((MLSTM_HEADS, MLSTM_D, MLSTM_D)),
            full((1, LANES)),
            full((1, LANES)),
            full((1, D_GROUP)),
            full((1, D_GROUP)),
        ],
        out_specs=pl.BlockSpec((L_BLOCK, 2 * D_GROUP), lambda bi, j: (bi * nj + j, 0)),
        out_shape=jax.ShapeDtypeStruct((b * s, 2 * D_GROUP), F32),
        scratch_shapes=[
            pltpu.VMEM((RET_HEADS, RET_DK, HEAD_W), F32),
            pltpu.VMEM((MLSTM_HEADS, MLSTM_D, 2 * HEAD_W), F32),
            pltpu.VMEM((1, LANES), F32),
            pltpu.VMEM((CONV_TAIL, D_GROUP), F32),
        ],
        compiler_params=pltpu.CompilerParams(
            dimension_semantics=("arbitrary", "arbitrary"), vmem_limit_bytes=VMEM_LIMIT),
        name="retention_mlstm_mixer",
    )(proj, cos_t, sin_t, intra, qd, kd, tri,
      ret_norm_g.reshape(1, D_GROUP), m_conv_w, m_conv_b.reshape(1, D_GROUP),
      m_wq.astype(BF16), m_wk.astype(BF16), pad(m_b_i), pad(m_b_f),
      m_norm_g.reshape(1, D_GROUP), m_skip.reshape(1, D_GROUP))


def _post_kernel(y_ref, x_ref, wout_ref, g2_ref, wq_ref, keys_ref,
                 x1_ref, xn_ref, sub_ref):
    x1 = x_ref[...] + jnp.dot(y_ref[...].astype(BF16), wout_ref[...],
                              preferred_element_type=F32)
    x1_ref[...] = x1
    xn = _rms(x1, g2_ref[...])
    xn_ref[...] = xn
    q = jnp.dot(xn.astype(BF16), wq_ref[...], preferred_element_type=F32)
    for i in range(2 * PEER_HEADS):
        lo = i * PEER_HALF
        sub_ref[lo:lo + PEER_NKEYS, :] = _dot_nt(keys_ref[i], q[:, lo:lo + PEER_HALF])


def _post(y, x2, w_out, norm2_g, peer_wq, keys):
    t, d = x2.shape
    nq = peer_wq.shape[1]
    row = lambda w: pl.BlockSpec((TM_PROJ, w), lambda i: (i, 0))
    full = lambda shape: pl.BlockSpec(shape, lambda i: (0,) * len(shape))
    return pl.pallas_call(
        _post_kernel,
        grid=(t // TM_PROJ,),
        in_specs=[row(d), row(d), full((d, d)), full((1, d)), full((d, nq)),
                  full(keys.shape)],
        out_specs=[row(d), row(d), pl.BlockSpec((nq, TM_PROJ), lambda i: (0, i))],
        out_shape=[jax.ShapeDtypeStruct((t, d), F32), jax.ShapeDtypeStruct((t, d), F32),
                   jax.ShapeDtypeStruct((nq, t), F32)],
        compiler_params=pltpu.CompilerParams(
            dimension_semantics=("arbitrary",), vmem_limit_bytes=VMEM_LIMIT),
        name="out_proj_peer_scores",
    )(y, x2, w_out, norm2_g, peer_wq, keys)


def _final_kernel(x1_ref, p_ref, g_ref, o_ref):
    o_ref[...] = _rms(x1_ref[...] + p_ref[...], g_ref[...])


def _final(x1, p, g):
    t, d = x1.shape
    row = pl.BlockSpec((TM_FINAL, d), lambda i: (i, 0))
    return pl.pallas_call(
        _final_kernel,
        grid=(t // TM_FINAL,),
        in_specs=[row, row, pl.BlockSpec((1, d), lambda i: (0, 0))],
        out_specs=row,
        out_shape=jax.ShapeDtypeStruct((t, d), F32),
        compiler_params=pltpu.CompilerParams(dimension_semantics=("arbitrary",)),
        name="final_rmsnorm",
    )(x1, p, g)


TM_ROUTE = 128
CAND_ROWS = 56


def _candidate_tables():
    pairs = [(k1, k2) for k1 in range(PEER_TOPK) for k2 in range(PEER_TOPK)
             if (k1 + 1) * (k2 + 1) <= PEER_TOPK]
    assert len(pairs) <= CAND_ROWS
    rep = np.zeros((CAND_ROWS, 2 * PEER_TOPK), np.float32)
    til = np.zeros((CAND_ROWS, 2 * PEER_TOPK), np.float32)
    for pos, (k1, k2) in enumerate(pairs):
        rep[pos, k1] = 1.0
        til[pos, PEER_TOPK + k2] = 1.0
    return jnp.asarray(rep), jnp.asarray(til), len(pairs)


def _dot_exact(a, b):
    return jnp.dot(a, b, preferred_element_type=F32, precision=lax.Precision.HIGHEST)


def _route_kernel(n_cand, sub_ref, rep_ref, til_ref, seg_ref, exp_ref, gate_ref, offs_ref,
                  s_ref, i_ref, best_ref, eid_ref, m0_ref):
    tm = sub_ref.shape[1]
    neg = -jnp.inf
    key_id = lax.broadcasted_iota(jnp.int32, (PEER_NKEYS, tm), 0).astype(F32)
    cand_id = lax.broadcasted_iota(jnp.int32, (CAND_ROWS, tm), 0).astype(F32)
    rep = rep_ref[...]
    til = til_ref[...]
    for h in range(PEER_HEADS):
        for p in range(2):
            lo = (2 * h + p) * PEER_NKEYS
            x = sub_ref[lo:lo + PEER_NKEYS, :]
            for k in range(PEER_TOPK):
                m = jnp.max(x, axis=0, keepdims=True)
                idx = jnp.min(jnp.where(x == m, key_id, float(PEER_NKEYS)), axis=0, keepdims=True)
                x = jnp.where(key_id == idx, neg, x)
                r = p * PEER_TOPK + k
                s_ref[r:r + 1, :] = m
                i_ref[r:r + 1, :] = idx
        s = s_ref[...]
        iv = i_ref[...]
        cand = _dot_exact(rep, s) + _dot_exact(til, s)
        cand = jnp.where(cand_id < float(n_cand), cand, neg)
        eid = _dot(rep, iv) * float(PEER_NKEYS) + _dot(til, iv)
        for k in range(PEER_TOPK):
            m = jnp.max(cand, axis=0, keepdims=True)
            pos = jnp.min(jnp.where(cand == m, cand_id, float(CAND_ROWS)), axis=0, keepdims=True)
            sel = cand_id == pos
            e = jnp.max(jnp.where(sel, eid, -1.0), axis=0, keepdims=True)
            cand = jnp.where(sel, neg, cand)
            r = h * PEER_TOPK + k
            best_ref[r:r + 1, :] = m
            eid_ref[r:r + 1, :] = e
            if k == 0:
                m0_ref[r:r + PEER_TOPK, :] = jnp.broadcast_to(m, (PEER_TOPK, tm))
    pexp = jnp.exp(best_ref[...] - m0_ref[...])
    gates = pexp / _dot_exact(seg_ref[...], pexp)
    gate_ref[...] = gates.T
    eid = eid_ref[...].astype(jnp.int32)
    exp_ref[...] = eid.T
    offs_ref[0] = eid * ROW_WORDS


def _peer_route(sub_t):
    nq, t = sub_t.shape
    rep, til, n_cand = _candidate_tables()
    e = PEER_HEADS * PEER_TOPK
    seg = jnp.asarray(np.kron(np.eye(PEER_HEADS), np.ones((PEER_TOPK, PEER_TOPK))), F32)
    full = lambda a: pl.BlockSpec(a.shape, lambda i: (0,) * a.ndim)
    out = pl.BlockSpec((TM_ROUTE, e), lambda i: (i, 0))
    return pl.pallas_call(
        functools.partial(_route_kernel, n_cand),
        grid=(t // TM_ROUTE,),
        in_specs=[pl.BlockSpec((nq, TM_ROUTE), lambda i: (0, i)), full(rep), full(til), full(seg)],
        out_specs=[out, out, pl.BlockSpec((1, e, TM_ROUTE), lambda i: (i, 0, 0))],
        out_shape=[jax.ShapeDtypeStruct((t, e), jnp.int32), jax.ShapeDtypeStruct((t, e), F32),
                   jax.ShapeDtypeStruct((t // TM_ROUTE, e, TM_ROUTE), jnp.int32)],
        scratch_shapes=[
            pltpu.VMEM((2 * PEER_TOPK, TM_ROUTE), F32),
            pltpu.VMEM((2 * PEER_TOPK, TM_ROUTE), F32),
            pltpu.VMEM((e, TM_ROUTE), F32),
            pltpu.VMEM((e, TM_ROUTE), F32),
            pltpu.VMEM((e, TM_ROUTE), F32),
        ],
        compiler_params=pltpu.CompilerParams(dimension_semantics=("arbitrary",)),
        name="peer_topk_route",
    )(sub_t, rep, til, seg)


SC_CORES = 2
SC_SUBCORES = 16
SC_LANES = 16
SC_WORKERS = SC_CORES * SC_SUBCORES
PEER_E = PEER_HEADS * PEER_TOPK
SC_ROWS = 32
SC_GATHERS = PEER_E // SC_ROWS
SC_TOK_BLOCK = 8
SC_QUARTER = 256


def _sc_mesh():
    return plsc.VectorSubcoreMesh(core_axis_name="c", subcore_axis_name="s")


def _sc_pipeline(idx_v, tab_hbm, rows_v, sems, compute):
    def gather(step, buf):
        return pltpu.make_async_copy(tab_hbm.at[idx_v.at[step]], rows_v.at[buf], sems.at[buf])

    gather(0, 0).start()

    @pl.loop(0, SC_TOK_BLOCK)
    def _(i):
        for c in range(SC_GATHERS):
            buf = c % 2
            step = i * SC_GATHERS + c
            gather(step, buf).wait()
            if c + 1 < SC_GATHERS:
                gather(step + 1, 1 - buf).start()
            else:
                @pl.when(i + 1 < SC_TOK_BLOCK)
                def _():
                    gather(step + 1, 1 - buf).start()
            compute(i, c, step, buf)


def _peer_down_dots(xn, idx, down, tok_start):
    t_all, d = xn.shape
    t = t_all - tok_start
    tok_w = t // SC_WORKERS
    nblk = tok_w // SC_TOK_BLOCK
    assert nblk * SC_TOK_BLOCK * SC_WORKERS == t
    lanes = SC_LANES

    @functools.partial(
        pl.kernel, mesh=_sc_mesh(),
        out_type=jax.ShapeDtypeStruct((t, PEER_E), F32),
        scratch_types=[
            pltpu.VMEM((SC_TOK_BLOCK * SC_GATHERS, SC_ROWS), jnp.int32),
            pltpu.VMEM((SC_TOK_BLOCK, d), F32),
            pltpu.VMEM((2, SC_ROWS, d), F32),
            pltpu.VMEM((SC_ROWS, lanes), F32),
            pltpu.VMEM((SC_TOK_BLOCK, PEER_E), F32),
            pltpu.SemaphoreType.DMA((2,)),
        ],
        compiler_params=pltpu.CompilerParams(needs_layout_passes=False),
        name="peer_down_dots",
    )
    def k(x_hbm, idx_hbm, tab_hbm, out_hbm, idx_v, x_v, rows_v, acc_v, out_v, sems):
        wid = lax.axis_index("s") * SC_CORES + lax.axis_index("c")
        lane = lax.iota(jnp.int32, lanes)

        @pl.loop(0, nblk)
        def _(blk):
            out0 = wid * tok_w + blk * SC_TOK_BLOCK
            tok0 = tok_start + out0
            pltpu.sync_copy(idx_hbm.at[pl.ds(tok0 * SC_GATHERS, SC_TOK_BLOCK * SC_GATHERS)], idx_v)
            pltpu.sync_copy(x_hbm.at[pl.ds(tok0, SC_TOK_BLOCK)], x_v)

            def compute(i, c, step, buf):
                def body(dc, accs):
                    xv = x_v[i, pl.ds(dc * lanes, lanes)]
                    return tuple(a + rows_v[buf, r, pl.ds(dc * lanes, lanes)] * xv
                                 for r, a in enumerate(accs))
                accs = lax.fori_loop(0, d // lanes, body,
                                     tuple(jnp.zeros((lanes,), F32) for _ in range(SC_ROWS)))
                for r in range(SC_ROWS):
                    acc_v[r, :] = accs[r]
                for g in range(SC_ROWS // lanes):
                    tot = jnp.zeros((lanes,), F32)
                    for l in range(lanes):
                        tot = tot + plsc.load_gather(
                            acc_v, [lane + g * lanes, jnp.full((lanes,), l, jnp.int32)])
                    out_v[i, pl.ds(c * SC_ROWS + g * lanes, lanes)] = tot

            _sc_pipeline(idx_v, tab_hbm, rows_v, sems, compute)
            pltpu.sync_copy(out_v, out_hbm.at[pl.ds(out0, SC_TOK_BLOCK)])

    return k(xn, idx.reshape(t_all * SC_GATHERS, SC_ROWS), down)


def _peer_up_sum(coef, idx, up, tok_start):
    t_all = coef.shape[0]
    t = t_all - tok_start
    d = up.shape[1]
    tok_w = t // SC_WORKERS
    nblk = tok_w // SC_TOK_BLOCK
    assert nblk * SC_TOK_BLOCK * SC_WORKERS == t
    lanes = SC_LANES
    nj = SC_QUARTER // lanes

    @functools.partial(
        pl.kernel, mesh=_sc_mesh(),
        out_type=jax.ShapeDtypeStruct((t, d), F32),
        scratch_types=[
            pltpu.VMEM((SC_TOK_BLOCK * SC_GATHERS, SC_ROWS), jnp.int32),
            pltpu.VMEM((SC_TOK_BLOCK * PEER_E,), F32),
            pltpu.VMEM((2, SC_ROWS, d), F32),
            pltpu.VMEM((SC_TOK_BLOCK, d), F32),
            pltpu.SemaphoreType.DMA((2,)),
        ],
        compiler_params=pltpu.CompilerParams(needs_layout_passes=False),
        name="peer_up_sum",
    )
    def k(coef_hbm, idx_hbm, tab_hbm, out_hbm, idx_v, coef_v, rows_v, out_v, sems):
        wid = lax.axis_index("s") * SC_CORES + lax.axis_index("c")

        @pl.loop(0, nblk)
        def _(blk):
            out0 = wid * tok_w + blk * SC_TOK_BLOCK
            tok0 = tok_start + out0
            pltpu.sync_copy(idx_hbm.at[pl.ds(tok0 * SC_GATHERS, SC_TOK_BLOCK * SC_GATHERS)], idx_v)
            pltpu.sync_copy(coef_hbm.at[pl.ds(tok0 * PEER_E, SC_TOK_BLOCK * PEER_E)], coef_v)

            def compute(i, c, step, buf):
                for q in range(d // SC_QUARTER):
                    col = lambda j: pl.ds(q * SC_QUARTER + j * lanes, lanes)
                    if c == 0:
                        init = tuple(jnp.zeros((lanes,), F32) for _ in range(nj))
                    else:
                        init = tuple(out_v[i, col(j)] for j in range(nj))

                    def body(r, accs):
                        cf = plsc.load_gather(
                            coef_v, [jnp.full((lanes,), step * SC_ROWS + r, jnp.int32)])
                        return tuple(a + rows_v[buf, r, col(j)] * cf for j, a in enumerate(accs))
                    accs = lax.fori_loop(0, SC_ROWS, body, init)
                    for j in range(nj):
                        out_v[i, col(j)] = accs[j]

            _sc_pipeline(idx_v, tab_hbm, rows_v, sems, compute)
            pltpu.sync_copy(out_v, out_hbm.at[pl.ds(out0, SC_TOK_BLOCK)])

    return k(coef.reshape(t_all * PEER_E), idx.reshape(t_all * SC_GATHERS, SC_ROWS), up)


TC_SUB = 128
TC_SUBS = 2
TC_STEP = TC_SUB * TC_SUBS
TC_GROUP = 4
ROW_WORDS = 4
ROW_HALVES = 2 * ROW_WORDS
STACK_COLS = PEER_E * ROW_HALVES
PEER_TC_DOWN_TOKENS = 22528
PEER_TC_UP_TOKENS = 22528


def _pack_rows_bf16(tab):
    n, d = tab.shape
    u = lax.bitcast_convert_type(tab, jnp.uint32)
    r = (u + jnp.uint32(0x7FFF) + ((u >> 16) & jnp.uint32(1))) >> 16
    r = r.reshape(n, ROW_WORDS, 2, LANES)
    return (r[:, :, 0, :] | (r[:, :, 1, :] << 16)).reshape(n * ROW_WORDS, LANES)


def _load_table_once(tab_hbm, tab_v, sem):
    @pl.when(pl.program_id(0) == 0)
    def _():
        cp = pltpu.make_async_copy(tab_hbm, tab_v, sem)
        cp.start()
        cp.wait()


def _tc_token_pipeline(idx_hbm, idx_s, isem, tab_v, stack_ref, compute):
    step = pl.program_id(0)

    def idx_copy(block, slot):
        return pltpu.make_async_copy(idx_hbm.at[block], idx_s[slot], isem.at[slot])

    @pl.when(step == 0)
    def _():
        idx_copy(0, 0).start()

    for sub in range(TC_SUBS):
        block = step * TC_SUBS + sub
        idx_copy(block, sub).wait()
        if sub + 1 < TC_SUBS:
            idx_copy(block + 1, sub + 1).start()
        else:
            @pl.when(step + 1 < pl.num_programs(0))
            def _():
                idx_copy(block + 1, 0).start()
        offs = idx_s[sub]

        def copy_rows(i, buf):
            for k in range(PEER_E):
                row0 = offs.at[k][i]
                stack_ref[buf, ROW_WORDS * k:ROW_WORDS * (k + 1), :] = (
                    tab_v[pl.ds(pl.multiple_of(row0, ROW_WORDS), ROW_WORDS), :])

        for g in range(TC_GROUP):
            copy_rows(g, g)

        def trip(ip, carry):
            i0 = ip * 2 * TC_GROUP
            for half in range(2):
                cur = half * TC_GROUP
                nxt = (1 - half) * TC_GROUP
                for g in range(TC_GROUP):
                    copy_rows(jnp.minimum(i0 + cur + TC_GROUP + g, TC_SUB - 1), nxt + g)
                for g in range(TC_GROUP):
                    compute(sub * TC_SUB + i0 + cur + g, cur + g)
            return carry
        lax.fori_loop(0, TC_SUB // (2 * TC_GROUP), trip, 0)


def _tc_peer_scratch(tabp):
    return [pltpu.VMEM(tabp.shape, jnp.uint32),
            pltpu.VMEM((2 * TC_GROUP, PEER_E * ROW_WORDS, LANES), jnp.uint32),
            [pltpu.SMEM((PEER_E, TC_SUB), jnp.int32) for _ in range(TC_SUBS)],
            pltpu.SemaphoreType.DMA,
            pltpu.SemaphoreType.DMA((TC_SUBS,))]


def _stack_selectors():
    col = np.arange(STACK_COLS)
    diag = (col[None, :] % ROW_HALVES == np.arange(ROW_HALVES)[:, None]).astype(np.float32)
    owner = (col[:, None] // ROW_HALVES == np.arange(PEER_E)[None, :]).astype(np.float32)
    return jnp.asarray(diag), jnp.asarray(owner)


def _tc_down_kernel(idx_hbm, x_ref, tab_hbm, diag_ref, fold_ref, o_ref,
                    tab_v, stack_ref, idx_s, sem, isem, z_ref):
    _load_table_once(tab_hbm, tab_v, sem)

    def compute(t, buf):
        stack = pltpu.bitcast(stack_ref[buf], BF16)
        xr = x_ref[pl.ds(t, 1), :]
        x8 = jnp.concatenate([xr[:, LANES * j:LANES * (j + 1)] for j in range(ROW_HALVES)],
                             axis=0).astype(BF16)
        a = lax.dot_general(x8, stack, (((1,), (1,)), ((), ())), preferred_element_type=F32)
        z_ref[pl.ds(t, 1), :] = jnp.sum(a * diag_ref[...], axis=0, keepdims=True)

    _tc_token_pipeline(idx_hbm, idx_s, isem, tab_v, stack_ref, compute)
    o_ref[...] = _dot_exact(z_ref[...], fold_ref[...])


def _peer_down_dots_tc(xn, offs, down, n_tok):
    tabp = _pack_rows_bf16(down)
    diag, owner = _stack_selectors()
    full = lambda a: pl.BlockSpec(a.shape, lambda i: (0,) * a.ndim)
    return pl.pallas_call(
        _tc_down_kernel,
        grid=(n_tok // TC_STEP,),
        in_specs=[pl.BlockSpec(memory_space=pl.ANY),
                  pl.BlockSpec((TC_STEP, xn.shape[1]), lambda i: (i, 0)),
                  pl.BlockSpec(memory_space=pl.ANY), full(diag), full(owner)],
        out_specs=pl.BlockSpec((TC_STEP, PEER_E), lambda i: (i, 0)),
        out_shape=jax.ShapeDtypeStruct((n_tok, PEER_E), F32),
        scratch_shapes=_tc_peer_scratch(tabp) + [pltpu.VMEM((TC_STEP, STACK_COLS), F32)],
        compiler_params=pltpu.CompilerParams(
            dimension_semantics=("arbitrary",), vmem_limit_bytes=VMEM_LIMIT),
        name="peer_down_dots_tc",
    )(offs, xn, tabp, diag, owner)


def _tc_up_kernel(idx_hbm, coef_ref, tab_hbm, spread_ref, diag_ref, o_ref,
                  tab_v, stack_ref, idx_s, sem, isem, ce_ref):
    _load_table_once(tab_hbm, tab_v, sem)
    c = coef_ref[...]
    c_hi = c.astype(BF16)
    c_lo = (c - c_hi.astype(F32)).astype(BF16)
    ce_ref[0] = jnp.dot(c_hi, spread_ref[...], preferred_element_type=F32)
    ce_ref[1] = jnp.dot(c_lo, spread_ref[...], preferred_element_type=F32)

    def compute(t, buf):
        stack = pltpu.bitcast(stack_ref[buf], BF16)
        d = diag_ref[...]
        lhs = jnp.concatenate([(ce_ref[0, pl.ds(t, 1), :] * d).astype(BF16),
                               (ce_ref[1, pl.ds(t, 1), :] * d).astype(BF16)], axis=0)
        r = jnp.dot(lhs, stack, preferred_element_type=F32)
        o_ref[t] = r[0:ROW_HALVES] + r[ROW_HALVES:2 * ROW_HALVES]

    _tc_token_pipeline(idx_hbm, idx_s, isem, tab_v, stack_ref, compute)


def _peer_up_sum_tc(coef, offs, up, n_tok):
    d = up.shape[1]
    tabp = _pack_rows_bf16(up)
    diag, owner = _stack_selectors()
    spread = owner.T.astype(BF16)
    full = lambda a: pl.BlockSpec(a.shape, lambda i: (0,) * a.ndim)
    out = pl.pallas_call(
        _tc_up_kernel,
        grid=(n_tok // TC_STEP,),
        in_specs=[pl.BlockSpec(memory_space=pl.ANY),
                  pl.BlockSpec((TC_STEP, PEER_E), lambda i: (i, 0)),
                  pl.BlockSpec(memory_space=pl.ANY), full(spread), full(diag)],
        out_specs=pl.BlockSpec((TC_STEP, ROW_HALVES, LANES), lambda i: (i, 0, 0)),
        out_shape=jax.ShapeDtypeStruct((n_tok, ROW_HALVES, LANES), F32),
        scratch_shapes=_tc_peer_scratch(tabp) + [pltpu.VMEM((2, TC_STEP, STACK_COLS), F32)],
        compiler_params=pltpu.CompilerParams(
            dimension_semantics=("arbitrary",), vmem_limit_bytes=VMEM_LIMIT),
        name="peer_up_sum_tc",
    )(offs, coef, tabp, spread, diag)
    return out.reshape(n_tok, d)


def _coef_kernel(pre_ref, gate_ref, o_ref):
    x = pre_ref[...]
    gelu = 0.5 * x * (1.0 + lax.erf(x * (2.0 ** -0.5)))
    o_ref[...] = gate_ref[...] * gelu


def _peer_coef(pre, gates):
    t, e = pre.shape
    row = pl.BlockSpec((TM_FINAL, e), lambda i: (i, 0))
    return pl.pallas_call(
        _coef_kernel,
        grid=(t // TM_FINAL,),
        in_specs=[row, row],
        out_specs=row,
        out_shape=jax.ShapeDtypeStruct((t, e), F32),
        compiler_params=pltpu.CompilerParams(dimension_semantics=("arbitrary",)),
        name="peer_gate_gelu",
    )(pre, gates)


def _peer_experts(xn, expert, gates, offs, down, up):
    pre_sc = _peer_down_dots(xn, expert, down, PEER_TC_DOWN_TOKENS)
    pre_tc = _peer_down_dots_tc(xn, offs, down, PEER_TC_DOWN_TOKENS)
    coef = _peer_coef(jnp.concatenate([pre_tc, pre_sc], axis=0), gates)
    p_sc = _peer_up_sum(coef, expert, up, PEER_TC_UP_TOKENS)
    p_tc = _peer_up_sum_tc(coef, offs, up, PEER_TC_UP_TOKENS)
    return jnp.concatenate([p_tc, p_sc], axis=0)


def kernel(x, norm1_g, w_in, ret_norm_g, m_conv_w, m_conv_b, m_wq, m_wk, m_b_i, m_b_f,
           m_norm_g, m_skip, w_out, norm2_g, peer_wq, peer_keys, peer_down, peer_up, final_g):
    b, s, d = x.shape
    t = b * s
    x2 = x.reshape(t, d)
    depth = norm1_g.shape[0]
    for i in range(depth):
        if i > 0:
            x2 = x1 + p
        wi = w_in[i]
        n_main = OFF_GI
        gate_pad = lambda w: jnp.pad(w, ((0, 0), (0, LANES - w.shape[1])))
        w_pad = jnp.concatenate(
            [wi[:, :n_main], gate_pad(wi[:, n_main:n_main + MLSTM_HEADS]),
             gate_pad(wi[:, n_main + MLSTM_HEADS:])], axis=1).astype(BF16)
        proj = _rms_proj(x2, norm1_g[i].reshape(1, d), w_pad)
        y = _mixer(proj, b, s, ret_norm_g[i], m_conv_w[i], m_conv_b[i], m_wq[i], m_wk[i],
                   m_b_i[i], m_b_f[i], m_norm_g[i], m_skip[i])
        keys = peer_keys[i].reshape(2 * PEER_HEADS, PEER_NKEYS, PEER_HALF).astype(BF16)
        x1, xn, sub_t = _post(y, x2, w_out[i].astype(BF16), norm2_g[i].reshape(1, d),
                              peer_wq[i].astype(BF16), keys)
        expert, gates, offs = _peer_route(sub_t)
        p = _peer_experts(xn, expert, gates, offs, peer_down[i], peer_up[i])
    out = _final(x1, p, final_g.reshape(1, d))
    return out.reshape(b, s, d)
```

```python
import functools

import numpy as np
import jax
import jax.numpy as jnp
from jax import lax
from jax.experimental import pallas as pl
from jax.experimental.pallas import tpu as pltpu
from jax.experimental.pallas import tpu_sc as plsc

F32 = jnp.float32
BF16 = jnp.bfloat16

D_MODEL = 1024
CHUNK = 64
RET_HEADS = 4
RET_DK = 128
MLSTM_HEADS = 4
MLSTM_D = 128
HEAD_W = 128
D_GROUP = 512
CONV_W = 4
ROPE_BASE = 10000.0
PEER_HEADS = 8
PEER_NKEYS = 128
PEER_TOPK = 16
PEER_HALF = 128
PEER_BLOCK = 128
EPS = 1e-6

OFF_RQ, OFF_RK, OFF_RV, OFF_RG = 0, 512, 1024, 1536
OFF_MX, OFF_MV, OFF_MO = 2048, 2560, 3072
OFF_GI, OFF_GF = 3584, 3712
D_PROJ_PAD = 3840

LANES = 128
CONV_TAIL = 8
VMEM_LIMIT = 56 * 1024 * 1024

TM_PROJ = 256
L_BLOCK = 256
TM_FINAL = 1024


def _rms(x, g):
    ms = jnp.mean(x * x, axis=-1, keepdims=True)
    return x * lax.rsqrt(ms + EPS) * g


def _group_norm(h, g):
    mu = jnp.mean(h, axis=-1, keepdims=True)
    d = h - mu
    var = jnp.mean(d * d, axis=-1, keepdims=True)
    return d * lax.rsqrt(var + EPS) * g


def _silu(x):
    return x * (1.0 / (1.0 + jnp.exp(-x)))


def _sigmoid(x):
    return 1.0 / (1.0 + jnp.exp(-x))


def _dot(a, b):
    return jnp.dot(a.astype(BF16), b.astype(BF16), preferred_element_type=F32)


def _dot_tn(a, b):
    return lax.dot_general(a.astype(BF16), b.astype(BF16), (((0,), (0,)), ((), ())),
                           preferred_element_type=F32)


def _dot_nt(a, b):
    return lax.dot_general(a.astype(BF16), b.astype(BF16), (((1,), (1,)), ((), ())),
                           preferred_element_type=F32)


def _rms_proj_kernel(x_ref, g_ref, w_ref, o_ref):
    h = _rms(x_ref[...], g_ref[...])
    o_ref[...] = jnp.dot(h.astype(BF16), w_ref[...], preferred_element_type=F32)


def _rms_proj(x2, g, w):
    t, d = x2.shape
    n = w.shape[1]
    return pl.pallas_call(
        _rms_proj_kernel,
        grid=(t // TM_PROJ,),
        in_specs=[
            pl.BlockSpec((TM_PROJ, d), lambda i: (i, 0)),
            pl.BlockSpec((1, d), lambda i: (0, 0)),
            pl.BlockSpec((d, n), lambda i: (0, 0)),
        ],
        out_specs=pl.BlockSpec((TM_PROJ, n), lambda i: (i, 0)),
        out_shape=jax.ShapeDtypeStruct((t, n), F32),
        compiler_params=pltpu.CompilerParams(
            dimension_semantics=("arbitrary",), vmem_limit_bytes=VMEM_LIMIT),
        name="rms_in_proj",
    )(x2, g, w)


def _mixer_kernel(chunk_decay,
                  proj_ref, cos_ref, sin_ref, intra_ref, qd_ref, kd_ref, tri_ref,
                  rng_ref, cw_ref, cb_ref, wq_ref, wk_ref, bi_ref, bf_ref,
                  mng_ref, skip_ref,
                  y_ref,
                  rstate_ref, cstate_ref, m_ref, tail_ref):
    j = pl.program_id(1)
    n_chunks = L_BLOCK // CHUNK

    @pl.when(j == 0)
    def _():
        rstate_ref[...] = jnp.zeros_like(rstate_ref)
        cstate_ref[...] = jnp.zeros_like(cstate_ref)
        m_ref[...] = jnp.zeros_like(m_ref)
        tail_ref[...] = jnp.zeros_like(tail_ref)

    cosv = cos_ref[...]
    sinv = sin_ref[...]
    k_scale = RET_DK ** -0.5

    for h in range(RET_HEADS):
        lo = h * HEAD_W
        q = proj_ref[:, OFF_RQ + lo:OFF_RQ + lo + HEAD_W]
        k = proj_ref[:, OFF_RK + lo:OFF_RK + lo + HEAD_W]
        q = q * cosv + pltpu.roll(q, HEAD_W // 2, 1) * sinv
        k = (k * cosv + pltpu.roll(k, HEAD_W // 2, 1) * sinv) * k_scale
        intra = intra_ref[h]
        qd = qd_ref[h]
        kd = kd_ref[h]
        g = rng_ref[:, lo:lo + HEAD_W]
        for c in range(n_chunks):
            r0 = c * CHUNK
            qc = q[r0:r0 + CHUNK]
            kc = k[r0:r0 + CHUNK]
            vc = proj_ref[r0:r0 + CHUNK, OFF_RV + lo:OFF_RV + lo + HEAD_W]
            gate = proj_ref[r0:r0 + CHUNK, OFF_RG + lo:OFF_RG + lo + HEAD_W]
            state = rstate_ref[h]
            scores = _dot_nt(qc, kc) * intra
            o = _dot(scores, vc) + _dot(qc, state) * qd
            rstate_ref[h] = chunk_decay[h] * state + _dot_tn(kc * kd, vc)
            y_ref[r0:r0 + CHUNK, lo:lo + HEAD_W] = _silu(gate) * _group_norm(o, g)

    mx = proj_ref[:, OFF_MX:OFF_MX + D_GROUP]
    xp = jnp.concatenate([tail_ref[...], mx], axis=0)
    base = CONV_TAIL - (CONV_W - 1)
    conv = cb_ref[...] + xp[base:base + L_BLOCK] * cw_ref[0:1, :]
    for t in range(1, CONV_W):
        conv = conv + xp[base + t:base + t + L_BLOCK] * cw_ref[t:t + 1, :]
    tail_ref[...] = mx[L_BLOCK - CONV_TAIL:L_BLOCK]
    cact = _silu(conv)

    gi = proj_ref[:, OFF_GI:OFF_GI + LANES] + bi_ref[...]
    gf = proj_ref[:, OFF_GF:OFF_GF + LANES] + bf_ref[...]
    logf = jnp.minimum(gf, 0.0) - jnp.log1p(jnp.exp(-jnp.abs(gf)))
    tri = tri_ref[...]
    lane = lax.broadcasted_iota(jnp.int32, (CHUNK, HEAD_W), 1)
    ones_col = (lane == 0).astype(F32)
    m_scale = MLSTM_D ** -0.5

    mq = []
    mk = []
    for h in range(MLSTM_HEADS):
        lo = h * HEAD_W
        ch = cact[:, lo:lo + HEAD_W]
        mq.append(_dot(ch, wq_ref[h]))
        mk.append(_dot(ch, wk_ref[h]) * m_scale)

    for c in range(n_chunks):
        r0 = c * CHUNK
        fcum = jnp.dot(tri, logf[r0:r0 + CHUNK], preferred_element_type=F32,
                       precision=lax.Precision.HIGHEST)
        a = fcum[CHUNK - 1:CHUNK]
        logw = a - fcum + gi[r0:r0 + CHUNK]
        m_in = jnp.max(logw, axis=0, keepdims=True)
        m_old = m_ref[...]
        m_new = jnp.maximum(a + m_old, m_in)
        decay = jnp.exp(a + m_old - m_new)
        w = jnp.exp(logw - m_new)
        em = jnp.exp(-m_new)
        m_ref[...] = m_new
        for h in range(MLSTM_HEADS):
            lo = h * HEAD_W
            qh = mq[h][r0:r0 + CHUNK]
            kh = mk[h][r0:r0 + CHUNK]
            vh = proj_ref[r0:r0 + CHUNK, OFF_MV + lo:OFF_MV + lo + HEAD_W]
            oh = proj_ref[r0:r0 + CHUNK, OFF_MO + lo:OFF_MO + lo + HEAD_W]
            ch = cact[r0:r0 + CHUNK, lo:lo + HEAD_W]
            v_ext = jnp.concatenate([vh, ones_col], axis=1)
            cmem = decay[:, h:h + 1] * cstate_ref[h] + _dot_tn(kh * w[:, h:h + 1], v_ext)
            cstate_ref[h] = cmem
            num = _dot(qh, cmem)
            den = jnp.maximum(jnp.abs(num[:, HEAD_W:HEAD_W + 1]), em[:, h:h + 1])
            hh = num[:, :HEAD_W] / den
            hm = _group_norm(hh, mng_ref[:, lo:lo + HEAD_W])
            ym = _sigmoid(oh) * (hm + skip_ref[:, lo:lo + HEAD_W] * ch)
            y_ref[r0:r0 + CHUNK, D_GROUP + lo:D_GROUP + lo + HEAD_W] = ym


def _retention_tables():
    h = np.arange(RET_HEADS, dtype=np.float64)
    log_g = np.log(1.0 - 2.0 ** (-5.0 - h))
    l = np.arange(CHUNK, dtype=np.float64)
    intra = np.exp(log_g[:, None, None] * np.abs(l[:, None] - l[None, :]))
    qd = np.exp(log_g[:, None] * (l + 1.0))[:, :, None] * np.ones((1, 1, HEAD_W))
    kd = np.exp(log_g[:, None] * (CHUNK - 1.0 - l))[:, :, None] * np.ones((1, 1, HEAD_W))
    cd = np.exp(log_g * CHUNK)
    return (jnp.asarray(intra, F32), jnp.asarray(qd, F32), jnp.asarray(kd, F32),
            tuple(float(v) for v in cd))


def _rotary_tables(s):
    half = HEAD_W // 2
    inv = ROPE_BASE ** (-np.arange(half, dtype=np.float64) / half)
    ang = np.arange(s, dtype=np.float64)[:, None] * inv[None, :]
    cos = np.cos(ang)
    sin = np.sin(ang)
    return (jnp.asarray(np.concatenate([cos, cos], axis=-1), F32),
            jnp.asarray(np.concatenate([-sin, sin], axis=-1), F32))


def _mixer(proj, b, s, ret_norm_g, m_conv_w, m_conv_b, m_wq, m_wk, m_b_i, m_b_f,
           m_norm_g, m_skip):
    intra, qd, kd, cd = _retention_tables()
    cos_t, sin_t = _rotary_tables(s)
    tri = jnp.asarray(np.tril(np.ones((CHUNK, CHUNK))), F32)
    nj = s // L_BLOCK
    pad = lambda v: jnp.pad(v.reshape(1, -1), ((0, 0), (0, LANES - v.size)))
    full = lambda shape: pl.BlockSpec(shape, lambda bi, j: (0,) * len(shape))
    return pl.pallas_call(
        functools.partial(_mixer_kernel, cd),
        grid=(b, nj),
        in_specs=[
            pl.BlockSpec((L_BLOCK, D_PROJ_PAD), lambda bi, j: (bi * nj + j, 0)),
            pl.BlockSpec((L_BLOCK, HEAD_W), lambda bi, j: (j, 0)),
            pl.BlockSpec((L_BLOCK, HEAD_W), lambda bi, j: (j, 0)),
            full((RET_HEADS, CHUNK, CHUNK)),
            full((RET_HEADS, CHUNK, HEAD_W)),
            full((RET_HEADS, CHUNK, HEAD_W)),
            full((CHUNK, CHUNK)),
            full((1, D_GROUP)),
            full((CONV_W, D_GROUP)),
            full((1, D_GROUP)),
            full((MLSTM_HEADS, MLSTM_D, MLSTM_D)),
            full((MLSTM_HEADS, MLSTM_D, MLSTM_D)),
            full((1, LANES)),
            full((1, LANES)),
            full((1, D_GROUP)),
            full((1, D_GROUP)),
        ],
        out_specs=pl.BlockSpec((L_BLOCK, 2 * D_GROUP), lambda bi, j: (bi * nj + j, 0)),
        out_shape=jax.ShapeDtypeStruct((b * s, 2 * D_GROUP), F32),
        scratch_shapes=[
            pltpu.VMEM((RET_HEADS, RET_DK, HEAD_W), F32),
            pltpu.VMEM((MLSTM_HEADS, MLSTM_D, 2 * HEAD_W), F32),
            pltpu.VMEM((1, LANES), F32),
            pltpu.VMEM((CONV_TAIL, D_GROUP), F32),
        ],
        compiler_params=pltpu.CompilerParams(
            dimension_semantics=("arbitrary", "arbitrary"), vmem_limit_bytes=VMEM_LIMIT),
        name="retention_mlstm_mixer",
    )(proj, cos_t, sin_t, intra, qd, kd, tri,
      ret_norm_g.reshape(1, D_GROUP), m_conv_w, m_conv_b.reshape(1, D_GROUP),
      m_wq.astype(BF16), m_wk.astype(BF16), pad(m_b_i), pad(m_b_f),
      m_norm_g.reshape(1, D_GROUP), m_skip.reshape(1, D_GROUP))


def _post_kernel(y_ref, x_ref, wout_ref, g2_ref, wq_ref, keys_ref,
                 x1_ref, xn_ref, sub_ref):
    x1 = x_ref[...] + jnp.dot(y_ref[...].astype(BF16), wout_ref[...],
                              preferred_element_type=F32)
    x1_ref[...] = x1
    xn = _rms(x1, g2_ref[...])
    xn_ref[...] = xn
    q = jnp.dot(xn.astype(BF16), wq_ref[...], preferred_element_type=F32)
    for i in range(2 * PEER_HEADS):
        lo = i * PEER_HALF
        sub_ref[lo:lo + PEER_NKEYS, :] = _dot_nt(keys_ref[i], q[:, lo:lo + PEER_HALF])


def _post(y, x2, w_out, norm2_g, peer_wq, keys):
    t, d = x2.shape
    nq = peer_wq.shape[1]
    row = lambda w: pl.BlockSpec((TM_PROJ, w), lambda i: (i, 0))
    full = lambda shape: pl.BlockSpec(shape, lambda i: (0,) * len(shape))
    return pl.pallas_call(
        _post_kernel,
        grid=(t // TM_PROJ,),
        in_specs=[row(d), row(d), full((d, d)), full((1, d)), full((d, nq)),
                  full(keys.shape)],
        out_specs=[row(d), row(d), pl.BlockSpec((nq, TM_PROJ), lambda i: (0, i))],
        out_shape=[jax.ShapeDtypeStruct((t, d), F32), jax.ShapeDtypeStruct((t, d), F32),
                   jax.ShapeDtypeStruct((nq, t), F32)],
        compiler_params=pltpu.CompilerParams(
            dimension_semantics=("arbitrary",), vmem_limit_bytes=VMEM_LIMIT),
        name="out_proj_peer_scores",
    )(y, x2, w_out, norm2_g, peer_wq, keys)


def _final_kernel(x1_ref, p_ref, g_ref, *rest):
    o_ref = rest[-1]
    o_ref[...] = _rms(x1_ref[...] + p_ref[...], g_ref[...])


def _final(x1, p_parts, g):
    t, d = x1.shape
    out = None
    row0 = 0
    for p in p_parts:
        b0 = row0 // TM_FINAL
        shifted = pl.BlockSpec((TM_FINAL, d), lambda i, b0=b0: (i + b0, 0))
        in_specs = [shifted, pl.BlockSpec((TM_FINAL, d), lambda i: (i, 0)),
                    pl.BlockSpec((1, d), lambda i: (0, 0))]
        args = [x1, p, g]
        if out is not None:
            in_specs.append(pl.BlockSpec(memory_space=pl.ANY))
            args.append(out)
        out = pl.pallas_call(
            _final_kernel,
            grid=(p.shape[0] // TM_FINAL,),
            in_specs=in_specs,
            out_specs=shifted,
            out_shape=jax.ShapeDtypeStruct((t, d), F32),
            input_output_aliases={} if len(args) == 3 else {3: 0},
            compiler_params=pltpu.CompilerParams(dimension_semantics=("arbitrary",)),
            name="final_rmsnorm",
        )(*args)
        row0 += p.shape[0]
    assert row0 == t
    return out


TM_ROUTE = 128
CAND_ROWS = 56


def _candidate_tables():
    pairs = [(k1, k2) for k1 in range(PEER_TOPK) for k2 in range(PEER_TOPK)
             if (k1 + 1) * (k2 + 1) <= PEER_TOPK]
    assert len(pairs) <= CAND_ROWS
    rep = np.zeros((CAND_ROWS, 2 * PEER_TOPK), np.float32)
    til = np.zeros((CAND_ROWS, 2 * PEER_TOPK), np.float32)
    for pos, (k1, k2) in enumerate(pairs):
        rep[pos, k1] = 1.0
        til[pos, PEER_TOPK + k2] = 1.0
    return jnp.asarray(rep), jnp.asarray(til), len(pairs)


def _dot_exact(a, b):
    return jnp.dot(a, b, preferred_element_type=F32, precision=lax.Precision.HIGHEST)


def _route_kernel(n_cand, sub_ref, rep_ref, til_ref, seg_ref, exp_ref, gate_ref, offs_ref,
                  s_ref, i_ref, best_ref, eid_ref, m0_ref):
    tm = sub_ref.shape[1]
    neg = -jnp.inf
    key_id = lax.broadcasted_iota(jnp.int32, (PEER_NKEYS, tm), 0).astype(F32)
    cand_id = lax.broadcasted_iota(jnp.int32, (CAND_ROWS, tm), 0).astype(F32)
    rep = rep_ref[...]
    til = til_ref[...]
    for h in range(PEER_HEADS):
        for p in range(2):
            lo = (2 * h + p) * PEER_NKEYS
            x = sub_ref[lo:lo + PEER_NKEYS, :]
            for k in range(PEER_TOPK):
                m = jnp.max(x, axis=0, keepdims=True)
                idx = jnp.min(jnp.where(x == m, key_id, float(PEER_NKEYS)), axis=0, keepdims=True)
                x = jnp.where(key_id == idx, neg, x)
                r = p * PEER_TOPK + k
                s_ref[r:r + 1, :] = m
                i_ref[r:r + 1, :] = idx
        s = s_ref[...]
        iv = i_ref[...]
        cand = _dot_exact(rep, s) + _dot_exact(til, s)
        cand = jnp.where(cand_id < float(n_cand), cand, neg)
        eid = _dot(rep, iv) * float(PEER_NKEYS) + _dot(til, iv)
        for k in range(PEER_TOPK):
            m = jnp.max(cand, axis=0, keepdims=True)
            pos = jnp.min(jnp.where(cand == m, cand_id, float(CAND_ROWS)), axis=0, keepdims=True)
            sel = cand_id == pos
            e = jnp.max(jnp.where(sel, eid, -1.0), axis=0, keepdims=True)
            cand = jnp.where(sel, neg, cand)
            r = h * PEER_TOPK + k
            best_ref[r:r + 1, :] = m
            eid_ref[r:r + 1, :] = e
            if k == 0:
                m0_ref[r:r + PEER_TOPK, :] = jnp.broadcast_to(m, (PEER_TOPK, tm))
    pexp = jnp.exp(best_ref[...] - m0_ref[...])
    gates = pexp / _dot_exact(seg_ref[...], pexp)
    gate_ref[...] = gates.T
    eid = eid_ref[...].astype(jnp.int32)
    exp_ref[...] = eid.T
    offs_ref[0] = eid * ROW_WORDS


def _peer_route(sub_t):
    nq, t = sub_t.shape
    rep, til, n_cand = _candidate_tables()
    e = PEER_HEADS * PEER_TOPK
    seg = jnp.asarray(np.kron(np.eye(PEER_HEADS), np.ones((PEER_TOPK, PEER_TOPK))), F32)
    full = lambda a: pl.BlockSpec(a.shape, lambda i: (0,) * a.ndim)
    out = pl.BlockSpec((TM_ROUTE, e), lambda i: (i, 0))
    return pl.pallas_call(
        functools.partial(_route_kernel, n_cand),
        grid=(t // TM_ROUTE,),
        in_specs=[pl.BlockSpec((nq, TM_ROUTE), lambda i: (0, i)), full(rep), full(til), full(seg)],
        out_specs=[out, out, pl.BlockSpec((1, e, TM_ROUTE), lambda i: (i, 0, 0))],
        out_shape=[jax.ShapeDtypeStruct((t, e), jnp.int32), jax.ShapeDtypeStruct((t, e), F32),
                   jax.ShapeDtypeStruct((t // TM_ROUTE, e, TM_ROUTE), jnp.int32)],
        scratch_shapes=[
            pltpu.VMEM((2 * PEER_TOPK, TM_ROUTE), F32),
            pltpu.VMEM((2 * PEER_TOPK, TM_ROUTE), F32),
            pltpu.VMEM((e, TM_ROUTE), F32),
            pltpu.VMEM((e, TM_ROUTE), F32),
            pltpu.VMEM((e, TM_ROUTE), F32),
        ],
        compiler_params=pltpu.CompilerParams(dimension_semantics=("arbitrary",)),
        name="peer_topk_route",
    )(sub_t, rep, til, seg)


SC_CORES = 2
SC_SUBCORES = 16
SC_LANES = 16
SC_WORKERS = SC_CORES * SC_SUBCORES
PEER_E = PEER_HEADS * PEER_TOPK
SC_ROWS = 32
SC_GATHERS = PEER_E // SC_ROWS
SC_TOK_BLOCK = 8
SC_QUARTER = 256


def _sc_mesh():
    return plsc.VectorSubcoreMesh(core_axis_name="c", subcore_axis_name="s")


def _sc_pipeline(idx_v, tab_hbm, rows_v, sems, compute):
    def gather(step, buf):
        return pltpu.make_async_copy(tab_hbm.at[idx_v.at[step]], rows_v.at[buf], sems.at[buf])

    gather(0, 0).start()

    @pl.loop(0, SC_TOK_BLOCK)
    def _(i):
        for c in range(SC_GATHERS):
            buf = c % 2
            step = i * SC_GATHERS + c
            gather(step, buf).wait()
            if c + 1 < SC_GATHERS:
                gather(step + 1, 1 - buf).start()
            else:
                @pl.when(i + 1 < SC_TOK_BLOCK)
                def _():
                    gather(step + 1, 1 - buf).start()
            compute(i, c, step, buf)


def _peer_down_dots(xn, idx, down, tok_start):
    t_all, d = xn.shape
    t = t_all - tok_start
    tok_w = t // SC_WORKERS
    nblk = tok_w // SC_TOK_BLOCK
    assert nblk * SC_TOK_BLOCK * SC_WORKERS == t
    lanes = SC_LANES

    @functools.partial(
        pl.kernel, mesh=_sc_mesh(),
        out_type=jax.ShapeDtypeStruct((t, PEER_E), F32),
        scratch_types=[
            pltpu.VMEM((SC_TOK_BLOCK * SC_GATHERS, SC_ROWS), jnp.int32),
            pltpu.VMEM((SC_TOK_BLOCK, d), F32),
            pltpu.VMEM((2, SC_ROWS, d), F32),
            pltpu.VMEM((SC_ROWS, lanes), F32),
            pltpu.VMEM((SC_TOK_BLOCK, PEER_E), F32),
            pltpu.SemaphoreType.DMA((2,)),
        ],
        compiler_params=pltpu.CompilerParams(needs_layout_passes=False),
        name="peer_down_dots",
    )
    def k(x_hbm, idx_hbm, tab_hbm, out_hbm, idx_v, x_v, rows_v, acc_v, out_v, sems):
        wid = lax.axis_index("s") * SC_CORES + lax.axis_index("c")
        lane = lax.iota(jnp.int32, lanes)

        @pl.loop(0, nblk)
        def _(blk):
            out0 = wid * tok_w + blk * SC_TOK_BLOCK
            tok0 = tok_start + out0
            pltpu.sync_copy(idx_hbm.at[pl.ds(tok0 * SC_GATHERS, SC_TOK_BLOCK * SC_GATHERS)], idx_v)
            pltpu.sync_copy(x_hbm.at[pl.ds(tok0, SC_TOK_BLOCK)], x_v)

            def compute(i, c, step, buf):
                def body(dc, accs):
                    xv = x_v[i, pl.ds(dc * lanes, lanes)]
                    return tuple(a + rows_v[buf, r, pl.ds(dc * lanes, lanes)] * xv
                                 for r, a in enumerate(accs))
                accs = lax.fori_loop(0, d // lanes, body,
                                     tuple(jnp.zeros((lanes,), F32) for _ in range(SC_ROWS)))
                for r in range(SC_ROWS):
                    acc_v[r, :] = accs[r]
                for g in range(SC_ROWS // lanes):
                    tot = jnp.zeros((lanes,), F32)
                    for l in range(lanes):
                        tot = tot + plsc.load_gather(
                            acc_v, [lane + g * lanes, jnp.full((lanes,), l, jnp.int32)])
                    out_v[i, pl.ds(c * SC_ROWS + g * lanes, lanes)] = tot

            _sc_pipeline(idx_v, tab_hbm, rows_v, sems, compute)
            pltpu.sync_copy(out_v, out_hbm.at[pl.ds(out0, SC_TOK_BLOCK)])

    return k(xn, idx.reshape(t_all * SC_GATHERS, SC_ROWS), down)


def _peer_up_sum(coef, idx, up, tok_start):
    t_all = coef.shape[0]
    t = t_all - tok_start
    d = up.shape[1]
    tok_w = t // SC_WORKERS
    nblk = tok_w // SC_TOK_BLOCK
    assert nblk * SC_TOK_BLOCK * SC_WORKERS == t
    lanes = SC_LANES
    nj = SC_QUARTER // lanes

    @functools.partial(
        pl.kernel, mesh=_sc_mesh(),
        out_type=jax.ShapeDtypeStruct((t, d), F32),
        scratch_types=[
            pltpu.VMEM((SC_TOK_BLOCK * SC_GATHERS, SC_ROWS), jnp.int32),
            pltpu.VMEM((SC_TOK_BLOCK * PEER_E,), F32),
            pltpu.VMEM((2, SC_ROWS, d), F32),
            pltpu.VMEM((SC_TOK_BLOCK, d), F32),
            pltpu.SemaphoreType.DMA((2,)),
        ],
        compiler_params=pltpu.CompilerParams(needs_layout_passes=False),
        name="peer_up_sum",
    )
    def k(coef_hbm, idx_hbm, tab_hbm, out_hbm, idx_v, coef_v, rows_v, out_v, sems):
        wid = lax.axis_index("s") * SC_CORES + lax.axis_index("c")

        @pl.loop(0, nblk)
        def _(blk):
            out0 = wid * tok_w + blk * SC_TOK_BLOCK
            tok0 = tok_start + out0
            pltpu.sync_copy(idx_hbm.at[pl.ds(tok0 * SC_GATHERS, SC_TOK_BLOCK * SC_GATHERS)], idx_v)
            pltpu.sync_copy(coef_hbm.at[pl.ds(tok0 * PEER_E, SC_TOK_BLOCK * PEER_E)], coef_v)

            def compute(i, c, step, buf):
                for q in range(d // SC_QUARTER):
                    col = lambda j: pl.ds(q * SC_QUARTER + j * lanes, lanes)
                    if c == 0:
                        init = tuple(jnp.zeros((lanes,), F32) for _ in range(nj))
                    else:
                        init = tuple(out_v[i, col(j)] for j in range(nj))

                    def body(r, accs):
                        cf = plsc.load_gather(
                            coef_v, [jnp.full((lanes,), step * SC_ROWS + r, jnp.int32)])
                        return tuple(a + rows_v[buf, r, col(j)] * cf for j, a in enumerate(accs))
                    accs = lax.fori_loop(0, SC_ROWS, body, init)
                    for j in range(nj):
                        out_v[i, col(j)] = accs[j]

            _sc_pipeline(idx_v, tab_hbm, rows_v, sems, compute)
            pltpu.sync_copy(out_v, out_hbm.at[pl.ds(out0, SC_TOK_BLOCK)])

    return k(coef.reshape(t_all * PEER_E), idx.reshape(t_all * SC_GATHERS, SC_ROWS), up)


TC_SUB = 128
TC_SUBS = 2
TC_STEP = TC_SUB * TC_SUBS
TC_GROUP = 4
ROW_WORDS = 4
ROW_HALVES = 2 * ROW_WORDS
STACK_COLS = PEER_E * ROW_HALVES
PEER_TC_DOWN_TOKENS = 22528
PEER_TC_UP_TOKENS = 22528


def _pack_rows_bf16(tab):
    n, d = tab.shape
    return pl.pallas_call(
        _pack_kernel,
        grid=(n // PACK_ROWS,),
        in_specs=[pl.BlockSpec((PACK_ROWS, d), lambda i: (i, 0))],
        out_specs=pl.BlockSpec((PACK_ROWS * ROW_WORDS, LANES), lambda i: (i, 0)),
        out_shape=jax.ShapeDtypeStruct((n * ROW_WORDS, LANES), jnp.uint32),
        compiler_params=pltpu.CompilerParams(dimension_semantics=("arbitrary",)),
        name="pack_rows_bf16",
    )(tab)


PACK_ROWS = 512


def _pack_kernel(x_ref, o_ref):
    u = pltpu.bitcast(x_ref[...], jnp.uint32)
    r = (u + jnp.uint32(0x7FFF) + ((u >> 16) & jnp.uint32(1))) >> 16
    for s in range(ROW_WORDS):
        lo = r[:, 2 * LANES * s:2 * LANES * s + LANES]
        hi = r[:, 2 * LANES * s + LANES:2 * LANES * (s + 1)]
        o_ref[pl.ds(s, PACK_ROWS, stride=ROW_WORDS), :] = lo | (hi << 16)


def _load_table_once(tab_hbm, tab_v, sem):
    @pl.when(pl.program_id(0) == 0)
    def _():
        cp = pltpu.make_async_copy(tab_hbm, tab_v, sem)
        cp.start()
        cp.wait()


def _tc_token_pipeline(idx_hbm, idx_s, isem, tab_v, stack_ref, compute):
    step = pl.program_id(0)

    def idx_copy(block, slot):
        return pltpu.make_async_copy(idx_hbm.at[block], idx_s[slot], isem.at[slot])

    @pl.when(step == 0)
    def _():
        idx_copy(0, 0).start()

    for sub in range(TC_SUBS):
        block = step * TC_SUBS + sub
        idx_copy(block, sub).wait()
        if sub + 1 < TC_SUBS:
            idx_copy(block + 1, sub + 1).start()
        else:
            @pl.when(step + 1 < pl.num_programs(0))
            def _():
                idx_copy(block + 1, 0).start()
        offs = idx_s[sub]

        def copy_rows(i, buf):
            for k in range(PEER_E):
                row0 = offs.at[k][i]
                stack_ref[buf, ROW_WORDS * k:ROW_WORDS * (k + 1), :] = (
                    tab_v[pl.ds(pl.multiple_of(row0, ROW_WORDS), ROW_WORDS), :])

        for g in range(TC_GROUP):
            copy_rows(g, g)

        def trip(ip, carry):
            i0 = ip * 2 * TC_GROUP
            for half in range(2):
                cur = half * TC_GROUP
                nxt = (1 - half) * TC_GROUP
                for g in range(TC_GROUP):
                    copy_rows(jnp.minimum(i0 + cur + TC_GROUP + g, TC_SUB - 1), nxt + g)
                for g in range(TC_GROUP):
                    compute(sub * TC_SUB + i0 + cur + g, cur + g)
            return carry
        lax.fori_loop(0, TC_SUB // (2 * TC_GROUP), trip, 0)


def _tc_peer_scratch(tabp):
    return [pltpu.VMEM(tabp.shape, jnp.uint32),
            pltpu.VMEM((2 * TC_GROUP, PEER_E * ROW_WORDS, LANES), jnp.uint32),
            [pltpu.SMEM((PEER_E, TC_SUB), jnp.int32) for _ in range(TC_SUBS)],
            pltpu.SemaphoreType.DMA,
            pltpu.SemaphoreType.DMA((TC_SUBS,))]


def _stack_selectors():
    col = np.arange(STACK_COLS)
    diag = (col[None, :] % ROW_HALVES == np.arange(ROW_HALVES)[:, None]).astype(np.float32)
    owner = (col[:, None] // ROW_HALVES == np.arange(PEER_E)[None, :]).astype(np.float32)
    return jnp.asarray(diag), jnp.asarray(owner)


def _tc_down_kernel(idx_hbm, x_ref, tab_hbm, diag_ref, fold_ref, o_ref,
                    tab_v, stack_ref, idx_s, sem, isem, z_ref):
    _load_table_once(tab_hbm, tab_v, sem)

    def compute(t, buf):
        stack = pltpu.bitcast(stack_ref[buf], BF16)
        xr = x_ref[pl.ds(t, 1), :]
        x8 = jnp.concatenate([xr[:, LANES * j:LANES * (j + 1)] for j in range(ROW_HALVES)],
                             axis=0).astype(BF16)
        a = lax.dot_general(x8, stack, (((1,), (1,)), ((), ())), preferred_element_type=F32)
        z_ref[pl.ds(t, 1), :] = jnp.sum(a * diag_ref[...], axis=0, keepdims=True)

    _tc_token_pipeline(idx_hbm, idx_s, isem, tab_v, stack_ref, compute)
    o_ref[...] = _dot_exact(z_ref[...], fold_ref[...])


def _peer_down_dots_tc(xn, offs, down, n_tok):
    tabp = _pack_rows_bf16(down)
    diag, owner = _stack_selectors()
    full = lambda a: pl.BlockSpec(a.shape, lambda i: (0,) * a.ndim)
    return pl.pallas_call(
        _tc_down_kernel,
        grid=(n_tok // TC_STEP,),
        in_specs=[pl.BlockSpec(memory_space=pl.ANY),
                  pl.BlockSpec((TC_STEP, xn.shape[1]), lambda i: (i, 0)),
                  pl.BlockSpec(memory_space=pl.ANY), full(diag), full(owner)],
        out_specs=pl.BlockSpec((TC_STEP, PEER_E), lambda i: (i, 0)),
        out_shape=jax.ShapeDtypeStruct((n_tok, PEER_E), F32),
        scratch_shapes=_tc_peer_scratch(tabp) + [pltpu.VMEM((TC_STEP, STACK_COLS), F32)],
        compiler_params=pltpu.CompilerParams(
            dimension_semantics=("arbitrary",), vmem_limit_bytes=VMEM_LIMIT),
        name="peer_down_dots_tc",
    )(offs, xn, tabp, diag, owner)


def _tc_up_kernel(idx_hbm, coef_ref, tab_hbm, spread_ref, diag_ref, o_ref,
                  tab_v, stack_ref, idx_s, sem, isem, ce_ref):
    _load_table_once(tab_hbm, tab_v, sem)
    c = coef_ref[...]
    c_hi = c.astype(BF16)
    c_lo = (c - c_hi.astype(F32)).astype(BF16)
    ce_ref[0] = jnp.dot(c_hi, spread_ref[...], preferred_element_type=F32)
    ce_ref[1] = jnp.dot(c_lo, spread_ref[...], preferred_element_type=F32)

    def compute(t, buf):
        stack = pltpu.bitcast(stack_ref[buf], BF16)
        d = diag_ref[...]
        lhs = jnp.concatenate([(ce_ref[0, pl.ds(t, 1), :] * d).astype(BF16),
                               (ce_ref[1, pl.ds(t, 1), :] * d).astype(BF16)], axis=0)
        r = jnp.dot(lhs, stack, preferred_element_type=F32)
        out8 = r[0:ROW_HALVES] + r[ROW_HALVES:2 * ROW_HALVES]
        o_ref[pl.ds(t, 1), :] = jnp.concatenate(
            [out8[j:j + 1, :] for j in range(ROW_HALVES)], axis=1)

    _tc_token_pipeline(idx_hbm, idx_s, isem, tab_v, stack_ref, compute)


def _peer_up_sum_tc(coef, offs, up, n_tok):
    d = up.shape[1]
    tabp = _pack_rows_bf16(up)
    diag, owner = _stack_selectors()
    spread = owner.T.astype(BF16)
    full = lambda a: pl.BlockSpec(a.shape, lambda i: (0,) * a.ndim)
    return pl.pallas_call(
        _tc_up_kernel,
        grid=(n_tok // TC_STEP,),
        in_specs=[pl.BlockSpec(memory_space=pl.ANY),
                  pl.BlockSpec((TC_STEP, PEER_E), lambda i: (i, 0)),
                  pl.BlockSpec(memory_space=pl.ANY), full(spread), full(diag)],
        out_specs=pl.BlockSpec((TC_STEP, d), lambda i: (i, 0)),
        out_shape=jax.ShapeDtypeStruct((n_tok, d), F32),
        scratch_shapes=_tc_peer_scratch(tabp) + [pltpu.VMEM((2, TC_STEP, STACK_COLS), F32)],
        compiler_params=pltpu.CompilerParams(
            dimension_semantics=("arbitrary",), vmem_limit_bytes=VMEM_LIMIT),
        name="peer_up_sum_tc",
    )(offs, coef, tabp, spread, diag)


def _coef_kernel(pre_ref, gate_ref, o_ref):
    x = pre_ref[...]
    gelu = 0.5 * x * (1.0 + lax.erf(x * (2.0 ** -0.5)))
    o_ref[...] = gate_ref[...] * gelu


def _peer_coef(pre, gates):
    t, e = pre.shape
    row = pl.BlockSpec((TM_FINAL, e), lambda i: (i, 0))
    return pl.pallas_call(
        _coef_kernel,
        grid=(t // TM_FINAL,),
        in_specs=[row, row],
        out_specs=row,
        out_shape=jax.ShapeDtypeStruct((t, e), F32),
        compiler_params=pltpu.CompilerParams(dimension_semantics=("arbitrary",)),
        name="peer_gate_gelu",
    )(pre, gates)


def _peer_experts(xn, expert, gates, offs, down, up):
    pre_sc = _peer_down_dots(xn, expert, down, PEER_TC_DOWN_TOKENS)
    pre_tc = _peer_down_dots_tc(xn, offs, down, PEER_TC_DOWN_TOKENS)
    coef = _peer_coef(jnp.concatenate([pre_tc, pre_sc], axis=0), gates)
    p_sc = _peer_up_sum(coef, expert, up, PEER_TC_UP_TOKENS)
    p_tc = _peer_up_sum_tc(coef, offs, up, PEER_TC_UP_TOKENS)
    return [p_tc, p_sc]


def kernel(x, norm1_g, w_in, ret_norm_g, m_conv_w, m_conv_b, m_wq, m_wk, m_b_i, m_b_f,
           m_norm_g, m_skip, w_out, norm2_g, peer_wq, peer_keys, peer_down, peer_up, final_g):
    b, s, d = x.shape
    t = b * s
    x2 = x.reshape(t, d)
    depth = norm1_g.shape[0]
    for i in range(depth):
        if i > 0:
            x2 = x1 + jnp.concatenate(p, axis=0)
        wi = w_in[i]
        n_main = OFF_GI
        gate_pad = lambda w: jnp.pad(w, ((0, 0), (0, LANES - w.shape[1])))
        w_pad = jnp.concatenate(
            [wi[:, :n_main], gate_pad(wi[:, n_main:n_main + MLSTM_HEADS]),
             gate_pad(wi[:, n_main + MLSTM_HEADS:])], axis=1).astype(BF16)
        proj = _rms_proj(x2, norm1_g[i].reshape(1, d), w_pad)
        y = _mixer(proj, b, s, ret_norm_g[i], m_conv_w[i], m_conv_b[i], m_wq[i], m_wk[i],
                   m_b_i[i], m_b_f[i], m_norm_g[i], m_skip[i])
        keys = peer_keys[i].reshape(2 * PEER_HEADS, PEER_NKEYS, PEER_HALF).astype(BF16)
        x1, xn, sub_t = _post(y, x2, w_out[i].astype(BF16), norm2_g[i].reshape(1, d),
                              peer_wq[i].astype(BF16), keys)
        expert, gates, offs = _peer_route(sub_t)
        p = _peer_experts(xn, expert, gates, offs, peer_down[i], peer_up[i])
    out = _final(x1, p, final_g.reshape(1, d))
    return out.reshape(b, s, d)
```

```python
import functools

import numpy as np
import jax
import jax.numpy as jnp
from jax import lax
from jax.experimental import pallas as pl
from jax.experimental.pallas import tpu as pltpu
from jax.experimental.pallas import tpu_sc as plsc

F32 = jnp.float32
BF16 = jnp.bfloat16

D_MODEL = 1024
CHUNK = 64
RET_HEADS = 4
RET_DK = 128
MLSTM_HEADS = 4
MLSTM_D = 128
HEAD_W = 128
D_GROUP = 512
CONV_W = 4
ROPE_BASE = 10000.0
PEER_HEADS = 8
PEER_NKEYS = 128
PEER_TOPK = 16
PEER_HALF = 128
PEER_BLOCK = 128
EPS = 1e-6

OFF_RQ, OFF_RK, OFF_RV, OFF_RG = 0, 512, 1024, 1536
OFF_MX, OFF_MV, OFF_MO = 2048, 2560, 3072
OFF_GI, OFF_GF = 3584, 3712
D_PROJ_PAD = 3840

LANES = 128
CONV_TAIL = 8
VMEM_LIMIT = 56 * 1024 * 1024

TM_PROJ = 256
L_BLOCK = 256
TM_FINAL = 512


def _rms(x, g):
    ms = jnp.mean(x * x, axis=-1, keepdims=True)
    return x * lax.rsqrt(ms + EPS) * g


def _group_norm(h, g):
    mu = jnp.mean(h, axis=-1, keepdims=True)
    d = h - mu
    var = jnp.mean(d * d, axis=-1, keepdims=True)
    return d * lax.rsqrt(var + EPS) * g


def _silu(x):
    return x * (1.0 / (1.0 + jnp.exp(-x)))


def _sigmoid(x):
    return 1.0 / (1.0 + jnp.exp(-x))


def _dot(a, b):
    return jnp.dot(a.astype(BF16), b.astype(BF16), preferred_element_type=F32)


def _dot_tn(a, b):
    return lax.dot_general(a.astype(BF16), b.astype(BF16), (((0,), (0,)), ((), ())),
                           preferred_element_type=F32)


def _dot_nt(a, b):
    return lax.dot_general(a.astype(BF16), b.astype(BF16), (((1,), (1,)), ((), ())),
                           preferred_element_type=F32)


def _rms_proj_kernel(x_ref, g_ref, w_ref, o_ref):
    h = _rms(x_ref[...], g_ref[...])
    o_ref[...] = jnp.dot(h.astype(BF16), w_ref[...], preferred_element_type=F32)


def _rms_proj(x2, row0, rows, g, w):
    d = x2.shape[1]
    t = rows
    n = w.shape[1]
    b0 = row0 // TM_PROJ
    return pl.pallas_call(
        _rms_proj_kernel,
        grid=(t // TM_PROJ,),
        in_specs=[
            pl.BlockSpec((TM_PROJ, d), lambda i: (i + b0, 0)),
            pl.BlockSpec((1, d), lambda i: (0, 0)),
            pl.BlockSpec((d, n), lambda i: (0, 0)),
        ],
        out_specs=pl.BlockSpec((TM_PROJ, n), lambda i: (i, 0)),
        out_shape=jax.ShapeDtypeStruct((t, n), F32),
        compiler_params=pltpu.CompilerParams(
            dimension_semantics=("arbitrary",), vmem_limit_bytes=VMEM_LIMIT),
        name="rms_in_proj",
    )(x2, g, w)


def _mixer_kernel(chunk_decay,
                  proj_ref, cos_ref, sin_ref, intra_ref, qd_ref, kd_ref, tri_ref,
                  rng_ref, cw_ref, cb_ref, wq_ref, wk_ref, bi_ref, bf_ref,
                  mng_ref, skip_ref,
                  y_ref,
                  rstate_ref, cstate_ref, m_ref, tail_ref):
    j = pl.program_id(1)
    n_chunks = L_BLOCK // CHUNK

    @pl.when(j == 0)
    def _():
        rstate_ref[...] = jnp.zeros_like(rstate_ref)
        cstate_ref[...] = jnp.zeros_like(cstate_ref)
        m_ref[...] = jnp.zeros_like(m_ref)
        tail_ref[...] = jnp.zeros_like(tail_ref)

    cosv = cos_ref[...]
    sinv = sin_ref[...]
    k_scale = RET_DK ** -0.5

    for h in range(RET_HEADS):
        lo = h * HEAD_W
        q = proj_ref[:, OFF_RQ + lo:OFF_RQ + lo + HEAD_W]
        k = proj_ref[:, OFF_RK + lo:OFF_RK + lo + HEAD_W]
        q = q * cosv + pltpu.roll(q, HEAD_W // 2, 1) * sinv
        k = (k * cosv + pltpu.roll(k, HEAD_W // 2, 1) * sinv) * k_scale
        intra = intra_ref[h]
        qd = qd_ref[h]
        kd = kd_ref[h]
        g = rng_ref[:, lo:lo + HEAD_W]
        for c in range(n_chunks):
            r0 = c * CHUNK
            qc = q[r0:r0 + CHUNK]
            kc = k[r0:r0 + CHUNK]
            vc = proj_ref[r0:r0 + CHUNK, OFF_RV + lo:OFF_RV + lo + HEAD_W]
            gate = proj_ref[r0:r0 + CHUNK, OFF_RG + lo:OFF_RG + lo + HEAD_W]
            state = rstate_ref[h]
            scores = _dot_nt(qc, kc) * intra
            o = _dot(scores, vc) + _dot(qc, state) * qd
            rstate_ref[h] = chunk_decay[h] * state + _dot_tn(kc * kd, vc)
            y_ref[r0:r0 + CHUNK, lo:lo + HEAD_W] = _silu(gate) * _group_norm(o, g)

    mx = proj_ref[:, OFF_MX:OFF_MX + D_GROUP]
    xp = jnp.concatenate([tail_ref[...], mx], axis=0)
    base = CONV_TAIL - (CONV_W - 1)
    conv = cb_ref[...] + xp[base:base + L_BLOCK] * cw_ref[0:1, :]
    for t in range(1, CONV_W):
        conv = conv + xp[base + t:base + t + L_BLOCK] * cw_ref[t:t + 1, :]
    tail_ref[...] = mx[L_BLOCK - CONV_TAIL:L_BLOCK]
    cact = _silu(conv)

    gi = proj_ref[:, OFF_GI:OFF_GI + LANES] + bi_ref[...]
    gf = proj_ref[:, OFF_GF:OFF_GF + LANES] + bf_ref[...]
    logf = jnp.minimum(gf, 0.0) - jnp.log1p(jnp.exp(-jnp.abs(gf)))
    tri = tri_ref[...]
    lane = lax.broadcasted_iota(jnp.int32, (CHUNK, HEAD_W), 1)
    ones_col = (lane == 0).astype(F32)
    m_scale = MLSTM_D ** -0.5

    mq = []
    mk = []
    for h in range(MLSTM_HEADS):
        lo = h * HEAD_W
        ch = cact[:, lo:lo + HEAD_W]
        mq.append(_dot(ch, wq_ref[h]))
        mk.append(_dot(ch, wk_ref[h]) * m_scale)

    for c in range(n_chunks):
        r0 = c * CHUNK
        fcum = jnp.dot(tri, logf[r0:r0 + CHUNK], preferred_element_type=F32,
                       precision=lax.Precision.HIGHEST)
        a = fcum[CHUNK - 1:CHUNK]
        logw = a - fcum + gi[r0:r0 + CHUNK]
        m_in = jnp.max(logw, axis=0, keepdims=True)
        m_old = m_ref[...]
        m_new = jnp.maximum(a + m_old, m_in)
        decay = jnp.exp(a + m_old - m_new)
        w = jnp.exp(logw - m_new)
        em = jnp.exp(-m_new)
        m_ref[...] = m_new
        for h in range(MLSTM_HEADS):
            lo = h * HEAD_W
            qh = mq[h][r0:r0 + CHUNK]
            kh = mk[h][r0:r0 + CHUNK]
            vh = proj_ref[r0:r0 + CHUNK, OFF_MV + lo:OFF_MV + lo + HEAD_W]
            oh = proj_ref[r0:r0 + CHUNK, OFF_MO + lo:OFF_MO + lo + HEAD_W]
            ch = cact[r0:r0 + CHUNK, lo:lo + HEAD_W]
            v_ext = jnp.concatenate([vh, ones_col], axis=1)
            cmem = decay[:, h:h + 1] * cstate_ref[h] + _dot_tn(kh * w[:, h:h + 1], v_ext)
            cstate_ref[h] = cmem
            num = _dot(qh, cmem)
            den = jnp.maximum(jnp.abs(num[:, HEAD_W:HEAD_W + 1]), em[:, h:h + 1])
            hh = num[:, :HEAD_W] / den
            hm = _group_norm(hh, mng_ref[:, lo:lo + HEAD_W])
            ym = _sigmoid(oh) * (hm + skip_ref[:, lo:lo + HEAD_W] * ch)
            y_ref[r0:r0 + CHUNK, D_GROUP + lo:D_GROUP + lo + HEAD_W] = ym


def _retention_tables():
    h = np.arange(RET_HEADS, dtype=np.float64)
    log_g = np.log(1.0 - 2.0 ** (-5.0 - h))
    l = np.arange(CHUNK, dtype=np.float64)
    intra = np.exp(log_g[:, None, None] * np.abs(l[:, None] - l[None, :]))
    qd = np.exp(log_g[:, None] * (l + 1.0))[:, :, None] * np.ones((1, 1, HEAD_W))
    kd = np.exp(log_g[:, None] * (CHUNK - 1.0 - l))[:, :, None] * np.ones((1, 1, HEAD_W))
    cd = np.exp(log_g * CHUNK)
    return (jnp.asarray(intra, F32), jnp.asarray(qd, F32), jnp.asarray(kd, F32),
            tuple(float(v) for v in cd))


def _rotary_tables(s):
    half = HEAD_W // 2
    inv = ROPE_BASE ** (-np.arange(half, dtype=np.float64) / half)
    ang = np.arange(s, dtype=np.float64)[:, None] * inv[None, :]
    cos = np.cos(ang)
    sin = np.sin(ang)
    return (jnp.asarray(np.concatenate([cos, cos], axis=-1), F32),
            jnp.asarray(np.concatenate([-sin, sin], axis=-1), F32))


def _mixer(proj, b, s, ret_norm_g, m_conv_w, m_conv_b, m_wq, m_wk, m_b_i, m_b_f,
           m_norm_g, m_skip):
    intra, qd, kd, cd = _retention_tables()
    cos_t, sin_t = _rotary_tables(s)
    tri = jnp.asarray(np.tril(np.ones((CHUNK, CHUNK))), F32)
    nj = s // L_BLOCK
    pad = lambda v: jnp.pad(v.reshape(1, -1), ((0, 0), (0, LANES - v.size)))
    full = lambda shape: pl.BlockSpec(shape, lambda bi, j: (0,) * len(shape))
    return pl.pallas_call(
        functools.partial(_mixer_kernel, cd),
        grid=(b, nj),
        in_specs=[
            pl.BlockSpec((L_BLOCK, D_PROJ_PAD), lambda bi, j: (bi * nj + j, 0)),
            pl.BlockSpec((L_BLOCK, HEAD_W), lambda bi, j: (j, 0)),
            pl.BlockSpec((L_BLOCK, HEAD_W), lambda bi, j: (j, 0)),
            full((RET_HEADS, CHUNK, CHUNK)),
            full((RET_HEADS, CHUNK, HEAD_W)),
            full((RET_HEADS, CHUNK, HEAD_W)),
            full((CHUNK, CHUNK)),
            full((1, D_GROUP)),
            full((CONV_W, D_GROUP)),
            full((1, D_GROUP)),
            full((MLSTM_HEADS, MLSTM_D, MLSTM_D)),
            full((MLSTM_HEADS, MLSTM_D, MLSTM_D)),
            full((1, LANES)),
            full((1, LANES)),
            full((1, D_GROUP)),
            full((1, D_GROUP)),
        ],
        out_specs=pl.BlockSpec((L_BLOCK, 2 * D_GROUP), lambda bi, j: (bi * nj + j, 0)),
        out_shape=jax.ShapeDtypeStruct((b * s, 2 * D_GROUP), F32),
        scratch_shapes=[
            pltpu.VMEM((RET_HEADS, RET_DK, HEAD_W), F32),
            pltpu.VMEM((MLSTM_HEADS, MLSTM_D, 2 * HEAD_W), F32),
            pltpu.VMEM((1, LANES), F32),
            pltpu.VMEM((CONV_TAIL, D_GROUP), F32),
        ],
        compiler_params=pltpu.CompilerParams(
            dimension_semantics=("arbitrary", "arbitrary"), vmem_limit_bytes=VMEM_LIMIT),
        name="retention_mlstm_mixer",
    )(proj, cos_t, sin_t, intra, qd, kd, tri,
      ret_norm_g.reshape(1, D_GROUP), m_conv_w, m_conv_b.reshape(1, D_GROUP),
      m_wq.astype(BF16), m_wk.astype(BF16), pad(m_b_i), pad(m_b_f),
      m_norm_g.reshape(1, D_GROUP), m_skip.reshape(1, D_GROUP))


def _post_kernel(y_ref, x_ref, wout_ref, g2_ref, wq_ref, keys_ref,
                 x1_ref, xn_ref, sub_ref):
    x1 = x_ref[...] + jnp.dot(y_ref[...].astype(BF16), wout_ref[...],
                              preferred_element_type=F32)
    x1_ref[...] = x1
    xn = _rms(x1, g2_ref[...])
    xn_ref[...] = xn
    q = jnp.dot(xn.astype(BF16), wq_ref[...], preferred_element_type=F32)
    for i in range(2 * PEER_HEADS):
        lo = i * PEER_HALF
        sub_ref[lo:lo + PEER_NKEYS, :] = _dot_nt(keys_ref[i], q[:, lo:lo + PEER_HALF])


def _post(y, x2, row0, w_out, norm2_g, peer_wq, keys):
    t, d = y.shape
    nq = peer_wq.shape[1]
    b0 = row0 // TM_PROJ
    row = lambda w: pl.BlockSpec((TM_PROJ, w), lambda i: (i, 0))
    full = lambda shape: pl.BlockSpec(shape, lambda i: (0,) * len(shape))
    return pl.pallas_call(
        _post_kernel,
        grid=(t // TM_PROJ,),
        in_specs=[row(d), pl.BlockSpec((TM_PROJ, d), lambda i: (i + b0, 0)), full((d, d)),
                  full((1, d)), full((d, nq)), full(keys.shape)],
        out_specs=[row(d), row(d), pl.BlockSpec((nq, TM_PROJ), lambda i: (0, i))],
        out_shape=[jax.ShapeDtypeStruct((t, d), F32), jax.ShapeDtypeStruct((t, d), F32),
                   jax.ShapeDtypeStruct((nq, t), F32)],
        compiler_params=pltpu.CompilerParams(
            dimension_semantics=("arbitrary",), vmem_limit_bytes=VMEM_LIMIT),
        name="out_proj_peer_scores",
    )(y, x2, w_out, norm2_g, peer_wq, keys)


def _final_kernel(x1_ref, p_ref, g_ref, *rest):
    o_ref = rest[-1]
    o_ref[...] = _rms(x1_ref[...] + p_ref[...], g_ref[...])


def _final(parts, g, t):
    d = g.shape[1]
    spec = lambda r0: pl.BlockSpec((TM_FINAL, d), lambda i, b0=r0 // TM_FINAL: (i + b0, 0))
    out = None
    row0 = 0
    for x1, x_row0, p, p_row0, rows in parts:
        in_specs = [spec(x_row0), spec(p_row0), pl.BlockSpec((1, d), lambda i: (0, 0))]
        args = [x1, p, g]
        if out is not None:
            in_specs.append(pl.BlockSpec(memory_space=pl.ANY))
            args.append(out)
        out = pl.pallas_call(
            _final_kernel,
            grid=(rows // TM_FINAL,),
            in_specs=in_specs,
            out_specs=spec(row0),
            out_shape=jax.ShapeDtypeStruct((t, d), F32),
            input_output_aliases={} if len(args) == 3 else {3: 0},
            compiler_params=pltpu.CompilerParams(dimension_semantics=("arbitrary",)),
            name="final_rmsnorm",
        )(*args)
        row0 += rows
    assert row0 == t
    return out


TM_ROUTE = 128
CAND_ROWS = 56


def _candidate_tables():
    pairs = [(k1, k2) for k1 in range(PEER_TOPK) for k2 in range(PEER_TOPK)
             if (k1 + 1) * (k2 + 1) <= PEER_TOPK]
    assert len(pairs) <= CAND_ROWS
    rep = np.zeros((CAND_ROWS, 2 * PEER_TOPK), np.float32)
    til = np.zeros((CAND_ROWS, 2 * PEER_TOPK), np.float32)
    for pos, (k1, k2) in enumerate(pairs):
        rep[pos, k1] = 1.0
        til[pos, PEER_TOPK + k2] = 1.0
    return jnp.asarray(rep), jnp.asarray(til), len(pairs)


def _dot_exact(a, b):
    return jnp.dot(a, b, preferred_element_type=F32, precision=lax.Precision.HIGHEST)


def _route_kernel(n_cand, sub_ref, rep_ref, til_ref, seg_ref, exp_ref, gate_ref, offs_ref,
                  s_ref, i_ref, best_ref, eid_ref, m0_ref):
    tm = sub_ref.shape[1]
    neg = -jnp.inf
    key_id = lax.broadcasted_iota(jnp.int32, (PEER_NKEYS, tm), 0).astype(F32)
    cand_id = lax.broadcasted_iota(jnp.int32, (CAND_ROWS, tm), 0).astype(F32)
    rep = rep_ref[...]
    til = til_ref[...]
    for h in range(PEER_HEADS):
        for p in range(2):
            lo = (2 * h + p) * PEER_NKEYS
            x = sub_ref[lo:lo + PEER_NKEYS, :]
            for k in range(PEER_TOPK):
                m = jnp.max(x, axis=0, keepdims=True)
                idx = jnp.min(jnp.where(x == m, key_id, float(PEER_NKEYS)), axis=0, keepdims=True)
                x = jnp.where(key_id == idx, neg, x)
                r = p * PEER_TOPK + k
                s_ref[r:r + 1, :] = m
                i_ref[r:r + 1, :] = idx
        s = s_ref[...]
        iv = i_ref[...]
        cand = _dot_exact(rep, s) + _dot_exact(til, s)
        cand = jnp.where(cand_id < float(n_cand), cand, neg)
        eid = _dot(rep, iv) * float(PEER_NKEYS) + _dot(til, iv)
        for k in range(PEER_TOPK):
            m = jnp.max(cand, axis=0, keepdims=True)
            pos = jnp.min(jnp.where(cand == m, cand_id, float(CAND_ROWS)), axis=0, keepdims=True)
            sel = cand_id == pos
            e = jnp.max(jnp.where(sel, eid, -1.0), axis=0, keepdims=True)
            cand = jnp.where(sel, neg, cand)
            r = h * PEER_TOPK + k
            best_ref[r:r + 1, :] = m
            eid_ref[r:r + 1, :] = e
            if k == 0:
                m0_ref[r:r + PEER_TOPK, :] = jnp.broadcast_to(m, (PEER_TOPK, tm))
    pexp = jnp.exp(best_ref[...] - m0_ref[...])
    gates = pexp / _dot_exact(seg_ref[...], pexp)
    gate_ref[...] = gates.T
    eid = eid_ref[...].astype(jnp.int32)
    exp_ref[...] = eid.T
    offs_ref[0] = eid * ROW_WORDS


def _peer_route(sub_t):
    nq, t = sub_t.shape
    rep, til, n_cand = _candidate_tables()
    e = PEER_HEADS * PEER_TOPK
    seg = jnp.asarray(np.kron(np.eye(PEER_HEADS), np.ones((PEER_TOPK, PEER_TOPK))), F32)
    full = lambda a: pl.BlockSpec(a.shape, lambda i: (0,) * a.ndim)
    out = pl.BlockSpec((TM_ROUTE, e), lambda i: (i, 0))
    return pl.pallas_call(
        functools.partial(_route_kernel, n_cand),
        grid=(t // TM_ROUTE,),
        in_specs=[pl.BlockSpec((nq, TM_ROUTE), lambda i: (0, i)), full(rep), full(til), full(seg)],
        out_specs=[out, out, pl.BlockSpec((1, e, TM_ROUTE), lambda i: (i, 0, 0))],
        out_shape=[jax.ShapeDtypeStruct((t, e), jnp.int32), jax.ShapeDtypeStruct((t, e), F32),
                   jax.ShapeDtypeStruct((t // TM_ROUTE, e, TM_ROUTE), jnp.int32)],
        scratch_shapes=[
            pltpu.VMEM((2 * PEER_TOPK, TM_ROUTE), F32),
            pltpu.VMEM((2 * PEER_TOPK, TM_ROUTE), F32),
            pltpu.VMEM((e, TM_ROUTE), F32),
            pltpu.VMEM((e, TM_ROUTE), F32),
            pltpu.VMEM((e, TM_ROUTE), F32),
        ],
        compiler_params=pltpu.CompilerParams(dimension_semantics=("arbitrary",)),
        name="peer_topk_route",
    )(sub_t, rep, til, seg)


SC_CORES = 2
SC_SUBCORES = 16
SC_LANES = 16
SC_WORKERS = SC_CORES * SC_SUBCORES
PEER_E = PEER_HEADS * PEER_TOPK
SC_ROWS = 32
SC_GATHERS = PEER_E // SC_ROWS
SC_TOK_BLOCK = 8
SC_QUARTER = 256


def _sc_mesh():
    return plsc.VectorSubcoreMesh(core_axis_name="c", subcore_axis_name="s")


def _sc_pipeline(idx_v, tab_hbm, rows_v, sems, compute):
    def gather(step, buf):
        return pltpu.make_async_copy(tab_hbm.at[idx_v.at[step]], rows_v.at[buf], sems.at[buf])

    gather(0, 0).start()

    @pl.loop(0, SC_TOK_BLOCK)
    def _(i):
        for c in range(SC_GATHERS):
            buf = c % 2
            step = i * SC_GATHERS + c
            gather(step, buf).wait()
            if c + 1 < SC_GATHERS:
                gather(step + 1, 1 - buf).start()
            else:
                @pl.when(i + 1 < SC_TOK_BLOCK)
                def _():
                    gather(step + 1, 1 - buf).start()
            compute(i, c, step, buf)


def _peer_down_dots(xn, idx, down, tok_start):
    t_all, d = xn.shape
    t = t_all - tok_start
    tok_w = t // SC_WORKERS
    nblk = tok_w // SC_TOK_BLOCK
    assert nblk * SC_TOK_BLOCK * SC_WORKERS == t
    lanes = SC_LANES

    @functools.partial(
        pl.kernel, mesh=_sc_mesh(),
        out_type=jax.ShapeDtypeStruct((t, PEER_E), F32),
        scratch_types=[
            pltpu.VMEM((SC_TOK_BLOCK * SC_GATHERS, SC_ROWS), jnp.int32),
            pltpu.VMEM((SC_TOK_BLOCK, d), F32),
            pltpu.VMEM((2, SC_ROWS, d), F32),
            pltpu.VMEM((SC_ROWS, lanes), F32),
            pltpu.VMEM((SC_TOK_BLOCK, PEER_E), F32),
            pltpu.SemaphoreType.DMA((2,)),
        ],
        compiler_params=pltpu.CompilerParams(needs_layout_passes=False),
        name="peer_down_dots",
    )
    def k(x_hbm, idx_hbm, tab_hbm, out_hbm, idx_v, x_v, rows_v, acc_v, out_v, sems):
        wid = lax.axis_index("s") * SC_CORES + lax.axis_index("c")
        lane = lax.iota(jnp.int32, lanes)

        @pl.loop(0, nblk)
        def _(blk):
            out0 = wid * tok_w + blk * SC_TOK_BLOCK
            tok0 = tok_start + out0
            pltpu.sync_copy(idx_hbm.at[pl.ds(tok0 * SC_GATHERS, SC_TOK_BLOCK * SC_GATHERS)], idx_v)
            pltpu.sync_copy(x_hbm.at[pl.ds(tok0, SC_TOK_BLOCK)], x_v)

            def compute(i, c, step, buf):
                def body(dc, accs):
                    xv = x_v[i, pl.ds(dc * lanes, lanes)]
                    return tuple(a + rows_v[buf, r, pl.ds(dc * lanes, lanes)] * xv
                                 for r, a in enumerate(accs))
                accs = lax.fori_loop(0, d // lanes, body,
                                     tuple(jnp.zeros((lanes,), F32) for _ in range(SC_ROWS)))
                for r in range(SC_ROWS):
                    acc_v[r, :] = accs[r]
                for g in range(SC_ROWS // lanes):
                    tot = jnp.zeros((lanes,), F32)
                    for l in range(lanes):
                        tot = tot + plsc.load_gather(
                            acc_v, [lane + g * lanes, jnp.full((lanes,), l, jnp.int32)])
                    out_v[i, pl.ds(c * SC_ROWS + g * lanes, lanes)] = tot

            _sc_pipeline(idx_v, tab_hbm, rows_v, sems, compute)
            pltpu.sync_copy(out_v, out_hbm.at[pl.ds(out0, SC_TOK_BLOCK)])

    return k(xn, idx.reshape(t_all * SC_GATHERS, SC_ROWS), down)


def _peer_up_sum(coef, idx, up, tok_start):
    t_all = coef.shape[0]
    t = t_all - tok_start
    d = up.shape[1]
    tok_w = t // SC_WORKERS
    nblk = tok_w // SC_TOK_BLOCK
    assert nblk * SC_TOK_BLOCK * SC_WORKERS == t
    lanes = SC_LANES
    nj = SC_QUARTER // lanes

    @functools.partial(
        pl.kernel, mesh=_sc_mesh(),
        out_type=jax.ShapeDtypeStruct((t, d), F32),
        scratch_types=[
            pltpu.VMEM((SC_TOK_BLOCK * SC_GATHERS, SC_ROWS), jnp.int32),
            pltpu.VMEM((SC_TOK_BLOCK * PEER_E,), F32),
            pltpu.VMEM((2, SC_ROWS, d), F32),
            pltpu.VMEM((SC_TOK_BLOCK, d), F32),
            pltpu.SemaphoreType.DMA((2,)),
        ],
        compiler_params=pltpu.CompilerParams(needs_layout_passes=False),
        name="peer_up_sum",
    )
    def k(coef_hbm, idx_hbm, tab_hbm, out_hbm, idx_v, coef_v, rows_v, out_v, sems):
        wid = lax.axis_index("s") * SC_CORES + lax.axis_index("c")

        @pl.loop(0, nblk)
        def _(blk):
            out0 = wid * tok_w + blk * SC_TOK_BLOCK
            tok0 = tok_start + out0
            pltpu.sync_copy(idx_hbm.at[pl.ds(tok0 * SC_GATHERS, SC_TOK_BLOCK * SC_GATHERS)], idx_v)
            pltpu.sync_copy(coef_hbm.at[pl.ds(tok0 * PEER_E, SC_TOK_BLOCK * PEER_E)], coef_v)

            def compute(i, c, step, buf):
                for q in range(d // SC_QUARTER):
                    col = lambda j: pl.ds(q * SC_QUARTER + j * lanes, lanes)
                    if c == 0:
                        init = tuple(jnp.zeros((lanes,), F32) for _ in range(nj))
                    else:
                        init = tuple(out_v[i, col(j)] for j in range(nj))

                    def body(r, accs):
                        cf = plsc.load_gather(
                            coef_v, [jnp.full((lanes,), step * SC_ROWS + r, jnp.int32)])
                        return tuple(a + rows_v[buf, r, col(j)] * cf for j, a in enumerate(accs))
                    accs = lax.fori_loop(0, SC_ROWS, body, init)
                    for j in range(nj):
                        out_v[i, col(j)] = accs[j]

            _sc_pipeline(idx_v, tab_hbm, rows_v, sems, compute)
            pltpu.sync_copy(out_v, out_hbm.at[pl.ds(out0, SC_TOK_BLOCK)])

    return k(coef.reshape(t_all * PEER_E), idx.reshape(t_all * SC_GATHERS, SC_ROWS), up)


TC_SUB = 128
TC_SUBS = 2
TC_STEP = TC_SUB * TC_SUBS
TC_GROUP = 8
ROW_WORDS = 4
ROW_HALVES = 2 * ROW_WORDS
STACK_COLS = PEER_E * ROW_HALVES
PEER_SC_DOWN_TOKENS = 14848
PEER_TC_UP_TOKENS = 23040


def _pack_rows_bf16(tab):
    n, d = tab.shape
    return pl.pallas_call(
        _pack_kernel,
        grid=(n // PACK_ROWS,),
        in_specs=[pl.BlockSpec((PACK_ROWS, d), lambda i: (i, 0))],
        out_specs=pl.BlockSpec((PACK_ROWS * ROW_WORDS, LANES), lambda i: (i, 0)),
        out_shape=jax.ShapeDtypeStruct((n * ROW_WORDS, LANES), jnp.uint32),
        compiler_params=pltpu.CompilerParams(dimension_semantics=("arbitrary",)),
        name="pack_rows_bf16",
    )(tab)


PACK_ROWS = 512


def _pack_kernel(x_ref, o_ref):
    u = pltpu.bitcast(x_ref[...], jnp.uint32)
    r = (u + jnp.uint32(0x7FFF) + ((u >> 16) & jnp.uint32(1))) >> 16
    for s in range(ROW_WORDS):
        lo = r[:, 2 * LANES * s:2 * LANES * s + LANES]
        hi = r[:, 2 * LANES * s + LANES:2 * LANES * (s + 1)]
        o_ref[pl.ds(s, PACK_ROWS, stride=ROW_WORDS), :] = lo | (hi << 16)


def _load_table_once(tab_hbm, tab_v, sem):
    @pl.when(pl.program_id(0) == 0)
    def _():
        cp = pltpu.make_async_copy(tab_hbm, tab_v, sem)
        cp.start()
        cp.wait()


def _tc_token_pipeline(idx_hbm, idx_s, isem, tab_v, stack_ref, compute):
    step = pl.program_id(0)

    def idx_copy(block, slot):
        return pltpu.make_async_copy(idx_hbm.at[block], idx_s[slot], isem.at[slot])

    @pl.when(step == 0)
    def _():
        idx_copy(0, 0).start()

    for sub in range(TC_SUBS):
        block = step * TC_SUBS + sub
        idx_copy(block, sub).wait()
        if sub + 1 < TC_SUBS:
            idx_copy(block + 1, sub + 1).start()
        else:
            @pl.when(step + 1 < pl.num_programs(0))
            def _():
                idx_copy(block + 1, 0).start()
        offs = idx_s[sub]

        def copy_rows(i, buf):
            for k in range(PEER_E):
                row0 = offs.at[k][i]
                stack_ref[buf, ROW_WORDS * k:ROW_WORDS * (k + 1), :] = (
                    tab_v[pl.ds(pl.multiple_of(row0, ROW_WORDS), ROW_WORDS), :])

        for g in range(TC_GROUP):
            copy_rows(g, g)

        def trip(ip, carry):
            i0 = ip * 2 * TC_GROUP
            for half in range(2):
                cur = half * TC_GROUP
                nxt = (1 - half) * TC_GROUP
                for g in range(TC_GROUP):
                    copy_rows(jnp.minimum(i0 + cur + TC_GROUP + g, TC_SUB - 1), nxt + g)
                for g in range(TC_GROUP):
                    compute(sub * TC_SUB + i0 + cur + g, cur + g)
            return carry
        lax.fori_loop(0, TC_SUB // (2 * TC_GROUP), trip, 0)


def _tc_peer_scratch(tabp):
    return [pltpu.VMEM(tabp.shape, jnp.uint32),
            pltpu.VMEM((2 * TC_GROUP, PEER_E * ROW_WORDS, LANES), jnp.uint32),
            [pltpu.SMEM((PEER_E, TC_SUB), jnp.int32) for _ in range(TC_SUBS)],
            pltpu.SemaphoreType.DMA,
            pltpu.SemaphoreType.DMA((TC_SUBS,))]


def _stack_selectors():
    col = np.arange(STACK_COLS)
    diag = (col[None, :] % ROW_HALVES == np.arange(ROW_HALVES)[:, None]).astype(np.float32)
    owner = (col[:, None] // ROW_HALVES == np.arange(PEER_E)[None, :]).astype(np.float32)
    return jnp.asarray(diag), jnp.asarray(owner)


def _tc_down_kernel(idx_hbm, x_ref, tab_hbm, diag_ref, fold_ref, o_ref,
                    tab_v, stack_ref, idx_s, sem, isem, z_ref):
    _load_table_once(tab_hbm, tab_v, sem)

    def compute(t, buf):
        stack = pltpu.bitcast(stack_ref[buf], BF16)
        xr = x_ref[pl.ds(t, 1), :]
        x8 = jnp.concatenate([xr[:, LANES * j:LANES * (j + 1)] for j in range(ROW_HALVES)],
                             axis=0).astype(BF16)
        a = lax.dot_general(x8, stack, (((1,), (1,)), ((), ())), preferred_element_type=F32)
        z_ref[pl.ds(t, 1), :] = jnp.sum(a * diag_ref[...], axis=0, keepdims=True)

    _tc_token_pipeline(idx_hbm, idx_s, isem, tab_v, stack_ref, compute)
    o_ref[...] = _dot_exact(z_ref[...], fold_ref[...])


def _peer_down_dots_tc(xn, offs, tabp, n_tok):
    diag, owner = _stack_selectors()
    full = lambda a: pl.BlockSpec(a.shape, lambda i: (0,) * a.ndim)
    return pl.pallas_call(
        _tc_down_kernel,
        grid=(n_tok // TC_STEP,),
        in_specs=[pl.BlockSpec(memory_space=pl.ANY),
                  pl.BlockSpec((TC_STEP, xn.shape[1]), lambda i: (i, 0)),
                  pl.BlockSpec(memory_space=pl.ANY), full(diag), full(owner)],
        out_specs=pl.BlockSpec((TC_STEP, PEER_E), lambda i: (i, 0)),
        out_shape=jax.ShapeDtypeStruct((n_tok, PEER_E), F32),
        scratch_shapes=_tc_peer_scratch(tabp) + [pltpu.VMEM((TC_STEP, STACK_COLS), F32)],
        compiler_params=pltpu.CompilerParams(
            dimension_semantics=("arbitrary",), vmem_limit_bytes=VMEM_LIMIT),
        name="peer_down_dots_tc",
    )(offs, xn, tabp, diag, owner)


def _tc_up_kernel(idx_hbm, coef_ref, tab_hbm, spread_ref, diag_ref, o_ref,
                  tab_v, stack_ref, idx_s, sem, isem, ce_ref):
    _load_table_once(tab_hbm, tab_v, sem)
    c = coef_ref[...]
    c_hi = c.astype(BF16)
    c_lo = (c - c_hi.astype(F32)).astype(BF16)
    ce_ref[0] = jnp.dot(c_hi, spread_ref[...], preferred_element_type=F32)
    ce_ref[1] = jnp.dot(c_lo, spread_ref[...], preferred_element_type=F32)

    def compute(t, buf):
        stack = pltpu.bitcast(stack_ref[buf], BF16)
        d = diag_ref[...]
        lhs = jnp.concatenate([(ce_ref[0, pl.ds(t, 1), :] * d).astype(BF16),
                               (ce_ref[1, pl.ds(t, 1), :] * d).astype(BF16)], axis=0)
        r = jnp.dot(lhs, stack, preferred_element_type=F32)
        out8 = r[0:ROW_HALVES] + r[ROW_HALVES:2 * ROW_HALVES]
        o_ref[pl.ds(t, 1), :] = jnp.concatenate(
            [out8[j:j + 1, :] for j in range(ROW_HALVES)], axis=1)

    _tc_token_pipeline(idx_hbm, idx_s, isem, tab_v, stack_ref, compute)


def _peer_up_sum_tc(coef, offs, tabp, n_tok):
    d = ROW_HALVES * LANES
    diag, owner = _stack_selectors()
    spread = owner.T.astype(BF16)
    full = lambda a: pl.BlockSpec(a.shape, lambda i: (0,) * a.ndim)
    return pl.pallas_call(
        _tc_up_kernel,
        grid=(n_tok // TC_STEP,),
        in_specs=[pl.BlockSpec(memory_space=pl.ANY),
                  pl.BlockSpec((TC_STEP, PEER_E), lambda i: (i, 0)),
                  pl.BlockSpec(memory_space=pl.ANY), full(spread), full(diag)],
        out_specs=pl.BlockSpec((TC_STEP, d), lambda i: (i, 0)),
        out_shape=jax.ShapeDtypeStruct((n_tok, d), F32),
        scratch_shapes=_tc_peer_scratch(tabp) + [pltpu.VMEM((2, TC_STEP, STACK_COLS), F32)],
        compiler_params=pltpu.CompilerParams(
            dimension_semantics=("arbitrary",), vmem_limit_bytes=VMEM_LIMIT),
        name="peer_up_sum_tc",
    )(offs, coef, tabp, spread, diag)


def _coef_kernel(pre_ref, gate_ref, o_ref):
    x = pre_ref[...]
    gelu = 0.5 * x * (1.0 + lax.erf(x * (2.0 ** -0.5)))
    o_ref[...] = gate_ref[...] * gelu


def _peer_coef(pre, gates):
    t, e = pre.shape
    row = pl.BlockSpec((TM_FINAL, e), lambda i: (i, 0))
    return pl.pallas_call(
        _coef_kernel,
        grid=(t // TM_FINAL,),
        in_specs=[row, row],
        out_specs=row,
        out_shape=jax.ShapeDtypeStruct((t, e), F32),
        compiler_params=pltpu.CompilerParams(dimension_semantics=("arbitrary",)),
        name="peer_gate_gelu",
    )(pre, gates)


def _front_half(x2, row0, rows, bsz, seq, w_pad, layer, i):
    (norm1_g, ret_norm_g, m_conv_w, m_conv_b, m_wq, m_wk, m_b_i, m_b_f, m_norm_g, m_skip,
     w_out, norm2_g, peer_wq, keys) = layer
    d = x2.shape[1]
    proj = _rms_proj(x2, row0, rows, norm1_g[i].reshape(1, d), w_pad)
    y = _mixer(proj, bsz, seq, ret_norm_g[i], m_conv_w[i], m_conv_b[i], m_wq[i], m_wk[i],
               m_b_i[i], m_b_f[i], m_norm_g[i], m_skip[i])
    x1, xn, sub_t = _post(y, x2, row0, w_out, norm2_g[i].reshape(1, d), peer_wq, keys)
    return (x1, xn) + tuple(_peer_route(sub_t))


def kernel(x, norm1_g, w_in, ret_norm_g, m_conv_w, m_conv_b, m_wq, m_wk, m_b_i, m_b_f,
           m_norm_g, m_skip, w_out, norm2_g, peer_wq, peer_keys, peer_down, peer_up, final_g):
    b, s, d = x.shape
    t = b * s
    th = t // 2
    x2 = x.reshape(t, d)
    depth = norm1_g.shape[0]
    for i in range(depth):
        if i > 0:
            x2 = jnp.concatenate([x1a, x1b], axis=0) + jnp.concatenate([p_tc, p_sc], axis=0)
        wi = w_in[i]
        n_main = OFF_GI
        gate_pad = lambda w: jnp.pad(w, ((0, 0), (0, LANES - w.shape[1])))
        w_pad = jnp.concatenate(
            [wi[:, :n_main], gate_pad(wi[:, n_main:n_main + MLSTM_HEADS]),
             gate_pad(wi[:, n_main + MLSTM_HEADS:])], axis=1).astype(BF16)
        keys = peer_keys[i].reshape(2 * PEER_HEADS, PEER_NKEYS, PEER_HALF).astype(BF16)
        layer = (norm1_g, ret_norm_g, m_conv_w, m_conv_b, m_wq, m_wk, m_b_i, m_b_f, m_norm_g,
                 m_skip, w_out[i].astype(BF16), norm2_g, peer_wq[i].astype(BF16), keys)
        down_p = _pack_rows_bf16(peer_down[i])
        up_p = _pack_rows_bf16(peer_up[i])

        x1a, xna, expa, gata, offa = _front_half(x2, 0, th, b // 2, s, w_pad, layer, i)
        n_tc_a = th - PEER_SC_DOWN_TOKENS
        pre_sc = _peer_down_dots(xna, expa, peer_down[i], n_tc_a)
        x1b, xnb, expb, gatb, offb = _front_half(x2, th, th, b // 2, s, w_pad, layer, i)
        pre_tc_a = _peer_down_dots_tc(xna, offa, down_p, n_tc_a)
        pre_tc_b = _peer_down_dots_tc(xnb, offb, down_p, th)

        expert = jnp.concatenate([expa, expb], axis=0)
        offs = jnp.concatenate([offa, offb], axis=0)
        coef = _peer_coef(jnp.concatenate([pre_tc_a, pre_sc, pre_tc_b], axis=0),
                          jnp.concatenate([gata, gatb], axis=0))
        n_tc = PEER_TC_UP_TOKENS
        p_sc = _peer_up_sum(coef, expert, peer_up[i], n_tc)
        p_tc = _peer_up_sum_tc(coef, offs, up_p, n_tc)
    out = _final([(x1a, 0, p_tc, 0, th), (x1b, 0, p_tc, th, n_tc - th),
                  (x1b, n_tc - th, p_sc, 0, t - n_tc)], final_g.reshape(1, d), t)
    return out.reshape(b, s, d)
```

```python
import functools

import numpy as np
import jax
import jax.numpy as jnp
from jax import lax
from jax.experimental import pallas as pl
from jax.experimental.pallas import tpu as pltpu
from jax.experimental.pallas import tpu_sc as plsc

F32 = jnp.float32
BF16 = jnp.bfloat16

D_MODEL = 1024
CHUNK = 64
RET_HEADS = 4
RET_DK = 128
MLSTM_HEADS = 4
MLSTM_D = 128
HEAD_W = 128
D_GROUP = 512
CONV_W = 4
ROPE_BASE = 10000.0
PEER_HEADS = 8
PEER_NKEYS = 128
PEER_TOPK = 16
PEER_HALF = 128
PEER_BLOCK = 128
EPS = 1e-6

OFF_RQ, OFF_RK, OFF_RV, OFF_RG = 0, 512, 1024, 1536
OFF_MX, OFF_MV, OFF_MO = 2048, 2560, 3072
OFF_GI, OFF_GF = 3584, 3712
D_PROJ_PAD = 3840

LANES = 128
CONV_TAIL = 8
VMEM_LIMIT = 56 * 1024 * 1024

TM_PROJ = 256
L_BLOCK = 256
TM_FINAL = 512


def _rms(x, g):
    ms = jnp.mean(x * x, axis=-1, keepdims=True)
    return x * lax.rsqrt(ms + EPS) * g


def _group_norm(h, g):
    mu = jnp.mean(h, axis=-1, keepdims=True)
    d = h - mu
    var = jnp.mean(d * d, axis=-1, keepdims=True)
    return d * lax.rsqrt(var + EPS) * g


def _silu(x):
    return x * (1.0 / (1.0 + jnp.exp(-x)))


def _sigmoid(x):
    return 1.0 / (1.0 + jnp.exp(-x))


def _dot(a, b):
    return jnp.dot(a.astype(BF16), b.astype(BF16), preferred_element_type=F32)


def _dot_tn(a, b):
    return lax.dot_general(a.astype(BF16), b.astype(BF16), (((0,), (0,)), ((), ())),
                           preferred_element_type=F32)


def _dot_nt(a, b):
    return lax.dot_general(a.astype(BF16), b.astype(BF16), (((1,), (1,)), ((), ())),
                           preferred_element_type=F32)


def _rms_proj_kernel(x_ref, g_ref, w_ref, o_ref):
    h = _rms(x_ref[...], g_ref[...])
    o_ref[...] = jnp.dot(h.astype(BF16), w_ref[...], preferred_element_type=F32)


def _rms_proj(x2, row0, rows, g, w):
    d = x2.shape[1]
    t = rows
    n = w.shape[1]
    b0 = row0 // TM_PROJ
    return pl.pallas_call(
        _rms_proj_kernel,
        grid=(t // TM_PROJ,),
        in_specs=[
            pl.BlockSpec((TM_PROJ, d), lambda i: (i + b0, 0)),
            pl.BlockSpec((1, d), lambda i: (0, 0)),
            pl.BlockSpec((d, n), lambda i: (0, 0)),
        ],
        out_specs=pl.BlockSpec((TM_PROJ, n), lambda i: (i, 0)),
        out_shape=jax.ShapeDtypeStruct((t, n), F32),
        compiler_params=pltpu.CompilerParams(
            dimension_semantics=("arbitrary",), vmem_limit_bytes=VMEM_LIMIT),
        name="rms_in_proj",
    )(x2, g, w)


def _mixer_kernel(chunk_decay,
                  proj_ref, cos_ref, sin_ref, intra_ref, qd_ref, kd_ref, tri_ref,
                  rng_ref, cw_ref, cb_ref, wq_ref, wk_ref, bi_ref, bf_ref,
                  mng_ref, skip_ref,
                  y_ref,
                  rstate_ref, cstate_ref, m_ref, tail_ref):
    j = pl.program_id(1)
    n_chunks = L_BLOCK // CHUNK

    @pl.when(j == 0)
    def _():
        rstate_ref[...] = jnp.zeros_like(rstate_ref)
        cstate_ref[...] = jnp.zeros_like(cstate_ref)
        m_ref[...] = jnp.zeros_like(m_ref)
        tail_ref[...] = jnp.zeros_like(tail_ref)

    cosv = cos_ref[...]
    sinv = sin_ref[...]
    k_scale = RET_DK ** -0.5

    for h in range(RET_HEADS):
        lo = h * HEAD_W
        q = proj_ref[:, OFF_RQ + lo:OFF_RQ + lo + HEAD_W]
        k = proj_ref[:, OFF_RK + lo:OFF_RK + lo + HEAD_W]
        q = q * cosv + pltpu.roll(q, HEAD_W // 2, 1) * sinv
        k = (k * cosv + pltpu.roll(k, HEAD_W // 2, 1) * sinv) * k_scale
        intra = intra_ref[h]
        qd = qd_ref[h]
        kd = kd_ref[h]
        g = rng_ref[:, lo:lo + HEAD_W]
        for c in range(n_chunks):
            r0 = c * CHUNK
            qc = q[r0:r0 + CHUNK]
            kc = k[r0:r0 + CHUNK]
            vc = proj_ref[r0:r0 + CHUNK, OFF_RV + lo:OFF_RV + lo + HEAD_W]
            gate = proj_ref[r0:r0 + CHUNK, OFF_RG + lo:OFF_RG + lo + HEAD_W]
            state = rstate_ref[h]
            scores = _dot_nt(qc, kc) * intra
            o = _dot(scores, vc) + _dot(qc, state) * qd
            rstate_ref[h] = chunk_decay[h] * state + _dot_tn(kc * kd, vc)
            y_ref[r0:r0 + CHUNK, lo:lo + HEAD_W] = _silu(gate) * _group_norm(o, g)

    mx = proj_ref[:, OFF_MX:OFF_MX + D_GROUP]
    xp = jnp.concatenate([tail_ref[...], mx], axis=0)
    base = CONV_TAIL - (CONV_W - 1)
    conv = cb_ref[...] + xp[base:base + L_BLOCK] * cw_ref[0:1, :]
    for t in range(1, CONV_W):
        conv = conv + xp[base + t:base + t + L_BLOCK] * cw_ref[t:t + 1, :]
    tail_ref[...] = mx[L_BLOCK - CONV_TAIL:L_BLOCK]
    cact = _silu(conv)

    gi = proj_ref[:, OFF_GI:OFF_GI + LANES] + bi_ref[...]
    gf = proj_ref[:, OFF_GF:OFF_GF + LANES] + bf_ref[...]
    logf = jnp.minimum(gf, 0.0) - jnp.log1p(jnp.exp(-jnp.abs(gf)))
    tri = tri_ref[...]
    lane = lax.broadcasted_iota(jnp.int32, (CHUNK, HEAD_W), 1)
    ones_col = (lane == 0).astype(F32)
    m_scale = MLSTM_D ** -0.5

    mq = []
    mk = []
    for h in range(MLSTM_HEADS):
        lo = h * HEAD_W
        ch = cact[:, lo:lo + HEAD_W]
        mq.append(_dot(ch, wq_ref[h]))
        mk.append(_dot(ch, wk_ref[h]) * m_scale)

    for c in range(n_chunks):
        r0 = c * CHUNK
        fcum = jnp.dot(tri, logf[r0:r0 + CHUNK], preferred_element_type=F32,
                       precision=lax.Precision.HIGHEST)
        a = fcum[CHUNK - 1:CHUNK]
        logw = a - fcum + gi[r0:r0 + CHUNK]
        m_in = jnp.max(logw, axis=0, keepdims=True)
        m_old = m_ref[...]
        m_new = jnp.maximum(a + m_old, m_in)
        decay = jnp.exp(a + m_old - m_new)
        w = jnp.exp(logw - m_new)
        em = jnp.exp(-m_new)
        m_ref[...] = m_new
        for h in range(MLSTM_HEADS):
            lo = h * HEAD_W
            qh = mq[h][r0:r0 + CHUNK]
            kh = mk[h][r0:r0 + CHUNK]
            vh = proj_ref[r0:r0 + CHUNK, OFF_MV + lo:OFF_MV + lo + HEAD_W]
            oh = proj_ref[r0:r0 + CHUNK, OFF_MO + lo:OFF_MO + lo + HEAD_W]
            ch = cact[r0:r0 + CHUNK, lo:lo + HEAD_W]
            v_ext = jnp.concatenate([vh, ones_col], axis=1)
            cmem = decay[:, h:h + 1] * cstate_ref[h] + _dot_tn(kh * w[:, h:h + 1], v_ext)
            cstate_ref[h] = cmem
            num = _dot(qh, cmem)
            den = jnp.maximum(jnp.abs(num[:, HEAD_W:HEAD_W + 1]), em[:, h:h + 1])
            hh = num[:, :HEAD_W] / den
            hm = _group_norm(hh, mng_ref[:, lo:lo + HEAD_W])
            ym = _sigmoid(oh) * (hm + skip_ref[:, lo:lo + HEAD_W] * ch)
            y_ref[r0:r0 + CHUNK, D_GROUP + lo:D_GROUP + lo + HEAD_W] = ym


def _retention_tables():
    h = np.arange(RET_HEADS, dtype=np.float64)
    log_g = np.log(1.0 - 2.0 ** (-5.0 - h))
    l = np.arange(CHUNK, dtype=np.float64)
    intra = np.exp(log_g[:, None, None] * np.abs(l[:, None] - l[None, :]))
    qd = np.exp(log_g[:, None] * (l + 1.0))[:, :, None] * np.ones((1, 1, HEAD_W))
    kd = np.exp(log_g[:, None] * (CHUNK - 1.0 - l))[:, :, None] * np.ones((1, 1, HEAD_W))
    cd = np.exp(log_g * CHUNK)
    return (jnp.asarray(intra, F32), jnp.asarray(qd, F32), jnp.asarray(kd, F32),
            tuple(float(v) for v in cd))


def _rotary_tables(s):
    half = HEAD_W // 2
    inv = ROPE_BASE ** (-np.arange(half, dtype=np.float64) / half)
    ang = np.arange(s, dtype=np.float64)[:, None] * inv[None, :]
    cos = np.cos(ang)
    sin = np.sin(ang)
    return (jnp.asarray(np.concatenate([cos, cos], axis=-1), F32),
            jnp.asarray(np.concatenate([-sin, sin], axis=-1), F32))


def _mixer(proj, b, s, ret_norm_g, m_conv_w, m_conv_b, m_wq, m_wk, m_b_i, m_b_f,
           m_norm_g, m_skip):
    intra, qd, kd, cd = _retention_tables()
    cos_t, sin_t = _rotary_tables(s)
    tri = jnp.asarray(np.tril(np.ones((CHUNK, CHUNK))), F32)
    nj = s // L_BLOCK
    pad = lambda v: jnp.pad(v.reshape(1, -1), ((0, 0), (0, LANES - v.size)))
    full = lambda shape: pl.BlockSpec(shape, lambda bi, j: (0,) * len(shape))
    return pl.pallas_call(
        functools.partial(_mixer_kernel, cd),
        grid=(b, nj),
        in_specs=[
            pl.BlockSpec((L_BLOCK, D_PROJ_PAD), lambda bi, j: (bi * nj + j, 0)),
            pl.BlockSpec((L_BLOCK, HEAD_W), lambda bi, j: (j, 0)),
            pl.BlockSpec((L_BLOCK, HEAD_W), lambda bi, j: (j, 0)),
            full((RET_HEADS, CHUNK, CHUNK)),
            full((RET_HEADS, CHUNK, HEAD_W)),
            full((RET_HEADS, CHUNK, HEAD_W)),
            full((CHUNK, CHUNK)),
            full((1, D_GROUP)),
            full((CONV_W, D_GROUP)),
            full((1, D_GROUP)),
            full((MLSTM_HEADS, MLSTM_D, MLSTM_D)),
            full((MLSTM_HEADS, MLSTM_D, MLSTM_D)),
            full((1, LANES)),
            full((1, LANES)),
            full((1, D_GROUP)),
            full((1, D_GROUP)),
        ],
        out_specs=pl.BlockSpec((L_BLOCK, 2 * D_GROUP), lambda bi, j: (bi * nj + j, 0)),
        out_shape=jax.ShapeDtypeStruct((b * s, 2 * D_GROUP), F32),
        scratch_shapes=[
            pltpu.VMEM((RET_HEADS, RET_DK, HEAD_W), F32),
            pltpu.VMEM((MLSTM_HEADS, MLSTM_D, 2 * HEAD_W), F32),
            pltpu.VMEM((1, LANES), F32),
            pltpu.VMEM((CONV_TAIL, D_GROUP), F32),
        ],
        compiler_params=pltpu.CompilerParams(
            dimension_semantics=("arbitrary", "arbitrary"), vmem_limit_bytes=VMEM_LIMIT),
        name="retention_mlstm_mixer",
    )(proj, cos_t, sin_t, intra, qd, kd, tri,
      ret_norm_g.reshape(1, D_GROUP), m_conv_w, m_conv_b.reshape(1, D_GROUP),
      m_wq.astype(BF16), m_wk.astype(BF16), pad(m_b_i), pad(m_b_f),
      m_norm_g.reshape(1, D_GROUP), m_skip.reshape(1, D_GROUP))


def _post_kernel(y_ref, x_ref, wout_ref, g2_ref, wq_ref, keys_ref,
                 x1_ref, xn_ref, sub_ref):
    x1 = x_ref[...] + jnp.dot(y_ref[...].astype(BF16), wout_ref[...],
                              preferred_element_type=F32)
    x1_ref[...] = x1
    xn = _rms(x1, g2_ref[...])
    xn_ref[...] = xn
    q = jnp.dot(xn.astype(BF16), wq_ref[...], preferred_element_type=F32)
    for i in range(2 * PEER_HEADS):
        lo = i * PEER_HALF
        sub_ref[lo:lo + PEER_NKEYS, :] = _dot_nt(keys_ref[i], q[:, lo:lo + PEER_HALF])


def _post(y, x2, row0, w_out, norm2_g, peer_wq, keys):
    t, d = y.shape
    nq = peer_wq.shape[1]
    b0 = row0 // TM_PROJ
    row = lambda w: pl.BlockSpec((TM_PROJ, w), lambda i: (i, 0))
    full = lambda shape: pl.BlockSpec(shape, lambda i: (0,) * len(shape))
    return pl.pallas_call(
        _post_kernel,
        grid=(t // TM_PROJ,),
        in_specs=[row(d), pl.BlockSpec((TM_PROJ, d), lambda i: (i + b0, 0)), full((d, d)),
                  full((1, d)), full((d, nq)), full(keys.shape)],
        out_specs=[row(d), row(d), pl.BlockSpec((nq, TM_PROJ), lambda i: (0, i))],
        out_shape=[jax.ShapeDtypeStruct((t, d), F32), jax.ShapeDtypeStruct((t, d), F32),
                   jax.ShapeDtypeStruct((nq, t), F32)],
        compiler_params=pltpu.CompilerParams(
            dimension_semantics=("arbitrary",), vmem_limit_bytes=VMEM_LIMIT),
        name="out_proj_peer_scores",
    )(y, x2, w_out, norm2_g, peer_wq, keys)


def _final_kernel(x1_ref, p_ref, g_ref, *rest):
    o_ref = rest[-1]
    o_ref[...] = _rms(x1_ref[...] + p_ref[...], g_ref[...])


def _final(parts, g, t):
    d = g.shape[1]
    spec = lambda r0: pl.BlockSpec((TM_FINAL, d), lambda i, b0=r0 // TM_FINAL: (i + b0, 0))
    out = None
    row0 = 0
    for x1, x_row0, p, p_row0, rows in parts:
        in_specs = [spec(x_row0), spec(p_row0), pl.BlockSpec((1, d), lambda i: (0, 0))]
        args = [x1, p, g]
        if out is not None:
            in_specs.append(pl.BlockSpec(memory_space=pl.ANY))
            args.append(out)
        out = pl.pallas_call(
            _final_kernel,
            grid=(rows // TM_FINAL,),
            in_specs=in_specs,
            out_specs=spec(row0),
            out_shape=jax.ShapeDtypeStruct((t, d), F32),
            input_output_aliases={} if len(args) == 3 else {3: 0},
            compiler_params=pltpu.CompilerParams(dimension_semantics=("arbitrary",)),
            name="final_rmsnorm",
        )(*args)
        row0 += rows
    assert row0 == t
    return out


TM_ROUTE = 128
CAND_ROWS = 56


def _candidate_tables():
    pairs = [(k1, k2) for k1 in range(PEER_TOPK) for k2 in range(PEER_TOPK)
             if (k1 + 1) * (k2 + 1) <= PEER_TOPK]
    assert len(pairs) <= CAND_ROWS
    rep = np.zeros((CAND_ROWS, 2 * PEER_TOPK), np.float32)
    til = np.zeros((CAND_ROWS, 2 * PEER_TOPK), np.float32)
    for pos, (k1, k2) in enumerate(pairs):
        rep[pos, k1] = 1.0
        til[pos, PEER_TOPK + k2] = 1.0
    return jnp.asarray(rep), jnp.asarray(til), len(pairs)


def _dot_exact(a, b):
    return jnp.dot(a, b, preferred_element_type=F32, precision=lax.Precision.HIGHEST)


def _route_kernel(n_cand, sub_ref, rep_ref, til_ref, seg_ref, exp_ref, gate_ref, offs_ref,
                  s_ref, i_ref, best_ref, eid_ref, m0_ref):
    tm = sub_ref.shape[1]
    neg = -jnp.inf
    key_id = lax.broadcasted_iota(jnp.int32, (PEER_NKEYS, tm), 0).astype(F32)
    cand_id = lax.broadcasted_iota(jnp.int32, (CAND_ROWS, tm), 0).astype(F32)
    rep = rep_ref[...]
    til = til_ref[...]
    for h in range(PEER_HEADS):
        for p in range(2):
            lo = (2 * h + p) * PEER_NKEYS
            x = sub_ref[lo:lo + PEER_NKEYS, :]
            for k in range(PEER_TOPK):
                m = jnp.max(x, axis=0, keepdims=True)
                idx = jnp.min(jnp.where(x == m, key_id, float(PEER_NKEYS)), axis=0, keepdims=True)
                x = jnp.where(key_id == idx, neg, x)
                r = p * PEER_TOPK + k
                s_ref[r:r + 1, :] = m
                i_ref[r:r + 1, :] = idx
        s = s_ref[...]
        iv = i_ref[...]
        cand = _dot_exact(rep, s) + _dot_exact(til, s)
        cand = jnp.where(cand_id < float(n_cand), cand, neg)
        eid = _dot(rep, iv) * float(PEER_NKEYS) + _dot(til, iv)
        for k in range(PEER_TOPK):
            m = jnp.max(cand, axis=0, keepdims=True)
            pos = jnp.min(jnp.where(cand == m, cand_id, float(CAND_ROWS)), axis=0, keepdims=True)
            sel = cand_id == pos
            e = jnp.max(jnp.where(sel, eid, -1.0), axis=0, keepdims=True)
            cand = jnp.where(sel, neg, cand)
            r = h * PEER_TOPK + k
            best_ref[r:r + 1, :] = m
            eid_ref[r:r + 1, :] = e
            if k == 0:
                m0_ref[r:r + PEER_TOPK, :] = jnp.broadcast_to(m, (PEER_TOPK, tm))
    pexp = jnp.exp(best_ref[...] - m0_ref[...])
    gates = pexp / _dot_exact(seg_ref[...], pexp)
    gate_ref[...] = gates.T
    eid = eid_ref[...].astype(jnp.int32)
    exp_ref[...] = eid.T
    offs_ref[0] = eid * ROW_WORDS


def _peer_route(sub_t):
    nq, t = sub_t.shape
    rep, til, n_cand = _candidate_tables()
    e = PEER_HEADS * PEER_TOPK
    seg = jnp.asarray(np.kron(np.eye(PEER_HEADS), np.ones((PEER_TOPK, PEER_TOPK))), F32)
    full = lambda a: pl.BlockSpec(a.shape, lambda i: (0,) * a.ndim)
    out = pl.BlockSpec((TM_ROUTE, e), lambda i: (i, 0))
    return pl.pallas_call(
        functools.partial(_route_kernel, n_cand),
        grid=(t // TM_ROUTE,),
        in_specs=[pl.BlockSpec((nq, TM_ROUTE), lambda i: (0, i)), full(rep), full(til), full(seg)],
        out_specs=[out, out, pl.BlockSpec((1, e, TM_ROUTE), lambda i: (i, 0, 0))],
        out_shape=[jax.ShapeDtypeStruct((t, e), jnp.int32), jax.ShapeDtypeStruct((t, e), F32),
                   jax.ShapeDtypeStruct((t // TM_ROUTE, e, TM_ROUTE), jnp.int32)],
        scratch_shapes=[
            pltpu.VMEM((2 * PEER_TOPK, TM_ROUTE), F32),
            pltpu.VMEM((2 * PEER_TOPK, TM_ROUTE), F32),
            pltpu.VMEM((e, TM_ROUTE), F32),
            pltpu.VMEM((e, TM_ROUTE), F32),
            pltpu.VMEM((e, TM_ROUTE), F32),
        ],
        compiler_params=pltpu.CompilerParams(dimension_semantics=("arbitrary",)),
        name="peer_topk_route",
    )(sub_t, rep, til, seg)


SC_CORES = 2
SC_SUBCORES = 16
SC_LANES = 16
SC_WORKERS = SC_CORES * SC_SUBCORES
PEER_E = PEER_HEADS * PEER_TOPK
SC_ROWS = 32
SC_GATHERS = PEER_E // SC_ROWS
SC_TOK_BLOCK = 8
SC_QUARTER = 256


def _sc_mesh():
    return plsc.VectorSubcoreMesh(core_axis_name="c", subcore_axis_name="s")


def _sc_pipeline(idx_v, tab_hbm, rows_v, sems, compute):
    def gather(step, buf):
        return pltpu.make_async_copy(tab_hbm.at[idx_v.at[step]], rows_v.at[buf], sems.at[buf])

    gather(0, 0).start()

    @pl.loop(0, SC_TOK_BLOCK)
    def _(i):
        for c in range(SC_GATHERS):
            buf = c % 2
            step = i * SC_GATHERS + c
            gather(step, buf).wait()
            if c + 1 < SC_GATHERS:
                gather(step + 1, 1 - buf).start()
            else:
                @pl.when(i + 1 < SC_TOK_BLOCK)
                def _():
                    gather(step + 1, 1 - buf).start()
            compute(i, c, step, buf)


def _peer_down_dots(xn, idx, down, tok_start):
    t_all, d = xn.shape
    t = t_all - tok_start
    tok_w = t // SC_WORKERS
    nblk = tok_w // SC_TOK_BLOCK
    assert nblk * SC_TOK_BLOCK * SC_WORKERS == t
    lanes = SC_LANES

    @functools.partial(
        pl.kernel, mesh=_sc_mesh(),
        out_type=jax.ShapeDtypeStruct((t, PEER_E), F32),
        scratch_types=[
            pltpu.VMEM((SC_TOK_BLOCK * SC_GATHERS, SC_ROWS), jnp.int32),
            pltpu.VMEM((SC_TOK_BLOCK, d), F32),
            pltpu.VMEM((2, SC_ROWS, d), F32),
            pltpu.VMEM((SC_ROWS, lanes), F32),
            pltpu.VMEM((SC_TOK_BLOCK, PEER_E), F32),
            pltpu.SemaphoreType.DMA((2,)),
        ],
        compiler_params=pltpu.CompilerParams(needs_layout_passes=False),
        name="peer_down_dots",
    )
    def k(x_hbm, idx_hbm, tab_hbm, out_hbm, idx_v, x_v, rows_v, acc_v, out_v, sems):
        wid = lax.axis_index("s") * SC_CORES + lax.axis_index("c")
        lane = lax.iota(jnp.int32, lanes)

        @pl.loop(0, nblk)
        def _(blk):
            out0 = wid * tok_w + blk * SC_TOK_BLOCK
            tok0 = tok_start + out0
            pltpu.sync_copy(idx_hbm.at[pl.ds(tok0 * SC_GATHERS, SC_TOK_BLOCK * SC_GATHERS)], idx_v)
            pltpu.sync_copy(x_hbm.at[pl.ds(tok0, SC_TOK_BLOCK)], x_v)

            def compute(i, c, step, buf):
                def body(dc, accs):
                    xv = x_v[i, pl.ds(dc * lanes, lanes)]
                    return tuple(a + rows_v[buf, r, pl.ds(dc * lanes, lanes)] * xv
                                 for r, a in enumerate(accs))
                accs = lax.fori_loop(0, d // lanes, body,
                                     tuple(jnp.zeros((lanes,), F32) for _ in range(SC_ROWS)))
                for r in range(SC_ROWS):
                    acc_v[r, :] = accs[r]
                for g in range(SC_ROWS // lanes):
                    tot = jnp.zeros((lanes,), F32)
                    for l in range(lanes):
                        tot = tot + plsc.load_gather(
                            acc_v, [lane + g * lanes, jnp.full((lanes,), l, jnp.int32)])
                    out_v[i, pl.ds(c * SC_ROWS + g * lanes, lanes)] = tot

            _sc_pipeline(idx_v, tab_hbm, rows_v, sems, compute)
            pltpu.sync_copy(out_v, out_hbm.at[pl.ds(out0, SC_TOK_BLOCK)])

    return k(xn, idx.reshape(t_all * SC_GATHERS, SC_ROWS), down)


def _peer_up_sum(coef, idx, up, tok_start):
    t_all = coef.shape[0]
    t = t_all - tok_start
    d = up.shape[1]
    tok_w = t // SC_WORKERS
    nblk = tok_w // SC_TOK_BLOCK
    assert nblk * SC_TOK_BLOCK * SC_WORKERS == t
    lanes = SC_LANES
    nj = SC_QUARTER // lanes

    @functools.partial(
        pl.kernel, mesh=_sc_mesh(),
        out_type=jax.ShapeDtypeStruct((t, d), F32),
        scratch_types=[
            pltpu.VMEM((SC_TOK_BLOCK * SC_GATHERS, SC_ROWS), jnp.int32),
            pltpu.VMEM((SC_TOK_BLOCK * PEER_E,), F32),
            pltpu.VMEM((2, SC_ROWS, d), F32),
            pltpu.VMEM((SC_TOK_BLOCK, d), F32),
            pltpu.SemaphoreType.DMA((2,)),
        ],
        compiler_params=pltpu.CompilerParams(needs_layout_passes=False),
        name="peer_up_sum",
    )
    def k(coef_hbm, idx_hbm, tab_hbm, out_hbm, idx_v, coef_v, rows_v, out_v, sems):
        wid = lax.axis_index("s") * SC_CORES + lax.axis_index("c")

        @pl.loop(0, nblk)
        def _(blk):
            out0 = wid * tok_w + blk * SC_TOK_BLOCK
            tok0 = tok_start + out0
            pltpu.sync_copy(idx_hbm.at[pl.ds(tok0 * SC_GATHERS, SC_TOK_BLOCK * SC_GATHERS)], idx_v)
            pltpu.sync_copy(coef_hbm.at[pl.ds(tok0 * PEER_E, SC_TOK_BLOCK * PEER_E)], coef_v)

            def compute(i, c, step, buf):
                for q in range(d // SC_QUARTER):
                    col = lambda j: pl.ds(q * SC_QUARTER + j * lanes, lanes)
                    if c == 0:
                        init = tuple(jnp.zeros((lanes,), F32) for _ in range(nj))
                    else:
                        init = tuple(out_v[i, col(j)] for j in range(nj))

                    def body(r, accs):
                        cf = plsc.load_gather(
                            coef_v, [jnp.full((lanes,), step * SC_ROWS + r, jnp.int32)])
                        return tuple(a + rows_v[buf, r, col(j)] * cf for j, a in enumerate(accs))
                    accs = lax.fori_loop(0, SC_ROWS, body, init)
                    for j in range(nj):
                        out_v[i, col(j)] = accs[j]

            _sc_pipeline(idx_v, tab_hbm, rows_v, sems, compute)
            pltpu.sync_copy(out_v, out_hbm.at[pl.ds(out0, SC_TOK_BLOCK)])

    return k(coef.reshape(t_all * PEER_E), idx.reshape(t_all * SC_GATHERS, SC_ROWS), up)


TC_SUB = 128
TC_SUBS = 2
TC_STEP = TC_SUB * TC_SUBS
TC_GROUP = 8
ROW_WORDS = 4
ROW_HALVES = 2 * ROW_WORDS
STACK_COLS = PEER_E * ROW_HALVES
FRONT_PARTS = 4
SC_DOWN_PARTS = 2
PEER_TC_UP_TOKENS = 23040


def _pack_rows_bf16(tab):
    n, d = tab.shape
    return pl.pallas_call(
        _pack_kernel,
        grid=(n // PACK_ROWS,),
        in_specs=[pl.BlockSpec((PACK_ROWS, d), lambda i: (i, 0))],
        out_specs=pl.BlockSpec((PACK_ROWS * ROW_WORDS, LANES), lambda i: (i, 0)),
        out_shape=jax.ShapeDtypeStruct((n * ROW_WORDS, LANES), jnp.uint32),
        compiler_params=pltpu.CompilerParams(dimension_semantics=("arbitrary",)),
        name="pack_rows_bf16",
    )(tab)


PACK_ROWS = 512


def _pack_kernel(x_ref, o_ref):
    u = pltpu.bitcast(x_ref[...], jnp.uint32)
    r = (u + jnp.uint32(0x7FFF) + ((u >> 16) & jnp.uint32(1))) >> 16
    for s in range(ROW_WORDS):
        lo = r[:, 2 * LANES * s:2 * LANES * s + LANES]
        hi = r[:, 2 * LANES * s + LANES:2 * LANES * (s + 1)]
        o_ref[pl.ds(s, PACK_ROWS, stride=ROW_WORDS), :] = lo | (hi << 16)


def _load_table_once(tab_hbm, tab_v, sem):
    @pl.when(pl.program_id(0) == 0)
    def _():
        cp = pltpu.make_async_copy(tab_hbm, tab_v, sem)
        cp.start()
        cp.wait()


def _tc_token_pipeline(idx_hbm, idx_s, isem, tab_v, stack_ref, compute):
    step = pl.program_id(0)

    def idx_copy(block, slot):
        return pltpu.make_async_copy(idx_hbm.at[block], idx_s[slot], isem.at[slot])

    @pl.when(step == 0)
    def _():
        idx_copy(0, 0).start()

    for sub in range(TC_SUBS):
        block = step * TC_SUBS + sub
        idx_copy(block, sub).wait()
        if sub + 1 < TC_SUBS:
            idx_copy(block + 1, sub + 1).start()
        else:
            @pl.when(step + 1 < pl.num_programs(0))
            def _():
                idx_copy(block + 1, 0).start()
        offs = idx_s[sub]

        def copy_rows(i, buf):
            for k in range(PEER_E):
                row0 = offs.at[k][i]
                stack_ref[buf, ROW_WORDS * k:ROW_WORDS * (k + 1), :] = (
                    tab_v[pl.ds(pl.multiple_of(row0, ROW_WORDS), ROW_WORDS), :])

        for g in range(TC_GROUP):
            copy_rows(g, g)

        def trip(ip, carry):
            i0 = ip * 2 * TC_GROUP
            for half in range(2):
                cur = half * TC_GROUP
                nxt = (1 - half) * TC_GROUP
                for g in range(TC_GROUP):
                    copy_rows(jnp.minimum(i0 + cur + TC_GROUP + g, TC_SUB - 1), nxt + g)
                for g in range(TC_GROUP):
                    compute(sub * TC_SUB + i0 + cur + g, cur + g)
            return carry
        lax.fori_loop(0, TC_SUB // (2 * TC_GROUP), trip, 0)


def _tc_peer_scratch(tabp):
    return [pltpu.VMEM(tabp.shape, jnp.uint32),
            pltpu.VMEM((2 * TC_GROUP, PEER_E * ROW_WORDS, LANES), jnp.uint32),
            [pltpu.SMEM((PEER_E, TC_SUB), jnp.int32) for _ in range(TC_SUBS)],
            pltpu.SemaphoreType.DMA,
            pltpu.SemaphoreType.DMA((TC_SUBS,))]


def _stack_selectors():
    col = np.arange(STACK_COLS)
    diag = (col[None, :] % ROW_HALVES == np.arange(ROW_HALVES)[:, None]).astype(np.float32)
    owner = (col[:, None] // ROW_HALVES == np.arange(PEER_E)[None, :]).astype(np.float32)
    return jnp.asarray(diag), jnp.asarray(owner)


def _tc_down_kernel(idx_hbm, x_ref, tab_hbm, diag_ref, fold_ref, o_ref,
                    tab_v, stack_ref, idx_s, sem, isem, z_ref):
    _load_table_once(tab_hbm, tab_v, sem)

    def compute(t, buf):
        stack = pltpu.bitcast(stack_ref[buf], BF16)
        xr = x_ref[pl.ds(t, 1), :]
        x8 = jnp.concatenate([xr[:, LANES * j:LANES * (j + 1)] for j in range(ROW_HALVES)],
                             axis=0).astype(BF16)
        a = lax.dot_general(x8, stack, (((1,), (1,)), ((), ())), preferred_element_type=F32)
        z_ref[pl.ds(t, 1), :] = jnp.sum(a * diag_ref[...], axis=0, keepdims=True)

    _tc_token_pipeline(idx_hbm, idx_s, isem, tab_v, stack_ref, compute)
    o_ref[...] = _dot_exact(z_ref[...], fold_ref[...])


def _peer_down_dots_tc(xn, offs, tabp, n_tok):
    diag, owner = _stack_selectors()
    full = lambda a: pl.BlockSpec(a.shape, lambda i: (0,) * a.ndim)
    return pl.pallas_call(
        _tc_down_kernel,
        grid=(n_tok // TC_STEP,),
        in_specs=[pl.BlockSpec(memory_space=pl.ANY),
                  pl.BlockSpec((TC_STEP, xn.shape[1]), lambda i: (i, 0)),
                  pl.BlockSpec(memory_space=pl.ANY), full(diag), full(owner)],
        out_specs=pl.BlockSpec((TC_STEP, PEER_E), lambda i: (i, 0)),
        out_shape=jax.ShapeDtypeStruct((n_tok, PEER_E), F32),
        scratch_shapes=_tc_peer_scratch(tabp) + [pltpu.VMEM((TC_STEP, STACK_COLS), F32)],
        compiler_params=pltpu.CompilerParams(
            dimension_semantics=("arbitrary",), vmem_limit_bytes=VMEM_LIMIT),
        name="peer_down_dots_tc",
    )(offs, xn, tabp, diag, owner)


def _tc_up_kernel(idx_hbm, coef_ref, tab_hbm, spread_ref, diag_ref, o_ref,
                  tab_v, stack_ref, idx_s, sem, isem, ce_ref):
    _load_table_once(tab_hbm, tab_v, sem)
    c = coef_ref[...]
    c_hi = c.astype(BF16)
    c_lo = (c - c_hi.astype(F32)).astype(BF16)
    ce_ref[0] = jnp.dot(c_hi, spread_ref[...], preferred_element_type=F32)
    ce_ref[1] = jnp.dot(c_lo, spread_ref[...], preferred_element_type=F32)

    def compute(t, buf):
        stack = pltpu.bitcast(stack_ref[buf], BF16)
        d = diag_ref[...]
        lhs = jnp.concatenate([(ce_ref[0, pl.ds(t, 1), :] * d).astype(BF16),
                               (ce_ref[1, pl.ds(t, 1), :] * d).astype(BF16)], axis=0)
        r = jnp.dot(lhs, stack, preferred_element_type=F32)
        out8 = r[0:ROW_HALVES] + r[ROW_HALVES:2 * ROW_HALVES]
        o_ref[pl.ds(t, 1), :] = jnp.concatenate(
            [out8[j:j + 1, :] for j in range(ROW_HALVES)], axis=1)

    _tc_token_pipeline(idx_hbm, idx_s, isem, tab_v, stack_ref, compute)


def _peer_up_sum_tc(coef, offs, tabp, n_tok):
    d = ROW_HALVES * LANES
    diag, owner = _stack_selectors()
    spread = owner.T.astype(BF16)
    full = lambda a: pl.BlockSpec(a.shape, lambda i: (0,) * a.ndim)
    return pl.pallas_call(
        _tc_up_kernel,
        grid=(n_tok // TC_STEP,),
        in_specs=[pl.BlockSpec(memory_space=pl.ANY),
                  pl.BlockSpec((TC_STEP, PEER_E), lambda i: (i, 0)),
                  pl.BlockSpec(memory_space=pl.ANY), full(spread), full(diag)],
        out_specs=pl.BlockSpec((TC_STEP, d), lambda i: (i, 0)),
        out_shape=jax.ShapeDtypeStruct((n_tok, d), F32),
        scratch_shapes=_tc_peer_scratch(tabp) + [pltpu.VMEM((2, TC_STEP, STACK_COLS), F32)],
        compiler_params=pltpu.CompilerParams(
            dimension_semantics=("arbitrary",), vmem_limit_bytes=VMEM_LIMIT),
        name="peer_up_sum_tc",
    )(offs, coef, tabp, spread, diag)


def _coef_kernel(pre_ref, gate_ref, o_ref):
    x = pre_ref[...]
    gelu = 0.5 * x * (1.0 + lax.erf(x * (2.0 ** -0.5)))
    o_ref[...] = gate_ref[...] * gelu


def _peer_coef(pre, gates):
    t, e = pre.shape
    row = pl.BlockSpec((TM_FINAL, e), lambda i: (i, 0))
    return pl.pallas_call(
        _coef_kernel,
        grid=(t // TM_FINAL,),
        in_specs=[row, row],
        out_specs=row,
        out_shape=jax.ShapeDtypeStruct((t, e), F32),
        compiler_params=pltpu.CompilerParams(dimension_semantics=("arbitrary",)),
        name="peer_gate_gelu",
    )(pre, gates)


def _front_half(x2, row0, rows, bsz, seq, w_pad, layer, i):
    (norm1_g, ret_norm_g, m_conv_w, m_conv_b, m_wq, m_wk, m_b_i, m_b_f, m_norm_g, m_skip,
     w_out, norm2_g, peer_wq, keys) = layer
    d = x2.shape[1]
    proj = _rms_proj(x2, row0, rows, norm1_g[i].reshape(1, d), w_pad)
    y = _mixer(proj, bsz, seq, ret_norm_g[i], m_conv_w[i], m_conv_b[i], m_wq[i], m_wk[i],
               m_b_i[i], m_b_f[i], m_norm_g[i], m_skip[i])
    x1, xn, sub_t = _post(y, x2, row0, w_out, norm2_g[i].reshape(1, d), peer_wq, keys)
    return (x1, xn) + tuple(_peer_route(sub_t))


def kernel(x, norm1_g, w_in, ret_norm_g, m_conv_w, m_conv_b, m_wq, m_wk, m_b_i, m_b_f,
           m_norm_g, m_skip, w_out, norm2_g, peer_wq, peer_keys, peer_down, peer_up, final_g):
    b, s, d = x.shape
    t = b * s
    tp = t // FRONT_PARTS
    x2 = x.reshape(t, d)
    depth = norm1_g.shape[0]
    for i in range(depth):
        if i > 0:
            x2 = jnp.concatenate(x1s, axis=0) + jnp.concatenate([p_tc, p_sc], axis=0)
        wi = w_in[i]
        n_main = OFF_GI
        gate_pad = lambda w: jnp.pad(w, ((0, 0), (0, LANES - w.shape[1])))
        w_pad = jnp.concatenate(
            [wi[:, :n_main], gate_pad(wi[:, n_main:n_main + MLSTM_HEADS]),
             gate_pad(wi[:, n_main + MLSTM_HEADS:])], axis=1).astype(BF16)
        keys = peer_keys[i].reshape(2 * PEER_HEADS, PEER_NKEYS, PEER_HALF).astype(BF16)
        layer = (norm1_g, ret_norm_g, m_conv_w, m_conv_b, m_wq, m_wk, m_b_i, m_b_f, m_norm_g,
                 m_skip, w_out[i].astype(BF16), norm2_g, peer_wq[i].astype(BF16), keys)
        down_p = _pack_rows_bf16(peer_down[i])
        up_p = _pack_rows_bf16(peer_up[i])

        fronts = []
        pre = []
        for q in range(FRONT_PARTS):
            fr = _front_half(x2, q * tp, tp, b // FRONT_PARTS, s, w_pad, layer, i)
            fronts.append(fr)
            if q < SC_DOWN_PARTS:
                pre.append(_peer_down_dots(fr[1], fr[2], peer_down[i], 0))
        for q in range(SC_DOWN_PARTS, FRONT_PARTS):
            pre.append(_peer_down_dots_tc(fronts[q][1], fronts[q][4], down_p, tp))
        x1s = [fr[0] for fr in fronts]
        expert = jnp.concatenate([fr[2] for fr in fronts], axis=0)
        gates = jnp.concatenate([fr[3] for fr in fronts], axis=0)
        offs = jnp.concatenate([fr[4] for fr in fronts], axis=0)
        coef = _peer_coef(jnp.concatenate(pre, axis=0), gates)
        n_tc = PEER_TC_UP_TOKENS
        p_sc = _peer_up_sum(coef, expert, peer_up[i], n_tc)
        p_tc = _peer_up_sum_tc(coef, offs, up_p, n_tc)
    out = _final(_row_ranges(x1s, [p_tc, p_sc]), final_g.reshape(1, d), t)
    return out.reshape(b, s, d)


def _row_ranges(xs, ps):
    def starts(arrs):
        out = [0]
        for a in arrs:
            out.append(out[-1] + a.shape[0])
        return out
    xb, pb = starts(xs), starts(ps)
    assert xb[-1] == pb[-1]
    cuts = sorted(set(xb) | set(pb))
    parts = []
    for lo, hi in zip(cuts[:-1], cuts[1:]):
        xi = max(k for k in range(len(xs)) if xb[k] <= lo)
        pi = max(k for k in range(len(ps)) if pb[k] <= lo)
        parts.append((xs[xi], lo - xb[xi], ps[pi], lo - pb[pi], hi - lo))
    return parts
```

```python
import functools

import numpy as np
import jax
import jax.numpy as jnp
from jax import lax
from jax.experimental import pallas as pl
from jax.experimental.pallas import tpu as pltpu
from jax.experimental.pallas import tpu_sc as plsc

F32 = jnp.float32
BF16 = jnp.bfloat16

D_MODEL = 1024
CHUNK = 64
RET_HEADS = 4
RET_DK = 128
MLSTM_HEADS = 4
MLSTM_D = 128
HEAD_W = 128
D_GROUP = 512
CONV_W = 4
ROPE_BASE = 10000.0
PEER_HEADS = 8
PEER_NKEYS = 128
PEER_TOPK = 16
PEER_HALF = 128
PEER_BLOCK = 128
EPS = 1e-6

OFF_RQ, OFF_RK, OFF_RV, OFF_RG = 0, 512, 1024, 1536
OFF_MX, OFF_MV, OFF_MO = 2048, 2560, 3072
OFF_GI, OFF_GF = 3584, 3712
D_PROJ_PAD = 3840

LANES = 128
CONV_TAIL = 8
VMEM_LIMIT = 56 * 1024 * 1024

TM_PROJ = 256
L_BLOCK = 256
TM_FINAL = 512


def _rms(x, g):
    ms = jnp.mean(x * x, axis=-1, keepdims=True)
    return x * lax.rsqrt(ms + EPS) * g


def _group_norm(h, g):
    mu = jnp.mean(h, axis=-1, keepdims=True)
    d = h - mu
    var = jnp.mean(d * d, axis=-1, keepdims=True)
    return d * lax.rsqrt(var + EPS) * g


def _silu(x):
    return x * (1.0 / (1.0 + jnp.exp(-x)))


def _sigmoid(x):
    return 1.0 / (1.0 + jnp.exp(-x))


def _dot(a, b):
    return jnp.dot(a.astype(BF16), b.astype(BF16), preferred_element_type=F32)


def _dot_tn(a, b):
    return lax.dot_general(a.astype(BF16), b.astype(BF16), (((0,), (0,)), ((), ())),
                           preferred_element_type=F32)


def _dot_nt(a, b):
    return lax.dot_general(a.astype(BF16), b.astype(BF16), (((1,), (1,)), ((), ())),
                           preferred_element_type=F32)


def _rms_proj_kernel(x_ref, g_ref, w_ref, o_ref):
    h = _rms(x_ref[...], g_ref[...])
    o_ref[...] = jnp.dot(h.astype(BF16), w_ref[...], preferred_element_type=F32)


def _rms_proj(x2, row0, rows, g, w):
    d = x2.shape[1]
    t = rows
    n = w.shape[1]
    b0 = row0 // TM_PROJ
    return pl.pallas_call(
        _rms_proj_kernel,
        grid=(t // TM_PROJ,),
        in_specs=[
            pl.BlockSpec((TM_PROJ, d), lambda i: (i + b0, 0)),
            pl.BlockSpec((1, d), lambda i: (0, 0)),
            pl.BlockSpec((d, n), lambda i: (0, 0)),
        ],
        out_specs=pl.BlockSpec((TM_PROJ, n), lambda i: (i, 0)),
        out_shape=jax.ShapeDtypeStruct((t, n), F32),
        compiler_params=pltpu.CompilerParams(
            dimension_semantics=("arbitrary",), vmem_limit_bytes=VMEM_LIMIT),
        name="rms_in_proj",
    )(x2, g, w)


def _mixer_kernel(chunk_decay,
                  proj_ref, cos_ref, sin_ref, intra_ref, qd_ref, kd_ref, tri_ref,
                  rng_ref, cw_ref, cb_ref, wq_ref, wk_ref, bi_ref, bf_ref,
                  mng_ref, skip_ref,
                  y_ref,
                  rstate_ref, cstate_ref, m_ref, tail_ref):
    j = pl.program_id(1)
    n_chunks = L_BLOCK // CHUNK

    @pl.when(j == 0)
    def _():
        rstate_ref[...] = jnp.zeros_like(rstate_ref)
        cstate_ref[...] = jnp.zeros_like(cstate_ref)
        m_ref[...] = jnp.zeros_like(m_ref)
        tail_ref[...] = jnp.zeros_like(tail_ref)

    cosv = cos_ref[...]
    sinv = sin_ref[...]
    k_scale = RET_DK ** -0.5

    for h in range(RET_HEADS):
        lo = h * HEAD_W
        q = proj_ref[:, OFF_RQ + lo:OFF_RQ + lo + HEAD_W]
        k = proj_ref[:, OFF_RK + lo:OFF_RK + lo + HEAD_W]
        q = q * cosv + pltpu.roll(q, HEAD_W // 2, 1) * sinv
        k = (k * cosv + pltpu.roll(k, HEAD_W // 2, 1) * sinv) * k_scale
        intra = intra_ref[h]
        qd = qd_ref[h]
        kd = kd_ref[h]
        g = rng_ref[:, lo:lo + HEAD_W]
        for c in range(n_chunks):
            r0 = c * CHUNK
            qc = q[r0:r0 + CHUNK]
            kc = k[r0:r0 + CHUNK]
            vc = proj_ref[r0:r0 + CHUNK, OFF_RV + lo:OFF_RV + lo + HEAD_W]
            gate = proj_ref[r0:r0 + CHUNK, OFF_RG + lo:OFF_RG + lo + HEAD_W]
            state = rstate_ref[h]
            scores = _dot_nt(qc, kc) * intra
            o = _dot(scores, vc) + _dot(qc, state) * qd
            rstate_ref[h] = chunk_decay[h] * state + _dot_tn(kc * kd, vc)
            y_ref[r0:r0 + CHUNK, lo:lo + HEAD_W] = _silu(gate) * _group_norm(o, g)

    mx = proj_ref[:, OFF_MX:OFF_MX + D_GROUP]
    xp = jnp.concatenate([tail_ref[...], mx], axis=0)
    base = CONV_TAIL - (CONV_W - 1)
    conv = cb_ref[...] + xp[base:base + L_BLOCK] * cw_ref[0:1, :]
    for t in range(1, CONV_W):
        conv = conv + xp[base + t:base + t + L_BLOCK] * cw_ref[t:t + 1, :]
    tail_ref[...] = mx[L_BLOCK - CONV_TAIL:L_BLOCK]
    cact = _silu(conv)

    gi = proj_ref[:, OFF_GI:OFF_GI + LANES] + bi_ref[...]
    gf = proj_ref[:, OFF_GF:OFF_GF + LANES] + bf_ref[...]
    logf = jnp.minimum(gf, 0.0) - jnp.log1p(jnp.exp(-jnp.abs(gf)))
    tri = tri_ref[...]
    lane = lax.broadcasted_iota(jnp.int32, (CHUNK, HEAD_W), 1)
    ones_col = (lane == 0).astype(F32)
    m_scale = MLSTM_D ** -0.5

    mq = []
    mk = []
    for h in range(MLSTM_HEADS):
        lo = h * HEAD_W
        ch = cact[:, lo:lo + HEAD_W]
        mq.append(_dot(ch, wq_ref[h]))
        mk.append(_dot(ch, wk_ref[h]) * m_scale)

    for c in range(n_chunks):
        r0 = c * CHUNK
        fcum = jnp.dot(tri, logf[r0:r0 + CHUNK], preferred_element_type=F32,
                       precision=lax.Precision.HIGHEST)
        a = fcum[CHUNK - 1:CHUNK]
        logw = a - fcum + gi[r0:r0 + CHUNK]
        m_in = jnp.max(logw, axis=0, keepdims=True)
        m_old = m_ref[...]
        m_new = jnp.maximum(a + m_old, m_in)
        decay = jnp.exp(a + m_old - m_new)
        w = jnp.exp(logw - m_new)
        em = jnp.exp(-m_new)
        m_ref[...] = m_new
        for h in range(MLSTM_HEADS):
            lo = h * HEAD_W
            qh = mq[h][r0:r0 + CHUNK]
            kh = mk[h][r0:r0 + CHUNK]
            vh = proj_ref[r0:r0 + CHUNK, OFF_MV + lo:OFF_MV + lo + HEAD_W]
            oh = proj_ref[r0:r0 + CHUNK, OFF_MO + lo:OFF_MO + lo + HEAD_W]
            ch = cact[r0:r0 + CHUNK, lo:lo + HEAD_W]
            v_ext = jnp.concatenate([vh, ones_col], axis=1)
            cmem = decay[:, h:h + 1] * cstate_ref[h] + _dot_tn(kh * w[:, h:h + 1], v_ext)
            cstate_ref[h] = cmem
            num = _dot(qh, cmem)
            den = jnp.maximum(jnp.abs(num[:, HEAD_W:HEAD_W + 1]), em[:, h:h + 1])
            hh = num[:, :HEAD_W] / den
            hm = _group_norm(hh, mng_ref[:, lo:lo + HEAD_W])
            ym = _sigmoid(oh) * (hm + skip_ref[:, lo:lo + HEAD_W] * ch)
            y_ref[r0:r0 + CHUNK, D_GROUP + lo:D_GROUP + lo + HEAD_W] = ym


def _retention_tables():
    h = np.arange(RET_HEADS, dtype=np.float64)
    log_g = np.log(1.0 - 2.0 ** (-5.0 - h))
    l = np.arange(CHUNK, dtype=np.float64)
    intra = np.exp(log_g[:, None, None] * np.abs(l[:, None] - l[None, :]))
    qd = np.exp(log_g[:, None] * (l + 1.0))[:, :, None] * np.ones((1, 1, HEAD_W))
    kd = np.exp(log_g[:, None] * (CHUNK - 1.0 - l))[:, :, None] * np.ones((1, 1, HEAD_W))
    cd = np.exp(log_g * CHUNK)
    return (jnp.asarray(intra, F32), jnp.asarray(qd, F32), jnp.asarray(kd, F32),
            tuple(float(v) for v in cd))


def _rotary_tables(s):
    half = HEAD_W // 2
    inv = ROPE_BASE ** (-np.arange(half, dtype=np.float64) / half)
    ang = np.arange(s, dtype=np.float64)[:, None] * inv[None, :]
    cos = np.cos(ang)
    sin = np.sin(ang)
    return (jnp.asarray(np.concatenate([cos, cos], axis=-1), F32),
            jnp.asarray(np.concatenate([-sin, sin], axis=-1), F32))


def _mixer(proj, b, s, ret_norm_g, m_conv_w, m_conv_b, m_wq, m_wk, m_b_i, m_b_f,
           m_norm_g, m_skip):
    intra, qd, kd, cd = _retention_tables()
    cos_t, sin_t = _rotary_tables(s)
    tri = jnp.asarray(np.tril(np.ones((CHUNK, CHUNK))), F32)
    nj = s // L_BLOCK
    pad = lambda v: jnp.pad(v.reshape(1, -1), ((0, 0), (0, LANES - v.size)))
    full = lambda shape: pl.BlockSpec(shape, lambda bi, j: (0,) * len(shape))
    return pl.pallas_call(
        functools.partial(_mixer_kernel, cd),
        grid=(b, nj),
        in_specs=[
            pl.BlockSpec((L_BLOCK, D_PROJ_PAD), lambda bi, j: (bi * nj + j, 0)),
            pl.BlockSpec((L_BLOCK, HEAD_W), lambda bi, j: (j, 0)),
            pl.BlockSpec((L_BLOCK, HEAD_W), lambda bi, j: (j, 0)),
            full((RET_HEADS, CHUNK, CHUNK)),
            full((RET_HEADS, CHUNK, HEAD_W)),
            full((RET_HEADS, CHUNK, HEAD_W)),
            full((CHUNK, CHUNK)),
            full((1, D_GROUP)),
            full((CONV_W, D_GROUP)),
            full((1, D_GROUP)),
            full((MLSTM_HEADS, MLSTM_D, MLSTM_D)),
            full((MLSTM_HEADS, MLSTM_D, MLSTM_D)),
            full((1, LANES)),
            full((1, LANES)),
            full((1, D_GROUP)),
            full((1, D_GROUP)),
        ],
        out_specs=pl.BlockSpec((L_BLOCK, 2 * D_GROUP), lambda bi, j: (bi * nj + j, 0)),
        out_shape=jax.ShapeDtypeStruct((b * s, 2 * D_GROUP), F32),
        scratch_shapes=[
            pltpu.VMEM((RET_HEADS, RET_DK, HEAD_W), F32),
            pltpu.VMEM((MLSTM_HEADS, MLSTM_D, 2 * HEAD_W), F32),
            pltpu.VMEM((1, LANES), F32),
            pltpu.VMEM((CONV_TAIL, D_GROUP), F32),
        ],
        compiler_params=pltpu.CompilerParams(
            dimension_semantics=("arbitrary", "arbitrary"), vmem_limit_bytes=VMEM_LIMIT),
        name="retention_mlstm_mixer",
    )(proj, cos_t, sin_t, intra, qd, kd, tri,
      ret_norm_g.reshape(1, D_GROUP), m_conv_w, m_conv_b.reshape(1, D_GROUP),
      m_wq.astype(BF16), m_wk.astype(BF16), pad(m_b_i), pad(m_b_f),
      m_norm_g.reshape(1, D_GROUP), m_skip.reshape(1, D_GROUP))


def _post_kernel(y_ref, x_ref, wout_ref, g2_ref, wq_ref, keys_ref,
                 x1_ref, xn_ref, sub_ref):
    x1 = x_ref[...] + jnp.dot(y_ref[...].astype(BF16), wout_ref[...],
                              preferred_element_type=F32)
    x1_ref[...] = x1
    xn = _rms(x1, g2_ref[...])
    xn_ref[...] = xn
    q = jnp.dot(xn.astype(BF16), wq_ref[...], preferred_element_type=F32)
    for i in range(2 * PEER_HEADS):
        lo = i * PEER_HALF
        sub_ref[lo:lo + PEER_NKEYS, :] = _dot_nt(keys_ref[i], q[:, lo:lo + PEER_HALF])


def _post(y, x2, row0, w_out, norm2_g, peer_wq, keys):
    t, d = y.shape
    nq = peer_wq.shape[1]
    b0 = row0 // TM_PROJ
    row = lambda w: pl.BlockSpec((TM_PROJ, w), lambda i: (i, 0))
    full = lambda shape: pl.BlockSpec(shape, lambda i: (0,) * len(shape))
    return pl.pallas_call(
        _post_kernel,
        grid=(t // TM_PROJ,),
        in_specs=[row(d), pl.BlockSpec((TM_PROJ, d), lambda i: (i + b0, 0)), full((d, d)),
                  full((1, d)), full((d, nq)), full(keys.shape)],
        out_specs=[row(d), row(d), pl.BlockSpec((nq, TM_PROJ), lambda i: (0, i))],
        out_shape=[jax.ShapeDtypeStruct((t, d), F32), jax.ShapeDtypeStruct((t, d), F32),
                   jax.ShapeDtypeStruct((nq, t), F32)],
        compiler_params=pltpu.CompilerParams(
            dimension_semantics=("arbitrary",), vmem_limit_bytes=VMEM_LIMIT),
        name="out_proj_peer_scores",
    )(y, x2, w_out, norm2_g, peer_wq, keys)


def _final_kernel(x1_ref, p_ref, g_ref, *rest):
    o_ref = rest[-1]
    o_ref[...] = _rms(x1_ref[...] + p_ref[...], g_ref[...])


def _final(parts, g, t):
    d = g.shape[1]
    spec = lambda r0: pl.BlockSpec((TM_FINAL, d), lambda i, b0=r0 // TM_FINAL: (i + b0, 0))
    out = None
    row0 = 0
    for x1, x_row0, p, p_row0, rows in parts:
        in_specs = [spec(x_row0), spec(p_row0), pl.BlockSpec((1, d), lambda i: (0, 0))]
        args = [x1, p, g]
        if out is not None:
            in_specs.append(pl.BlockSpec(memory_space=pl.ANY))
            args.append(out)
        out = pl.pallas_call(
            _final_kernel,
            grid=(rows // TM_FINAL,),
            in_specs=in_specs,
            out_specs=spec(row0),
            out_shape=jax.ShapeDtypeStruct((t, d), F32),
            input_output_aliases={} if len(args) == 3 else {3: 0},
            compiler_params=pltpu.CompilerParams(dimension_semantics=("arbitrary",)),
            name="final_rmsnorm",
        )(*args)
        row0 += rows
    assert row0 == t
    return out


TM_ROUTE = 128
CAND_ROWS = 56


def _candidate_tables():
    pairs = [(k1, k2) for k1 in range(PEER_TOPK) for k2 in range(PEER_TOPK)
             if (k1 + 1) * (k2 + 1) <= PEER_TOPK]
    assert len(pairs) <= CAND_ROWS
    rep = np.zeros((CAND_ROWS, 2 * PEER_TOPK), np.float32)
    til = np.zeros((CAND_ROWS, 2 * PEER_TOPK), np.float32)
    for pos, (k1, k2) in enumerate(pairs):
        rep[pos, k1] = 1.0
        til[pos, PEER_TOPK + k2] = 1.0
    return jnp.asarray(rep), jnp.asarray(til), len(pairs)


def _dot_exact(a, b):
    return jnp.dot(a, b, preferred_element_type=F32, precision=lax.Precision.HIGHEST)


def _route_kernel(n_cand, sub_ref, rep_ref, til_ref, seg_ref, exp_ref, gate_ref, offs_ref,
                  s_ref, i_ref, best_ref, eid_ref, m0_ref):
    tm = sub_ref.shape[1]
    neg = -jnp.inf
    key_id = lax.broadcasted_iota(jnp.int32, (PEER_NKEYS, tm), 0).astype(F32)
    cand_id = lax.broadcasted_iota(jnp.int32, (CAND_ROWS, tm), 0).astype(F32)
    rep = rep_ref[...]
    til = til_ref[...]
    for h in range(PEER_HEADS):
        for p in range(2):
            lo = (2 * h + p) * PEER_NKEYS
            x = sub_ref[lo:lo + PEER_NKEYS, :]
            for k in range(PEER_TOPK):
                m = jnp.max(x, axis=0, keepdims=True)
                idx = jnp.min(jnp.where(x == m, key_id, float(PEER_NKEYS)), axis=0, keepdims=True)
                x = jnp.where(key_id == idx, neg, x)
                r = p * PEER_TOPK + k
                s_ref[r:r + 1, :] = m
                i_ref[r:r + 1, :] = idx
        s = s_ref[...]
        iv = i_ref[...]
        cand = _dot_exact(rep, s) + _dot_exact(til, s)
        cand = jnp.where(cand_id < float(n_cand), cand, neg)
        eid = _dot(rep, iv) * float(PEER_NKEYS) + _dot(til, iv)
        for k in range(PEER_TOPK):
            m = jnp.max(cand, axis=0, keepdims=True)
            pos = jnp.min(jnp.where(cand == m, cand_id, float(CAND_ROWS)), axis=0, keepdims=True)
            sel = cand_id == pos
            e = jnp.max(jnp.where(sel, eid, -1.0), axis=0, keepdims=True)
            cand = jnp.where(sel, neg, cand)
            r = h * PEER_TOPK + k
            best_ref[r:r + 1, :] = m
            eid_ref[r:r + 1, :] = e
            if k == 0:
                m0_ref[r:r + PEER_TOPK, :] = jnp.broadcast_to(m, (PEER_TOPK, tm))
    pexp = jnp.exp(best_ref[...] - m0_ref[...])
    gates = pexp / _dot_exact(seg_ref[...], pexp)
    gate_ref[...] = gates.T
    eid = eid_ref[...].astype(jnp.int32)
    exp_ref[...] = eid.T
    offs_ref[0] = eid * ROW_WORDS


def _peer_route(sub_t):
    nq, t = sub_t.shape
    rep, til, n_cand = _candidate_tables()
    e = PEER_HEADS * PEER_TOPK
    seg = jnp.asarray(np.kron(np.eye(PEER_HEADS), np.ones((PEER_TOPK, PEER_TOPK))), F32)
    full = lambda a: pl.BlockSpec(a.shape, lambda i: (0,) * a.ndim)
    out = pl.BlockSpec((TM_ROUTE, e), lambda i: (i, 0))
    return pl.pallas_call(
        functools.partial(_route_kernel, n_cand),
        grid=(t // TM_ROUTE,),
        in_specs=[pl.BlockSpec((nq, TM_ROUTE), lambda i: (0, i)), full(rep), full(til), full(seg)],
        out_specs=[out, out, pl.BlockSpec((1, e, TM_ROUTE), lambda i: (i, 0, 0))],
        out_shape=[jax.ShapeDtypeStruct((t, e), jnp.int32), jax.ShapeDtypeStruct((t, e), F32),
                   jax.ShapeDtypeStruct((t // TM_ROUTE, e, TM_ROUTE), jnp.int32)],
        scratch_shapes=[
            pltpu.VMEM((2 * PEER_TOPK, TM_ROUTE), F32),
            pltpu.VMEM((2 * PEER_TOPK, TM_ROUTE), F32),
            pltpu.VMEM((e, TM_ROUTE), F32),
            pltpu.VMEM((e, TM_ROUTE), F32),
            pltpu.VMEM((e, TM_ROUTE), F32),
        ],
        compiler_params=pltpu.CompilerParams(dimension_semantics=("arbitrary",)),
        name="peer_topk_route",
    )(sub_t, rep, til, seg)


SC_CORES = 2
SC_SUBCORES = 16
SC_LANES = 16
SC_WORKERS = SC_CORES * SC_SUBCORES
PEER_E = PEER_HEADS * PEER_TOPK
SC_ROWS = 32
SC_GATHERS = PEER_E // SC_ROWS
SC_TOK_BLOCK = 16
SC_QUARTER = 256


def _sc_mesh():
    return plsc.VectorSubcoreMesh(core_axis_name="c", subcore_axis_name="s")


def _sc_pipeline(idx_v, tab_hbm, rows_v, sems, compute):
    def gather(step, buf):
        return pltpu.make_async_copy(tab_hbm.at[idx_v.at[step]], rows_v.at[buf], sems.at[buf])

    gather(0, 0).start()

    @pl.loop(0, SC_TOK_BLOCK)
    def _(i):
        for c in range(SC_GATHERS):
            buf = c % 2
            step = i * SC_GATHERS + c
            gather(step, buf).wait()
            if c + 1 < SC_GATHERS:
                gather(step + 1, 1 - buf).start()
            else:
                @pl.when(i + 1 < SC_TOK_BLOCK)
                def _():
                    gather(step + 1, 1 - buf).start()
            compute(i, c, step, buf)


def _peer_down_dots(xn, idx, down, tok_start):
    t_all, d = xn.shape
    t = t_all - tok_start
    tok_w = t // SC_WORKERS
    nblk = tok_w // SC_TOK_BLOCK
    assert nblk * SC_TOK_BLOCK * SC_WORKERS == t
    lanes = SC_LANES

    @functools.partial(
        pl.kernel, mesh=_sc_mesh(),
        out_type=jax.ShapeDtypeStruct((t, PEER_E), F32),
        scratch_types=[
            pltpu.VMEM((SC_TOK_BLOCK * SC_GATHERS, SC_ROWS), jnp.int32),
            pltpu.VMEM((SC_TOK_BLOCK, d), F32),
            pltpu.VMEM((2, SC_ROWS, d), F32),
            pltpu.VMEM((SC_ROWS, lanes), F32),
            pltpu.VMEM((SC_TOK_BLOCK, PEER_E), F32),
            pltpu.SemaphoreType.DMA((2,)),
        ],
        compiler_params=pltpu.CompilerParams(needs_layout_passes=False),
        name="peer_down_dots",
    )
    def k(x_hbm, idx_hbm, tab_hbm, out_hbm, idx_v, x_v, rows_v, acc_v, out_v, sems):
        wid = lax.axis_index("s") * SC_CORES + lax.axis_index("c")
        lane = lax.iota(jnp.int32, lanes)

        @pl.loop(0, nblk)
        def _(blk):
            out0 = wid * tok_w + blk * SC_TOK_BLOCK
            tok0 = tok_start + out0
            pltpu.sync_copy(idx_hbm.at[pl.ds(tok0 * SC_GATHERS, SC_TOK_BLOCK * SC_GATHERS)], idx_v)
            pltpu.sync_copy(x_hbm.at[pl.ds(tok0, SC_TOK_BLOCK)], x_v)

            def compute(i, c, step, buf):
                def body(dc, accs):
                    xv = x_v[i, pl.ds(dc * lanes, lanes)]
                    return tuple(a + rows_v[buf, r, pl.ds(dc * lanes, lanes)] * xv
                                 for r, a in enumerate(accs))
                accs = lax.fori_loop(0, d // lanes, body,
                                     tuple(jnp.zeros((lanes,), F32) for _ in range(SC_ROWS)))
                for r in range(SC_ROWS):
                    acc_v[r, :] = accs[r]
                for g in range(SC_ROWS // lanes):
                    tot = jnp.zeros((lanes,), F32)
                    for l in range(lanes):
                        tot = tot + plsc.load_gather(
                            acc_v, [lane + g * lanes, jnp.full((lanes,), l, jnp.int32)])
                    out_v[i, pl.ds(c * SC_ROWS + g * lanes, lanes)] = tot

            _sc_pipeline(idx_v, tab_hbm, rows_v, sems, compute)
            pltpu.sync_copy(out_v, out_hbm.at[pl.ds(out0, SC_TOK_BLOCK)])

    return k(xn, idx.reshape(t_all * SC_GATHERS, SC_ROWS), down)


def _peer_up_sum(coef, idx, up, tok_start):
    t_all = coef.shape[0]
    t = t_all - tok_start
    d = up.shape[1]
    tok_w = t // SC_WORKERS
    nblk = tok_w // SC_TOK_BLOCK
    assert nblk * SC_TOK_BLOCK * SC_WORKERS == t
    lanes = SC_LANES
    nj = SC_QUARTER // lanes

    @functools.partial(
        pl.kernel, mesh=_sc_mesh(),
        out_type=jax.ShapeDtypeStruct((t, d), F32),
        scratch_types=[
            pltpu.VMEM((SC_TOK_BLOCK * SC_GATHERS, SC_ROWS), jnp.int32),
            pltpu.VMEM((SC_TOK_BLOCK * PEER_E,), F32),
            pltpu.VMEM((2, SC_ROWS, d), F32),
            pltpu.VMEM((SC_TOK_BLOCK, d), F32),
            pltpu.SemaphoreType.DMA((2,)),
        ],
        compiler_params=pltpu.CompilerParams(needs_layout_passes=False),
        name="peer_up_sum",
    )
    def k(coef_hbm, idx_hbm, tab_hbm, out_hbm, idx_v, coef_v, rows_v, out_v, sems):
        wid = lax.axis_index("s") * SC_CORES + lax.axis_index("c")

        @pl.loop(0, nblk)
        def _(blk):
            out0 = wid * tok_w + blk * SC_TOK_BLOCK
            tok0 = tok_start + out0
            pltpu.sync_copy(idx_hbm.at[pl.ds(tok0 * SC_GATHERS, SC_TOK_BLOCK * SC_GATHERS)], idx_v)
            pltpu.sync_copy(coef_hbm.at[pl.ds(tok0 * PEER_E, SC_TOK_BLOCK * PEER_E)], coef_v)

            def compute(i, c, step, buf):
                for q in range(d // SC_QUARTER):
                    col = lambda j: pl.ds(q * SC_QUARTER + j * lanes, lanes)
                    if c == 0:
                        init = tuple(jnp.zeros((lanes,), F32) for _ in range(nj))
                    else:
                        init = tuple(out_v[i, col(j)] for j in range(nj))

                    def body(r, accs):
                        cf = plsc.load_gather(
                            coef_v, [jnp.full((lanes,), step * SC_ROWS + r, jnp.int32)])
                        return tuple(a + rows_v[buf, r, col(j)] * cf for j, a in enumerate(accs))
                    accs = lax.fori_loop(0, SC_ROWS, body, init)
                    for j in range(nj):
                        out_v[i, col(j)] = accs[j]

            _sc_pipeline(idx_v, tab_hbm, rows_v, sems, compute)
            pltpu.sync_copy(out_v, out_hbm.at[pl.ds(out0, SC_TOK_BLOCK)])

    return k(coef.reshape(t_all * PEER_E), idx.reshape(t_all * SC_GATHERS, SC_ROWS), up)


TC_SUB = 128
TC_SUBS = 2
TC_STEP = TC_SUB * TC_SUBS
TC_GROUP = 8
ROW_WORDS = 4
ROW_HALVES = 2 * ROW_WORDS
STACK_COLS = PEER_E * ROW_HALVES
FRONT_PARTS = 4
SC_DOWN_PARTS = 2
SC_DOWN_EXTRA = 1024
PEER_TC_UP_TOKENS = 23040


def _pack_rows_bf16(tab):
    n, d = tab.shape
    return pl.pallas_call(
        _pack_kernel,
        grid=(n // PACK_ROWS,),
        in_specs=[pl.BlockSpec((PACK_ROWS, d), lambda i: (i, 0))],
        out_specs=pl.BlockSpec((PACK_ROWS * ROW_WORDS, LANES), lambda i: (i, 0)),
        out_shape=jax.ShapeDtypeStruct((n * ROW_WORDS, LANES), jnp.uint32),
        compiler_params=pltpu.CompilerParams(dimension_semantics=("arbitrary",)),
        name="pack_rows_bf16",
    )(tab)


PACK_ROWS = 512


def _pack_kernel(x_ref, o_ref):
    u = pltpu.bitcast(x_ref[...], jnp.uint32)
    r = (u + jnp.uint32(0x7FFF) + ((u >> 16) & jnp.uint32(1))) >> 16
    for s in range(ROW_WORDS):
        lo = r[:, 2 * LANES * s:2 * LANES * s + LANES]
        hi = r[:, 2 * LANES * s + LANES:2 * LANES * (s + 1)]
        o_ref[pl.ds(s, PACK_ROWS, stride=ROW_WORDS), :] = lo | (hi << 16)


def _load_table_once(tab_hbm, tab_v, sem):
    @pl.when(pl.program_id(0) == 0)
    def _():
        cp = pltpu.make_async_copy(tab_hbm, tab_v, sem)
        cp.start()
        cp.wait()


def _tc_token_pipeline(idx_hbm, idx_s, isem, tab_v, stack_ref, compute):
    step = pl.program_id(0)

    def idx_copy(block, slot):
        return pltpu.make_async_copy(idx_hbm.at[block], idx_s[slot], isem.at[slot])

    @pl.when(step == 0)
    def _():
        idx_copy(0, 0).start()

    for sub in range(TC_SUBS):
        block = step * TC_SUBS + sub
        idx_copy(block, sub).wait()
        if sub + 1 < TC_SUBS:
            idx_copy(block + 1, sub + 1).start()
        else:
            @pl.when(step + 1 < pl.num_programs(0))
            def _():
                idx_copy(block + 1, 0).start()
        offs = idx_s[sub]

        def copy_rows(i, buf):
            for k in range(PEER_E):
                row0 = offs.at[k][i]
                stack_ref[buf, ROW_WORDS * k:ROW_WORDS * (k + 1), :] = (
                    tab_v[pl.ds(pl.multiple_of(row0, ROW_WORDS), ROW_WORDS), :])

        for g in range(TC_GROUP):
            copy_rows(g, g)

        def trip(ip, carry):
            i0 = ip * 2 * TC_GROUP
            for half in range(2):
                cur = half * TC_GROUP
                nxt = (1 - half) * TC_GROUP
                for g in range(TC_GROUP):
                    copy_rows(jnp.minimum(i0 + cur + TC_GROUP + g, TC_SUB - 1), nxt + g)
                for g in range(TC_GROUP):
                    compute(sub * TC_SUB + i0 + cur + g, cur + g)
            return carry
        lax.fori_loop(0, TC_SUB // (2 * TC_GROUP), trip, 0)


def _tc_peer_scratch(tabp):
    return [pltpu.VMEM(tabp.shape, jnp.uint32),
            pltpu.VMEM((2 * TC_GROUP, PEER_E * ROW_WORDS, LANES), jnp.uint32),
            [pltpu.SMEM((PEER_E, TC_SUB), jnp.int32) for _ in range(TC_SUBS)],
            pltpu.SemaphoreType.DMA,
            pltpu.SemaphoreType.DMA((TC_SUBS,))]


def _stack_selectors():
    col = np.arange(STACK_COLS)
    diag = (col[None, :] % ROW_HALVES == np.arange(ROW_HALVES)[:, None]).astype(np.float32)
    owner = (col[:, None] // ROW_HALVES == np.arange(PEER_E)[None, :]).astype(np.float32)
    return jnp.asarray(diag), jnp.asarray(owner)


def _tc_down_kernel(idx_hbm, x_ref, tab_hbm, diag_ref, fold_ref, o_ref,
                    tab_v, stack_ref, idx_s, sem, isem, z_ref):
    _load_table_once(tab_hbm, tab_v, sem)

    def compute(t, buf):
        stack = pltpu.bitcast(stack_ref[buf], BF16)
        xr = x_ref[pl.ds(t, 1), :]
        x8 = jnp.concatenate([xr[:, LANES * j:LANES * (j + 1)] for j in range(ROW_HALVES)],
                             axis=0).astype(BF16)
        a = lax.dot_general(x8, stack, (((1,), (1,)), ((), ())), preferred_element_type=F32)
        z_ref[pl.ds(t, 1), :] = jnp.sum(a * diag_ref[...], axis=0, keepdims=True)

    _tc_token_pipeline(idx_hbm, idx_s, isem, tab_v, stack_ref, compute)
    o_ref[...] = _dot_exact(z_ref[...], fold_ref[...])


def _peer_down_dots_tc(xn, offs, tabp, n_tok):
    diag, owner = _stack_selectors()
    full = lambda a: pl.BlockSpec(a.shape, lambda i: (0,) * a.ndim)
    return pl.pallas_call(
        _tc_down_kernel,
        grid=(n_tok // TC_STEP,),
        in_specs=[pl.BlockSpec(memory_space=pl.ANY),
                  pl.BlockSpec((TC_STEP, xn.shape[1]), lambda i: (i, 0)),
                  pl.BlockSpec(memory_space=pl.ANY), full(diag), full(owner)],
        out_specs=pl.BlockSpec((TC_STEP, PEER_E), lambda i: (i, 0)),
        out_shape=jax.ShapeDtypeStruct((n_tok, PEER_E), F32),
        scratch_shapes=_tc_peer_scratch(tabp) + [pltpu.VMEM((TC_STEP, STACK_COLS), F32)],
        compiler_params=pltpu.CompilerParams(
            dimension_semantics=("arbitrary",), vmem_limit_bytes=VMEM_LIMIT),
        name="peer_down_dots_tc",
    )(offs, xn, tabp, diag, owner)


def _tc_up_kernel(idx_hbm, coef_ref, tab_hbm, spread_ref, diag_ref, o_ref,
                  tab_v, stack_ref, idx_s, sem, isem, ce_ref):
    _load_table_once(tab_hbm, tab_v, sem)
    c = coef_ref[...]
    c_hi = c.astype(BF16)
    c_lo = (c - c_hi.astype(F32)).astype(BF16)
    ce_ref[0] = jnp.dot(c_hi, spread_ref[...], preferred_element_type=F32)
    ce_ref[1] = jnp.dot(c_lo, spread_ref[...], preferred_element_type=F32)

    def compute(t, buf):
        stack = pltpu.bitcast(stack_ref[buf], BF16)
        d = diag_ref[...]
        lhs = jnp.concatenate([(ce_ref[0, pl.ds(t, 1), :] * d).astype(BF16),
                               (ce_ref[1, pl.ds(t, 1), :] * d).astype(BF16)], axis=0)
        r = jnp.dot(lhs, stack, preferred_element_type=F32)
        out8 = r[0:ROW_HALVES] + r[ROW_HALVES:2 * ROW_HALVES]
        o_ref[pl.ds(t, 1), :] = jnp.concatenate(
            [out8[j:j + 1, :] for j in range(ROW_HALVES)], axis=1)

    _tc_token_pipeline(idx_hbm, idx_s, isem, tab_v, stack_ref, compute)


def _peer_up_sum_tc(coef, offs, tabp, n_tok):
    d = ROW_HALVES * LANES
    diag, owner = _stack_selectors()
    spread = owner.T.astype(BF16)
    full = lambda a: pl.BlockSpec(a.shape, lambda i: (0,) * a.ndim)
    return pl.pallas_call(
        _tc_up_kernel,
        grid=(n_tok // TC_STEP,),
        in_specs=[pl.BlockSpec(memory_space=pl.ANY),
                  pl.BlockSpec((TC_STEP, PEER_E), lambda i: (i, 0)),
                  pl.BlockSpec(memory_space=pl.ANY), full(spread), full(diag)],
        out_specs=pl.BlockSpec((TC_STEP, d), lambda i: (i, 0)),
        out_shape=jax.ShapeDtypeStruct((n_tok, d), F32),
        scratch_shapes=_tc_peer_scratch(tabp) + [pltpu.VMEM((2, TC_STEP, STACK_COLS), F32)],
        compiler_params=pltpu.CompilerParams(
            dimension_semantics=("arbitrary",), vmem_limit_bytes=VMEM_LIMIT),
        name="peer_up_sum_tc",
    )(offs, coef, tabp, spread, diag)


def _coef_kernel(pre_ref, gate_ref, o_ref):
    x = pre_ref[...]
    gelu = 0.5 * x * (1.0 + lax.erf(x * (2.0 ** -0.5)))
    o_ref[...] = gate_ref[...] * gelu


def _peer_coef(pre, gates):
    t, e = pre.shape
    row = pl.BlockSpec((TM_FINAL, e), lambda i: (i, 0))
    return pl.pallas_call(
        _coef_kernel,
        grid=(t // TM_FINAL,),
        in_specs=[row, row],
        out_specs=row,
        out_shape=jax.ShapeDtypeStruct((t, e), F32),
        compiler_params=pltpu.CompilerParams(dimension_semantics=("arbitrary",)),
        name="peer_gate_gelu",
    )(pre, gates)


def _front_half(x2, row0, rows, bsz, seq, w_pad, layer, i):
    (norm1_g, ret_norm_g, m_conv_w, m_conv_b, m_wq, m_wk, m_b_i, m_b_f, m_norm_g, m_skip,
     w_out, norm2_g, peer_wq, keys) = layer
    d = x2.shape[1]
    proj = _rms_proj(x2, row0, rows, norm1_g[i].reshape(1, d), w_pad)
    y = _mixer(proj, bsz, seq, ret_norm_g[i], m_conv_w[i], m_conv_b[i], m_wq[i], m_wk[i],
               m_b_i[i], m_b_f[i], m_norm_g[i], m_skip[i])
    x1, xn, sub_t = _post(y, x2, row0, w_out, norm2_g[i].reshape(1, d), peer_wq, keys)
    return (x1, xn) + tuple(_peer_route(sub_t))


def kernel(x, norm1_g, w_in, ret_norm_g, m_conv_w, m_conv_b, m_wq, m_wk, m_b_i, m_b_f,
           m_norm_g, m_skip, w_out, norm2_g, peer_wq, peer_keys, peer_down, peer_up, final_g):
    b, s, d = x.shape
    t = b * s
    tp = t // FRONT_PARTS
    x2 = x.reshape(t, d)
    depth = norm1_g.shape[0]
    for i in range(depth):
        if i > 0:
            x2 = jnp.concatenate(x1s, axis=0) + jnp.concatenate([p_tc, p_sc], axis=0)
        wi = w_in[i]
        n_main = OFF_GI
        gate_pad = lambda w: jnp.pad(w, ((0, 0), (0, LANES - w.shape[1])))
        w_pad = jnp.concatenate(
            [wi[:, :n_main], gate_pad(wi[:, n_main:n_main + MLSTM_HEADS]),
             gate_pad(wi[:, n_main + MLSTM_HEADS:])], axis=1).astype(BF16)
        keys = peer_keys[i].reshape(2 * PEER_HEADS, PEER_NKEYS, PEER_HALF).astype(BF16)
        layer = (norm1_g, ret_norm_g, m_conv_w, m_conv_b, m_wq, m_wk, m_b_i, m_b_f, m_norm_g,
                 m_skip, w_out[i].astype(BF16), norm2_g, peer_wq[i].astype(BF16), keys)
        down_p = _pack_rows_bf16(peer_down[i])
        up_p = _pack_rows_bf16(peer_up[i])

        fronts = []
        pre = [None] * FRONT_PARTS
        for q in range(FRONT_PARTS):
            fr = _front_half(x2, q * tp, tp, b // FRONT_PARTS, s, w_pad, layer, i)
            fronts.append(fr)
            if q < SC_DOWN_PARTS:
                pre[q] = [_peer_down_dots(fr[1], fr[2], peer_down[i], 0)]
            elif q == SC_DOWN_PARTS:
                pre[q] = [None, _peer_down_dots(fr[1], fr[2], peer_down[i], tp - SC_DOWN_EXTRA)]
        for q in range(SC_DOWN_PARTS, FRONT_PARTS):
            n_tc_q = tp - SC_DOWN_EXTRA if q == SC_DOWN_PARTS else tp
            pre_tc = _peer_down_dots_tc(fronts[q][1], fronts[q][4], down_p, n_tc_q)
            if q == SC_DOWN_PARTS:
                pre[q][0] = pre_tc
            else:
                pre[q] = [pre_tc]
        pre = [piece for part in pre for piece in part]
        x1s = [fr[0] for fr in fronts]
        expert = jnp.concatenate([fr[2] for fr in fronts], axis=0)
        gates = jnp.concatenate([fr[3] for fr in fronts], axis=0)
        offs = jnp.concatenate([fr[4] for fr in fronts], axis=0)
        coef = _peer_coef(jnp.concatenate(pre, axis=0), gates)
        n_tc = PEER_TC_UP_TOKENS
        p_sc = _peer_up_sum(coef, expert, peer_up[i], n_tc)
        p_tc = _peer_up_sum_tc(coef, offs, up_p, n_tc)
    out = _final(_row_ranges(x1s, [p_tc, p_sc]), final_g.reshape(1, d), t)
    return out.reshape(b, s, d)


def _row_ranges(xs, ps):
    def starts(arrs):
        out = [0]
        for a in arrs:
            out.append(out[-1] + a.shape[0])
        return out
    xb, pb = starts(xs), starts(ps)
    assert xb[-1] == pb[-1]
    cuts = sorted(set(xb) | set(pb))
    parts = []
    for lo, hi in zip(cuts[:-1], cuts[1:]):
        xi = max(k for k in range(len(xs)) if xb[k] <= lo)
        pi = max(k for k in range(len(ps)) if pb[k] <= lo)
        parts.append((xs[xi], lo - xb[xi], ps[pi], lo - pb[pi], hi - lo))
    return parts
```

```python
import functools

import numpy as np
import jax
import jax.numpy as jnp
from jax import lax
from jax.experimental import pallas as pl
from jax.experimental.pallas import tpu as pltpu
from jax.experimental.pallas import tpu_sc as plsc

F32 = jnp.float32
BF16 = jnp.bfloat16

D_MODEL = 1024
CHUNK = 64
RET_HEADS = 4
RET_DK = 128
MLSTM_HEADS = 4
MLSTM_D = 128
HEAD_W = 128
D_GROUP = 512
CONV_W = 4
ROPE_BASE = 10000.0
PEER_HEADS = 8
PEER_NKEYS = 128
PEER_TOPK = 16
PEER_HALF = 128
PEER_BLOCK = 128
EPS = 1e-6

OFF_RQ, OFF_RK, OFF_RV, OFF_RG = 0, 512, 1024, 1536
OFF_MX, OFF_MV, OFF_MO = 2048, 2560, 3072
OFF_GI, OFF_GF = 3584, 3712
D_PROJ_PAD = 3840

LANES = 128
CONV_TAIL = 8
VMEM_LIMIT = 56 * 1024 * 1024

TM_PROJ = 256
L_BLOCK = 256
TM_FINAL = 512


def _rms(x, g):
    ms = jnp.mean(x * x, axis=-1, keepdims=True)
    return x * lax.rsqrt(ms + EPS) * g


def _group_norm(h, g):
    mu = jnp.mean(h, axis=-1, keepdims=True)
    d = h - mu
    var = jnp.mean(d * d, axis=-1, keepdims=True)
    return d * lax.rsqrt(var + EPS) * g


def _silu(x):
    return x * (1.0 / (1.0 + jnp.exp(-x)))


def _sigmoid(x):
    return 1.0 / (1.0 + jnp.exp(-x))


def _dot(a, b):
    return jnp.dot(a.astype(BF16), b.astype(BF16), preferred_element_type=F32)


def _dot_tn(a, b):
    return lax.dot_general(a.astype(BF16), b.astype(BF16), (((0,), (0,)), ((), ())),
                           preferred_element_type=F32)


def _dot_nt(a, b):
    return lax.dot_general(a.astype(BF16), b.astype(BF16), (((1,), (1,)), ((), ())),
                           preferred_element_type=F32)


def _rms_proj_kernel(x_ref, g_ref, w_ref, o_ref):
    h = _rms(x_ref[...], g_ref[...])
    o_ref[...] = jnp.dot(h.astype(BF16), w_ref[...], preferred_element_type=F32)


def _rms_proj(x2, row0, rows, g, w):
    d = x2.shape[1]
    t = rows
    n = w.shape[1]
    b0 = row0 // TM_PROJ
    return pl.pallas_call(
        _rms_proj_kernel,
        grid=(t // TM_PROJ,),
        in_specs=[
            pl.BlockSpec((TM_PROJ, d), lambda i: (i + b0, 0)),
            pl.BlockSpec((1, d), lambda i: (0, 0)),
            pl.BlockSpec((d, n), lambda i: (0, 0)),
        ],
        out_specs=pl.BlockSpec((TM_PROJ, n), lambda i: (i, 0)),
        out_shape=jax.ShapeDtypeStruct((t, n), F32),
        compiler_params=pltpu.CompilerParams(
            dimension_semantics=("arbitrary",), vmem_limit_bytes=VMEM_LIMIT),
        name="rms_in_proj",
    )(x2, g, w)


def _mixer_kernel(chunk_decay,
                  proj_ref, cos_ref, sin_ref, intra_ref, qd_ref, kd_ref, tri_ref,
                  rng_ref, cw_ref, cb_ref, wq_ref, wk_ref, bi_ref, bf_ref,
                  mng_ref, skip_ref,
                  y_ref,
                  rstate_ref, cstate_ref, m_ref, tail_ref):
    j = pl.program_id(1)
    n_chunks = L_BLOCK // CHUNK

    @pl.when(j == 0)
    def _():
        rstate_ref[...] = jnp.zeros_like(rstate_ref)
        cstate_ref[...] = jnp.zeros_like(cstate_ref)
        m_ref[...] = jnp.zeros_like(m_ref)
        tail_ref[...] = jnp.zeros_like(tail_ref)

    cosv = cos_ref[...]
    sinv = sin_ref[...]
    k_scale = RET_DK ** -0.5

    for h in range(RET_HEADS):
        lo = h * HEAD_W
        q = proj_ref[:, OFF_RQ + lo:OFF_RQ + lo + HEAD_W]
        k = proj_ref[:, OFF_RK + lo:OFF_RK + lo + HEAD_W]
        q = q * cosv + pltpu.roll(q, HEAD_W // 2, 1) * sinv
        k = (k * cosv + pltpu.roll(k, HEAD_W // 2, 1) * sinv) * k_scale
        intra = intra_ref[h]
        qd = qd_ref[h]
        kd = kd_ref[h]
        g = rng_ref[:, lo:lo + HEAD_W]
        for c in range(n_chunks):
            r0 = c * CHUNK
            qc = q[r0:r0 + CHUNK]
            kc = k[r0:r0 + CHUNK]
            vc = proj_ref[r0:r0 + CHUNK, OFF_RV + lo:OFF_RV + lo + HEAD_W]
            gate = proj_ref[r0:r0 + CHUNK, OFF_RG + lo:OFF_RG + lo + HEAD_W]
            state = rstate_ref[h]
            scores = _dot_nt(qc, kc) * intra
            o = _dot(scores, vc) + _dot(qc, state) * qd
            rstate_ref[h] = chunk_decay[h] * state + _dot_tn(kc * kd, vc)
            y_ref[r0:r0 + CHUNK, lo:lo + HEAD_W] = _silu(gate) * _group_norm(o, g)

    mx = proj_ref[:, OFF_MX:OFF_MX + D_GROUP]
    xp = jnp.concatenate([tail_ref[...], mx], axis=0)
    base = CONV_TAIL - (CONV_W - 1)
    conv = cb_ref[...] + xp[base:base + L_BLOCK] * cw_ref[0:1, :]
    for t in range(1, CONV_W):
        conv = conv + xp[base + t:base + t + L_BLOCK] * cw_ref[t:t + 1, :]
    tail_ref[...] = mx[L_BLOCK - CONV_TAIL:L_BLOCK]
    cact = _silu(conv)

    gi = proj_ref[:, OFF_GI:OFF_GI + LANES] + bi_ref[...]
    gf = proj_ref[:, OFF_GF:OFF_GF + LANES] + bf_ref[...]
    logf = jnp.minimum(gf, 0.0) - jnp.log1p(jnp.exp(-jnp.abs(gf)))
    tri = tri_ref[...]
    lane = lax.broadcasted_iota(jnp.int32, (CHUNK, HEAD_W), 1)
    ones_col = (lane == 0).astype(F32)
    m_scale = MLSTM_D ** -0.5

    mq = []
    mk = []
    for h in range(MLSTM_HEADS):
        lo = h * HEAD_W
        ch = cact[:, lo:lo + HEAD_W]
        mq.append(_dot(ch, wq_ref[h]))
        mk.append(_dot(ch, wk_ref[h]) * m_scale)

    for c in range(n_chunks):
        r0 = c * CHUNK
        fcum = jnp.dot(tri, logf[r0:r0 + CHUNK], preferred_element_type=F32,
                       precision=lax.Precision.HIGHEST)
        a = fcum[CHUNK - 1:CHUNK]
        logw = a - fcum + gi[r0:r0 + CHUNK]
        m_in = jnp.max(logw, axis=0, keepdims=True)
        m_old = m_ref[...]
        m_new = jnp.maximum(a + m_old, m_in)
        decay = jnp.exp(a + m_old - m_new)
        w = jnp.exp(logw - m_new)
        em = jnp.exp(-m_new)
        m_ref[...] = m_new
        for h in range(MLSTM_HEADS):
            lo = h * HEAD_W
            qh = mq[h][r0:r0 + CHUNK]
            kh = mk[h][r0:r0 + CHUNK]
            vh = proj_ref[r0:r0 + CHUNK, OFF_MV + lo:OFF_MV + lo + HEAD_W]
            oh = proj_ref[r0:r0 + CHUNK, OFF_MO + lo:OFF_MO + lo + HEAD_W]
            ch = cact[r0:r0 + CHUNK, lo:lo + HEAD_W]
            v_ext = jnp.concatenate([vh, ones_col], axis=1)
            cmem = decay[:, h:h + 1] * cstate_ref[h] + _dot_tn(kh * w[:, h:h + 1], v_ext)
            cstate_ref[h] = cmem
            num = _dot(qh, cmem)
            den = jnp.maximum(jnp.abs(num[:, HEAD_W:HEAD_W + 1]), em[:, h:h + 1])
            hh = num[:, :HEAD_W] / den
            hm = _group_norm(hh, mng_ref[:, lo:lo + HEAD_W])
            ym = _sigmoid(oh) * (hm + skip_ref[:, lo:lo + HEAD_W] * ch)
            y_ref[r0:r0 + CHUNK, D_GROUP + lo:D_GROUP + lo + HEAD_W] = ym


def _retention_tables():
    h = np.arange(RET_HEADS, dtype=np.float64)
    log_g = np.log(1.0 - 2.0 ** (-5.0 - h))
    l = np.arange(CHUNK, dtype=np.float64)
    intra = np.exp(log_g[:, None, None] * np.abs(l[:, None] - l[None, :]))
    qd = np.exp(log_g[:, None] * (l + 1.0))[:, :, None] * np.ones((1, 1, HEAD_W))
    kd = np.exp(log_g[:, None] * (CHUNK - 1.0 - l))[:, :, None] * np.ones((1, 1, HEAD_W))
    cd = np.exp(log_g * CHUNK)
    return (jnp.asarray(intra, F32), jnp.asarray(qd, F32), jnp.asarray(kd, F32),
            tuple(float(v) for v in cd))


def _rotary_tables(s):
    half = HEAD_W // 2
    inv = ROPE_BASE ** (-np.arange(half, dtype=np.float64) / half)
    ang = np.arange(s, dtype=np.float64)[:, None] * inv[None, :]
    cos = np.cos(ang)
    sin = np.sin(ang)
    return (jnp.asarray(np.concatenate([cos, cos], axis=-1), F32),
            jnp.asarray(np.concatenate([-sin, sin], axis=-1), F32))


def _mixer(proj, b, s, ret_norm_g, m_conv_w, m_conv_b, m_wq, m_wk, m_b_i, m_b_f,
           m_norm_g, m_skip):
    intra, qd, kd, cd = _retention_tables()
    cos_t, sin_t = _rotary_tables(s)
    tri = jnp.asarray(np.tril(np.ones((CHUNK, CHUNK))), F32)
    nj = s // L_BLOCK
    pad = lambda v: jnp.pad(v.reshape(1, -1), ((0, 0), (0, LANES - v.size)))
    full = lambda shape: pl.BlockSpec(shape, lambda bi, j: (0,) * len(shape))
    return pl.pallas_call(
        functools.partial(_mixer_kernel, cd),
        grid=(b, nj),
        in_specs=[
            pl.BlockSpec((L_BLOCK, D_PROJ_PAD), lambda bi, j: (bi * nj + j, 0)),
            pl.BlockSpec((L_BLOCK, HEAD_W), lambda bi, j: (j, 0)),
            pl.BlockSpec((L_BLOCK, HEAD_W), lambda bi, j: (j, 0)),
            full((RET_HEADS, CHUNK, CHUNK)),
            full((RET_HEADS, CHUNK, HEAD_W)),
            full((RET_HEADS, CHUNK, HEAD_W)),
            full((CHUNK, CHUNK)),
            full((1, D_GROUP)),
            full((CONV_W, D_GROUP)),
            full((1, D_GROUP)),
            full((MLSTM_HEADS, MLSTM_D, MLSTM_D)),
            full((MLSTM_HEADS, MLSTM_D, MLSTM_D)),
            full((1, LANES)),
            full((1, LANES)),
            full((1, D_GROUP)),
            full((1, D_GROUP)),
        ],
        out_specs=pl.BlockSpec((L_BLOCK, 2 * D_GROUP), lambda bi, j: (bi * nj + j, 0)),
        out_shape=jax.ShapeDtypeStruct((b * s, 2 * D_GROUP), F32),
        scratch_shapes=[
            pltpu.VMEM((RET_HEADS, RET_DK, HEAD_W), F32),
            pltpu.VMEM((MLSTM_HEADS, MLSTM_D, 2 * HEAD_W), F32),
            pltpu.VMEM((1, LANES), F32),
            pltpu.VMEM((CONV_TAIL, D_GROUP), F32),
        ],
        compiler_params=pltpu.CompilerParams(
            dimension_semantics=("arbitrary", "arbitrary"), vmem_limit_bytes=VMEM_LIMIT),
        name="retention_mlstm_mixer",
    )(proj, cos_t, sin_t, intra, qd, kd, tri,
      ret_norm_g.reshape(1, D_GROUP), m_conv_w, m_conv_b.reshape(1, D_GROUP),
      m_wq.astype(BF16), m_wk.astype(BF16), pad(m_b_i), pad(m_b_f),
      m_norm_g.reshape(1, D_GROUP), m_skip.reshape(1, D_GROUP))


def _post_kernel(y_ref, x_ref, wout_ref, g2_ref, wq_ref, keys_ref,
                 x1_ref, xn_ref, sub_ref):
    x1 = x_ref[...] + jnp.dot(y_ref[...].astype(BF16), wout_ref[...],
                              preferred_element_type=F32)
    x1_ref[...] = x1
    xn = _rms(x1, g2_ref[...])
    xn_ref[...] = xn
    q = jnp.dot(xn.astype(BF16), wq_ref[...], preferred_element_type=F32)
    for i in range(2 * PEER_HEADS):
        lo = i * PEER_HALF
        sub_ref[lo:lo + PEER_NKEYS, :] = _dot_nt(keys_ref[i], q[:, lo:lo + PEER_HALF])


def _post(y, x2, row0, w_out, norm2_g, peer_wq, keys):
    t, d = y.shape
    nq = peer_wq.shape[1]
    b0 = row0 // TM_PROJ
    row = lambda w: pl.BlockSpec((TM_PROJ, w), lambda i: (i, 0))
    full = lambda shape: pl.BlockSpec(shape, lambda i: (0,) * len(shape))
    return pl.pallas_call(
        _post_kernel,
        grid=(t // TM_PROJ,),
        in_specs=[row(d), pl.BlockSpec((TM_PROJ, d), lambda i: (i + b0, 0)), full((d, d)),
                  full((1, d)), full((d, nq)), full(keys.shape)],
        out_specs=[row(d), row(d), pl.BlockSpec((nq, TM_PROJ), lambda i: (0, i))],
        out_shape=[jax.ShapeDtypeStruct((t, d), F32), jax.ShapeDtypeStruct((t, d), F32),
                   jax.ShapeDtypeStruct((nq, t), F32)],
        compiler_params=pltpu.CompilerParams(
            dimension_semantics=("arbitrary",), vmem_limit_bytes=VMEM_LIMIT),
        name="out_proj_peer_scores",
    )(y, x2, w_out, norm2_g, peer_wq, keys)


def _final_kernel(x1_ref, p_ref, g_ref, *rest):
    o_ref = rest[-1]
    o_ref[...] = _rms(x1_ref[...] + p_ref[...], g_ref[...])


def _final(parts, g, t):
    d = g.shape[1]
    spec = lambda r0: pl.BlockSpec((TM_FINAL, d), lambda i, b0=r0 // TM_FINAL: (i + b0, 0))
    out = None
    row0 = 0
    for x1, x_row0, p, p_row0, rows in parts:
        in_specs = [spec(x_row0), spec(p_row0), pl.BlockSpec((1, d), lambda i: (0, 0))]
        args = [x1, p, g]
        if out is not None:
            in_specs.append(pl.BlockSpec(memory_space=pl.ANY))
            args.append(out)
        out = pl.pallas_call(
            _final_kernel,
            grid=(rows // TM_FINAL,),
            in_specs=in_specs,
            out_specs=spec(row0),
            out_shape=jax.ShapeDtypeStruct((t, d), F32),
            input_output_aliases={} if len(args) == 3 else {3: 0},
            compiler_params=pltpu.CompilerParams(dimension_semantics=("arbitrary",)),
            name="final_rmsnorm",
        )(*args)
        row0 += rows
    assert row0 == t
    return out


TM_ROUTE = 256
CAND_ROWS = 56


def _candidate_tables():
    pairs = [(k1, k2) for k1 in range(PEER_TOPK) for k2 in range(PEER_TOPK)
             if (k1 + 1) * (k2 + 1) <= PEER_TOPK]
    assert len(pairs) <= CAND_ROWS
    rep = np.zeros((CAND_ROWS, 2 * PEER_TOPK), np.float32)
    til = np.zeros((CAND_ROWS, 2 * PEER_TOPK), np.float32)
    for pos, (k1, k2) in enumerate(pairs):
        rep[pos, k1] = 1.0
        til[pos, PEER_TOPK + k2] = 1.0
    return jnp.asarray(rep), jnp.asarray(til), len(pairs)


def _dot_exact(a, b):
    return jnp.dot(a, b, preferred_element_type=F32, precision=lax.Precision.HIGHEST)


def _route_kernel(n_cand, sub_ref, rep_ref, til_ref, seg_ref, exp_ref, gate_ref, offs_ref,
                  s_ref, i_ref, best_ref, eid_ref, m0_ref):
    tm = sub_ref.shape[1]
    neg = -jnp.inf
    key_id = lax.broadcasted_iota(jnp.int32, (PEER_NKEYS, tm), 0).astype(F32)
    cand_id = lax.broadcasted_iota(jnp.int32, (CAND_ROWS, tm), 0).astype(F32)
    rep = rep_ref[...]
    til = til_ref[...]
    for h in range(PEER_HEADS):
        for p in range(2):
            lo = (2 * h + p) * PEER_NKEYS
            x = sub_ref[lo:lo + PEER_NKEYS, :]
            for k in range(PEER_TOPK):
                m = jnp.max(x, axis=0, keepdims=True)
                idx = jnp.min(jnp.where(x == m, key_id, float(PEER_NKEYS)), axis=0, keepdims=True)
                x = jnp.where(key_id == idx, neg, x)
                r = p * PEER_TOPK + k
                s_ref[r:r + 1, :] = m
                i_ref[r:r + 1, :] = idx
        s = s_ref[...]
        iv = i_ref[...]
        cand = _dot_exact(rep, s) + _dot_exact(til, s)
        cand = jnp.where(cand_id < float(n_cand), cand, neg)
        eid = _dot(rep, iv) * float(PEER_NKEYS) + _dot(til, iv)
        for k in range(PEER_TOPK):
            m = jnp.max(cand, axis=0, keepdims=True)
            pos = jnp.min(jnp.where(cand == m, cand_id, float(CAND_ROWS)), axis=0, keepdims=True)
            sel = cand_id == pos
            e = jnp.max(jnp.where(sel, eid, -1.0), axis=0, keepdims=True)
            cand = jnp.where(sel, neg, cand)
            r = h * PEER_TOPK + k
            best_ref[r:r + 1, :] = m
            eid_ref[r:r + 1, :] = e
            if k == 0:
                m0_ref[r:r + PEER_TOPK, :] = jnp.broadcast_to(m, (PEER_TOPK, tm))
    pexp = jnp.exp(best_ref[...] - m0_ref[...])
    gates = pexp / _dot_exact(seg_ref[...], pexp)
    gate_ref[...] = gates.T
    eid = eid_ref[...].astype(jnp.int32)
    exp_ref[...] = eid.T
    for blk in range(tm // LANES):
        offs_ref[blk] = eid[:, blk * LANES:(blk + 1) * LANES] * ROW_WORDS


def _peer_route(sub_t):
    nq, t = sub_t.shape
    rep, til, n_cand = _candidate_tables()
    e = PEER_HEADS * PEER_TOPK
    seg = jnp.asarray(np.kron(np.eye(PEER_HEADS), np.ones((PEER_TOPK, PEER_TOPK))), F32)
    full = lambda a: pl.BlockSpec(a.shape, lambda i: (0,) * a.ndim)
    out = pl.BlockSpec((TM_ROUTE, e), lambda i: (i, 0))
    return pl.pallas_call(
        functools.partial(_route_kernel, n_cand),
        grid=(t // TM_ROUTE,),
        in_specs=[pl.BlockSpec((nq, TM_ROUTE), lambda i: (0, i)), full(rep), full(til), full(seg)],
        out_specs=[out, out, pl.BlockSpec((TM_ROUTE // LANES, e, LANES), lambda i: (i, 0, 0))],
        out_shape=[jax.ShapeDtypeStruct((t, e), jnp.int32), jax.ShapeDtypeStruct((t, e), F32),
                   jax.ShapeDtypeStruct((t // LANES, e, LANES), jnp.int32)],
        scratch_shapes=[
            pltpu.VMEM((2 * PEER_TOPK, TM_ROUTE), F32),
            pltpu.VMEM((2 * PEER_TOPK, TM_ROUTE), F32),
            pltpu.VMEM((e, TM_ROUTE), F32),
            pltpu.VMEM((e, TM_ROUTE), F32),
            pltpu.VMEM((e, TM_ROUTE), F32),
        ],
        compiler_params=pltpu.CompilerParams(dimension_semantics=("arbitrary",)),
        name="peer_topk_route",
    )(sub_t, rep, til, seg)


SC_CORES = 2
SC_SUBCORES = 16
SC_LANES = 16
SC_WORKERS = SC_CORES * SC_SUBCORES
PEER_E = PEER_HEADS * PEER_TOPK
SC_ROWS = 32
SC_GATHERS = PEER_E // SC_ROWS
SC_TOK_BLOCK = 16
SC_QUARTER = 256


def _sc_mesh():
    return plsc.VectorSubcoreMesh(core_axis_name="c", subcore_axis_name="s")


def _sc_pipeline(idx_v, tab_hbm, rows_v, sems, compute):
    def gather(step, buf):
        return pltpu.make_async_copy(tab_hbm.at[idx_v.at[step]], rows_v.at[buf], sems.at[buf])

    gather(0, 0).start()

    @pl.loop(0, SC_TOK_BLOCK)
    def _(i):
        for c in range(SC_GATHERS):
            buf = c % 2
            step = i * SC_GATHERS + c
            gather(step, buf).wait()
            if c + 1 < SC_GATHERS:
                gather(step + 1, 1 - buf).start()
            else:
                @pl.when(i + 1 < SC_TOK_BLOCK)
                def _():
                    gather(step + 1, 1 - buf).start()
            compute(i, c, step, buf)


def _peer_down_dots(xn, idx, down, tok_start):
    t_all, d = xn.shape
    t = t_all - tok_start
    tok_w = t // SC_WORKERS
    nblk = tok_w // SC_TOK_BLOCK
    assert nblk * SC_TOK_BLOCK * SC_WORKERS == t
    lanes = SC_LANES

    @functools.partial(
        pl.kernel, mesh=_sc_mesh(),
        out_type=jax.ShapeDtypeStruct((t, PEER_E), F32),
        scratch_types=[
            pltpu.VMEM((SC_TOK_BLOCK * SC_GATHERS, SC_ROWS), jnp.int32),
            pltpu.VMEM((SC_TOK_BLOCK, d), F32),
            pltpu.VMEM((2, SC_ROWS, d), F32),
            pltpu.VMEM((SC_ROWS, lanes), F32),
            pltpu.VMEM((SC_TOK_BLOCK, PEER_E), F32),
            pltpu.SemaphoreType.DMA((2,)),
        ],
        compiler_params=pltpu.CompilerParams(needs_layout_passes=False),
        name="peer_down_dots",
    )
    def k(x_hbm, idx_hbm, tab_hbm, out_hbm, idx_v, x_v, rows_v, acc_v, out_v, sems):
        wid = lax.axis_index("s") * SC_CORES + lax.axis_index("c")
        lane = lax.iota(jnp.int32, lanes)

        @pl.loop(0, nblk)
        def _(blk):
            out0 = wid * tok_w + blk * SC_TOK_BLOCK
            tok0 = tok_start + out0
            pltpu.sync_copy(idx_hbm.at[pl.ds(tok0 * SC_GATHERS, SC_TOK_BLOCK * SC_GATHERS)], idx_v)
            pltpu.sync_copy(x_hbm.at[pl.ds(tok0, SC_TOK_BLOCK)], x_v)

            def compute(i, c, step, buf):
                def body(dc, accs):
                    xv = x_v[i, pl.ds(dc * lanes, lanes)]
                    return tuple(a + rows_v[buf, r, pl.ds(dc * lanes, lanes)] * xv
                                 for r, a in enumerate(accs))
                accs = lax.fori_loop(0, d // lanes, body,
                                     tuple(jnp.zeros((lanes,), F32) for _ in range(SC_ROWS)))
                for r in range(SC_ROWS):
                    acc_v[r, :] = accs[r]
                for g in range(SC_ROWS // lanes):
                    tot = jnp.zeros((lanes,), F32)
                    for l in range(lanes):
                        tot = tot + plsc.load_gather(
                            acc_v, [lane + g * lanes, jnp.full((lanes,), l, jnp.int32)])
                    out_v[i, pl.ds(c * SC_ROWS + g * lanes, lanes)] = tot

            _sc_pipeline(idx_v, tab_hbm, rows_v, sems, compute)
            pltpu.sync_copy(out_v, out_hbm.at[pl.ds(out0, SC_TOK_BLOCK)])

    return k(xn, idx.reshape(t_all * SC_GATHERS, SC_ROWS), down)


def _peer_up_sum(coef, idx, up, tok_start):
    t_all = coef.shape[0]
    t = t_all - tok_start
    d = up.shape[1]
    tok_w = t // SC_WORKERS
    nblk = tok_w // SC_TOK_BLOCK
    assert nblk * SC_TOK_BLOCK * SC_WORKERS == t
    lanes = SC_LANES
    nj = SC_QUARTER // lanes

    @functools.partial(
        pl.kernel, mesh=_sc_mesh(),
        out_type=jax.ShapeDtypeStruct((t, d), F32),
        scratch_types=[
            pltpu.VMEM((SC_TOK_BLOCK * SC_GATHERS, SC_ROWS), jnp.int32),
            pltpu.VMEM((SC_TOK_BLOCK * PEER_E,), F32),
            pltpu.VMEM((2, SC_ROWS, d), F32),
            pltpu.VMEM((SC_TOK_BLOCK, d), F32),
            pltpu.SemaphoreType.DMA((2,)),
        ],
        compiler_params=pltpu.CompilerParams(needs_layout_passes=False),
        name="peer_up_sum",
    )
    def k(coef_hbm, idx_hbm, tab_hbm, out_hbm, idx_v, coef_v, rows_v, out_v, sems):
        wid = lax.axis_index("s") * SC_CORES + lax.axis_index("c")

        @pl.loop(0, nblk)
        def _(blk):
            out0 = wid * tok_w + blk * SC_TOK_BLOCK
            tok0 = tok_start + out0
            pltpu.sync_copy(idx_hbm.at[pl.ds(tok0 * SC_GATHERS, SC_TOK_BLOCK * SC_GATHERS)], idx_v)
            pltpu.sync_copy(coef_hbm.at[pl.ds(tok0 * PEER_E, SC_TOK_BLOCK * PEER_E)], coef_v)

            def compute(i, c, step, buf):
                for q in range(d // SC_QUARTER):
                    col = lambda j: pl.ds(q * SC_QUARTER + j * lanes, lanes)
                    if c == 0:
                        init = tuple(jnp.zeros((lanes,), F32) for _ in range(nj))
                    else:
                        init = tuple(out_v[i, col(j)] for j in range(nj))

                    def body(r, accs):
                        cf = plsc.load_gather(
                            coef_v, [jnp.full((lanes,), step * SC_ROWS + r, jnp.int32)])
                        return tuple(a + rows_v[buf, r, col(j)] * cf for j, a in enumerate(accs))
                    accs = lax.fori_loop(0, SC_ROWS, body, init)
                    for j in range(nj):
                        out_v[i, col(j)] = accs[j]

            _sc_pipeline(idx_v, tab_hbm, rows_v, sems, compute)
            pltpu.sync_copy(out_v, out_hbm.at[pl.ds(out0, SC_TOK_BLOCK)])

    return k(coef.reshape(t_all * PEER_E), idx.reshape(t_all * SC_GATHERS, SC_ROWS), up)


TC_SUB = 128
TC_SUBS = 2
TC_STEP = TC_SUB * TC_SUBS
TC_GROUP = 8
ROW_WORDS = 4
ROW_HALVES = 2 * ROW_WORDS
STACK_COLS = PEER_E * ROW_HALVES
FRONT_PARTS = 4
SC_DOWN_PARTS = 2
SC_DOWN_EXTRA = 1024
PEER_TC_UP_TOKENS = 23040


def _pack_rows_bf16(tab):
    n, d = tab.shape
    return pl.pallas_call(
        _pack_kernel,
        grid=(n // PACK_ROWS,),
        in_specs=[pl.BlockSpec((PACK_ROWS, d), lambda i: (i, 0))],
        out_specs=pl.BlockSpec((PACK_ROWS * ROW_WORDS, LANES), lambda i: (i, 0)),
        out_shape=jax.ShapeDtypeStruct((n * ROW_WORDS, LANES), jnp.uint32),
        compiler_params=pltpu.CompilerParams(dimension_semantics=("arbitrary",)),
        name="pack_rows_bf16",
    )(tab)


PACK_ROWS = 512


def _pack_kernel(x_ref, o_ref):
    u = pltpu.bitcast(x_ref[...], jnp.uint32)
    r = (u + jnp.uint32(0x7FFF) + ((u >> 16) & jnp.uint32(1))) >> 16
    for s in range(ROW_WORDS):
        lo = r[:, 2 * LANES * s:2 * LANES * s + LANES]
        hi = r[:, 2 * LANES * s + LANES:2 * LANES * (s + 1)]
        o_ref[pl.ds(s, PACK_ROWS, stride=ROW_WORDS), :] = lo | (hi << 16)


def _load_table_once(tab_hbm, tab_v, sem):
    @pl.when(pl.program_id(0) == 0)
    def _():
        cp = pltpu.make_async_copy(tab_hbm, tab_v, sem)
        cp.start()
        cp.wait()


def _tc_token_pipeline(idx_hbm, idx_s, isem, tab_v, stack_ref, compute):
    step = pl.program_id(0)

    def idx_copy(block, slot):
        return pltpu.make_async_copy(idx_hbm.at[block], idx_s[slot], isem.at[slot])

    @pl.when(step == 0)
    def _():
        idx_copy(0, 0).start()

    for sub in range(TC_SUBS):
        block = step * TC_SUBS + sub
        idx_copy(block, sub).wait()
        if sub + 1 < TC_SUBS:
            idx_copy(block + 1, sub + 1).start()
        else:
            @pl.when(step + 1 < pl.num_programs(0))
            def _():
                idx_copy(block + 1, 0).start()
        offs = idx_s[sub]

        def copy_rows(i, buf):
            for k in range(PEER_E):
                row0 = offs.at[k][i]
                stack_ref[buf, ROW_WORDS * k:ROW_WORDS * (k + 1), :] = (
                    tab_v[pl.ds(pl.multiple_of(row0, ROW_WORDS), ROW_WORDS), :])

        for g in range(TC_GROUP):
            copy_rows(g, g)

        def trip(ip, carry):
            i0 = ip * 2 * TC_GROUP
            for half in range(2):
                cur = half * TC_GROUP
                nxt = (1 - half) * TC_GROUP
                for g in range(TC_GROUP):
                    copy_rows(jnp.minimum(i0 + cur + TC_GROUP + g, TC_SUB - 1), nxt + g)
                for g in range(TC_GROUP):
                    compute(sub * TC_SUB + i0 + cur + g, cur + g)
            return carry
        lax.fori_loop(0, TC_SUB // (2 * TC_GROUP), trip, 0)


def _tc_peer_scratch(tabp):
    return [pltpu.VMEM(tabp.shape, jnp.uint32),
            pltpu.VMEM((2 * TC_GROUP, PEER_E * ROW_WORDS, LANES), jnp.uint32),
            [pltpu.SMEM((PEER_E, TC_SUB), jnp.int32) for _ in range(TC_SUBS)],
            pltpu.SemaphoreType.DMA,
            pltpu.SemaphoreType.DMA((TC_SUBS,))]


def _stack_selectors():
    col = np.arange(STACK_COLS)
    diag = (col[None, :] % ROW_HALVES == np.arange(ROW_HALVES)[:, None]).astype(np.float32)
    owner = (col[:, None] // ROW_HALVES == np.arange(PEER_E)[None, :]).astype(np.float32)
    return jnp.asarray(diag), jnp.asarray(owner)


def _tc_down_kernel(idx_hbm, x_ref, tab_hbm, diag_ref, fold_ref, o_ref,
                    tab_v, stack_ref, idx_s, sem, isem, z_ref):
    _load_table_once(tab_hbm, tab_v, sem)

    def compute(t, buf):
        stack = pltpu.bitcast(stack_ref[buf], BF16)
        xr = x_ref[pl.ds(t, 1), :]
        x8 = jnp.concatenate([xr[:, LANES * j:LANES * (j + 1)] for j in range(ROW_HALVES)],
                             axis=0).astype(BF16)
        a = lax.dot_general(x8, stack, (((1,), (1,)), ((), ())), preferred_element_type=F32)
        z_ref[pl.ds(t, 1), :] = jnp.sum(a * diag_ref[...], axis=0, keepdims=True)

    _tc_token_pipeline(idx_hbm, idx_s, isem, tab_v, stack_ref, compute)
    o_ref[...] = _dot_exact(z_ref[...], fold_ref[...])


def _peer_down_dots_tc(xn, offs, tabp, n_tok):
    diag, owner = _stack_selectors()
    full = lambda a: pl.BlockSpec(a.shape, lambda i: (0,) * a.ndim)
    return pl.pallas_call(
        _tc_down_kernel,
        grid=(n_tok // TC_STEP,),
        in_specs=[pl.BlockSpec(memory_space=pl.ANY),
                  pl.BlockSpec((TC_STEP, xn.shape[1]), lambda i: (i, 0)),
                  pl.BlockSpec(memory_space=pl.ANY), full(diag), full(owner)],
        out_specs=pl.BlockSpec((TC_STEP, PEER_E), lambda i: (i, 0)),
        out_shape=jax.ShapeDtypeStruct((n_tok, PEER_E), F32),
        scratch_shapes=_tc_peer_scratch(tabp) + [pltpu.VMEM((TC_STEP, STACK_COLS), F32)],
        compiler_params=pltpu.CompilerParams(
            dimension_semantics=("arbitrary",), vmem_limit_bytes=VMEM_LIMIT),
        name="peer_down_dots_tc",
    )(offs, xn, tabp, diag, owner)


def _tc_up_kernel(idx_hbm, coef_ref, tab_hbm, spread_ref, diag_ref, o_ref,
                  tab_v, stack_ref, idx_s, sem, isem, ce_ref):
    _load_table_once(tab_hbm, tab_v, sem)
    c = coef_ref[...]
    c_hi = c.astype(BF16)
    c_lo = (c - c_hi.astype(F32)).astype(BF16)
    ce_ref[0] = jnp.dot(c_hi, spread_ref[...], preferred_element_type=F32)
    ce_ref[1] = jnp.dot(c_lo, spread_ref[...], preferred_element_type=F32)

    def compute(t, buf):
        stack = pltpu.bitcast(stack_ref[buf], BF16)
        d = diag_ref[...]
        lhs = jnp.concatenate([(ce_ref[0, pl.ds(t, 1), :] * d).astype(BF16),
                               (ce_ref[1, pl.ds(t, 1), :] * d).astype(BF16)], axis=0)
        r = jnp.dot(lhs, stack, preferred_element_type=F32)
        out8 = r[0:ROW_HALVES] + r[ROW_HALVES:2 * ROW_HALVES]
        o_ref[pl.ds(t, 1), :] = jnp.concatenate(
            [out8[j:j + 1, :] for j in range(ROW_HALVES)], axis=1)

    _tc_token_pipeline(idx_hbm, idx_s, isem, tab_v, stack_ref, compute)


def _peer_up_sum_tc(coef, offs, tabp, n_tok):
    d = ROW_HALVES * LANES
    diag, owner = _stack_selectors()
    spread = owner.T.astype(BF16)
    full = lambda a: pl.BlockSpec(a.shape, lambda i: (0,) * a.ndim)
    return pl.pallas_call(
        _tc_up_kernel,
        grid=(n_tok // TC_STEP,),
        in_specs=[pl.BlockSpec(memory_space=pl.ANY),
                  pl.BlockSpec((TC_STEP, PEER_E), lambda i: (i, 0)),
                  pl.BlockSpec(memory_space=pl.ANY), full(spread), full(diag)],
        out_specs=pl.BlockSpec((TC_STEP, d), lambda i: (i, 0)),
        out_shape=jax.ShapeDtypeStruct((n_tok, d), F32),
        scratch_shapes=_tc_peer_scratch(tabp) + [pltpu.VMEM((2, TC_STEP, STACK_COLS), F32)],
        compiler_params=pltpu.CompilerParams(
            dimension_semantics=("arbitrary",), vmem_limit_bytes=VMEM_LIMIT),
        name="peer_up_sum_tc",
    )(offs, coef, tabp, spread, diag)


def _coef_kernel(pre_ref, gate_ref, o_ref):
    x = pre_ref[...]
    gelu = 0.5 * x * (1.0 + lax.erf(x * (2.0 ** -0.5)))
    o_ref[...] = gate_ref[...] * gelu


def _peer_coef(pre, gates):
    t, e = pre.shape
    row = pl.BlockSpec((TM_FINAL, e), lambda i: (i, 0))
    return pl.pallas_call(
        _coef_kernel,
        grid=(t // TM_FINAL,),
        in_specs=[row, row],
        out_specs=row,
        out_shape=jax.ShapeDtypeStruct((t, e), F32),
        compiler_params=pltpu.CompilerParams(dimension_semantics=("arbitrary",)),
        name="peer_gate_gelu",
    )(pre, gates)


def _front_half(x2, row0, rows, bsz, seq, w_pad, layer, i):
    (norm1_g, ret_norm_g, m_conv_w, m_conv_b, m_wq, m_wk, m_b_i, m_b_f, m_norm_g, m_skip,
     w_out, norm2_g, peer_wq, keys) = layer
    d = x2.shape[1]
    proj = _rms_proj(x2, row0, rows, norm1_g[i].reshape(1, d), w_pad)
    y = _mixer(proj, bsz, seq, ret_norm_g[i], m_conv_w[i], m_conv_b[i], m_wq[i], m_wk[i],
               m_b_i[i], m_b_f[i], m_norm_g[i], m_skip[i])
    x1, xn, sub_t = _post(y, x2, row0, w_out, norm2_g[i].reshape(1, d), peer_wq, keys)
    return (x1, xn) + tuple(_peer_route(sub_t))


def kernel(x, norm1_g, w_in, ret_norm_g, m_conv_w, m_conv_b, m_wq, m_wk, m_b_i, m_b_f,
           m_norm_g, m_skip, w_out, norm2_g, peer_wq, peer_keys, peer_down, peer_up, final_g):
    b, s, d = x.shape
    t = b * s
    tp = t // FRONT_PARTS
    x2 = x.reshape(t, d)
    depth = norm1_g.shape[0]
    for i in range(depth):
        if i > 0:
            x2 = jnp.concatenate(x1s, axis=0) + jnp.concatenate([p_tc, p_sc], axis=0)
        wi = w_in[i]
        n_main = OFF_GI
        gate_pad = lambda w: jnp.pad(w, ((0, 0), (0, LANES - w.shape[1])))
        w_pad = jnp.concatenate(
            [wi[:, :n_main], gate_pad(wi[:, n_main:n_main + MLSTM_HEADS]),
             gate_pad(wi[:, n_main + MLSTM_HEADS:])], axis=1).astype(BF16)
        keys = peer_keys[i].reshape(2 * PEER_HEADS, PEER_NKEYS, PEER_HALF).astype(BF16)
        layer = (norm1_g, ret_norm_g, m_conv_w, m_conv_b, m_wq, m_wk, m_b_i, m_b_f, m_norm_g,
                 m_skip, w_out[i].astype(BF16), norm2_g, peer_wq[i].astype(BF16), keys)
        down_p = _pack_rows_bf16(peer_down[i])
        up_p = _pack_rows_bf16(peer_up[i])

        fronts = []
        pre = [None] * FRONT_PARTS
        for q in range(FRONT_PARTS):
            fr = _front_half(x2, q * tp, tp, b // FRONT_PARTS, s, w_pad, layer, i)
            fronts.append(fr)
            if q < SC_DOWN_PARTS:
                pre[q] = [_peer_down_dots(fr[1], fr[2], peer_down[i], 0)]
            elif q == SC_DOWN_PARTS:
                pre[q] = [None, _peer_down_dots(fr[1], fr[2], peer_down[i], tp - SC_DOWN_EXTRA)]
        for q in range(SC_DOWN_PARTS, FRONT_PARTS):
            n_tc_q = tp - SC_DOWN_EXTRA if q == SC_DOWN_PARTS else tp
            pre_tc = _peer_down_dots_tc(fronts[q][1], fronts[q][4], down_p, n_tc_q)
            if q == SC_DOWN_PARTS:
                pre[q][0] = pre_tc
            else:
                pre[q] = [pre_tc]
        pre = [piece for part in pre for piece in part]
        x1s = [fr[0] for fr in fronts]
        expert = jnp.concatenate([fr[2] for fr in fronts], axis=0)
        gates = jnp.concatenate([fr[3] for fr in fronts], axis=0)
        offs = jnp.concatenate([fr[4] for fr in fronts], axis=0)
        coef = _peer_coef(jnp.concatenate(pre, axis=0), gates)
        n_tc = PEER_TC_UP_TOKENS
        p_sc = _peer_up_sum(coef, expert, peer_up[i], n_tc)
        p_tc = _peer_up_sum_tc(coef, offs, up_p, n_tc)
    out = _final(_row_ranges(x1s, [p_tc, p_sc]), final_g.reshape(1, d), t)
    return out.reshape(b, s, d)


def _row_ranges(xs, ps):
    def starts(arrs):
        out = [0]
        for a in arrs:
            out.append(out[-1] + a.shape[0])
        return out
    xb, pb = starts(xs), starts(ps)
    assert xb[-1] == pb[-1]
    cuts = sorted(set(xb) | set(pb))
    parts = []
    for lo, hi in zip(cuts[:-1], cuts[1:]):
        xi = max(k for k in range(len(xs)) if xb[k] <= lo)
        pi = max(k for k in range(len(ps)) if pb[k] <= lo)
        parts.append((xs[xi], lo - xb[xi], ps[pi], lo - pb[pi], hi - lo))
    return parts
```

```python
import functools

import numpy as np
import jax
import jax.numpy as jnp
from jax import lax
from jax.experimental import pallas as pl
from jax.experimental.pallas import tpu as pltpu
from jax.experimental.pallas import tpu_sc as plsc

F32 = jnp.float32
BF16 = jnp.bfloat16

D_MODEL = 1024
CHUNK = 64
RET_HEADS = 4
RET_DK = 128
MLSTM_HEADS = 4
MLSTM_D = 128
HEAD_W = 128
D_GROUP = 512
CONV_W = 4
ROPE_BASE = 10000.0
PEER_HEADS = 8
PEER_NKEYS = 128
PEER_TOPK = 16
PEER_HALF = 128
PEER_BLOCK = 128
EPS = 1e-6

OFF_RQ, OFF_RK, OFF_RV, OFF_RG = 0, 512, 1024, 1536
OFF_MX, OFF_MV, OFF_MO = 2048, 2560, 3072
OFF_GI, OFF_GF = 3584, 3712
D_PROJ_PAD = 3840

LANES = 128
CONV_TAIL = 8
VMEM_LIMIT = 56 * 1024 * 1024

TM_PROJ = 256
L_BLOCK = 256
TM_FINAL = 512


def _rms(x, g):
    ms = jnp.mean(x * x, axis=-1, keepdims=True)
    return x * lax.rsqrt(ms + EPS) * g


def _group_norm(h, g):
    mu = jnp.mean(h, axis=-1, keepdims=True)
    d = h - mu
    var = jnp.mean(d * d, axis=-1, keepdims=True)
    return d * lax.rsqrt(var + EPS) * g


def _silu(x):
    return x * (1.0 / (1.0 + jnp.exp(-x)))


def _sigmoid(x):
    return 1.0 / (1.0 + jnp.exp(-x))


def _dot(a, b):
    return jnp.dot(a.astype(BF16), b.astype(BF16), preferred_element_type=F32)


def _dot_tn(a, b):
    return lax.dot_general(a.astype(BF16), b.astype(BF16), (((0,), (0,)), ((), ())),
                           preferred_element_type=F32)


def _dot_nt(a, b):
    return lax.dot_general(a.astype(BF16), b.astype(BF16), (((1,), (1,)), ((), ())),
                           preferred_element_type=F32)


def _rms_proj_kernel(x_ref, g_ref, w_ref, o_ref):
    h = _rms(x_ref[...], g_ref[...])
    o_ref[...] = jnp.dot(h.astype(BF16), w_ref[...], preferred_element_type=F32)


def _rms_proj(x2, row0, rows, g, w):
    d = x2.shape[1]
    t = rows
    n = w.shape[1]
    b0 = row0 // TM_PROJ
    return pl.pallas_call(
        _rms_proj_kernel,
        grid=(t // TM_PROJ,),
        in_specs=[
            pl.BlockSpec((TM_PROJ, d), lambda i: (i + b0, 0)),
            pl.BlockSpec((1, d), lambda i: (0, 0)),
            pl.BlockSpec((d, n), lambda i: (0, 0)),
        ],
        out_specs=pl.BlockSpec((TM_PROJ, n), lambda i: (i, 0)),
        out_shape=jax.ShapeDtypeStruct((t, n), F32),
        compiler_params=pltpu.CompilerParams(
            dimension_semantics=("arbitrary",), vmem_limit_bytes=VMEM_LIMIT),
        name="rms_in_proj",
    )(x2, g, w)


def _mixer_kernel(chunk_decay,
                  proj_ref, cos_ref, sin_ref, intra_ref, qd_ref, kd_ref, tri_ref,
                  rng_ref, cw_ref, cb_ref, wq_ref, wk_ref, bi_ref, bf_ref,
                  mng_ref, skip_ref,
                  y_ref,
                  rstate_ref, cstate_ref, m_ref, tail_ref):
    j = pl.program_id(1)
    n_chunks = L_BLOCK // CHUNK

    @pl.when(j == 0)
    def _():
        rstate_ref[...] = jnp.zeros_like(rstate_ref)
        cstate_ref[...] = jnp.zeros_like(cstate_ref)
        m_ref[...] = jnp.zeros_like(m_ref)
        tail_ref[...] = jnp.zeros_like(tail_ref)

    cosv = cos_ref[...]
    sinv = sin_ref[...]
    k_scale = RET_DK ** -0.5

    for h in range(RET_HEADS):
        lo = h * HEAD_W
        q = proj_ref[:, OFF_RQ + lo:OFF_RQ + lo + HEAD_W]
        k = proj_ref[:, OFF_RK + lo:OFF_RK + lo + HEAD_W]
        q = q * cosv + pltpu.roll(q, HEAD_W // 2, 1) * sinv
        k = (k * cosv + pltpu.roll(k, HEAD_W // 2, 1) * sinv) * k_scale
        intra = intra_ref[h]
        qd = qd_ref[h]
        kd = kd_ref[h]
        g = rng_ref[:, lo:lo + HEAD_W]
        for c in range(n_chunks):
            r0 = c * CHUNK
            qc = q[r0:r0 + CHUNK]
            kc = k[r0:r0 + CHUNK]
            vc = proj_ref[r0:r0 + CHUNK, OFF_RV + lo:OFF_RV + lo + HEAD_W]
            gate = proj_ref[r0:r0 + CHUNK, OFF_RG + lo:OFF_RG + lo + HEAD_W]
            state = rstate_ref[h]
            scores = _dot_nt(qc, kc) * intra
            o = _dot(scores, vc) + _dot(qc, state) * qd
            rstate_ref[h] = chunk_decay[h] * state + _dot_tn(kc * kd, vc)
            y_ref[r0:r0 + CHUNK, lo:lo + HEAD_W] = _silu(gate) * _group_norm(o, g)

    mx = proj_ref[:, OFF_MX:OFF_MX + D_GROUP]
    xp = jnp.concatenate([tail_ref[...], mx], axis=0)
    base = CONV_TAIL - (CONV_W - 1)
    conv = cb_ref[...] + xp[base:base + L_BLOCK] * cw_ref[0:1, :]
    for t in range(1, CONV_W):
        conv = conv + xp[base + t:base + t + L_BLOCK] * cw_ref[t:t + 1, :]
    tail_ref[...] = mx[L_BLOCK - CONV_TAIL:L_BLOCK]
    cact = _silu(conv)

    gi = proj_ref[:, OFF_GI:OFF_GI + LANES] + bi_ref[...]
    gf = proj_ref[:, OFF_GF:OFF_GF + LANES] + bf_ref[...]
    logf = jnp.minimum(gf, 0.0) - jnp.log1p(jnp.exp(-jnp.abs(gf)))
    tri = tri_ref[...]
    lane = lax.broadcasted_iota(jnp.int32, (CHUNK, HEAD_W), 1)
    ones_col = (lane == 0).astype(F32)
    m_scale = MLSTM_D ** -0.5

    mq = []
    mk = []
    for h in range(MLSTM_HEADS):
        lo = h * HEAD_W
        ch = cact[:, lo:lo + HEAD_W]
        mq.append(_dot(ch, wq_ref[h]))
        mk.append(_dot(ch, wk_ref[h]) * m_scale)

    for c in range(n_chunks):
        r0 = c * CHUNK
        fcum = jnp.dot(tri, logf[r0:r0 + CHUNK], preferred_element_type=F32,
                       precision=lax.Precision.HIGHEST)
        a = fcum[CHUNK - 1:CHUNK]
        logw = a - fcum + gi[r0:r0 + CHUNK]
        m_in = jnp.max(logw, axis=0, keepdims=True)
        m_old = m_ref[...]
        m_new = jnp.maximum(a + m_old, m_in)
        decay = jnp.exp(a + m_old - m_new)
        w = jnp.exp(logw - m_new)
        em = jnp.exp(-m_new)
        m_ref[...] = m_new
        for h in range(MLSTM_HEADS):
            lo = h * HEAD_W
            qh = mq[h][r0:r0 + CHUNK]
            kh = mk[h][r0:r0 + CHUNK]
            vh = proj_ref[r0:r0 + CHUNK, OFF_MV + lo:OFF_MV + lo + HEAD_W]
            oh = proj_ref[r0:r0 + CHUNK, OFF_MO + lo:OFF_MO + lo + HEAD_W]
            ch = cact[r0:r0 + CHUNK, lo:lo + HEAD_W]
            v_ext = jnp.concatenate([vh, ones_col], axis=1)
            cmem = decay[:, h:h + 1] * cstate_ref[h] + _dot_tn(kh * w[:, h:h + 1], v_ext)
            cstate_ref[h] = cmem
            num = _dot(qh, cmem)
            den = jnp.maximum(jnp.abs(num[:, HEAD_W:HEAD_W + 1]), em[:, h:h + 1])
            hh = num[:, :HEAD_W] / den
            hm = _group_norm(hh, mng_ref[:, lo:lo + HEAD_W])
            ym = _sigmoid(oh) * (hm + skip_ref[:, lo:lo + HEAD_W] * ch)
            y_ref[r0:r0 + CHUNK, D_GROUP + lo:D_GROUP + lo + HEAD_W] = ym


def _retention_tables():
    h = np.arange(RET_HEADS, dtype=np.float64)
    log_g = np.log(1.0 - 2.0 ** (-5.0 - h))
    l = np.arange(CHUNK, dtype=np.float64)
    intra = np.exp(log_g[:, None, None] * np.abs(l[:, None] - l[None, :]))
    qd = np.exp(log_g[:, None] * (l + 1.0))[:, :, None] * np.ones((1, 1, HEAD_W))
    kd = np.exp(log_g[:, None] * (CHUNK - 1.0 - l))[:, :, None] * np.ones((1, 1, HEAD_W))
    cd = np.exp(log_g * CHUNK)
    return (jnp.asarray(intra, F32), jnp.asarray(qd, F32), jnp.asarray(kd, F32),
            tuple(float(v) for v in cd))


def _rotary_tables(s):
    half = HEAD_W // 2
    inv = ROPE_BASE ** (-np.arange(half, dtype=np.float64) / half)
    ang = np.arange(s, dtype=np.float64)[:, None] * inv[None, :]
    cos = np.cos(ang)
    sin = np.sin(ang)
    return (jnp.asarray(np.concatenate([cos, cos], axis=-1), F32),
            jnp.asarray(np.concatenate([-sin, sin], axis=-1), F32))


def _mixer(proj, b, s, ret_norm_g, m_conv_w, m_conv_b, m_wq, m_wk, m_b_i, m_b_f,
           m_norm_g, m_skip):
    intra, qd, kd, cd = _retention_tables()
    cos_t, sin_t = _rotary_tables(s)
    tri = jnp.asarray(np.tril(np.ones((CHUNK, CHUNK))), F32)
    nj = s // L_BLOCK
    pad = lambda v: jnp.pad(v.reshape(1, -1), ((0, 0), (0, LANES - v.size)))
    full = lambda shape: pl.BlockSpec(shape, lambda bi, j: (0,) * len(shape))
    return pl.pallas_call(
        functools.partial(_mixer_kernel, cd),
        grid=(b, nj),
        in_specs=[
            pl.BlockSpec((L_BLOCK, D_PROJ_PAD), lambda bi, j: (bi * nj + j, 0)),
            pl.BlockSpec((L_BLOCK, HEAD_W), lambda bi, j: (j, 0)),
            pl.BlockSpec((L_BLOCK, HEAD_W), lambda bi, j: (j, 0)),
            full((RET_HEADS, CHUNK, CHUNK)),
            full((RET_HEADS, CHUNK, HEAD_W)),
            full((RET_HEADS, CHUNK, HEAD_W)),
            full((CHUNK, CHUNK)),
            full((1, D_GROUP)),
            full((CONV_W, D_GROUP)),
            full((1, D_GROUP)),
            full((MLSTM_HEADS, MLSTM_D, MLSTM_D)),
            full((MLSTM_HEADS, MLSTM_D, MLSTM_D)),
            full((1, LANES)),
            full((1, LANES)),
            full((1, D_GROUP)),
            full((1, D_GROUP)),
        ],
        out_specs=pl.BlockSpec((L_BLOCK, 2 * D_GROUP), lambda bi, j: (bi * nj + j, 0)),
        out_shape=jax.ShapeDtypeStruct((b * s, 2 * D_GROUP), F32),
        scratch_shapes=[
            pltpu.VMEM((RET_HEADS, RET_DK, HEAD_W), F32),
            pltpu.VMEM((MLSTM_HEADS, MLSTM_D, 2 * HEAD_W), F32),
            pltpu.VMEM((1, LANES), F32),
            pltpu.VMEM((CONV_TAIL, D_GROUP), F32),
        ],
        compiler_params=pltpu.CompilerParams(
            dimension_semantics=("arbitrary", "arbitrary"), vmem_limit_bytes=VMEM_LIMIT),
        name="retention_mlstm_mixer",
    )(proj, cos_t, sin_t, intra, qd, kd, tri,
      ret_norm_g.reshape(1, D_GROUP), m_conv_w, m_conv_b.reshape(1, D_GROUP),
      m_wq.astype(BF16), m_wk.astype(BF16), pad(m_b_i), pad(m_b_f),
      m_norm_g.reshape(1, D_GROUP), m_skip.reshape(1, D_GROUP))


def _post_kernel(y_ref, x_ref, wout_ref, g2_ref, wq_ref, keys_ref,
                 x1_ref, xn_ref, sub_ref):
    x1 = x_ref[...] + jnp.dot(y_ref[...].astype(BF16), wout_ref[...],
                              preferred_element_type=F32)
    x1_ref[...] = x1
    xn = _rms(x1, g2_ref[...])
    xn_ref[...] = xn
    q = jnp.dot(xn.astype(BF16), wq_ref[...], preferred_element_type=F32)
    for i in range(2 * PEER_HEADS):
        lo = i * PEER_HALF
        sub_ref[lo:lo + PEER_NKEYS, :] = _dot_nt(keys_ref[i], q[:, lo:lo + PEER_HALF])


def _post(y, x2, row0, w_out, norm2_g, peer_wq, keys):
    t, d = y.shape
    nq = peer_wq.shape[1]
    b0 = row0 // TM_PROJ
    row = lambda w: pl.BlockSpec((TM_PROJ, w), lambda i: (i, 0))
    full = lambda shape: pl.BlockSpec(shape, lambda i: (0,) * len(shape))
    return pl.pallas_call(
        _post_kernel,
        grid=(t // TM_PROJ,),
        in_specs=[row(d), pl.BlockSpec((TM_PROJ, d), lambda i: (i + b0, 0)), full((d, d)),
                  full((1, d)), full((d, nq)), full(keys.shape)],
        out_specs=[row(d), row(d), pl.BlockSpec((nq, TM_PROJ), lambda i: (0, i))],
        out_shape=[jax.ShapeDtypeStruct((t, d), F32), jax.ShapeDtypeStruct((t, d), F32),
                   jax.ShapeDtypeStruct((nq, t), F32)],
        compiler_params=pltpu.CompilerParams(
            dimension_semantics=("arbitrary",), vmem_limit_bytes=VMEM_LIMIT),
        name="out_proj_peer_scores",
    )(y, x2, w_out, norm2_g, peer_wq, keys)


def _final_kernel(x1_ref, p_ref, g_ref, *rest):
    o_ref = rest[-1]
    o_ref[...] = _rms(x1_ref[...] + p_ref[...], g_ref[...])


def _final(parts, g, t):
    d = g.shape[1]
    spec = lambda r0: pl.BlockSpec((TM_FINAL, d), lambda i, b0=r0 // TM_FINAL: (i + b0, 0))
    out = None
    row0 = 0
    for x1, x_row0, p, p_row0, rows in parts:
        in_specs = [spec(x_row0), spec(p_row0), pl.BlockSpec((1, d), lambda i: (0, 0))]
        args = [x1, p, g]
        if out is not None:
            in_specs.append(pl.BlockSpec(memory_space=pl.ANY))
            args.append(out)
        out = pl.pallas_call(
            _final_kernel,
            grid=(rows // TM_FINAL,),
            in_specs=in_specs,
            out_specs=spec(row0),
            out_shape=jax.ShapeDtypeStruct((t, d), F32),
            input_output_aliases={} if len(args) == 3 else {3: 0},
            compiler_params=pltpu.CompilerParams(dimension_semantics=("arbitrary",)),
            name="final_rmsnorm",
        )(*args)
        row0 += rows
    assert row0 == t
    return out


TM_ROUTE = 256
CAND_ROWS = 56


def _candidate_tables():
    pairs = [(k1, k2) for k1 in range(PEER_TOPK) for k2 in range(PEER_TOPK)
             if (k1 + 1) * (k2 + 1) <= PEER_TOPK]
    assert len(pairs) <= CAND_ROWS
    rep = np.zeros((CAND_ROWS, 2 * PEER_TOPK), np.float32)
    til = np.zeros((CAND_ROWS, 2 * PEER_TOPK), np.float32)
    for pos, (k1, k2) in enumerate(pairs):
        rep[pos, k1] = 1.0
        til[pos, PEER_TOPK + k2] = 1.0
    return jnp.asarray(rep), jnp.asarray(til), len(pairs)


def _dot_exact(a, b):
    return jnp.dot(a, b, preferred_element_type=F32, precision=lax.Precision.HIGHEST)


def _route_kernel(n_cand, sub_ref, rep_ref, til_ref, seg_ref, exp_ref, gate_ref, offs_ref,
                  s_ref, i_ref, best_ref, eid_ref, m0_ref):
    tm = sub_ref.shape[1]
    neg = -jnp.inf
    key_id = lax.broadcasted_iota(jnp.int32, (PEER_NKEYS, tm), 0).astype(F32)
    cand_id = lax.broadcasted_iota(jnp.int32, (CAND_ROWS, tm), 0).astype(F32)
    rep = rep_ref[...]
    til = til_ref[...]
    for h in range(PEER_HEADS):
        for p in range(2):
            lo = (2 * h + p) * PEER_NKEYS
            x = sub_ref[lo:lo + PEER_NKEYS, :]
            for k in range(PEER_TOPK):
                m = jnp.max(x, axis=0, keepdims=True)
                idx = jnp.min(jnp.where(x == m, key_id, float(PEER_NKEYS)), axis=0, keepdims=True)
                x = jnp.where(key_id == idx, neg, x)
                r = p * PEER_TOPK + k
                s_ref[r:r + 1, :] = m
                i_ref[r:r + 1, :] = idx
        s = s_ref[...]
        iv = i_ref[...]
        cand = _dot_exact(rep, s) + _dot_exact(til, s)
        cand = jnp.where(cand_id < float(n_cand), cand, neg)
        eid = _dot(rep, iv) * float(PEER_NKEYS) + _dot(til, iv)
        for k in range(PEER_TOPK):
            m = jnp.max(cand, axis=0, keepdims=True)
            pos = jnp.min(jnp.where(cand == m, cand_id, float(CAND_ROWS)), axis=0, keepdims=True)
            sel = cand_id == pos
            e = jnp.max(jnp.where(sel, eid, -1.0), axis=0, keepdims=True)
            cand = jnp.where(sel, neg, cand)
            r = h * PEER_TOPK + k
            best_ref[r:r + 1, :] = m
            eid_ref[r:r + 1, :] = e
            if k == 0:
                m0_ref[r:r + PEER_TOPK, :] = jnp.broadcast_to(m, (PEER_TOPK, tm))
    pexp = jnp.exp(best_ref[...] - m0_ref[...])
    gates = pexp / _dot_exact(seg_ref[...], pexp)
    gate_ref[...] = gates.T
    eid = eid_ref[...].astype(jnp.int32)
    exp_ref[...] = eid.T
    for blk in range(tm // LANES):
        offs_ref[blk] = eid[:, blk * LANES:(blk + 1) * LANES] * ROW_WORDS


def _peer_route(sub_t):
    nq, t = sub_t.shape
    rep, til, n_cand = _candidate_tables()
    e = PEER_HEADS * PEER_TOPK
    seg = jnp.asarray(np.kron(np.eye(PEER_HEADS), np.ones((PEER_TOPK, PEER_TOPK))), F32)
    full = lambda a: pl.BlockSpec(a.shape, lambda i: (0,) * a.ndim)
    out = pl.BlockSpec((TM_ROUTE, e), lambda i: (i, 0))
    return pl.pallas_call(
        functools.partial(_route_kernel, n_cand),
        grid=(t // TM_ROUTE,),
        in_specs=[pl.BlockSpec((nq, TM_ROUTE), lambda i: (0, i)), full(rep), full(til), full(seg)],
        out_specs=[out, out, pl.BlockSpec((TM_ROUTE // LANES, e, LANES), lambda i: (i, 0, 0))],
        out_shape=[jax.ShapeDtypeStruct((t, e), jnp.int32), jax.ShapeDtypeStruct((t, e), F32),
                   jax.ShapeDtypeStruct((t // LANES, e, LANES), jnp.int32)],
        scratch_shapes=[
            pltpu.VMEM((2 * PEER_TOPK, TM_ROUTE), F32),
            pltpu.VMEM((2 * PEER_TOPK, TM_ROUTE), F32),
            pltpu.VMEM((e, TM_ROUTE), F32),
            pltpu.VMEM((e, TM_ROUTE), F32),
            pltpu.VMEM((e, TM_ROUTE), F32),
        ],
        compiler_params=pltpu.CompilerParams(dimension_semantics=("arbitrary",)),
        name="peer_topk_route",
    )(sub_t, rep, til, seg)


SC_CORES = 2
SC_SUBCORES = 16
SC_LANES = 16
SC_WORKERS = SC_CORES * SC_SUBCORES
PEER_E = PEER_HEADS * PEER_TOPK
SC_ROWS = 32
SC_GATHERS = PEER_E // SC_ROWS
SC_TOK_BLOCK = 16
SC_QUARTER = 256


def _sc_mesh():
    return plsc.VectorSubcoreMesh(core_axis_name="c", subcore_axis_name="s")


def _sc_pipeline(idx_v, tab_hbm, rows_v, sems, compute):
    def gather(step, buf):
        return pltpu.make_async_copy(tab_hbm.at[idx_v.at[step]], rows_v.at[buf], sems.at[buf])

    gather(0, 0).start()

    @pl.loop(0, SC_TOK_BLOCK)
    def _(i):
        for c in range(SC_GATHERS):
            buf = c % 2
            step = i * SC_GATHERS + c
            gather(step, buf).wait()
            if c + 1 < SC_GATHERS:
                gather(step + 1, 1 - buf).start()
            else:
                @pl.when(i + 1 < SC_TOK_BLOCK)
                def _():
                    gather(step + 1, 1 - buf).start()
            compute(i, c, step, buf)


def _peer_down_dots(xn, idx, down, tok_start):
    t_all, d = xn.shape
    t = t_all - tok_start
    tok_w = t // SC_WORKERS
    nblk = tok_w // SC_TOK_BLOCK
    assert nblk * SC_TOK_BLOCK * SC_WORKERS == t
    lanes = SC_LANES

    @functools.partial(
        pl.kernel, mesh=_sc_mesh(),
        out_type=jax.ShapeDtypeStruct((t, PEER_E), F32),
        scratch_types=[
            pltpu.VMEM((SC_TOK_BLOCK * SC_GATHERS, SC_ROWS), jnp.int32),
            pltpu.VMEM((SC_TOK_BLOCK, d), F32),
            pltpu.VMEM((2, SC_ROWS, d), F32),
            pltpu.VMEM((SC_ROWS, lanes), F32),
            pltpu.VMEM((SC_TOK_BLOCK, PEER_E), F32),
            pltpu.SemaphoreType.DMA((2,)),
        ],
        compiler_params=pltpu.CompilerParams(needs_layout_passes=False),
        name="peer_down_dots",
    )
    def k(x_hbm, idx_hbm, tab_hbm, out_hbm, idx_v, x_v, rows_v, acc_v, out_v, sems):
        wid = lax.axis_index("s") * SC_CORES + lax.axis_index("c")
        lane = lax.iota(jnp.int32, lanes)

        @pl.loop(0, nblk)
        def _(blk):
            out0 = wid * tok_w + blk * SC_TOK_BLOCK
            tok0 = tok_start + out0
            pltpu.sync_copy(idx_hbm.at[pl.ds(tok0 * SC_GATHERS, SC_TOK_BLOCK * SC_GATHERS)], idx_v)
            pltpu.sync_copy(x_hbm.at[pl.ds(tok0, SC_TOK_BLOCK)], x_v)

            def compute(i, c, step, buf):
                def body(dc, accs):
                    xv = x_v[i, pl.ds(dc * lanes, lanes)]
                    return tuple(a + rows_v[buf, r, pl.ds(dc * lanes, lanes)] * xv
                                 for r, a in enumerate(accs))
                accs = lax.fori_loop(0, d // lanes, body,
                                     tuple(jnp.zeros((lanes,), F32) for _ in range(SC_ROWS)))
                for r in range(SC_ROWS):
                    acc_v[r, :] = accs[r]
                for g in range(SC_ROWS // lanes):
                    tot = jnp.zeros((lanes,), F32)
                    for l in range(lanes):
                        tot = tot + plsc.load_gather(
                            acc_v, [lane + g * lanes, jnp.full((lanes,), l, jnp.int32)])
                    out_v[i, pl.ds(c * SC_ROWS + g * lanes, lanes)] = tot

            _sc_pipeline(idx_v, tab_hbm, rows_v, sems, compute)
            pltpu.sync_copy(out_v, out_hbm.at[pl.ds(out0, SC_TOK_BLOCK)])

    return k(xn, idx.reshape(t_all * SC_GATHERS, SC_ROWS), down)


def _peer_up_sum(coef, idx, up, tok_start):
    t_all = coef.shape[0]
    t = t_all - tok_start
    d = up.shape[1]
    tok_w = t // SC_WORKERS
    nblk = tok_w // SC_TOK_BLOCK
    assert nblk * SC_TOK_BLOCK * SC_WORKERS == t
    lanes = SC_LANES
    nj = SC_QUARTER // lanes

    @functools.partial(
        pl.kernel, mesh=_sc_mesh(),
        out_type=jax.ShapeDtypeStruct((t, d), F32),
        scratch_types=[
            pltpu.VMEM((SC_TOK_BLOCK * SC_GATHERS, SC_ROWS), jnp.int32),
            pltpu.VMEM((SC_TOK_BLOCK * PEER_E,), F32),
            pltpu.VMEM((2, SC_ROWS, d), F32),
            pltpu.VMEM((SC_TOK_BLOCK, d), F32),
            pltpu.SemaphoreType.DMA((2,)),
        ],
        compiler_params=pltpu.CompilerParams(needs_layout_passes=False),
        name="peer_up_sum",
    )
    def k(coef_hbm, idx_hbm, tab_hbm, out_hbm, idx_v, coef_v, rows_v, out_v, sems):
        wid = lax.axis_index("s") * SC_CORES + lax.axis_index("c")

        @pl.loop(0, nblk)
        def _(blk):
            out0 = wid * tok_w + blk * SC_TOK_BLOCK
            tok0 = tok_start + out0
            pltpu.sync_copy(idx_hbm.at[pl.ds(tok0 * SC_GATHERS, SC_TOK_BLOCK * SC_GATHERS)], idx_v)
            pltpu.sync_copy(coef_hbm.at[pl.ds(tok0 * PEER_E, SC_TOK_BLOCK * PEER_E)], coef_v)

            def compute(i, c, step, buf):
                for q in range(d // SC_QUARTER):
                    col = lambda j: pl.ds(q * SC_QUARTER + j * lanes, lanes)
                    if c == 0:
                        init = tuple(jnp.zeros((lanes,), F32) for _ in range(nj))
                    else:
                        init = tuple(out_v[i, col(j)] for j in range(nj))

                    def body(r, accs):
                        cf = plsc.load_gather(
                            coef_v, [jnp.full((lanes,), step * SC_ROWS + r, jnp.int32)])
                        return tuple(a + rows_v[buf, r, col(j)] * cf for j, a in enumerate(accs))
                    accs = lax.fori_loop(0, SC_ROWS, body, init)
                    for j in range(nj):
                        out_v[i, col(j)] = accs[j]

            _sc_pipeline(idx_v, tab_hbm, rows_v, sems, compute)
            pltpu.sync_copy(out_v, out_hbm.at[pl.ds(out0, SC_TOK_BLOCK)])

    return k(coef.reshape(t_all * PEER_E), idx.reshape(t_all * SC_GATHERS, SC_ROWS), up)


TC_SUB = 128
TC_SUBS = 2
TC_STEP = TC_SUB * TC_SUBS
TC_TRIP = 8
TC_KCHUNKS = 4
ROW_WORDS = 4
ROW_HALVES = 2 * ROW_WORDS
STACK_COLS = PEER_E * ROW_HALVES
FRONT_PARTS = 4
SC_DOWN_PARTS = 2
SC_DOWN_EXTRA = 1024
PEER_TC_UP_TOKENS = 23040


def _pack_rows_bf16(tab):
    n, d = tab.shape
    return pl.pallas_call(
        _pack_kernel,
        grid=(n // PACK_ROWS,),
        in_specs=[pl.BlockSpec((PACK_ROWS, d), lambda i: (i, 0))],
        out_specs=pl.BlockSpec((PACK_ROWS * ROW_WORDS, LANES), lambda i: (i, 0)),
        out_shape=jax.ShapeDtypeStruct((n * ROW_WORDS, LANES), jnp.uint32),
        compiler_params=pltpu.CompilerParams(dimension_semantics=("arbitrary",)),
        name="pack_rows_bf16",
    )(tab)


PACK_ROWS = 512


def _pack_kernel(x_ref, o_ref):
    u = pltpu.bitcast(x_ref[...], jnp.uint32)
    r = (u + jnp.uint32(0x7FFF) + ((u >> 16) & jnp.uint32(1))) >> 16
    for s in range(ROW_WORDS):
        lo = r[:, 2 * LANES * s:2 * LANES * s + LANES]
        hi = r[:, 2 * LANES * s + LANES:2 * LANES * (s + 1)]
        o_ref[pl.ds(s, PACK_ROWS, stride=ROW_WORDS), :] = lo | (hi << 16)


def _load_table_once(tab_hbm, tab_v, sem):
    @pl.when(pl.program_id(0) == 0)
    def _():
        cp = pltpu.make_async_copy(tab_hbm, tab_v, sem)
        cp.start()
        cp.wait()


def _tc_token_pipeline(idx_hbm, idx_s, isem, tab_v, per_token):
    step = pl.program_id(0)

    def idx_copy(block, slot):
        return pltpu.make_async_copy(idx_hbm.at[block], idx_s[slot], isem.at[slot])

    @pl.when(step == 0)
    def _():
        idx_copy(0, 0).start()

    for sub in range(TC_SUBS):
        block = step * TC_SUBS + sub
        idx_copy(block, sub).wait()
        if sub + 1 < TC_SUBS:
            idx_copy(block + 1, sub + 1).start()
        else:
            @pl.when(step + 1 < pl.num_programs(0))
            def _():
                idx_copy(block + 1, 0).start()
        offs = idx_s[sub]

        def chunks(i, offs=offs):
            per = PEER_E // TC_KCHUNKS
            for ch in range(TC_KCHUNKS):
                pieces = [tab_v[pl.ds(pl.multiple_of(offs.at[k][i], ROW_WORDS), ROW_WORDS), :]
                          for k in range(ch * per, (ch + 1) * per)]
                yield pltpu.bitcast(jnp.concatenate(pieces, axis=0), BF16)

        def trip(ip, carry, sub=sub, chunks=chunks):
            for u in range(TC_TRIP):
                i = ip * TC_TRIP + u
                per_token(sub * TC_SUB + i, chunks(i))
            return carry
        lax.fori_loop(0, TC_SUB // TC_TRIP, trip, 0)


def _tc_peer_scratch(tabp):
    return [pltpu.VMEM(tabp.shape, jnp.uint32),
            [pltpu.SMEM((PEER_E, TC_SUB), jnp.int32) for _ in range(TC_SUBS)],
            pltpu.SemaphoreType.DMA,
            pltpu.SemaphoreType.DMA((TC_SUBS,))]


def _stack_selectors():
    col = np.arange(STACK_COLS)
    diag = (col[None, :] % ROW_HALVES == np.arange(ROW_HALVES)[:, None]).astype(np.float32)
    owner = (col[:, None] // ROW_HALVES == np.arange(PEER_E)[None, :]).astype(np.float32)
    return jnp.asarray(diag), jnp.asarray(owner)


def _tc_down_kernel(idx_hbm, x_ref, tab_hbm, diag_ref, fold_ref, o_ref,
                    tab_v, idx_s, sem, isem, z_ref):
    _load_table_once(tab_hbm, tab_v, sem)
    cols = STACK_COLS // TC_KCHUNKS

    def per_token(t, chunks):
        xr = x_ref[pl.ds(t, 1), :]
        x8 = jnp.concatenate([xr[:, LANES * j:LANES * (j + 1)] for j in range(ROW_HALVES)],
                             axis=0).astype(BF16)
        parts = []
        for ch, rows in enumerate(chunks):
            a = lax.dot_general(x8, rows, (((1,), (1,)), ((), ())), preferred_element_type=F32)
            parts.append(jnp.sum(a * diag_ref[:, cols * ch:cols * (ch + 1)], axis=0, keepdims=True))
        z_ref[pl.ds(t, 1), :] = jnp.concatenate(parts, axis=1)

    _tc_token_pipeline(idx_hbm, idx_s, isem, tab_v, per_token)
    o_ref[...] = _dot_exact(z_ref[...], fold_ref[...])


def _peer_down_dots_tc(xn, offs, tabp, n_tok):
    diag, owner = _stack_selectors()
    full = lambda a: pl.BlockSpec(a.shape, lambda i: (0,) * a.ndim)
    return pl.pallas_call(
        _tc_down_kernel,
        grid=(n_tok // TC_STEP,),
        in_specs=[pl.BlockSpec(memory_space=pl.ANY),
                  pl.BlockSpec((TC_STEP, xn.shape[1]), lambda i: (i, 0)),
                  pl.BlockSpec(memory_space=pl.ANY), full(diag), full(owner)],
        out_specs=pl.BlockSpec((TC_STEP, PEER_E), lambda i: (i, 0)),
        out_shape=jax.ShapeDtypeStruct((n_tok, PEER_E), F32),
        scratch_shapes=_tc_peer_scratch(tabp) + [pltpu.VMEM((TC_STEP, STACK_COLS), F32)],
        compiler_params=pltpu.CompilerParams(
            dimension_semantics=("arbitrary",), vmem_limit_bytes=VMEM_LIMIT),
        name="peer_down_dots_tc",
    )(offs, xn, tabp, diag, owner)


def _tc_up_kernel(idx_hbm, coef_ref, tab_hbm, spread_ref, diag_ref, o_ref,
                  tab_v, idx_s, sem, isem, ce_ref):
    _load_table_once(tab_hbm, tab_v, sem)
    c = coef_ref[...]
    c_hi = c.astype(BF16)
    c_lo = (c - c_hi.astype(F32)).astype(BF16)
    ce_ref[0] = jnp.dot(c_hi, spread_ref[...], preferred_element_type=F32)
    ce_ref[1] = jnp.dot(c_lo, spread_ref[...], preferred_element_type=F32)

    cols = STACK_COLS // TC_KCHUNKS

    def per_token(t, chunks):
        d = diag_ref[...]
        lhs = jnp.concatenate([(ce_ref[0, pl.ds(t, 1), :] * d).astype(BF16),
                               (ce_ref[1, pl.ds(t, 1), :] * d).astype(BF16)], axis=0)
        r = jnp.zeros((2 * ROW_HALVES, LANES), F32)
        for ch, rows in enumerate(chunks):
            r = r + jnp.dot(lhs[:, cols * ch:cols * (ch + 1)], rows, preferred_element_type=F32)
        out8 = r[0:ROW_HALVES] + r[ROW_HALVES:2 * ROW_HALVES]
        o_ref[pl.ds(t, 1), :] = jnp.concatenate(
            [out8[j:j + 1, :] for j in range(ROW_HALVES)], axis=1)

    _tc_token_pipeline(idx_hbm, idx_s, isem, tab_v, per_token)


def _peer_up_sum_tc(coef, offs, tabp, n_tok):
    d = ROW_HALVES * LANES
    diag, owner = _stack_selectors()
    spread = owner.T.astype(BF16)
    full = lambda a: pl.BlockSpec(a.shape, lambda i: (0,) * a.ndim)
    return pl.pallas_call(
        _tc_up_kernel,
        grid=(n_tok // TC_STEP,),
        in_specs=[pl.BlockSpec(memory_space=pl.ANY),
                  pl.BlockSpec((TC_STEP, PEER_E), lambda i: (i, 0)),
                  pl.BlockSpec(memory_space=pl.ANY), full(spread), full(diag)],
        out_specs=pl.BlockSpec((TC_STEP, d), lambda i: (i, 0)),
        out_shape=jax.ShapeDtypeStruct((n_tok, d), F32),
        scratch_shapes=_tc_peer_scratch(tabp) + [pltpu.VMEM((2, TC_STEP, STACK_COLS), F32)],
        compiler_params=pltpu.CompilerParams(
            dimension_semantics=("arbitrary",), vmem_limit_bytes=VMEM_LIMIT),
        name="peer_up_sum_tc",
    )(offs, coef, tabp, spread, diag)


def _coef_kernel(pre_ref, gate_ref, o_ref):
    x = pre_ref[...]
    gelu = 0.5 * x * (1.0 + lax.erf(x * (2.0 ** -0.5)))
    o_ref[...] = gate_ref[...] * gelu


def _peer_coef(pre, gates):
    t, e = pre.shape
    row = pl.BlockSpec((TM_FINAL, e), lambda i: (i, 0))
    return pl.pallas_call(
        _coef_kernel,
        grid=(t // TM_FINAL,),
        in_specs=[row, row],
        out_specs=row,
        out_shape=jax.ShapeDtypeStruct((t, e), F32),
        compiler_params=pltpu.CompilerParams(dimension_semantics=("arbitrary",)),
        name="peer_gate_gelu",
    )(pre, gates)


def _front_half(x2, row0, rows, bsz, seq, w_pad, layer, i):
    (norm1_g, ret_norm_g, m_conv_w, m_conv_b, m_wq, m_wk, m_b_i, m_b_f, m_norm_g, m_skip,
     w_out, norm2_g, peer_wq, keys) = layer
    d = x2.shape[1]
    proj = _rms_proj(x2, row0, rows, norm1_g[i].reshape(1, d), w_pad)
    y = _mixer(proj, bsz, seq, ret_norm_g[i], m_conv_w[i], m_conv_b[i], m_wq[i], m_wk[i],
               m_b_i[i], m_b_f[i], m_norm_g[i], m_skip[i])
    x1, xn, sub_t = _post(y, x2, row0, w_out, norm2_g[i].reshape(1, d), peer_wq, keys)
    return (x1, xn) + tuple(_peer_route(sub_t))


def kernel(x, norm1_g, w_in, ret_norm_g, m_conv_w, m_conv_b, m_wq, m_wk, m_b_i, m_b_f,
           m_norm_g, m_skip, w_out, norm2_g, peer_wq, peer_keys, peer_down, peer_up, final_g):
    b, s, d = x.shape
    t = b * s
    tp = t // FRONT_PARTS
    x2 = x.reshape(t, d)
    depth = norm1_g.shape[0]
    for i in range(depth):
        if i > 0:
            x2 = jnp.concatenate(x1s, axis=0) + jnp.concatenate([p_tc, p_sc], axis=0)
        wi = w_in[i]
        n_main = OFF_GI
        gate_pad = lambda w: jnp.pad(w, ((0, 0), (0, LANES - w.shape[1])))
        w_pad = jnp.concatenate(
            [wi[:, :n_main], gate_pad(wi[:, n_main:n_main + MLSTM_HEADS]),
             gate_pad(wi[:, n_main + MLSTM_HEADS:])], axis=1).astype(BF16)
        keys = peer_keys[i].reshape(2 * PEER_HEADS, PEER_NKEYS, PEER_HALF).astype(BF16)
        layer = (norm1_g, ret_norm_g, m_conv_w, m_conv_b, m_wq, m_wk, m_b_i, m_b_f, m_norm_g,
                 m_skip, w_out[i].astype(BF16), norm2_g, peer_wq[i].astype(BF16), keys)
        down_p = _pack_rows_bf16(peer_down[i])
        up_p = _pack_rows_bf16(peer_up[i])

        fronts = []
        pre = [None] * FRONT_PARTS
        for q in range(FRONT_PARTS):
            fr = _front_half(x2, q * tp, tp, b // FRONT_PARTS, s, w_pad, layer, i)
            fronts.append(fr)
            if q < SC_DOWN_PARTS:
                pre[q] = [_peer_down_dots(fr[1], fr[2], peer_down[i], 0)]
            elif q == SC_DOWN_PARTS:
                pre[q] = [None, _peer_down_dots(fr[1], fr[2], peer_down[i], tp - SC_DOWN_EXTRA)]
        for q in range(SC_DOWN_PARTS, FRONT_PARTS):
            n_tc_q = tp - SC_DOWN_EXTRA if q == SC_DOWN_PARTS else tp
            pre_tc = _peer_down_dots_tc(fronts[q][1], fronts[q][4], down_p, n_tc_q)
            if q == SC_DOWN_PARTS:
                pre[q][0] = pre_tc
            else:
                pre[q] = [pre_tc]
        pre = [piece for part in pre for piece in part]
        x1s = [fr[0] for fr in fronts]
        expert = jnp.concatenate([fr[2] for fr in fronts], axis=0)
        gates = jnp.concatenate([fr[3] for fr in fronts], axis=0)
        offs = jnp.concatenate([fr[4] for fr in fronts], axis=0)
        coef = _peer_coef(jnp.concatenate(pre, axis=0), gates)
        n_tc = PEER_TC_UP_TOKENS
        p_sc = _peer_up_sum(coef, expert, peer_up[i], n_tc)
        p_tc = _peer_up_sum_tc(coef, offs, up_p, n_tc)
    out = _final(_row_ranges(x1s, [p_tc, p_sc]), final_g.reshape(1, d), t)
    return out.reshape(b, s, d)


def _row_ranges(xs, ps):
    def starts(arrs):
        out = [0]
        for a in arrs:
            out.append(out[-1] + a.shape[0])
        return out
    xb, pb = starts(xs), starts(ps)
    assert xb[-1] == pb[-1]
    cuts = sorted(set(xb) | set(pb))
    parts = []
    for lo, hi in zip(cuts[:-1], cuts[1:]):
        xi = max(k for k in range(len(xs)) if xb[k] <= lo)
        pi = max(k for k in range(len(ps)) if pb[k] <= lo)
        parts.append((xs[xi], lo - xb[xi], ps[pi], lo - pb[pi], hi - lo))
    return parts
```

```python
import functools

import numpy as np
import jax
import jax.numpy as jnp
from jax import lax
from jax.experimental import pallas as pl
from jax.experimental.pallas import tpu as pltpu
from jax.experimental.pallas import tpu_sc as plsc

F32 = jnp.float32
BF16 = jnp.bfloat16

D_MODEL = 1024
CHUNK = 64
RET_HEADS = 4
RET_DK = 128
MLSTM_HEADS = 4
MLSTM_D = 128
HEAD_W = 128
D_GROUP = 512
CONV_W = 4
ROPE_BASE = 10000.0
PEER_HEADS = 8
PEER_NKEYS = 128
PEER_TOPK = 16
PEER_HALF = 128
PEER_BLOCK = 128
EPS = 1e-6

OFF_RQ, OFF_RK, OFF_RV, OFF_RG = 0, 512, 1024, 1536
OFF_MX, OFF_MV, OFF_MO = 2048, 2560, 3072
OFF_GI, OFF_GF = 3584, 3712
D_PROJ_PAD = 3840

LANES = 128
CONV_TAIL = 8
VMEM_LIMIT = 56 * 1024 * 1024

TM_PROJ = 256
L_BLOCK = 256
TM_FINAL = 256


def _rms(x, g):
    ms = jnp.mean(x * x, axis=-1, keepdims=True)
    return x * lax.rsqrt(ms + EPS) * g


def _group_norm(h, g):
    mu = jnp.mean(h, axis=-1, keepdims=True)
    d = h - mu
    var = jnp.mean(d * d, axis=-1, keepdims=True)
    return d * lax.rsqrt(var + EPS) * g


def _silu(x):
    return x * (1.0 / (1.0 + jnp.exp(-x)))


def _sigmoid(x):
    return 1.0 / (1.0 + jnp.exp(-x))


def _dot(a, b):
    return jnp.dot(a.astype(BF16), b.astype(BF16), preferred_element_type=F32)


def _dot_tn(a, b):
    return lax.dot_general(a.astype(BF16), b.astype(BF16), (((0,), (0,)), ((), ())),
                           preferred_element_type=F32)


def _dot_nt(a, b):
    return lax.dot_general(a.astype(BF16), b.astype(BF16), (((1,), (1,)), ((), ())),
                           preferred_element_type=F32)


def _rms_proj_kernel(x_ref, g_ref, w_ref, o_ref):
    h = _rms(x_ref[...], g_ref[...])
    o_ref[...] = jnp.dot(h.astype(BF16), w_ref[...], preferred_element_type=F32)


def _rms_proj(x2, row0, rows, g, w):
    d = x2.shape[1]
    t = rows
    n = w.shape[1]
    b0 = row0 // TM_PROJ
    return pl.pallas_call(
        _rms_proj_kernel,
        grid=(t // TM_PROJ,),
        in_specs=[
            pl.BlockSpec((TM_PROJ, d), lambda i: (i + b0, 0)),
            pl.BlockSpec((1, d), lambda i: (0, 0)),
            pl.BlockSpec((d, n), lambda i: (0, 0)),
        ],
        out_specs=pl.BlockSpec((TM_PROJ, n), lambda i: (i, 0)),
        out_shape=jax.ShapeDtypeStruct((t, n), F32),
        compiler_params=pltpu.CompilerParams(
            dimension_semantics=("arbitrary",), vmem_limit_bytes=VMEM_LIMIT),
        name="rms_in_proj",
    )(x2, g, w)


def _mixer_kernel(chunk_decay,
                  proj_ref, cos_ref, sin_ref, intra_ref, qd_ref, kd_ref, tri_ref,
                  rng_ref, cw_ref, cb_ref, wq_ref, wk_ref, bi_ref, bf_ref,
                  mng_ref, skip_ref,
                  y_ref,
                  rstate_ref, cstate_ref, m_ref, tail_ref):
    j = pl.program_id(1)
    n_chunks = L_BLOCK // CHUNK

    @pl.when(j == 0)
    def _():
        rstate_ref[...] = jnp.zeros_like(rstate_ref)
        cstate_ref[...] = jnp.zeros_like(cstate_ref)
        m_ref[...] = jnp.zeros_like(m_ref)
        tail_ref[...] = jnp.zeros_like(tail_ref)

    cosv = cos_ref[...]
    sinv = sin_ref[...]
    k_scale = RET_DK ** -0.5

    for h in range(RET_HEADS):
        lo = h * HEAD_W
        q = proj_ref[:, OFF_RQ + lo:OFF_RQ + lo + HEAD_W]
        k = proj_ref[:, OFF_RK + lo:OFF_RK + lo + HEAD_W]
        q = q * cosv + pltpu.roll(q, HEAD_W // 2, 1) * sinv
        k = (k * cosv + pltpu.roll(k, HEAD_W // 2, 1) * sinv) * k_scale
        intra = intra_ref[h]
        qd = qd_ref[h]
        kd = kd_ref[h]
        g = rng_ref[:, lo:lo + HEAD_W]
        for c in range(n_chunks):
            r0 = c * CHUNK
            qc = q[r0:r0 + CHUNK]
            kc = k[r0:r0 + CHUNK]
            vc = proj_ref[r0:r0 + CHUNK, OFF_RV + lo:OFF_RV + lo + HEAD_W]
            gate = proj_ref[r0:r0 + CHUNK, OFF_RG + lo:OFF_RG + lo + HEAD_W]
            state = rstate_ref[h]
            scores = _dot_nt(qc, kc) * intra
            o = _dot(scores, vc) + _dot(qc, state) * qd
            rstate_ref[h] = chunk_decay[h] * state + _dot_tn(kc * kd, vc)
            y_ref[r0:r0 + CHUNK, lo:lo + HEAD_W] = _silu(gate) * _group_norm(o, g)

    mx = proj_ref[:, OFF_MX:OFF_MX + D_GROUP]
    xp = jnp.concatenate([tail_ref[...], mx], axis=0)
    base = CONV_TAIL - (CONV_W - 1)
    conv = cb_ref[...] + xp[base:base + L_BLOCK] * cw_ref[0:1, :]
    for t in range(1, CONV_W):
        conv = conv + xp[base + t:base + t + L_BLOCK] * cw_ref[t:t + 1, :]
    tail_ref[...] = mx[L_BLOCK - CONV_TAIL:L_BLOCK]
    cact = _silu(conv)

    gi = proj_ref[:, OFF_GI:OFF_GI + LANES] + bi_ref[...]
    gf = proj_ref[:, OFF_GF:OFF_GF + LANES] + bf_ref[...]
    logf = jnp.minimum(gf, 0.0) - jnp.log1p(jnp.exp(-jnp.abs(gf)))
    tri = tri_ref[...]
    lane = lax.broadcasted_iota(jnp.int32, (CHUNK, HEAD_W), 1)
    ones_col = (lane == 0).astype(F32)
    m_scale = MLSTM_D ** -0.5

    mq = []
    mk = []
    for h in range(MLSTM_HEADS):
        lo = h * HEAD_W
        ch = cact[:, lo:lo + HEAD_W]
        mq.append(_dot(ch, wq_ref[h]))
        mk.append(_dot(ch, wk_ref[h]) * m_scale)

    for c in range(n_chunks):
        r0 = c * CHUNK
        fcum = jnp.dot(tri, logf[r0:r0 + CHUNK], preferred_element_type=F32,
                       precision=lax.Precision.HIGHEST)
        a = fcum[CHUNK - 1:CHUNK]
        logw = a - fcum + gi[r0:r0 + CHUNK]
        m_in = jnp.max(logw, axis=0, keepdims=True)
        m_old = m_ref[...]
        m_new = jnp.maximum(a + m_old, m_in)
        decay = jnp.exp(a + m_old - m_new)
        w = jnp.exp(logw - m_new)
        em = jnp.exp(-m_new)
        m_ref[...] = m_new
        for h in range(MLSTM_HEADS):
            lo = h * HEAD_W
            qh = mq[h][r0:r0 + CHUNK]
            kh = mk[h][r0:r0 + CHUNK]
            vh = proj_ref[r0:r0 + CHUNK, OFF_MV + lo:OFF_MV + lo + HEAD_W]
            oh = proj_ref[r0:r0 + CHUNK, OFF_MO + lo:OFF_MO + lo + HEAD_W]
            ch = cact[r0:r0 + CHUNK, lo:lo + HEAD_W]
            v_ext = jnp.concatenate([vh, ones_col], axis=1)
            cmem = decay[:, h:h + 1] * cstate_ref[h] + _dot_tn(kh * w[:, h:h + 1], v_ext)
            cstate_ref[h] = cmem
            num = _dot(qh, cmem)
            den = jnp.maximum(jnp.abs(num[:, HEAD_W:HEAD_W + 1]), em[:, h:h + 1])
            hh = num[:, :HEAD_W] / den
            hm = _group_norm(hh, mng_ref[:, lo:lo + HEAD_W])
            ym = _sigmoid(oh) * (hm + skip_ref[:, lo:lo + HEAD_W] * ch)
            y_ref[r0:r0 + CHUNK, D_GROUP + lo:D_GROUP + lo + HEAD_W] = ym


def _retention_tables():
    h = np.arange(RET_HEADS, dtype=np.float64)
    log_g = np.log(1.0 - 2.0 ** (-5.0 - h))
    l = np.arange(CHUNK, dtype=np.float64)
    intra = np.exp(log_g[:, None, None] * np.abs(l[:, None] - l[None, :]))
    qd = np.exp(log_g[:, None] * (l + 1.0))[:, :, None] * np.ones((1, 1, HEAD_W))
    kd = np.exp(log_g[:, None] * (CHUNK - 1.0 - l))[:, :, None] * np.ones((1, 1, HEAD_W))
    cd = np.exp(log_g * CHUNK)
    return (jnp.asarray(intra, F32), jnp.asarray(qd, F32), jnp.asarray(kd, F32),
            tuple(float(v) for v in cd))


def _rotary_tables(s):
    half = HEAD_W // 2
    inv = ROPE_BASE ** (-np.arange(half, dtype=np.float64) / half)
    ang = np.arange(s, dtype=np.float64)[:, None] * inv[None, :]
    cos = np.cos(ang)
    sin = np.sin(ang)
    return (jnp.asarray(np.concatenate([cos, cos], axis=-1), F32),
            jnp.asarray(np.concatenate([-sin, sin], axis=-1), F32))


def _mixer(proj, b, s, ret_norm_g, m_conv_w, m_conv_b, m_wq, m_wk, m_b_i, m_b_f,
           m_norm_g, m_skip):
    intra, qd, kd, cd = _retention_tables()
    cos_t, sin_t = _rotary_tables(s)
    tri = jnp.asarray(np.tril(np.ones((CHUNK, CHUNK))), F32)
    nj = s // L_BLOCK
    pad = lambda v: jnp.pad(v.reshape(1, -1), ((0, 0), (0, LANES - v.size)))
    full = lambda shape: pl.BlockSpec(shape, lambda bi, j: (0,) * len(shape))
    return pl.pallas_call(
        functools.partial(_mixer_kernel, cd),
        grid=(b, nj),
        in_specs=[
            pl.BlockSpec((L_BLOCK, D_PROJ_PAD), lambda bi, j: (bi * nj + j, 0)),
            pl.BlockSpec((L_BLOCK, HEAD_W), lambda bi, j: (j, 0)),
            pl.BlockSpec((L_BLOCK, HEAD_W), lambda bi, j: (j, 0)),
            full((RET_HEADS, CHUNK, CHUNK)),
            full((RET_HEADS, CHUNK, HEAD_W)),
            full((RET_HEADS, CHUNK, HEAD_W)),
            full((CHUNK, CHUNK)),
            full((1, D_GROUP)),
            full((CONV_W, D_GROUP)),
            full((1, D_GROUP)),
            full((MLSTM_HEADS, MLSTM_D, MLSTM_D)),
            full((MLSTM_HEADS, MLSTM_D, MLSTM_D)),
            full((1, LANES)),
            full((1, LANES)),
            full((1, D_GROUP)),
            full((1, D_GROUP)),
        ],
        out_specs=pl.BlockSpec((L_BLOCK, 2 * D_GROUP), lambda bi, j: (bi * nj + j, 0)),
        out_shape=jax.ShapeDtypeStruct((b * s, 2 * D_GROUP), F32),
        scratch_shapes=[
            pltpu.VMEM((RET_HEADS, RET_DK, HEAD_W), F32),
            pltpu.VMEM((MLSTM_HEADS, MLSTM_D, 2 * HEAD_W), F32),
            pltpu.VMEM((1, LANES), F32),
            pltpu.VMEM((CONV_TAIL, D_GROUP), F32),
        ],
        compiler_params=pltpu.CompilerParams(
            dimension_semantics=("arbitrary", "arbitrary"), vmem_limit_bytes=VMEM_LIMIT),
        name="retention_mlstm_mixer",
    )(proj, cos_t, sin_t, intra, qd, kd, tri,
      ret_norm_g.reshape(1, D_GROUP), m_conv_w, m_conv_b.reshape(1, D_GROUP),
      m_wq.astype(BF16), m_wk.astype(BF16), pad(m_b_i), pad(m_b_f),
      m_norm_g.reshape(1, D_GROUP), m_skip.reshape(1, D_GROUP))


def _post_kernel(y_ref, x_ref, wout_ref, g2_ref, wq_ref, keys_ref,
                 x1_ref, xn_ref, sub_ref):
    x1 = x_ref[...] + jnp.dot(y_ref[...].astype(BF16), wout_ref[...],
                              preferred_element_type=F32)
    x1_ref[...] = x1
    xn = _rms(x1, g2_ref[...])
    xn_ref[...] = xn
    q = jnp.dot(xn.astype(BF16), wq_ref[...], preferred_element_type=F32)
    for i in range(2 * PEER_HEADS):
        lo = i * PEER_HALF
        sub_ref[lo:lo + PEER_NKEYS, :] = _dot_nt(keys_ref[i], q[:, lo:lo + PEER_HALF])


def _post(y, x2, row0, w_out, norm2_g, peer_wq, keys):
    t, d = y.shape
    nq = peer_wq.shape[1]
    b0 = row0 // TM_PROJ
    row = lambda w: pl.BlockSpec((TM_PROJ, w), lambda i: (i, 0))
    full = lambda shape: pl.BlockSpec(shape, lambda i: (0,) * len(shape))
    return pl.pallas_call(
        _post_kernel,
        grid=(t // TM_PROJ,),
        in_specs=[row(d), pl.BlockSpec((TM_PROJ, d), lambda i: (i + b0, 0)), full((d, d)),
                  full((1, d)), full((d, nq)), full(keys.shape)],
        out_specs=[row(d), row(d), pl.BlockSpec((nq, TM_PROJ), lambda i: (0, i))],
        out_shape=[jax.ShapeDtypeStruct((t, d), F32), jax.ShapeDtypeStruct((t, d), F32),
                   jax.ShapeDtypeStruct((nq, t), F32)],
        compiler_params=pltpu.CompilerParams(
            dimension_semantics=("arbitrary",), vmem_limit_bytes=VMEM_LIMIT),
        name="out_proj_peer_scores",
    )(y, x2, w_out, norm2_g, peer_wq, keys)


def _final_kernel(x1_ref, p_ref, g_ref, *rest):
    o_ref = rest[-1]
    o_ref[...] = _rms(x1_ref[...] + p_ref[...], g_ref[...])


def _final(parts, g, t):
    d = g.shape[1]
    spec = lambda r0: pl.BlockSpec((TM_FINAL, d), lambda i, b0=r0 // TM_FINAL: (i + b0, 0))
    out = None
    row0 = 0
    for x1, x_row0, p, p_row0, rows in parts:
        in_specs = [spec(x_row0), spec(p_row0), pl.BlockSpec((1, d), lambda i: (0, 0))]
        args = [x1, p, g]
        if out is not None:
            in_specs.append(pl.BlockSpec(memory_space=pl.ANY))
            args.append(out)
        out = pl.pallas_call(
            _final_kernel,
            grid=(rows // TM_FINAL,),
            in_specs=in_specs,
            out_specs=spec(row0),
            out_shape=jax.ShapeDtypeStruct((t, d), F32),
            input_output_aliases={} if len(args) == 3 else {3: 0},
            compiler_params=pltpu.CompilerParams(dimension_semantics=("arbitrary",)),
            name="final_rmsnorm",
        )(*args)
        row0 += rows
    assert row0 == t
    return out


TM_ROUTE = 256
CAND_ROWS = 56


def _candidate_tables():
    pairs = [(k1, k2) for k1 in range(PEER_TOPK) for k2 in range(PEER_TOPK)
             if (k1 + 1) * (k2 + 1) <= PEER_TOPK]
    assert len(pairs) <= CAND_ROWS
    rep = np.zeros((CAND_ROWS, 2 * PEER_TOPK), np.float32)
    til = np.zeros((CAND_ROWS, 2 * PEER_TOPK), np.float32)
    for pos, (k1, k2) in enumerate(pairs):
        rep[pos, k1] = 1.0
        til[pos, PEER_TOPK + k2] = 1.0
    return jnp.asarray(rep), jnp.asarray(til), len(pairs)


def _dot_exact(a, b):
    return jnp.dot(a, b, preferred_element_type=F32, precision=lax.Precision.HIGHEST)


def _route_kernel(n_cand, sub_ref, rep_ref, til_ref, seg_ref, exp_ref, gate_ref, offs_ref,
                  s_ref, i_ref, best_ref, eid_ref, m0_ref):
    tm = sub_ref.shape[1]
    neg = -jnp.inf
    key_id = lax.broadcasted_iota(jnp.int32, (PEER_NKEYS, tm), 0).astype(F32)
    cand_id = lax.broadcasted_iota(jnp.int32, (CAND_ROWS, tm), 0).astype(F32)
    rep = rep_ref[...]
    til = til_ref[...]
    for h in range(PEER_HEADS):
        for p in range(2):
            lo = (2 * h + p) * PEER_NKEYS
            x = sub_ref[lo:lo + PEER_NKEYS, :]
            for k in range(PEER_TOPK):
                m = jnp.max(x, axis=0, keepdims=True)
                idx = jnp.min(jnp.where(x == m, key_id, float(PEER_NKEYS)), axis=0, keepdims=True)
                x = jnp.where(key_id == idx, neg, x)
                r = p * PEER_TOPK + k
                s_ref[r:r + 1, :] = m
                i_ref[r:r + 1, :] = idx
        s = s_ref[...]
        iv = i_ref[...]
        cand = _dot_exact(rep, s) + _dot_exact(til, s)
        cand = jnp.where(cand_id < float(n_cand), cand, neg)
        eid = _dot(rep, iv) * float(PEER_NKEYS) + _dot(til, iv)
        for k in range(PEER_TOPK):
            m = jnp.max(cand, axis=0, keepdims=True)
            pos = jnp.min(jnp.where(cand == m, cand_id, float(CAND_ROWS)), axis=0, keepdims=True)
            sel = cand_id == pos
            e = jnp.max(jnp.where(sel, eid, -1.0), axis=0, keepdims=True)
            cand = jnp.where(sel, neg, cand)
            r = h * PEER_TOPK + k
            best_ref[r:r + 1, :] = m
            eid_ref[r:r + 1, :] = e
            if k == 0:
                m0_ref[r:r + PEER_TOPK, :] = jnp.broadcast_to(m, (PEER_TOPK, tm))
    pexp = jnp.exp(best_ref[...] - m0_ref[...])
    gates = pexp / _dot_exact(seg_ref[...], pexp)
    gate_ref[...] = gates.T
    eid = eid_ref[...].astype(jnp.int32)
    exp_ref[...] = eid.T
    for blk in range(tm // LANES):
        offs_ref[blk] = eid[:, blk * LANES:(blk + 1) * LANES] * ROW_WORDS


def _peer_route(sub_t):
    nq, t = sub_t.shape
    rep, til, n_cand = _candidate_tables()
    e = PEER_HEADS * PEER_TOPK
    seg = jnp.asarray(np.kron(np.eye(PEER_HEADS), np.ones((PEER_TOPK, PEER_TOPK))), F32)
    full = lambda a: pl.BlockSpec(a.shape, lambda i: (0,) * a.ndim)
    out = pl.BlockSpec((TM_ROUTE, e), lambda i: (i, 0))
    return pl.pallas_call(
        functools.partial(_route_kernel, n_cand),
        grid=(t // TM_ROUTE,),
        in_specs=[pl.BlockSpec((nq, TM_ROUTE), lambda i: (0, i)), full(rep), full(til), full(seg)],
        out_specs=[out, out, pl.BlockSpec((TM_ROUTE // LANES, e, LANES), lambda i: (i, 0, 0))],
        out_shape=[jax.ShapeDtypeStruct((t, e), jnp.int32), jax.ShapeDtypeStruct((t, e), F32),
                   jax.ShapeDtypeStruct((t // LANES, e, LANES), jnp.int32)],
        scratch_shapes=[
            pltpu.VMEM((2 * PEER_TOPK, TM_ROUTE), F32),
            pltpu.VMEM((2 * PEER_TOPK, TM_ROUTE), F32),
            pltpu.VMEM((e, TM_ROUTE), F32),
            pltpu.VMEM((e, TM_ROUTE), F32),
            pltpu.VMEM((e, TM_ROUTE), F32),
        ],
        compiler_params=pltpu.CompilerParams(dimension_semantics=("arbitrary",)),
        name="peer_topk_route",
    )(sub_t, rep, til, seg)


SC_CORES = 2
SC_SUBCORES = 16
SC_LANES = 16
SC_WORKERS = SC_CORES * SC_SUBCORES
PEER_E = PEER_HEADS * PEER_TOPK
SC_ROWS = 32
SC_GATHERS = PEER_E // SC_ROWS
SC_TOK_BLOCK = 8
SC_QUARTER = 256


def _sc_mesh():
    return plsc.VectorSubcoreMesh(core_axis_name="c", subcore_axis_name="s")


def _sc_pipeline(idx_v, tab_hbm, rows_v, sems, compute):
    def gather(step, buf):
        return pltpu.make_async_copy(tab_hbm.at[idx_v.at[step]], rows_v.at[buf], sems.at[buf])

    gather(0, 0).start()

    @pl.loop(0, SC_TOK_BLOCK)
    def _(i):
        for c in range(SC_GATHERS):
            buf = c % 2
            step = i * SC_GATHERS + c
            gather(step, buf).wait()
            if c + 1 < SC_GATHERS:
                gather(step + 1, 1 - buf).start()
            else:
                @pl.when(i + 1 < SC_TOK_BLOCK)
                def _():
                    gather(step + 1, 1 - buf).start()
            compute(i, c, step, buf)


def _peer_down_dots(xn, idx, down, tok_start):
    t_all, d = xn.shape
    t = t_all - tok_start
    tok_w = t // SC_WORKERS
    nblk = tok_w // SC_TOK_BLOCK
    assert nblk * SC_TOK_BLOCK * SC_WORKERS == t
    lanes = SC_LANES

    @functools.partial(
        pl.kernel, mesh=_sc_mesh(),
        out_type=jax.ShapeDtypeStruct((t, PEER_E), F32),
        scratch_types=[
            pltpu.VMEM((SC_TOK_BLOCK * SC_GATHERS, SC_ROWS), jnp.int32),
            pltpu.VMEM((SC_TOK_BLOCK, d), F32),
            pltpu.VMEM((2, SC_ROWS, d), F32),
            pltpu.VMEM((SC_ROWS, lanes), F32),
            pltpu.VMEM((SC_TOK_BLOCK, PEER_E), F32),
            pltpu.SemaphoreType.DMA((2,)),
        ],
        compiler_params=pltpu.CompilerParams(needs_layout_passes=False),
        name="peer_down_dots",
    )
    def k(x_hbm, idx_hbm, tab_hbm, out_hbm, idx_v, x_v, rows_v, acc_v, out_v, sems):
        wid = lax.axis_index("s") * SC_CORES + lax.axis_index("c")
        lane = lax.iota(jnp.int32, lanes)

        @pl.loop(0, nblk)
        def _(blk):
            out0 = wid * tok_w + blk * SC_TOK_BLOCK
            tok0 = tok_start + out0
            pltpu.sync_copy(idx_hbm.at[pl.ds(tok0 * SC_GATHERS, SC_TOK_BLOCK * SC_GATHERS)], idx_v)
            pltpu.sync_copy(x_hbm.at[pl.ds(tok0, SC_TOK_BLOCK)], x_v)

            def compute(i, c, step, buf):
                def body(dc, accs):
                    xv = x_v[i, pl.ds(dc * lanes, lanes)]
                    return tuple(a + rows_v[buf, r, pl.ds(dc * lanes, lanes)] * xv
                                 for r, a in enumerate(accs))
                accs = lax.fori_loop(0, d // lanes, body,
                                     tuple(jnp.zeros((lanes,), F32) for _ in range(SC_ROWS)))
                for r in range(SC_ROWS):
                    acc_v[r, :] = accs[r]
                for g in range(SC_ROWS // lanes):
                    tot = jnp.zeros((lanes,), F32)
                    for l in range(lanes):
                        tot = tot + plsc.load_gather(
                            acc_v, [lane + g * lanes, jnp.full((lanes,), l, jnp.int32)])
                    out_v[i, pl.ds(c * SC_ROWS + g * lanes, lanes)] = tot

            _sc_pipeline(idx_v, tab_hbm, rows_v, sems, compute)
            pltpu.sync_copy(out_v, out_hbm.at[pl.ds(out0, SC_TOK_BLOCK)])

    return k(xn, idx.reshape(t_all * SC_GATHERS, SC_ROWS), down)


def _peer_up_sum(coef, idx, up, tok_start):
    t_all = coef.shape[0]
    t = t_all - tok_start
    d = up.shape[1]
    tok_w = t // SC_WORKERS
    nblk = tok_w // SC_TOK_BLOCK
    assert nblk * SC_TOK_BLOCK * SC_WORKERS == t
    lanes = SC_LANES
    nj = SC_QUARTER // lanes

    @functools.partial(
        pl.kernel, mesh=_sc_mesh(),
        out_type=jax.ShapeDtypeStruct((t, d), F32),
        scratch_types=[
            pltpu.VMEM((SC_TOK_BLOCK * SC_GATHERS, SC_ROWS), jnp.int32),
            pltpu.VMEM((SC_TOK_BLOCK * PEER_E,), F32),
            pltpu.VMEM((2, SC_ROWS, d), F32),
            pltpu.VMEM((SC_TOK_BLOCK, d), F32),
            pltpu.SemaphoreType.DMA((2,)),
        ],
        compiler_params=pltpu.CompilerParams(needs_layout_passes=False),
        name="peer_up_sum",
    )
    def k(coef_hbm, idx_hbm, tab_hbm, out_hbm, idx_v, coef_v, rows_v, out_v, sems):
        wid = lax.axis_index("s") * SC_CORES + lax.axis_index("c")

        @pl.loop(0, nblk)
        def _(blk):
            out0 = wid * tok_w + blk * SC_TOK_BLOCK
            tok0 = tok_start + out0
            pltpu.sync_copy(idx_hbm.at[pl.ds(tok0 * SC_GATHERS, SC_TOK_BLOCK * SC_GATHERS)], idx_v)
            pltpu.sync_copy(coef_hbm.at[pl.ds(tok0 * PEER_E, SC_TOK_BLOCK * PEER_E)], coef_v)

            def compute(i, c, step, buf):
                for q in range(d // SC_QUARTER):
                    col = lambda j: pl.ds(q * SC_QUARTER + j * lanes, lanes)
                    if c == 0:
                        init = tuple(jnp.zeros((lanes,), F32) for _ in range(nj))
                    else:
                        init = tuple(out_v[i, col(j)] for j in range(nj))

                    def body(r, accs):
                        cf = plsc.load_gather(
                            coef_v, [jnp.full((lanes,), step * SC_ROWS + r, jnp.int32)])
                        return tuple(a + rows_v[buf, r, col(j)] * cf for j, a in enumerate(accs))
                    accs = lax.fori_loop(0, SC_ROWS, body, init)
                    for j in range(nj):
                        out_v[i, col(j)] = accs[j]

            _sc_pipeline(idx_v, tab_hbm, rows_v, sems, compute)
            pltpu.sync_copy(out_v, out_hbm.at[pl.ds(out0, SC_TOK_BLOCK)])

    return k(coef.reshape(t_all * PEER_E), idx.reshape(t_all * SC_GATHERS, SC_ROWS), up)


TC_SUB = 128
TC_SUBS = 2
TC_STEP = TC_SUB * TC_SUBS
TC_TRIP = 8
TC_KCHUNKS = 4
ROW_WORDS = 4
ROW_HALVES = 2 * ROW_WORDS
STACK_COLS = PEER_E * ROW_HALVES
FRONT_PARTS = 4
SC_DOWN_PARTS = 2
SC_DOWN_EXTRA = 0
PEER_TC_UP_TOKENS = 24320


def _pack_rows_bf16(tab):
    n, d = tab.shape
    return pl.pallas_call(
        _pack_kernel,
        grid=(n // PACK_ROWS,),
        in_specs=[pl.BlockSpec((PACK_ROWS, d), lambda i: (i, 0))],
        out_specs=pl.BlockSpec((PACK_ROWS * ROW_WORDS, LANES), lambda i: (i, 0)),
        out_shape=jax.ShapeDtypeStruct((n * ROW_WORDS, LANES), jnp.uint32),
        compiler_params=pltpu.CompilerParams(dimension_semantics=("arbitrary",)),
        name="pack_rows_bf16",
    )(tab)


PACK_ROWS = 512


def _pack_kernel(x_ref, o_ref):
    u = pltpu.bitcast(x_ref[...], jnp.uint32)
    r = (u + jnp.uint32(0x7FFF) + ((u >> 16) & jnp.uint32(1))) >> 16
    for s in range(ROW_WORDS):
        lo = r[:, 2 * LANES * s:2 * LANES * s + LANES]
        hi = r[:, 2 * LANES * s + LANES:2 * LANES * (s + 1)]
        o_ref[pl.ds(s, PACK_ROWS, stride=ROW_WORDS), :] = lo | (hi << 16)


def _load_table_once(tab_hbm, tab_v, sem):
    @pl.when(pl.program_id(0) == 0)
    def _():
        cp = pltpu.make_async_copy(tab_hbm, tab_v, sem)
        cp.start()
        cp.wait()


def _tc_token_pipeline(idx_hbm, idx_s, isem, tab_v, per_token):
    step = pl.program_id(0)

    def idx_copy(block, slot):
        return pltpu.make_async_copy(idx_hbm.at[block], idx_s[slot], isem.at[slot])

    @pl.when(step == 0)
    def _():
        idx_copy(0, 0).start()

    for sub in range(TC_SUBS):
        block = step * TC_SUBS + sub
        idx_copy(block, sub).wait()
        if sub + 1 < TC_SUBS:
            idx_copy(block + 1, sub + 1).start()
        else:
            @pl.when(step + 1 < pl.num_programs(0))
            def _():
                idx_copy(block + 1, 0).start()
        offs = idx_s[sub]

        def chunks(i, offs=offs):
            per = PEER_E // TC_KCHUNKS
            for ch in range(TC_KCHUNKS):
                pieces = [tab_v[pl.ds(pl.multiple_of(offs.at[k][i], ROW_WORDS), ROW_WORDS), :]
                          for k in range(ch * per, (ch + 1) * per)]
                yield pltpu.bitcast(jnp.concatenate(pieces, axis=0), BF16)

        def trip(ip, carry, sub=sub, chunks=chunks):
            for u in range(TC_TRIP):
                i = ip * TC_TRIP + u
                per_token(sub * TC_SUB + i, chunks(i))
            return carry
        lax.fori_loop(0, TC_SUB // TC_TRIP, trip, 0)


def _tc_peer_scratch(tabp):
    return [pltpu.VMEM(tabp.shape, jnp.uint32),
            [pltpu.SMEM((PEER_E, TC_SUB), jnp.int32) for _ in range(TC_SUBS)],
            pltpu.SemaphoreType.DMA,
            pltpu.SemaphoreType.DMA((TC_SUBS,))]


def _stack_selectors():
    col = np.arange(STACK_COLS)
    diag = (col[None, :] % ROW_HALVES == np.arange(ROW_HALVES)[:, None]).astype(np.float32)
    owner = (col[:, None] // ROW_HALVES == np.arange(PEER_E)[None, :]).astype(np.float32)
    return jnp.asarray(diag), jnp.asarray(owner)


def _tc_down_kernel(idx_hbm, x_ref, tab_hbm, diag_ref, fold_ref, o_ref,
                    tab_v, idx_s, sem, isem, z_ref):
    _load_table_once(tab_hbm, tab_v, sem)
    cols = STACK_COLS // TC_KCHUNKS

    def per_token(t, chunks):
        xr = x_ref[pl.ds(t, 1), :]
        x8 = jnp.concatenate([xr[:, LANES * j:LANES * (j + 1)] for j in range(ROW_HALVES)],
                             axis=0).astype(BF16)
        parts = []
        for ch, rows in enumerate(chunks):
            a = lax.dot_general(x8, rows, (((1,), (1,)), ((), ())), preferred_element_type=F32)
            parts.append(jnp.sum(a * diag_ref[:, cols * ch:cols * (ch + 1)], axis=0, keepdims=True))
        z_ref[pl.ds(t, 1), :] = jnp.concatenate(parts, axis=1)

    _tc_token_pipeline(idx_hbm, idx_s, isem, tab_v, per_token)
    o_ref[...] = _dot_exact(z_ref[...], fold_ref[...])


def _peer_down_dots_tc(xn, offs, tabp, n_tok):
    diag, owner = _stack_selectors()
    full = lambda a: pl.BlockSpec(a.shape, lambda i: (0,) * a.ndim)
    return pl.pallas_call(
        _tc_down_kernel,
        grid=(n_tok // TC_STEP,),
        in_specs=[pl.BlockSpec(memory_space=pl.ANY),
                  pl.BlockSpec((TC_STEP, xn.shape[1]), lambda i: (i, 0)),
                  pl.BlockSpec(memory_space=pl.ANY), full(diag), full(owner)],
        out_specs=pl.BlockSpec((TC_STEP, PEER_E), lambda i: (i, 0)),
        out_shape=jax.ShapeDtypeStruct((n_tok, PEER_E), F32),
        scratch_shapes=_tc_peer_scratch(tabp) + [pltpu.VMEM((TC_STEP, STACK_COLS), F32)],
        compiler_params=pltpu.CompilerParams(
            dimension_semantics=("arbitrary",), vmem_limit_bytes=VMEM_LIMIT),
        name="peer_down_dots_tc",
    )(offs, xn, tabp, diag, owner)


def _tc_up_kernel(idx_hbm, coef_ref, tab_hbm, spread_ref, diag_ref, o_ref,
                  tab_v, idx_s, sem, isem, ce_ref):
    _load_table_once(tab_hbm, tab_v, sem)
    c = coef_ref[...]
    c_hi = c.astype(BF16)
    c_lo = (c - c_hi.astype(F32)).astype(BF16)
    ce_ref[0] = jnp.dot(c_hi, spread_ref[...], preferred_element_type=F32)
    ce_ref[1] = jnp.dot(c_lo, spread_ref[...], preferred_element_type=F32)

    cols = STACK_COLS // TC_KCHUNKS

    def per_token(t, chunks):
        d = diag_ref[...]
        lhs = jnp.concatenate([(ce_ref[0, pl.ds(t, 1), :] * d).astype(BF16),
                               (ce_ref[1, pl.ds(t, 1), :] * d).astype(BF16)], axis=0)
        r = jnp.zeros((2 * ROW_HALVES, LANES), F32)
        for ch, rows in enumerate(chunks):
            r = r + jnp.dot(lhs[:, cols * ch:cols * (ch + 1)], rows, preferred_element_type=F32)
        out8 = r[0:ROW_HALVES] + r[ROW_HALVES:2 * ROW_HALVES]
        o_ref[pl.ds(t, 1), :] = jnp.concatenate(
            [out8[j:j + 1, :] for j in range(ROW_HALVES)], axis=1)

    _tc_token_pipeline(idx_hbm, idx_s, isem, tab_v, per_token)


def _peer_up_sum_tc(coef, offs, tabp, n_tok):
    d = ROW_HALVES * LANES
    diag, owner = _stack_selectors()
    spread = owner.T.astype(BF16)
    full = lambda a: pl.BlockSpec(a.shape, lambda i: (0,) * a.ndim)
    return pl.pallas_call(
        _tc_up_kernel,
        grid=(n_tok // TC_STEP,),
        in_specs=[pl.BlockSpec(memory_space=pl.ANY),
                  pl.BlockSpec((TC_STEP, PEER_E), lambda i: (i, 0)),
                  pl.BlockSpec(memory_space=pl.ANY), full(spread), full(diag)],
        out_specs=pl.BlockSpec((TC_STEP, d), lambda i: (i, 0)),
        out_shape=jax.ShapeDtypeStruct((n_tok, d), F32),
        scratch_shapes=_tc_peer_scratch(tabp) + [pltpu.VMEM((2, TC_STEP, STACK_COLS), F32)],
        compiler_params=pltpu.CompilerParams(
            dimension_semantics=("arbitrary",), vmem_limit_bytes=VMEM_LIMIT),
        name="peer_up_sum_tc",
    )(offs, coef, tabp, spread, diag)


def _coef_kernel(pre_ref, gate_ref, o_ref):
    x = pre_ref[...]
    gelu = 0.5 * x * (1.0 + lax.erf(x * (2.0 ** -0.5)))
    o_ref[...] = gate_ref[...] * gelu


def _peer_coef(pre, gates):
    t, e = pre.shape
    row = pl.BlockSpec((TM_FINAL, e), lambda i: (i, 0))
    return pl.pallas_call(
        _coef_kernel,
        grid=(t // TM_FINAL,),
        in_specs=[row, row],
        out_specs=row,
        out_shape=jax.ShapeDtypeStruct((t, e), F32),
        compiler_params=pltpu.CompilerParams(dimension_semantics=("arbitrary",)),
        name="peer_gate_gelu",
    )(pre, gates)


def _front_half(x2, row0, rows, bsz, seq, w_pad, layer, i):
    (norm1_g, ret_norm_g, m_conv_w, m_conv_b, m_wq, m_wk, m_b_i, m_b_f, m_norm_g, m_skip,
     w_out, norm2_g, peer_wq, keys) = layer
    d = x2.shape[1]
    proj = _rms_proj(x2, row0, rows, norm1_g[i].reshape(1, d), w_pad)
    y = _mixer(proj, bsz, seq, ret_norm_g[i], m_conv_w[i], m_conv_b[i], m_wq[i], m_wk[i],
               m_b_i[i], m_b_f[i], m_norm_g[i], m_skip[i])
    x1, xn, sub_t = _post(y, x2, row0, w_out, norm2_g[i].reshape(1, d), peer_wq, keys)
    return (x1, xn) + tuple(_peer_route(sub_t))


def kernel(x, norm1_g, w_in, ret_norm_g, m_conv_w, m_conv_b, m_wq, m_wk, m_b_i, m_b_f,
           m_norm_g, m_skip, w_out, norm2_g, peer_wq, peer_keys, peer_down, peer_up, final_g):
    b, s, d = x.shape
    t = b * s
    tp = t // FRONT_PARTS
    x2 = x.reshape(t, d)
    depth = norm1_g.shape[0]
    for i in range(depth):
        if i > 0:
            x2 = jnp.concatenate(x1s, axis=0) + jnp.concatenate([p_tc, p_sc], axis=0)
        wi = w_in[i]
        n_main = OFF_GI
        gate_pad = lambda w: jnp.pad(w, ((0, 0), (0, LANES - w.shape[1])))
        w_pad = jnp.concatenate(
            [wi[:, :n_main], gate_pad(wi[:, n_main:n_main + MLSTM_HEADS]),
             gate_pad(wi[:, n_main + MLSTM_HEADS:])], axis=1).astype(BF16)
        keys = peer_keys[i].reshape(2 * PEER_HEADS, PEER_NKEYS, PEER_HALF).astype(BF16)
        layer = (norm1_g, ret_norm_g, m_conv_w, m_conv_b, m_wq, m_wk, m_b_i, m_b_f, m_norm_g,
                 m_skip, w_out[i].astype(BF16), norm2_g, peer_wq[i].astype(BF16), keys)
        down_p = _pack_rows_bf16(peer_down[i])
        up_p = _pack_rows_bf16(peer_up[i])

        fronts = []
        pre = [None] * FRONT_PARTS
        for q in range(FRONT_PARTS):
            fr = _front_half(x2, q * tp, tp, b // FRONT_PARTS, s, w_pad, layer, i)
            fronts.append(fr)
            if q < SC_DOWN_PARTS:
                pre[q] = [_peer_down_dots(fr[1], fr[2], peer_down[i], 0)]
            elif q == SC_DOWN_PARTS and SC_DOWN_EXTRA:
                pre[q] = [None, _peer_down_dots(fr[1], fr[2], peer_down[i], tp - SC_DOWN_EXTRA)]
        for q in range(SC_DOWN_PARTS, FRONT_PARTS):
            n_tc_q = tp - SC_DOWN_EXTRA if q == SC_DOWN_PARTS else tp
            pre_tc = _peer_down_dots_tc(fronts[q][1], fronts[q][4], down_p, n_tc_q)
            if pre[q] is None:
                pre[q] = [pre_tc]
            else:
                pre[q][0] = pre_tc
        pre = [piece for part in pre for piece in part]
        x1s = [fr[0] for fr in fronts]
        expert = jnp.concatenate([fr[2] for fr in fronts], axis=0)
        gates = jnp.concatenate([fr[3] for fr in fronts], axis=0)
        offs = jnp.concatenate([fr[4] for fr in fronts], axis=0)
        coef = _peer_coef(jnp.concatenate(pre, axis=0), gates)
        n_tc = PEER_TC_UP_TOKENS
        p_sc = _peer_up_sum(coef, expert, peer_up[i], n_tc)
        p_tc = _peer_up_sum_tc(coef, offs, up_p, n_tc)
    out = _final(_row_ranges(x1s, [p_tc, p_sc]), final_g.reshape(1, d), t)
    return out.reshape(b, s, d)


def _row_ranges(xs, ps):
    def starts(arrs):
        out = [0]
        for a in arrs:
            out.append(out[-1] + a.shape[0])
        return out
    xb, pb = starts(xs), starts(ps)
    assert xb[-1] == pb[-1]
    cuts = sorted(set(xb) | set(pb))
    parts = []
    for lo, hi in zip(cuts[:-1], cuts[1:]):
        xi = max(k for k in range(len(xs)) if xb[k] <= lo)
        pi = max(k for k in range(len(ps)) if pb[k] <= lo)
        parts.append((xs[xi], lo - xb[xi], ps[pi], lo - pb[pi], hi - lo))
    return parts
```

```python
import functools

import numpy as np
import jax
import jax.numpy as jnp
from jax import lax
from jax.experimental import pallas as pl
from jax.experimental.pallas import tpu as pltpu
from jax.experimental.pallas import tpu_sc as plsc

F32 = jnp.float32
BF16 = jnp.bfloat16

D_MODEL = 1024
CHUNK = 64
RET_HEADS = 4
RET_DK = 128
MLSTM_HEADS = 4
MLSTM_D = 128
HEAD_W = 128
D_GROUP = 512
CONV_W = 4
ROPE_BASE = 10000.0
PEER_HEADS = 8
PEER_NKEYS = 128
PEER_TOPK = 16
PEER_HALF = 128
PEER_BLOCK = 128
EPS = 1e-6

OFF_RQ, OFF_RK, OFF_RV, OFF_RG = 0, 512, 1024, 1536
OFF_MX, OFF_MV, OFF_MO = 2048, 2560, 3072
OFF_GI, OFF_GF = 3584, 3712
D_PROJ_PAD = 3840

LANES = 128
CONV_TAIL = 8
VMEM_LIMIT = 56 * 1024 * 1024

TM_PROJ = 256
L_BLOCK = 256
TM_FINAL = 256


def _rms(x, g):
    ms = jnp.mean(x * x, axis=-1, keepdims=True)
    return x * lax.rsqrt(ms + EPS) * g


def _group_norm(h, g):
    mu = jnp.mean(h, axis=-1, keepdims=True)
    d = h - mu
    var = jnp.mean(d * d, axis=-1, keepdims=True)
    return d * lax.rsqrt(var + EPS) * g


def _silu(x):
    return x * (1.0 / (1.0 + jnp.exp(-x)))


def _sigmoid(x):
    return 1.0 / (1.0 + jnp.exp(-x))


def _dot(a, b):
    return jnp.dot(a.astype(BF16), b.astype(BF16), preferred_element_type=F32)


def _dot_tn(a, b):
    return lax.dot_general(a.astype(BF16), b.astype(BF16), (((0,), (0,)), ((), ())),
                           preferred_element_type=F32)


def _dot_nt(a, b):
    return lax.dot_general(a.astype(BF16), b.astype(BF16), (((1,), (1,)), ((), ())),
                           preferred_element_type=F32)


def _rms_proj_kernel(x_ref, g_ref, w_ref, o_ref):
    h = _rms(x_ref[...], g_ref[...])
    o_ref[...] = jnp.dot(h.astype(BF16), w_ref[...], preferred_element_type=F32)


def _rms_proj(x2, row0, rows, g, w):
    d = x2.shape[1]
    t = rows
    n = w.shape[1]
    b0 = row0 // TM_PROJ
    return pl.pallas_call(
        _rms_proj_kernel,
        grid=(t // TM_PROJ,),
        in_specs=[
            pl.BlockSpec((TM_PROJ, d), lambda i: (i + b0, 0)),
            pl.BlockSpec((1, d), lambda i: (0, 0)),
            pl.BlockSpec((d, n), lambda i: (0, 0)),
        ],
        out_specs=pl.BlockSpec((TM_PROJ, n), lambda i: (i, 0)),
        out_shape=jax.ShapeDtypeStruct((t, n), F32),
        compiler_params=pltpu.CompilerParams(
            dimension_semantics=("arbitrary",), vmem_limit_bytes=VMEM_LIMIT),
        name="rms_in_proj",
    )(x2, g, w)


def _mixer_kernel(chunk_decay,
                  proj_ref, cos_ref, sin_ref, intra_ref, qd_ref, kd_ref, tri_ref,
                  rng_ref, cw_ref, cb_ref, wq_ref, wk_ref, bi_ref, bf_ref,
                  mng_ref, skip_ref,
                  y_ref,
                  rstate_ref, cstate_ref, m_ref, tail_ref):
    j = pl.program_id(1)
    n_chunks = L_BLOCK // CHUNK

    @pl.when(j == 0)
    def _():
        rstate_ref[...] = jnp.zeros_like(rstate_ref)
        cstate_ref[...] = jnp.zeros_like(cstate_ref)
        m_ref[...] = jnp.zeros_like(m_ref)
        tail_ref[...] = jnp.zeros_like(tail_ref)

    cosv = cos_ref[...]
    sinv = sin_ref[...]
    k_scale = RET_DK ** -0.5

    for h in range(RET_HEADS):
        lo = h * HEAD_W
        q = proj_ref[:, OFF_RQ + lo:OFF_RQ + lo + HEAD_W]
        k = proj_ref[:, OFF_RK + lo:OFF_RK + lo + HEAD_W]
        q = q * cosv + pltpu.roll(q, HEAD_W // 2, 1) * sinv
        k = (k * cosv + pltpu.roll(k, HEAD_W // 2, 1) * sinv) * k_scale
        intra = intra_ref[h]
        qd = qd_ref[h]
        kd = kd_ref[h]
        g = rng_ref[:, lo:lo + HEAD_W]
        for c in range(n_chunks):
            r0 = c * CHUNK
            qc = q[r0:r0 + CHUNK]
            kc = k[r0:r0 + CHUNK]
            vc = proj_ref[r0:r0 + CHUNK, OFF_RV + lo:OFF_RV + lo + HEAD_W]
            gate = proj_ref[r0:r0 + CHUNK, OFF_RG + lo:OFF_RG + lo + HEAD_W]
            state = rstate_ref[h]
            scores = _dot_nt(qc, kc) * intra
            o = _dot(scores, vc) + _dot(qc, state) * qd
            rstate_ref[h] = chunk_decay[h] * state + _dot_tn(kc * kd, vc)
            y_ref[r0:r0 + CHUNK, lo:lo + HEAD_W] = _silu(gate) * _group_norm(o, g)

    mx = proj_ref[:, OFF_MX:OFF_MX + D_GROUP]
    xp = jnp.concatenate([tail_ref[...], mx], axis=0)
    base = CONV_TAIL - (CONV_W - 1)
    conv = cb_ref[...] + xp[base:base + L_BLOCK] * cw_ref[0:1, :]
    for t in range(1, CONV_W):
        conv = conv + xp[base + t:base + t + L_BLOCK] * cw_ref[t:t + 1, :]
    tail_ref[...] = mx[L_BLOCK - CONV_TAIL:L_BLOCK]
    cact = _silu(conv)

    gi = proj_ref[:, OFF_GI:OFF_GI + LANES] + bi_ref[...]
    gf = proj_ref[:, OFF_GF:OFF_GF + LANES] + bf_ref[...]
    logf = jnp.minimum(gf, 0.0) - jnp.log1p(jnp.exp(-jnp.abs(gf)))
    tri = tri_ref[...]
    lane = lax.broadcasted_iota(jnp.int32, (CHUNK, HEAD_W), 1)
    ones_col = (lane == 0).astype(F32)
    m_scale = MLSTM_D ** -0.5

    mq = []
    mk = []
    for h in range(MLSTM_HEADS):
        lo = h * HEAD_W
        ch = cact[:, lo:lo + HEAD_W]
        mq.append(_dot(ch, wq_ref[h]))
        mk.append(_dot(ch, wk_ref[h]) * m_scale)

    for c in range(n_chunks):
        r0 = c * CHUNK
        fcum = jnp.dot(tri, logf[r0:r0 + CHUNK], preferred_element_type=F32,
                       precision=lax.Precision.HIGHEST)
        a = fcum[CHUNK - 1:CHUNK]
        logw = a - fcum + gi[r0:r0 + CHUNK]
        m_in = jnp.max(logw, axis=0, keepdims=True)
        m_old = m_ref[...]
        m_new = jnp.maximum(a + m_old, m_in)
        decay = jnp.exp(a + m_old - m_new)
        w = jnp.exp(logw - m_new)
        em = jnp.exp(-m_new)
        m_ref[...] = m_new
        for h in range(MLSTM_HEADS):
            lo = h * HEAD_W
            qh = mq[h][r0:r0 + CHUNK]
            kh = mk[h][r0:r0 + CHUNK]
            vh = proj_ref[r0:r0 + CHUNK, OFF_MV + lo:OFF_MV + lo + HEAD_W]
            oh = proj_ref[r0:r0 + CHUNK, OFF_MO + lo:OFF_MO + lo + HEAD_W]
            ch = cact[r0:r0 + CHUNK, lo:lo + HEAD_W]
            v_ext = jnp.concatenate([vh, ones_col], axis=1)
            cmem = decay[:, h:h + 1] * cstate_ref[h] + _dot_tn(kh * w[:, h:h + 1], v_ext)
            cstate_ref[h] = cmem
            num = _dot(qh, cmem)
            den = jnp.maximum(jnp.abs(num[:, HEAD_W:HEAD_W + 1]), em[:, h:h + 1])
            hh = num[:, :HEAD_W] / den
            hm = _group_norm(hh, mng_ref[:, lo:lo + HEAD_W])
            ym = _sigmoid(oh) * (hm + skip_ref[:, lo:lo + HEAD_W] * ch)
            y_ref[r0:r0 + CHUNK, D_GROUP + lo:D_GROUP + lo + HEAD_W] = ym


def _retention_tables():
    h = np.arange(RET_HEADS, dtype=np.float64)
    log_g = np.log(1.0 - 2.0 ** (-5.0 - h))
    l = np.arange(CHUNK, dtype=np.float64)
    intra = np.exp(log_g[:, None, None] * np.abs(l[:, None] - l[None, :]))
    qd = np.exp(log_g[:, None] * (l + 1.0))[:, :, None] * np.ones((1, 1, HEAD_W))
    kd = np.exp(log_g[:, None] * (CHUNK - 1.0 - l))[:, :, None] * np.ones((1, 1, HEAD_W))
    cd = np.exp(log_g * CHUNK)
    return (jnp.asarray(intra, F32), jnp.asarray(qd, F32), jnp.asarray(kd, F32),
            tuple(float(v) for v in cd))


def _rotary_tables(s):
    half = HEAD_W // 2
    inv = ROPE_BASE ** (-np.arange(half, dtype=np.float64) / half)
    ang = np.arange(s, dtype=np.float64)[:, None] * inv[None, :]
    cos = np.cos(ang)
    sin = np.sin(ang)
    return (jnp.asarray(np.concatenate([cos, cos], axis=-1), F32),
            jnp.asarray(np.concatenate([-sin, sin], axis=-1), F32))


def _mixer(proj, b, s, ret_norm_g, m_conv_w, m_conv_b, m_wq, m_wk, m_b_i, m_b_f,
           m_norm_g, m_skip):
    intra, qd, kd, cd = _retention_tables()
    cos_t, sin_t = _rotary_tables(s)
    tri = jnp.asarray(np.tril(np.ones((CHUNK, CHUNK))), F32)
    nj = s // L_BLOCK
    pad = lambda v: jnp.pad(v.reshape(1, -1), ((0, 0), (0, LANES - v.size)))
    full = lambda shape: pl.BlockSpec(shape, lambda bi, j: (0,) * len(shape))
    return pl.pallas_call(
        functools.partial(_mixer_kernel, cd),
        grid=(b, nj),
        in_specs=[
            pl.BlockSpec((L_BLOCK, D_PROJ_PAD), lambda bi, j: (bi * nj + j, 0)),
            pl.BlockSpec((L_BLOCK, HEAD_W), lambda bi, j: (j, 0)),
            pl.BlockSpec((L_BLOCK, HEAD_W), lambda bi, j: (j, 0)),
            full((RET_HEADS, CHUNK, CHUNK)),
            full((RET_HEADS, CHUNK, HEAD_W)),
            full((RET_HEADS, CHUNK, HEAD_W)),
            full((CHUNK, CHUNK)),
            full((1, D_GROUP)),
            full((CONV_W, D_GROUP)),
            full((1, D_GROUP)),
            full((MLSTM_HEADS, MLSTM_D, MLSTM_D)),
            full((MLSTM_HEADS, MLSTM_D, MLSTM_D)),
            full((1, LANES)),
            full((1, LANES)),
            full((1, D_GROUP)),
            full((1, D_GROUP)),
        ],
        out_specs=pl.BlockSpec((L_BLOCK, 2 * D_GROUP), lambda bi, j: (bi * nj + j, 0)),
        out_shape=jax.ShapeDtypeStruct((b * s, 2 * D_GROUP), F32),
        scratch_shapes=[
            pltpu.VMEM((RET_HEADS, RET_DK, HEAD_W), F32),
            pltpu.VMEM((MLSTM_HEADS, MLSTM_D, 2 * HEAD_W), F32),
            pltpu.VMEM((1, LANES), F32),
            pltpu.VMEM((CONV_TAIL, D_GROUP), F32),
        ],
        compiler_params=pltpu.CompilerParams(
            dimension_semantics=("arbitrary", "arbitrary"), vmem_limit_bytes=VMEM_LIMIT),
        name="retention_mlstm_mixer",
    )(proj, cos_t, sin_t, intra, qd, kd, tri,
      ret_norm_g.reshape(1, D_GROUP), m_conv_w, m_conv_b.reshape(1, D_GROUP),
      m_wq.astype(BF16), m_wk.astype(BF16), pad(m_b_i), pad(m_b_f),
      m_norm_g.reshape(1, D_GROUP), m_skip.reshape(1, D_GROUP))


def _post_kernel(y_ref, x_ref, wout_ref, g2_ref, wq_ref, keys_ref,
                 x1_ref, xn_ref, sub_ref):
    x1 = x_ref[...] + jnp.dot(y_ref[...].astype(BF16), wout_ref[...],
                              preferred_element_type=F32)
    x1_ref[...] = x1
    xn = _rms(x1, g2_ref[...])
    xn_ref[...] = xn
    q = jnp.dot(xn.astype(BF16), wq_ref[...], preferred_element_type=F32)
    for i in range(2 * PEER_HEADS):
        lo = i * PEER_HALF
        sub_ref[lo:lo + PEER_NKEYS, :] = _dot_nt(keys_ref[i], q[:, lo:lo + PEER_HALF])


def _post(y, x2, row0, w_out, norm2_g, peer_wq, keys):
    t, d = y.shape
    nq = peer_wq.shape[1]
    b0 = row0 // TM_PROJ
    row = lambda w: pl.BlockSpec((TM_PROJ, w), lambda i: (i, 0))
    full = lambda shape: pl.BlockSpec(shape, lambda i: (0,) * len(shape))
    return pl.pallas_call(
        _post_kernel,
        grid=(t // TM_PROJ,),
        in_specs=[row(d), pl.BlockSpec((TM_PROJ, d), lambda i: (i + b0, 0)), full((d, d)),
                  full((1, d)), full((d, nq)), full(keys.shape)],
        out_specs=[row(d), row(d), pl.BlockSpec((nq, TM_PROJ), lambda i: (0, i))],
        out_shape=[jax.ShapeDtypeStruct((t, d), F32), jax.ShapeDtypeStruct((t, d), F32),
                   jax.ShapeDtypeStruct((nq, t), F32)],
        compiler_params=pltpu.CompilerParams(
            dimension_semantics=("arbitrary",), vmem_limit_bytes=VMEM_LIMIT),
        name="out_proj_peer_scores",
    )(y, x2, w_out, norm2_g, peer_wq, keys)


def _final_kernel(x1_ref, p_ref, g_ref, *rest):
    o_ref = rest[-1]
    o_ref[...] = _rms(x1_ref[...] + p_ref[...], g_ref[...])


def _final(parts, g, t):
    d = g.shape[1]
    spec = lambda r0: pl.BlockSpec((TM_FINAL, d), lambda i, b0=r0 // TM_FINAL: (i + b0, 0))
    out = None
    row0 = 0
    for x1, x_row0, p, p_row0, rows in parts:
        in_specs = [spec(x_row0), spec(p_row0), pl.BlockSpec((1, d), lambda i: (0, 0))]
        args = [x1, p, g]
        if out is not None:
            in_specs.append(pl.BlockSpec(memory_space=pl.ANY))
            args.append(out)
        out = pl.pallas_call(
            _final_kernel,
            grid=(rows // TM_FINAL,),
            in_specs=in_specs,
            out_specs=spec(row0),
            out_shape=jax.ShapeDtypeStruct((t, d), F32),
            input_output_aliases={} if len(args) == 3 else {3: 0},
            compiler_params=pltpu.CompilerParams(dimension_semantics=("arbitrary",)),
            name="final_rmsnorm",
        )(*args)
        row0 += rows
    assert row0 == t
    return out


TM_ROUTE = 256
CAND_ROWS = 56


def _candidate_tables():
    pairs = [(k1, k2) for k1 in range(PEER_TOPK) for k2 in range(PEER_TOPK)
             if (k1 + 1) * (k2 + 1) <= PEER_TOPK]
    assert len(pairs) <= CAND_ROWS
    rep = np.zeros((CAND_ROWS, 2 * PEER_TOPK), np.float32)
    til = np.zeros((CAND_ROWS, 2 * PEER_TOPK), np.float32)
    for pos, (k1, k2) in enumerate(pairs):
        rep[pos, k1] = 1.0
        til[pos, PEER_TOPK + k2] = 1.0
    return jnp.asarray(rep), jnp.asarray(til), len(pairs)


def _dot_exact(a, b):
    return jnp.dot(a, b, preferred_element_type=F32, precision=lax.Precision.HIGHEST)


def _route_kernel(n_cand, sub_ref, rep_ref, til_ref, seg_ref, exp_ref, gate_ref, offs_ref,
                  s_ref, i_ref, best_ref, eid_ref, m0_ref):
    tm = sub_ref.shape[1]
    neg = -jnp.inf
    key_id = lax.broadcasted_iota(jnp.int32, (PEER_NKEYS, tm), 0).astype(F32)
    cand_id = lax.broadcasted_iota(jnp.int32, (CAND_ROWS, tm), 0).astype(F32)
    rep = rep_ref[...]
    til = til_ref[...]
    for h in range(PEER_HEADS):
        for p in range(2):
            lo = (2 * h + p) * PEER_NKEYS
            x = sub_ref[lo:lo + PEER_NKEYS, :]
            for k in range(PEER_TOPK):
                m = jnp.max(x, axis=0, keepdims=True)
                idx = jnp.min(jnp.where(x == m, key_id, float(PEER_NKEYS)), axis=0, keepdims=True)
                x = jnp.where(key_id == idx, neg, x)
                r = p * PEER_TOPK + k
                s_ref[r:r + 1, :] = m
                i_ref[r:r + 1, :] = idx
        s = s_ref[...]
        iv = i_ref[...]
        cand = _dot_exact(rep, s) + _dot_exact(til, s)
        cand = jnp.where(cand_id < float(n_cand), cand, neg)
        eid = _dot(rep, iv) * float(PEER_NKEYS) + _dot(til, iv)
        for k in range(PEER_TOPK):
            m = jnp.max(cand, axis=0, keepdims=True)
            pos = jnp.min(jnp.where(cand == m, cand_id, float(CAND_ROWS)), axis=0, keepdims=True)
            sel = cand_id == pos
            e = jnp.max(jnp.where(sel, eid, -1.0), axis=0, keepdims=True)
            cand = jnp.where(sel, neg, cand)
            r = h * PEER_TOPK + k
            best_ref[r:r + 1, :] = m
            eid_ref[r:r + 1, :] = e
            if k == 0:
                m0_ref[r:r + PEER_TOPK, :] = jnp.broadcast_to(m, (PEER_TOPK, tm))
    pexp = jnp.exp(best_ref[...] - m0_ref[...])
    gates = pexp / _dot_exact(seg_ref[...], pexp)
    gate_ref[...] = gates.T
    eid = eid_ref[...].astype(jnp.int32)
    exp_ref[...] = eid.T
    for blk in range(tm // LANES):
        offs_ref[blk] = eid[:, blk * LANES:(blk + 1) * LANES] * ROW_WORDS


def _peer_route(sub_t):
    nq, t = sub_t.shape
    rep, til, n_cand = _candidate_tables()
    e = PEER_HEADS * PEER_TOPK
    seg = jnp.asarray(np.kron(np.eye(PEER_HEADS), np.ones((PEER_TOPK, PEER_TOPK))), F32)
    full = lambda a: pl.BlockSpec(a.shape, lambda i: (0,) * a.ndim)
    out = pl.BlockSpec((TM_ROUTE, e), lambda i: (i, 0))
    return pl.pallas_call(
        functools.partial(_route_kernel, n_cand),
        grid=(t // TM_ROUTE,),
        in_specs=[pl.BlockSpec((nq, TM_ROUTE), lambda i: (0, i)), full(rep), full(til), full(seg)],
        out_specs=[out, out, pl.BlockSpec((TM_ROUTE // LANES, e, LANES), lambda i: (i, 0, 0))],
        out_shape=[jax.ShapeDtypeStruct((t, e), jnp.int32), jax.ShapeDtypeStruct((t, e), F32),
                   jax.ShapeDtypeStruct((t // LANES, e, LANES), jnp.int32)],
        scratch_shapes=[
            pltpu.VMEM((2 * PEER_TOPK, TM_ROUTE), F32),
            pltpu.VMEM((2 * PEER_TOPK, TM_ROUTE), F32),
            pltpu.VMEM((e, TM_ROUTE), F32),
            pltpu.VMEM((e, TM_ROUTE), F32),
            pltpu.VMEM((e, TM_ROUTE), F32),
        ],
        compiler_params=pltpu.CompilerParams(dimension_semantics=("arbitrary",)),
        name="peer_topk_route",
    )(sub_t, rep, til, seg)


SC_CORES = 2
SC_SUBCORES = 16
SC_LANES = 16
SC_WORKERS = SC_CORES * SC_SUBCORES
PEER_E = PEER_HEADS * PEER_TOPK
SC_ROWS = 32
SC_GATHERS = PEER_E // SC_ROWS
SC_TOK_BLOCK = 8
SC_QUARTER = 256


def _sc_mesh():
    return plsc.VectorSubcoreMesh(core_axis_name="c", subcore_axis_name="s")


def _sc_pipeline(idx_v, tab_hbm, rows_v, sems, compute):
    def gather(step, buf):
        return pltpu.make_async_copy(tab_hbm.at[idx_v.at[step]], rows_v.at[buf], sems.at[buf])

    gather(0, 0).start()

    @pl.loop(0, SC_TOK_BLOCK)
    def _(i):
        for c in range(SC_GATHERS):
            buf = c % 2
            step = i * SC_GATHERS + c
            gather(step, buf).wait()
            if c + 1 < SC_GATHERS:
                gather(step + 1, 1 - buf).start()
            else:
                @pl.when(i + 1 < SC_TOK_BLOCK)
                def _():
                    gather(step + 1, 1 - buf).start()
            compute(i, c, step, buf)


def _peer_down_dots(xn, idx, down, tok_start):
    t_all, d = xn.shape
    t = t_all - tok_start
    tok_w = t // SC_WORKERS
    nblk = tok_w // SC_TOK_BLOCK
    assert nblk * SC_TOK_BLOCK * SC_WORKERS == t
    lanes = SC_LANES

    @functools.partial(
        pl.kernel, mesh=_sc_mesh(),
        out_type=jax.ShapeDtypeStruct((t, PEER_E), F32),
        scratch_types=[
            pltpu.VMEM((SC_TOK_BLOCK * SC_GATHERS, SC_ROWS), jnp.int32),
            pltpu.VMEM((SC_TOK_BLOCK, d), F32),
            pltpu.VMEM((2, SC_ROWS, d), F32),
            pltpu.VMEM((SC_ROWS, lanes), F32),
            pltpu.VMEM((SC_TOK_BLOCK, PEER_E), F32),
            pltpu.SemaphoreType.DMA((2,)),
        ],
        compiler_params=pltpu.CompilerParams(needs_layout_passes=False),
        name="peer_down_dots",
    )
    def k(x_hbm, idx_hbm, tab_hbm, out_hbm, idx_v, x_v, rows_v, acc_v, out_v, sems):
        wid = lax.axis_index("s") * SC_CORES + lax.axis_index("c")
        lane = lax.iota(jnp.int32, lanes)

        @pl.loop(0, nblk)
        def _(blk):
            out0 = wid * tok_w + blk * SC_TOK_BLOCK
            tok0 = tok_start + out0
            pltpu.sync_copy(idx_hbm.at[pl.ds(tok0 * SC_GATHERS, SC_TOK_BLOCK * SC_GATHERS)], idx_v)
            pltpu.sync_copy(x_hbm.at[pl.ds(tok0, SC_TOK_BLOCK)], x_v)

            def compute(i, c, step, buf):
                def body(dc, accs):
                    xv = x_v[i, pl.ds(dc * lanes, lanes)]
                    return tuple(a + rows_v[buf, r, pl.ds(dc * lanes, lanes)] * xv
                                 for r, a in enumerate(accs))
                accs = lax.fori_loop(0, d // lanes, body,
                                     tuple(jnp.zeros((lanes,), F32) for _ in range(SC_ROWS)))
                for r in range(SC_ROWS):
                    acc_v[r, :] = accs[r]
                for g in range(SC_ROWS // lanes):
                    tot = jnp.zeros((lanes,), F32)
                    for l in range(lanes):
                        tot = tot + plsc.load_gather(
                            acc_v, [lane + g * lanes, jnp.full((lanes,), l, jnp.int32)])
                    out_v[i, pl.ds(c * SC_ROWS + g * lanes, lanes)] = tot

            _sc_pipeline(idx_v, tab_hbm, rows_v, sems, compute)
            pltpu.sync_copy(out_v, out_hbm.at[pl.ds(out0, SC_TOK_BLOCK)])

    return k(xn, idx.reshape(t_all * SC_GATHERS, SC_ROWS), down)


def _peer_up_sum(coef, idx, up, tok_start):
    t_all = coef.shape[0]
    t = t_all - tok_start
    d = up.shape[1]
    tok_w = t // SC_WORKERS
    nblk = tok_w // SC_TOK_BLOCK
    assert nblk * SC_TOK_BLOCK * SC_WORKERS == t
    lanes = SC_LANES
    nj = SC_QUARTER // lanes

    @functools.partial(
        pl.kernel, mesh=_sc_mesh(),
        out_type=jax.ShapeDtypeStruct((t, d), F32),
        scratch_types=[
            pltpu.VMEM((SC_TOK_BLOCK * SC_GATHERS, SC_ROWS), jnp.int32),
            pltpu.VMEM((SC_TOK_BLOCK * PEER_E,), F32),
            pltpu.VMEM((2, SC_ROWS, d), F32),
            pltpu.VMEM((SC_TOK_BLOCK, d), F32),
            pltpu.SemaphoreType.DMA((2,)),
        ],
        compiler_params=pltpu.CompilerParams(needs_layout_passes=False),
        name="peer_up_sum",
    )
    def k(coef_hbm, idx_hbm, tab_hbm, out_hbm, idx_v, coef_v, rows_v, out_v, sems):
        wid = lax.axis_index("s") * SC_CORES + lax.axis_index("c")

        @pl.loop(0, nblk)
        def _(blk):
            out0 = wid * tok_w + blk * SC_TOK_BLOCK
            tok0 = tok_start + out0
            pltpu.sync_copy(idx_hbm.at[pl.ds(tok0 * SC_GATHERS, SC_TOK_BLOCK * SC_GATHERS)], idx_v)
            pltpu.sync_copy(coef_hbm.at[pl.ds(tok0 * PEER_E, SC_TOK_BLOCK * PEER_E)], coef_v)

            def compute(i, c, step, buf):
                for q in range(d // SC_QUARTER):
                    col = lambda j: pl.ds(q * SC_QUARTER + j * lanes, lanes)
                    if c == 0:
                        init = tuple(jnp.zeros((lanes,), F32) for _ in range(nj))
                    else:
                        init = tuple(out_v[i, col(j)] for j in range(nj))

                    def body(r, accs):
                        cf = plsc.load_gather(
                            coef_v, [jnp.full((lanes,), step * SC_ROWS + r, jnp.int32)])
                        return tuple(a + rows_v[buf, r, col(j)] * cf for j, a in enumerate(accs))
                    accs = lax.fori_loop(0, SC_ROWS, body, init)
                    for j in range(nj):
                        out_v[i, col(j)] = accs[j]

            _sc_pipeline(idx_v, tab_hbm, rows_v, sems, compute)
            pltpu.sync_copy(out_v, out_hbm.at[pl.ds(out0, SC_TOK_BLOCK)])

    return k(coef.reshape(t_all * PEER_E), idx.reshape(t_all * SC_GATHERS, SC_ROWS), up)


TC_SUB = 128
TC_SUBS = 2
TC_STEP = TC_SUB * TC_SUBS
TC_TRIP = 32
TC_KCHUNKS = 4
ROW_WORDS = 4
ROW_HALVES = 2 * ROW_WORDS
STACK_COLS = PEER_E * ROW_HALVES
FRONT_PARTS = 4
SC_DOWN_PARTS = 1
SC_DOWN_EXTRA = 7424
PEER_TC_UP_TOKENS = 25088


def _pack_rows_bf16(tab):
    n, d = tab.shape
    return pl.pallas_call(
        _pack_kernel,
        grid=(n // PACK_ROWS,),
        in_specs=[pl.BlockSpec((PACK_ROWS, d), lambda i: (i, 0))],
        out_specs=pl.BlockSpec((PACK_ROWS * ROW_WORDS, LANES), lambda i: (i, 0)),
        out_shape=jax.ShapeDtypeStruct((n * ROW_WORDS, LANES), jnp.uint32),
        compiler_params=pltpu.CompilerParams(dimension_semantics=("arbitrary",)),
        name="pack_rows_bf16",
    )(tab)


PACK_ROWS = 512


def _pack_kernel(x_ref, o_ref):
    u = pltpu.bitcast(x_ref[...], jnp.uint32)
    r = (u + jnp.uint32(0x7FFF) + ((u >> 16) & jnp.uint32(1))) >> 16
    for s in range(ROW_WORDS):
        lo = r[:, 2 * LANES * s:2 * LANES * s + LANES]
        hi = r[:, 2 * LANES * s + LANES:2 * LANES * (s + 1)]
        o_ref[pl.ds(s, PACK_ROWS, stride=ROW_WORDS), :] = lo | (hi << 16)


def _load_table_once(tab_hbm, tab_v, sem):
    @pl.when(pl.program_id(0) == 0)
    def _():
        cp = pltpu.make_async_copy(tab_hbm, tab_v, sem)
        cp.start()
        cp.wait()


def _tc_token_pipeline(idx_hbm, idx_s, isem, tab_v, per_token):
    step = pl.program_id(0)

    def idx_copy(block, slot):
        return pltpu.make_async_copy(idx_hbm.at[block], idx_s[slot], isem.at[slot])

    @pl.when(step == 0)
    def _():
        idx_copy(0, 0).start()

    for sub in range(TC_SUBS):
        block = step * TC_SUBS + sub
        idx_copy(block, sub).wait()
        if sub + 1 < TC_SUBS:
            idx_copy(block + 1, sub + 1).start()
        else:
            @pl.when(step + 1 < pl.num_programs(0))
            def _():
                idx_copy(block + 1, 0).start()
        offs = idx_s[sub]

        def chunks(i, offs=offs):
            per = PEER_E // TC_KCHUNKS
            for ch in range(TC_KCHUNKS):
                pieces = [tab_v[pl.ds(pl.multiple_of(offs.at[k][i], ROW_WORDS), ROW_WORDS), :]
                          for k in range(ch * per, (ch + 1) * per)]
                yield pltpu.bitcast(jnp.concatenate(pieces, axis=0), BF16)

        def trip(ip, carry, sub=sub, chunks=chunks):
            for u in range(TC_TRIP):
                i = ip * TC_TRIP + u
                per_token(sub * TC_SUB + i, chunks(i))
            return carry
        lax.fori_loop(0, TC_SUB // TC_TRIP, trip, 0)


def _tc_peer_scratch(tabp):
    return [pltpu.VMEM(tabp.shape, jnp.uint32),
            [pltpu.SMEM((PEER_E, TC_SUB), jnp.int32) for _ in range(TC_SUBS)],
            pltpu.SemaphoreType.DMA,
            pltpu.SemaphoreType.DMA((TC_SUBS,))]


def _stack_selectors():
    col = np.arange(STACK_COLS)
    diag = (col[None, :] % ROW_HALVES == np.arange(ROW_HALVES)[:, None]).astype(np.float32)
    owner = (col[:, None] // ROW_HALVES == np.arange(PEER_E)[None, :]).astype(np.float32)
    return jnp.asarray(diag), jnp.asarray(owner)


def _tc_down_kernel(idx_hbm, x_ref, tab_hbm, diag_ref, fold_ref, o_ref,
                    tab_v, idx_s, sem, isem, z_ref):
    _load_table_once(tab_hbm, tab_v, sem)
    cols = STACK_COLS // TC_KCHUNKS

    def per_token(t, chunks):
        xr = x_ref[pl.ds(t, 1), :]
        x8 = jnp.concatenate([xr[:, LANES * j:LANES * (j + 1)] for j in range(ROW_HALVES)],
                             axis=0).astype(BF16)
        parts = []
        for ch, rows in enumerate(chunks):
            a = lax.dot_general(x8, rows, (((1,), (1,)), ((), ())), preferred_element_type=F32)
            parts.append(jnp.sum(a * diag_ref[:, cols * ch:cols * (ch + 1)], axis=0, keepdims=True))
        z_ref[pl.ds(t, 1), :] = jnp.concatenate(parts, axis=1)

    _tc_token_pipeline(idx_hbm, idx_s, isem, tab_v, per_token)
    o_ref[...] = _dot_exact(z_ref[...], fold_ref[...])


def _peer_down_dots_tc(xn, offs, tabp, n_tok):
    diag, owner = _stack_selectors()
    full = lambda a: pl.BlockSpec(a.shape, lambda i: (0,) * a.ndim)
    return pl.pallas_call(
        _tc_down_kernel,
        grid=(n_tok // TC_STEP,),
        in_specs=[pl.BlockSpec(memory_space=pl.ANY),
                  pl.BlockSpec((TC_STEP, xn.shape[1]), lambda i: (i, 0)),
                  pl.BlockSpec(memory_space=pl.ANY), full(diag), full(owner)],
        out_specs=pl.BlockSpec((TC_STEP, PEER_E), lambda i: (i, 0)),
        out_shape=jax.ShapeDtypeStruct((n_tok, PEER_E), F32),
        scratch_shapes=_tc_peer_scratch(tabp) + [pltpu.VMEM((TC_STEP, STACK_COLS), F32)],
        compiler_params=pltpu.CompilerParams(
            dimension_semantics=("arbitrary",), vmem_limit_bytes=VMEM_LIMIT),
        name="peer_down_dots_tc",
    )(offs, xn, tabp, diag, owner)


def _tc_up_kernel(idx_hbm, coef_ref, tab_hbm, spread_ref, diag_ref, o_ref,
                  tab_v, idx_s, sem, isem, ce_ref):
    _load_table_once(tab_hbm, tab_v, sem)
    c = coef_ref[...]
    c_hi = c.astype(BF16)
    c_lo = (c - c_hi.astype(F32)).astype(BF16)
    ce_ref[0] = jnp.dot(c_hi, spread_ref[...], preferred_element_type=F32)
    ce_ref[1] = jnp.dot(c_lo, spread_ref[...], preferred_element_type=F32)

    cols = STACK_COLS // TC_KCHUNKS

    def per_token(t, chunks):
        d = diag_ref[...]
        lhs = jnp.concatenate([(ce_ref[0, pl.ds(t, 1), :] * d).astype(BF16),
                               (ce_ref[1, pl.ds(t, 1), :] * d).astype(BF16)], axis=0)
        r = jnp.zeros((2 * ROW_HALVES, LANES), F32)
        for ch, rows in enumerate(chunks):
            r = r + jnp.dot(lhs[:, cols * ch:cols * (ch + 1)], rows, preferred_element_type=F32)
        out8 = r[0:ROW_HALVES] + r[ROW_HALVES:2 * ROW_HALVES]
        o_ref[pl.ds(t, 1), :] = jnp.concatenate(
            [out8[j:j + 1, :] for j in range(ROW_HALVES)], axis=1)

    _tc_token_pipeline(idx_hbm, idx_s, isem, tab_v, per_token)


def _peer_up_sum_tc(coef, offs, tabp, n_tok):
    d = ROW_HALVES * LANES
    diag, owner = _stack_selectors()
    spread = owner.T.astype(BF16)
    full = lambda a: pl.BlockSpec(a.shape, lambda i: (0,) * a.ndim)
    return pl.pallas_call(
        _tc_up_kernel,
        grid=(n_tok // TC_STEP,),
        in_specs=[pl.BlockSpec(memory_space=pl.ANY),
                  pl.BlockSpec((TC_STEP, PEER_E), lambda i: (i, 0)),
                  pl.BlockSpec(memory_space=pl.ANY), full(spread), full(diag)],
        out_specs=pl.BlockSpec((TC_STEP, d), lambda i: (i, 0)),
        out_shape=jax.ShapeDtypeStruct((n_tok, d), F32),
        scratch_shapes=_tc_peer_scratch(tabp) + [pltpu.VMEM((2, TC_STEP, STACK_COLS), F32)],
        compiler_params=pltpu.CompilerParams(
            dimension_semantics=("arbitrary",), vmem_limit_bytes=VMEM_LIMIT),
        name="peer_up_sum_tc",
    )(offs, coef, tabp, spread, diag)


def _coef_kernel(pre_ref, gate_ref, o_ref):
    x = pre_ref[...]
    gelu = 0.5 * x * (1.0 + lax.erf(x * (2.0 ** -0.5)))
    o_ref[...] = gate_ref[...] * gelu


def _peer_coef(pre, gates):
    t, e = pre.shape
    row = pl.BlockSpec((TM_FINAL, e), lambda i: (i, 0))
    return pl.pallas_call(
        _coef_kernel,
        grid=(t // TM_FINAL,),
        in_specs=[row, row],
        out_specs=row,
        out_shape=jax.ShapeDtypeStruct((t, e), F32),
        compiler_params=pltpu.CompilerParams(dimension_semantics=("arbitrary",)),
        name="peer_gate_gelu",
    )(pre, gates)


def _front_half(x2, row0, rows, bsz, seq, w_pad, layer, i):
    (norm1_g, ret_norm_g, m_conv_w, m_conv_b, m_wq, m_wk, m_b_i, m_b_f, m_norm_g, m_skip,
     w_out, norm2_g, peer_wq, keys) = layer
    d = x2.shape[1]
    proj = _rms_proj(x2, row0, rows, norm1_g[i].reshape(1, d), w_pad)
    y = _mixer(proj, bsz, seq, ret_norm_g[i], m_conv_w[i], m_conv_b[i], m_wq[i], m_wk[i],
               m_b_i[i], m_b_f[i], m_norm_g[i], m_skip[i])
    x1, xn, sub_t = _post(y, x2, row0, w_out, norm2_g[i].reshape(1, d), peer_wq, keys)
    return (x1, xn) + tuple(_peer_route(sub_t))


def kernel(x, norm1_g, w_in, ret_norm_g, m_conv_w, m_conv_b, m_wq, m_wk, m_b_i, m_b_f,
           m_norm_g, m_skip, w_out, norm2_g, peer_wq, peer_keys, peer_down, peer_up, final_g):
    b, s, d = x.shape
    t = b * s
    tp = t // FRONT_PARTS
    x2 = x.reshape(t, d)
    depth = norm1_g.shape[0]
    for i in range(depth):
        if i > 0:
            x2 = jnp.concatenate(x1s, axis=0) + jnp.concatenate([p_tc, p_sc], axis=0)
        wi = w_in[i]
        n_main = OFF_GI
        gate_pad = lambda w: jnp.pad(w, ((0, 0), (0, LANES - w.shape[1])))
        w_pad = jnp.concatenate(
            [wi[:, :n_main], gate_pad(wi[:, n_main:n_main + MLSTM_HEADS]),
             gate_pad(wi[:, n_main + MLSTM_HEADS:])], axis=1).astype(BF16)
        keys = peer_keys[i].reshape(2 * PEER_HEADS, PEER_NKEYS, PEER_HALF).astype(BF16)
        layer = (norm1_g, ret_norm_g, m_conv_w, m_conv_b, m_wq, m_wk, m_b_i, m_b_f, m_norm_g,
                 m_skip, w_out[i].astype(BF16), norm2_g, peer_wq[i].astype(BF16), keys)
        down_p = _pack_rows_bf16(peer_down[i])
        up_p = _pack_rows_bf16(peer_up[i])

        fronts = []
        pre = [None] * FRONT_PARTS
        for q in range(FRONT_PARTS):
            fr = _front_half(x2, q * tp, tp, b // FRONT_PARTS, s, w_pad, layer, i)
            fronts.append(fr)
            if q < SC_DOWN_PARTS:
                pre[q] = [_peer_down_dots(fr[1], fr[2], peer_down[i], 0)]
            elif q == SC_DOWN_PARTS and SC_DOWN_EXTRA:
                pre[q] = [None, _peer_down_dots(fr[1], fr[2], peer_down[i], tp - SC_DOWN_EXTRA)]
        for q in range(SC_DOWN_PARTS, FRONT_PARTS):
            n_tc_q = tp - SC_DOWN_EXTRA if q == SC_DOWN_PARTS else tp
            pre_tc = _peer_down_dots_tc(fronts[q][1], fronts[q][4], down_p, n_tc_q)
            if pre[q] is None:
                pre[q] = [pre_tc]
            else:
                pre[q][0] = pre_tc
        pre = [piece for part in pre for piece in part]
        x1s = [fr[0] for fr in fronts]
        expert = jnp.concatenate([fr[2] for fr in fronts], axis=0)
        gates = jnp.concatenate([fr[3] for fr in fronts], axis=0)
        offs = jnp.concatenate([fr[4] for fr in fronts], axis=0)
        coef = _peer_coef(jnp.concatenate(pre, axis=0), gates)
        n_tc = PEER_TC_UP_TOKENS
        p_sc = _peer_up_sum(coef, expert, peer_up[i], n_tc)
        p_tc = _peer_up_sum_tc(coef, offs, up_p, n_tc)
    out = _final(_row_ranges(x1s, [p_tc, p_sc]), final_g.reshape(1, d), t)
    return out.reshape(b, s, d)


def _row_ranges(xs, ps):
    def starts(arrs):
        out = [0]
        for a in arrs:
            out.append(out[-1] + a.shape[0])
        return out
    xb, pb = starts(xs), starts(ps)
    assert xb[-1] == pb[-1]
    cuts = sorted(set(xb) | set(pb))
    parts = []
    for lo, hi in zip(cuts[:-1], cuts[1:]):
        xi = max(k for k in range(len(xs)) if xb[k] <= lo)
        pi = max(k for k in range(len(ps)) if pb[k] <= lo)
        parts.append((xs[xi], lo - xb[xi], ps[pi], lo - pb[pi], hi - lo))
    return parts
```

```python
import functools

import numpy as np
import jax
import jax.numpy as jnp
from jax import lax
from jax.experimental import pallas as pl
from jax.experimental.pallas import tpu as pltpu
from jax.experimental.pallas import tpu_sc as plsc

F32 = jnp.float32
BF16 = jnp.bfloat16

D_MODEL = 1024
CHUNK = 64
RET_HEADS = 4
RET_DK = 128
MLSTM_HEADS = 4
MLSTM_D = 128
HEAD_W = 128
D_GROUP = 512
CONV_W = 4
ROPE_BASE = 10000.0
PEER_HEADS = 8
PEER_NKEYS = 128
PEER_TOPK = 16
PEER_HALF = 128
PEER_BLOCK = 128
EPS = 1e-6

OFF_RQ, OFF_RK, OFF_RV, OFF_RG = 0, 512, 1024, 1536
OFF_MX, OFF_MV, OFF_MO = 2048, 2560, 3072
OFF_GI, OFF_GF = 3584, 3712
D_PROJ_PAD = 3840

LANES = 128
CONV_TAIL = 8
VMEM_LIMIT = 56 * 1024 * 1024

TM_PROJ = 256
L_BLOCK = 256
TM_FINAL = 256


def _rms(x, g):
    ms = jnp.mean(x * x, axis=-1, keepdims=True)
    return x * lax.rsqrt(ms + EPS) * g


def _group_norm(h, g):
    mu = jnp.mean(h, axis=-1, keepdims=True)
    d = h - mu
    var = jnp.mean(d * d, axis=-1, keepdims=True)
    return d * lax.rsqrt(var + EPS) * g


def _silu(x):
    return x * (1.0 / (1.0 + jnp.exp(-x)))


def _sigmoid(x):
    return 1.0 / (1.0 + jnp.exp(-x))


def _dot(a, b):
    return jnp.dot(a.astype(BF16), b.astype(BF16), preferred_element_type=F32)


def _dot_tn(a, b):
    return lax.dot_general(a.astype(BF16), b.astype(BF16), (((0,), (0,)), ((), ())),
                           preferred_element_type=F32)


def _dot_nt(a, b):
    return lax.dot_general(a.astype(BF16), b.astype(BF16), (((1,), (1,)), ((), ())),
                           preferred_element_type=F32)


def _rms_proj_kernel(x_ref, g_ref, w_ref, o_ref):
    h = _rms(x_ref[...], g_ref[...])
    o_ref[...] = jnp.dot(h.astype(BF16), w_ref[...], preferred_element_type=F32)


def _rms_proj(x2, row0, rows, g, w):
    d = x2.shape[1]
    t = rows
    n = w.shape[1]
    b0 = row0 // TM_PROJ
    return pl.pallas_call(
        _rms_proj_kernel,
        grid=(t // TM_PROJ,),
        in_specs=[
            pl.BlockSpec((TM_PROJ, d), lambda i: (i + b0, 0)),
            pl.BlockSpec((1, d), lambda i: (0, 0)),
            pl.BlockSpec((d, n), lambda i: (0, 0)),
        ],
        out_specs=pl.BlockSpec((TM_PROJ, n), lambda i: (i, 0)),
        out_shape=jax.ShapeDtypeStruct((t, n), F32),
        compiler_params=pltpu.CompilerParams(
            dimension_semantics=("arbitrary",), vmem_limit_bytes=VMEM_LIMIT),
        name="rms_in_proj",
    )(x2, g, w)


def _mixer_kernel(chunk_decay,
                  proj_ref, cos_ref, sin_ref, intra_ref, qd_ref, kd_ref, tri_ref,
                  rng_ref, cw_ref, cb_ref, wq_ref, wk_ref, bi_ref, bf_ref,
                  mng_ref, skip_ref,
                  y_ref,
                  rstate_ref, cstate_ref, m_ref, tail_ref):
    j = pl.program_id(1)
    n_chunks = L_BLOCK // CHUNK

    @pl.when(j == 0)
    def _():
        rstate_ref[...] = jnp.zeros_like(rstate_ref)
        cstate_ref[...] = jnp.zeros_like(cstate_ref)
        m_ref[...] = jnp.zeros_like(m_ref)
        tail_ref[...] = jnp.zeros_like(tail_ref)

    cosv = cos_ref[...]
    sinv = sin_ref[...]
    k_scale = RET_DK ** -0.5

    for h in range(RET_HEADS):
        lo = h * HEAD_W
        q = proj_ref[:, OFF_RQ + lo:OFF_RQ + lo + HEAD_W]
        k = proj_ref[:, OFF_RK + lo:OFF_RK + lo + HEAD_W]
        q = q * cosv + pltpu.roll(q, HEAD_W // 2, 1) * sinv
        k = (k * cosv + pltpu.roll(k, HEAD_W // 2, 1) * sinv) * k_scale
        intra = intra_ref[h]
        qd = qd_ref[h]
        kd = kd_ref[h]
        g = rng_ref[:, lo:lo + HEAD_W]
        for c in range(n_chunks):
            r0 = c * CHUNK
            qc = q[r0:r0 + CHUNK]
            kc = k[r0:r0 + CHUNK]
            vc = proj_ref[r0:r0 + CHUNK, OFF_RV + lo:OFF_RV + lo + HEAD_W]
            gate = proj_ref[r0:r0 + CHUNK, OFF_RG + lo:OFF_RG + lo + HEAD_W]
            state = rstate_ref[h]
            scores = _dot_nt(qc, kc) * intra
            o = _dot(scores, vc) + _dot(qc, state) * qd
            rstate_ref[h] = chunk_decay[h] * state + _dot_tn(kc * kd, vc)
            y_ref[r0:r0 + CHUNK, lo:lo + HEAD_W] = _silu(gate) * _group_norm(o, g)

    mx = proj_ref[:, OFF_MX:OFF_MX + D_GROUP]
    xp = jnp.concatenate([tail_ref[...], mx], axis=0)
    base = CONV_TAIL - (CONV_W - 1)
    conv = cb_ref[...] + xp[base:base + L_BLOCK] * cw_ref[0:1, :]
    for t in range(1, CONV_W):
        conv = conv + xp[base + t:base + t + L_BLOCK] * cw_ref[t:t + 1, :]
    tail_ref[...] = mx[L_BLOCK - CONV_TAIL:L_BLOCK]
    cact = _silu(conv)

    gi = proj_ref[:, OFF_GI:OFF_GI + LANES] + bi_ref[...]
    gf = proj_ref[:, OFF_GF:OFF_GF + LANES] + bf_ref[...]
    logf = jnp.minimum(gf, 0.0) - jnp.log1p(jnp.exp(-jnp.abs(gf)))
    tri = tri_ref[...]
    lane = lax.broadcasted_iota(jnp.int32, (CHUNK, HEAD_W), 1)
    ones_col = (lane == 0).astype(F32)
    m_scale = MLSTM_D ** -0.5

    mq = []
    mk = []
    for h in range(MLSTM_HEADS):
        lo = h * HEAD_W
        ch = cact[:, lo:lo + HEAD_W]
        mq.append(_dot(ch, wq_ref[h]))
        mk.append(_dot(ch, wk_ref[h]) * m_scale)

    for c in range(n_chunks):
        r0 = c * CHUNK
        fcum = jnp.dot(tri, logf[r0:r0 + CHUNK], preferred_element_type=F32,
                       precision=lax.Precision.HIGHEST)
        a = fcum[CHUNK - 1:CHUNK]
        logw = a - fcum + gi[r0:r0 + CHUNK]
        m_in = jnp.max(logw, axis=0, keepdims=True)
        m_old = m_ref[...]
        m_new = jnp.maximum(a + m_old, m_in)
        decay = jnp.exp(a + m_old - m_new)
        w = jnp.exp(logw - m_new)
        em = jnp.exp(-m_new)
        m_ref[...] = m_new
        for h in range(MLSTM_HEADS):
            lo = h * HEAD_W
            qh = mq[h][r0:r0 + CHUNK]
            kh = mk[h][r0:r0 + CHUNK]
            vh = proj_ref[r0:r0 + CHUNK, OFF_MV + lo:OFF_MV + lo + HEAD_W]
            oh = proj_ref[r0:r0 + CHUNK, OFF_MO + lo:OFF_MO + lo + HEAD_W]
            ch = cact[r0:r0 + CHUNK, lo:lo + HEAD_W]
            v_ext = jnp.concatenate([vh, ones_col], axis=1)
            cmem = decay[:, h:h + 1] * cstate_ref[h] + _dot_tn(kh * w[:, h:h + 1], v_ext)
            cstate_ref[h] = cmem
            num = _dot(qh, cmem)
            den = jnp.maximum(jnp.abs(num[:, HEAD_W:HEAD_W + 1]), em[:, h:h + 1])
            hh = num[:, :HEAD_W] / den
            hm = _group_norm(hh, mng_ref[:, lo:lo + HEAD_W])
            ym = _sigmoid(oh) * (hm + skip_ref[:, lo:lo + HEAD_W] * ch)
            y_ref[r0:r0 + CHUNK, D_GROUP + lo:D_GROUP + lo + HEAD_W] = ym


def _retention_tables():
    h = np.arange(RET_HEADS, dtype=np.float64)
    log_g = np.log(1.0 - 2.0 ** (-5.0 - h))
    l = np.arange(CHUNK, dtype=np.float64)
    intra = np.exp(log_g[:, None, None] * np.abs(l[:, None] - l[None, :]))
    qd = np.exp(log_g[:, None] * (l + 1.0))[:, :, None] * np.ones((1, 1, HEAD_W))
    kd = np.exp(log_g[:, None] * (CHUNK - 1.0 - l))[:, :, None] * np.ones((1, 1, HEAD_W))
    cd = np.exp(log_g * CHUNK)
    return (jnp.asarray(intra, F32), jnp.asarray(qd, F32), jnp.asarray(kd, F32),
            tuple(float(v) for v in cd))


def _rotary_tables(s):
    half = HEAD_W // 2
    inv = ROPE_BASE ** (-np.arange(half, dtype=np.float64) / half)
    ang = np.arange(s, dtype=np.float64)[:, None] * inv[None, :]
    cos = np.cos(ang)
    sin = np.sin(ang)
    return (jnp.asarray(np.concatenate([cos, cos], axis=-1), F32),
            jnp.asarray(np.concatenate([-sin, sin], axis=-1), F32))


def _mixer(proj, b, s, ret_norm_g, m_conv_w, m_conv_b, m_wq, m_wk, m_b_i, m_b_f,
           m_norm_g, m_skip):
    intra, qd, kd, cd = _retention_tables()
    cos_t, sin_t = _rotary_tables(s)
    tri = jnp.asarray(np.tril(np.ones((CHUNK, CHUNK))), F32)
    nj = s // L_BLOCK
    pad = lambda v: jnp.pad(v.reshape(1, -1), ((0, 0), (0, LANES - v.size)))
    full = lambda shape: pl.BlockSpec(shape, lambda bi, j: (0,) * len(shape))
    return pl.pallas_call(
        functools.partial(_mixer_kernel, cd),
        grid=(b, nj),
        in_specs=[
            pl.BlockSpec((L_BLOCK, D_PROJ_PAD), lambda bi, j: (bi * nj + j, 0)),
            pl.BlockSpec((L_BLOCK, HEAD_W), lambda bi, j: (j, 0)),
            pl.BlockSpec((L_BLOCK, HEAD_W), lambda bi, j: (j, 0)),
            full((RET_HEADS, CHUNK, CHUNK)),
            full((RET_HEADS, CHUNK, HEAD_W)),
            full((RET_HEADS, CHUNK, HEAD_W)),
            full((CHUNK, CHUNK)),
            full((1, D_GROUP)),
            full((CONV_W, D_GROUP)),
            full((1, D_GROUP)),
            full((MLSTM_HEADS, MLSTM_D, MLSTM_D)),
            full((MLSTM_HEADS, MLSTM_D, MLSTM_D)),
            full((1, LANES)),
            full((1, LANES)),
            full((1, D_GROUP)),
            full((1, D_GROUP)),
        ],
        out_specs=pl.BlockSpec((L_BLOCK, 2 * D_GROUP), lambda bi, j: (bi * nj + j, 0)),
        out_shape=jax.ShapeDtypeStruct((b * s, 2 * D_GROUP), F32),
        scratch_shapes=[
            pltpu.VMEM((RET_HEADS, RET_DK, HEAD_W), F32),
            pltpu.VMEM((MLSTM_HEADS, MLSTM_D, 2 * HEAD_W), F32),
            pltpu.VMEM((1, LANES), F32),
            pltpu.VMEM((CONV_TAIL, D_GROUP), F32),
        ],
        compiler_params=pltpu.CompilerParams(
            dimension_semantics=("arbitrary", "arbitrary"), vmem_limit_bytes=VMEM_LIMIT),
        name="retention_mlstm_mixer",
    )(proj, cos_t, sin_t, intra, qd, kd, tri,
      ret_norm_g.reshape(1, D_GROUP), m_conv_w, m_conv_b.reshape(1, D_GROUP),
      m_wq.astype(BF16), m_wk.astype(BF16), pad(m_b_i), pad(m_b_f),
      m_norm_g.reshape(1, D_GROUP), m_skip.reshape(1, D_GROUP))


def _post_kernel(y_ref, x_ref, wout_ref, g2_ref, wq_ref, keys_ref,
                 x1_ref, xn_ref, sub_ref):
    x1 = x_ref[...] + jnp.dot(y_ref[...].astype(BF16), wout_ref[...],
                              preferred_element_type=F32)
    x1_ref[...] = x1
    xn = _rms(x1, g2_ref[...])
    xn_ref[...] = xn
    q = jnp.dot(xn.astype(BF16), wq_ref[...], preferred_element_type=F32)
    for i in range(2 * PEER_HEADS):
        lo = i * PEER_HALF
        sub_ref[lo:lo + PEER_NKEYS, :] = _dot_nt(keys_ref[i], q[:, lo:lo + PEER_HALF])


def _post(y, x2, row0, w_out, norm2_g, peer_wq, keys):
    t, d = y.shape
    nq = peer_wq.shape[1]
    b0 = row0 // TM_PROJ
    row = lambda w: pl.BlockSpec((TM_PROJ, w), lambda i: (i, 0))
    full = lambda shape: pl.BlockSpec(shape, lambda i: (0,) * len(shape))
    return pl.pallas_call(
        _post_kernel,
        grid=(t // TM_PROJ,),
        in_specs=[row(d), pl.BlockSpec((TM_PROJ, d), lambda i: (i + b0, 0)), full((d, d)),
                  full((1, d)), full((d, nq)), full(keys.shape)],
        out_specs=[row(d), row(d), pl.BlockSpec((nq, TM_PROJ), lambda i: (0, i))],
        out_shape=[jax.ShapeDtypeStruct((t, d), F32), jax.ShapeDtypeStruct((t, d), F32),
                   jax.ShapeDtypeStruct((nq, t), F32)],
        compiler_params=pltpu.CompilerParams(
            dimension_semantics=("arbitrary",), vmem_limit_bytes=VMEM_LIMIT),
        name="out_proj_peer_scores",
    )(y, x2, w_out, norm2_g, peer_wq, keys)


def _final_kernel(x1_ref, p_ref, g_ref, *rest):
    o_ref = rest[-1]
    o_ref[...] = _rms(x1_ref[...] + p_ref[...], g_ref[...])


def _final(parts, g, t):
    d = g.shape[1]
    spec = lambda r0: pl.BlockSpec((TM_FINAL, d), lambda i, b0=r0 // TM_FINAL: (i + b0, 0))
    out = None
    row0 = 0
    for x1, x_row0, p, p_row0, rows in parts:
        in_specs = [spec(x_row0), spec(p_row0), pl.BlockSpec((1, d), lambda i: (0, 0))]
        args = [x1, p, g]
        if out is not None:
            in_specs.append(pl.BlockSpec(memory_space=pl.ANY))
            args.append(out)
        out = pl.pallas_call(
            _final_kernel,
            grid=(rows // TM_FINAL,),
            in_specs=in_specs,
            out_specs=spec(row0),
            out_shape=jax.ShapeDtypeStruct((t, d), F32),
            input_output_aliases={} if len(args) == 3 else {3: 0},
            compiler_params=pltpu.CompilerParams(dimension_semantics=("arbitrary",)),
            name="final_rmsnorm",
        )(*args)
        row0 += rows
    assert row0 == t
    return out


TM_ROUTE = 256
CAND_ROWS = 56


def _candidate_tables():
    pairs = [(k1, k2) for k1 in range(PEER_TOPK) for k2 in range(PEER_TOPK)
             if (k1 + 1) * (k2 + 1) <= PEER_TOPK]
    assert len(pairs) <= CAND_ROWS
    rep = np.zeros((CAND_ROWS, 2 * PEER_TOPK), np.float32)
    til = np.zeros((CAND_ROWS, 2 * PEER_TOPK), np.float32)
    for pos, (k1, k2) in enumerate(pairs):
        rep[pos, k1] = 1.0
        til[pos, PEER_TOPK + k2] = 1.0
    return jnp.asarray(rep), jnp.asarray(til), len(pairs)


def _dot_exact(a, b):
    return jnp.dot(a, b, preferred_element_type=F32, precision=lax.Precision.HIGHEST)


def _route_kernel(n_cand, sub_ref, rep_ref, til_ref, seg_ref, exp_ref, gate_ref, offs_ref,
                  s_ref, i_ref, best_ref, eid_ref, m0_ref):
    tm = sub_ref.shape[1]
    neg = -jnp.inf
    key_id = lax.broadcasted_iota(jnp.int32, (PEER_NKEYS, tm), 0).astype(F32)
    cand_id = lax.broadcasted_iota(jnp.int32, (CAND_ROWS, tm), 0).astype(F32)
    rep = rep_ref[...]
    til = til_ref[...]
    for h in range(PEER_HEADS):
        for p in range(2):
            lo = (2 * h + p) * PEER_NKEYS
            x = sub_ref[lo:lo + PEER_NKEYS, :]
            for k in range(PEER_TOPK):
                m = jnp.max(x, axis=0, keepdims=True)
                idx = jnp.min(jnp.where(x == m, key_id, float(PEER_NKEYS)), axis=0, keepdims=True)
                x = jnp.where(key_id == idx, neg, x)
                r = p * PEER_TOPK + k
                s_ref[r:r + 1, :] = m
                i_ref[r:r + 1, :] = idx
        s = s_ref[...]
        iv = i_ref[...]
        cand = _dot_exact(rep, s) + _dot_exact(til, s)
        cand = jnp.where(cand_id < float(n_cand), cand, neg)
        eid = _dot(rep, iv) * float(PEER_NKEYS) + _dot(til, iv)
        for k in range(PEER_TOPK):
            m = jnp.max(cand, axis=0, keepdims=True)
            pos = jnp.min(jnp.where(cand == m, cand_id, float(CAND_ROWS)), axis=0, keepdims=True)
            sel = cand_id == pos
            e = jnp.max(jnp.where(sel, eid, -1.0), axis=0, keepdims=True)
            cand = jnp.where(sel, neg, cand)
            r = h * PEER_TOPK + k
            best_ref[r:r + 1, :] = m
            eid_ref[r:r + 1, :] = e
            if k == 0:
                m0_ref[r:r + PEER_TOPK, :] = jnp.broadcast_to(m, (PEER_TOPK, tm))
    pexp = jnp.exp(best_ref[...] - m0_ref[...])
    gates = pexp / _dot_exact(seg_ref[...], pexp)
    gate_ref[...] = gates.T
    eid = eid_ref[...].astype(jnp.int32)
    exp_ref[...] = eid.T
    for blk in range(tm // LANES):
        offs_ref[blk] = eid[:, blk * LANES:(blk + 1) * LANES] * ROW_WORDS


def _peer_route(sub_t):
    nq, t = sub_t.shape
    rep, til, n_cand = _candidate_tables()
    e = PEER_HEADS * PEER_TOPK
    seg = jnp.asarray(np.kron(np.eye(PEER_HEADS), np.ones((PEER_TOPK, PEER_TOPK))), F32)
    full = lambda a: pl.BlockSpec(a.shape, lambda i: (0,) * a.ndim)
    out = pl.BlockSpec((TM_ROUTE, e), lambda i: (i, 0))
    return pl.pallas_call(
        functools.partial(_route_kernel, n_cand),
        grid=(t // TM_ROUTE,),
        in_specs=[pl.BlockSpec((nq, TM_ROUTE), lambda i: (0, i)), full(rep), full(til), full(seg)],
        out_specs=[out, out, pl.BlockSpec((TM_ROUTE // LANES, e, LANES), lambda i: (i, 0, 0))],
        out_shape=[jax.ShapeDtypeStruct((t, e), jnp.int32), jax.ShapeDtypeStruct((t, e), F32),
                   jax.ShapeDtypeStruct((t // LANES, e, LANES), jnp.int32)],
        scratch_shapes=[
            pltpu.VMEM((2 * PEER_TOPK, TM_ROUTE), F32),
            pltpu.VMEM((2 * PEER_TOPK, TM_ROUTE), F32),
            pltpu.VMEM((e, TM_ROUTE), F32),
            pltpu.VMEM((e, TM_ROUTE), F32),
            pltpu.VMEM((e, TM_ROUTE), F32),
        ],
        compiler_params=pltpu.CompilerParams(dimension_semantics=("arbitrary",)),
        name="peer_topk_route",
    )(sub_t, rep, til, seg)


SC_CORES = 2
SC_SUBCORES = 16
SC_LANES = 16
SC_WORKERS = SC_CORES * SC_SUBCORES
PEER_E = PEER_HEADS * PEER_TOPK
SC_ROWS = 32
SC_GATHERS = PEER_E // SC_ROWS
SC_TOK_BLOCK = 8
SC_QUARTER = 256


def _sc_mesh():
    return plsc.VectorSubcoreMesh(core_axis_name="c", subcore_axis_name="s")


def _sc_pipeline(idx_v, tab_hbm, rows_v, sems, compute):
    def gather(i, c, buf):
        window = idx_v.at[i, pl.ds(c * SC_ROWS, SC_ROWS)]
        return pltpu.make_async_copy(tab_hbm.at[window], rows_v.at[buf], sems.at[buf])

    gather(0, 0, 0).start()

    @pl.loop(0, SC_TOK_BLOCK)
    def _(i):
        for c in range(SC_GATHERS):
            buf = c % 2
            gather(i, c, buf).wait()
            if c + 1 < SC_GATHERS:
                gather(i, c + 1, 1 - buf).start()
            else:
                @pl.when(i + 1 < SC_TOK_BLOCK)
                def _():
                    gather(i + 1, 0, 1 - buf).start()
            compute(i, c, i * SC_GATHERS + c, buf)


def _peer_down_dots(xn, idx, down, tok_start):
    t_all, d = xn.shape
    t = t_all - tok_start
    tok_w = t // SC_WORKERS
    nblk = tok_w // SC_TOK_BLOCK
    assert nblk * SC_TOK_BLOCK * SC_WORKERS == t
    lanes = SC_LANES

    @functools.partial(
        pl.kernel, mesh=_sc_mesh(),
        out_type=jax.ShapeDtypeStruct((t, PEER_E), F32),
        scratch_types=[
            pltpu.VMEM((SC_TOK_BLOCK, PEER_E), jnp.int32),
            pltpu.VMEM((SC_TOK_BLOCK, d), F32),
            pltpu.VMEM((2, SC_ROWS, d), F32),
            pltpu.VMEM((SC_ROWS, lanes), F32),
            pltpu.VMEM((SC_TOK_BLOCK, PEER_E), F32),
            pltpu.SemaphoreType.DMA((2,)),
        ],
        compiler_params=pltpu.CompilerParams(needs_layout_passes=False),
        name="peer_down_dots",
    )
    def k(x_hbm, idx_hbm, tab_hbm, out_hbm, idx_v, x_v, rows_v, acc_v, out_v, sems):
        wid = lax.axis_index("s") * SC_CORES + lax.axis_index("c")
        lane = lax.iota(jnp.int32, lanes)

        @pl.loop(0, nblk)
        def _(blk):
            out0 = wid * tok_w + blk * SC_TOK_BLOCK
            tok0 = tok_start + out0
            pltpu.sync_copy(idx_hbm.at[pl.ds(tok0, SC_TOK_BLOCK)], idx_v)
            pltpu.sync_copy(x_hbm.at[pl.ds(tok0, SC_TOK_BLOCK)], x_v)

            def compute(i, c, step, buf):
                def body(dc, accs):
                    xv = x_v[i, pl.ds(dc * lanes, lanes)]
                    return tuple(a + rows_v[buf, r, pl.ds(dc * lanes, lanes)] * xv
                                 for r, a in enumerate(accs))
                accs = lax.fori_loop(0, d // lanes, body,
                                     tuple(jnp.zeros((lanes,), F32) for _ in range(SC_ROWS)))
                for r in range(SC_ROWS):
                    acc_v[r, :] = accs[r]
                for g in range(SC_ROWS // lanes):
                    tot = jnp.zeros((lanes,), F32)
                    for l in range(lanes):
                        tot = tot + plsc.load_gather(
                            acc_v, [lane + g * lanes, jnp.full((lanes,), l, jnp.int32)])
                    out_v[i, pl.ds(c * SC_ROWS + g * lanes, lanes)] = tot

            _sc_pipeline(idx_v, tab_hbm, rows_v, sems, compute)
            pltpu.sync_copy(out_v, out_hbm.at[pl.ds(out0, SC_TOK_BLOCK)])

    return k(xn, idx, down)


def _peer_up_sum(coef, idx, up, tok_start):
    t_all = coef.shape[0]
    t = t_all - tok_start
    d = up.shape[1]
    tok_w = t // SC_WORKERS
    nblk = tok_w // SC_TOK_BLOCK
    assert nblk * SC_TOK_BLOCK * SC_WORKERS == t
    lanes = SC_LANES
    nj = SC_QUARTER // lanes

    @functools.partial(
        pl.kernel, mesh=_sc_mesh(),
        out_type=jax.ShapeDtypeStruct((t, d), F32),
        scratch_types=[
            pltpu.VMEM((SC_TOK_BLOCK, PEER_E), jnp.int32),
            pltpu.VMEM((SC_TOK_BLOCK * PEER_E,), F32),
            pltpu.VMEM((2, SC_ROWS, d), F32),
            pltpu.VMEM((SC_TOK_BLOCK, d), F32),
            pltpu.SemaphoreType.DMA((2,)),
        ],
        compiler_params=pltpu.CompilerParams(needs_layout_passes=False),
        name="peer_up_sum",
    )
    def k(coef_hbm, idx_hbm, tab_hbm, out_hbm, idx_v, coef_v, rows_v, out_v, sems):
        wid = lax.axis_index("s") * SC_CORES + lax.axis_index("c")

        @pl.loop(0, nblk)
        def _(blk):
            out0 = wid * tok_w + blk * SC_TOK_BLOCK
            tok0 = tok_start + out0
            pltpu.sync_copy(idx_hbm.at[pl.ds(tok0, SC_TOK_BLOCK)], idx_v)
            pltpu.sync_copy(coef_hbm.at[pl.ds(tok0 * PEER_E, SC_TOK_BLOCK * PEER_E)], coef_v)

            def compute(i, c, step, buf):
                for q in range(d // SC_QUARTER):
                    col = lambda j: pl.ds(q * SC_QUARTER + j * lanes, lanes)
                    if c == 0:
                        init = tuple(jnp.zeros((lanes,), F32) for _ in range(nj))
                    else:
                        init = tuple(out_v[i, col(j)] for j in range(nj))

                    def body(r, accs):
                        cf = plsc.load_gather(
                            coef_v, [jnp.full((lanes,), step * SC_ROWS + r, jnp.int32)])
                        return tuple(a + rows_v[buf, r, col(j)] * cf for j, a in enumerate(accs))
                    accs = lax.fori_loop(0, SC_ROWS, body, init)
                    for j in range(nj):
                        out_v[i, col(j)] = accs[j]

            _sc_pipeline(idx_v, tab_hbm, rows_v, sems, compute)
            pltpu.sync_copy(out_v, out_hbm.at[pl.ds(out0, SC_TOK_BLOCK)])

    return k(coef.reshape(t_all * PEER_E), idx, up)


TC_SUB = 128
TC_SUBS = 2
TC_STEP = TC_SUB * TC_SUBS
TC_TRIP = 32
TC_KCHUNKS = 4
ROW_WORDS = 4
ROW_HALVES = 2 * ROW_WORDS
STACK_COLS = PEER_E * ROW_HALVES
FRONT_PARTS = 4
SC_DOWN_PARTS = 1
SC_DOWN_EXTRA = 7424
PEER_TC_UP_TOKENS = 25088


def _pack_rows_bf16(tab):
    n, d = tab.shape
    return pl.pallas_call(
        _pack_kernel,
        grid=(n // PACK_ROWS,),
        in_specs=[pl.BlockSpec((PACK_ROWS, d), lambda i: (i, 0))],
        out_specs=pl.BlockSpec((PACK_ROWS * ROW_WORDS, LANES), lambda i: (i, 0)),
        out_shape=jax.ShapeDtypeStruct((n * ROW_WORDS, LANES), jnp.uint32),
        compiler_params=pltpu.CompilerParams(dimension_semantics=("arbitrary",)),
        name="pack_rows_bf16",
    )(tab)


PACK_ROWS = 512


def _pack_kernel(x_ref, o_ref):
    u = pltpu.bitcast(x_ref[...], jnp.uint32)
    r = (u + jnp.uint32(0x7FFF) + ((u >> 16) & jnp.uint32(1))) >> 16
    for s in range(ROW_WORDS):
        lo = r[:, 2 * LANES * s:2 * LANES * s + LANES]
        hi = r[:, 2 * LANES * s + LANES:2 * LANES * (s + 1)]
        o_ref[pl.ds(s, PACK_ROWS, stride=ROW_WORDS), :] = lo | (hi << 16)


def _load_table_once(tab_hbm, tab_v, sem):
    @pl.when(pl.program_id(0) == 0)
    def _():
        cp = pltpu.make_async_copy(tab_hbm, tab_v, sem)
        cp.start()
        cp.wait()


def _tc_token_pipeline(idx_hbm, idx_s, isem, tab_v, per_token):
    step = pl.program_id(0)

    def idx_copy(block, slot):
        return pltpu.make_async_copy(idx_hbm.at[block], idx_s[slot], isem.at[slot])

    @pl.when(step == 0)
    def _():
        idx_copy(0, 0).start()

    for sub in range(TC_SUBS):
        block = step * TC_SUBS + sub
        idx_copy(block, sub).wait()
        if sub + 1 < TC_SUBS:
            idx_copy(block + 1, sub + 1).start()
        else:
            @pl.when(step + 1 < pl.num_programs(0))
            def _():
                idx_copy(block + 1, 0).start()
        offs = idx_s[sub]

        def chunks(i, offs=offs):
            per = PEER_E // TC_KCHUNKS
            for ch in range(TC_KCHUNKS):
                pieces = [tab_v[pl.ds(pl.multiple_of(offs.at[k][i], ROW_WORDS), ROW_WORDS), :]
                          for k in range(ch * per, (ch + 1) * per)]
                yield pltpu.bitcast(jnp.concatenate(pieces, axis=0), BF16)

        def trip(ip, carry, sub=sub, chunks=chunks):
            for u in range(TC_TRIP):
                i = ip * TC_TRIP + u
                per_token(sub * TC_SUB + i, chunks(i))
            return carry
        lax.fori_loop(0, TC_SUB // TC_TRIP, trip, 0)


def _tc_peer_scratch(tabp):
    return [pltpu.VMEM(tabp.shape, jnp.uint32),
            [pltpu.SMEM((PEER_E, TC_SUB), jnp.int32) for _ in range(TC_SUBS)],
            pltpu.SemaphoreType.DMA,
            pltpu.SemaphoreType.DMA((TC_SUBS,))]


def _stack_selectors():
    col = np.arange(STACK_COLS)
    diag = (col[None, :] % ROW_HALVES == np.arange(ROW_HALVES)[:, None]).astype(np.float32)
    owner = (col[:, None] // ROW_HALVES == np.arange(PEER_E)[None, :]).astype(np.float32)
    return jnp.asarray(diag), jnp.asarray(owner)


def _tc_down_kernel(idx_hbm, x_ref, tab_hbm, diag_ref, fold_ref, o_ref,
                    tab_v, idx_s, sem, isem, z_ref):
    _load_table_once(tab_hbm, tab_v, sem)
    cols = STACK_COLS // TC_KCHUNKS

    def per_token(t, chunks):
        xr = x_ref[pl.ds(t, 1), :]
        x8 = jnp.concatenate([xr[:, LANES * j:LANES * (j + 1)] for j in range(ROW_HALVES)],
                             axis=0).astype(BF16)
        parts = []
        for ch, rows in enumerate(chunks):
            a = lax.dot_general(x8, rows, (((1,), (1,)), ((), ())), preferred_element_type=F32)
            parts.append(jnp.sum(a * diag_ref[:, cols * ch:cols * (ch + 1)], axis=0, keepdims=True))
        z_ref[pl.ds(t, 1), :] = jnp.concatenate(parts, axis=1)

    _tc_token_pipeline(idx_hbm, idx_s, isem, tab_v, per_token)
    o_ref[...] = _dot_exact(z_ref[...], fold_ref[...])


def _peer_down_dots_tc(xn, offs, tabp, n_tok):
    diag, owner = _stack_selectors()
    full = lambda a: pl.BlockSpec(a.shape, lambda i: (0,) * a.ndim)
    return pl.pallas_call(
        _tc_down_kernel,
        grid=(n_tok // TC_STEP,),
        in_specs=[pl.BlockSpec(memory_space=pl.ANY),
                  pl.BlockSpec((TC_STEP, xn.shape[1]), lambda i: (i, 0)),
                  pl.BlockSpec(memory_space=pl.ANY), full(diag), full(owner)],
        out_specs=pl.BlockSpec((TC_STEP, PEER_E), lambda i: (i, 0)),
        out_shape=jax.ShapeDtypeStruct((n_tok, PEER_E), F32),
        scratch_shapes=_tc_peer_scratch(tabp) + [pltpu.VMEM((TC_STEP, STACK_COLS), F32)],
        compiler_params=pltpu.CompilerParams(
            dimension_semantics=("arbitrary",), vmem_limit_bytes=VMEM_LIMIT),
        name="peer_down_dots_tc",
    )(offs, xn, tabp, diag, owner)


def _tc_up_kernel(idx_hbm, coef_ref, tab_hbm, spread_ref, diag_ref, o_ref,
                  tab_v, idx_s, sem, isem, ce_ref):
    _load_table_once(tab_hbm, tab_v, sem)
    c = coef_ref[...]
    c_hi = c.astype(BF16)
    c_lo = (c - c_hi.astype(F32)).astype(BF16)
    ce_ref[0] = jnp.dot(c_hi, spread_ref[...], preferred_element_type=F32)
    ce_ref[1] = jnp.dot(c_lo, spread_ref[...], preferred_element_type=F32)

    cols = STACK_COLS // TC_KCHUNKS

    def per_token(t, chunks):
        d = diag_ref[...]
        lhs = jnp.concatenate([(ce_ref[0, pl.ds(t, 1), :] * d).astype(BF16),
                               (ce_ref[1, pl.ds(t, 1), :] * d).astype(BF16)], axis=0)
        r = jnp.zeros((2 * ROW_HALVES, LANES), F32)
        for ch, rows in enumerate(chunks):
            r = r + jnp.dot(lhs[:, cols * ch:cols * (ch + 1)], rows, preferred_element_type=F32)
        out8 = r[0:ROW_HALVES] + r[ROW_HALVES:2 * ROW_HALVES]
        o_ref[pl.ds(t, 1), :] = jnp.concatenate(
            [out8[j:j + 1, :] for j in range(ROW_HALVES)], axis=1)

    _tc_token_pipeline(idx_hbm, idx_s, isem, tab_v, per_token)


def _peer_up_sum_tc(coef, offs, tabp, n_tok):
    d = ROW_HALVES * LANES
    diag, owner = _stack_selectors()
    spread = owner.T.astype(BF16)
    full = lambda a: pl.BlockSpec(a.shape, lambda i: (0,) * a.ndim)
    return pl.pallas_call(
        _tc_up_kernel,
        grid=(n_tok // TC_STEP,),
        in_specs=[pl.BlockSpec(memory_space=pl.ANY),
                  pl.BlockSpec((TC_STEP, PEER_E), lambda i: (i, 0)),
                  pl.BlockSpec(memory_space=pl.ANY), full(spread), full(diag)],
        out_specs=pl.BlockSpec((TC_STEP, d), lambda i: (i, 0)),
        out_shape=jax.ShapeDtypeStruct((n_tok, d), F32),
        scratch_shapes=_tc_peer_scratch(tabp) + [pltpu.VMEM((2, TC_STEP, STACK_COLS), F32)],
        compiler_params=pltpu.CompilerParams(
            dimension_semantics=("arbitrary",), vmem_limit_bytes=VMEM_LIMIT),
        name="peer_up_sum_tc",
    )(offs, coef, tabp, spread, diag)


def _coef_kernel(pre_ref, gate_ref, o_ref):
    x = pre_ref[...]
    gelu = 0.5 * x * (1.0 + lax.erf(x * (2.0 ** -0.5)))
    o_ref[...] = gate_ref[...] * gelu


def _peer_coef(pre, gates):
    t, e = pre.shape
    row = pl.BlockSpec((TM_FINAL, e), lambda i: (i, 0))
    return pl.pallas_call(
        _coef_kernel,
        grid=(t // TM_FINAL,),
        in_specs=[row, row],
        out_specs=row,
        out_shape=jax.ShapeDtypeStruct((t, e), F32),
        compiler_params=pltpu.CompilerParams(dimension_semantics=("arbitrary",)),
        name="peer_gate_gelu",
    )(pre, gates)


def _front_half(x2, row0, rows, bsz, seq, w_pad, layer, i):
    (norm1_g, ret_norm_g, m_conv_w, m_conv_b, m_wq, m_wk, m_b_i, m_b_f, m_norm_g, m_skip,
     w_out, norm2_g, peer_wq, keys) = layer
    d = x2.shape[1]
    proj = _rms_proj(x2, row0, rows, norm1_g[i].reshape(1, d), w_pad)
    y = _mixer(proj, bsz, seq, ret_norm_g[i], m_conv_w[i], m_conv_b[i], m_wq[i], m_wk[i],
               m_b_i[i], m_b_f[i], m_norm_g[i], m_skip[i])
    x1, xn, sub_t = _post(y, x2, row0, w_out, norm2_g[i].reshape(1, d), peer_wq, keys)
    return (x1, xn) + tuple(_peer_route(sub_t))


def kernel(x, norm1_g, w_in, ret_norm_g, m_conv_w, m_conv_b, m_wq, m_wk, m_b_i, m_b_f,
           m_norm_g, m_skip, w_out, norm2_g, peer_wq, peer_keys, peer_down, peer_up, final_g):
    b, s, d = x.shape
    t = b * s
    tp = t // FRONT_PARTS
    x2 = x.reshape(t, d)
    depth = norm1_g.shape[0]
    for i in range(depth):
        if i > 0:
            x2 = jnp.concatenate(x1s, axis=0) + jnp.concatenate([p_tc, p_sc], axis=0)
        wi = w_in[i]
        n_main = OFF_GI
        gate_pad = lambda w: jnp.pad(w, ((0, 0), (0, LANES - w.shape[1])))
        w_pad = jnp.concatenate(
            [wi[:, :n_main], gate_pad(wi[:, n_main:n_main + MLSTM_HEADS]),
             gate_pad(wi[:, n_main + MLSTM_HEADS:])], axis=1).astype(BF16)
        keys = peer_keys[i].reshape(2 * PEER_HEADS, PEER_NKEYS, PEER_HALF).astype(BF16)
        layer = (norm1_g, ret_norm_g, m_conv_w, m_conv_b, m_wq, m_wk, m_b_i, m_b_f, m_norm_g,
                 m_skip, w_out[i].astype(BF16), norm2_g, peer_wq[i].astype(BF16), keys)
        down_p = _pack_rows_bf16(peer_down[i])
        up_p = _pack_rows_bf16(peer_up[i])

        fronts = []
        pre = [None] * FRONT_PARTS
        for q in range(FRONT_PARTS):
            fr = _front_half(x2, q * tp, tp, b // FRONT_PARTS, s, w_pad, layer, i)
            fronts.append(fr)
            if q < SC_DOWN_PARTS:
                pre[q] = [_peer_down_dots(fr[1], fr[2], peer_down[i], 0)]
            elif q == SC_DOWN_PARTS and SC_DOWN_EXTRA:
                pre[q] = [None, _peer_down_dots(fr[1], fr[2], peer_down[i], tp - SC_DOWN_EXTRA)]
        for q in range(SC_DOWN_PARTS, FRONT_PARTS):
            n_tc_q = tp - SC_DOWN_EXTRA if q == SC_DOWN_PARTS else tp
            pre_tc = _peer_down_dots_tc(fronts[q][1], fronts[q][4], down_p, n_tc_q)
            if pre[q] is None:
                pre[q] = [pre_tc]
            else:
                pre[q][0] = pre_tc
        pre = [piece for part in pre for piece in part]
        x1s = [fr[0] for fr in fronts]
        expert = jnp.concatenate([fr[2] for fr in fronts], axis=0)
        gates = jnp.concatenate([fr[3] for fr in fronts], axis=0)
        offs = jnp.concatenate([fr[4] for fr in fronts], axis=0)
        coef = _peer_coef(jnp.concatenate(pre, axis=0), gates)
        n_tc = PEER_TC_UP_TOKENS
        p_sc = _peer_up_sum(coef, expert, peer_up[i], n_tc)
        p_tc = _peer_up_sum_tc(coef, offs, up_p, n_tc)
    out = _final(_row_ranges(x1s, [p_tc, p_sc]), final_g.reshape(1, d), t)
    return out.reshape(b, s, d)


def _row_ranges(xs, ps):
    def starts(arrs):
        out = [0]
        for a in arrs:
            out.append(out[-1] + a.shape[0])
        return out
    xb, pb = starts(xs), starts(ps)
    assert xb[-1] == pb[-1]
    cuts = sorted(set(xb) | set(pb))
    parts = []
    for lo, hi in zip(cuts[:-1], cuts[1:]):
        xi = max(k for k in range(len(xs)) if xb[k] <= lo)
        pi = max(k for k in range(len(ps)) if pb[k] <= lo)
        parts.append((xs[xi], lo - xb[xi], ps[pi], lo - pb[pi], hi - lo))
    return parts
```

```python
import functools

import numpy as np
import jax
import jax.numpy as jnp
from jax import lax
from jax.experimental import pallas as pl
from jax.experimental.pallas import tpu as pltpu
from jax.experimental.pallas import tpu_sc as plsc

F32 = jnp.float32
BF16 = jnp.bfloat16

D_MODEL = 1024
CHUNK = 64
RET_HEADS = 4
RET_DK = 128
MLSTM_HEADS = 4
MLSTM_D = 128
HEAD_W = 128
D_GROUP = 512
CONV_W = 4
ROPE_BASE = 10000.0
PEER_HEADS = 8
PEER_NKEYS = 128
PEER_TOPK = 16
PEER_HALF = 128
PEER_BLOCK = 128
EPS = 1e-6

OFF_RQ, OFF_RK, OFF_RV, OFF_RG = 0, 512, 1024, 1536
OFF_MX, OFF_MV, OFF_MO = 2048, 2560, 3072
OFF_GI, OFF_GF = 3584, 3712
D_PROJ_PAD = 3840

LANES = 128
CONV_TAIL = 8
VMEM_LIMIT = 56 * 1024 * 1024

TM_PROJ = 256
L_BLOCK = 256
TM_FINAL = 256


def _rms(x, g):
    ms = jnp.mean(x * x, axis=-1, keepdims=True)
    return x * lax.rsqrt(ms + EPS) * g


def _group_norm(h, g):
    mu = jnp.mean(h, axis=-1, keepdims=True)
    d = h - mu
    var = jnp.mean(d * d, axis=-1, keepdims=True)
    return d * lax.rsqrt(var + EPS) * g


def _silu(x):
    return x * (1.0 / (1.0 + jnp.exp(-x)))


def _sigmoid(x):
    return 1.0 / (1.0 + jnp.exp(-x))


def _dot(a, b):
    return jnp.dot(a.astype(BF16), b.astype(BF16), preferred_element_type=F32)


def _dot_tn(a, b):
    return lax.dot_general(a.astype(BF16), b.astype(BF16), (((0,), (0,)), ((), ())),
                           preferred_element_type=F32)


def _dot_nt(a, b):
    return lax.dot_general(a.astype(BF16), b.astype(BF16), (((1,), (1,)), ((), ())),
                           preferred_element_type=F32)


def _rms_proj_kernel(x_ref, g_ref, w_ref, o_ref):
    h = _rms(x_ref[...], g_ref[...])
    o_ref[...] = jnp.dot(h.astype(BF16), w_ref[...], preferred_element_type=F32)


def _rms_proj(x2, row0, rows, g, w):
    d = x2.shape[1]
    t = rows
    n = w.shape[1]
    b0 = row0 // TM_PROJ
    return pl.pallas_call(
        _rms_proj_kernel,
        grid=(t // TM_PROJ,),
        in_specs=[
            pl.BlockSpec((TM_PROJ, d), lambda i: (i + b0, 0)),
            pl.BlockSpec((1, d), lambda i: (0, 0)),
            pl.BlockSpec((d, n), lambda i: (0, 0)),
        ],
        out_specs=pl.BlockSpec((TM_PROJ, n), lambda i: (i, 0)),
        out_shape=jax.ShapeDtypeStruct((t, n), F32),
        compiler_params=pltpu.CompilerParams(
            dimension_semantics=("arbitrary",), vmem_limit_bytes=VMEM_LIMIT),
        name="rms_in_proj",
    )(x2, g, w)


def _mixer_kernel(chunk_decay,
                  proj_ref, cos_ref, sin_ref, intra_ref, qd_ref, kd_ref, tri_ref,
                  rng_ref, cw_ref, cb_ref, wq_ref, wk_ref, bi_ref, bf_ref,
                  mng_ref, skip_ref,
                  y_ref,
                  rstate_ref, cstate_ref, m_ref, tail_ref):
    j = pl.program_id(1)
    n_chunks = L_BLOCK // CHUNK

    @pl.when(j == 0)
    def _():
        rstate_ref[...] = jnp.zeros_like(rstate_ref)
        cstate_ref[...] = jnp.zeros_like(cstate_ref)
        m_ref[...] = jnp.zeros_like(m_ref)
        tail_ref[...] = jnp.zeros_like(tail_ref)

    cosv = cos_ref[...]
    sinv = sin_ref[...]
    k_scale = RET_DK ** -0.5

    for h in range(RET_HEADS):
        lo = h * HEAD_W
        q = proj_ref[:, OFF_RQ + lo:OFF_RQ + lo + HEAD_W]
        k = proj_ref[:, OFF_RK + lo:OFF_RK + lo + HEAD_W]
        q = q * cosv + pltpu.roll(q, HEAD_W // 2, 1) * sinv
        k = (k * cosv + pltpu.roll(k, HEAD_W // 2, 1) * sinv) * k_scale
        intra = intra_ref[h]
        qd = qd_ref[h]
        kd = kd_ref[h]
        g = rng_ref[:, lo:lo + HEAD_W]
        for c in range(n_chunks):
            r0 = c * CHUNK
            qc = q[r0:r0 + CHUNK]
            kc = k[r0:r0 + CHUNK]
            vc = proj_ref[r0:r0 + CHUNK, OFF_RV + lo:OFF_RV + lo + HEAD_W]
            gate = proj_ref[r0:r0 + CHUNK, OFF_RG + lo:OFF_RG + lo + HEAD_W]
            state = rstate_ref[h]
            scores = _dot_nt(qc, kc) * intra
            o = _dot(scores, vc) + _dot(qc, state) * qd
            rstate_ref[h] = chunk_decay[h] * state + _dot_tn(kc * kd, vc)
            y_ref[r0:r0 + CHUNK, lo:lo + HEAD_W] = _silu(gate) * _group_norm(o, g)

    mx = proj_ref[:, OFF_MX:OFF_MX + D_GROUP]
    xp = jnp.concatenate([tail_ref[...], mx], axis=0)
    base = CONV_TAIL - (CONV_W - 1)
    conv = cb_ref[...] + xp[base:base + L_BLOCK] * cw_ref[0:1, :]
    for t in range(1, CONV_W):
        conv = conv + xp[base + t:base + t + L_BLOCK] * cw_ref[t:t + 1, :]
    tail_ref[...] = mx[L_BLOCK - CONV_TAIL:L_BLOCK]
    cact = _silu(conv)

    gi = proj_ref[:, OFF_GI:OFF_GI + LANES] + bi_ref[...]
    gf = proj_ref[:, OFF_GF:OFF_GF + LANES] + bf_ref[...]
    logf = jnp.minimum(gf, 0.0) - jnp.log1p(jnp.exp(-jnp.abs(gf)))
    tri = tri_ref[...]
    lane = lax.broadcasted_iota(jnp.int32, (CHUNK, HEAD_W), 1)
    ones_col = (lane == 0).astype(F32)
    m_scale = MLSTM_D ** -0.5

    mq = []
    mk = []
    for h in range(MLSTM_HEADS):
        lo = h * HEAD_W
        ch = cact[:, lo:lo + HEAD_W]
        mq.append(_dot(ch, wq_ref[h]))
        mk.append(_dot(ch, wk_ref[h]) * m_scale)

    for c in range(n_chunks):
        r0 = c * CHUNK
        fcum = jnp.dot(tri, logf[r0:r0 + CHUNK], preferred_element_type=F32,
                       precision=lax.Precision.HIGHEST)
        a = fcum[CHUNK - 1:CHUNK]
        logw = a - fcum + gi[r0:r0 + CHUNK]
        m_in = jnp.max(logw, axis=0, keepdims=True)
        m_old = m_ref[...]
        m_new = jnp.maximum(a + m_old, m_in)
        decay = jnp.exp(a + m_old - m_new)
        w = jnp.exp(logw - m_new)
        em = jnp.exp(-m_new)
        m_ref[...] = m_new
        for h in range(MLSTM_HEADS):
            lo = h * HEAD_W
            qh = mq[h][r0:r0 + CHUNK]
            kh = mk[h][r0:r0 + CHUNK]
            vh = proj_ref[r0:r0 + CHUNK, OFF_MV + lo:OFF_MV + lo + HEAD_W]
            oh = proj_ref[r0:r0 + CHUNK, OFF_MO + lo:OFF_MO + lo + HEAD_W]
            ch = cact[r0:r0 + CHUNK, lo:lo + HEAD_W]
            v_ext = jnp.concatenate([vh, ones_col], axis=1)
            cmem = decay[:, h:h + 1] * cstate_ref[h] + _dot_tn(kh * w[:, h:h + 1], v_ext)
            cstate_ref[h] = cmem
            num = _dot(qh, cmem)
            den = jnp.maximum(jnp.abs(num[:, HEAD_W:HEAD_W + 1]), em[:, h:h + 1])
            hh = num[:, :HEAD_W] / den
            hm = _group_norm(hh, mng_ref[:, lo:lo + HEAD_W])
            ym = _sigmoid(oh) * (hm + skip_ref[:, lo:lo + HEAD_W] * ch)
            y_ref[r0:r0 + CHUNK, D_GROUP + lo:D_GROUP + lo + HEAD_W] = ym


def _retention_tables():
    h = np.arange(RET_HEADS, dtype=np.float64)
    log_g = np.log(1.0 - 2.0 ** (-5.0 - h))
    l = np.arange(CHUNK, dtype=np.float64)
    intra = np.exp(log_g[:, None, None] * np.abs(l[:, None] - l[None, :]))
    qd = np.exp(log_g[:, None] * (l + 1.0))[:, :, None] * np.ones((1, 1, HEAD_W))
    kd = np.exp(log_g[:, None] * (CHUNK - 1.0 - l))[:, :, None] * np.ones((1, 1, HEAD_W))
    cd = np.exp(log_g * CHUNK)
    return (jnp.asarray(intra, F32), jnp.asarray(qd, F32), jnp.asarray(kd, F32),
            tuple(float(v) for v in cd))


def _rotary_tables(s):
    half = HEAD_W // 2
    inv = ROPE_BASE ** (-np.arange(half, dtype=np.float64) / half)
    ang = np.arange(s, dtype=np.float64)[:, None] * inv[None, :]
    cos = np.cos(ang)
    sin = np.sin(ang)
    return (jnp.asarray(np.concatenate([cos, cos], axis=-1), F32),
            jnp.asarray(np.concatenate([-sin, sin], axis=-1), F32))


def _mixer(proj, b, s, ret_norm_g, m_conv_w, m_conv_b, m_wq, m_wk, m_b_i, m_b_f,
           m_norm_g, m_skip):
    intra, qd, kd, cd = _retention_tables()
    cos_t, sin_t = _rotary_tables(s)
    tri = jnp.asarray(np.tril(np.ones((CHUNK, CHUNK))), F32)
    nj = s // L_BLOCK
    pad = lambda v: jnp.pad(v.reshape(1, -1), ((0, 0), (0, LANES - v.size)))
    full = lambda shape: pl.BlockSpec(shape, lambda bi, j: (0,) * len(shape))
    return pl.pallas_call(
        functools.partial(_mixer_kernel, cd),
        grid=(b, nj),
        in_specs=[
            pl.BlockSpec((L_BLOCK, D_PROJ_PAD), lambda bi, j: (bi * nj + j, 0)),
            pl.BlockSpec((L_BLOCK, HEAD_W), lambda bi, j: (j, 0)),
            pl.BlockSpec((L_BLOCK, HEAD_W), lambda bi, j: (j, 0)),
            full((RET_HEADS, CHUNK, CHUNK)),
            full((RET_HEADS, CHUNK, HEAD_W)),
            full((RET_HEADS, CHUNK, HEAD_W)),
            full((CHUNK, CHUNK)),
            full((1, D_GROUP)),
            full((CONV_W, D_GROUP)),
            full((1, D_GROUP)),
            full((MLSTM_HEADS, MLSTM_D, MLSTM_D)),
            full((MLSTM_HEADS, MLSTM_D, MLSTM_D)),
            full((1, LANES)),
            full((1, LANES)),
            full((1, D_GROUP)),
            full((1, D_GROUP)),
        ],
        out_specs=pl.BlockSpec((L_BLOCK, 2 * D_GROUP), lambda bi, j: (bi * nj + j, 0)),
        out_shape=jax.ShapeDtypeStruct((b * s, 2 * D_GROUP), F32),
        scratch_shapes=[
            pltpu.VMEM((RET_HEADS, RET_DK, HEAD_W), F32),
            pltpu.VMEM((MLSTM_HEADS, MLSTM_D, 2 * HEAD_W), F32),
            pltpu.VMEM((1, LANES), F32),
            pltpu.VMEM((CONV_TAIL, D_GROUP), F32),
        ],
        compiler_params=pltpu.CompilerParams(
            dimension_semantics=("arbitrary", "arbitrary"), vmem_limit_bytes=VMEM_LIMIT),
        name="retention_mlstm_mixer",
    )(proj, cos_t, sin_t, intra, qd, kd, tri,
      ret_norm_g.reshape(1, D_GROUP), m_conv_w, m_conv_b.reshape(1, D_GROUP),
      m_wq.astype(BF16), m_wk.astype(BF16), pad(m_b_i), pad(m_b_f),
      m_norm_g.reshape(1, D_GROUP), m_skip.reshape(1, D_GROUP))


def _post_kernel(y_ref, x_ref, wout_ref, g2_ref, wq_ref, keys_ref,
                 x1_ref, xn_ref, sub_ref):
    x1 = x_ref[...] + jnp.dot(y_ref[...].astype(BF16), wout_ref[...],
                              preferred_element_type=F32)
    x1_ref[...] = x1
    xn = _rms(x1, g2_ref[...])
    xn_ref[...] = xn
    q = jnp.dot(xn.astype(BF16), wq_ref[...], preferred_element_type=F32)
    for i in range(2 * PEER_HEADS):
        lo = i * PEER_HALF
        sub_ref[lo:lo + PEER_NKEYS, :] = _dot_nt(keys_ref[i], q[:, lo:lo + PEER_HALF])


def _post(y, x2, row0, w_out, norm2_g, peer_wq, keys):
    t, d = y.shape
    nq = peer_wq.shape[1]
    b0 = row0 // TM_PROJ
    row = lambda w: pl.BlockSpec((TM_PROJ, w), lambda i: (i, 0))
    full = lambda shape: pl.BlockSpec(shape, lambda i: (0,) * len(shape))
    return pl.pallas_call(
        _post_kernel,
        grid=(t // TM_PROJ,),
        in_specs=[row(d), pl.BlockSpec((TM_PROJ, d), lambda i: (i + b0, 0)), full((d, d)),
                  full((1, d)), full((d, nq)), full(keys.shape)],
        out_specs=[row(d), row(d), pl.BlockSpec((nq, TM_PROJ), lambda i: (0, i))],
        out_shape=[jax.ShapeDtypeStruct((t, d), F32), jax.ShapeDtypeStruct((t, d), F32),
                   jax.ShapeDtypeStruct((nq, t), F32)],
        compiler_params=pltpu.CompilerParams(
            dimension_semantics=("arbitrary",), vmem_limit_bytes=VMEM_LIMIT),
        name="out_proj_peer_scores",
    )(y, x2, w_out, norm2_g, peer_wq, keys)


def _final_kernel(x1_ref, p_ref, g_ref, *rest):
    o_ref = rest[-1]
    o_ref[...] = _rms(x1_ref[...] + p_ref[...], g_ref[...])


def _final(parts, g, t):
    d = g.shape[1]
    spec = lambda r0: pl.BlockSpec((TM_FINAL, d), lambda i, b0=r0 // TM_FINAL: (i + b0, 0))
    out = None
    row0 = 0
    for x1, x_row0, p, p_row0, rows in parts:
        in_specs = [spec(x_row0), spec(p_row0), pl.BlockSpec((1, d), lambda i: (0, 0))]
        args = [x1, p, g]
        if out is not None:
            in_specs.append(pl.BlockSpec(memory_space=pl.ANY))
            args.append(out)
        out = pl.pallas_call(
            _final_kernel,
            grid=(rows // TM_FINAL,),
            in_specs=in_specs,
            out_specs=spec(row0),
            out_shape=jax.ShapeDtypeStruct((t, d), F32),
            input_output_aliases={} if len(args) == 3 else {3: 0},
            compiler_params=pltpu.CompilerParams(dimension_semantics=("arbitrary",)),
            name="final_rmsnorm",
        )(*args)
        row0 += rows
    assert row0 == t
    return out


TM_ROUTE = 256
CAND_ROWS = 56


def _candidate_tables():
    pairs = [(k1, k2) for k1 in range(PEER_TOPK) for k2 in range(PEER_TOPK)
             if (k1 + 1) * (k2 + 1) <= PEER_TOPK]
    assert len(pairs) <= CAND_ROWS
    rep = np.zeros((CAND_ROWS, 2 * PEER_TOPK), np.float32)
    til = np.zeros((CAND_ROWS, 2 * PEER_TOPK), np.float32)
    for pos, (k1, k2) in enumerate(pairs):
        rep[pos, k1] = 1.0
        til[pos, PEER_TOPK + k2] = 1.0
    return jnp.asarray(rep), jnp.asarray(til), len(pairs)


def _dot_exact(a, b):
    return jnp.dot(a, b, preferred_element_type=F32, precision=lax.Precision.HIGHEST)


def _route_kernel(n_cand, sub_ref, rep_ref, til_ref, seg_ref, exp_ref, gate_ref, offs_ref,
                  s_ref, i_ref, best_ref, eid_ref, m0_ref):
    tm = sub_ref.shape[1]
    neg = -jnp.inf
    key_id = lax.broadcasted_iota(jnp.int32, (PEER_NKEYS, tm), 0).astype(F32)
    cand_id = lax.broadcasted_iota(jnp.int32, (CAND_ROWS, tm), 0).astype(F32)
    rep = rep_ref[...]
    til = til_ref[...]
    for h in range(PEER_HEADS):
        for p in range(2):
            lo = (2 * h + p) * PEER_NKEYS
            x = sub_ref[lo:lo + PEER_NKEYS, :]
            for k in range(PEER_TOPK):
                m = jnp.max(x, axis=0, keepdims=True)
                idx = jnp.min(jnp.where(x == m, key_id, float(PEER_NKEYS)), axis=0, keepdims=True)
                x = jnp.where(key_id == idx, neg, x)
                r = p * PEER_TOPK + k
                s_ref[r:r + 1, :] = m
                i_ref[r:r + 1, :] = idx
        s = s_ref[...]
        iv = i_ref[...]
        cand = _dot_exact(rep, s) + _dot_exact(til, s)
        cand = jnp.where(cand_id < float(n_cand), cand, neg)
        eid = _dot(rep, iv) * float(PEER_NKEYS) + _dot(til, iv)
        for k in range(PEER_TOPK):
            m = jnp.max(cand, axis=0, keepdims=True)
            pos = jnp.min(jnp.where(cand == m, cand_id, float(CAND_ROWS)), axis=0, keepdims=True)
            sel = cand_id == pos
            e = jnp.max(jnp.where(sel, eid, -1.0), axis=0, keepdims=True)
            cand = jnp.where(sel, neg, cand)
            r = h * PEER_TOPK + k
            best_ref[r:r + 1, :] = m
            eid_ref[r:r + 1, :] = e
            if k == 0:
                m0_ref[r:r + PEER_TOPK, :] = jnp.broadcast_to(m, (PEER_TOPK, tm))
    pexp = jnp.exp(best_ref[...] - m0_ref[...])
    gates = pexp / _dot_exact(seg_ref[...], pexp)
    gate_ref[...] = gates.T
    eid = eid_ref[...].astype(jnp.int32)
    exp_ref[...] = eid.T
    for blk in range(tm // LANES):
        offs_ref[blk] = eid[:, blk * LANES:(blk + 1) * LANES] * ROW_WORDS


def _peer_route(sub_t):
    nq, t = sub_t.shape
    rep, til, n_cand = _candidate_tables()
    e = PEER_HEADS * PEER_TOPK
    seg = jnp.asarray(np.kron(np.eye(PEER_HEADS), np.ones((PEER_TOPK, PEER_TOPK))), F32)
    full = lambda a: pl.BlockSpec(a.shape, lambda i: (0,) * a.ndim)
    out = pl.BlockSpec((TM_ROUTE, e), lambda i: (i, 0))
    return pl.pallas_call(
        functools.partial(_route_kernel, n_cand),
        grid=(t // TM_ROUTE,),
        in_specs=[pl.BlockSpec((nq, TM_ROUTE), lambda i: (0, i)), full(rep), full(til), full(seg)],
        out_specs=[out, out, pl.BlockSpec((TM_ROUTE // LANES, e, LANES), lambda i: (i, 0, 0))],
        out_shape=[jax.ShapeDtypeStruct((t, e), jnp.int32), jax.ShapeDtypeStruct((t, e), F32),
                   jax.ShapeDtypeStruct((t // LANES, e, LANES), jnp.int32)],
        scratch_shapes=[
            pltpu.VMEM((2 * PEER_TOPK, TM_ROUTE), F32),
            pltpu.VMEM((2 * PEER_TOPK, TM_ROUTE), F32),
            pltpu.VMEM((e, TM_ROUTE), F32),
            pltpu.VMEM((e, TM_ROUTE), F32),
            pltpu.VMEM((e, TM_ROUTE), F32),
        ],
        compiler_params=pltpu.CompilerParams(dimension_semantics=("arbitrary",)),
        name="peer_topk_route",
    )(sub_t, rep, til, seg)


SC_CORES = 2
SC_SUBCORES = 16
SC_LANES = 16
SC_WORKERS = SC_CORES * SC_SUBCORES
PEER_E = PEER_HEADS * PEER_TOPK
SC_ROWS = 32
SC_GATHERS = PEER_E // SC_ROWS
SC_TOK_BLOCK = 8
SC_QUARTER = 256


def _sc_mesh():
    return plsc.VectorSubcoreMesh(core_axis_name="c", subcore_axis_name="s")


def _sc_pipeline(idx_v, tab_hbm, rows_v, sems, compute):
    def gather(i, c, buf):
        window = idx_v.at[i, pl.ds(c * SC_ROWS, SC_ROWS)]
        return pltpu.make_async_copy(tab_hbm.at[window], rows_v.at[buf], sems.at[buf])

    gather(0, 0, 0).start()

    @pl.loop(0, SC_TOK_BLOCK)
    def _(i):
        for c in range(SC_GATHERS):
            buf = c % 2
            gather(i, c, buf).wait()
            if c + 1 < SC_GATHERS:
                gather(i, c + 1, 1 - buf).start()
            else:
                @pl.when(i + 1 < SC_TOK_BLOCK)
                def _():
                    gather(i + 1, 0, 1 - buf).start()
            compute(i, c, i * SC_GATHERS + c, buf)


def _peer_down_dots(xn, idx, down, tok_start):
    t_all, d = xn.shape
    t = t_all - tok_start
    tok_w = t // SC_WORKERS
    nblk = tok_w // SC_TOK_BLOCK
    assert nblk * SC_TOK_BLOCK * SC_WORKERS == t
    lanes = SC_LANES

    @functools.partial(
        pl.kernel, mesh=_sc_mesh(),
        out_type=jax.ShapeDtypeStruct((t, PEER_E), F32),
        scratch_types=[
            pltpu.VMEM((SC_TOK_BLOCK, PEER_E), jnp.int32),
            pltpu.VMEM((SC_TOK_BLOCK, d), F32),
            pltpu.VMEM((2, SC_ROWS, d), F32),
            pltpu.VMEM((SC_ROWS, lanes), F32),
            pltpu.VMEM((SC_TOK_BLOCK, PEER_E), F32),
            pltpu.SemaphoreType.DMA((2,)),
        ],
        compiler_params=pltpu.CompilerParams(needs_layout_passes=False),
        name="peer_down_dots",
    )
    def k(x_hbm, idx_hbm, tab_hbm, out_hbm, idx_v, x_v, rows_v, acc_v, out_v, sems):
        wid = lax.axis_index("s") * SC_CORES + lax.axis_index("c")
        lane = lax.iota(jnp.int32, lanes)

        @pl.loop(0, nblk)
        def _(blk):
            out0 = wid * tok_w + blk * SC_TOK_BLOCK
            tok0 = tok_start + out0
            pltpu.sync_copy(idx_hbm.at[pl.ds(tok0, SC_TOK_BLOCK)], idx_v)
            pltpu.sync_copy(x_hbm.at[pl.ds(tok0, SC_TOK_BLOCK)], x_v)

            def compute(i, c, step, buf):
                def body(dc, accs):
                    xv = x_v[i, pl.ds(dc * lanes, lanes)]
                    return tuple(a + rows_v[buf, r, pl.ds(dc * lanes, lanes)] * xv
                                 for r, a in enumerate(accs))
                accs = lax.fori_loop(0, d // lanes, body,
                                     tuple(jnp.zeros((lanes,), F32) for _ in range(SC_ROWS)))
                for r in range(SC_ROWS):
                    acc_v[r, :] = accs[r]
                for g in range(SC_ROWS // lanes):
                    tot = jnp.zeros((lanes,), F32)
                    for l in range(lanes):
                        tot = tot + plsc.load_gather(
                            acc_v, [lane + g * lanes, jnp.full((lanes,), l, jnp.int32)])
                    out_v[i, pl.ds(c * SC_ROWS + g * lanes, lanes)] = tot

            _sc_pipeline(idx_v, tab_hbm, rows_v, sems, compute)
            pltpu.sync_copy(out_v, out_hbm.at[pl.ds(out0, SC_TOK_BLOCK)])

    return k(xn, idx, down)


def _peer_up_sum(coef, idx, up, tok_start):
    t_all = coef.shape[0]
    t = t_all - tok_start
    d = up.shape[1]
    tok_w = t // SC_WORKERS
    nblk = tok_w // SC_TOK_BLOCK
    assert nblk * SC_TOK_BLOCK * SC_WORKERS == t
    lanes = SC_LANES
    nj = SC_QUARTER // lanes

    @functools.partial(
        pl.kernel, mesh=_sc_mesh(),
        out_type=jax.ShapeDtypeStruct((t, d), F32),
        scratch_types=[
            pltpu.VMEM((SC_TOK_BLOCK, PEER_E), jnp.int32),
            pltpu.VMEM((SC_TOK_BLOCK * PEER_E,), F32),
            pltpu.VMEM((2, SC_ROWS, d), F32),
            pltpu.VMEM((SC_TOK_BLOCK, d), F32),
            pltpu.SemaphoreType.DMA((2,)),
        ],
        compiler_params=pltpu.CompilerParams(needs_layout_passes=False),
        name="peer_up_sum",
    )
    def k(coef_hbm, idx_hbm, tab_hbm, out_hbm, idx_v, coef_v, rows_v, out_v, sems):
        wid = lax.axis_index("s") * SC_CORES + lax.axis_index("c")

        @pl.loop(0, nblk)
        def _(blk):
            out0 = wid * tok_w + blk * SC_TOK_BLOCK
            tok0 = tok_start + out0
            pltpu.sync_copy(idx_hbm.at[pl.ds(tok0, SC_TOK_BLOCK)], idx_v)
            pltpu.sync_copy(coef_hbm.at[pl.ds(tok0 * PEER_E, SC_TOK_BLOCK * PEER_E)], coef_v)

            def compute(i, c, step, buf):
                for q in range(d // SC_QUARTER):
                    col = lambda j: pl.ds(q * SC_QUARTER + j * lanes, lanes)
                    if c == 0:
                        init = tuple(jnp.zeros((lanes,), F32) for _ in range(nj))
                    else:
                        init = tuple(out_v[i, col(j)] for j in range(nj))

                    def body(r, accs):
                        cf = plsc.load_gather(
                            coef_v, [jnp.full((lanes,), step * SC_ROWS + r, jnp.int32)])
                        return tuple(a + rows_v[buf, r, col(j)] * cf for j, a in enumerate(accs))
                    accs = lax.fori_loop(0, SC_ROWS, body, init)
                    for j in range(nj):
                        out_v[i, col(j)] = accs[j]

            _sc_pipeline(idx_v, tab_hbm, rows_v, sems, compute)
            pltpu.sync_copy(out_v, out_hbm.at[pl.ds(out0, SC_TOK_BLOCK)])

    return k(coef.reshape(t_all * PEER_E), idx, up)


TC_SUB = 128
TC_SUBS = 2
TC_STEP = TC_SUB * TC_SUBS
TC_TRIP = 32
TC_KCHUNKS = 4
ROW_WORDS = 4
ROW_HALVES = 2 * ROW_WORDS
STACK_COLS = PEER_E * ROW_HALVES
FRONT_PARTS = 4
SC_DOWN_PARTS = 1
SC_DOWN_EXTRA = 6656
PEER_TC_UP_TOKENS = 25088


def _pack_rows_bf16(tab):
    n, d = tab.shape
    return pl.pallas_call(
        _pack_kernel,
        grid=(n // PACK_ROWS,),
        in_specs=[pl.BlockSpec((PACK_ROWS, d), lambda i: (i, 0))],
        out_specs=pl.BlockSpec((PACK_ROWS * ROW_WORDS, LANES), lambda i: (i, 0)),
        out_shape=jax.ShapeDtypeStruct((n * ROW_WORDS, LANES), jnp.uint32),
        compiler_params=pltpu.CompilerParams(dimension_semantics=("arbitrary",)),
        name="pack_rows_bf16",
    )(tab)


PACK_ROWS = 512


def _pack_kernel(x_ref, o_ref):
    u = pltpu.bitcast(x_ref[...], jnp.uint32)
    r = (u + jnp.uint32(0x7FFF) + ((u >> 16) & jnp.uint32(1))) >> 16
    for s in range(ROW_WORDS):
        lo = r[:, 2 * LANES * s:2 * LANES * s + LANES]
        hi = r[:, 2 * LANES * s + LANES:2 * LANES * (s + 1)]
        o_ref[pl.ds(s, PACK_ROWS, stride=ROW_WORDS), :] = lo | (hi << 16)


def _load_table_once(tab_hbm, tab_v, sem):
    @pl.when(pl.program_id(0) == 0)
    def _():
        cp = pltpu.make_async_copy(tab_hbm, tab_v, sem)
        cp.start()
        cp.wait()


def _tc_token_pipeline(idx_hbm, idx_s, isem, tab_v, per_token):
    step = pl.program_id(0)

    def idx_copy(block, slot):
        return pltpu.make_async_copy(idx_hbm.at[block], idx_s[slot], isem.at[slot])

    @pl.when(step == 0)
    def _():
        idx_copy(0, 0).start()

    for sub in range(TC_SUBS):
        block = step * TC_SUBS + sub
        idx_copy(block, sub).wait()
        if sub + 1 < TC_SUBS:
            idx_copy(block + 1, sub + 1).start()
        else:
            @pl.when(step + 1 < pl.num_programs(0))
            def _():
                idx_copy(block + 1, 0).start()
        offs = idx_s[sub]

        def chunks(i, offs=offs):
            per = PEER_E // TC_KCHUNKS
            for ch in range(TC_KCHUNKS):
                pieces = [tab_v[pl.ds(pl.multiple_of(offs.at[k][i], ROW_WORDS), ROW_WORDS), :]
                          for k in range(ch * per, (ch + 1) * per)]
                yield pltpu.bitcast(jnp.concatenate(pieces, axis=0), BF16)

        def trip(ip, carry, sub=sub, chunks=chunks):
            for u in range(TC_TRIP):
                i = ip * TC_TRIP + u
                per_token(sub * TC_SUB + i, chunks(i))
            return carry
        lax.fori_loop(0, TC_SUB // TC_TRIP, trip, 0)


def _tc_peer_scratch(tabp):
    return [pltpu.VMEM(tabp.shape, jnp.uint32),
            [pltpu.SMEM((PEER_E, TC_SUB), jnp.int32) for _ in range(TC_SUBS)],
            pltpu.SemaphoreType.DMA,
            pltpu.SemaphoreType.DMA((TC_SUBS,))]


def _stack_selectors():
    col = np.arange(STACK_COLS)
    diag = (col[None, :] % ROW_HALVES == np.arange(ROW_HALVES)[:, None]).astype(np.float32)
    owner = (col[:, None] // ROW_HALVES == np.arange(PEER_E)[None, :]).astype(np.float32)
    return jnp.asarray(diag), jnp.asarray(owner)


def _tc_down_kernel(idx_hbm, x_ref, tab_hbm, diag_ref, fold_ref, o_ref,
                    tab_v, idx_s, sem, isem, z_ref):
    _load_table_once(tab_hbm, tab_v, sem)
    cols = STACK_COLS // TC_KCHUNKS

    def per_token(t, chunks):
        xr = x_ref[pl.ds(t, 1), :]
        x8 = jnp.concatenate([xr[:, LANES * j:LANES * (j + 1)] for j in range(ROW_HALVES)],
                             axis=0).astype(BF16)
        parts = []
        for ch, rows in enumerate(chunks):
            a = lax.dot_general(x8, rows, (((1,), (1,)), ((), ())), preferred_element_type=F32)
            parts.append(jnp.sum(a * diag_ref[:, cols * ch:cols * (ch + 1)], axis=0, keepdims=True))
        z_ref[pl.ds(t, 1), :] = jnp.concatenate(parts, axis=1)

    _tc_token_pipeline(idx_hbm, idx_s, isem, tab_v, per_token)
    o_ref[...] = _dot_exact(z_ref[...], fold_ref[...])


def _peer_down_dots_tc(xn, offs, tabp, n_tok):
    diag, owner = _stack_selectors()
    full = lambda a: pl.BlockSpec(a.shape, lambda i: (0,) * a.ndim)
    return pl.pallas_call(
        _tc_down_kernel,
        grid=(n_tok // TC_STEP,),
        in_specs=[pl.BlockSpec(memory_space=pl.ANY),
                  pl.BlockSpec((TC_STEP, xn.shape[1]), lambda i: (i, 0)),
                  pl.BlockSpec(memory_space=pl.ANY), full(diag), full(owner)],
        out_specs=pl.BlockSpec((TC_STEP, PEER_E), lambda i: (i, 0)),
        out_shape=jax.ShapeDtypeStruct((n_tok, PEER_E), F32),
        scratch_shapes=_tc_peer_scratch(tabp) + [pltpu.VMEM((TC_STEP, STACK_COLS), F32)],
        compiler_params=pltpu.CompilerParams(
            dimension_semantics=("arbitrary",), vmem_limit_bytes=VMEM_LIMIT),
        name="peer_down_dots_tc",
    )(offs, xn, tabp, diag, owner)


def _tc_up_kernel(idx_hbm, coef_ref, tab_hbm, spread_ref, diag_ref, o_ref,
                  tab_v, idx_s, sem, isem, ce_ref):
    _load_table_once(tab_hbm, tab_v, sem)
    c = coef_ref[...]
    c_hi = c.astype(BF16)
    c_lo = (c - c_hi.astype(F32)).astype(BF16)
    ce_ref[0] = jnp.dot(c_hi, spread_ref[...], preferred_element_type=F32)
    ce_ref[1] = jnp.dot(c_lo, spread_ref[...], preferred_element_type=F32)

    cols = STACK_COLS // TC_KCHUNKS

    def per_token(t, chunks):
        d = diag_ref[...]
        lhs = jnp.concatenate([(ce_ref[0, pl.ds(t, 1), :] * d).astype(BF16),
                               (ce_ref[1, pl.ds(t, 1), :] * d).astype(BF16)], axis=0)
        r = jnp.zeros((2 * ROW_HALVES, LANES), F32)
        for ch, rows in enumerate(chunks):
            r = r + jnp.dot(lhs[:, cols * ch:cols * (ch + 1)], rows, preferred_element_type=F32)
        out8 = r[0:ROW_HALVES] + r[ROW_HALVES:2 * ROW_HALVES]
        o_ref[pl.ds(t, 1), :] = jnp.concatenate(
            [out8[j:j + 1, :] for j in range(ROW_HALVES)], axis=1)

    _tc_token_pipeline(idx_hbm, idx_s, isem, tab_v, per_token)


def _peer_up_sum_tc(coef, offs, tabp, n_tok):
    d = ROW_HALVES * LANES
    diag, owner = _stack_selectors()
    spread = owner.T.astype(BF16)
    full = lambda a: pl.BlockSpec(a.shape, lambda i: (0,) * a.ndim)
    return pl.pallas_call(
        _tc_up_kernel,
        grid=(n_tok // TC_STEP,),
        in_specs=[pl.BlockSpec(memory_space=pl.ANY),
                  pl.BlockSpec((TC_STEP, PEER_E), lambda i: (i, 0)),
                  pl.BlockSpec(memory_space=pl.ANY), full(spread), full(diag)],
        out_specs=pl.BlockSpec((TC_STEP, d), lambda i: (i, 0)),
        out_shape=jax.ShapeDtypeStruct((n_tok, d), F32),
        scratch_shapes=_tc_peer_scratch(tabp) + [pltpu.VMEM((2, TC_STEP, STACK_COLS), F32)],
        compiler_params=pltpu.CompilerParams(
            dimension_semantics=("arbitrary",), vmem_limit_bytes=VMEM_LIMIT),
        name="peer_up_sum_tc",
    )(offs, coef, tabp, spread, diag)


def _coef_kernel(pre_ref, gate_ref, o_ref):
    x = pre_ref[...]
    gelu = 0.5 * x * (1.0 + lax.erf(x * (2.0 ** -0.5)))
    o_ref[...] = gate_ref[...] * gelu


def _peer_coef(pre, gates):
    t, e = pre.shape
    row = pl.BlockSpec((TM_FINAL, e), lambda i: (i, 0))
    return pl.pallas_call(
        _coef_kernel,
        grid=(t // TM_FINAL,),
        in_specs=[row, row],
        out_specs=row,
        out_shape=jax.ShapeDtypeStruct((t, e), F32),
        compiler_params=pltpu.CompilerParams(dimension_semantics=("arbitrary",)),
        name="peer_gate_gelu",
    )(pre, gates)


def _front_half(x2, row0, rows, bsz, seq, w_pad, layer, i):
    (norm1_g, ret_norm_g, m_conv_w, m_conv_b, m_wq, m_wk, m_b_i, m_b_f, m_norm_g, m_skip,
     w_out, norm2_g, peer_wq, keys) = layer
    d = x2.shape[1]
    proj = _rms_proj(x2, row0, rows, norm1_g[i].reshape(1, d), w_pad)
    y = _mixer(proj, bsz, seq, ret_norm_g[i], m_conv_w[i], m_conv_b[i], m_wq[i], m_wk[i],
               m_b_i[i], m_b_f[i], m_norm_g[i], m_skip[i])
    x1, xn, sub_t = _post(y, x2, row0, w_out, norm2_g[i].reshape(1, d), peer_wq, keys)
    return (x1, xn) + tuple(_peer_route(sub_t))


def kernel(x, norm1_g, w_in, ret_norm_g, m_conv_w, m_conv_b, m_wq, m_wk, m_b_i, m_b_f,
           m_norm_g, m_skip, w_out, norm2_g, peer_wq, peer_keys, peer_down, peer_up, final_g):
    b, s, d = x.shape
    t = b * s
    tp = t // FRONT_PARTS
    x2 = x.reshape(t, d)
    depth = norm1_g.shape[0]
    for i in range(depth):
        if i > 0:
            x2 = jnp.concatenate(x1s, axis=0) + jnp.concatenate([p_tc, p_sc], axis=0)
        wi = w_in[i]
        n_main = OFF_GI
        gate_pad = lambda w: jnp.pad(w, ((0, 0), (0, LANES - w.shape[1])))
        w_pad = jnp.concatenate(
            [wi[:, :n_main], gate_pad(wi[:, n_main:n_main + MLSTM_HEADS]),
             gate_pad(wi[:, n_main + MLSTM_HEADS:])], axis=1).astype(BF16)
        keys = peer_keys[i].reshape(2 * PEER_HEADS, PEER_NKEYS, PEER_HALF).astype(BF16)
        layer = (norm1_g, ret_norm_g, m_conv_w, m_conv_b, m_wq, m_wk, m_b_i, m_b_f, m_norm_g,
                 m_skip, w_out[i].astype(BF16), norm2_g, peer_wq[i].astype(BF16), keys)
        down_p = _pack_rows_bf16(peer_down[i])
        up_p = _pack_rows_bf16(peer_up[i])

        fronts = []
        pre = [None] * FRONT_PARTS
        for q in range(FRONT_PARTS):
            fr = _front_half(x2, q * tp, tp, b // FRONT_PARTS, s, w_pad, layer, i)
            fronts.append(fr)
            if q < SC_DOWN_PARTS:
                pre[q] = [_peer_down_dots(fr[1], fr[2], peer_down[i], 0)]
            elif q == SC_DOWN_PARTS and SC_DOWN_EXTRA:
                pre[q] = [None, _peer_down_dots(fr[1], fr[2], peer_down[i], tp - SC_DOWN_EXTRA)]
        for q in range(SC_DOWN_PARTS, FRONT_PARTS):
            n_tc_q = tp - SC_DOWN_EXTRA if q == SC_DOWN_PARTS else tp
            pre_tc = _peer_down_dots_tc(fronts[q][1], fronts[q][4], down_p, n_tc_q)
            if pre[q] is None:
                pre[q] = [pre_tc]
            else:
                pre[q][0] = pre_tc
        pre = [piece for part in pre for piece in part]
        x1s = [fr[0] for fr in fronts]
        expert = jnp.concatenate([fr[2] for fr in fronts], axis=0)
        gates = jnp.concatenate([fr[3] for fr in fronts], axis=0)
        offs = jnp.concatenate([fr[4] for fr in fronts], axis=0)
        coef = _peer_coef(jnp.concatenate(pre, axis=0), gates)
        n_tc = PEER_TC_UP_TOKENS
        p_sc = _peer_up_sum(coef, expert, peer_up[i], n_tc)
        p_tc = _peer_up_sum_tc(coef, offs, up_p, n_tc)
    out = _final(_row_ranges(x1s, [p_tc, p_sc]), final_g.reshape(1, d), t)
    return out.reshape(b, s, d)


def _row_ranges(xs, ps):
    def starts(arrs):
        out = [0]
        for a in arrs:
            out.append(out[-1] + a.shape[0])
        return out
    xb, pb = starts(xs), starts(ps)
    assert xb[-1] == pb[-1]
    cuts = sorted(set(xb) | set(pb))
    parts = []
    for lo, hi in zip(cuts[:-1], cuts[1:]):
        xi = max(k for k in range(len(xs)) if xb[k] <= lo)
        pi = max(k for k in range(len(ps)) if pb[k] <= lo)
        parts.append((xs[xi], lo - xb[xi], ps[pi], lo - pb[pi], hi - lo))
    return parts
```

```python
import functools

import numpy as np
import jax
import jax.numpy as jnp
from jax import lax
from jax.experimental import pallas as pl
from jax.experimental.pallas import tpu as pltpu
from jax.experimental.pallas import tpu_sc as plsc

F32 = jnp.float32
BF16 = jnp.bfloat16

D_MODEL = 1024
CHUNK = 64
RET_HEADS = 4
RET_DK = 128
MLSTM_HEADS = 4
MLSTM_D = 128
HEAD_W = 128
D_GROUP = 512
CONV_W = 4
ROPE_BASE = 10000.0
PEER_HEADS = 8
PEER_NKEYS = 128
PEER_TOPK = 16
PEER_HALF = 128
PEER_BLOCK = 128
EPS = 1e-6

OFF_RQ, OFF_RK, OFF_RV, OFF_RG = 0, 512, 1024, 1536
OFF_MX, OFF_MV, OFF_MO = 2048, 2560, 3072
OFF_GI, OFF_GF = 3584, 3712
D_PROJ_PAD = 3840

LANES = 128
CONV_TAIL = 8
VMEM_LIMIT = 56 * 1024 * 1024

TM_PROJ = 256
L_BLOCK = 256
TM_FINAL = 512
TM_COEF = 2048


def _rms(x, g):
    ms = jnp.mean(x * x, axis=-1, keepdims=True)
    return x * lax.rsqrt(ms + EPS) * g


def _group_norm(h, g):
    mu = jnp.mean(h, axis=-1, keepdims=True)
    d = h - mu
    var = jnp.mean(d * d, axis=-1, keepdims=True)
    return d * lax.rsqrt(var + EPS) * g


def _silu(x):
    return x * (1.0 / (1.0 + jnp.exp(-x)))


def _sigmoid(x):
    return 1.0 / (1.0 + jnp.exp(-x))


def _dot(a, b):
    return jnp.dot(a.astype(BF16), b.astype(BF16), preferred_element_type=F32)


def _dot_tn(a, b):
    return lax.dot_general(a.astype(BF16), b.astype(BF16), (((0,), (0,)), ((), ())),
                           preferred_element_type=F32)


def _dot_nt(a, b):
    return lax.dot_general(a.astype(BF16), b.astype(BF16), (((1,), (1,)), ((), ())),
                           preferred_element_type=F32)


def _rms_proj_kernel(x_ref, g_ref, w_ref, o_ref):
    h = _rms(x_ref[...], g_ref[...])
    o_ref[...] = jnp.dot(h.astype(BF16), w_ref[...], preferred_element_type=F32)


def _rms_proj(x2, row0, rows, g, w):
    d = x2.shape[1]
    t = rows
    n = w.shape[1]
    b0 = row0 // TM_PROJ
    return pl.pallas_call(
        _rms_proj_kernel,
        grid=(t // TM_PROJ,),
        in_specs=[
            pl.BlockSpec((TM_PROJ, d), lambda i: (i + b0, 0)),
            pl.BlockSpec((1, d), lambda i: (0, 0)),
            pl.BlockSpec((d, n), lambda i: (0, 0)),
        ],
        out_specs=pl.BlockSpec((TM_PROJ, n), lambda i: (i, 0)),
        out_shape=jax.ShapeDtypeStruct((t, n), F32),
        compiler_params=pltpu.CompilerParams(
            dimension_semantics=("arbitrary",), vmem_limit_bytes=VMEM_LIMIT),
        name="rms_in_proj",
    )(x2, g, w)


def _mixer_kernel(chunk_decay,
                  proj_ref, cos_ref, sin_ref, intra_ref, qd_ref, kd_ref, tri_ref,
                  rng_ref, cw_ref, cb_ref, wq_ref, wk_ref, bi_ref, bf_ref,
                  mng_ref, skip_ref,
                  y_ref,
                  rstate_ref, cstate_ref, m_ref, tail_ref):
    j = pl.program_id(1)
    n_chunks = L_BLOCK // CHUNK

    @pl.when(j == 0)
    def _():
        rstate_ref[...] = jnp.zeros_like(rstate_ref)
        cstate_ref[...] = jnp.zeros_like(cstate_ref)
        m_ref[...] = jnp.zeros_like(m_ref)
        tail_ref[...] = jnp.zeros_like(tail_ref)

    cosv = cos_ref[...]
    sinv = sin_ref[...]
    k_scale = RET_DK ** -0.5

    for h in range(RET_HEADS):
        lo = h * HEAD_W
        q = proj_ref[:, OFF_RQ + lo:OFF_RQ + lo + HEAD_W]
        k = proj_ref[:, OFF_RK + lo:OFF_RK + lo + HEAD_W]
        q = q * cosv + pltpu.roll(q, HEAD_W // 2, 1) * sinv
        k = (k * cosv + pltpu.roll(k, HEAD_W // 2, 1) * sinv) * k_scale
        intra = intra_ref[h]
        qd = qd_ref[h]
        kd = kd_ref[h]
        g = rng_ref[:, lo:lo + HEAD_W]
        for c in range(n_chunks):
            r0 = c * CHUNK
            qc = q[r0:r0 + CHUNK]
            kc = k[r0:r0 + CHUNK]
            vc = proj_ref[r0:r0 + CHUNK, OFF_RV + lo:OFF_RV + lo + HEAD_W]
            gate = proj_ref[r0:r0 + CHUNK, OFF_RG + lo:OFF_RG + lo + HEAD_W]
            state = rstate_ref[h]
            scores = _dot_nt(qc, kc) * intra
            o = _dot(scores, vc) + _dot(qc, state) * qd
            rstate_ref[h] = chunk_decay[h] * state + _dot_tn(kc * kd, vc)
            y_ref[r0:r0 + CHUNK, lo:lo + HEAD_W] = _silu(gate) * _group_norm(o, g)

    mx = proj_ref[:, OFF_MX:OFF_MX + D_GROUP]
    xp = jnp.concatenate([tail_ref[...], mx], axis=0)
    base = CONV_TAIL - (CONV_W - 1)
    conv = cb_ref[...] + xp[base:base + L_BLOCK] * cw_ref[0:1, :]
    for t in range(1, CONV_W):
        conv = conv + xp[base + t:base + t + L_BLOCK] * cw_ref[t:t + 1, :]
    tail_ref[...] = mx[L_BLOCK - CONV_TAIL:L_BLOCK]
    cact = _silu(conv)

    gi = proj_ref[:, OFF_GI:OFF_GI + LANES] + bi_ref[...]
    gf = proj_ref[:, OFF_GF:OFF_GF + LANES] + bf_ref[...]
    logf = jnp.minimum(gf, 0.0) - jnp.log1p(jnp.exp(-jnp.abs(gf)))
    tri = tri_ref[...]
    lane = lax.broadcasted_iota(jnp.int32, (CHUNK, HEAD_W), 1)
    ones_col = (lane == 0).astype(F32)
    m_scale = MLSTM_D ** -0.5

    mq = []
    mk = []
    for h in range(MLSTM_HEADS):
        lo = h * HEAD_W
        ch = cact[:, lo:lo + HEAD_W]
        mq.append(_dot(ch, wq_ref[h]))
        mk.append(_dot(ch, wk_ref[h]) * m_scale)

    for c in range(n_chunks):
        r0 = c * CHUNK
        fcum = jnp.dot(tri, logf[r0:r0 + CHUNK], preferred_element_type=F32,
                       precision=lax.Precision.HIGHEST)
        a = fcum[CHUNK - 1:CHUNK]
        logw = a - fcum + gi[r0:r0 + CHUNK]
        m_in = jnp.max(logw, axis=0, keepdims=True)
        m_old = m_ref[...]
        m_new = jnp.maximum(a + m_old, m_in)
        decay = jnp.exp(a + m_old - m_new)
        w = jnp.exp(logw - m_new)
        em = jnp.exp(-m_new)
        m_ref[...] = m_new
        for h in range(MLSTM_HEADS):
            lo = h * HEAD_W
            qh = mq[h][r0:r0 + CHUNK]
            kh = mk[h][r0:r0 + CHUNK]
            vh = proj_ref[r0:r0 + CHUNK, OFF_MV + lo:OFF_MV + lo + HEAD_W]
            oh = proj_ref[r0:r0 + CHUNK, OFF_MO + lo:OFF_MO + lo + HEAD_W]
            ch = cact[r0:r0 + CHUNK, lo:lo + HEAD_W]
            v_ext = jnp.concatenate([vh, ones_col], axis=1)
            cmem = decay[:, h:h + 1] * cstate_ref[h] + _dot_tn(kh * w[:, h:h + 1], v_ext)
            cstate_ref[h] = cmem
            num = _dot(qh, cmem)
            den = jnp.maximum(jnp.abs(num[:, HEAD_W:HEAD_W + 1]), em[:, h:h + 1])
            hh = num[:, :HEAD_W] / den
            hm = _group_norm(hh, mng_ref[:, lo:lo + HEAD_W])
            ym = _sigmoid(oh) * (hm + skip_ref[:, lo:lo + HEAD_W] * ch)
            y_ref[r0:r0 + CHUNK, D_GROUP + lo:D_GROUP + lo + HEAD_W] = ym


def _retention_tables():
    h = np.arange(RET_HEADS, dtype=np.float64)
    log_g = np.log(1.0 - 2.0 ** (-5.0 - h))
    l = np.arange(CHUNK, dtype=np.float64)
    intra = np.exp(log_g[:, None, None] * np.abs(l[:, None] - l[None, :]))
    qd = np.exp(log_g[:, None] * (l + 1.0))[:, :, None] * np.ones((1, 1, HEAD_W))
    kd = np.exp(log_g[:, None] * (CHUNK - 1.0 - l))[:, :, None] * np.ones((1, 1, HEAD_W))
    cd = np.exp(log_g * CHUNK)
    return (jnp.asarray(intra, F32), jnp.asarray(qd, F32), jnp.asarray(kd, F32),
            tuple(float(v) for v in cd))


def _rotary_tables(s):
    half = HEAD_W // 2
    inv = ROPE_BASE ** (-np.arange(half, dtype=np.float64) / half)
    ang = np.arange(s, dtype=np.float64)[:, None] * inv[None, :]
    cos = np.cos(ang)
    sin = np.sin(ang)
    return (jnp.asarray(np.concatenate([cos, cos], axis=-1), F32),
            jnp.asarray(np.concatenate([-sin, sin], axis=-1), F32))


def _mixer(proj, b, s, ret_norm_g, m_conv_w, m_conv_b, m_wq, m_wk, m_b_i, m_b_f,
           m_norm_g, m_skip):
    intra, qd, kd, cd = _retention_tables()
    cos_t, sin_t = _rotary_tables(s)
    tri = jnp.asarray(np.tril(np.ones((CHUNK, CHUNK))), F32)
    nj = s // L_BLOCK
    pad = lambda v: jnp.pad(v.reshape(1, -1), ((0, 0), (0, LANES - v.size)))
    full = lambda shape: pl.BlockSpec(shape, lambda bi, j: (0,) * len(shape))
    return pl.pallas_call(
        functools.partial(_mixer_kernel, cd),
        grid=(b, nj),
        in_specs=[
            pl.BlockSpec((L_BLOCK, D_PROJ_PAD), lambda bi, j: (bi * nj + j, 0)),
            pl.BlockSpec((L_BLOCK, HEAD_W), lambda bi, j: (j, 0)),
            pl.BlockSpec((L_BLOCK, HEAD_W), lambda bi, j: (j, 0)),
            full((RET_HEADS, CHUNK, CHUNK)),
            full((RET_HEADS, CHUNK, HEAD_W)),
            full((RET_HEADS, CHUNK, HEAD_W)),
            full((CHUNK, CHUNK)),
            full((1, D_GROUP)),
            full((CONV_W, D_GROUP)),
            full((1, D_GROUP)),
            full((MLSTM_HEADS, MLSTM_D, MLSTM_D)),
            full((MLSTM_HEADS, MLSTM_D, MLSTM_D)),
            full((1, LANES)),
            full((1, LANES)),
            full((1, D_GROUP)),
            full((1, D_GROUP)),
        ],
        out_specs=pl.BlockSpec((L_BLOCK, 2 * D_GROUP), lambda bi, j: (bi * nj + j, 0)),
        out_shape=jax.ShapeDtypeStruct((b * s, 2 * D_GROUP), F32),
        scratch_shapes=[
            pltpu.VMEM((RET_HEADS, RET_DK, HEAD_W), F32),
            pltpu.VMEM((MLSTM_HEADS, MLSTM_D, 2 * HEAD_W), F32),
            pltpu.VMEM((1, LANES), F32),
            pltpu.VMEM((CONV_TAIL, D_GROUP), F32),
        ],
        compiler_params=pltpu.CompilerParams(
            dimension_semantics=("arbitrary", "arbitrary"), vmem_limit_bytes=VMEM_LIMIT),
        name="retention_mlstm_mixer",
    )(proj, cos_t, sin_t, intra, qd, kd, tri,
      ret_norm_g.reshape(1, D_GROUP), m_conv_w, m_conv_b.reshape(1, D_GROUP),
      m_wq.astype(BF16), m_wk.astype(BF16), pad(m_b_i), pad(m_b_f),
      m_norm_g.reshape(1, D_GROUP), m_skip.reshape(1, D_GROUP))


def _post_kernel(y_ref, x_ref, wout_ref, g2_ref, wq_ref, keys_ref,
                 x1_ref, xn_ref, sub_ref):
    x1 = x_ref[...] + jnp.dot(y_ref[...].astype(BF16), wout_ref[...],
                              preferred_element_type=F32)
    x1_ref[...] = x1
    xn = _rms(x1, g2_ref[...])
    xn_ref[...] = xn
    q = jnp.dot(xn.astype(BF16), wq_ref[...], preferred_element_type=F32)
    for i in range(2 * PEER_HEADS):
        lo = i * PEER_HALF
        sub_ref[lo:lo + PEER_NKEYS, :] = _dot_nt(keys_ref[i], q[:, lo:lo + PEER_HALF])


def _post(y, x2, row0, w_out, norm2_g, peer_wq, keys):
    t, d = y.shape
    nq = peer_wq.shape[1]
    b0 = row0 // TM_PROJ
    row = lambda w: pl.BlockSpec((TM_PROJ, w), lambda i: (i, 0))
    full = lambda shape: pl.BlockSpec(shape, lambda i: (0,) * len(shape))
    return pl.pallas_call(
        _post_kernel,
        grid=(t // TM_PROJ,),
        in_specs=[row(d), pl.BlockSpec((TM_PROJ, d), lambda i: (i + b0, 0)), full((d, d)),
                  full((1, d)), full((d, nq)), full(keys.shape)],
        out_specs=[row(d), row(d), pl.BlockSpec((nq, TM_PROJ), lambda i: (0, i))],
        out_shape=[jax.ShapeDtypeStruct((t, d), F32), jax.ShapeDtypeStruct((t, d), F32),
                   jax.ShapeDtypeStruct((nq, t), F32)],
        compiler_params=pltpu.CompilerParams(
            dimension_semantics=("arbitrary",), vmem_limit_bytes=VMEM_LIMIT),
        name="out_proj_peer_scores",
    )(y, x2, w_out, norm2_g, peer_wq, keys)


def _final_kernel(x1_ref, p_ref, g_ref, *rest):
    o_ref = rest[-1]
    o_ref[...] = _rms(x1_ref[...] + p_ref[...], g_ref[...])


def _final(parts, g, t):
    d = g.shape[1]
    spec = lambda r0: pl.BlockSpec((TM_FINAL, d), lambda i, b0=r0 // TM_FINAL: (i + b0, 0))
    out = None
    row0 = 0
    for x1, x_row0, p, p_row0, rows in parts:
        in_specs = [spec(x_row0), spec(p_row0), pl.BlockSpec((1, d), lambda i: (0, 0))]
        args = [x1, p, g]
        if out is not None:
            in_specs.append(pl.BlockSpec(memory_space=pl.ANY))
            args.append(out)
        out = pl.pallas_call(
            _final_kernel,
            grid=(rows // TM_FINAL,),
            in_specs=in_specs,
            out_specs=spec(row0),
            out_shape=jax.ShapeDtypeStruct((t, d), F32),
            input_output_aliases={} if len(args) == 3 else {3: 0},
            compiler_params=pltpu.CompilerParams(dimension_semantics=("arbitrary",)),
            name="final_rmsnorm",
        )(*args)
        row0 += rows
    assert row0 == t
    return out


TM_ROUTE = 256
CAND_ROWS = 56


def _candidate_tables():
    pairs = [(k1, k2) for k1 in range(PEER_TOPK) for k2 in range(PEER_TOPK)
             if (k1 + 1) * (k2 + 1) <= PEER_TOPK]
    assert len(pairs) <= CAND_ROWS
    rep = np.zeros((CAND_ROWS, 2 * PEER_TOPK), np.float32)
    til = np.zeros((CAND_ROWS, 2 * PEER_TOPK), np.float32)
    for pos, (k1, k2) in enumerate(pairs):
        rep[pos, k1] = 1.0
        til[pos, PEER_TOPK + k2] = 1.0
    return jnp.asarray(rep), jnp.asarray(til), len(pairs)


def _dot_exact(a, b):
    return jnp.dot(a, b, preferred_element_type=F32, precision=lax.Precision.HIGHEST)


def _route_kernel(n_cand, sub_ref, rep_ref, til_ref, seg_ref, exp_ref, gate_ref, offs_ref,
                  s_ref, i_ref, best_ref, eid_ref, m0_ref):
    tm = sub_ref.shape[1]
    neg = -jnp.inf
    key_id = lax.broadcasted_iota(jnp.int32, (PEER_NKEYS, tm), 0).astype(F32)
    cand_id = lax.broadcasted_iota(jnp.int32, (CAND_ROWS, tm), 0).astype(F32)
    rep = rep_ref[...]
    til = til_ref[...]
    for h in range(PEER_HEADS):
        for p in range(2):
            lo = (2 * h + p) * PEER_NKEYS
            x = sub_ref[lo:lo + PEER_NKEYS, :]
            for k in range(PEER_TOPK):
                m = jnp.max(x, axis=0, keepdims=True)
                idx = jnp.min(jnp.where(x == m, key_id, float(PEER_NKEYS)), axis=0, keepdims=True)
                x = jnp.where(key_id == idx, neg, x)
                r = p * PEER_TOPK + k
                s_ref[r:r + 1, :] = m
                i_ref[r:r + 1, :] = idx
        s = s_ref[...]
        iv = i_ref[...]
        cand = _dot_exact(rep, s) + _dot_exact(til, s)
        cand = jnp.where(cand_id < float(n_cand), cand, neg)
        eid = _dot(rep, iv) * float(PEER_NKEYS) + _dot(til, iv)
        for k in range(PEER_TOPK):
            m = jnp.max(cand, axis=0, keepdims=True)
            pos = jnp.min(jnp.where(cand == m, cand_id, float(CAND_ROWS)), axis=0, keepdims=True)
            sel = cand_id == pos
            e = jnp.max(jnp.where(sel, eid, -1.0), axis=0, keepdims=True)
            cand = jnp.where(sel, neg, cand)
            r = h * PEER_TOPK + k
            best_ref[r:r + 1, :] = m
            eid_ref[r:r + 1, :] = e
            if k == 0:
                m0_ref[r:r + PEER_TOPK, :] = jnp.broadcast_to(m, (PEER_TOPK, tm))
    pexp = jnp.exp(best_ref[...] - m0_ref[...])
    gates = pexp / _dot_exact(seg_ref[...], pexp)
    gate_ref[...] = gates.T
    eid = eid_ref[...].astype(jnp.int32)
    exp_ref[...] = eid.T
    for blk in range(tm // LANES):
        offs_ref[blk] = eid[:, blk * LANES:(blk + 1) * LANES] * ROW_WORDS


def _peer_route(sub_t):
    nq, t = sub_t.shape
    rep, til, n_cand = _candidate_tables()
    e = PEER_HEADS * PEER_TOPK
    seg = jnp.asarray(np.kron(np.eye(PEER_HEADS), np.ones((PEER_TOPK, PEER_TOPK))), F32)
    full = lambda a: pl.BlockSpec(a.shape, lambda i: (0,) * a.ndim)
    out = pl.BlockSpec((TM_ROUTE, e), lambda i: (i, 0))
    return pl.pallas_call(
        functools.partial(_route_kernel, n_cand),
        grid=(t // TM_ROUTE,),
        in_specs=[pl.BlockSpec((nq, TM_ROUTE), lambda i: (0, i)), full(rep), full(til), full(seg)],
        out_specs=[out, out, pl.BlockSpec((TM_ROUTE // LANES, e, LANES), lambda i: (i, 0, 0))],
        out_shape=[jax.ShapeDtypeStruct((t, e), jnp.int32), jax.ShapeDtypeStruct((t, e), F32),
                   jax.ShapeDtypeStruct((t // LANES, e, LANES), jnp.int32)],
        scratch_shapes=[
            pltpu.VMEM((2 * PEER_TOPK, TM_ROUTE), F32),
            pltpu.VMEM((2 * PEER_TOPK, TM_ROUTE), F32),
            pltpu.VMEM((e, TM_ROUTE), F32),
            pltpu.VMEM((e, TM_ROUTE), F32),
            pltpu.VMEM((e, TM_ROUTE), F32),
        ],
        compiler_params=pltpu.CompilerParams(dimension_semantics=("arbitrary",)),
        name="peer_topk_route",
    )(sub_t, rep, til, seg)


SC_CORES = 2
SC_SUBCORES = 16
SC_LANES = 16
SC_WORKERS = SC_CORES * SC_SUBCORES
PEER_E = PEER_HEADS * PEER_TOPK
SC_ROWS = 32
SC_GATHERS = PEER_E // SC_ROWS
SC_TOK_BLOCK = 8
SC_QUARTER = 256


def _sc_mesh():
    return plsc.VectorSubcoreMesh(core_axis_name="c", subcore_axis_name="s")


def _sc_pipeline(idx_v, tab_hbm, rows_v, sems, compute):
    def gather(i, c, buf):
        window = idx_v.at[i, pl.ds(c * SC_ROWS, SC_ROWS)]
        return pltpu.make_async_copy(tab_hbm.at[window], rows_v.at[buf], sems.at[buf])

    gather(0, 0, 0).start()

    @pl.loop(0, SC_TOK_BLOCK)
    def _(i):
        for c in range(SC_GATHERS):
            buf = c % 2
            gather(i, c, buf).wait()
            if c + 1 < SC_GATHERS:
                gather(i, c + 1, 1 - buf).start()
            else:
                @pl.when(i + 1 < SC_TOK_BLOCK)
                def _():
                    gather(i + 1, 0, 1 - buf).start()
            compute(i, c, i * SC_GATHERS + c, buf)


def _peer_down_dots(xn, idx, down, tok_start):
    t_all, d = xn.shape
    t = t_all - tok_start
    tok_w = t // SC_WORKERS
    nblk = tok_w // SC_TOK_BLOCK
    assert nblk * SC_TOK_BLOCK * SC_WORKERS == t
    lanes = SC_LANES

    @functools.partial(
        pl.kernel, mesh=_sc_mesh(),
        out_type=jax.ShapeDtypeStruct((t, PEER_E), F32),
        scratch_types=[
            pltpu.VMEM((SC_TOK_BLOCK, PEER_E), jnp.int32),
            pltpu.VMEM((SC_TOK_BLOCK, d), F32),
            pltpu.VMEM((2, SC_ROWS, d), F32),
            pltpu.VMEM((SC_ROWS, lanes), F32),
            pltpu.VMEM((SC_TOK_BLOCK, PEER_E), F32),
            pltpu.SemaphoreType.DMA((2,)),
        ],
        compiler_params=pltpu.CompilerParams(needs_layout_passes=False),
        name="peer_down_dots",
    )
    def k(x_hbm, idx_hbm, tab_hbm, out_hbm, idx_v, x_v, rows_v, acc_v, out_v, sems):
        wid = lax.axis_index("s") * SC_CORES + lax.axis_index("c")
        lane = lax.iota(jnp.int32, lanes)

        @pl.loop(0, nblk)
        def _(blk):
            out0 = wid * tok_w + blk * SC_TOK_BLOCK
            tok0 = tok_start + out0
            pltpu.sync_copy(idx_hbm.at[pl.ds(tok0, SC_TOK_BLOCK)], idx_v)
            pltpu.sync_copy(x_hbm.at[pl.ds(tok0, SC_TOK_BLOCK)], x_v)

            def compute(i, c, step, buf):
                def body(dc, accs):
                    xv = x_v[i, pl.ds(dc * lanes, lanes)]
                    return tuple(a + rows_v[buf, r, pl.ds(dc * lanes, lanes)] * xv
                                 for r, a in enumerate(accs))
                accs = lax.fori_loop(0, d // lanes, body,
                                     tuple(jnp.zeros((lanes,), F32) for _ in range(SC_ROWS)))
                for r in range(SC_ROWS):
                    acc_v[r, :] = accs[r]
                for g in range(SC_ROWS // lanes):
                    tot = jnp.zeros((lanes,), F32)
                    for l in range(lanes):
                        tot = tot + plsc.load_gather(
                            acc_v, [lane + g * lanes, jnp.full((lanes,), l, jnp.int32)])
                    out_v[i, pl.ds(c * SC_ROWS + g * lanes, lanes)] = tot

            _sc_pipeline(idx_v, tab_hbm, rows_v, sems, compute)
            pltpu.sync_copy(out_v, out_hbm.at[pl.ds(out0, SC_TOK_BLOCK)])

    return k(xn, idx, down)


def _peer_up_sum(coef, idx, up, tok_start):
    t_all = coef.shape[0]
    t = t_all - tok_start
    d = up.shape[1]
    tok_w = t // SC_WORKERS
    nblk = tok_w // SC_TOK_BLOCK
    assert nblk * SC_TOK_BLOCK * SC_WORKERS == t
    lanes = SC_LANES
    nj = SC_QUARTER // lanes

    @functools.partial(
        pl.kernel, mesh=_sc_mesh(),
        out_type=jax.ShapeDtypeStruct((t, d), F32),
        scratch_types=[
            pltpu.VMEM((SC_TOK_BLOCK, PEER_E), jnp.int32),
            pltpu.VMEM((SC_TOK_BLOCK * PEER_E,), F32),
            pltpu.VMEM((2, SC_ROWS, d), F32),
            pltpu.VMEM((SC_TOK_BLOCK, d), F32),
            pltpu.SemaphoreType.DMA((2,)),
        ],
        compiler_params=pltpu.CompilerParams(needs_layout_passes=False),
        name="peer_up_sum",
    )
    def k(coef_hbm, idx_hbm, tab_hbm, out_hbm, idx_v, coef_v, rows_v, out_v, sems):
        wid = lax.axis_index("s") * SC_CORES + lax.axis_index("c")

        @pl.loop(0, nblk)
        def _(blk):
            out0 = wid * tok_w + blk * SC_TOK_BLOCK
            tok0 = tok_start + out0
            pltpu.sync_copy(idx_hbm.at[pl.ds(tok0, SC_TOK_BLOCK)], idx_v)
            pltpu.sync_copy(coef_hbm.at[pl.ds(tok0 * PEER_E, SC_TOK_BLOCK * PEER_E)], coef_v)

            def compute(i, c, step, buf):
                for q in range(d // SC_QUARTER):
                    col = lambda j: pl.ds(q * SC_QUARTER + j * lanes, lanes)
                    if c == 0:
                        init = tuple(jnp.zeros((lanes,), F32) for _ in range(nj))
                    else:
                        init = tuple(out_v[i, col(j)] for j in range(nj))

                    def body(r, accs):
                        cf = plsc.load_gather(
                            coef_v, [jnp.full((lanes,), step * SC_ROWS + r, jnp.int32)])
                        return tuple(a + rows_v[buf, r, col(j)] * cf for j, a in enumerate(accs))
                    accs = lax.fori_loop(0, SC_ROWS, body, init)
                    for j in range(nj):
                        out_v[i, col(j)] = accs[j]

            _sc_pipeline(idx_v, tab_hbm, rows_v, sems, compute)
            pltpu.sync_copy(out_v, out_hbm.at[pl.ds(out0, SC_TOK_BLOCK)])

    return k(coef.reshape(t_all * PEER_E), idx, up)


TC_SUB = 128
TC_SUBS = 2
TC_STEP = TC_SUB * TC_SUBS
TC_TRIP = 64
TC_KCHUNKS = 4
ROW_WORDS = 4
ROW_HALVES = 2 * ROW_WORDS
STACK_COLS = PEER_E * ROW_HALVES
FRONT_PARTS = 4
SC_DOWN_PARTS = 1
SC_DOWN_EXTRA = 6656
PEER_TC_UP_TOKENS = 25088


def _pack_rows_bf16(tab):
    n, d = tab.shape
    return pl.pallas_call(
        _pack_kernel,
        grid=(n // PACK_ROWS,),
        in_specs=[pl.BlockSpec((PACK_ROWS, d), lambda i: (i, 0))],
        out_specs=pl.BlockSpec((PACK_ROWS * ROW_WORDS, LANES), lambda i: (i, 0)),
        out_shape=jax.ShapeDtypeStruct((n * ROW_WORDS, LANES), jnp.uint32),
        compiler_params=pltpu.CompilerParams(dimension_semantics=("arbitrary",)),
        name="pack_rows_bf16",
    )(tab)


PACK_ROWS = 512


def _pack_kernel(x_ref, o_ref):
    u = pltpu.bitcast(x_ref[...], jnp.uint32)
    r = (u + jnp.uint32(0x7FFF) + ((u >> 16) & jnp.uint32(1))) >> 16
    for s in range(ROW_WORDS):
        lo = r[:, 2 * LANES * s:2 * LANES * s + LANES]
        hi = r[:, 2 * LANES * s + LANES:2 * LANES * (s + 1)]
        o_ref[pl.ds(s, PACK_ROWS, stride=ROW_WORDS), :] = lo | (hi << 16)


def _load_table_once(tab_hbm, tab_v, sem):
    @pl.when(pl.program_id(0) == 0)
    def _():
        cp = pltpu.make_async_copy(tab_hbm, tab_v, sem)
        cp.start()
        cp.wait()


def _tc_token_pipeline(idx_hbm, idx_s, isem, tab_v, per_token):
    step = pl.program_id(0)

    def idx_copy(block, slot):
        return pltpu.make_async_copy(idx_hbm.at[block], idx_s[slot], isem.at[slot])

    @pl.when(step == 0)
    def _():
        idx_copy(0, 0).start()

    for sub in range(TC_SUBS):
        block = step * TC_SUBS + sub
        idx_copy(block, sub).wait()
        if sub + 1 < TC_SUBS:
            idx_copy(block + 1, sub + 1).start()
        else:
            @pl.when(step + 1 < pl.num_programs(0))
            def _():
                idx_copy(block + 1, 0).start()
        offs = idx_s[sub]

        def chunks(i, offs=offs):
            per = PEER_E // TC_KCHUNKS
            for ch in range(TC_KCHUNKS):
                pieces = [tab_v[pl.ds(pl.multiple_of(offs.at[k][i], ROW_WORDS), ROW_WORDS), :]
                          for k in range(ch * per, (ch + 1) * per)]
                yield pltpu.bitcast(jnp.concatenate(pieces, axis=0), BF16)

        def trip(ip, carry, sub=sub, chunks=chunks):
            for u in range(TC_TRIP):
                i = ip * TC_TRIP + u
                per_token(sub * TC_SUB + i, chunks(i))
            return carry
        lax.fori_loop(0, TC_SUB // TC_TRIP, trip, 0)


def _tc_peer_scratch(tabp):
    return [pltpu.VMEM(tabp.shape, jnp.uint32),
            [pltpu.SMEM((PEER_E, TC_SUB), jnp.int32) for _ in range(TC_SUBS)],
            pltpu.SemaphoreType.DMA,
            pltpu.SemaphoreType.DMA((TC_SUBS,))]


def _stack_selectors():
    col = np.arange(STACK_COLS)
    diag = (col[None, :] % ROW_HALVES == np.arange(ROW_HALVES)[:, None]).astype(np.float32)
    owner = (col[:, None] // ROW_HALVES == np.arange(PEER_E)[None, :]).astype(np.float32)
    return jnp.asarray(diag), jnp.asarray(owner)


def _tc_down_kernel(idx_hbm, x_ref, tab_hbm, diag_ref, fold_ref, o_ref,
                    tab_v, idx_s, sem, isem, z_ref):
    _load_table_once(tab_hbm, tab_v, sem)
    cols = STACK_COLS // TC_KCHUNKS

    def per_token(t, chunks):
        xr = x_ref[pl.ds(t, 1), :]
        x8 = jnp.concatenate([xr[:, LANES * j:LANES * (j + 1)] for j in range(ROW_HALVES)],
                             axis=0).astype(BF16)
        parts = []
        for ch, rows in enumerate(chunks):
            a = lax.dot_general(x8, rows, (((1,), (1,)), ((), ())), preferred_element_type=F32)
            parts.append(jnp.sum(a * diag_ref[:, cols * ch:cols * (ch + 1)], axis=0, keepdims=True))
        z_ref[pl.ds(t, 1), :] = jnp.concatenate(parts, axis=1)

    _tc_token_pipeline(idx_hbm, idx_s, isem, tab_v, per_token)
    o_ref[...] = _dot_exact(z_ref[...], fold_ref[...])


def _peer_down_dots_tc(xn, offs, tabp, n_tok):
    diag, owner = _stack_selectors()
    full = lambda a: pl.BlockSpec(a.shape, lambda i: (0,) * a.ndim)
    return pl.pallas_call(
        _tc_down_kernel,
        grid=(n_tok // TC_STEP,),
        in_specs=[pl.BlockSpec(memory_space=pl.ANY),
                  pl.BlockSpec((TC_STEP, xn.shape[1]), lambda i: (i, 0)),
                  pl.BlockSpec(memory_space=pl.ANY), full(diag), full(owner)],
        out_specs=pl.BlockSpec((TC_STEP, PEER_E), lambda i: (i, 0)),
        out_shape=jax.ShapeDtypeStruct((n_tok, PEER_E), F32),
        scratch_shapes=_tc_peer_scratch(tabp) + [pltpu.VMEM((TC_STEP, STACK_COLS), F32)],
        compiler_params=pltpu.CompilerParams(
            dimension_semantics=("arbitrary",), vmem_limit_bytes=VMEM_LIMIT),
        name="peer_down_dots_tc",
    )(offs, xn, tabp, diag, owner)


def _tc_up_kernel(idx_hbm, coef_ref, tab_hbm, spread_ref, diag_ref, o_ref,
                  tab_v, idx_s, sem, isem, ce_ref):
    _load_table_once(tab_hbm, tab_v, sem)
    c = coef_ref[...]
    c_hi = c.astype(BF16)
    c_lo = (c - c_hi.astype(F32)).astype(BF16)
    ce_ref[0] = jnp.dot(c_hi, spread_ref[...], preferred_element_type=F32)
    ce_ref[1] = jnp.dot(c_lo, spread_ref[...], preferred_element_type=F32)

    cols = STACK_COLS // TC_KCHUNKS

    def per_token(t, chunks):
        d = diag_ref[...]
        lhs = jnp.concatenate([(ce_ref[0, pl.ds(t, 1), :] * d).astype(BF16),
                               (ce_ref[1, pl.ds(t, 1), :] * d).astype(BF16)], axis=0)
        r = jnp.zeros((2 * ROW_HALVES, LANES), F32)
        for ch, rows in enumerate(chunks):
            r = r + jnp.dot(lhs[:, cols * ch:cols * (ch + 1)], rows, preferred_element_type=F32)
        out8 = r[0:ROW_HALVES] + r[ROW_HALVES:2 * ROW_HALVES]
        o_ref[pl.ds(t, 1), :] = jnp.concatenate(
            [out8[j:j + 1, :] for j in range(ROW_HALVES)], axis=1)

    _tc_token_pipeline(idx_hbm, idx_s, isem, tab_v, per_token)


def _peer_up_sum_tc(coef, offs, tabp, n_tok):
    d = ROW_HALVES * LANES
    diag, owner = _stack_selectors()
    spread = owner.T.astype(BF16)
    full = lambda a: pl.BlockSpec(a.shape, lambda i: (0,) * a.ndim)
    return pl.pallas_call(
        _tc_up_kernel,
        grid=(n_tok // TC_STEP,),
        in_specs=[pl.BlockSpec(memory_space=pl.ANY),
                  pl.BlockSpec((TC_STEP, PEER_E), lambda i: (i, 0)),
                  pl.BlockSpec(memory_space=pl.ANY), full(spread), full(diag)],
        out_specs=pl.BlockSpec((TC_STEP, d), lambda i: (i, 0)),
        out_shape=jax.ShapeDtypeStruct((n_tok, d), F32),
        scratch_shapes=_tc_peer_scratch(tabp) + [pltpu.VMEM((2, TC_STEP, STACK_COLS), F32)],
        compiler_params=pltpu.CompilerParams(
            dimension_semantics=("arbitrary",), vmem_limit_bytes=VMEM_LIMIT),
        name="peer_up_sum_tc",
    )(offs, coef, tabp, spread, diag)


def _coef_kernel(pre_ref, gate_ref, o_ref):
    x = pre_ref[...]
    gelu = 0.5 * x * (1.0 + lax.erf(x * (2.0 ** -0.5)))
    o_ref[...] = gate_ref[...] * gelu


def _peer_coef(pre, gates):
    t, e = pre.shape
    row = pl.BlockSpec((TM_COEF, e), lambda i: (i, 0))
    return pl.pallas_call(
        _coef_kernel,
        grid=(t // TM_COEF,),
        in_specs=[row, row],
        out_specs=row,
        out_shape=jax.ShapeDtypeStruct((t, e), F32),
        compiler_params=pltpu.CompilerParams(dimension_semantics=("arbitrary",)),
        name="peer_gate_gelu",
    )(pre, gates)


def _front_half(x2, row0, rows, bsz, seq, w_pad, layer, i):
    (norm1_g, ret_norm_g, m_conv_w, m_conv_b, m_wq, m_wk, m_b_i, m_b_f, m_norm_g, m_skip,
     w_out, norm2_g, peer_wq, keys) = layer
    d = x2.shape[1]
    proj = _rms_proj(x2, row0, rows, norm1_g[i].reshape(1, d), w_pad)
    y = _mixer(proj, bsz, seq, ret_norm_g[i], m_conv_w[i], m_conv_b[i], m_wq[i], m_wk[i],
               m_b_i[i], m_b_f[i], m_norm_g[i], m_skip[i])
    x1, xn, sub_t = _post(y, x2, row0, w_out, norm2_g[i].reshape(1, d), peer_wq, keys)
    return (x1, xn) + tuple(_peer_route(sub_t))


def kernel(x, norm1_g, w_in, ret_norm_g, m_conv_w, m_conv_b, m_wq, m_wk, m_b_i, m_b_f,
           m_norm_g, m_skip, w_out, norm2_g, peer_wq, peer_keys, peer_down, peer_up, final_g):
    b, s, d = x.shape
    t = b * s
    tp = t // FRONT_PARTS
    x2 = x.reshape(t, d)
    depth = norm1_g.shape[0]
    for i in range(depth):
        if i > 0:
            x2 = jnp.concatenate(x1s, axis=0) + jnp.concatenate([p_tc, p_sc], axis=0)
        wi = w_in[i]
        n_main = OFF_GI
        gate_pad = lambda w: jnp.pad(w, ((0, 0), (0, LANES - w.shape[1])))
        w_pad = jnp.concatenate(
            [wi[:, :n_main], gate_pad(wi[:, n_main:n_main + MLSTM_HEADS]),
             gate_pad(wi[:, n_main + MLSTM_HEADS:])], axis=1).astype(BF16)
        keys = peer_keys[i].reshape(2 * PEER_HEADS, PEER_NKEYS, PEER_HALF).astype(BF16)
        layer = (norm1_g, ret_norm_g, m_conv_w, m_conv_b, m_wq, m_wk, m_b_i, m_b_f, m_norm_g,
                 m_skip, w_out[i].astype(BF16), norm2_g, peer_wq[i].astype(BF16), keys)
        down_p = _pack_rows_bf16(peer_down[i])
        up_p = _pack_rows_bf16(peer_up[i])

        fronts = []
        pre = [None] * FRONT_PARTS
        for q in range(FRONT_PARTS):
            fr = _front_half(x2, q * tp, tp, b // FRONT_PARTS, s, w_pad, layer, i)
            fronts.append(fr)
            if q < SC_DOWN_PARTS:
                pre[q] = [_peer_down_dots(fr[1], fr[2], peer_down[i], 0)]
            elif q == SC_DOWN_PARTS and SC_DOWN_EXTRA:
                pre[q] = [None, _peer_down_dots(fr[1], fr[2], peer_down[i], tp - SC_DOWN_EXTRA)]
        for q in range(SC_DOWN_PARTS, FRONT_PARTS):
            n_tc_q = tp - SC_DOWN_EXTRA if q == SC_DOWN_PARTS else tp
            pre_tc = _peer_down_dots_tc(fronts[q][1], fronts[q][4], down_p, n_tc_q)
            if pre[q] is None:
                pre[q] = [pre_tc]
            else:
                pre[q][0] = pre_tc
        pre = [piece for part in pre for piece in part]
        x1s = [fr[0] for fr in fronts]
        expert = jnp.concatenate([fr[2] for fr in fronts], axis=0)
        gates = jnp.concatenate([fr[3] for fr in fronts], axis=0)
        offs = jnp.concatenate([fr[4] for fr in fronts], axis=0)
        coef = _peer_coef(jnp.concatenate(pre, axis=0), gates)
        n_tc = PEER_TC_UP_TOKENS
        p_sc = _peer_up_sum(coef, expert, peer_up[i], n_tc)
        p_tc = _peer_up_sum_tc(coef, offs, up_p, n_tc)
    out = _final(_row_ranges(x1s, [p_tc, p_sc]), final_g.reshape(1, d), t)
    return out.reshape(b, s, d)


def _row_ranges(xs, ps):
    def starts(arrs):
        out = [0]
        for a in arrs:
            out.append(out[-1] + a.shape[0])
        return out
    xb, pb = starts(xs), starts(ps)
    assert xb[-1] == pb[-1]
    cuts = sorted(set(xb) | set(pb))
    parts = []
    for lo, hi in zip(cuts[:-1], cuts[1:]):
        xi = max(k for k in range(len(xs)) if xb[k] <= lo)
        pi = max(k for k in range(len(ps)) if pb[k] <= lo)
        parts.append((xs[xi], lo - xb[xi], ps[pi], lo - pb[pi], hi - lo))
    return parts
```

```python
import functools

import numpy as np
import jax
import jax.numpy as jnp
from jax import lax
from jax.experimental import pallas as pl
from jax.experimental.pallas import tpu as pltpu
from jax.experimental.pallas import tpu_sc as plsc

F32 = jnp.float32
BF16 = jnp.bfloat16

D_MODEL = 1024
CHUNK = 64
RET_HEADS = 4
RET_DK = 128
MLSTM_HEADS = 4
MLSTM_D = 128
HEAD_W = 128
D_GROUP = 512
CONV_W = 4
ROPE_BASE = 10000.0
PEER_HEADS = 8
PEER_NKEYS = 128
PEER_TOPK = 16
PEER_HALF = 128
PEER_BLOCK = 128
EPS = 1e-6

OFF_RQ, OFF_RK, OFF_RV, OFF_RG = 0, 512, 1024, 1536
OFF_MX, OFF_MV, OFF_MO = 2048, 2560, 3072
OFF_GI, OFF_GF = 3584, 3712
D_PROJ_PAD = 3840

LANES = 128
CONV_TAIL = 8
VMEM_LIMIT = 56 * 1024 * 1024

TM_PROJ = 256
L_BLOCK = 256
TM_FINAL = 512
TM_COEF = 2048


def _rms(x, g):
    ms = jnp.mean(x * x, axis=-1, keepdims=True)
    return x * lax.rsqrt(ms + EPS) * g


def _group_norm(h, g):
    mu = jnp.mean(h, axis=-1, keepdims=True)
    d = h - mu
    var = jnp.mean(d * d, axis=-1, keepdims=True)
    return d * lax.rsqrt(var + EPS) * g


def _silu(x):
    return x * (1.0 / (1.0 + jnp.exp(-x)))


def _sigmoid(x):
    return 1.0 / (1.0 + jnp.exp(-x))


def _dot(a, b):
    return jnp.dot(a.astype(BF16), b.astype(BF16), preferred_element_type=F32)


def _dot_tn(a, b):
    return lax.dot_general(a.astype(BF16), b.astype(BF16), (((0,), (0,)), ((), ())),
                           preferred_element_type=F32)


def _dot_nt(a, b):
    return lax.dot_general(a.astype(BF16), b.astype(BF16), (((1,), (1,)), ((), ())),
                           preferred_element_type=F32)


def _rms_proj_kernel(x_ref, g_ref, w_ref, o_ref):
    h = _rms(x_ref[...], g_ref[...])
    o_ref[...] = jnp.dot(h.astype(BF16), w_ref[...], preferred_element_type=F32)


def _rms_proj(x2, row0, rows, g, w):
    d = x2.shape[1]
    t = rows
    n = w.shape[1]
    b0 = row0 // TM_PROJ
    return pl.pallas_call(
        _rms_proj_kernel,
        grid=(t // TM_PROJ,),
        in_specs=[
            pl.BlockSpec((TM_PROJ, d), lambda i: (i + b0, 0)),
            pl.BlockSpec((1, d), lambda i: (0, 0)),
            pl.BlockSpec((d, n), lambda i: (0, 0)),
        ],
        out_specs=pl.BlockSpec((TM_PROJ, n), lambda i: (i, 0)),
        out_shape=jax.ShapeDtypeStruct((t, n), F32),
        compiler_params=pltpu.CompilerParams(
            dimension_semantics=("arbitrary",), vmem_limit_bytes=VMEM_LIMIT),
        name="rms_in_proj",
    )(x2, g, w)


def _mixer_kernel(chunk_decay,
                  proj_ref, cos_ref, sin_ref, intra_ref, qd_ref, kd_ref, tri_ref,
                  rng_ref, cw_ref, cb_ref, wq_ref, wk_ref, bi_ref, bf_ref,
                  mng_ref, skip_ref,
                  y_ref,
                  rstate_ref, cstate_ref, m_ref, tail_ref):
    j = pl.program_id(1)
    n_chunks = L_BLOCK // CHUNK

    @pl.when(j == 0)
    def _():
        rstate_ref[...] = jnp.zeros_like(rstate_ref)
        cstate_ref[...] = jnp.zeros_like(cstate_ref)
        m_ref[...] = jnp.zeros_like(m_ref)
        tail_ref[...] = jnp.zeros_like(tail_ref)

    cosv = cos_ref[...]
    sinv = sin_ref[...]
    k_scale = RET_DK ** -0.5

    for h in range(RET_HEADS):
        lo = h * HEAD_W
        q = proj_ref[:, OFF_RQ + lo:OFF_RQ + lo + HEAD_W]
        k = proj_ref[:, OFF_RK + lo:OFF_RK + lo + HEAD_W]
        q = q * cosv + pltpu.roll(q, HEAD_W // 2, 1) * sinv
        k = (k * cosv + pltpu.roll(k, HEAD_W // 2, 1) * sinv) * k_scale
        intra = intra_ref[h]
        qd = qd_ref[h]
        kd = kd_ref[h]
        g = rng_ref[:, lo:lo + HEAD_W]
        for c in range(n_chunks):
            r0 = c * CHUNK
            qc = q[r0:r0 + CHUNK]
            kc = k[r0:r0 + CHUNK]
            vc = proj_ref[r0:r0 + CHUNK, OFF_RV + lo:OFF_RV + lo + HEAD_W]
            gate = proj_ref[r0:r0 + CHUNK, OFF_RG + lo:OFF_RG + lo + HEAD_W]
            state = rstate_ref[h]
            scores = _dot_nt(qc, kc) * intra
            o = _dot(scores, vc) + _dot(qc, state) * qd
            rstate_ref[h] = chunk_decay[h] * state + _dot_tn(kc * kd, vc)
            y_ref[r0:r0 + CHUNK, lo:lo + HEAD_W] = _silu(gate) * _group_norm(o, g)

    mx = proj_ref[:, OFF_MX:OFF_MX + D_GROUP]
    xp = jnp.concatenate([tail_ref[...], mx], axis=0)
    base = CONV_TAIL - (CONV_W - 1)
    conv = cb_ref[...] + xp[base:base + L_BLOCK] * cw_ref[0:1, :]
    for t in range(1, CONV_W):
        conv = conv + xp[base + t:base + t + L_BLOCK] * cw_ref[t:t + 1, :]
    tail_ref[...] = mx[L_BLOCK - CONV_TAIL:L_BLOCK]
    cact = _silu(conv)

    gi = proj_ref[:, OFF_GI:OFF_GI + LANES] + bi_ref[...]
    gf = proj_ref[:, OFF_GF:OFF_GF + LANES] + bf_ref[...]
    logf = jnp.minimum(gf, 0.0) - jnp.log1p(jnp.exp(-jnp.abs(gf)))
    tri = tri_ref[...]
    lane = lax.broadcasted_iota(jnp.int32, (CHUNK, HEAD_W), 1)
    ones_col = (lane == 0).astype(F32)
    m_scale = MLSTM_D ** -0.5

    mq = []
    mk = []
    for h in range(MLSTM_HEADS):
        lo = h * HEAD_W
        ch = cact[:, lo:lo + HEAD_W]
        mq.append(_dot(ch, wq_ref[h]))
        mk.append(_dot(ch, wk_ref[h]) * m_scale)

    for c in range(n_chunks):
        r0 = c * CHUNK
        fcum = jnp.dot(tri, logf[r0:r0 + CHUNK], preferred_element_type=F32,
                       precision=lax.Precision.HIGHEST)
        a = fcum[CHUNK - 1:CHUNK]
        logw = a - fcum + gi[r0:r0 + CHUNK]
        m_in = jnp.max(logw, axis=0, keepdims=True)
        m_old = m_ref[...]
        m_new = jnp.maximum(a + m_old, m_in)
        decay = jnp.exp(a + m_old - m_new)
        w = jnp.exp(logw - m_new)
        em = jnp.exp(-m_new)
        m_ref[...] = m_new
        for h in range(MLSTM_HEADS):
            lo = h * HEAD_W
            qh = mq[h][r0:r0 + CHUNK]
            kh = mk[h][r0:r0 + CHUNK]
            vh = proj_ref[r0:r0 + CHUNK, OFF_MV + lo:OFF_MV + lo + HEAD_W]
            oh = proj_ref[r0:r0 + CHUNK, OFF_MO + lo:OFF_MO + lo + HEAD_W]
            ch = cact[r0:r0 + CHUNK, lo:lo + HEAD_W]
            v_ext = jnp.concatenate([vh, ones_col], axis=1)
            cmem = decay[:, h:h + 1] * cstate_ref[h] + _dot_tn(kh * w[:, h:h + 1], v_ext)
            cstate_ref[h] = cmem
            num = _dot(qh, cmem)
            den = jnp.maximum(jnp.abs(num[:, HEAD_W:HEAD_W + 1]), em[:, h:h + 1])
            hh = num[:, :HEAD_W] / den
            hm = _group_norm(hh, mng_ref[:, lo:lo + HEAD_W])
            ym = _sigmoid(oh) * (hm + skip_ref[:, lo:lo + HEAD_W] * ch)
            y_ref[r0:r0 + CHUNK, D_GROUP + lo:D_GROUP + lo + HEAD_W] = ym


def _retention_tables():
    h = np.arange(RET_HEADS, dtype=np.float64)
    log_g = np.log(1.0 - 2.0 ** (-5.0 - h))
    l = np.arange(CHUNK, dtype=np.float64)
    intra = np.exp(log_g[:, None, None] * np.abs(l[:, None] - l[None, :]))
    qd = np.exp(log_g[:, None] * (l + 1.0))[:, :, None] * np.ones((1, 1, HEAD_W))
    kd = np.exp(log_g[:, None] * (CHUNK - 1.0 - l))[:, :, None] * np.ones((1, 1, HEAD_W))
    cd = np.exp(log_g * CHUNK)
    return (jnp.asarray(intra, F32), jnp.asarray(qd, F32), jnp.asarray(kd, F32),
            tuple(float(v) for v in cd))


def _rotary_tables(s):
    half = HEAD_W // 2
    inv = ROPE_BASE ** (-np.arange(half, dtype=np.float64) / half)
    ang = np.arange(s, dtype=np.float64)[:, None] * inv[None, :]
    cos = np.cos(ang)
    sin = np.sin(ang)
    return (jnp.asarray(np.concatenate([cos, cos], axis=-1), F32),
            jnp.asarray(np.concatenate([-sin, sin], axis=-1), F32))


def _mixer(proj, b, s, ret_norm_g, m_conv_w, m_conv_b, m_wq, m_wk, m_b_i, m_b_f,
           m_norm_g, m_skip):
    intra, qd, kd, cd = _retention_tables()
    cos_t, sin_t = _rotary_tables(s)
    tri = jnp.asarray(np.tril(np.ones((CHUNK, CHUNK))), F32)
    nj = s // L_BLOCK
    pad = lambda v: jnp.pad(v.reshape(1, -1), ((0, 0), (0, LANES - v.size)))
    full = lambda shape: pl.BlockSpec(shape, lambda bi, j: (0,) * len(shape))
    return pl.pallas_call(
        functools.partial(_mixer_kernel, cd),
        grid=(b, nj),
        in_specs=[
            pl.BlockSpec((L_BLOCK, D_PROJ_PAD), lambda bi, j: (bi * nj + j, 0)),
            pl.BlockSpec((L_BLOCK, HEAD_W), lambda bi, j: (j, 0)),
            pl.BlockSpec((L_BLOCK, HEAD_W), lambda bi, j: (j, 0)),
            full((RET_HEADS, CHUNK, CHUNK)),
            full((RET_HEADS, CHUNK, HEAD_W)),
            full((RET_HEADS, CHUNK, HEAD_W)),
            full((CHUNK, CHUNK)),
            full((1, D_GROUP)),
            full((CONV_W, D_GROUP)),
            full((1, D_GROUP)),
            full((MLSTM_HEADS, MLSTM_D, MLSTM_D)),
            full((MLSTM_HEADS, MLSTM_D, MLSTM_D)),
            full((1, LANES)),
            full((1, LANES)),
            full((1, D_GROUP)),
            full((1, D_GROUP)),
        ],
        out_specs=pl.BlockSpec((L_BLOCK, 2 * D_GROUP), lambda bi, j: (bi * nj + j, 0)),
        out_shape=jax.ShapeDtypeStruct((b * s, 2 * D_GROUP), F32),
        scratch_shapes=[
            pltpu.VMEM((RET_HEADS, RET_DK, HEAD_W), F32),
            pltpu.VMEM((MLSTM_HEADS, MLSTM_D, 2 * HEAD_W), F32),
            pltpu.VMEM((1, LANES), F32),
            pltpu.VMEM((CONV_TAIL, D_GROUP), F32),
        ],
        compiler_params=pltpu.CompilerParams(
            dimension_semantics=("arbitrary", "arbitrary"), vmem_limit_bytes=VMEM_LIMIT),
        name="retention_mlstm_mixer",
    )(proj, cos_t, sin_t, intra, qd, kd, tri,
      ret_norm_g.reshape(1, D_GROUP), m_conv_w, m_conv_b.reshape(1, D_GROUP),
      m_wq.astype(BF16), m_wk.astype(BF16), pad(m_b_i), pad(m_b_f),
      m_norm_g.reshape(1, D_GROUP), m_skip.reshape(1, D_GROUP))


def _post_kernel(y_ref, x_ref, wout_ref, g2_ref, wq_ref, keys_ref,
                 x1_ref, xn_ref, sub_ref):
    x1 = x_ref[...] + jnp.dot(y_ref[...].astype(BF16), wout_ref[...],
                              preferred_element_type=F32)
    x1_ref[...] = x1
    xn = _rms(x1, g2_ref[...])
    xn_ref[...] = xn
    q = jnp.dot(xn.astype(BF16), wq_ref[...], preferred_element_type=F32)
    for i in range(2 * PEER_HEADS):
        lo = i * PEER_HALF
        sub_ref[lo:lo + PEER_NKEYS, :] = _dot_nt(keys_ref[i], q[:, lo:lo + PEER_HALF])


def _post(y, x2, row0, w_out, norm2_g, peer_wq, keys):
    t, d = y.shape
    nq = peer_wq.shape[1]
    b0 = row0 // TM_PROJ
    row = lambda w: pl.BlockSpec((TM_PROJ, w), lambda i: (i, 0))
    full = lambda shape: pl.BlockSpec(shape, lambda i: (0,) * len(shape))
    return pl.pallas_call(
        _post_kernel,
        grid=(t // TM_PROJ,),
        in_specs=[row(d), pl.BlockSpec((TM_PROJ, d), lambda i: (i + b0, 0)), full((d, d)),
                  full((1, d)), full((d, nq)), full(keys.shape)],
        out_specs=[row(d), row(d), pl.BlockSpec((nq, TM_PROJ), lambda i: (0, i))],
        out_shape=[jax.ShapeDtypeStruct((t, d), F32), jax.ShapeDtypeStruct((t, d), F32),
                   jax.ShapeDtypeStruct((nq, t), F32)],
        compiler_params=pltpu.CompilerParams(
            dimension_semantics=("arbitrary",), vmem_limit_bytes=VMEM_LIMIT),
        name="out_proj_peer_scores",
    )(y, x2, w_out, norm2_g, peer_wq, keys)


def _final_kernel(x1_ref, p_ref, g_ref, *rest):
    o_ref = rest[-1]
    o_ref[...] = _rms(x1_ref[...] + p_ref[...], g_ref[...])


def _final(parts, g, t):
    d = g.shape[1]
    spec = lambda r0: pl.BlockSpec((TM_FINAL, d), lambda i, b0=r0 // TM_FINAL: (i + b0, 0))
    out = None
    row0 = 0
    for x1, x_row0, p, p_row0, rows in parts:
        in_specs = [spec(x_row0), spec(p_row0), pl.BlockSpec((1, d), lambda i: (0, 0))]
        args = [x1, p, g]
        if out is not None:
            in_specs.append(pl.BlockSpec(memory_space=pl.ANY))
            args.append(out)
        out = pl.pallas_call(
            _final_kernel,
            grid=(rows // TM_FINAL,),
            in_specs=in_specs,
            out_specs=spec(row0),
            out_shape=jax.ShapeDtypeStruct((t, d), F32),
            input_output_aliases={} if len(args) == 3 else {3: 0},
            compiler_params=pltpu.CompilerParams(dimension_semantics=("arbitrary",)),
            name="final_rmsnorm",
        )(*args)
        row0 += rows
    assert row0 == t
    return out


TM_ROUTE = 256
CAND_ROWS = 56


def _candidate_tables():
    pairs = [(k1, k2) for k1 in range(PEER_TOPK) for k2 in range(PEER_TOPK)
             if (k1 + 1) * (k2 + 1) <= PEER_TOPK]
    assert len(pairs) <= CAND_ROWS
    rep = np.zeros((CAND_ROWS, 2 * PEER_TOPK), np.float32)
    til = np.zeros((CAND_ROWS, 2 * PEER_TOPK), np.float32)
    for pos, (k1, k2) in enumerate(pairs):
        rep[pos, k1] = 1.0
        til[pos, PEER_TOPK + k2] = 1.0
    return jnp.asarray(rep), jnp.asarray(til), len(pairs)


def _dot_exact(a, b):
    return jnp.dot(a, b, preferred_element_type=F32, precision=lax.Precision.HIGHEST)


def _route_kernel(n_cand, sub_ref, rep_ref, til_ref, seg_ref, exp_ref, gate_ref, offs_ref,
                  s_ref, i_ref, best_ref, eid_ref, m0_ref):
    tm = sub_ref.shape[1]
    neg = -jnp.inf
    key_id = lax.broadcasted_iota(jnp.int32, (PEER_NKEYS, tm), 0).astype(F32)
    cand_id = lax.broadcasted_iota(jnp.int32, (CAND_ROWS, tm), 0).astype(F32)
    rep = rep_ref[...]
    til = til_ref[...]
    for h in range(PEER_HEADS):
        for p in range(2):
            lo = (2 * h + p) * PEER_NKEYS
            x = sub_ref[lo:lo + PEER_NKEYS, :]
            for k in range(PEER_TOPK):
                m = jnp.max(x, axis=0, keepdims=True)
                idx = jnp.min(jnp.where(x == m, key_id, float(PEER_NKEYS)), axis=0, keepdims=True)
                x = jnp.where(key_id == idx, neg, x)
                r = p * PEER_TOPK + k
                s_ref[r:r + 1, :] = m
                i_ref[r:r + 1, :] = idx
        s = s_ref[...]
        iv = i_ref[...]
        cand = _dot_exact(rep, s) + _dot_exact(til, s)
        cand = jnp.where(cand_id < float(n_cand), cand, neg)
        eid = _dot(rep, iv) * float(PEER_NKEYS) + _dot(til, iv)
        for k in range(PEER_TOPK):
            m = jnp.max(cand, axis=0, keepdims=True)
            pos = jnp.min(jnp.where(cand == m, cand_id, float(CAND_ROWS)), axis=0, keepdims=True)
            sel = cand_id == pos
            e = jnp.max(jnp.where(sel, eid, -1.0), axis=0, keepdims=True)
            cand = jnp.where(sel, neg, cand)
            r = h * PEER_TOPK + k
            best_ref[r:r + 1, :] = m
            eid_ref[r:r + 1, :] = e
            if k == 0:
                m0_ref[r:r + PEER_TOPK, :] = jnp.broadcast_to(m, (PEER_TOPK, tm))
    pexp = jnp.exp(best_ref[...] - m0_ref[...])
    gates = pexp / _dot_exact(seg_ref[...], pexp)
    gate_ref[...] = gates.T
    eid = eid_ref[...].astype(jnp.int32)
    exp_ref[...] = eid.T
    for blk in range(tm // LANES):
        offs_ref[blk] = eid[:, blk * LANES:(blk + 1) * LANES] * ROW_WORDS


def _peer_route(sub_t):
    nq, t = sub_t.shape
    rep, til, n_cand = _candidate_tables()
    e = PEER_HEADS * PEER_TOPK
    seg = jnp.asarray(np.kron(np.eye(PEER_HEADS), np.ones((PEER_TOPK, PEER_TOPK))), F32)
    full = lambda a: pl.BlockSpec(a.shape, lambda i: (0,) * a.ndim)
    out = pl.BlockSpec((TM_ROUTE, e), lambda i: (i, 0))
    return pl.pallas_call(
        functools.partial(_route_kernel, n_cand),
        grid=(t // TM_ROUTE,),
        in_specs=[pl.BlockSpec((nq, TM_ROUTE), lambda i: (0, i)), full(rep), full(til), full(seg)],
        out_specs=[out, out, pl.BlockSpec((TM_ROUTE // LANES, e, LANES), lambda i: (i, 0, 0))],
        out_shape=[jax.ShapeDtypeStruct((t, e), jnp.int32), jax.ShapeDtypeStruct((t, e), F32),
                   jax.ShapeDtypeStruct((t // LANES, e, LANES), jnp.int32)],
        scratch_shapes=[
            pltpu.VMEM((2 * PEER_TOPK, TM_ROUTE), F32),
            pltpu.VMEM((2 * PEER_TOPK, TM_ROUTE), F32),
            pltpu.VMEM((e, TM_ROUTE), F32),
            pltpu.VMEM((e, TM_ROUTE), F32),
            pltpu.VMEM((e, TM_ROUTE), F32),
        ],
        compiler_params=pltpu.CompilerParams(dimension_semantics=("arbitrary",)),
        name="peer_topk_route",
    )(sub_t, rep, til, seg)


SC_CORES = 2
SC_SUBCORES = 16
SC_LANES = 16
SC_WORKERS = SC_CORES * SC_SUBCORES
PEER_E = PEER_HEADS * PEER_TOPK
SC_ROWS = 32
SC_GATHERS = PEER_E // SC_ROWS
SC_TOK_BLOCK = 8
SC_QUARTER = 256


def _sc_mesh():
    return plsc.VectorSubcoreMesh(core_axis_name="c", subcore_axis_name="s")


def _sc_pipeline(idx_v, tab_hbm, rows_v, sems, compute):
    def gather(i, c, buf):
        window = idx_v.at[i, pl.ds(c * SC_ROWS, SC_ROWS)]
        return pltpu.make_async_copy(tab_hbm.at[window], rows_v.at[buf], sems.at[buf])

    gather(0, 0, 0).start()

    @pl.loop(0, SC_TOK_BLOCK)
    def _(i):
        for c in range(SC_GATHERS):
            buf = c % 2
            gather(i, c, buf).wait()
            if c + 1 < SC_GATHERS:
                gather(i, c + 1, 1 - buf).start()
            else:
                @pl.when(i + 1 < SC_TOK_BLOCK)
                def _():
                    gather(i + 1, 0, 1 - buf).start()
            compute(i, c, i * SC_GATHERS + c, buf)


def _peer_down_dots(xn, idx, down, tok_start):
    t_all, d = xn.shape
    t = t_all - tok_start
    tok_w = t // SC_WORKERS
    nblk = tok_w // SC_TOK_BLOCK
    assert nblk * SC_TOK_BLOCK * SC_WORKERS == t
    lanes = SC_LANES

    @functools.partial(
        pl.kernel, mesh=_sc_mesh(),
        out_type=jax.ShapeDtypeStruct((t, PEER_E), F32),
        scratch_types=[
            pltpu.VMEM((SC_TOK_BLOCK, PEER_E), jnp.int32),
            pltpu.VMEM((SC_TOK_BLOCK, d), F32),
            pltpu.VMEM((2, SC_ROWS, d), F32),
            pltpu.VMEM((SC_ROWS, lanes), F32),
            pltpu.VMEM((SC_TOK_BLOCK, PEER_E), F32),
            pltpu.SemaphoreType.DMA((2,)),
        ],
        compiler_params=pltpu.CompilerParams(needs_layout_passes=False),
        name="peer_down_dots",
    )
    def k(x_hbm, idx_hbm, tab_hbm, out_hbm, idx_v, x_v, rows_v, acc_v, out_v, sems):
        wid = lax.axis_index("s") * SC_CORES + lax.axis_index("c")
        lane = lax.iota(jnp.int32, lanes)

        @pl.loop(0, nblk)
        def _(blk):
            out0 = wid * tok_w + blk * SC_TOK_BLOCK
            tok0 = tok_start + out0
            pltpu.sync_copy(idx_hbm.at[pl.ds(tok0, SC_TOK_BLOCK)], idx_v)
            pltpu.sync_copy(x_hbm.at[pl.ds(tok0, SC_TOK_BLOCK)], x_v)

            def compute(i, c, step, buf):
                def body(dc, accs):
                    xv = x_v[i, pl.ds(dc * lanes, lanes)]
                    return tuple(a + rows_v[buf, r, pl.ds(dc * lanes, lanes)] * xv
                                 for r, a in enumerate(accs))
                accs = lax.fori_loop(0, d // lanes, body,
                                     tuple(jnp.zeros((lanes,), F32) for _ in range(SC_ROWS)))
                for r in range(SC_ROWS):
                    acc_v[r, :] = accs[r]
                for g in range(SC_ROWS // lanes):
                    tot = jnp.zeros((lanes,), F32)
                    for l in range(lanes):
                        tot = tot + plsc.load_gather(
                            acc_v, [lane + g * lanes, jnp.full((lanes,), l, jnp.int32)])
                    out_v[i, pl.ds(c * SC_ROWS + g * lanes, lanes)] = tot

            _sc_pipeline(idx_v, tab_hbm, rows_v, sems, compute)
            pltpu.sync_copy(out_v, out_hbm.at[pl.ds(out0, SC_TOK_BLOCK)])

    return k(xn, idx, down)


def _peer_up_sum(coef, idx, up, tok_start):
    t_all = coef.shape[0]
    t = t_all - tok_start
    d = up.shape[1]
    tok_w = t // SC_WORKERS
    nblk = tok_w // SC_TOK_BLOCK
    assert nblk * SC_TOK_BLOCK * SC_WORKERS == t
    lanes = SC_LANES
    nj = SC_QUARTER // lanes

    @functools.partial(
        pl.kernel, mesh=_sc_mesh(),
        out_type=jax.ShapeDtypeStruct((t, d), F32),
        scratch_types=[
            pltpu.VMEM((SC_TOK_BLOCK, PEER_E), jnp.int32),
            pltpu.VMEM((SC_TOK_BLOCK * PEER_E,), F32),
            pltpu.VMEM((2, SC_ROWS, d), F32),
            pltpu.VMEM((SC_TOK_BLOCK, d), F32),
            pltpu.SemaphoreType.DMA((2,)),
        ],
        compiler_params=pltpu.CompilerParams(needs_layout_passes=False),
        name="peer_up_sum",
    )
    def k(coef_hbm, idx_hbm, tab_hbm, out_hbm, idx_v, coef_v, rows_v, out_v, sems):
        wid = lax.axis_index("s") * SC_CORES + lax.axis_index("c")

        @pl.loop(0, nblk)
        def _(blk):
            out0 = wid * tok_w + blk * SC_TOK_BLOCK
            tok0 = tok_start + out0
            pltpu.sync_copy(idx_hbm.at[pl.ds(tok0, SC_TOK_BLOCK)], idx_v)
            pltpu.sync_copy(coef_hbm.at[pl.ds(tok0 * PEER_E, SC_TOK_BLOCK * PEER_E)], coef_v)

            def compute(i, c, step, buf):
                for q in range(d // SC_QUARTER):
                    col = lambda j: pl.ds(q * SC_QUARTER + j * lanes, lanes)
                    if c == 0:
                        init = tuple(jnp.zeros((lanes,), F32) for _ in range(nj))
                    else:
                        init = tuple(out_v[i, col(j)] for j in range(nj))

                    def body(r, accs):
                        cf = plsc.load_gather(
                            coef_v, [jnp.full((lanes,), step * SC_ROWS + r, jnp.int32)])
                        return tuple(a + rows_v[buf, r, col(j)] * cf for j, a in enumerate(accs))
                    accs = lax.fori_loop(0, SC_ROWS, body, init)
                    for j in range(nj):
                        out_v[i, col(j)] = accs[j]

            _sc_pipeline(idx_v, tab_hbm, rows_v, sems, compute)
            pltpu.sync_copy(out_v, out_hbm.at[pl.ds(out0, SC_TOK_BLOCK)])

    return k(coef.reshape(t_all * PEER_E), idx, up)


TC_SUB = 128
TC_SUBS = 2
TC_STEP = TC_SUB * TC_SUBS
TC_TRIP = 128
TC_KCHUNKS = 4
ROW_WORDS = 4
ROW_HALVES = 2 * ROW_WORDS
STACK_COLS = PEER_E * ROW_HALVES
FRONT_PARTS = 4
SC_DOWN_PARTS = 1
SC_DOWN_EXTRA = 6144
PEER_TC_UP_TOKENS = 25600


def _pack_rows_bf16(tab):
    n, d = tab.shape
    return pl.pallas_call(
        _pack_kernel,
        grid=(n // PACK_ROWS,),
        in_specs=[pl.BlockSpec((PACK_ROWS, d), lambda i: (i, 0))],
        out_specs=pl.BlockSpec((PACK_ROWS * ROW_WORDS, LANES), lambda i: (i, 0)),
        out_shape=jax.ShapeDtypeStruct((n * ROW_WORDS, LANES), jnp.uint32),
        compiler_params=pltpu.CompilerParams(dimension_semantics=("arbitrary",)),
        name="pack_rows_bf16",
    )(tab)


PACK_ROWS = 512


def _pack_kernel(x_ref, o_ref):
    u = pltpu.bitcast(x_ref[...], jnp.uint32)
    r = (u + jnp.uint32(0x7FFF) + ((u >> 16) & jnp.uint32(1))) >> 16
    for s in range(ROW_WORDS):
        lo = r[:, 2 * LANES * s:2 * LANES * s + LANES]
        hi = r[:, 2 * LANES * s + LANES:2 * LANES * (s + 1)]
        o_ref[pl.ds(s, PACK_ROWS, stride=ROW_WORDS), :] = lo | (hi << 16)


def _load_table_once(tab_hbm, tab_v, sem):
    @pl.when(pl.program_id(0) == 0)
    def _():
        cp = pltpu.make_async_copy(tab_hbm, tab_v, sem)
        cp.start()
        cp.wait()


def _tc_token_pipeline(idx_hbm, idx_s, isem, tab_v, per_token):
    step = pl.program_id(0)

    def idx_copy(block, slot):
        return pltpu.make_async_copy(idx_hbm.at[block], idx_s[slot], isem.at[slot])

    @pl.when(step == 0)
    def _():
        idx_copy(0, 0).start()

    for sub in range(TC_SUBS):
        block = step * TC_SUBS + sub
        idx_copy(block, sub).wait()
        if sub + 1 < TC_SUBS:
            idx_copy(block + 1, sub + 1).start()
        else:
            @pl.when(step + 1 < pl.num_programs(0))
            def _():
                idx_copy(block + 1, 0).start()
        offs = idx_s[sub]

        def chunks(i, offs=offs):
            per = PEER_E // TC_KCHUNKS
            for ch in range(TC_KCHUNKS):
                pieces = [tab_v[pl.ds(pl.multiple_of(offs.at[k][i], ROW_WORDS), ROW_WORDS), :]
                          for k in range(ch * per, (ch + 1) * per)]
                yield pltpu.bitcast(jnp.concatenate(pieces, axis=0), BF16)

        def trip(ip, carry, sub=sub, chunks=chunks):
            for u in range(TC_TRIP):
                i = ip * TC_TRIP + u
                per_token(sub * TC_SUB + i, chunks(i))
            return carry
        lax.fori_loop(0, TC_SUB // TC_TRIP, trip, 0)


def _tc_peer_scratch(tabp):
    return [pltpu.VMEM(tabp.shape, jnp.uint32),
            [pltpu.SMEM((PEER_E, TC_SUB), jnp.int32) for _ in range(TC_SUBS)],
            pltpu.SemaphoreType.DMA,
            pltpu.SemaphoreType.DMA((TC_SUBS,))]


def _stack_selectors():
    col = np.arange(STACK_COLS)
    diag = (col[None, :] % ROW_HALVES == np.arange(ROW_HALVES)[:, None]).astype(np.float32)
    owner = (col[:, None] // ROW_HALVES == np.arange(PEER_E)[None, :]).astype(np.float32)
    return jnp.asarray(diag), jnp.asarray(owner)


def _tc_down_kernel(idx_hbm, x_ref, tab_hbm, diag_ref, fold_ref, o_ref,
                    tab_v, idx_s, sem, isem, z_ref):
    _load_table_once(tab_hbm, tab_v, sem)
    cols = STACK_COLS // TC_KCHUNKS

    def per_token(t, chunks):
        xr = x_ref[pl.ds(t, 1), :]
        x8 = jnp.concatenate([xr[:, LANES * j:LANES * (j + 1)] for j in range(ROW_HALVES)],
                             axis=0).astype(BF16)
        parts = []
        for ch, rows in enumerate(chunks):
            a = lax.dot_general(x8, rows, (((1,), (1,)), ((), ())), preferred_element_type=F32)
            parts.append(jnp.sum(a * diag_ref[:, cols * ch:cols * (ch + 1)], axis=0, keepdims=True))
        z_ref[pl.ds(t, 1), :] = jnp.concatenate(parts, axis=1)

    _tc_token_pipeline(idx_hbm, idx_s, isem, tab_v, per_token)
    o_ref[...] = _dot_exact(z_ref[...], fold_ref[...])


def _peer_down_dots_tc(xn, offs, tabp, n_tok):
    diag, owner = _stack_selectors()
    full = lambda a: pl.BlockSpec(a.shape, lambda i: (0,) * a.ndim)
    return pl.pallas_call(
        _tc_down_kernel,
        grid=(n_tok // TC_STEP,),
        in_specs=[pl.BlockSpec(memory_space=pl.ANY),
                  pl.BlockSpec((TC_STEP, xn.shape[1]), lambda i: (i, 0)),
                  pl.BlockSpec(memory_space=pl.ANY), full(diag), full(owner)],
        out_specs=pl.BlockSpec((TC_STEP, PEER_E), lambda i: (i, 0)),
        out_shape=jax.ShapeDtypeStruct((n_tok, PEER_E), F32),
        scratch_shapes=_tc_peer_scratch(tabp) + [pltpu.VMEM((TC_STEP, STACK_COLS), F32)],
        compiler_params=pltpu.CompilerParams(
            dimension_semantics=("arbitrary",), vmem_limit_bytes=VMEM_LIMIT),
        name="peer_down_dots_tc",
    )(offs, xn, tabp, diag, owner)


def _tc_up_kernel(idx_hbm, coef_ref, tab_hbm, spread_ref, diag_ref, o_ref,
                  tab_v, idx_s, sem, isem, ce_ref):
    _load_table_once(tab_hbm, tab_v, sem)
    c = coef_ref[...]
    c_hi = c.astype(BF16)
    c_lo = (c - c_hi.astype(F32)).astype(BF16)
    ce_ref[0] = jnp.dot(c_hi, spread_ref[...], preferred_element_type=F32)
    ce_ref[1] = jnp.dot(c_lo, spread_ref[...], preferred_element_type=F32)

    cols = STACK_COLS // TC_KCHUNKS

    def per_token(t, chunks):
        d = diag_ref[...]
        lhs = jnp.concatenate([(ce_ref[0, pl.ds(t, 1), :] * d).astype(BF16),
                               (ce_ref[1, pl.ds(t, 1), :] * d).astype(BF16)], axis=0)
        r = jnp.zeros((2 * ROW_HALVES, LANES), F32)
        for ch, rows in enumerate(chunks):
            r = r + jnp.dot(lhs[:, cols * ch:cols * (ch + 1)], rows, preferred_element_type=F32)
        out8 = r[0:ROW_HALVES] + r[ROW_HALVES:2 * ROW_HALVES]
        o_ref[pl.ds(t, 1), :] = jnp.concatenate(
            [out8[j:j + 1, :] for j in range(ROW_HALVES)], axis=1)

    _tc_token_pipeline(idx_hbm, idx_s, isem, tab_v, per_token)


def _peer_up_sum_tc(coef, offs, tabp, n_tok):
    d = ROW_HALVES * LANES
    diag, owner = _stack_selectors()
    spread = owner.T.astype(BF16)
    full = lambda a: pl.BlockSpec(a.shape, lambda i: (0,) * a.ndim)
    return pl.pallas_call(
        _tc_up_kernel,
        grid=(n_tok // TC_STEP,),
        in_specs=[pl.BlockSpec(memory_space=pl.ANY),
                  pl.BlockSpec((TC_STEP, PEER_E), lambda i: (i, 0)),
                  pl.BlockSpec(memory_space=pl.ANY), full(spread), full(diag)],
        out_specs=pl.BlockSpec((TC_STEP, d), lambda i: (i, 0)),
        out_shape=jax.ShapeDtypeStruct((n_tok, d), F32),
        scratch_shapes=_tc_peer_scratch(tabp) + [pltpu.VMEM((2, TC_STEP, STACK_COLS), F32)],
        compiler_params=pltpu.CompilerParams(
            dimension_semantics=("arbitrary",), vmem_limit_bytes=VMEM_LIMIT),
        name="peer_up_sum_tc",
    )(offs, coef, tabp, spread, diag)


def _coef_kernel(pre_ref, gate_ref, o_ref):
    x = pre_ref[...]
    gelu = 0.5 * x * (1.0 + lax.erf(x * (2.0 ** -0.5)))
    o_ref[...] = gate_ref[...] * gelu


def _peer_coef(pre, gates):
    t, e = pre.shape
    row = pl.BlockSpec((TM_COEF, e), lambda i: (i, 0))
    return pl.pallas_call(
        _coef_kernel,
        grid=(t // TM_COEF,),
        in_specs=[row, row],
        out_specs=row,
        out_shape=jax.ShapeDtypeStruct((t, e), F32),
        compiler_params=pltpu.CompilerParams(dimension_semantics=("arbitrary",)),
        name="peer_gate_gelu",
    )(pre, gates)


def _front_half(x2, row0, rows, bsz, seq, w_pad, layer, i):
    (norm1_g, ret_norm_g, m_conv_w, m_conv_b, m_wq, m_wk, m_b_i, m_b_f, m_norm_g, m_skip,
     w_out, norm2_g, peer_wq, keys) = layer
    d = x2.shape[1]
    proj = _rms_proj(x2, row0, rows, norm1_g[i].reshape(1, d), w_pad)
    y = _mixer(proj, bsz, seq, ret_norm_g[i], m_conv_w[i], m_conv_b[i], m_wq[i], m_wk[i],
               m_b_i[i], m_b_f[i], m_norm_g[i], m_skip[i])
    x1, xn, sub_t = _post(y, x2, row0, w_out, norm2_g[i].reshape(1, d), peer_wq, keys)
    return (x1, xn) + tuple(_peer_route(sub_t))


def kernel(x, norm1_g, w_in, ret_norm_g, m_conv_w, m_conv_b, m_wq, m_wk, m_b_i, m_b_f,
           m_norm_g, m_skip, w_out, norm2_g, peer_wq, peer_keys, peer_down, peer_up, final_g):
    b, s, d = x.shape
    t = b * s
    tp = t // FRONT_PARTS
    x2 = x.reshape(t, d)
    depth = norm1_g.shape[0]
    for i in range(depth):
        if i > 0:
            x2 = jnp.concatenate(x1s, axis=0) + jnp.concatenate([p_tc, p_sc], axis=0)
        wi = w_in[i]
        n_main = OFF_GI
        gate_pad = lambda w: jnp.pad(w, ((0, 0), (0, LANES - w.shape[1])))
        w_pad = jnp.concatenate(
            [wi[:, :n_main], gate_pad(wi[:, n_main:n_main + MLSTM_HEADS]),
             gate_pad(wi[:, n_main + MLSTM_HEADS:])], axis=1).astype(BF16)
        keys = peer_keys[i].reshape(2 * PEER_HEADS, PEER_NKEYS, PEER_HALF).astype(BF16)
        layer = (norm1_g, ret_norm_g, m_conv_w, m_conv_b, m_wq, m_wk, m_b_i, m_b_f, m_norm_g,
                 m_skip, w_out[i].astype(BF16), norm2_g, peer_wq[i].astype(BF16), keys)
        down_p = _pack_rows_bf16(peer_down[i])
        up_p = _pack_rows_bf16(peer_up[i])

        fronts = []
        pre = [None] * FRONT_PARTS
        for q in range(FRONT_PARTS):
            fr = _front_half(x2, q * tp, tp, b // FRONT_PARTS, s, w_pad, layer, i)
            fronts.append(fr)
            if q < SC_DOWN_PARTS:
                pre[q] = [_peer_down_dots(fr[1], fr[2], peer_down[i], 0)]
            elif q == SC_DOWN_PARTS and SC_DOWN_EXTRA:
                pre[q] = [None, _peer_down_dots(fr[1], fr[2], peer_down[i], tp - SC_DOWN_EXTRA)]
        for q in range(SC_DOWN_PARTS, FRONT_PARTS):
            n_tc_q = tp - SC_DOWN_EXTRA if q == SC_DOWN_PARTS else tp
            pre_tc = _peer_down_dots_tc(fronts[q][1], fronts[q][4], down_p, n_tc_q)
            if pre[q] is None:
                pre[q] = [pre_tc]
            else:
                pre[q][0] = pre_tc
        pre = [piece for part in pre for piece in part]
        x1s = [fr[0] for fr in fronts]
        expert = jnp.concatenate([fr[2] for fr in fronts], axis=0)
        gates = jnp.concatenate([fr[3] for fr in fronts], axis=0)
        offs = jnp.concatenate([fr[4] for fr in fronts], axis=0)
        coef = _peer_coef(jnp.concatenate(pre, axis=0), gates)
        n_tc = PEER_TC_UP_TOKENS
        p_sc = _peer_up_sum(coef, expert, peer_up[i], n_tc)
        p_tc = _peer_up_sum_tc(coef, offs, up_p, n_tc)
    out = _final(_row_ranges(x1s, [p_tc, p_sc]), final_g.reshape(1, d), t)
    return out.reshape(b, s, d)


def _row_ranges(xs, ps):
    def starts(arrs):
        out = [0]
        for a in arrs:
            out.append(out[-1] + a.shape[0])
        return out
    xb, pb = starts(xs), starts(ps)
    assert xb[-1] == pb[-1]
    cuts = sorted(set(xb) | set(pb))
    parts = []
    for lo, hi in zip(cuts[:-1], cuts[1:]):
        xi = max(k for k in range(len(xs)) if xb[k] <= lo)
        pi = max(k for k in range(len(ps)) if pb[k] <= lo)
        parts.append((xs[xi], lo - xb[xi], ps[pi], lo - pb[pi], hi - lo))
    return parts
```

```python
import functools

import numpy as np
import jax
import jax.numpy as jnp
from jax import lax
from jax.experimental import pallas as pl
from jax.experimental.pallas import tpu as pltpu
from jax.experimental.pallas import tpu_sc as plsc

F32 = jnp.float32
BF16 = jnp.bfloat16

D_MODEL = 1024
CHUNK = 64
RET_HEADS = 4
RET_DK = 128
MLSTM_HEADS = 4
MLSTM_D = 128
HEAD_W = 128
D_GROUP = 512
CONV_W = 4
ROPE_BASE = 10000.0
PEER_HEADS = 8
PEER_NKEYS = 128
PEER_TOPK = 16
PEER_HALF = 128
PEER_BLOCK = 128
EPS = 1e-6

OFF_RQ, OFF_RK, OFF_RV, OFF_RG = 0, 512, 1024, 1536
OFF_MX, OFF_MV, OFF_MO = 2048, 2560, 3072
OFF_GI, OFF_GF = 3584, 3712
D_PROJ_PAD = 3840

LANES = 128
CONV_TAIL = 8
VMEM_LIMIT = 56 * 1024 * 1024

TM_PROJ = 256
L_BLOCK = 256
TM_FINAL = 512
TM_COEF = 2048


def _rms(x, g):
    ms = jnp.mean(x * x, axis=-1, keepdims=True)
    return x * lax.rsqrt(ms + EPS) * g


def _group_norm(h, g):
    mu = jnp.mean(h, axis=-1, keepdims=True)
    d = h - mu
    var = jnp.mean(d * d, axis=-1, keepdims=True)
    return d * lax.rsqrt(var + EPS) * g


def _silu(x):
    return x * (1.0 / (1.0 + jnp.exp(-x)))


def _sigmoid(x):
    return 1.0 / (1.0 + jnp.exp(-x))


def _dot(a, b):
    return jnp.dot(a.astype(BF16), b.astype(BF16), preferred_element_type=F32)


def _dot_tn(a, b):
    return lax.dot_general(a.astype(BF16), b.astype(BF16), (((0,), (0,)), ((), ())),
                           preferred_element_type=F32)


def _dot_nt(a, b):
    return lax.dot_general(a.astype(BF16), b.astype(BF16), (((1,), (1,)), ((), ())),
                           preferred_element_type=F32)


def _rms_proj_kernel(x_ref, g_ref, w_ref, o_ref):
    h = _rms(x_ref[...], g_ref[...])
    o_ref[...] = jnp.dot(h.astype(BF16), w_ref[...], preferred_element_type=F32)


def _rms_proj(x2, row0, rows, g, w):
    d = x2.shape[1]
    t = rows
    n = w.shape[1]
    b0 = row0 // TM_PROJ
    return pl.pallas_call(
        _rms_proj_kernel,
        grid=(t // TM_PROJ,),
        in_specs=[
            pl.BlockSpec((TM_PROJ, d), lambda i: (i + b0, 0)),
            pl.BlockSpec((1, d), lambda i: (0, 0)),
            pl.BlockSpec((d, n), lambda i: (0, 0)),
        ],
        out_specs=pl.BlockSpec((TM_PROJ, n), lambda i: (i, 0)),
        out_shape=jax.ShapeDtypeStruct((t, n), F32),
        compiler_params=pltpu.CompilerParams(
            dimension_semantics=("arbitrary",), vmem_limit_bytes=VMEM_LIMIT),
        name="rms_in_proj",
    )(x2, g, w)


def _mixer_kernel(chunk_decay,
                  proj_ref, cos_ref, sin_ref, intra_ref, qd_ref, kd_ref, tri_ref,
                  rng_ref, cw_ref, cb_ref, wq_ref, wk_ref, bi_ref, bf_ref,
                  mng_ref, skip_ref,
                  y_ref,
                  rstate_ref, cstate_ref, m_ref, tail_ref):
    j = pl.program_id(1)
    n_chunks = L_BLOCK // CHUNK

    @pl.when(j == 0)
    def _():
        rstate_ref[...] = jnp.zeros_like(rstate_ref)
        cstate_ref[...] = jnp.zeros_like(cstate_ref)
        m_ref[...] = jnp.zeros_like(m_ref)
        tail_ref[...] = jnp.zeros_like(tail_ref)

    cosv = cos_ref[...]
    sinv = sin_ref[...]
    k_scale = RET_DK ** -0.5

    for h in range(RET_HEADS):
        lo = h * HEAD_W
        q = proj_ref[:, OFF_RQ + lo:OFF_RQ + lo + HEAD_W]
        k = proj_ref[:, OFF_RK + lo:OFF_RK + lo + HEAD_W]
        q = q * cosv + pltpu.roll(q, HEAD_W // 2, 1) * sinv
        k = (k * cosv + pltpu.roll(k, HEAD_W // 2, 1) * sinv) * k_scale
        intra = intra_ref[h]
        qd = qd_ref[h]
        kd = kd_ref[h]
        g = rng_ref[:, lo:lo + HEAD_W]
        for c in range(n_chunks):
            r0 = c * CHUNK
            qc = q[r0:r0 + CHUNK]
            kc = k[r0:r0 + CHUNK]
            vc = proj_ref[r0:r0 + CHUNK, OFF_RV + lo:OFF_RV + lo + HEAD_W]
            gate = proj_ref[r0:r0 + CHUNK, OFF_RG + lo:OFF_RG + lo + HEAD_W]
            state = rstate_ref[h]
            scores = _dot_nt(qc, kc) * intra
            o = _dot(scores, vc) + _dot(qc, state) * qd
            rstate_ref[h] = chunk_decay[h] * state + _dot_tn(kc * kd, vc)
            y_ref[r0:r0 + CHUNK, lo:lo + HEAD_W] = _silu(gate) * _group_norm(o, g)

    mx = proj_ref[:, OFF_MX:OFF_MX + D_GROUP]
    xp = jnp.concatenate([tail_ref[...], mx], axis=0)
    base = CONV_TAIL - (CONV_W - 1)
    conv = cb_ref[...] + xp[base:base + L_BLOCK] * cw_ref[0:1, :]
    for t in range(1, CONV_W):
        conv = conv + xp[base + t:base + t + L_BLOCK] * cw_ref[t:t + 1, :]
    tail_ref[...] = mx[L_BLOCK - CONV_TAIL:L_BLOCK]
    cact = _silu(conv)

    gi = proj_ref[:, OFF_GI:OFF_GI + LANES] + bi_ref[...]
    gf = proj_ref[:, OFF_GF:OFF_GF + LANES] + bf_ref[...]
    logf = jnp.minimum(gf, 0.0) - jnp.log1p(jnp.exp(-jnp.abs(gf)))
    tri = tri_ref[...]
    lane = lax.broadcasted_iota(jnp.int32, (CHUNK, HEAD_W), 1)
    ones_col = (lane == 0).astype(F32)
    m_scale = MLSTM_D ** -0.5

    mq = []
    mk = []
    for h in range(MLSTM_HEADS):
        lo = h * HEAD_W
        ch = cact[:, lo:lo + HEAD_W]
        mq.append(_dot(ch, wq_ref[h]))
        mk.append(_dot(ch, wk_ref[h]) * m_scale)

    for c in range(n_chunks):
        r0 = c * CHUNK
        fcum = jnp.dot(tri, logf[r0:r0 + CHUNK], preferred_element_type=F32,
                       precision=lax.Precision.HIGHEST)
        a = fcum[CHUNK - 1:CHUNK]
        logw = a - fcum + gi[r0:r0 + CHUNK]
        m_in = jnp.max(logw, axis=0, keepdims=True)
        m_old = m_ref[...]
        m_new = jnp.maximum(a + m_old, m_in)
        decay = jnp.exp(a + m_old - m_new)
        w = jnp.exp(logw - m_new)
        em = jnp.exp(-m_new)
        m_ref[...] = m_new
        for h in range(MLSTM_HEADS):
            lo = h * HEAD_W
            qh = mq[h][r0:r0 + CHUNK]
            kh = mk[h][r0:r0 + CHUNK]
            vh = proj_ref[r0:r0 + CHUNK, OFF_MV + lo:OFF_MV + lo + HEAD_W]
            oh = proj_ref[r0:r0 + CHUNK, OFF_MO + lo:OFF_MO + lo + HEAD_W]
            ch = cact[r0:r0 + CHUNK, lo:lo + HEAD_W]
            v_ext = jnp.concatenate([vh, ones_col], axis=1)
            cmem = decay[:, h:h + 1] * cstate_ref[h] + _dot_tn(kh * w[:, h:h + 1], v_ext)
            cstate_ref[h] = cmem
            num = _dot(qh, cmem)
            den = jnp.maximum(jnp.abs(num[:, HEAD_W:HEAD_W + 1]), em[:, h:h + 1])
            hh = num[:, :HEAD_W] / den
            hm = _group_norm(hh, mng_ref[:, lo:lo + HEAD_W])
            ym = _sigmoid(oh) * (hm + skip_ref[:, lo:lo + HEAD_W] * ch)
            y_ref[r0:r0 + CHUNK, D_GROUP + lo:D_GROUP + lo + HEAD_W] = ym


def _retention_tables():
    h = np.arange(RET_HEADS, dtype=np.float64)
    log_g = np.log(1.0 - 2.0 ** (-5.0 - h))
    l = np.arange(CHUNK, dtype=np.float64)
    intra = np.exp(log_g[:, None, None] * np.abs(l[:, None] - l[None, :]))
    qd = np.exp(log_g[:, None] * (l + 1.0))[:, :, None] * np.ones((1, 1, HEAD_W))
    kd = np.exp(log_g[:, None] * (CHUNK - 1.0 - l))[:, :, None] * np.ones((1, 1, HEAD_W))
    cd = np.exp(log_g * CHUNK)
    return (jnp.asarray(intra, F32), jnp.asarray(qd, F32), jnp.asarray(kd, F32),
            tuple(float(v) for v in cd))


def _rotary_tables(s):
    half = HEAD_W // 2
    inv = ROPE_BASE ** (-np.arange(half, dtype=np.float64) / half)
    ang = np.arange(s, dtype=np.float64)[:, None] * inv[None, :]
    cos = np.cos(ang)
    sin = np.sin(ang)
    return (jnp.asarray(np.concatenate([cos, cos], axis=-1), F32),
            jnp.asarray(np.concatenate([-sin, sin], axis=-1), F32))


def _mixer(proj, b, s, ret_norm_g, m_conv_w, m_conv_b, m_wq, m_wk, m_b_i, m_b_f,
           m_norm_g, m_skip):
    intra, qd, kd, cd = _retention_tables()
    cos_t, sin_t = _rotary_tables(s)
    tri = jnp.asarray(np.tril(np.ones((CHUNK, CHUNK))), F32)
    nj = s // L_BLOCK
    pad = lambda v: jnp.pad(v.reshape(1, -1), ((0, 0), (0, LANES - v.size)))
    full = lambda shape: pl.BlockSpec(shape, lambda bi, j: (0,) * len(shape))
    return pl.pallas_call(
        functools.partial(_mixer_kernel, cd),
        grid=(b, nj),
        in_specs=[
            pl.BlockSpec((L_BLOCK, D_PROJ_PAD), lambda bi, j: (bi * nj + j, 0)),
            pl.BlockSpec((L_BLOCK, HEAD_W), lambda bi, j: (j, 0)),
            pl.BlockSpec((L_BLOCK, HEAD_W), lambda bi, j: (j, 0)),
            full((RET_HEADS, CHUNK, CHUNK)),
            full((RET_HEADS, CHUNK, HEAD_W)),
            full((RET_HEADS, CHUNK, HEAD_W)),
            full((CHUNK, CHUNK)),
            full((1, D_GROUP)),
            full((CONV_W, D_GROUP)),
            full((1, D_GROUP)),
            full((MLSTM_HEADS, MLSTM_D, MLSTM_D)),
            full((MLSTM_HEADS, MLSTM_D, MLSTM_D)),
            full((1, LANES)),
            full((1, LANES)),
            full((1, D_GROUP)),
            full((1, D_GROUP)),
        ],
        out_specs=pl.BlockSpec((L_BLOCK, 2 * D_GROUP), lambda bi, j: (bi * nj + j, 0)),
        out_shape=jax.ShapeDtypeStruct((b * s, 2 * D_GROUP), F32),
        scratch_shapes=[
            pltpu.VMEM((RET_HEADS, RET_DK, HEAD_W), F32),
            pltpu.VMEM((MLSTM_HEADS, MLSTM_D, 2 * HEAD_W), F32),
            pltpu.VMEM((1, LANES), F32),
            pltpu.VMEM((CONV_TAIL, D_GROUP), F32),
        ],
        compiler_params=pltpu.CompilerParams(
            dimension_semantics=("arbitrary", "arbitrary"), vmem_limit_bytes=VMEM_LIMIT),
        name="retention_mlstm_mixer",
    )(proj, cos_t, sin_t, intra, qd, kd, tri,
      ret_norm_g.reshape(1, D_GROUP), m_conv_w, m_conv_b.reshape(1, D_GROUP),
      m_wq.astype(BF16), m_wk.astype(BF16), pad(m_b_i), pad(m_b_f),
      m_norm_g.reshape(1, D_GROUP), m_skip.reshape(1, D_GROUP))


def _post_kernel(y_ref, x_ref, wout_ref, g2_ref, wq_ref, keys_ref,
                 x1_ref, xn_ref, sub_ref):
    x1 = x_ref[...] + jnp.dot(y_ref[...].astype(BF16), wout_ref[...],
                              preferred_element_type=F32)
    x1_ref[...] = x1
    xn = _rms(x1, g2_ref[...])
    xn_ref[...] = xn
    q = jnp.dot(xn.astype(BF16), wq_ref[...], preferred_element_type=F32)
    for i in range(2 * PEER_HEADS):
        lo = i * PEER_HALF
        sub_ref[lo:lo + PEER_NKEYS, :] = _dot_nt(keys_ref[i], q[:, lo:lo + PEER_HALF])


def _post(y, x2, row0, w_out, norm2_g, peer_wq, keys):
    t, d = y.shape
    nq = peer_wq.shape[1]
    b0 = row0 // TM_PROJ
    row = lambda w: pl.BlockSpec((TM_PROJ, w), lambda i: (i, 0))
    full = lambda shape: pl.BlockSpec(shape, lambda i: (0,) * len(shape))
    return pl.pallas_call(
        _post_kernel,
        grid=(t // TM_PROJ,),
        in_specs=[row(d), pl.BlockSpec((TM_PROJ, d), lambda i: (i + b0, 0)), full((d, d)),
                  full((1, d)), full((d, nq)), full(keys.shape)],
        out_specs=[row(d), row(d), pl.BlockSpec((nq, TM_PROJ), lambda i: (0, i))],
        out_shape=[jax.ShapeDtypeStruct((t, d), F32), jax.ShapeDtypeStruct((t, d), F32),
                   jax.ShapeDtypeStruct((nq, t), F32)],
        compiler_params=pltpu.CompilerParams(
            dimension_semantics=("arbitrary",), vmem_limit_bytes=VMEM_LIMIT),
        name="out_proj_peer_scores",
    )(y, x2, w_out, norm2_g, peer_wq, keys)


def _final_kernel(x1_ref, p_ref, g_ref, *rest):
    o_ref = rest[-1]
    o_ref[...] = _rms(x1_ref[...] + p_ref[...], g_ref[...])


def _final(parts, g, t):
    d = g.shape[1]
    spec = lambda r0: pl.BlockSpec((TM_FINAL, d), lambda i, b0=r0 // TM_FINAL: (i + b0, 0))
    out = None
    row0 = 0
    for x1, x_row0, p, p_row0, rows in parts:
        in_specs = [spec(x_row0), spec(p_row0), pl.BlockSpec((1, d), lambda i: (0, 0))]
        args = [x1, p, g]
        if out is not None:
            in_specs.append(pl.BlockSpec(memory_space=pl.ANY))
            args.append(out)
        out = pl.pallas_call(
            _final_kernel,
            grid=(rows // TM_FINAL,),
            in_specs=in_specs,
            out_specs=spec(row0),
            out_shape=jax.ShapeDtypeStruct((t, d), F32),
            input_output_aliases={} if len(args) == 3 else {3: 0},
            compiler_params=pltpu.CompilerParams(dimension_semantics=("arbitrary",)),
            name="final_rmsnorm",
        )(*args)
        row0 += rows
    assert row0 == t
    return out


TM_ROUTE = 256
CAND_ROWS = 56


def _candidate_tables():
    pairs = [(k1, k2) for k1 in range(PEER_TOPK) for k2 in range(PEER_TOPK)
             if (k1 + 1) * (k2 + 1) <= PEER_TOPK]
    assert len(pairs) <= CAND_ROWS
    rep = np.zeros((CAND_ROWS, 2 * PEER_TOPK), np.float32)
    til = np.zeros((CAND_ROWS, 2 * PEER_TOPK), np.float32)
    for pos, (k1, k2) in enumerate(pairs):
        rep[pos, k1] = 1.0
        til[pos, PEER_TOPK + k2] = 1.0
    return jnp.asarray(rep), jnp.asarray(til), len(pairs)


def _dot_exact(a, b):
    return jnp.dot(a, b, preferred_element_type=F32, precision=lax.Precision.HIGHEST)


def _route_kernel(n_cand, sub_ref, rep_ref, til_ref, seg_ref, exp_ref, gate_ref, offs_ref,
                  s_ref, i_ref, best_ref, eid_ref, m0_ref):
    tm = sub_ref.shape[1]
    neg = -jnp.inf
    key_id = lax.broadcasted_iota(jnp.int32, (PEER_NKEYS, tm), 0).astype(F32)
    cand_id = lax.broadcasted_iota(jnp.int32, (CAND_ROWS, tm), 0).astype(F32)
    rep = rep_ref[...]
    til = til_ref[...]
    for h in range(PEER_HEADS):
        for p in range(2):
            lo = (2 * h + p) * PEER_NKEYS
            x = sub_ref[lo:lo + PEER_NKEYS, :]
            for k in range(PEER_TOPK):
                m = jnp.max(x, axis=0, keepdims=True)
                idx = jnp.min(jnp.where(x == m, key_id, float(PEER_NKEYS)), axis=0, keepdims=True)
                x = jnp.where(key_id == idx, neg, x)
                r = p * PEER_TOPK + k
                s_ref[r:r + 1, :] = m
                i_ref[r:r + 1, :] = idx
        s = s_ref[...]
        iv = i_ref[...]
        cand = _dot_exact(rep, s) + _dot_exact(til, s)
        cand = jnp.where(cand_id < float(n_cand), cand, neg)
        eid = _dot(rep, iv) * float(PEER_NKEYS) + _dot(til, iv)
        for k in range(PEER_TOPK):
            m = jnp.max(cand, axis=0, keepdims=True)
            pos = jnp.min(jnp.where(cand == m, cand_id, float(CAND_ROWS)), axis=0, keepdims=True)
            sel = cand_id == pos
            e = jnp.max(jnp.where(sel, eid, -1.0), axis=0, keepdims=True)
            cand = jnp.where(sel, neg, cand)
            r = h * PEER_TOPK + k
            best_ref[r:r + 1, :] = m
            eid_ref[r:r + 1, :] = e
            if k == 0:
                m0_ref[r:r + PEER_TOPK, :] = jnp.broadcast_to(m, (PEER_TOPK, tm))
    pexp = jnp.exp(best_ref[...] - m0_ref[...])
    gates = pexp / _dot_exact(seg_ref[...], pexp)
    gate_ref[...] = gates.T
    eid = eid_ref[...].astype(jnp.int32)
    exp_ref[...] = eid.T
    for blk in range(tm // LANES):
        offs_ref[blk] = eid[:, blk * LANES:(blk + 1) * LANES] * ROW_WORDS


def _peer_route(sub_t):
    nq, t = sub_t.shape
    rep, til, n_cand = _candidate_tables()
    e = PEER_HEADS * PEER_TOPK
    seg = jnp.asarray(np.kron(np.eye(PEER_HEADS), np.ones((PEER_TOPK, PEER_TOPK))), F32)
    full = lambda a: pl.BlockSpec(a.shape, lambda i: (0,) * a.ndim)
    out = pl.BlockSpec((TM_ROUTE, e), lambda i: (i, 0))
    return pl.pallas_call(
        functools.partial(_route_kernel, n_cand),
        grid=(t // TM_ROUTE,),
        in_specs=[pl.BlockSpec((nq, TM_ROUTE), lambda i: (0, i)), full(rep), full(til), full(seg)],
        out_specs=[out, out, pl.BlockSpec((TM_ROUTE // LANES, e, LANES), lambda i: (i, 0, 0))],
        out_shape=[jax.ShapeDtypeStruct((t, e), jnp.int32), jax.ShapeDtypeStruct((t, e), F32),
                   jax.ShapeDtypeStruct((t // LANES, e, LANES), jnp.int32)],
        scratch_shapes=[
            pltpu.VMEM((2 * PEER_TOPK, TM_ROUTE), F32),
            pltpu.VMEM((2 * PEER_TOPK, TM_ROUTE), F32),
            pltpu.VMEM((e, TM_ROUTE), F32),
            pltpu.VMEM((e, TM_ROUTE), F32),
            pltpu.VMEM((e, TM_ROUTE), F32),
        ],
        compiler_params=pltpu.CompilerParams(dimension_semantics=("arbitrary",)),
        name="peer_topk_route",
    )(sub_t, rep, til, seg)


SC_CORES = 2
SC_SUBCORES = 16
SC_LANES = 16
SC_WORKERS = SC_CORES * SC_SUBCORES
PEER_E = PEER_HEADS * PEER_TOPK
SC_ROWS = 32
SC_GATHERS = PEER_E // SC_ROWS
SC_TOK_BLOCK = 8
SC_QUARTER = 256


def _sc_mesh():
    return plsc.VectorSubcoreMesh(core_axis_name="c", subcore_axis_name="s")


def _sc_pipeline(idx_v, tab_hbm, rows_v, sems, compute):
    def gather(i, c, buf):
        window = idx_v.at[i, pl.ds(c * SC_ROWS, SC_ROWS)]
        return pltpu.make_async_copy(tab_hbm.at[window], rows_v.at[buf], sems.at[buf])

    gather(0, 0, 0).start()

    @pl.loop(0, SC_TOK_BLOCK)
    def _(i):
        for c in range(SC_GATHERS):
            buf = c % 2
            gather(i, c, buf).wait()
            if c + 1 < SC_GATHERS:
                gather(i, c + 1, 1 - buf).start()
            else:
                @pl.when(i + 1 < SC_TOK_BLOCK)
                def _():
                    gather(i + 1, 0, 1 - buf).start()
            compute(i, c, i * SC_GATHERS + c, buf)


def _peer_down_dots(xn, idx, down, tok_start):
    t_all, d = xn.shape
    t = t_all - tok_start
    tok_w = t // SC_WORKERS
    nblk = tok_w // SC_TOK_BLOCK
    assert nblk * SC_TOK_BLOCK * SC_WORKERS == t
    lanes = SC_LANES

    @functools.partial(
        pl.kernel, mesh=_sc_mesh(),
        out_type=jax.ShapeDtypeStruct((t, PEER_E), F32),
        scratch_types=[
            pltpu.VMEM((SC_TOK_BLOCK, PEER_E), jnp.int32),
            pltpu.VMEM((SC_TOK_BLOCK, d), F32),
            pltpu.VMEM((2, SC_ROWS, d), F32),
            pltpu.VMEM((SC_ROWS, lanes), F32),
            pltpu.VMEM((SC_TOK_BLOCK, PEER_E), F32),
            pltpu.SemaphoreType.DMA((2,)),
        ],
        compiler_params=pltpu.CompilerParams(needs_layout_passes=False),
        name="peer_down_dots",
    )
    def k(x_hbm, idx_hbm, tab_hbm, out_hbm, idx_v, x_v, rows_v, acc_v, out_v, sems):
        wid = lax.axis_index("s") * SC_CORES + lax.axis_index("c")
        lane = lax.iota(jnp.int32, lanes)

        @pl.loop(0, nblk)
        def _(blk):
            out0 = wid * tok_w + blk * SC_TOK_BLOCK
            tok0 = tok_start + out0
            pltpu.sync_copy(idx_hbm.at[pl.ds(tok0, SC_TOK_BLOCK)], idx_v)
            pltpu.sync_copy(x_hbm.at[pl.ds(tok0, SC_TOK_BLOCK)], x_v)

            def compute(i, c, step, buf):
                def body(dc, accs):
                    xv = x_v[i, pl.ds(dc * lanes, lanes)]
                    return tuple(a + rows_v[buf, r, pl.ds(dc * lanes, lanes)] * xv
                                 for r, a in enumerate(accs))
                accs = lax.fori_loop(0, d // lanes, body,
                                     tuple(jnp.zeros((lanes,), F32) for _ in range(SC_ROWS)))
                for r in range(SC_ROWS):
                    acc_v[r, :] = accs[r]
                for g in range(SC_ROWS // lanes):
                    tot = jnp.zeros((lanes,), F32)
                    for l in range(lanes):
                        tot = tot + plsc.load_gather(
                            acc_v, [lane + g * lanes, jnp.full((lanes,), l, jnp.int32)])
                    out_v[i, pl.ds(c * SC_ROWS + g * lanes, lanes)] = tot

            _sc_pipeline(idx_v, tab_hbm, rows_v, sems, compute)
            pltpu.sync_copy(out_v, out_hbm.at[pl.ds(out0, SC_TOK_BLOCK)])

    return k(xn, idx, down)


def _peer_up_sum(coef, idx, up, tok_start):
    t_all = coef.shape[0]
    t = t_all - tok_start
    d = up.shape[1]
    tok_w = t // SC_WORKERS
    nblk = tok_w // SC_TOK_BLOCK
    assert nblk * SC_TOK_BLOCK * SC_WORKERS == t
    lanes = SC_LANES
    nj = SC_QUARTER // lanes

    @functools.partial(
        pl.kernel, mesh=_sc_mesh(),
        out_type=jax.ShapeDtypeStruct((t, d), F32),
        scratch_types=[
            pltpu.VMEM((SC_TOK_BLOCK, PEER_E), jnp.int32),
            pltpu.VMEM((SC_TOK_BLOCK * PEER_E,), F32),
            pltpu.VMEM((2, SC_ROWS, d), F32),
            pltpu.VMEM((SC_TOK_BLOCK, d), F32),
            pltpu.SemaphoreType.DMA((2,)),
        ],
        compiler_params=pltpu.CompilerParams(needs_layout_passes=False),
        name="peer_up_sum",
    )
    def k(coef_hbm, idx_hbm, tab_hbm, out_hbm, idx_v, coef_v, rows_v, out_v, sems):
        wid = lax.axis_index("s") * SC_CORES + lax.axis_index("c")

        @pl.loop(0, nblk)
        def _(blk):
            out0 = wid * tok_w + blk * SC_TOK_BLOCK
            tok0 = tok_start + out0
            pltpu.sync_copy(idx_hbm.at[pl.ds(tok0, SC_TOK_BLOCK)], idx_v)
            pltpu.sync_copy(coef_hbm.at[pl.ds(tok0 * PEER_E, SC_TOK_BLOCK * PEER_E)], coef_v)

            def compute(i, c, step, buf):
                for q in range(d // SC_QUARTER):
                    col = lambda j: pl.ds(q * SC_QUARTER + j * lanes, lanes)
                    if c == 0:
                        init = tuple(jnp.zeros((lanes,), F32) for _ in range(nj))
                    else:
                        init = tuple(out_v[i, col(j)] for j in range(nj))

                    def body(r, accs):
                        cf = plsc.load_gather(
                            coef_v, [jnp.full((lanes,), step * SC_ROWS + r, jnp.int32)])
                        return tuple(a + rows_v[buf, r, col(j)] * cf for j, a in enumerate(accs))
                    accs = lax.fori_loop(0, SC_ROWS, body, init)
                    for j in range(nj):
                        out_v[i, col(j)] = accs[j]

            _sc_pipeline(idx_v, tab_hbm, rows_v, sems, compute)
            pltpu.sync_copy(out_v, out_hbm.at[pl.ds(out0, SC_TOK_BLOCK)])

    return k(coef.reshape(t_all * PEER_E), idx, up)


TC_SUB = 128
TC_SUBS = 2
TC_STEP = TC_SUB * TC_SUBS
TC_TRIP = 64
TC_KCHUNKS = 4
ROW_WORDS = 4
ROW_HALVES = 2 * ROW_WORDS
STACK_COLS = PEER_E * ROW_HALVES
FRONT_PARTS = 8
SC_DOWN_PARTS = 4
SC_DOWN_EXTRA = 1024
PEER_TC_UP_TOKENS = 25088


def _pack_rows_bf16(tab):
    n, d = tab.shape
    return pl.pallas_call(
        _pack_kernel,
        grid=(n // PACK_ROWS,),
        in_specs=[pl.BlockSpec((PACK_ROWS, d), lambda i: (i, 0))],
        out_specs=pl.BlockSpec((PACK_ROWS * ROW_WORDS, LANES), lambda i: (i, 0)),
        out_shape=jax.ShapeDtypeStruct((n * ROW_WORDS, LANES), jnp.uint32),
        compiler_params=pltpu.CompilerParams(dimension_semantics=("arbitrary",)),
        name="pack_rows_bf16",
    )(tab)


PACK_ROWS = 512


def _pack_kernel(x_ref, o_ref):
    u = pltpu.bitcast(x_ref[...], jnp.uint32)
    r = (u + jnp.uint32(0x7FFF) + ((u >> 16) & jnp.uint32(1))) >> 16
    for s in range(ROW_WORDS):
        lo = r[:, 2 * LANES * s:2 * LANES * s + LANES]
        hi = r[:, 2 * LANES * s + LANES:2 * LANES * (s + 1)]
        o_ref[pl.ds(s, PACK_ROWS, stride=ROW_WORDS), :] = lo | (hi << 16)


def _load_table_once(tab_hbm, tab_v, sem):
    @pl.when(pl.program_id(0) == 0)
    def _():
        cp = pltpu.make_async_copy(tab_hbm, tab_v, sem)
        cp.start()
        cp.wait()


def _tc_token_pipeline(idx_hbm, idx_s, isem, tab_v, per_token):
    step = pl.program_id(0)

    def idx_copy(block, slot):
        return pltpu.make_async_copy(idx_hbm.at[block], idx_s[slot], isem.at[slot])

    @pl.when(step == 0)
    def _():
        idx_copy(0, 0).start()

    for sub in range(TC_SUBS):
        block = step * TC_SUBS + sub
        idx_copy(block, sub).wait()
        if sub + 1 < TC_SUBS:
            idx_copy(block + 1, sub + 1).start()
        else:
            @pl.when(step + 1 < pl.num_programs(0))
            def _():
                idx_copy(block + 1, 0).start()
        offs = idx_s[sub]

        def chunks(i, offs=offs):
            per = PEER_E // TC_KCHUNKS
            for ch in range(TC_KCHUNKS):
                pieces = [tab_v[pl.ds(pl.multiple_of(offs.at[k][i], ROW_WORDS), ROW_WORDS), :]
                          for k in range(ch * per, (ch + 1) * per)]
                yield pltpu.bitcast(jnp.concatenate(pieces, axis=0), BF16)

        def trip(ip, carry, sub=sub, chunks=chunks):
            for u in range(TC_TRIP):
                i = ip * TC_TRIP + u
                per_token(sub * TC_SUB + i, chunks(i))
            return carry
        lax.fori_loop(0, TC_SUB // TC_TRIP, trip, 0)


def _tc_peer_scratch(tabp):
    return [pltpu.VMEM(tabp.shape, jnp.uint32),
            [pltpu.SMEM((PEER_E, TC_SUB), jnp.int32) for _ in range(TC_SUBS)],
            pltpu.SemaphoreType.DMA,
            pltpu.SemaphoreType.DMA((TC_SUBS,))]


def _stack_selectors():
    col = np.arange(STACK_COLS)
    diag = (col[None, :] % ROW_HALVES == np.arange(ROW_HALVES)[:, None]).astype(np.float32)
    owner = (col[:, None] // ROW_HALVES == np.arange(PEER_E)[None, :]).astype(np.float32)
    return jnp.asarray(diag), jnp.asarray(owner)


def _tc_down_kernel(idx_hbm, x_ref, tab_hbm, diag_ref, fold_ref, o_ref,
                    tab_v, idx_s, sem, isem, z_ref):
    _load_table_once(tab_hbm, tab_v, sem)
    cols = STACK_COLS // TC_KCHUNKS

    def per_token(t, chunks):
        xr = x_ref[pl.ds(t, 1), :]
        x8 = jnp.concatenate([xr[:, LANES * j:LANES * (j + 1)] for j in range(ROW_HALVES)],
                             axis=0).astype(BF16)
        parts = []
        for ch, rows in enumerate(chunks):
            a = lax.dot_general(x8, rows, (((1,), (1,)), ((), ())), preferred_element_type=F32)
            parts.append(jnp.sum(a * diag_ref[:, cols * ch:cols * (ch + 1)], axis=0, keepdims=True))
        z_ref[pl.ds(t, 1), :] = jnp.concatenate(parts, axis=1)

    _tc_token_pipeline(idx_hbm, idx_s, isem, tab_v, per_token)
    o_ref[...] = _dot_exact(z_ref[...], fold_ref[...])


def _peer_down_dots_tc(xn, offs, tabp, n_tok):
    diag, owner = _stack_selectors()
    full = lambda a: pl.BlockSpec(a.shape, lambda i: (0,) * a.ndim)
    return pl.pallas_call(
        _tc_down_kernel,
        grid=(n_tok // TC_STEP,),
        in_specs=[pl.BlockSpec(memory_space=pl.ANY),
                  pl.BlockSpec((TC_STEP, xn.shape[1]), lambda i: (i, 0)),
                  pl.BlockSpec(memory_space=pl.ANY), full(diag), full(owner)],
        out_specs=pl.BlockSpec((TC_STEP, PEER_E), lambda i: (i, 0)),
        out_shape=jax.ShapeDtypeStruct((n_tok, PEER_E), F32),
        scratch_shapes=_tc_peer_scratch(tabp) + [pltpu.VMEM((TC_STEP, STACK_COLS), F32)],
        compiler_params=pltpu.CompilerParams(
            dimension_semantics=("arbitrary",), vmem_limit_bytes=VMEM_LIMIT),
        name="peer_down_dots_tc",
    )(offs, xn, tabp, diag, owner)


def _tc_up_kernel(idx_hbm, coef_ref, tab_hbm, spread_ref, diag_ref, o_ref,
                  tab_v, idx_s, sem, isem, ce_ref):
    _load_table_once(tab_hbm, tab_v, sem)
    c = coef_ref[...]
    c_hi = c.astype(BF16)
    c_lo = (c - c_hi.astype(F32)).astype(BF16)
    ce_ref[0] = jnp.dot(c_hi, spread_ref[...], preferred_element_type=F32)
    ce_ref[1] = jnp.dot(c_lo, spread_ref[...], preferred_element_type=F32)

    cols = STACK_COLS // TC_KCHUNKS

    def per_token(t, chunks):
        d = diag_ref[...]
        lhs = jnp.concatenate([(ce_ref[0, pl.ds(t, 1), :] * d).astype(BF16),
                               (ce_ref[1, pl.ds(t, 1), :] * d).astype(BF16)], axis=0)
        r = jnp.zeros((2 * ROW_HALVES, LANES), F32)
        for ch, rows in enumerate(chunks):
            r = r + jnp.dot(lhs[:, cols * ch:cols * (ch + 1)], rows, preferred_element_type=F32)
        out8 = r[0:ROW_HALVES] + r[ROW_HALVES:2 * ROW_HALVES]
        o_ref[pl.ds(t, 1), :] = jnp.concatenate(
            [out8[j:j + 1, :] for j in range(ROW_HALVES)], axis=1)

    _tc_token_pipeline(idx_hbm, idx_s, isem, tab_v, per_token)


def _peer_up_sum_tc(coef, offs, tabp, n_tok):
    d = ROW_HALVES * LANES
    diag, owner = _stack_selectors()
    spread = owner.T.astype(BF16)
    full = lambda a: pl.BlockSpec(a.shape, lambda i: (0,) * a.ndim)
    return pl.pallas_call(
        _tc_up_kernel,
        grid=(n_tok // TC_STEP,),
        in_specs=[pl.BlockSpec(memory_space=pl.ANY),
                  pl.BlockSpec((TC_STEP, PEER_E), lambda i: (i, 0)),
                  pl.BlockSpec(memory_space=pl.ANY), full(spread), full(diag)],
        out_specs=pl.BlockSpec((TC_STEP, d), lambda i: (i, 0)),
        out_shape=jax.ShapeDtypeStruct((n_tok, d), F32),
        scratch_shapes=_tc_peer_scratch(tabp) + [pltpu.VMEM((2, TC_STEP, STACK_COLS), F32)],
        compiler_params=pltpu.CompilerParams(
            dimension_semantics=("arbitrary",), vmem_limit_bytes=VMEM_LIMIT),
        name="peer_up_sum_tc",
    )(offs, coef, tabp, spread, diag)


def _coef_kernel(pre_ref, gate_ref, o_ref):
    x = pre_ref[...]
    gelu = 0.5 * x * (1.0 + lax.erf(x * (2.0 ** -0.5)))
    o_ref[...] = gate_ref[...] * gelu


def _peer_coef(pre, gates):
    t, e = pre.shape
    row = pl.BlockSpec((TM_COEF, e), lambda i: (i, 0))
    return pl.pallas_call(
        _coef_kernel,
        grid=(t // TM_COEF,),
        in_specs=[row, row],
        out_specs=row,
        out_shape=jax.ShapeDtypeStruct((t, e), F32),
        compiler_params=pltpu.CompilerParams(dimension_semantics=("arbitrary",)),
        name="peer_gate_gelu",
    )(pre, gates)


def _front_half(x2, row0, rows, bsz, seq, w_pad, layer, i):
    (norm1_g, ret_norm_g, m_conv_w, m_conv_b, m_wq, m_wk, m_b_i, m_b_f, m_norm_g, m_skip,
     w_out, norm2_g, peer_wq, keys) = layer
    d = x2.shape[1]
    proj = _rms_proj(x2, row0, rows, norm1_g[i].reshape(1, d), w_pad)
    y = _mixer(proj, bsz, seq, ret_norm_g[i], m_conv_w[i], m_conv_b[i], m_wq[i], m_wk[i],
               m_b_i[i], m_b_f[i], m_norm_g[i], m_skip[i])
    x1, xn, sub_t = _post(y, x2, row0, w_out, norm2_g[i].reshape(1, d), peer_wq, keys)
    return (x1, xn) + tuple(_peer_route(sub_t))


def kernel(x, norm1_g, w_in, ret_norm_g, m_conv_w, m_conv_b, m_wq, m_wk, m_b_i, m_b_f,
           m_norm_g, m_skip, w_out, norm2_g, peer_wq, peer_keys, peer_down, peer_up, final_g):
    b, s, d = x.shape
    t = b * s
    tp = t // FRONT_PARTS
    x2 = x.reshape(t, d)
    depth = norm1_g.shape[0]
    for i in range(depth):
        if i > 0:
            x2 = jnp.concatenate(x1s, axis=0) + jnp.concatenate([p_tc, p_sc], axis=0)
        wi = w_in[i]
        n_main = OFF_GI
        gate_pad = lambda w: jnp.pad(w, ((0, 0), (0, LANES - w.shape[1])))
        w_pad = jnp.concatenate(
            [wi[:, :n_main], gate_pad(wi[:, n_main:n_main + MLSTM_HEADS]),
             gate_pad(wi[:, n_main + MLSTM_HEADS:])], axis=1).astype(BF16)
        keys = peer_keys[i].reshape(2 * PEER_HEADS, PEER_NKEYS, PEER_HALF).astype(BF16)
        layer = (norm1_g, ret_norm_g, m_conv_w, m_conv_b, m_wq, m_wk, m_b_i, m_b_f, m_norm_g,
                 m_skip, w_out[i].astype(BF16), norm2_g, peer_wq[i].astype(BF16), keys)
        down_p = _pack_rows_bf16(peer_down[i])
        up_p = _pack_rows_bf16(peer_up[i])

        fronts = []
        pre = [None] * FRONT_PARTS
        for q in range(FRONT_PARTS):
            fr = _front_half(x2, q * tp, tp, b // FRONT_PARTS, s, w_pad, layer, i)
            fronts.append(fr)
            if q < SC_DOWN_PARTS:
                pre[q] = [_peer_down_dots(fr[1], fr[2], peer_down[i], 0)]
            elif q == SC_DOWN_PARTS and SC_DOWN_EXTRA:
                pre[q] = [None, _peer_down_dots(fr[1], fr[2], peer_down[i], tp - SC_DOWN_EXTRA)]
        for q in range(SC_DOWN_PARTS, FRONT_PARTS):
            n_tc_q = tp - SC_DOWN_EXTRA if q == SC_DOWN_PARTS else tp
            pre_tc = _peer_down_dots_tc(fronts[q][1], fronts[q][4], down_p, n_tc_q)
            if pre[q] is None:
                pre[q] = [pre_tc]
            else:
                pre[q][0] = pre_tc
        pre = [piece for part in pre for piece in part]
        x1s = [fr[0] for fr in fronts]
        expert = jnp.concatenate([fr[2] for fr in fronts], axis=0)
        gates = jnp.concatenate([fr[3] for fr in fronts], axis=0)
        offs = jnp.concatenate([fr[4] for fr in fronts], axis=0)
        coef = _peer_coef(jnp.concatenate(pre, axis=0), gates)
        n_tc = PEER_TC_UP_TOKENS
        p_sc = _peer_up_sum(coef, expert, peer_up[i], n_tc)
        p_tc = _peer_up_sum_tc(coef, offs, up_p, n_tc)
    out = _final(_row_ranges(x1s, [p_tc, p_sc]), final_g.reshape(1, d), t)
    return out.reshape(b, s, d)


def _row_ranges(xs, ps):
    def starts(arrs):
        out = [0]
        for a in arrs:
            out.append(out[-1] + a.shape[0])
        return out
    xb, pb = starts(xs), starts(ps)
    assert xb[-1] == pb[-1]
    cuts = sorted(set(xb) | set(pb))
    parts = []
    for lo, hi in zip(cuts[:-1], cuts[1:]):
        xi = max(k for k in range(len(xs)) if xb[k] <= lo)
        pi = max(k for k in range(len(ps)) if pb[k] <= lo)
        parts.append((xs[xi], lo - xb[xi], ps[pi], lo - pb[pi], hi - lo))
    return parts
```

```python
import functools

import numpy as np
import jax
import jax.numpy as jnp
from jax import lax
from jax.experimental import pallas as pl
from jax.experimental.pallas import tpu as pltpu
from jax.experimental.pallas import tpu_sc as plsc

F32 = jnp.float32
BF16 = jnp.bfloat16

D_MODEL = 1024
CHUNK = 64
RET_HEADS = 4
RET_DK = 128
MLSTM_HEADS = 4
MLSTM_D = 128
HEAD_W = 128
D_GROUP = 512
CONV_W = 4
ROPE_BASE = 10000.0
PEER_HEADS = 8
PEER_NKEYS = 128
PEER_TOPK = 16
PEER_HALF = 128
PEER_BLOCK = 128
EPS = 1e-6

OFF_RQ, OFF_RK, OFF_RV, OFF_RG = 0, 512, 1024, 1536
OFF_MX, OFF_MV, OFF_MO = 2048, 2560, 3072
OFF_GI, OFF_GF = 3584, 3712
D_PROJ_PAD = 3840

LANES = 128
CONV_TAIL = 8
VMEM_LIMIT = 56 * 1024 * 1024

TM_PROJ = 512
L_BLOCK = 256
TM_FINAL = 512
TM_COEF = 2048


def _rms(x, g):
    ms = jnp.mean(x * x, axis=-1, keepdims=True)
    return x * lax.rsqrt(ms + EPS) * g


def _group_norm(h, g):
    mu = jnp.mean(h, axis=-1, keepdims=True)
    d = h - mu
    var = jnp.mean(d * d, axis=-1, keepdims=True)
    return d * lax.rsqrt(var + EPS) * g


def _silu(x):
    return x * (1.0 / (1.0 + jnp.exp(-x)))


def _sigmoid(x):
    return 1.0 / (1.0 + jnp.exp(-x))


def _dot(a, b):
    return jnp.dot(a.astype(BF16), b.astype(BF16), preferred_element_type=F32)


def _dot_tn(a, b):
    return lax.dot_general(a.astype(BF16), b.astype(BF16), (((0,), (0,)), ((), ())),
                           preferred_element_type=F32)


def _dot_nt(a, b):
    return lax.dot_general(a.astype(BF16), b.astype(BF16), (((1,), (1,)), ((), ())),
                           preferred_element_type=F32)


def _rms_proj_kernel(x_ref, g_ref, w_ref, o_ref):
    h = _rms(x_ref[...], g_ref[...])
    o_ref[...] = jnp.dot(h.astype(BF16), w_ref[...], preferred_element_type=F32)


def _rms_proj(x2, row0, rows, g, w):
    d = x2.shape[1]
    t = rows
    n = w.shape[1]
    b0 = row0 // TM_PROJ
    return pl.pallas_call(
        _rms_proj_kernel,
        grid=(t // TM_PROJ,),
        in_specs=[
            pl.BlockSpec((TM_PROJ, d), lambda i: (i + b0, 0)),
            pl.BlockSpec((1, d), lambda i: (0, 0)),
            pl.BlockSpec((d, n), lambda i: (0, 0)),
        ],
        out_specs=pl.BlockSpec((TM_PROJ, n), lambda i: (i, 0)),
        out_shape=jax.ShapeDtypeStruct((t, n), F32),
        compiler_params=pltpu.CompilerParams(
            dimension_semantics=("arbitrary",), vmem_limit_bytes=VMEM_LIMIT),
        name="rms_in_proj",
    )(x2, g, w)


def _mixer_kernel(chunk_decay,
                  proj_ref, cos_ref, sin_ref, intra_ref, qd_ref, kd_ref, tri_ref,
                  rng_ref, cw_ref, cb_ref, wq_ref, wk_ref, bi_ref, bf_ref,
                  mng_ref, skip_ref,
                  y_ref,
                  rstate_ref, cstate_ref, m_ref, tail_ref):
    j = pl.program_id(1)
    n_chunks = L_BLOCK // CHUNK

    @pl.when(j == 0)
    def _():
        rstate_ref[...] = jnp.zeros_like(rstate_ref)
        cstate_ref[...] = jnp.zeros_like(cstate_ref)
        m_ref[...] = jnp.zeros_like(m_ref)
        tail_ref[...] = jnp.zeros_like(tail_ref)

    cosv = cos_ref[...]
    sinv = sin_ref[...]
    k_scale = RET_DK ** -0.5

    for h in range(RET_HEADS):
        lo = h * HEAD_W
        q = proj_ref[:, OFF_RQ + lo:OFF_RQ + lo + HEAD_W]
        k = proj_ref[:, OFF_RK + lo:OFF_RK + lo + HEAD_W]
        q = q * cosv + pltpu.roll(q, HEAD_W // 2, 1) * sinv
        k = (k * cosv + pltpu.roll(k, HEAD_W // 2, 1) * sinv) * k_scale
        intra = intra_ref[h]
        qd = qd_ref[h]
        kd = kd_ref[h]
        g = rng_ref[:, lo:lo + HEAD_W]
        for c in range(n_chunks):
            r0 = c * CHUNK
            qc = q[r0:r0 + CHUNK]
            kc = k[r0:r0 + CHUNK]
            vc = proj_ref[r0:r0 + CHUNK, OFF_RV + lo:OFF_RV + lo + HEAD_W]
            gate = proj_ref[r0:r0 + CHUNK, OFF_RG + lo:OFF_RG + lo + HEAD_W]
            state = rstate_ref[h]
            scores = _dot_nt(qc, kc) * intra
            o = _dot(scores, vc) + _dot(qc, state) * qd
            rstate_ref[h] = chunk_decay[h] * state + _dot_tn(kc * kd, vc)
            y_ref[r0:r0 + CHUNK, lo:lo + HEAD_W] = _silu(gate) * _group_norm(o, g)

    mx = proj_ref[:, OFF_MX:OFF_MX + D_GROUP]
    xp = jnp.concatenate([tail_ref[...], mx], axis=0)
    base = CONV_TAIL - (CONV_W - 1)
    conv = cb_ref[...] + xp[base:base + L_BLOCK] * cw_ref[0:1, :]
    for t in range(1, CONV_W):
        conv = conv + xp[base + t:base + t + L_BLOCK] * cw_ref[t:t + 1, :]
    tail_ref[...] = mx[L_BLOCK - CONV_TAIL:L_BLOCK]
    cact = _silu(conv)

    gi = proj_ref[:, OFF_GI:OFF_GI + LANES] + bi_ref[...]
    gf = proj_ref[:, OFF_GF:OFF_GF + LANES] + bf_ref[...]
    logf = jnp.minimum(gf, 0.0) - jnp.log1p(jnp.exp(-jnp.abs(gf)))
    tri = tri_ref[...]
    lane = lax.broadcasted_iota(jnp.int32, (CHUNK, HEAD_W), 1)
    ones_col = (lane == 0).astype(F32)
    m_scale = MLSTM_D ** -0.5

    mq = []
    mk = []
    for h in range(MLSTM_HEADS):
        lo = h * HEAD_W
        ch = cact[:, lo:lo + HEAD_W]
        mq.append(_dot(ch, wq_ref[h]))
        mk.append(_dot(ch, wk_ref[h]) * m_scale)

    for c in range(n_chunks):
        r0 = c * CHUNK
        fcum = jnp.dot(tri, logf[r0:r0 + CHUNK], preferred_element_type=F32,
                       precision=lax.Precision.HIGHEST)
        a = fcum[CHUNK - 1:CHUNK]
        logw = a - fcum + gi[r0:r0 + CHUNK]
        m_in = jnp.max(logw, axis=0, keepdims=True)
        m_old = m_ref[...]
        m_new = jnp.maximum(a + m_old, m_in)
        decay = jnp.exp(a + m_old - m_new)
        w = jnp.exp(logw - m_new)
        em = jnp.exp(-m_new)
        m_ref[...] = m_new
        for h in range(MLSTM_HEADS):
            lo = h * HEAD_W
            qh = mq[h][r0:r0 + CHUNK]
            kh = mk[h][r0:r0 + CHUNK]
            vh = proj_ref[r0:r0 + CHUNK, OFF_MV + lo:OFF_MV + lo + HEAD_W]
            oh = proj_ref[r0:r0 + CHUNK, OFF_MO + lo:OFF_MO + lo + HEAD_W]
            ch = cact[r0:r0 + CHUNK, lo:lo + HEAD_W]
            v_ext = jnp.concatenate([vh, ones_col], axis=1)
            cmem = decay[:, h:h + 1] * cstate_ref[h] + _dot_tn(kh * w[:, h:h + 1], v_ext)
            cstate_ref[h] = cmem
            num = _dot(qh, cmem)
            den = jnp.maximum(jnp.abs(num[:, HEAD_W:HEAD_W + 1]), em[:, h:h + 1])
            hh = num[:, :HEAD_W] / den
            hm = _group_norm(hh, mng_ref[:, lo:lo + HEAD_W])
            ym = _sigmoid(oh) * (hm + skip_ref[:, lo:lo + HEAD_W] * ch)
            y_ref[r0:r0 + CHUNK, D_GROUP + lo:D_GROUP + lo + HEAD_W] = ym


def _retention_tables():
    h = np.arange(RET_HEADS, dtype=np.float64)
    log_g = np.log(1.0 - 2.0 ** (-5.0 - h))
    l = np.arange(CHUNK, dtype=np.float64)
    intra = np.exp(log_g[:, None, None] * np.abs(l[:, None] - l[None, :]))
    qd = np.exp(log_g[:, None] * (l + 1.0))[:, :, None] * np.ones((1, 1, HEAD_W))
    kd = np.exp(log_g[:, None] * (CHUNK - 1.0 - l))[:, :, None] * np.ones((1, 1, HEAD_W))
    cd = np.exp(log_g * CHUNK)
    return (jnp.asarray(intra, F32), jnp.asarray(qd, F32), jnp.asarray(kd, F32),
            tuple(float(v) for v in cd))


def _rotary_tables(s):
    half = HEAD_W // 2
    inv = ROPE_BASE ** (-np.arange(half, dtype=np.float64) / half)
    ang = np.arange(s, dtype=np.float64)[:, None] * inv[None, :]
    cos = np.cos(ang)
    sin = np.sin(ang)
    return (jnp.asarray(np.concatenate([cos, cos], axis=-1), F32),
            jnp.asarray(np.concatenate([-sin, sin], axis=-1), F32))


def _mixer(proj, b, s, ret_norm_g, m_conv_w, m_conv_b, m_wq, m_wk, m_b_i, m_b_f,
           m_norm_g, m_skip):
    intra, qd, kd, cd = _retention_tables()
    cos_t, sin_t = _rotary_tables(s)
    tri = jnp.asarray(np.tril(np.ones((CHUNK, CHUNK))), F32)
    nj = s // L_BLOCK
    pad = lambda v: jnp.pad(v.reshape(1, -1), ((0, 0), (0, LANES - v.size)))
    full = lambda shape: pl.BlockSpec(shape, lambda bi, j: (0,) * len(shape))
    return pl.pallas_call(
        functools.partial(_mixer_kernel, cd),
        grid=(b, nj),
        in_specs=[
            pl.BlockSpec((L_BLOCK, D_PROJ_PAD), lambda bi, j: (bi * nj + j, 0)),
            pl.BlockSpec((L_BLOCK, HEAD_W), lambda bi, j: (j, 0)),
            pl.BlockSpec((L_BLOCK, HEAD_W), lambda bi, j: (j, 0)),
            full((RET_HEADS, CHUNK, CHUNK)),
            full((RET_HEADS, CHUNK, HEAD_W)),
            full((RET_HEADS, CHUNK, HEAD_W)),
            full((CHUNK, CHUNK)),
            full((1, D_GROUP)),
            full((CONV_W, D_GROUP)),
            full((1, D_GROUP)),
            full((MLSTM_HEADS, MLSTM_D, MLSTM_D)),
            full((MLSTM_HEADS, MLSTM_D, MLSTM_D)),
            full((1, LANES)),
            full((1, LANES)),
            full((1, D_GROUP)),
            full((1, D_GROUP)),
        ],
        out_specs=pl.BlockSpec((L_BLOCK, 2 * D_GROUP), lambda bi, j: (bi * nj + j, 0)),
        out_shape=jax.ShapeDtypeStruct((b * s, 2 * D_GROUP), F32),
        scratch_shapes=[
            pltpu.VMEM((RET_HEADS, RET_DK, HEAD_W), F32),
            pltpu.VMEM((MLSTM_HEADS, MLSTM_D, 2 * HEAD_W), F32),
            pltpu.VMEM((1, LANES), F32),
            pltpu.VMEM((CONV_TAIL, D_GROUP), F32),
        ],
        compiler_params=pltpu.CompilerParams(
            dimension_semantics=("arbitrary", "arbitrary"), vmem_limit_bytes=VMEM_LIMIT),
        name="retention_mlstm_mixer",
    )(proj, cos_t, sin_t, intra, qd, kd, tri,
      ret_norm_g.reshape(1, D_GROUP), m_conv_w, m_conv_b.reshape(1, D_GROUP),
      m_wq.astype(BF16), m_wk.astype(BF16), pad(m_b_i), pad(m_b_f),
      m_norm_g.reshape(1, D_GROUP), m_skip.reshape(1, D_GROUP))


def _post_kernel(y_ref, x_ref, wout_ref, g2_ref, wq_ref, keys_ref,
                 x1_ref, xn_ref, sub_ref):
    x1 = x_ref[...] + jnp.dot(y_ref[...].astype(BF16), wout_ref[...],
                              preferred_element_type=F32)
    x1_ref[...] = x1
    xn = _rms(x1, g2_ref[...])
    xn_ref[...] = xn
    q = jnp.dot(xn.astype(BF16), wq_ref[...], preferred_element_type=F32)
    for i in range(2 * PEER_HEADS):
        lo = i * PEER_HALF
        sub_ref[lo:lo + PEER_NKEYS, :] = _dot_nt(keys_ref[i], q[:, lo:lo + PEER_HALF])


def _post(y, x2, row0, w_out, norm2_g, peer_wq, keys):
    t, d = y.shape
    nq = peer_wq.shape[1]
    b0 = row0 // TM_PROJ
    row = lambda w: pl.BlockSpec((TM_PROJ, w), lambda i: (i, 0))
    full = lambda shape: pl.BlockSpec(shape, lambda i: (0,) * len(shape))
    return pl.pallas_call(
        _post_kernel,
        grid=(t // TM_PROJ,),
        in_specs=[row(d), pl.BlockSpec((TM_PROJ, d), lambda i: (i + b0, 0)), full((d, d)),
                  full((1, d)), full((d, nq)), full(keys.shape)],
        out_specs=[row(d), row(d), pl.BlockSpec((nq, TM_PROJ), lambda i: (0, i))],
        out_shape=[jax.ShapeDtypeStruct((t, d), F32), jax.ShapeDtypeStruct((t, d), F32),
                   jax.ShapeDtypeStruct((nq, t), F32)],
        compiler_params=pltpu.CompilerParams(
            dimension_semantics=("arbitrary",), vmem_limit_bytes=VMEM_LIMIT),
        name="out_proj_peer_scores",
    )(y, x2, w_out, norm2_g, peer_wq, keys)


def _final_kernel(x1_ref, p_ref, g_ref, *rest):
    o_ref = rest[-1]
    o_ref[...] = _rms(x1_ref[...] + p_ref[...], g_ref[...])


def _final(parts, g, t):
    d = g.shape[1]
    spec = lambda r0: pl.BlockSpec((TM_FINAL, d), lambda i, b0=r0 // TM_FINAL: (i + b0, 0))
    out = None
    row0 = 0
    for x1, x_row0, p, p_row0, rows in parts:
        in_specs = [spec(x_row0), spec(p_row0), pl.BlockSpec((1, d), lambda i: (0, 0))]
        args = [x1, p, g]
        if out is not None:
            in_specs.append(pl.BlockSpec(memory_space=pl.ANY))
            args.append(out)
        out = pl.pallas_call(
            _final_kernel,
            grid=(rows // TM_FINAL,),
            in_specs=in_specs,
            out_specs=spec(row0),
            out_shape=jax.ShapeDtypeStruct((t, d), F32),
            input_output_aliases={} if len(args) == 3 else {3: 0},
            compiler_params=pltpu.CompilerParams(dimension_semantics=("arbitrary",)),
            name="final_rmsnorm",
        )(*args)
        row0 += rows
    assert row0 == t
    return out


TM_ROUTE = 256
CAND_ROWS = 56


def _candidate_tables():
    pairs = [(k1, k2) for k1 in range(PEER_TOPK) for k2 in range(PEER_TOPK)
             if (k1 + 1) * (k2 + 1) <= PEER_TOPK]
    assert len(pairs) <= CAND_ROWS
    rep = np.zeros((CAND_ROWS, 2 * PEER_TOPK), np.float32)
    til = np.zeros((CAND_ROWS, 2 * PEER_TOPK), np.float32)
    for pos, (k1, k2) in enumerate(pairs):
        rep[pos, k1] = 1.0
        til[pos, PEER_TOPK + k2] = 1.0
    return jnp.asarray(rep), jnp.asarray(til), len(pairs)


def _dot_exact(a, b):
    return jnp.dot(a, b, preferred_element_type=F32, precision=lax.Precision.HIGHEST)


def _route_kernel(n_cand, sub_ref, rep_ref, til_ref, seg_ref, exp_ref, gate_ref, offs_ref,
                  s_ref, i_ref, best_ref, eid_ref, m0_ref):
    tm = sub_ref.shape[1]
    neg = -jnp.inf
    key_id = lax.broadcasted_iota(jnp.int32, (PEER_NKEYS, tm), 0).astype(F32)
    cand_id = lax.broadcasted_iota(jnp.int32, (CAND_ROWS, tm), 0).astype(F32)
    rep = rep_ref[...]
    til = til_ref[...]
    for h in range(PEER_HEADS):
        for p in range(2):
            lo = (2 * h + p) * PEER_NKEYS
            x = sub_ref[lo:lo + PEER_NKEYS, :]
            for k in range(PEER_TOPK):
                m = jnp.max(x, axis=0, keepdims=True)
                idx = jnp.min(jnp.where(x == m, key_id, float(PEER_NKEYS)), axis=0, keepdims=True)
                x = jnp.where(key_id == idx, neg, x)
                r = p * PEER_TOPK + k
                s_ref[r:r + 1, :] = m
                i_ref[r:r + 1, :] = idx
        s = s_ref[...]
        iv = i_ref[...]
        cand = _dot_exact(rep, s) + _dot_exact(til, s)
        cand = jnp.where(cand_id < float(n_cand), cand, neg)
        eid = _dot(rep, iv) * float(PEER_NKEYS) + _dot(til, iv)
        for k in range(PEER_TOPK):
            m = jnp.max(cand, axis=0, keepdims=True)
            pos = jnp.min(jnp.where(cand == m, cand_id, float(CAND_ROWS)), axis=0, keepdims=True)
            sel = cand_id == pos
            e = jnp.max(jnp.where(sel, eid, -1.0), axis=0, keepdims=True)
            cand = jnp.where(sel, neg, cand)
            r = h * PEER_TOPK + k
            best_ref[r:r + 1, :] = m
            eid_ref[r:r + 1, :] = e
            if k == 0:
                m0_ref[r:r + PEER_TOPK, :] = jnp.broadcast_to(m, (PEER_TOPK, tm))
    pexp = jnp.exp(best_ref[...] - m0_ref[...])
    gates = pexp / _dot_exact(seg_ref[...], pexp)
    gate_ref[...] = gates.T
    eid = eid_ref[...].astype(jnp.int32)
    exp_ref[...] = eid.T
    for blk in range(tm // LANES):
        offs_ref[blk] = eid[:, blk * LANES:(blk + 1) * LANES] * ROW_WORDS


def _peer_route(sub_t):
    nq, t = sub_t.shape
    rep, til, n_cand = _candidate_tables()
    e = PEER_HEADS * PEER_TOPK
    seg = jnp.asarray(np.kron(np.eye(PEER_HEADS), np.ones((PEER_TOPK, PEER_TOPK))), F32)
    full = lambda a: pl.BlockSpec(a.shape, lambda i: (0,) * a.ndim)
    out = pl.BlockSpec((TM_ROUTE, e), lambda i: (i, 0))
    return pl.pallas_call(
        functools.partial(_route_kernel, n_cand),
        grid=(t // TM_ROUTE,),
        in_specs=[pl.BlockSpec((nq, TM_ROUTE), lambda i: (0, i)), full(rep), full(til), full(seg)],
        out_specs=[out, out, pl.BlockSpec((TM_ROUTE // LANES, e, LANES), lambda i: (i, 0, 0))],
        out_shape=[jax.ShapeDtypeStruct((t, e), jnp.int32), jax.ShapeDtypeStruct((t, e), F32),
                   jax.ShapeDtypeStruct((t // LANES, e, LANES), jnp.int32)],
        scratch_shapes=[
            pltpu.VMEM((2 * PEER_TOPK, TM_ROUTE), F32),
            pltpu.VMEM((2 * PEER_TOPK, TM_ROUTE), F32),
            pltpu.VMEM((e, TM_ROUTE), F32),
            pltpu.VMEM((e, TM_ROUTE), F32),
            pltpu.VMEM((e, TM_ROUTE), F32),
        ],
        compiler_params=pltpu.CompilerParams(dimension_semantics=("arbitrary",)),
        name="peer_topk_route",
    )(sub_t, rep, til, seg)


SC_CORES = 2
SC_SUBCORES = 16
SC_LANES = 16
SC_WORKERS = SC_CORES * SC_SUBCORES
PEER_E = PEER_HEADS * PEER_TOPK
SC_ROWS = 32
SC_GATHERS = PEER_E // SC_ROWS
SC_TOK_BLOCK = 8
SC_QUARTER = 256


def _sc_mesh():
    return plsc.VectorSubcoreMesh(core_axis_name="c", subcore_axis_name="s")


def _sc_pipeline(idx_v, tab_hbm, rows_v, sems, compute):
    def gather(i, c, buf):
        window = idx_v.at[i, pl.ds(c * SC_ROWS, SC_ROWS)]
        return pltpu.make_async_copy(tab_hbm.at[window], rows_v.at[buf], sems.at[buf])

    gather(0, 0, 0).start()

    @pl.loop(0, SC_TOK_BLOCK)
    def _(i):
        for c in range(SC_GATHERS):
            buf = c % 2
            gather(i, c, buf).wait()
            if c + 1 < SC_GATHERS:
                gather(i, c + 1, 1 - buf).start()
            else:
                @pl.when(i + 1 < SC_TOK_BLOCK)
                def _():
                    gather(i + 1, 0, 1 - buf).start()
            compute(i, c, i * SC_GATHERS + c, buf)


def _peer_down_dots(xn, idx, down, tok_start):
    t_all, d = xn.shape
    t = t_all - tok_start
    tok_w = t // SC_WORKERS
    nblk = tok_w // SC_TOK_BLOCK
    assert nblk * SC_TOK_BLOCK * SC_WORKERS == t
    lanes = SC_LANES

    @functools.partial(
        pl.kernel, mesh=_sc_mesh(),
        out_type=jax.ShapeDtypeStruct((t, PEER_E), F32),
        scratch_types=[
            pltpu.VMEM((SC_TOK_BLOCK, PEER_E), jnp.int32),
            pltpu.VMEM((SC_TOK_BLOCK, d), F32),
            pltpu.VMEM((2, SC_ROWS, d), F32),
            pltpu.VMEM((SC_ROWS, lanes), F32),
            pltpu.VMEM((SC_TOK_BLOCK, PEER_E), F32),
            pltpu.SemaphoreType.DMA((2,)),
        ],
        compiler_params=pltpu.CompilerParams(needs_layout_passes=False),
        name="peer_down_dots",
    )
    def k(x_hbm, idx_hbm, tab_hbm, out_hbm, idx_v, x_v, rows_v, acc_v, out_v, sems):
        wid = lax.axis_index("s") * SC_CORES + lax.axis_index("c")
        lane = lax.iota(jnp.int32, lanes)

        @pl.loop(0, nblk)
        def _(blk):
            out0 = wid * tok_w + blk * SC_TOK_BLOCK
            tok0 = tok_start + out0
            pltpu.sync_copy(idx_hbm.at[pl.ds(tok0, SC_TOK_BLOCK)], idx_v)
            pltpu.sync_copy(x_hbm.at[pl.ds(tok0, SC_TOK_BLOCK)], x_v)

            def compute(i, c, step, buf):
                def body(dc, accs):
                    xv = x_v[i, pl.ds(dc * lanes, lanes)]
                    return tuple(a + rows_v[buf, r, pl.ds(dc * lanes, lanes)] * xv
                                 for r, a in enumerate(accs))
                accs = lax.fori_loop(0, d // lanes, body,
                                     tuple(jnp.zeros((lanes,), F32) for _ in range(SC_ROWS)))
                for r in range(SC_ROWS):
                    acc_v[r, :] = accs[r]
                for g in range(SC_ROWS // lanes):
                    tot = jnp.zeros((lanes,), F32)
                    for l in range(lanes):
                        tot = tot + plsc.load_gather(
                            acc_v, [lane + g * lanes, jnp.full((lanes,), l, jnp.int32)])
                    out_v[i, pl.ds(c * SC_ROWS + g * lanes, lanes)] = tot

            _sc_pipeline(idx_v, tab_hbm, rows_v, sems, compute)
            pltpu.sync_copy(out_v, out_hbm.at[pl.ds(out0, SC_TOK_BLOCK)])

    return k(xn, idx, down)


def _peer_up_sum(coef, idx, up, tok_start):
    t_all = coef.shape[0]
    t = t_all - tok_start
    d = up.shape[1]
    tok_w = t // SC_WORKERS
    nblk = tok_w // SC_TOK_BLOCK
    assert nblk * SC_TOK_BLOCK * SC_WORKERS == t
    lanes = SC_LANES
    nj = SC_QUARTER // lanes

    @functools.partial(
        pl.kernel, mesh=_sc_mesh(),
        out_type=jax.ShapeDtypeStruct((t, d), F32),
        scratch_types=[
            pltpu.VMEM((SC_TOK_BLOCK, PEER_E), jnp.int32),
            pltpu.VMEM((SC_TOK_BLOCK * PEER_E,), F32),
            pltpu.VMEM((2, SC_ROWS, d), F32),
            pltpu.VMEM((SC_TOK_BLOCK, d), F32),
            pltpu.SemaphoreType.DMA((2,)),
        ],
        compiler_params=pltpu.CompilerParams(needs_layout_passes=False),
        name="peer_up_sum",
    )
    def k(coef_hbm, idx_hbm, tab_hbm, out_hbm, idx_v, coef_v, rows_v, out_v, sems):
        wid = lax.axis_index("s") * SC_CORES + lax.axis_index("c")

        @pl.loop(0, nblk)
        def _(blk):
            out0 = wid * tok_w + blk * SC_TOK_BLOCK
            tok0 = tok_start + out0
            pltpu.sync_copy(idx_hbm.at[pl.ds(tok0, SC_TOK_BLOCK)], idx_v)
            pltpu.sync_copy(coef_hbm.at[pl.ds(tok0 * PEER_E, SC_TOK_BLOCK * PEER_E)], coef_v)

            def compute(i, c, step, buf):
                for q in range(d // SC_QUARTER):
                    col = lambda j: pl.ds(q * SC_QUARTER + j * lanes, lanes)
                    if c == 0:
                        init = tuple(jnp.zeros((lanes,), F32) for _ in range(nj))
                    else:
                        init = tuple(out_v[i, col(j)] for j in range(nj))

                    def body(r, accs):
                        cf = plsc.load_gather(
                            coef_v, [jnp.full((lanes,), step * SC_ROWS + r, jnp.int32)])
                        return tuple(a + rows_v[buf, r, col(j)] * cf for j, a in enumerate(accs))
                    accs = lax.fori_loop(0, SC_ROWS, body, init)
                    for j in range(nj):
                        out_v[i, col(j)] = accs[j]

            _sc_pipeline(idx_v, tab_hbm, rows_v, sems, compute)
            pltpu.sync_copy(out_v, out_hbm.at[pl.ds(out0, SC_TOK_BLOCK)])

    return k(coef.reshape(t_all * PEER_E), idx, up)


TC_SUB = 128
TC_SUBS = 2
TC_STEP = TC_SUB * TC_SUBS
TC_TRIP = 64
TC_KCHUNKS = 4
ROW_WORDS = 4
ROW_HALVES = 2 * ROW_WORDS
STACK_COLS = PEER_E * ROW_HALVES
FRONT_PARTS = 4
SC_DOWN_PARTS = 1
SC_DOWN_EXTRA = 6656
PEER_TC_UP_TOKENS = 25088


def _pack_rows_bf16(tab):
    n, d = tab.shape
    return pl.pallas_call(
        _pack_kernel,
        grid=(n // PACK_ROWS,),
        in_specs=[pl.BlockSpec((PACK_ROWS, d), lambda i: (i, 0))],
        out_specs=pl.BlockSpec((PACK_ROWS * ROW_WORDS, LANES), lambda i: (i, 0)),
        out_shape=jax.ShapeDtypeStruct((n * ROW_WORDS, LANES), jnp.uint32),
        compiler_params=pltpu.CompilerParams(dimension_semantics=("arbitrary",)),
        name="pack_rows_bf16",
    )(tab)


PACK_ROWS = 512


def _pack_kernel(x_ref, o_ref):
    u = pltpu.bitcast(x_ref[...], jnp.uint32)
    r = (u + jnp.uint32(0x7FFF) + ((u >> 16) & jnp.uint32(1))) >> 16
    for s in range(ROW_WORDS):
        lo = r[:, 2 * LANES * s:2 * LANES * s + LANES]
        hi = r[:, 2 * LANES * s + LANES:2 * LANES * (s + 1)]
        o_ref[pl.ds(s, PACK_ROWS, stride=ROW_WORDS), :] = lo | (hi << 16)


def _load_table_once(tab_hbm, tab_v, sem):
    @pl.when(pl.program_id(0) == 0)
    def _():
        cp = pltpu.make_async_copy(tab_hbm, tab_v, sem)
        cp.start()
        cp.wait()


def _tc_token_pipeline(idx_hbm, idx_s, isem, tab_v, per_token):
    step = pl.program_id(0)

    def idx_copy(block, slot):
        return pltpu.make_async_copy(idx_hbm.at[block], idx_s[slot], isem.at[slot])

    @pl.when(step == 0)
    def _():
        idx_copy(0, 0).start()

    for sub in range(TC_SUBS):
        block = step * TC_SUBS + sub
        idx_copy(block, sub).wait()
        if sub + 1 < TC_SUBS:
            idx_copy(block + 1, sub + 1).start()
        else:
            @pl.when(step + 1 < pl.num_programs(0))
            def _():
                idx_copy(block + 1, 0).start()
        offs = idx_s[sub]

        def chunks(i, offs=offs):
            per = PEER_E // TC_KCHUNKS
            for ch in range(TC_KCHUNKS):
                pieces = [tab_v[pl.ds(pl.multiple_of(offs.at[k][i], ROW_WORDS), ROW_WORDS), :]
                          for k in range(ch * per, (ch + 1) * per)]
                yield pltpu.bitcast(jnp.concatenate(pieces, axis=0), BF16)

        def trip(ip, carry, sub=sub, chunks=chunks):
            for u in range(TC_TRIP):
                i = ip * TC_TRIP + u
                per_token(sub * TC_SUB + i, chunks(i))
            return carry
        lax.fori_loop(0, TC_SUB // TC_TRIP, trip, 0)


def _tc_peer_scratch(tabp):
    return [pltpu.VMEM(tabp.shape, jnp.uint32),
            [pltpu.SMEM((PEER_E, TC_SUB), jnp.int32) for _ in range(TC_SUBS)],
            pltpu.SemaphoreType.DMA,
            pltpu.SemaphoreType.DMA((TC_SUBS,))]


def _stack_selectors():
    col = np.arange(STACK_COLS)
    diag = (col[None, :] % ROW_HALVES == np.arange(ROW_HALVES)[:, None]).astype(np.float32)
    owner = (col[:, None] // ROW_HALVES == np.arange(PEER_E)[None, :]).astype(np.float32)
    return jnp.asarray(diag), jnp.asarray(owner)


def _tc_down_kernel(idx_hbm, x_ref, tab_hbm, diag_ref, fold_ref, o_ref,
                    tab_v, idx_s, sem, isem, z_ref):
    _load_table_once(tab_hbm, tab_v, sem)
    cols = STACK_COLS // TC_KCHUNKS

    def per_token(t, chunks):
        xr = x_ref[pl.ds(t, 1), :]
        x8 = jnp.concatenate([xr[:, LANES * j:LANES * (j + 1)] for j in range(ROW_HALVES)],
                             axis=0).astype(BF16)
        parts = []
        for ch, rows in enumerate(chunks):
            a = lax.dot_general(x8, rows, (((1,), (1,)), ((), ())), preferred_element_type=F32)
            parts.append(jnp.sum(a * diag_ref[:, cols * ch:cols * (ch + 1)], axis=0, keepdims=True))
        z_ref[pl.ds(t, 1), :] = jnp.concatenate(parts, axis=1)

    _tc_token_pipeline(idx_hbm, idx_s, isem, tab_v, per_token)
    o_ref[...] = _dot_exact(z_ref[...], fold_ref[...])


def _peer_down_dots_tc(xn, offs, tabp, n_tok):
    diag, owner = _stack_selectors()
    full = lambda a: pl.BlockSpec(a.shape, lambda i: (0,) * a.ndim)
    return pl.pallas_call(
        _tc_down_kernel,
        grid=(n_tok // TC_STEP,),
        in_specs=[pl.BlockSpec(memory_space=pl.ANY),
                  pl.BlockSpec((TC_STEP, xn.shape[1]), lambda i: (i, 0)),
                  pl.BlockSpec(memory_space=pl.ANY), full(diag), full(owner)],
        out_specs=pl.BlockSpec((TC_STEP, PEER_E), lambda i: (i, 0)),
        out_shape=jax.ShapeDtypeStruct((n_tok, PEER_E), F32),
        scratch_shapes=_tc_peer_scratch(tabp) + [pltpu.VMEM((TC_STEP, STACK_COLS), F32)],
        compiler_params=pltpu.CompilerParams(
            dimension_semantics=("arbitrary",), vmem_limit_bytes=VMEM_LIMIT),
        name="peer_down_dots_tc",
    )(offs, xn, tabp, diag, owner)


def _tc_up_kernel(idx_hbm, coef_ref, tab_hbm, spread_ref, diag_ref, o_ref,
                  tab_v, idx_s, sem, isem, ce_ref):
    _load_table_once(tab_hbm, tab_v, sem)
    c = coef_ref[...]
    c_hi = c.astype(BF16)
    c_lo = (c - c_hi.astype(F32)).astype(BF16)
    ce_ref[0] = jnp.dot(c_hi, spread_ref[...], preferred_element_type=F32)
    ce_ref[1] = jnp.dot(c_lo, spread_ref[...], preferred_element_type=F32)

    cols = STACK_COLS // TC_KCHUNKS

    def per_token(t, chunks):
        d = diag_ref[...]
        lhs = jnp.concatenate([(ce_ref[0, pl.ds(t, 1), :] * d).astype(BF16),
                               (ce_ref[1, pl.ds(t, 1), :] * d).astype(BF16)], axis=0)
        r = jnp.zeros((2 * ROW_HALVES, LANES), F32)
        for ch, rows in enumerate(chunks):
            r = r + jnp.dot(lhs[:, cols * ch:cols * (ch + 1)], rows, preferred_element_type=F32)
        out8 = r[0:ROW_HALVES] + r[ROW_HALVES:2 * ROW_HALVES]
        o_ref[pl.ds(t, 1), :] = jnp.concatenate(
            [out8[j:j + 1, :] for j in range(ROW_HALVES)], axis=1)

    _tc_token_pipeline(idx_hbm, idx_s, isem, tab_v, per_token)


def _peer_up_sum_tc(coef, offs, tabp, n_tok):
    d = ROW_HALVES * LANES
    diag, owner = _stack_selectors()
    spread = owner.T.astype(BF16)
    full = lambda a: pl.BlockSpec(a.shape, lambda i: (0,) * a.ndim)
    return pl.pallas_call(
        _tc_up_kernel,
        grid=(n_tok // TC_STEP,),
        in_specs=[pl.BlockSpec(memory_space=pl.ANY),
                  pl.BlockSpec((TC_STEP, PEER_E), lambda i: (i, 0)),
                  pl.BlockSpec(memory_space=pl.ANY), full(spread), full(diag)],
        out_specs=pl.BlockSpec((TC_STEP, d), lambda i: (i, 0)),
        out_shape=jax.ShapeDtypeStruct((n_tok, d), F32),
        scratch_shapes=_tc_peer_scratch(tabp) + [pltpu.VMEM((2, TC_STEP, STACK_COLS), F32)],
        compiler_params=pltpu.CompilerParams(
            dimension_semantics=("arbitrary",), vmem_limit_bytes=VMEM_LIMIT),
        name="peer_up_sum_tc",
    )(offs, coef, tabp, spread, diag)


def _coef_kernel(pre_ref, gate_ref, o_ref):
    x = pre_ref[...]
    gelu = 0.5 * x * (1.0 + lax.erf(x * (2.0 ** -0.5)))
    o_ref[...] = gate_ref[...] * gelu


def _peer_coef(pre, gates):
    t, e = pre.shape
    row = pl.BlockSpec((TM_COEF, e), lambda i: (i, 0))
    return pl.pallas_call(
        _coef_kernel,
        grid=(t // TM_COEF,),
        in_specs=[row, row],
        out_specs=row,
        out_shape=jax.ShapeDtypeStruct((t, e), F32),
        compiler_params=pltpu.CompilerParams(dimension_semantics=("arbitrary",)),
        name="peer_gate_gelu",
    )(pre, gates)


def _front_half(x2, row0, rows, bsz, seq, w_pad, layer, i):
    (norm1_g, ret_norm_g, m_conv_w, m_conv_b, m_wq, m_wk, m_b_i, m_b_f, m_norm_g, m_skip,
     w_out, norm2_g, peer_wq, keys) = layer
    d = x2.shape[1]
    proj = _rms_proj(x2, row0, rows, norm1_g[i].reshape(1, d), w_pad)
    y = _mixer(proj, bsz, seq, ret_norm_g[i], m_conv_w[i], m_conv_b[i], m_wq[i], m_wk[i],
               m_b_i[i], m_b_f[i], m_norm_g[i], m_skip[i])
    x1, xn, sub_t = _post(y, x2, row0, w_out, norm2_g[i].reshape(1, d), peer_wq, keys)
    return (x1, xn) + tuple(_peer_route(sub_t))


def kernel(x, norm1_g, w_in, ret_norm_g, m_conv_w, m_conv_b, m_wq, m_wk, m_b_i, m_b_f,
           m_norm_g, m_skip, w_out, norm2_g, peer_wq, peer_keys, peer_down, peer_up, final_g):
    b, s, d = x.shape
    t = b * s
    tp = t // FRONT_PARTS
    x2 = x.reshape(t, d)
    depth = norm1_g.shape[0]
    for i in range(depth):
        if i > 0:
            x2 = jnp.concatenate(x1s, axis=0) + jnp.concatenate([p_tc, p_sc], axis=0)
        wi = w_in[i]
        n_main = OFF_GI
        gate_pad = lambda w: jnp.pad(w, ((0, 0), (0, LANES - w.shape[1])))
        w_pad = jnp.concatenate(
            [wi[:, :n_main], gate_pad(wi[:, n_main:n_main + MLSTM_HEADS]),
             gate_pad(wi[:, n_main + MLSTM_HEADS:])], axis=1).astype(BF16)
        keys = peer_keys[i].reshape(2 * PEER_HEADS, PEER_NKEYS, PEER_HALF).astype(BF16)
        layer = (norm1_g, ret_norm_g, m_conv_w, m_conv_b, m_wq, m_wk, m_b_i, m_b_f, m_norm_g,
                 m_skip, w_out[i].astype(BF16), norm2_g, peer_wq[i].astype(BF16), keys)
        down_p = _pack_rows_bf16(peer_down[i])
        up_p = _pack_rows_bf16(peer_up[i])

        fronts = []
        pre = [None] * FRONT_PARTS
        for q in range(FRONT_PARTS):
            fr = _front_half(x2, q * tp, tp, b // FRONT_PARTS, s, w_pad, layer, i)
            fronts.append(fr)
            if q < SC_DOWN_PARTS:
                pre[q] = [_peer_down_dots(fr[1], fr[2], peer_down[i], 0)]
            elif q == SC_DOWN_PARTS and SC_DOWN_EXTRA:
                pre[q] = [None, _peer_down_dots(fr[1], fr[2], peer_down[i], tp - SC_DOWN_EXTRA)]
        for q in range(SC_DOWN_PARTS, FRONT_PARTS):
            n_tc_q = tp - SC_DOWN_EXTRA if q == SC_DOWN_PARTS else tp
            pre_tc = _peer_down_dots_tc(fronts[q][1], fronts[q][4], down_p, n_tc_q)
            if pre[q] is None:
                pre[q] = [pre_tc]
            else:
                pre[q][0] = pre_tc
        pre = [piece for part in pre for piece in part]
        x1s = [fr[0] for fr in fronts]
        expert = jnp.concatenate([fr[2] for fr in fronts], axis=0)
        gates = jnp.concatenate([fr[3] for fr in fronts], axis=0)
        offs = jnp.concatenate([fr[4] for fr in fronts], axis=0)
        coef = _peer_coef(jnp.concatenate(pre, axis=0), gates)
        n_tc = PEER_TC_UP_TOKENS
        p_sc = _peer_up_sum(coef, expert, peer_up[i], n_tc)
        p_tc = _peer_up_sum_tc(coef, offs, up_p, n_tc)
    out = _final(_row_ranges(x1s, [p_tc, p_sc]), final_g.reshape(1, d), t)
    return out.reshape(b, s, d)


def _row_ranges(xs, ps):
    def starts(arrs):
        out = [0]
        for a in arrs:
            out.append(out[-1] + a.shape[0])
        return out
    xb, pb = starts(xs), starts(ps)
    assert xb[-1] == pb[-1]
    cuts = sorted(set(xb) | set(pb))
    parts = []
    for lo, hi in zip(cuts[:-1], cuts[1:]):
        xi = max(k for k in range(len(xs)) if xb[k] <= lo)
        pi = max(k for k in range(len(ps)) if pb[k] <= lo)
        parts.append((xs[xi], lo - xb[xi], ps[pi], lo - pb[pi], hi - lo))
    return parts
```

```python
import functools

import numpy as np
import jax
import jax.numpy as jnp
from jax import lax
from jax.experimental import pallas as pl
from jax.experimental.pallas import tpu as pltpu
from jax.experimental.pallas import tpu_sc as plsc

F32 = jnp.float32
BF16 = jnp.bfloat16

D_MODEL = 1024
CHUNK = 64
RET_HEADS = 4
RET_DK = 128
MLSTM_HEADS = 4
MLSTM_D = 128
HEAD_W = 128
D_GROUP = 512
CONV_W = 4
ROPE_BASE = 10000.0
PEER_HEADS = 8
PEER_NKEYS = 128
PEER_TOPK = 16
PEER_HALF = 128
PEER_BLOCK = 128
EPS = 1e-6

OFF_RQ, OFF_RK, OFF_RV, OFF_RG = 0, 512, 1024, 1536
OFF_MX, OFF_MV, OFF_MO = 2048, 2560, 3072
OFF_GI, OFF_GF = 3584, 3712
D_PROJ_PAD = 3840

LANES = 128
CONV_TAIL = 8
VMEM_LIMIT = 56 * 1024 * 1024

TM_PROJ = 256
L_BLOCK = 256
TM_FINAL = 512
TM_COEF = 2048


def _rms(x, g):
    ms = jnp.mean(x * x, axis=-1, keepdims=True)
    return x * lax.rsqrt(ms + EPS) * g


def _group_norm(h, g):
    mu = jnp.mean(h, axis=-1, keepdims=True)
    d = h - mu
    var = jnp.mean(d * d, axis=-1, keepdims=True)
    return d * lax.rsqrt(var + EPS) * g


def _silu(x):
    return x * (1.0 / (1.0 + jnp.exp(-x)))


def _sigmoid(x):
    return 1.0 / (1.0 + jnp.exp(-x))


def _dot(a, b):
    return jnp.dot(a.astype(BF16), b.astype(BF16), preferred_element_type=F32)


def _dot_tn(a, b):
    return lax.dot_general(a.astype(BF16), b.astype(BF16), (((0,), (0,)), ((), ())),
                           preferred_element_type=F32)


def _dot_nt(a, b):
    return lax.dot_general(a.astype(BF16), b.astype(BF16), (((1,), (1,)), ((), ())),
                           preferred_element_type=F32)


def _rms_proj_kernel(x_ref, g_ref, w_ref, o_ref):
    h = _rms(x_ref[...], g_ref[...])
    o_ref[...] = jnp.dot(h.astype(BF16), w_ref[...], preferred_element_type=F32)


def _rms_proj(x2, row0, rows, g, w):
    d = x2.shape[1]
    t = rows
    n = w.shape[1]
    b0 = row0 // TM_PROJ
    return pl.pallas_call(
        _rms_proj_kernel,
        grid=(t // TM_PROJ,),
        in_specs=[
            pl.BlockSpec((TM_PROJ, d), lambda i: (i + b0, 0)),
            pl.BlockSpec((1, d), lambda i: (0, 0)),
            pl.BlockSpec((d, n), lambda i: (0, 0)),
        ],
        out_specs=pl.BlockSpec((TM_PROJ, n), lambda i: (i, 0)),
        out_shape=jax.ShapeDtypeStruct((t, n), F32),
        compiler_params=pltpu.CompilerParams(
            dimension_semantics=("arbitrary",), vmem_limit_bytes=VMEM_LIMIT),
        name="rms_in_proj",
    )(x2, g, w)


def _mixer_kernel(chunk_decay,
                  proj_ref, cos_ref, sin_ref, intra_ref, qd_ref, kd_ref, tri_ref,
                  rng_ref, cw_ref, cb_ref, wq_ref, wk_ref, bi_ref, bf_ref,
                  mng_ref, skip_ref,
                  y_ref,
                  rstate_ref, cstate_ref, m_ref, tail_ref):
    j = pl.program_id(1)
    n_chunks = L_BLOCK // CHUNK

    @pl.when(j == 0)
    def _():
        rstate_ref[...] = jnp.zeros_like(rstate_ref)
        cstate_ref[...] = jnp.zeros_like(cstate_ref)
        m_ref[...] = jnp.zeros_like(m_ref)
        tail_ref[...] = jnp.zeros_like(tail_ref)

    cosv = cos_ref[...]
    sinv = sin_ref[...]
    k_scale = RET_DK ** -0.5

    for h in range(RET_HEADS):
        lo = h * HEAD_W
        q = proj_ref[:, OFF_RQ + lo:OFF_RQ + lo + HEAD_W]
        k = proj_ref[:, OFF_RK + lo:OFF_RK + lo + HEAD_W]
        q = q * cosv + pltpu.roll(q, HEAD_W // 2, 1) * sinv
        k = (k * cosv + pltpu.roll(k, HEAD_W // 2, 1) * sinv) * k_scale
        intra = intra_ref[h]
        qd = qd_ref[h]
        kd = kd_ref[h]
        g = rng_ref[:, lo:lo + HEAD_W]
        for c in range(n_chunks):
            r0 = c * CHUNK
            qc = q[r0:r0 + CHUNK]
            kc = k[r0:r0 + CHUNK]
            vc = proj_ref[r0:r0 + CHUNK, OFF_RV + lo:OFF_RV + lo + HEAD_W]
            gate = proj_ref[r0:r0 + CHUNK, OFF_RG + lo:OFF_RG + lo + HEAD_W]
            state = rstate_ref[h]
            scores = _dot_nt(qc, kc) * intra
            o = _dot(scores, vc) + _dot(qc, state) * qd
            rstate_ref[h] = chunk_decay[h] * state + _dot_tn(kc * kd, vc)
            y_ref[r0:r0 + CHUNK, lo:lo + HEAD_W] = _silu(gate) * _group_norm(o, g)

    mx = proj_ref[:, OFF_MX:OFF_MX + D_GROUP]
    xp = jnp.concatenate([tail_ref[...], mx], axis=0)
    base = CONV_TAIL - (CONV_W - 1)
    conv = cb_ref[...] + xp[base:base + L_BLOCK] * cw_ref[0:1, :]
    for t in range(1, CONV_W):
        conv = conv + xp[base + t:base + t + L_BLOCK] * cw_ref[t:t + 1, :]
    tail_ref[...] = mx[L_BLOCK - CONV_TAIL:L_BLOCK]
    cact = _silu(conv)

    gi = proj_ref[:, OFF_GI:OFF_GI + LANES] + bi_ref[...]
    gf = proj_ref[:, OFF_GF:OFF_GF + LANES] + bf_ref[...]
    logf = jnp.minimum(gf, 0.0) - jnp.log1p(jnp.exp(-jnp.abs(gf)))
    tri = tri_ref[...]
    lane = lax.broadcasted_iota(jnp.int32, (CHUNK, HEAD_W), 1)
    ones_col = (lane == 0).astype(F32)
    m_scale = MLSTM_D ** -0.5

    mq = []
    mk = []
    for h in range(MLSTM_HEADS):
        lo = h * HEAD_W
        ch = cact[:, lo:lo + HEAD_W]
        mq.append(_dot(ch, wq_ref[h]))
        mk.append(_dot(ch, wk_ref[h]) * m_scale)

    for c in range(n_chunks):
        r0 = c * CHUNK
        fcum = jnp.dot(tri, logf[r0:r0 + CHUNK], preferred_element_type=F32,
                       precision=lax.Precision.HIGHEST)
        a = fcum[CHUNK - 1:CHUNK]
        logw = a - fcum + gi[r0:r0 + CHUNK]
        m_in = jnp.max(logw, axis=0, keepdims=True)
        m_old = m_ref[...]
        m_new = jnp.maximum(a + m_old, m_in)
        decay = jnp.exp(a + m_old - m_new)
        w = jnp.exp(logw - m_new)
        em = jnp.exp(-m_new)
        m_ref[...] = m_new
        for h in range(MLSTM_HEADS):
            lo = h * HEAD_W
            qh = mq[h][r0:r0 + CHUNK]
            kh = mk[h][r0:r0 + CHUNK]
            vh = proj_ref[r0:r0 + CHUNK, OFF_MV + lo:OFF_MV + lo + HEAD_W]
            oh = proj_ref[r0:r0 + CHUNK, OFF_MO + lo:OFF_MO + lo + HEAD_W]
            ch = cact[r0:r0 + CHUNK, lo:lo + HEAD_W]
            v_ext = jnp.concatenate([vh, ones_col], axis=1)
            cmem = decay[:, h:h + 1] * cstate_ref[h] + _dot_tn(kh * w[:, h:h + 1], v_ext)
            cstate_ref[h] = cmem
            num = _dot(qh, cmem)
            den = jnp.maximum(jnp.abs(num[:, HEAD_W:HEAD_W + 1]), em[:, h:h + 1])
            hh = num[:, :HEAD_W] / den
            hm = _group_norm(hh, mng_ref[:, lo:lo + HEAD_W])
            ym = _sigmoid(oh) * (hm + skip_ref[:, lo:lo + HEAD_W] * ch)
            y_ref[r0:r0 + CHUNK, D_GROUP + lo:D_GROUP + lo + HEAD_W] = ym


def _retention_tables():
    h = np.arange(RET_HEADS, dtype=np.float64)
    log_g = np.log(1.0 - 2.0 ** (-5.0 - h))
    l = np.arange(CHUNK, dtype=np.float64)
    intra = np.exp(log_g[:, None, None] * np.abs(l[:, None] - l[None, :]))
    qd = np.exp(log_g[:, None] * (l + 1.0))[:, :, None] * np.ones((1, 1, HEAD_W))
    kd = np.exp(log_g[:, None] * (CHUNK - 1.0 - l))[:, :, None] * np.ones((1, 1, HEAD_W))
    cd = np.exp(log_g * CHUNK)
    return (jnp.asarray(intra, F32), jnp.asarray(qd, F32), jnp.asarray(kd, F32),
            tuple(float(v) for v in cd))


def _rotary_tables(s):
    half = HEAD_W // 2
    inv = ROPE_BASE ** (-np.arange(half, dtype=np.float64) / half)
    ang = np.arange(s, dtype=np.float64)[:, None] * inv[None, :]
    cos = np.cos(ang)
    sin = np.sin(ang)
    return (jnp.asarray(np.concatenate([cos, cos], axis=-1), F32),
            jnp.asarray(np.concatenate([-sin, sin], axis=-1), F32))


def _mixer(proj, b, s, ret_norm_g, m_conv_w, m_conv_b, m_wq, m_wk, m_b_i, m_b_f,
           m_norm_g, m_skip):
    intra, qd, kd, cd = _retention_tables()
    cos_t, sin_t = _rotary_tables(s)
    tri = jnp.asarray(np.tril(np.ones((CHUNK, CHUNK))), F32)
    nj = s // L_BLOCK
    pad = lambda v: jnp.pad(v.reshape(1, -1), ((0, 0), (0, LANES - v.size)))
    full = lambda shape: pl.BlockSpec(shape, lambda bi, j: (0,) * len(shape))
    return pl.pallas_call(
        functools.partial(_mixer_kernel, cd),
        grid=(b, nj),
        in_specs=[
            pl.BlockSpec((L_BLOCK, D_PROJ_PAD), lambda bi, j: (bi * nj + j, 0)),
            pl.BlockSpec((L_BLOCK, HEAD_W), lambda bi, j: (j, 0)),
            pl.BlockSpec((L_BLOCK, HEAD_W), lambda bi, j: (j, 0)),
            full((RET_HEADS, CHUNK, CHUNK)),
            full((RET_HEADS, CHUNK, HEAD_W)),
            full((RET_HEADS, CHUNK, HEAD_W)),
            full((CHUNK, CHUNK)),
            full((1, D_GROUP)),
            full((CONV_W, D_GROUP)),
            full((1, D_GROUP)),
            full((MLSTM_HEADS, MLSTM_D, MLSTM_D)),
            full((MLSTM_HEADS, MLSTM_D, MLSTM_D)),
            full((1, LANES)),
            full((1, LANES)),
            full((1, D_GROUP)),
            full((1, D_GROUP)),
        ],
        out_specs=pl.BlockSpec((L_BLOCK, 2 * D_GROUP), lambda bi, j: (bi * nj + j, 0)),
        out_shape=jax.ShapeDtypeStruct((b * s, 2 * D_GROUP), F32),
        scratch_shapes=[
            pltpu.VMEM((RET_HEADS, RET_DK, HEAD_W), F32),
            pltpu.VMEM((MLSTM_HEADS, MLSTM_D, 2 * HEAD_W), F32),
            pltpu.VMEM((1, LANES), F32),
            pltpu.VMEM((CONV_TAIL, D_GROUP), F32),
        ],
        compiler_params=pltpu.CompilerParams(
            dimension_semantics=("arbitrary", "arbitrary"), vmem_limit_bytes=VMEM_LIMIT),
        name="retention_mlstm_mixer",
    )(proj, cos_t, sin_t, intra, qd, kd, tri,
      ret_norm_g.reshape(1, D_GROUP), m_conv_w, m_conv_b.reshape(1, D_GROUP),
      m_wq.astype(BF16), m_wk.astype(BF16), pad(m_b_i), pad(m_b_f),
      m_norm_g.reshape(1, D_GROUP), m_skip.reshape(1, D_GROUP))


def _post_kernel(y_ref, x_ref, wout_ref, g2_ref, wq_ref, keys_ref,
                 x1_ref, xn_ref, sub_ref):
    x1 = x_ref[...] + jnp.dot(y_ref[...].astype(BF16), wout_ref[...],
                              preferred_element_type=F32)
    x1_ref[...] = x1
    xn = _rms(x1, g2_ref[...])
    xn_ref[...] = xn
    q = jnp.dot(xn.astype(BF16), wq_ref[...], preferred_element_type=F32)
    for i in range(2 * PEER_HEADS):
        lo = i * PEER_HALF
        sub_ref[lo:lo + PEER_NKEYS, :] = _dot_nt(keys_ref[i], q[:, lo:lo + PEER_HALF])


def _post(y, x2, row0, w_out, norm2_g, peer_wq, keys):
    t, d = y.shape
    nq = peer_wq.shape[1]
    b0 = row0 // TM_PROJ
    row = lambda w: pl.BlockSpec((TM_PROJ, w), lambda i: (i, 0))
    full = lambda shape: pl.BlockSpec(shape, lambda i: (0,) * len(shape))
    return pl.pallas_call(
        _post_kernel,
        grid=(t // TM_PROJ,),
        in_specs=[row(d), pl.BlockSpec((TM_PROJ, d), lambda i: (i + b0, 0)), full((d, d)),
                  full((1, d)), full((d, nq)), full(keys.shape)],
        out_specs=[row(d), row(d), pl.BlockSpec((nq, TM_PROJ), lambda i: (0, i))],
        out_shape=[jax.ShapeDtypeStruct((t, d), F32), jax.ShapeDtypeStruct((t, d), F32),
                   jax.ShapeDtypeStruct((nq, t), F32)],
        compiler_params=pltpu.CompilerParams(
            dimension_semantics=("arbitrary",), vmem_limit_bytes=VMEM_LIMIT),
        name="out_proj_peer_scores",
    )(y, x2, w_out, norm2_g, peer_wq, keys)


def _final_kernel(x1_ref, p_ref, g_ref, *rest):
    o_ref = rest[-1]
    o_ref[...] = _rms(x1_ref[...] + p_ref[...], g_ref[...])


def _final(parts, g, t):
    d = g.shape[1]
    spec = lambda r0: pl.BlockSpec((TM_FINAL, d), lambda i, b0=r0 // TM_FINAL: (i + b0, 0))
    out = None
    row0 = 0
    for x1, x_row0, p, p_row0, rows in parts:
        in_specs = [spec(x_row0), spec(p_row0), pl.BlockSpec((1, d), lambda i: (0, 0))]
        args = [x1, p, g]
        if out is not None:
            in_specs.append(pl.BlockSpec(memory_space=pl.ANY))
            args.append(out)
        out = pl.pallas_call(
            _final_kernel,
            grid=(rows // TM_FINAL,),
            in_specs=in_specs,
            out_specs=spec(row0),
            out_shape=jax.ShapeDtypeStruct((t, d), F32),
            input_output_aliases={} if len(args) == 3 else {3: 0},
            compiler_params=pltpu.CompilerParams(dimension_semantics=("arbitrary",)),
            name="final_rmsnorm",
        )(*args)
        row0 += rows
    assert row0 == t
    return out


TM_ROUTE = 256
CAND_ROWS = 56


def _candidate_tables():
    pairs = [(k1, k2) for k1 in range(PEER_TOPK) for k2 in range(PEER_TOPK)
             if (k1 + 1) * (k2 + 1) <= PEER_TOPK]
    assert len(pairs) <= CAND_ROWS
    rep = np.zeros((CAND_ROWS, 2 * PEER_TOPK), np.float32)
    til = np.zeros((CAND_ROWS, 2 * PEER_TOPK), np.float32)
    for pos, (k1, k2) in enumerate(pairs):
        rep[pos, k1] = 1.0
        til[pos, PEER_TOPK + k2] = 1.0
    return jnp.asarray(rep), jnp.asarray(til), len(pairs)


def _dot_exact(a, b):
    return jnp.dot(a, b, preferred_element_type=F32, precision=lax.Precision.HIGHEST)


def _route_kernel(n_cand, sub_ref, rep_ref, til_ref, seg_ref, exp_ref, gate_ref, offs_ref,
                  s_ref, i_ref, best_ref, eid_ref, m0_ref):
    tm = sub_ref.shape[1]
    neg = -jnp.inf
    key_id = lax.broadcasted_iota(jnp.int32, (PEER_NKEYS, tm), 0).astype(F32)
    cand_id = lax.broadcasted_iota(jnp.int32, (CAND_ROWS, tm), 0).astype(F32)
    rep = rep_ref[...]
    til = til_ref[...]
    for h in range(PEER_HEADS):
        for p in range(2):
            lo = (2 * h + p) * PEER_NKEYS
            x = sub_ref[lo:lo + PEER_NKEYS, :]
            for k in range(PEER_TOPK):
                m = jnp.max(x, axis=0, keepdims=True)
                idx = jnp.min(jnp.where(x == m, key_id, float(PEER_NKEYS)), axis=0, keepdims=True)
                x = jnp.where(key_id == idx, neg, x)
                r = p * PEER_TOPK + k
                s_ref[r:r + 1, :] = m
                i_ref[r:r + 1, :] = idx
        s = s_ref[...]
        iv = i_ref[...]
        cand = _dot_exact(rep, s) + _dot_exact(til, s)
        cand = jnp.where(cand_id < float(n_cand), cand, neg)
        eid = _dot(rep, iv) * float(PEER_NKEYS) + _dot(til, iv)
        for k in range(PEER_TOPK):
            m = jnp.max(cand, axis=0, keepdims=True)
            pos = jnp.min(jnp.where(cand == m, cand_id, float(CAND_ROWS)), axis=0, keepdims=True)
            sel = cand_id == pos
            e = jnp.max(jnp.where(sel, eid, -1.0), axis=0, keepdims=True)
            cand = jnp.where(sel, neg, cand)
            r = h * PEER_TOPK + k
            best_ref[r:r + 1, :] = m
            eid_ref[r:r + 1, :] = e
            if k == 0:
                m0_ref[r:r + PEER_TOPK, :] = jnp.broadcast_to(m, (PEER_TOPK, tm))
    pexp = jnp.exp(best_ref[...] - m0_ref[...])
    gates = pexp / _dot_exact(seg_ref[...], pexp)
    gate_ref[...] = gates.T
    eid = eid_ref[...].astype(jnp.int32)
    exp_ref[...] = eid.T
    for blk in range(tm // LANES):
        offs_ref[blk] = eid[:, blk * LANES:(blk + 1) * LANES] * ROW_WORDS


def _peer_route(sub_t):
    nq, t = sub_t.shape
    rep, til, n_cand = _candidate_tables()
    e = PEER_HEADS * PEER_TOPK
    seg = jnp.asarray(np.kron(np.eye(PEER_HEADS), np.ones((PEER_TOPK, PEER_TOPK))), F32)
    full = lambda a: pl.BlockSpec(a.shape, lambda i: (0,) * a.ndim)
    out = pl.BlockSpec((TM_ROUTE, e), lambda i: (i, 0))
    return pl.pallas_call(
        functools.partial(_route_kernel, n_cand),
        grid=(t // TM_ROUTE,),
        in_specs=[pl.BlockSpec((nq, TM_ROUTE), lambda i: (0, i)), full(rep), full(til), full(seg)],
        out_specs=[out, out, pl.BlockSpec((TM_ROUTE // LANES, e, LANES), lambda i: (i, 0, 0))],
        out_shape=[jax.ShapeDtypeStruct((t, e), jnp.int32), jax.ShapeDtypeStruct((t, e), F32),
                   jax.ShapeDtypeStruct((t // LANES, e, LANES), jnp.int32)],
        scratch_shapes=[
            pltpu.VMEM((2 * PEER_TOPK, TM_ROUTE), F32),
            pltpu.VMEM((2 * PEER_TOPK, TM_ROUTE), F32),
            pltpu.VMEM((e, TM_ROUTE), F32),
            pltpu.VMEM((e, TM_ROUTE), F32),
            pltpu.VMEM((e, TM_ROUTE), F32),
        ],
        compiler_params=pltpu.CompilerParams(dimension_semantics=("arbitrary",)),
        name="peer_topk_route",
    )(sub_t, rep, til, seg)


SC_CORES = 2
SC_SUBCORES = 16
SC_LANES = 16
SC_WORKERS = SC_CORES * SC_SUBCORES
PEER_E = PEER_HEADS * PEER_TOPK
SC_ROWS = 32
SC_GATHERS = PEER_E // SC_ROWS
SC_TOK_BLOCK = 16
SC_QUARTER = 256


def _sc_mesh():
    return plsc.VectorSubcoreMesh(core_axis_name="c", subcore_axis_name="s")


def _sc_pipeline(idx_v, tab_hbm, rows_v, sems, compute):
    def gather(i, c, buf):
        window = idx_v.at[i, pl.ds(c * SC_ROWS, SC_ROWS)]
        return pltpu.make_async_copy(tab_hbm.at[window], rows_v.at[buf], sems.at[buf])

    gather(0, 0, 0).start()

    @pl.loop(0, SC_TOK_BLOCK)
    def _(i):
        for c in range(SC_GATHERS):
            buf = c % 2
            gather(i, c, buf).wait()
            if c + 1 < SC_GATHERS:
                gather(i, c + 1, 1 - buf).start()
            else:
                @pl.when(i + 1 < SC_TOK_BLOCK)
                def _():
                    gather(i + 1, 0, 1 - buf).start()
            compute(i, c, i * SC_GATHERS + c, buf)


def _peer_down_dots(xn, idx, down, tok_start):
    t_all, d = xn.shape
    t = t_all - tok_start
    tok_w = t // SC_WORKERS
    nblk = tok_w // SC_TOK_BLOCK
    assert nblk * SC_TOK_BLOCK * SC_WORKERS == t
    lanes = SC_LANES

    @functools.partial(
        pl.kernel, mesh=_sc_mesh(),
        out_type=jax.ShapeDtypeStruct((t, PEER_E), F32),
        scratch_types=[
            pltpu.VMEM((SC_TOK_BLOCK, PEER_E), jnp.int32),
            pltpu.VMEM((SC_TOK_BLOCK, d), F32),
            pltpu.VMEM((2, SC_ROWS, d), F32),
            pltpu.VMEM((SC_ROWS, lanes), F32),
            pltpu.VMEM((SC_TOK_BLOCK, PEER_E), F32),
            pltpu.SemaphoreType.DMA((2,)),
        ],
        compiler_params=pltpu.CompilerParams(needs_layout_passes=False),
        name="peer_down_dots",
    )
    def k(x_hbm, idx_hbm, tab_hbm, out_hbm, idx_v, x_v, rows_v, acc_v, out_v, sems):
        wid = lax.axis_index("s") * SC_CORES + lax.axis_index("c")
        lane = lax.iota(jnp.int32, lanes)

        @pl.loop(0, nblk)
        def _(blk):
            out0 = wid * tok_w + blk * SC_TOK_BLOCK
            tok0 = tok_start + out0
            pltpu.sync_copy(idx_hbm.at[pl.ds(tok0, SC_TOK_BLOCK)], idx_v)
            pltpu.sync_copy(x_hbm.at[pl.ds(tok0, SC_TOK_BLOCK)], x_v)

            def compute(i, c, step, buf):
                def body(dc, accs):
                    xv = x_v[i, pl.ds(dc * lanes, lanes)]
                    return tuple(a + rows_v[buf, r, pl.ds(dc * lanes, lanes)] * xv
                                 for r, a in enumerate(accs))
                accs = lax.fori_loop(0, d // lanes, body,
                                     tuple(jnp.zeros((lanes,), F32) for _ in range(SC_ROWS)))
                for r in range(SC_ROWS):
                    acc_v[r, :] = accs[r]
                for g in range(SC_ROWS // lanes):
                    tot = jnp.zeros((lanes,), F32)
                    for l in range(lanes):
                        tot = tot + plsc.load_gather(
                            acc_v, [lane + g * lanes, jnp.full((lanes,), l, jnp.int32)])
                    out_v[i, pl.ds(c * SC_ROWS + g * lanes, lanes)] = tot

            _sc_pipeline(idx_v, tab_hbm, rows_v, sems, compute)
            pltpu.sync_copy(out_v, out_hbm.at[pl.ds(out0, SC_TOK_BLOCK)])

    return k(xn, idx, down)


def _peer_up_sum(coef, idx, up, tok_start):
    t_all = coef.shape[0]
    t = t_all - tok_start
    d = up.shape[1]
    tok_w = t // SC_WORKERS
    nblk = tok_w // SC_TOK_BLOCK
    assert nblk * SC_TOK_BLOCK * SC_WORKERS == t
    lanes = SC_LANES
    nj = SC_QUARTER // lanes

    @functools.partial(
        pl.kernel, mesh=_sc_mesh(),
        out_type=jax.ShapeDtypeStruct((t, d), F32),
        scratch_types=[
            pltpu.VMEM((SC_TOK_BLOCK, PEER_E), jnp.int32),
            pltpu.VMEM((SC_TOK_BLOCK * PEER_E,), F32),
            pltpu.VMEM((2, SC_ROWS, d), F32),
            pltpu.VMEM((SC_TOK_BLOCK, d), F32),
            pltpu.SemaphoreType.DMA((2,)),
        ],
        compiler_params=pltpu.CompilerParams(needs_layout_passes=False),
        name="peer_up_sum",
    )
    def k(coef_hbm, idx_hbm, tab_hbm, out_hbm, idx_v, coef_v, rows_v, out_v, sems):
        wid = lax.axis_index("s") * SC_CORES + lax.axis_index("c")

        @pl.loop(0, nblk)
        def _(blk):
            out0 = wid * tok_w + blk * SC_TOK_BLOCK
            tok0 = tok_start + out0
            pltpu.sync_copy(idx_hbm.at[pl.ds(tok0, SC_TOK_BLOCK)], idx_v)
            pltpu.sync_copy(coef_hbm.at[pl.ds(tok0 * PEER_E, SC_TOK_BLOCK * PEER_E)], coef_v)

            def compute(i, c, step, buf):
                for q in range(d // SC_QUARTER):
                    col = lambda j: pl.ds(q * SC_QUARTER + j * lanes, lanes)
                    if c == 0:
                        init = tuple(jnp.zeros((lanes,), F32) for _ in range(nj))
                    else:
                        init = tuple(out_v[i, col(j)] for j in range(nj))

                    def body(r, accs):
                        cf = plsc.load_gather(
                            coef_v, [jnp.full((lanes,), step * SC_ROWS + r, jnp.int32)])
                        return tuple(a + rows_v[buf, r, col(j)] * cf for j, a in enumerate(accs))
                    accs = lax.fori_loop(0, SC_ROWS, body, init)
                    for j in range(nj):
                        out_v[i, col(j)] = accs[j]

            _sc_pipeline(idx_v, tab_hbm, rows_v, sems, compute)
            pltpu.sync_copy(out_v, out_hbm.at[pl.ds(out0, SC_TOK_BLOCK)])

    return k(coef.reshape(t_all * PEER_E), idx, up)


TC_SUB = 128
TC_SUBS = 2
TC_STEP = TC_SUB * TC_SUBS
TC_TRIP = 64
TC_KCHUNKS = 4
ROW_WORDS = 4
ROW_HALVES = 2 * ROW_WORDS
STACK_COLS = PEER_E * ROW_HALVES
FRONT_PARTS = 4
SC_DOWN_PARTS = 1
SC_DOWN_EXTRA = 7168
PEER_TC_UP_TOKENS = 24576


def _pack_rows_bf16(tab):
    n, d = tab.shape
    return pl.pallas_call(
        _pack_kernel,
        grid=(n // PACK_ROWS,),
        in_specs=[pl.BlockSpec((PACK_ROWS, d), lambda i: (i, 0))],
        out_specs=pl.BlockSpec((PACK_ROWS * ROW_WORDS, LANES), lambda i: (i, 0)),
        out_shape=jax.ShapeDtypeStruct((n * ROW_WORDS, LANES), jnp.uint32),
        compiler_params=pltpu.CompilerParams(dimension_semantics=("arbitrary",)),
        name="pack_rows_bf16",
    )(tab)


PACK_ROWS = 512


def _pack_kernel(x_ref, o_ref):
    u = pltpu.bitcast(x_ref[...], jnp.uint32)
    r = (u + jnp.uint32(0x7FFF) + ((u >> 16) & jnp.uint32(1))) >> 16
    for s in range(ROW_WORDS):
        lo = r[:, 2 * LANES * s:2 * LANES * s + LANES]
        hi = r[:, 2 * LANES * s + LANES:2 * LANES * (s + 1)]
        o_ref[pl.ds(s, PACK_ROWS, stride=ROW_WORDS), :] = lo | (hi << 16)


def _load_table_once(tab_hbm, tab_v, sem):
    @pl.when(pl.program_id(0) == 0)
    def _():
        cp = pltpu.make_async_copy(tab_hbm, tab_v, sem)
        cp.start()
        cp.wait()


def _tc_token_pipeline(idx_hbm, idx_s, isem, tab_v, per_token):
    step = pl.program_id(0)

    def idx_copy(block, slot):
        return pltpu.make_async_copy(idx_hbm.at[block], idx_s[slot], isem.at[slot])

    @pl.when(step == 0)
    def _():
        idx_copy(0, 0).start()

    for sub in range(TC_SUBS):
        block = step * TC_SUBS + sub
        idx_copy(block, sub).wait()
        if sub + 1 < TC_SUBS:
            idx_copy(block + 1, sub + 1).start()
        else:
            @pl.when(step + 1 < pl.num_programs(0))
            def _():
                idx_copy(block + 1, 0).start()
        offs = idx_s[sub]

        def chunks(i, offs=offs):
            per = PEER_E // TC_KCHUNKS
            for ch in range(TC_KCHUNKS):
                pieces = [tab_v[pl.ds(pl.multiple_of(offs.at[k][i], ROW_WORDS), ROW_WORDS), :]
                          for k in range(ch * per, (ch + 1) * per)]
                yield pltpu.bitcast(jnp.concatenate(pieces, axis=0), BF16)

        def trip(ip, carry, sub=sub, chunks=chunks):
            for u in range(TC_TRIP):
                i = ip * TC_TRIP + u
                per_token(sub * TC_SUB + i, chunks(i))
            return carry
        lax.fori_loop(0, TC_SUB // TC_TRIP, trip, 0)


def _tc_peer_scratch(tabp):
    return [pltpu.VMEM(tabp.shape, jnp.uint32),
            [pltpu.SMEM((PEER_E, TC_SUB), jnp.int32) for _ in range(TC_SUBS)],
            pltpu.SemaphoreType.DMA,
            pltpu.SemaphoreType.DMA((TC_SUBS,))]


def _stack_selectors():
    col = np.arange(STACK_COLS)
    diag = (col[None, :] % ROW_HALVES == np.arange(ROW_HALVES)[:, None]).astype(np.float32)
    owner = (col[:, None] // ROW_HALVES == np.arange(PEER_E)[None, :]).astype(np.float32)
    return jnp.asarray(diag), jnp.asarray(owner)


def _tc_down_kernel(idx_hbm, x_ref, tab_hbm, diag_ref, fold_ref, o_ref,
                    tab_v, idx_s, sem, isem, z_ref):
    _load_table_once(tab_hbm, tab_v, sem)
    cols = STACK_COLS // TC_KCHUNKS

    def per_token(t, chunks):
        xr = x_ref[pl.ds(t, 1), :]
        x8 = jnp.concatenate([xr[:, LANES * j:LANES * (j + 1)] for j in range(ROW_HALVES)],
                             axis=0).astype(BF16)
        parts = []
        for ch, rows in enumerate(chunks):
            a = lax.dot_general(x8, rows, (((1,), (1,)), ((), ())), preferred_element_type=F32)
            parts.append(jnp.sum(a * diag_ref[:, cols * ch:cols * (ch + 1)], axis=0, keepdims=True))
        z_ref[pl.ds(t, 1), :] = jnp.concatenate(parts, axis=1)

    _tc_token_pipeline(idx_hbm, idx_s, isem, tab_v, per_token)
    o_ref[...] = _dot_exact(z_ref[...], fold_ref[...])


def _peer_down_dots_tc(xn, offs, tabp, n_tok):
    diag, owner = _stack_selectors()
    full = lambda a: pl.BlockSpec(a.shape, lambda i: (0,) * a.ndim)
    return pl.pallas_call(
        _tc_down_kernel,
        grid=(n_tok // TC_STEP,),
        in_specs=[pl.BlockSpec(memory_space=pl.ANY),
                  pl.BlockSpec((TC_STEP, xn.shape[1]), lambda i: (i, 0)),
                  pl.BlockSpec(memory_space=pl.ANY), full(diag), full(owner)],
        out_specs=pl.BlockSpec((TC_STEP, PEER_E), lambda i: (i, 0)),
        out_shape=jax.ShapeDtypeStruct((n_tok, PEER_E), F32),
        scratch_shapes=_tc_peer_scratch(tabp) + [pltpu.VMEM((TC_STEP, STACK_COLS), F32)],
        compiler_params=pltpu.CompilerParams(
            dimension_semantics=("arbitrary",), vmem_limit_bytes=VMEM_LIMIT),
        name="peer_down_dots_tc",
    )(offs, xn, tabp, diag, owner)


def _tc_up_kernel(idx_hbm, coef_ref, tab_hbm, spread_ref, diag_ref, o_ref,
                  tab_v, idx_s, sem, isem, ce_ref):
    _load_table_once(tab_hbm, tab_v, sem)
    c = coef_ref[...]
    c_hi = c.astype(BF16)
    c_lo = (c - c_hi.astype(F32)).astype(BF16)
    ce_ref[0] = jnp.dot(c_hi, spread_ref[...], preferred_element_type=F32)
    ce_ref[1] = jnp.dot(c_lo, spread_ref[...], preferred_element_type=F32)

    cols = STACK_COLS // TC_KCHUNKS

    def per_token(t, chunks):
        d = diag_ref[...]
        lhs = jnp.concatenate([(ce_ref[0, pl.ds(t, 1), :] * d).astype(BF16),
                               (ce_ref[1, pl.ds(t, 1), :] * d).astype(BF16)], axis=0)
        r = jnp.zeros((2 * ROW_HALVES, LANES), F32)
        for ch, rows in enumerate(chunks):
            r = r + jnp.dot(lhs[:, cols * ch:cols * (ch + 1)], rows, preferred_element_type=F32)
        out8 = r[0:ROW_HALVES] + r[ROW_HALVES:2 * ROW_HALVES]
        o_ref[pl.ds(t, 1), :] = jnp.concatenate(
            [out8[j:j + 1, :] for j in range(ROW_HALVES)], axis=1)

    _tc_token_pipeline(idx_hbm, idx_s, isem, tab_v, per_token)


def _peer_up_sum_tc(coef, offs, tabp, n_tok):
    d = ROW_HALVES * LANES
    diag, owner = _stack_selectors()
    spread = owner.T.astype(BF16)
    full = lambda a: pl.BlockSpec(a.shape, lambda i: (0,) * a.ndim)
    return pl.pallas_call(
        _tc_up_kernel,
        grid=(n_tok // TC_STEP,),
        in_specs=[pl.BlockSpec(memory_space=pl.ANY),
                  pl.BlockSpec((TC_STEP, PEER_E), lambda i: (i, 0)),
                  pl.BlockSpec(memory_space=pl.ANY), full(spread), full(diag)],
        out_specs=pl.BlockSpec((TC_STEP, d), lambda i: (i, 0)),
        out_shape=jax.ShapeDtypeStruct((n_tok, d), F32),
        scratch_shapes=_tc_peer_scratch(tabp) + [pltpu.VMEM((2, TC_STEP, STACK_COLS), F32)],
        compiler_params=pltpu.CompilerParams(
            dimension_semantics=("arbitrary",), vmem_limit_bytes=VMEM_LIMIT),
        name="peer_up_sum_tc",
    )(offs, coef, tabp, spread, diag)


def _coef_kernel(pre_ref, gate_ref, o_ref):
    x = pre_ref[...]
    gelu = 0.5 * x * (1.0 + lax.erf(x * (2.0 ** -0.5)))
    o_ref[...] = gate_ref[...] * gelu


def _peer_coef(pre, gates):
    t, e = pre.shape
    row = pl.BlockSpec((TM_COEF, e), lambda i: (i, 0))
    return pl.pallas_call(
        _coef_kernel,
        grid=(t // TM_COEF,),
        in_specs=[row, row],
        out_specs=row,
        out_shape=jax.ShapeDtypeStruct((t, e), F32),
        compiler_params=pltpu.CompilerParams(dimension_semantics=("arbitrary",)),
        name="peer_gate_gelu",
    )(pre, gates)


def _front_half(x2, row0, rows, bsz, seq, w_pad, layer, i):
    (norm1_g, ret_norm_g, m_conv_w, m_conv_b, m_wq, m_wk, m_b_i, m_b_f, m_norm_g, m_skip,
     w_out, norm2_g, peer_wq, keys) = layer
    d = x2.shape[1]
    proj = _rms_proj(x2, row0, rows, norm1_g[i].reshape(1, d), w_pad)
    y = _mixer(proj, bsz, seq, ret_norm_g[i], m_conv_w[i], m_conv_b[i], m_wq[i], m_wk[i],
               m_b_i[i], m_b_f[i], m_norm_g[i], m_skip[i])
    x1, xn, sub_t = _post(y, x2, row0, w_out, norm2_g[i].reshape(1, d), peer_wq, keys)
    return (x1, xn) + tuple(_peer_route(sub_t))


def kernel(x, norm1_g, w_in, ret_norm_g, m_conv_w, m_conv_b, m_wq, m_wk, m_b_i, m_b_f,
           m_norm_g, m_skip, w_out, norm2_g, peer_wq, peer_keys, peer_down, peer_up, final_g):
    b, s, d = x.shape
    t = b * s
    tp = t // FRONT_PARTS
    x2 = x.reshape(t, d)
    depth = norm1_g.shape[0]
    for i in range(depth):
        if i > 0:
            x2 = jnp.concatenate(x1s, axis=0) + jnp.concatenate([p_tc, p_sc], axis=0)
        wi = w_in[i]
        n_main = OFF_GI
        gate_pad = lambda w: jnp.pad(w, ((0, 0), (0, LANES - w.shape[1])))
        w_pad = jnp.concatenate(
            [wi[:, :n_main], gate_pad(wi[:, n_main:n_main + MLSTM_HEADS]),
             gate_pad(wi[:, n_main + MLSTM_HEADS:])], axis=1).astype(BF16)
        keys = peer_keys[i].reshape(2 * PEER_HEADS, PEER_NKEYS, PEER_HALF).astype(BF16)
        layer = (norm1_g, ret_norm_g, m_conv_w, m_conv_b, m_wq, m_wk, m_b_i, m_b_f, m_norm_g,
                 m_skip, w_out[i].astype(BF16), norm2_g, peer_wq[i].astype(BF16), keys)
        down_p = _pack_rows_bf16(peer_down[i])
        up_p = _pack_rows_bf16(peer_up[i])

        fronts = []
        pre = [None] * FRONT_PARTS
        for q in range(FRONT_PARTS):
            fr = _front_half(x2, q * tp, tp, b // FRONT_PARTS, s, w_pad, layer, i)
            fronts.append(fr)
            if q < SC_DOWN_PARTS:
                pre[q] = [_peer_down_dots(fr[1], fr[2], peer_down[i], 0)]
            elif q == SC_DOWN_PARTS and SC_DOWN_EXTRA:
                pre[q] = [None, _peer_down_dots(fr[1], fr[2], peer_down[i], tp - SC_DOWN_EXTRA)]
        for q in range(SC_DOWN_PARTS, FRONT_PARTS):
            n_tc_q = tp - SC_DOWN_EXTRA if q == SC_DOWN_PARTS else tp
            pre_tc = _peer_down_dots_tc(fronts[q][1], fronts[q][4], down_p, n_tc_q)
            if pre[q] is None:
                pre[q] = [pre_tc]
            else:
                pre[q][0] = pre_tc
        pre = [piece for part in pre for piece in part]
        x1s = [fr[0] for fr in fronts]
        expert = jnp.concatenate([fr[2] for fr in fronts], axis=0)
        gates = jnp.concatenate([fr[3] for fr in fronts], axis=0)
        offs = jnp.concatenate([fr[4] for fr in fronts], axis=0)
        coef = _peer_coef(jnp.concatenate(pre, axis=0), gates)
        n_tc = PEER_TC_UP_TOKENS
        p_sc = _peer_up_sum(coef, expert, peer_up[i], n_tc)
        p_tc = _peer_up_sum_tc(coef, offs, up_p, n_tc)
    out = _final(_row_ranges(x1s, [p_tc, p_sc]), final_g.reshape(1, d), t)
    return out.reshape(b, s, d)


def _row_ranges(xs, ps):
    def starts(arrs):
        out = [0]
        for a in arrs:
            out.append(out[-1] + a.shape[0])
        return out
    xb, pb = starts(xs), starts(ps)
    assert xb[-1] == pb[-1]
    cuts = sorted(set(xb) | set(pb))
    parts = []
    for lo, hi in zip(cuts[:-1], cuts[1:]):
        xi = max(k for k in range(len(xs)) if xb[k] <= lo)
        pi = max(k for k in range(len(ps)) if pb[k] <= lo)
        parts.append((xs[xi], lo - xb[xi], ps[pi], lo - pb[pi], hi - lo))
    return parts
```

```python
import functools

import numpy as np
import jax
import jax.numpy as jnp
from jax import lax
from jax.experimental import pallas as pl
from jax.experimental.pallas import tpu as pltpu
from jax.experimental.pallas import tpu_sc as plsc

F32 = jnp.float32
BF16 = jnp.bfloat16

D_MODEL = 1024
CHUNK = 64
RET_HEADS = 4
RET_DK = 128
MLSTM_HEADS = 4
MLSTM_D = 128
HEAD_W = 128
D_GROUP = 512
CONV_W = 4
ROPE_BASE = 10000.0
PEER_HEADS = 8
PEER_NKEYS = 128
PEER_TOPK = 16
PEER_HALF = 128
PEER_BLOCK = 128
EPS = 1e-6

OFF_RQ, OFF_RK, OFF_RV, OFF_RG = 0, 512, 1024, 1536
OFF_MX, OFF_MV, OFF_MO = 2048, 2560, 3072
OFF_GI, OFF_GF = 3584, 3712
D_PROJ_PAD = 3840

LANES = 128
CONV_TAIL = 8
VMEM_LIMIT = 56 * 1024 * 1024

TM_PROJ = 256
L_BLOCK = 256
TM_FINAL = 512
TM_COEF = 512


def _rms(x, g):
    ms = jnp.mean(x * x, axis=-1, keepdims=True)
    return x * lax.rsqrt(ms + EPS) * g


def _group_norm(h, g):
    mu = jnp.mean(h, axis=-1, keepdims=True)
    d = h - mu
    var = jnp.mean(d * d, axis=-1, keepdims=True)
    return d * lax.rsqrt(var + EPS) * g


def _silu(x):
    return x * (1.0 / (1.0 + jnp.exp(-x)))


def _sigmoid(x):
    return 1.0 / (1.0 + jnp.exp(-x))


def _dot(a, b):
    return jnp.dot(a.astype(BF16), b.astype(BF16), preferred_element_type=F32)


def _dot_tn(a, b):
    return lax.dot_general(a.astype(BF16), b.astype(BF16), (((0,), (0,)), ((), ())),
                           preferred_element_type=F32)


def _dot_nt(a, b):
    return lax.dot_general(a.astype(BF16), b.astype(BF16), (((1,), (1,)), ((), ())),
                           preferred_element_type=F32)


def _rms_proj_kernel(x_ref, g_ref, w_ref, o_ref):
    h = _rms(x_ref[...], g_ref[...])
    o_ref[...] = jnp.dot(h.astype(BF16), w_ref[...], preferred_element_type=F32)


def _rms_proj(x2, row0, rows, g, w):
    d = x2.shape[1]
    t = rows
    n = w.shape[1]
    b0 = row0 // TM_PROJ
    return pl.pallas_call(
        _rms_proj_kernel,
        grid=(t // TM_PROJ,),
        in_specs=[
            pl.BlockSpec((TM_PROJ, d), lambda i: (i + b0, 0)),
            pl.BlockSpec((1, d), lambda i: (0, 0)),
            pl.BlockSpec((d, n), lambda i: (0, 0)),
        ],
        out_specs=pl.BlockSpec((TM_PROJ, n), lambda i: (i, 0)),
        out_shape=jax.ShapeDtypeStruct((t, n), F32),
        compiler_params=pltpu.CompilerParams(
            dimension_semantics=("arbitrary",), vmem_limit_bytes=VMEM_LIMIT),
        name="rms_in_proj",
    )(x2, g, w)


def _mixer_kernel(chunk_decay,
                  proj_ref, cos_ref, sin_ref, intra_ref, qd_ref, kd_ref, tri_ref,
                  rng_ref, cw_ref, cb_ref, wq_ref, wk_ref, bi_ref, bf_ref,
                  mng_ref, skip_ref,
                  y_ref,
                  rstate_ref, cstate_ref, m_ref, tail_ref):
    j = pl.program_id(1)
    n_chunks = L_BLOCK // CHUNK

    @pl.when(j == 0)
    def _():
        rstate_ref[...] = jnp.zeros_like(rstate_ref)
        cstate_ref[...] = jnp.zeros_like(cstate_ref)
        m_ref[...] = jnp.zeros_like(m_ref)
        tail_ref[...] = jnp.zeros_like(tail_ref)

    cosv = cos_ref[...]
    sinv = sin_ref[...]
    k_scale = RET_DK ** -0.5

    for h in range(RET_HEADS):
        lo = h * HEAD_W
        q = proj_ref[:, OFF_RQ + lo:OFF_RQ + lo + HEAD_W]
        k = proj_ref[:, OFF_RK + lo:OFF_RK + lo + HEAD_W]
        q = q * cosv + pltpu.roll(q, HEAD_W // 2, 1) * sinv
        k = (k * cosv + pltpu.roll(k, HEAD_W // 2, 1) * sinv) * k_scale
        intra = intra_ref[h]
        qd = qd_ref[h]
        kd = kd_ref[h]
        g = rng_ref[:, lo:lo + HEAD_W]
        for c in range(n_chunks):
            r0 = c * CHUNK
            qc = q[r0:r0 + CHUNK]
            kc = k[r0:r0 + CHUNK]
            vc = proj_ref[r0:r0 + CHUNK, OFF_RV + lo:OFF_RV + lo + HEAD_W]
            gate = proj_ref[r0:r0 + CHUNK, OFF_RG + lo:OFF_RG + lo + HEAD_W]
            state = rstate_ref[h]
            scores = _dot_nt(qc, kc) * intra
            o = _dot(scores, vc) + _dot(qc, state) * qd
            rstate_ref[h] = chunk_decay[h] * state + _dot_tn(kc * kd, vc)
            y_ref[r0:r0 + CHUNK, lo:lo + HEAD_W] = _silu(gate) * _group_norm(o, g)

    mx = proj_ref[:, OFF_MX:OFF_MX + D_GROUP]
    xp = jnp.concatenate([tail_ref[...], mx], axis=0)
    base = CONV_TAIL - (CONV_W - 1)
    conv = cb_ref[...] + xp[base:base + L_BLOCK] * cw_ref[0:1, :]
    for t in range(1, CONV_W):
        conv = conv + xp[base + t:base + t + L_BLOCK] * cw_ref[t:t + 1, :]
    tail_ref[...] = mx[L_BLOCK - CONV_TAIL:L_BLOCK]
    cact = _silu(conv)

    gi = proj_ref[:, OFF_GI:OFF_GI + LANES] + bi_ref[...]
    gf = proj_ref[:, OFF_GF:OFF_GF + LANES] + bf_ref[...]
    logf = jnp.minimum(gf, 0.0) - jnp.log1p(jnp.exp(-jnp.abs(gf)))
    tri = tri_ref[...]
    lane = lax.broadcasted_iota(jnp.int32, (CHUNK, HEAD_W), 1)
    ones_col = (lane == 0).astype(F32)
    m_scale = MLSTM_D ** -0.5

    mq = []
    mk = []
    for h in range(MLSTM_HEADS):
        lo = h * HEAD_W
        ch = cact[:, lo:lo + HEAD_W]
        mq.append(_dot(ch, wq_ref[h]))
        mk.append(_dot(ch, wk_ref[h]) * m_scale)

    for c in range(n_chunks):
        r0 = c * CHUNK
        fcum = jnp.dot(tri, logf[r0:r0 + CHUNK], preferred_element_type=F32,
                       precision=lax.Precision.HIGHEST)
        a = fcum[CHUNK - 1:CHUNK]
        logw = a - fcum + gi[r0:r0 + CHUNK]
        m_in = jnp.max(logw, axis=0, keepdims=True)
        m_old = m_ref[...]
        m_new = jnp.maximum(a + m_old, m_in)
        decay = jnp.exp(a + m_old - m_new)
        w = jnp.exp(logw - m_new)
        em = jnp.exp(-m_new)
        m_ref[...] = m_new
        for h in range(MLSTM_HEADS):
            lo = h * HEAD_W
            qh = mq[h][r0:r0 + CHUNK]
            kh = mk[h][r0:r0 + CHUNK]
            vh = proj_ref[r0:r0 + CHUNK, OFF_MV + lo:OFF_MV + lo + HEAD_W]
            oh = proj_ref[r0:r0 + CHUNK, OFF_MO + lo:OFF_MO + lo + HEAD_W]
            ch = cact[r0:r0 + CHUNK, lo:lo + HEAD_W]
            v_ext = jnp.concatenate([vh, ones_col], axis=1)
            cmem = decay[:, h:h + 1] * cstate_ref[h] + _dot_tn(kh * w[:, h:h + 1], v_ext)
            cstate_ref[h] = cmem
            num = _dot(qh, cmem)
            den = jnp.maximum(jnp.abs(num[:, HEAD_W:HEAD_W + 1]), em[:, h:h + 1])
            hh = num[:, :HEAD_W] / den
            hm = _group_norm(hh, mng_ref[:, lo:lo + HEAD_W])
            ym = _sigmoid(oh) * (hm + skip_ref[:, lo:lo + HEAD_W] * ch)
            y_ref[r0:r0 + CHUNK, D_GROUP + lo:D_GROUP + lo + HEAD_W] = ym


def _retention_tables():
    h = np.arange(RET_HEADS, dtype=np.float64)
    log_g = np.log(1.0 - 2.0 ** (-5.0 - h))
    l = np.arange(CHUNK, dtype=np.float64)
    intra = np.exp(log_g[:, None, None] * np.abs(l[:, None] - l[None, :]))
    qd = np.exp(log_g[:, None] * (l + 1.0))[:, :, None] * np.ones((1, 1, HEAD_W))
    kd = np.exp(log_g[:, None] * (CHUNK - 1.0 - l))[:, :, None] * np.ones((1, 1, HEAD_W))
    cd = np.exp(log_g * CHUNK)
    return (jnp.asarray(intra, F32), jnp.asarray(qd, F32), jnp.asarray(kd, F32),
            tuple(float(v) for v in cd))


def _rotary_tables(s):
    half = HEAD_W // 2
    inv = ROPE_BASE ** (-np.arange(half, dtype=np.float64) / half)
    ang = np.arange(s, dtype=np.float64)[:, None] * inv[None, :]
    cos = np.cos(ang)
    sin = np.sin(ang)
    return (jnp.asarray(np.concatenate([cos, cos], axis=-1), F32),
            jnp.asarray(np.concatenate([-sin, sin], axis=-1), F32))


def _mixer(proj, b, s, ret_norm_g, m_conv_w, m_conv_b, m_wq, m_wk, m_b_i, m_b_f,
           m_norm_g, m_skip):
    intra, qd, kd, cd = _retention_tables()
    cos_t, sin_t = _rotary_tables(s)
    tri = jnp.asarray(np.tril(np.ones((CHUNK, CHUNK))), F32)
    nj = s // L_BLOCK
    pad = lambda v: jnp.pad(v.reshape(1, -1), ((0, 0), (0, LANES - v.size)))
    full = lambda shape: pl.BlockSpec(shape, lambda bi, j: (0,) * len(shape))
    return pl.pallas_call(
        functools.partial(_mixer_kernel, cd),
        grid=(b, nj),
        in_specs=[
            pl.BlockSpec((L_BLOCK, D_PROJ_PAD), lambda bi, j: (bi * nj + j, 0)),
            pl.BlockSpec((L_BLOCK, HEAD_W), lambda bi, j: (j, 0)),
            pl.BlockSpec((L_BLOCK, HEAD_W), lambda bi, j: (j, 0)),
            full((RET_HEADS, CHUNK, CHUNK)),
            full((RET_HEADS, CHUNK, HEAD_W)),
            full((RET_HEADS, CHUNK, HEAD_W)),
            full((CHUNK, CHUNK)),
            full((1, D_GROUP)),
            full((CONV_W, D_GROUP)),
            full((1, D_GROUP)),
            full((MLSTM_HEADS, MLSTM_D, MLSTM_D)),
            full((MLSTM_HEADS, MLSTM_D, MLSTM_D)),
            full((1, LANES)),
            full((1, LANES)),
            full((1, D_GROUP)),
            full((1, D_GROUP)),
        ],
        out_specs=pl.BlockSpec((L_BLOCK, 2 * D_GROUP), lambda bi, j: (bi * nj + j, 0)),
        out_shape=jax.ShapeDtypeStruct((b * s, 2 * D_GROUP), F32),
        scratch_shapes=[
            pltpu.VMEM((RET_HEADS, RET_DK, HEAD_W), F32),
            pltpu.VMEM((MLSTM_HEADS, MLSTM_D, 2 * HEAD_W), F32),
            pltpu.VMEM((1, LANES), F32),
            pltpu.VMEM((CONV_TAIL, D_GROUP), F32),
        ],
        compiler_params=pltpu.CompilerParams(
            dimension_semantics=("arbitrary", "arbitrary"), vmem_limit_bytes=VMEM_LIMIT),
        name="retention_mlstm_mixer",
    )(proj, cos_t, sin_t, intra, qd, kd, tri,
      ret_norm_g.reshape(1, D_GROUP), m_conv_w, m_conv_b.reshape(1, D_GROUP),
      m_wq.astype(BF16), m_wk.astype(BF16), pad(m_b_i), pad(m_b_f),
      m_norm_g.reshape(1, D_GROUP), m_skip.reshape(1, D_GROUP))


def _post_kernel(y_ref, x_ref, wout_ref, g2_ref, wq_ref, keys_ref,
                 x1_ref, xn_ref, sub_ref):
    x1 = x_ref[...] + jnp.dot(y_ref[...].astype(BF16), wout_ref[...],
                              preferred_element_type=F32)
    x1_ref[...] = x1
    xn = _rms(x1, g2_ref[...])
    xn_ref[...] = xn
    q = jnp.dot(xn.astype(BF16), wq_ref[...], preferred_element_type=F32)
    for i in range(2 * PEER_HEADS):
        lo = i * PEER_HALF
        sub_ref[lo:lo + PEER_NKEYS, :] = _dot_nt(keys_ref[i], q[:, lo:lo + PEER_HALF])


def _post(y, x2, row0, w_out, norm2_g, peer_wq, keys):
    t, d = y.shape
    nq = peer_wq.shape[1]
    b0 = row0 // TM_PROJ
    row = lambda w: pl.BlockSpec((TM_PROJ, w), lambda i: (i, 0))
    full = lambda shape: pl.BlockSpec(shape, lambda i: (0,) * len(shape))
    return pl.pallas_call(
        _post_kernel,
        grid=(t // TM_PROJ,),
        in_specs=[row(d), pl.BlockSpec((TM_PROJ, d), lambda i: (i + b0, 0)), full((d, d)),
                  full((1, d)), full((d, nq)), full(keys.shape)],
        out_specs=[row(d), row(d), pl.BlockSpec((nq, TM_PROJ), lambda i: (0, i))],
        out_shape=[jax.ShapeDtypeStruct((t, d), F32), jax.ShapeDtypeStruct((t, d), F32),
                   jax.ShapeDtypeStruct((nq, t), F32)],
        compiler_params=pltpu.CompilerParams(
            dimension_semantics=("arbitrary",), vmem_limit_bytes=VMEM_LIMIT),
        name="out_proj_peer_scores",
    )(y, x2, w_out, norm2_g, peer_wq, keys)


def _final_kernel(x1_ref, p_ref, g_ref, *rest):
    o_ref = rest[-1]
    o_ref[...] = _rms(x1_ref[...] + p_ref[...], g_ref[...])


def _final(parts, g, t):
    d = g.shape[1]
    spec = lambda r0: pl.BlockSpec((TM_FINAL, d), lambda i, b0=r0 // TM_FINAL: (i + b0, 0))
    out = None
    row0 = 0
    for x1, x_row0, p, p_row0, rows in parts:
        in_specs = [spec(x_row0), spec(p_row0), pl.BlockSpec((1, d), lambda i: (0, 0))]
        args = [x1, p, g]
        if out is not None:
            in_specs.append(pl.BlockSpec(memory_space=pl.ANY))
            args.append(out)
        out = pl.pallas_call(
            _final_kernel,
            grid=(rows // TM_FINAL,),
            in_specs=in_specs,
            out_specs=spec(row0),
            out_shape=jax.ShapeDtypeStruct((t, d), F32),
            input_output_aliases={} if len(args) == 3 else {3: 0},
            compiler_params=pltpu.CompilerParams(dimension_semantics=("arbitrary",)),
            name="final_rmsnorm",
        )(*args)
        row0 += rows
    assert row0 == t
    return out


TM_ROUTE = 256
CAND_ROWS = 56


def _candidate_tables():
    pairs = [(k1, k2) for k1 in range(PEER_TOPK) for k2 in range(PEER_TOPK)
             if (k1 + 1) * (k2 + 1) <= PEER_TOPK]
    assert len(pairs) <= CAND_ROWS
    rep = np.zeros((CAND_ROWS, 2 * PEER_TOPK), np.float32)
    til = np.zeros((CAND_ROWS, 2 * PEER_TOPK), np.float32)
    for pos, (k1, k2) in enumerate(pairs):
        rep[pos, k1] = 1.0
        til[pos, PEER_TOPK + k2] = 1.0
    return jnp.asarray(rep), jnp.asarray(til), len(pairs)


def _dot_exact(a, b):
    return jnp.dot(a, b, preferred_element_type=F32, precision=lax.Precision.HIGHEST)


def _route_kernel(n_cand, sub_ref, rep_ref, til_ref, seg_ref, exp_ref, gate_ref, offs_ref,
                  s_ref, i_ref, best_ref, eid_ref, m0_ref):
    tm = sub_ref.shape[1]
    neg = -jnp.inf
    key_id = lax.broadcasted_iota(jnp.int32, (PEER_NKEYS, tm), 0).astype(F32)
    cand_id = lax.broadcasted_iota(jnp.int32, (CAND_ROWS, tm), 0).astype(F32)
    rep = rep_ref[...]
    til = til_ref[...]
    for h in range(PEER_HEADS):
        for p in range(2):
            lo = (2 * h + p) * PEER_NKEYS
            x = sub_ref[lo:lo + PEER_NKEYS, :]
            for k in range(PEER_TOPK):
                m = jnp.max(x, axis=0, keepdims=True)
                idx = jnp.min(jnp.where(x == m, key_id, float(PEER_NKEYS)), axis=0, keepdims=True)
                x = jnp.where(key_id == idx, neg, x)
                r = p * PEER_TOPK + k
                s_ref[r:r + 1, :] = m
                i_ref[r:r + 1, :] = idx
        s = s_ref[...]
        iv = i_ref[...]
        cand = _dot_exact(rep, s) + _dot_exact(til, s)
        cand = jnp.where(cand_id < float(n_cand), cand, neg)
        eid = _dot(rep, iv) * float(PEER_NKEYS) + _dot(til, iv)
        for k in range(PEER_TOPK):
            m = jnp.max(cand, axis=0, keepdims=True)
            pos = jnp.min(jnp.where(cand == m, cand_id, float(CAND_ROWS)), axis=0, keepdims=True)
            sel = cand_id == pos
            e = jnp.max(jnp.where(sel, eid, -1.0), axis=0, keepdims=True)
            cand = jnp.where(sel, neg, cand)
            r = h * PEER_TOPK + k
            best_ref[r:r + 1, :] = m
            eid_ref[r:r + 1, :] = e
            if k == 0:
                m0_ref[r:r + PEER_TOPK, :] = jnp.broadcast_to(m, (PEER_TOPK, tm))
    pexp = jnp.exp(best_ref[...] - m0_ref[...])
    gates = pexp / _dot_exact(seg_ref[...], pexp)
    gate_ref[...] = gates.T
    eid = eid_ref[...].astype(jnp.int32)
    exp_ref[...] = eid.T
    for blk in range(tm // LANES):
        offs_ref[blk] = eid[:, blk * LANES:(blk + 1) * LANES] * ROW_WORDS


def _peer_route(sub_t):
    nq, t = sub_t.shape
    rep, til, n_cand = _candidate_tables()
    e = PEER_HEADS * PEER_TOPK
    seg = jnp.asarray(np.kron(np.eye(PEER_HEADS), np.ones((PEER_TOPK, PEER_TOPK))), F32)
    full = lambda a: pl.BlockSpec(a.shape, lambda i: (0,) * a.ndim)
    out = pl.BlockSpec((TM_ROUTE, e), lambda i: (i, 0))
    return pl.pallas_call(
        functools.partial(_route_kernel, n_cand),
        grid=(t // TM_ROUTE,),
        in_specs=[pl.BlockSpec((nq, TM_ROUTE), lambda i: (0, i)), full(rep), full(til), full(seg)],
        out_specs=[out, out, pl.BlockSpec((TM_ROUTE // LANES, e, LANES), lambda i: (i, 0, 0))],
        out_shape=[jax.ShapeDtypeStruct((t, e), jnp.int32), jax.ShapeDtypeStruct((t, e), F32),
                   jax.ShapeDtypeStruct((t // LANES, e, LANES), jnp.int32)],
        scratch_shapes=[
            pltpu.VMEM((2 * PEER_TOPK, TM_ROUTE), F32),
            pltpu.VMEM((2 * PEER_TOPK, TM_ROUTE), F32),
            pltpu.VMEM((e, TM_ROUTE), F32),
            pltpu.VMEM((e, TM_ROUTE), F32),
            pltpu.VMEM((e, TM_ROUTE), F32),
        ],
        compiler_params=pltpu.CompilerParams(dimension_semantics=("arbitrary",)),
        name="peer_topk_route",
    )(sub_t, rep, til, seg)


SC_CORES = 2
SC_SUBCORES = 16
SC_LANES = 16
SC_WORKERS = SC_CORES * SC_SUBCORES
PEER_E = PEER_HEADS * PEER_TOPK
SC_ROWS = 32
SC_GATHERS = PEER_E // SC_ROWS
SC_TOK_BLOCK = 8
SC_QUARTER = 256


def _sc_mesh():
    return plsc.VectorSubcoreMesh(core_axis_name="c", subcore_axis_name="s")


def _sc_pipeline(idx_v, tab_hbm, rows_v, sems, compute):
    def gather(i, c, buf):
        window = idx_v.at[i, pl.ds(c * SC_ROWS, SC_ROWS)]
        return pltpu.make_async_copy(tab_hbm.at[window], rows_v.at[buf], sems.at[buf])

    gather(0, 0, 0).start()

    @pl.loop(0, SC_TOK_BLOCK)
    def _(i):
        for c in range(SC_GATHERS):
            buf = c % 2
            gather(i, c, buf).wait()
            if c + 1 < SC_GATHERS:
                gather(i, c + 1, 1 - buf).start()
            else:
                @pl.when(i + 1 < SC_TOK_BLOCK)
                def _():
                    gather(i + 1, 0, 1 - buf).start()
            compute(i, c, i * SC_GATHERS + c, buf)


def _peer_down_dots(xn, idx, down, tok_start):
    t_all, d = xn.shape
    t = t_all - tok_start
    tok_w = t // SC_WORKERS
    nblk = tok_w // SC_TOK_BLOCK
    assert nblk * SC_TOK_BLOCK * SC_WORKERS == t
    lanes = SC_LANES

    @functools.partial(
        pl.kernel, mesh=_sc_mesh(),
        out_type=jax.ShapeDtypeStruct((t, PEER_E), F32),
        scratch_types=[
            pltpu.VMEM((SC_TOK_BLOCK, PEER_E), jnp.int32),
            pltpu.VMEM((SC_TOK_BLOCK, d), F32),
            pltpu.VMEM((2, SC_ROWS, d), F32),
            pltpu.VMEM((SC_ROWS, lanes), F32),
            pltpu.VMEM((SC_TOK_BLOCK, PEER_E), F32),
            pltpu.SemaphoreType.DMA((2,)),
        ],
        compiler_params=pltpu.CompilerParams(needs_layout_passes=False),
        name="peer_down_dots",
    )
    def k(x_hbm, idx_hbm, tab_hbm, out_hbm, idx_v, x_v, rows_v, acc_v, out_v, sems):
        wid = lax.axis_index("s") * SC_CORES + lax.axis_index("c")
        lane = lax.iota(jnp.int32, lanes)

        @pl.loop(0, nblk)
        def _(blk):
            out0 = wid * tok_w + blk * SC_TOK_BLOCK
            tok0 = tok_start + out0
            pltpu.sync_copy(idx_hbm.at[pl.ds(tok0, SC_TOK_BLOCK)], idx_v)
            pltpu.sync_copy(x_hbm.at[pl.ds(tok0, SC_TOK_BLOCK)], x_v)

            def compute(i, c, step, buf):
                def body(dc, accs):
                    xv = x_v[i, pl.ds(dc * lanes, lanes)]
                    return tuple(a + rows_v[buf, r, pl.ds(dc * lanes, lanes)] * xv
                                 for r, a in enumerate(accs))
                accs = lax.fori_loop(0, d // lanes, body,
                                     tuple(jnp.zeros((lanes,), F32) for _ in range(SC_ROWS)))
                for r in range(SC_ROWS):
                    acc_v[r, :] = accs[r]
                for g in range(SC_ROWS // lanes):
                    tot = jnp.zeros((lanes,), F32)
                    for l in range(lanes):
                        tot = tot + plsc.load_gather(
                            acc_v, [lane + g * lanes, jnp.full((lanes,), l, jnp.int32)])
                    out_v[i, pl.ds(c * SC_ROWS + g * lanes, lanes)] = tot

            _sc_pipeline(idx_v, tab_hbm, rows_v, sems, compute)
            pltpu.sync_copy(out_v, out_hbm.at[pl.ds(out0, SC_TOK_BLOCK)])

    return k(xn, idx, down)


def _peer_up_sum(coef, idx, up, tok_start):
    t_all = coef.shape[0]
    t = t_all - tok_start
    d = up.shape[1]
    tok_w = t // SC_WORKERS
    nblk = tok_w // SC_TOK_BLOCK
    assert nblk * SC_TOK_BLOCK * SC_WORKERS == t
    lanes = SC_LANES
    nj = SC_QUARTER // lanes

    @functools.partial(
        pl.kernel, mesh=_sc_mesh(),
        out_type=jax.ShapeDtypeStruct((t, d), F32),
        scratch_types=[
            pltpu.VMEM((SC_TOK_BLOCK, PEER_E), jnp.int32),
            pltpu.VMEM((SC_TOK_BLOCK * PEER_E,), F32),
            pltpu.VMEM((2, SC_ROWS, d), F32),
            pltpu.VMEM((SC_TOK_BLOCK, d), F32),
            pltpu.SemaphoreType.DMA((2,)),
        ],
        compiler_params=pltpu.CompilerParams(needs_layout_passes=False),
        name="peer_up_sum",
    )
    def k(coef_hbm, idx_hbm, tab_hbm, out_hbm, idx_v, coef_v, rows_v, out_v, sems):
        wid = lax.axis_index("s") * SC_CORES + lax.axis_index("c")

        @pl.loop(0, nblk)
        def _(blk):
            out0 = wid * tok_w + blk * SC_TOK_BLOCK
            tok0 = tok_start + out0
            pltpu.sync_copy(idx_hbm.at[pl.ds(tok0, SC_TOK_BLOCK)], idx_v)
            pltpu.sync_copy(coef_hbm.at[pl.ds(tok0 * PEER_E, SC_TOK_BLOCK * PEER_E)], coef_v)

            def compute(i, c, step, buf):
                for q in range(d // SC_QUARTER):
                    col = lambda j: pl.ds(q * SC_QUARTER + j * lanes, lanes)
                    if c == 0:
                        init = tuple(jnp.zeros((lanes,), F32) for _ in range(nj))
                    else:
                        init = tuple(out_v[i, col(j)] for j in range(nj))

                    def body(r, accs):
                        cf = plsc.load_gather(
                            coef_v, [jnp.full((lanes,), step * SC_ROWS + r, jnp.int32)])
                        return tuple(a + rows_v[buf, r, col(j)] * cf for j, a in enumerate(accs))
                    accs = lax.fori_loop(0, SC_ROWS, body, init)
                    for j in range(nj):
                        out_v[i, col(j)] = accs[j]

            _sc_pipeline(idx_v, tab_hbm, rows_v, sems, compute)
            pltpu.sync_copy(out_v, out_hbm.at[pl.ds(out0, SC_TOK_BLOCK)])

    return k(coef.reshape(t_all * PEER_E), idx, up)


TC_SUB = 128
TC_SUBS = 2
TC_STEP = TC_SUB * TC_SUBS
TC_TRIP = 64
TC_KCHUNKS = 4
ROW_WORDS = 4
ROW_HALVES = 2 * ROW_WORDS
STACK_COLS = PEER_E * ROW_HALVES
FRONT_PARTS = 4
SC_DOWN_PARTS = 1
SC_DOWN_EXTRA = 6656
PEER_TC_UP_TOKENS = 25088


def _pack_rows_bf16(tab):
    n, d = tab.shape
    return pl.pallas_call(
        _pack_kernel,
        grid=(n // PACK_ROWS,),
        in_specs=[pl.BlockSpec((PACK_ROWS, d), lambda i: (i, 0))],
        out_specs=pl.BlockSpec((PACK_ROWS * ROW_WORDS, LANES), lambda i: (i, 0)),
        out_shape=jax.ShapeDtypeStruct((n * ROW_WORDS, LANES), jnp.uint32),
        compiler_params=pltpu.CompilerParams(dimension_semantics=("arbitrary",)),
        name="pack_rows_bf16",
    )(tab)


PACK_ROWS = 512


def _pack_kernel(x_ref, o_ref):
    u = pltpu.bitcast(x_ref[...], jnp.uint32)
    r = (u + jnp.uint32(0x7FFF) + ((u >> 16) & jnp.uint32(1))) >> 16
    for s in range(ROW_WORDS):
        lo = r[:, 2 * LANES * s:2 * LANES * s + LANES]
        hi = r[:, 2 * LANES * s + LANES:2 * LANES * (s + 1)]
        o_ref[pl.ds(s, PACK_ROWS, stride=ROW_WORDS), :] = lo | (hi << 16)


def _load_table_once(tab_hbm, tab_v, sem):
    @pl.when(pl.program_id(0) == 0)
    def _():
        cp = pltpu.make_async_copy(tab_hbm, tab_v, sem)
        cp.start()
        cp.wait()


def _tc_token_pipeline(idx_hbm, idx_s, isem, tab_v, per_token):
    step = pl.program_id(0)

    def idx_copy(block, slot):
        return pltpu.make_async_copy(idx_hbm.at[block], idx_s[slot], isem.at[slot])

    @pl.when(step == 0)
    def _():
        idx_copy(0, 0).start()

    for sub in range(TC_SUBS):
        block = step * TC_SUBS + sub
        idx_copy(block, sub).wait()
        if sub + 1 < TC_SUBS:
            idx_copy(block + 1, sub + 1).start()
        else:
            @pl.when(step + 1 < pl.num_programs(0))
            def _():
                idx_copy(block + 1, 0).start()
        offs = idx_s[sub]

        def chunks(i, offs=offs):
            per = PEER_E // TC_KCHUNKS
            for ch in range(TC_KCHUNKS):
                pieces = [tab_v[pl.ds(pl.multiple_of(offs.at[k][i], ROW_WORDS), ROW_WORDS), :]
                          for k in range(ch * per, (ch + 1) * per)]
                yield pltpu.bitcast(jnp.concatenate(pieces, axis=0), BF16)

        def trip(ip, carry, sub=sub, chunks=chunks):
            for u in range(TC_TRIP):
                i = ip * TC_TRIP + u
                per_token(sub * TC_SUB + i, chunks(i))
            return carry
        lax.fori_loop(0, TC_SUB // TC_TRIP, trip, 0)


def _tc_peer_scratch(tabp):
    return [pltpu.VMEM(tabp.shape, jnp.uint32),
            [pltpu.SMEM((PEER_E, TC_SUB), jnp.int32) for _ in range(TC_SUBS)],
            pltpu.SemaphoreType.DMA,
            pltpu.SemaphoreType.DMA((TC_SUBS,))]


def _stack_selectors():
    col = np.arange(STACK_COLS)
    diag = (col[None, :] % ROW_HALVES == np.arange(ROW_HALVES)[:, None]).astype(np.float32)
    owner = (col[:, None] // ROW_HALVES == np.arange(PEER_E)[None, :]).astype(np.float32)
    return jnp.asarray(diag), jnp.asarray(owner)


def _gate_gelu(pre, gate):
    return gate * (0.5 * pre * (1.0 + lax.erf(pre * (2.0 ** -0.5))))


def _tc_down_kernel(idx_hbm, x_ref, tab_hbm, diag_ref, fold_ref, gate_ref, o_ref,
                    tab_v, idx_s, sem, isem, z_ref):
    _load_table_once(tab_hbm, tab_v, sem)
    cols = STACK_COLS // TC_KCHUNKS

    def per_token(t, chunks):
        xr = x_ref[pl.ds(t, 1), :]
        x8 = jnp.concatenate([xr[:, LANES * j:LANES * (j + 1)] for j in range(ROW_HALVES)],
                             axis=0).astype(BF16)
        parts = []
        for ch, rows in enumerate(chunks):
            a = lax.dot_general(x8, rows, (((1,), (1,)), ((), ())), preferred_element_type=F32)
            parts.append(jnp.sum(a * diag_ref[:, cols * ch:cols * (ch + 1)], axis=0, keepdims=True))
        z_ref[pl.ds(t, 1), :] = jnp.concatenate(parts, axis=1)

    _tc_token_pipeline(idx_hbm, idx_s, isem, tab_v, per_token)
    o_ref[...] = _gate_gelu(_dot_exact(z_ref[...], fold_ref[...]), gate_ref[...])


def _peer_down_dots_tc(xn, offs, gates, tabp, n_tok):
    diag, owner = _stack_selectors()
    full = lambda a: pl.BlockSpec(a.shape, lambda i: (0,) * a.ndim)
    return pl.pallas_call(
        _tc_down_kernel,
        grid=(n_tok // TC_STEP,),
        in_specs=[pl.BlockSpec(memory_space=pl.ANY),
                  pl.BlockSpec((TC_STEP, xn.shape[1]), lambda i: (i, 0)),
                  pl.BlockSpec(memory_space=pl.ANY), full(diag), full(owner),
                  pl.BlockSpec((TC_STEP, PEER_E), lambda i: (i, 0))],
        out_specs=pl.BlockSpec((TC_STEP, PEER_E), lambda i: (i, 0)),
        out_shape=jax.ShapeDtypeStruct((n_tok, PEER_E), F32),
        scratch_shapes=_tc_peer_scratch(tabp) + [pltpu.VMEM((TC_STEP, STACK_COLS), F32)],
        compiler_params=pltpu.CompilerParams(
            dimension_semantics=("arbitrary",), vmem_limit_bytes=VMEM_LIMIT),
        name="peer_down_dots_tc",
    )(offs, xn, tabp, diag, owner, gates)


def _tc_up_kernel(idx_hbm, coef_ref, tab_hbm, spread_ref, diag_ref, o_ref,
                  tab_v, idx_s, sem, isem, ce_ref):
    _load_table_once(tab_hbm, tab_v, sem)
    c = coef_ref[...]
    c_hi = c.astype(BF16)
    c_lo = (c - c_hi.astype(F32)).astype(BF16)
    ce_ref[0] = jnp.dot(c_hi, spread_ref[...], preferred_element_type=F32)
    ce_ref[1] = jnp.dot(c_lo, spread_ref[...], preferred_element_type=F32)

    cols = STACK_COLS // TC_KCHUNKS

    def per_token(t, chunks):
        d = diag_ref[...]
        lhs = jnp.concatenate([(ce_ref[0, pl.ds(t, 1), :] * d).astype(BF16),
                               (ce_ref[1, pl.ds(t, 1), :] * d).astype(BF16)], axis=0)
        r = jnp.zeros((2 * ROW_HALVES, LANES), F32)
        for ch, rows in enumerate(chunks):
            r = r + jnp.dot(lhs[:, cols * ch:cols * (ch + 1)], rows, preferred_element_type=F32)
        out8 = r[0:ROW_HALVES] + r[ROW_HALVES:2 * ROW_HALVES]
        o_ref[pl.ds(t, 1), :] = jnp.concatenate(
            [out8[j:j + 1, :] for j in range(ROW_HALVES)], axis=1)

    _tc_token_pipeline(idx_hbm, idx_s, isem, tab_v, per_token)


def _peer_up_sum_tc(coef, offs, tabp, n_tok):
    d = ROW_HALVES * LANES
    diag, owner = _stack_selectors()
    spread = owner.T.astype(BF16)
    full = lambda a: pl.BlockSpec(a.shape, lambda i: (0,) * a.ndim)
    return pl.pallas_call(
        _tc_up_kernel,
        grid=(n_tok // TC_STEP,),
        in_specs=[pl.BlockSpec(memory_space=pl.ANY),
                  pl.BlockSpec((TC_STEP, PEER_E), lambda i: (i, 0)),
                  pl.BlockSpec(memory_space=pl.ANY), full(spread), full(diag)],
        out_specs=pl.BlockSpec((TC_STEP, d), lambda i: (i, 0)),
        out_shape=jax.ShapeDtypeStruct((n_tok, d), F32),
        scratch_shapes=_tc_peer_scratch(tabp) + [pltpu.VMEM((2, TC_STEP, STACK_COLS), F32)],
        compiler_params=pltpu.CompilerParams(
            dimension_semantics=("arbitrary",), vmem_limit_bytes=VMEM_LIMIT),
        name="peer_up_sum_tc",
    )(offs, coef, tabp, spread, diag)


def _coef_kernel(pre_ref, gate_ref, o_ref):
    o_ref[...] = _gate_gelu(pre_ref[...], gate_ref[...])


def _peer_coef(pre, gates, gate_row0):
    t, e = pre.shape
    b0 = gate_row0 // TM_COEF
    row = pl.BlockSpec((TM_COEF, e), lambda i: (i, 0))
    return pl.pallas_call(
        _coef_kernel,
        grid=(t // TM_COEF,),
        in_specs=[row, pl.BlockSpec((TM_COEF, e), lambda i: (i + b0, 0))],
        out_specs=row,
        out_shape=jax.ShapeDtypeStruct((t, e), F32),
        compiler_params=pltpu.CompilerParams(dimension_semantics=("arbitrary",)),
        name="peer_gate_gelu",
    )(pre, gates)


def _front_half(x2, row0, rows, bsz, seq, w_pad, layer, i):
    (norm1_g, ret_norm_g, m_conv_w, m_conv_b, m_wq, m_wk, m_b_i, m_b_f, m_norm_g, m_skip,
     w_out, norm2_g, peer_wq, keys) = layer
    d = x2.shape[1]
    proj = _rms_proj(x2, row0, rows, norm1_g[i].reshape(1, d), w_pad)
    y = _mixer(proj, bsz, seq, ret_norm_g[i], m_conv_w[i], m_conv_b[i], m_wq[i], m_wk[i],
               m_b_i[i], m_b_f[i], m_norm_g[i], m_skip[i])
    x1, xn, sub_t = _post(y, x2, row0, w_out, norm2_g[i].reshape(1, d), peer_wq, keys)
    return (x1, xn) + tuple(_peer_route(sub_t))


def kernel(x, norm1_g, w_in, ret_norm_g, m_conv_w, m_conv_b, m_wq, m_wk, m_b_i, m_b_f,
           m_norm_g, m_skip, w_out, norm2_g, peer_wq, peer_keys, peer_down, peer_up, final_g):
    b, s, d = x.shape
    t = b * s
    tp = t // FRONT_PARTS
    x2 = x.reshape(t, d)
    depth = norm1_g.shape[0]
    for i in range(depth):
        if i > 0:
            x2 = jnp.concatenate(x1s, axis=0) + jnp.concatenate([p_tc, p_sc], axis=0)
        wi = w_in[i]
        n_main = OFF_GI
        gate_pad = lambda w: jnp.pad(w, ((0, 0), (0, LANES - w.shape[1])))
        w_pad = jnp.concatenate(
            [wi[:, :n_main], gate_pad(wi[:, n_main:n_main + MLSTM_HEADS]),
             gate_pad(wi[:, n_main + MLSTM_HEADS:])], axis=1).astype(BF16)
        keys = peer_keys[i].reshape(2 * PEER_HEADS, PEER_NKEYS, PEER_HALF).astype(BF16)
        layer = (norm1_g, ret_norm_g, m_conv_w, m_conv_b, m_wq, m_wk, m_b_i, m_b_f, m_norm_g,
                 m_skip, w_out[i].astype(BF16), norm2_g, peer_wq[i].astype(BF16), keys)
        down_p = _pack_rows_bf16(peer_down[i])
        up_p = _pack_rows_bf16(peer_up[i])

        fronts = []
        pre = [None] * FRONT_PARTS
        for q in range(FRONT_PARTS):
            fr = _front_half(x2, q * tp, tp, b // FRONT_PARTS, s, w_pad, layer, i)
            fronts.append(fr)
            if q < SC_DOWN_PARTS:
                pre[q] = [_peer_coef(_peer_down_dots(fr[1], fr[2], peer_down[i], 0), fr[3], 0)]
            elif q == SC_DOWN_PARTS and SC_DOWN_EXTRA:
                pre[q] = [None, _peer_coef(_peer_down_dots(fr[1], fr[2], peer_down[i], tp - SC_DOWN_EXTRA),
                                           fr[3], tp - SC_DOWN_EXTRA)]
        for q in range(SC_DOWN_PARTS, FRONT_PARTS):
            n_tc_q = tp - SC_DOWN_EXTRA if q == SC_DOWN_PARTS else tp
            pre_tc = _peer_down_dots_tc(fronts[q][1], fronts[q][4], fronts[q][3], down_p, n_tc_q)
            if pre[q] is None:
                pre[q] = [pre_tc]
            else:
                pre[q][0] = pre_tc
        pre = [piece for part in pre for piece in part]
        x1s = [fr[0] for fr in fronts]
        expert = jnp.concatenate([fr[2] for fr in fronts], axis=0)
        offs = jnp.concatenate([fr[4] for fr in fronts], axis=0)
        coef = jnp.concatenate(pre, axis=0)
        n_tc = PEER_TC_UP_TOKENS
        p_sc = _peer_up_sum(coef, expert, peer_up[i], n_tc)
        p_tc = _peer_up_sum_tc(coef, offs, up_p, n_tc)
    out = _final(_row_ranges(x1s, [p_tc, p_sc]), final_g.reshape(1, d), t)
    return out.reshape(b, s, d)


def _row_ranges(xs, ps):
    def starts(arrs):
        out = [0]
        for a in arrs:
            out.append(out[-1] + a.shape[0])
        return out
    xb, pb = starts(xs), starts(ps)
    assert xb[-1] == pb[-1]
    cuts = sorted(set(xb) | set(pb))
    parts = []
    for lo, hi in zip(cuts[:-1], cuts[1:]):
        xi = max(k for k in range(len(xs)) if xb[k] <= lo)
        pi = max(k for k in range(len(ps)) if pb[k] <= lo)
        parts.append((xs[xi], lo - xb[xi], ps[pi], lo - pb[pi], hi - lo))
    return parts
```
